```python
import jax, jax.numpy as jnp
from jax import lax
import numpy as np

D_MODEL = 1024
BATCH = 8
SEQ = 4096
DEPTH = 1

HEAD_DIM = 64
MIX_WIDTH = D_MODEL
ATT_WIDTH = MIX_WIDTH // 2
CONV_WIDTH = MIX_WIDTH - ATT_WIDTH
N_ATT_HEADS = ATT_WIDTH // HEAD_DIM
N_CONV_GROUPS = CONV_WIDTH // HEAD_DIM
N_GROUPS = N_ATT_HEADS + N_CONV_GROUPS
CONV_K = 3
D_FF = 256 * ((8 * D_MODEL // 3 + 255) // 256)
Q_BLOCK = 128
N_MOD = 9
EPS = 1e-6
IN_COLS = 3 * ATT_WIDTH + N_ATT_HEADS + 3 * CONV_WIDTH

kernel_name = "hybrid_fox_shortconv_macaron_adaln"


def rmsnorm(x, g):
    xf = x.astype(jnp.float32)
    y = xf * lax.rsqrt(jnp.mean(xf * xf, axis=-1, keepdims=True) + EPS)
    return (y * g.astype(jnp.float32)).astype(x.dtype)


def modulate(h, shift, scale):
    return h * (1.0 + scale[:, None, :]) + shift[:, None, :]


def swiglu(h, w_gate, w_up, w_down):
    return (jax.nn.silu(h @ w_gate) * (h @ w_up)) @ w_down


def forgetting_attention(q, k, v, log_f):
    S = q.shape[2]
    scale = 1.0 / np.sqrt(HEAD_DIM).astype(np.float32)
    F = jnp.cumsum(log_f, axis=-1)
    outs = []
    for i in range(S // Q_BLOCK):
        q0, q1 = i * Q_BLOCK, (i + 1) * Q_BLOCK
        qb = q[:, :, q0:q1]
        kb = k[:, :, :q1]
        vb = v[:, :, :q1]
        s = jnp.einsum('bhqd,bhkd->bhqk', qb, kb).astype(jnp.float32) * scale
        s = s + F[:, :, q0:q1, None] - F[:, :, None, :q1]
        qpos = q0 + jnp.arange(Q_BLOCK)
        kpos = jnp.arange(q1)
        s = jnp.where(kpos[None, :] <= qpos[:, None], s, -jnp.inf)
        p = jax.nn.softmax(s, axis=-1)
        outs.append(jnp.einsum('bhqk,bhkd->bhqd', p.astype(v.dtype), vb))
    return jnp.concatenate(outs, axis=2)


def short_conv(u, conv_w):
    S = u.shape[1]
    up = jnp.pad(u, ((0, 0), (CONV_K - 1, 0), (0, 0)))
    y = conv_w[0] * up[:, 0:S]
    for j in range(1, CONV_K):
        y = y + conv_w[j] * up[:, j:j + S]
    return y


def hybrid_mixer(h, w_in, forget_bias, conv_w, group_norm_g, w_out):
    B, S, _ = h.shape
    proj = h @ w_in
    o = 0
    q = proj[..., o:o + ATT_WIDTH]; o += ATT_WIDTH
    k = proj[..., o:o + ATT_WIDTH]; o += ATT_WIDTH
    v = proj[..., o:o + ATT_WIDTH]; o += ATT_WIDTH
    f_logit = proj[..., o:o + N_ATT_HEADS]; o += N_ATT_HEADS
    gate_b = proj[..., o:o + CONV_WIDTH]; o += CONV_WIDTH
    gate_c = proj[..., o:o + CONV_WIDTH]; o += CONV_WIDTH
    xc = proj[..., o:o + CONV_WIDTH]

    def heads(t):
        return t.reshape(B, S, N_ATT_HEADS, HEAD_DIM).transpose(0, 2, 1, 3)
    log_f = jax.nn.log_sigmoid(f_logit.astype(jnp.float32) + forget_bias.astype(jnp.float32))
    log_f = log_f.transpose(0, 2, 1)
    att = forgetting_attention(heads(q), heads(k), heads(v), log_f)
    att = att.transpose(0, 2, 1, 3).reshape(B, S, ATT_WIDTH)

    cv = gate_b * short_conv(gate_c * xc, conv_w)

    y = jnp.concatenate([att, cv], axis=-1).reshape(B, S, N_GROUPS, HEAD_DIM)
    y = rmsnorm(y, group_norm_g.reshape(N_GROUPS, HEAD_DIM))
    return y.reshape(B, S, MIX_WIDTH) @ w_out


def _fwd_setup_inputs(seed: int = 0) -> dict:
    key = jax.random.key(seed)
    ks = jax.random.split(key, 20)
    f32 = jnp.float32
    D = D_MODEL

    def nrm(k, shape, fan_in):
        return jax.random.normal(k, shape, f32) * (fan_in ** -0.5)

    def gain(k, n):
        return 1.0 + 0.02 * jax.random.normal(k, (n,), f32)

    return {
        "x": jax.random.normal(ks[0], (BATCH, SEQ, D), f32),
        "c": jax.random.normal(ks[1], (BATCH, D), f32),
        "ada_w": nrm(ks[2], (D, N_MOD * D), D),
        "ada_b": 0.02 * jax.random.normal(ks[3], (N_MOD * D,), f32),
        "norm1_g": gain(ks[4], D),
        "ffn1_w_gate": nrm(ks[5], (D, D_FF), D),
        "ffn1_w_up": nrm(ks[6], (D, D_FF), D),
        "ffn1_w_down": nrm(ks[7], (D_FF, D), D_FF),
        "norm2_g": gain(ks[8], D),
        "w_in": nrm(ks[9], (D, IN_COLS), D),
        "forget_bias": 3.0 + 3.0 * jax.random.uniform(ks[10], (N_ATT_HEADS,), f32),
        "conv_w": nrm(ks[11], (CONV_K, CONV_WIDTH), CONV_K),
        "group_norm_g": gain(ks[12], MIX_WIDTH),
        "w_out": nrm(ks[13], (MIX_WIDTH, D), MIX_WIDTH),
        "norm3_g": gain(ks[14], D),
        "ffn2_w_gate": nrm(ks[15], (D, D_FF), D),
        "ffn2_w_up": nrm(ks[16], (D, D_FF), D),
        "ffn2_w_down": nrm(ks[17], (D_FF, D), D_FF),
        "final_g": gain(ks[18], D),
    }


def _fwd_reference(x, c, ada_w, ada_b, norm1_g, ffn1_w_gate, ffn1_w_up, ffn1_w_down,
              norm2_g, w_in, forget_bias, conv_w, group_norm_g, w_out,
              norm3_g, ffn2_w_gate, ffn2_w_up, ffn2_w_down, final_g):
    mod = jax.nn.silu(c) @ ada_w + ada_b
    (sh1, sc1, g1, sh2, sc2, g2, sh3, sc3, g3) = jnp.split(mod, N_MOD, axis=-1)
    for _ in range(DEPTH):
        h = modulate(rmsnorm(x, norm1_g), sh1, sc1)
        x = x + 0.5 * g1[:, None, :] * swiglu(h, ffn1_w_gate, ffn1_w_up, ffn1_w_down)
        h = modulate(rmsnorm(x, norm2_g), sh2, sc2)
        x = x + g2[:, None, :] * hybrid_mixer(h, w_in, forget_bias, conv_w, group_norm_g, w_out)
        h = modulate(rmsnorm(x, norm3_g), sh3, sc3)
        x = x + 0.5 * g3[:, None, :] * swiglu(h, ffn2_w_gate, ffn2_w_up, ffn2_w_down)
    return rmsnorm(x, final_g)


import jax as _jax
import jax.numpy as _jnp

TWIN_FORMAT = 'train_step'
FWD_PARAMS = ['x', 'c', 'ada_w', 'ada_b', 'norm1_g', 'ffn1_w_gate', 'ffn1_w_up', 'ffn1_w_down', 'norm2_g', 'w_in', 'forget_bias', 'conv_w', 'group_norm_g', 'w_out', 'norm3_g', 'ffn2_w_gate', 'ffn2_w_up', 'ffn2_w_down', 'final_g']
TWIN_WEIGHTS = ['ada_w', 'ada_b', 'norm1_g', 'ffn1_w_gate', 'ffn1_w_up', 'ffn1_w_down', 'norm2_g', 'w_in', 'forget_bias', 'conv_w', 'group_norm_g', 'w_out', 'norm3_g', 'ffn2_w_gate', 'ffn2_w_up', 'ffn2_w_down', 'final_g']
TWIN_DIFF_INPUT = 'x'
TWIN_INPUTS = ['x', 'c', 'ada_w', 'ada_b', 'norm1_g', 'ffn1_w_gate', 'ffn1_w_up', 'ffn1_w_down', 'norm2_g', 'w_in', 'forget_bias', 'conv_w', 'group_norm_g', 'w_out', 'norm3_g', 'ffn2_w_gate', 'ffn2_w_up', 'ffn2_w_down', 'final_g', 'loss_target', 'm_ada_w', 'm_ada_b', 'm_norm1_g', 'm_ffn1_w_gate', 'm_ffn1_w_up', 'm_ffn1_w_down', 'm_norm2_g', 'm_w_in', 'm_forget_bias', 'm_conv_w', 'm_group_norm_g', 'm_w_out', 'm_norm3_g', 'm_ffn2_w_gate', 'm_ffn2_w_up', 'm_ffn2_w_down', 'm_final_g', 'v_ada_w', 'v_ada_b', 'v_norm1_g', 'v_ffn1_w_gate', 'v_ffn1_w_up', 'v_ffn1_w_down', 'v_norm2_g', 'v_w_in', 'v_forget_bias', 'v_conv_w', 'v_group_norm_g', 'v_w_out', 'v_norm3_g', 'v_ffn2_w_gate', 'v_ffn2_w_up', 'v_ffn2_w_down', 'v_final_g']
TWIN_OUTPUTS = ['loss', 'grad_x', 'grad_ada_w', 'grad_ada_b', 'grad_norm1_g', 'grad_ffn1_w_gate', 'grad_ffn1_w_up', 'grad_ffn1_w_down', 'grad_norm2_g', 'grad_w_in', 'grad_forget_bias', 'grad_conv_w', 'grad_group_norm_g', 'grad_w_out', 'grad_norm3_g', 'grad_ffn2_w_gate', 'grad_ffn2_w_up', 'grad_ffn2_w_down', 'grad_final_g', 'delta_ada_w', 'delta_ada_b', 'delta_norm1_g', 'delta_ffn1_w_gate', 'delta_ffn1_w_up', 'delta_ffn1_w_down', 'delta_norm2_g', 'delta_w_in', 'delta_forget_bias', 'delta_conv_w', 'delta_group_norm_g', 'delta_w_out', 'delta_norm3_g', 'delta_ffn2_w_gate', 'delta_ffn2_w_up', 'delta_ffn2_w_down', 'delta_final_g', 'new_m_ada_w', 'new_m_ada_b', 'new_m_norm1_g', 'new_m_ffn1_w_gate', 'new_m_ffn1_w_up', 'new_m_ffn1_w_down', 'new_m_norm2_g', 'new_m_w_in', 'new_m_forget_bias', 'new_m_conv_w', 'new_m_group_norm_g', 'new_m_w_out', 'new_m_norm3_g', 'new_m_ffn2_w_gate', 'new_m_ffn2_w_up', 'new_m_ffn2_w_down', 'new_m_final_g', 'new_v_ada_w', 'new_v_ada_b', 'new_v_norm1_g', 'new_v_ffn1_w_gate', 'new_v_ffn1_w_up', 'new_v_ffn1_w_down', 'new_v_norm2_g', 'new_v_w_in', 'new_v_forget_bias', 'new_v_conv_w', 'new_v_group_norm_g', 'new_v_w_out', 'new_v_norm3_g', 'new_v_ffn2_w_gate', 'new_v_ffn2_w_up', 'new_v_ffn2_w_down', 'new_v_final_g']
TWIN_LEAF_KINDS = {'loss': 'loss', 'grad_x': 'grad_x', 'grad_ada_w': 'grad_w', 'grad_ada_b': 'grad_w', 'grad_norm1_g': 'grad_w', 'grad_ffn1_w_gate': 'grad_w', 'grad_ffn1_w_up': 'grad_w', 'grad_ffn1_w_down': 'grad_w', 'grad_norm2_g': 'grad_w', 'grad_w_in': 'grad_w', 'grad_forget_bias': 'grad_w', 'grad_conv_w': 'grad_w', 'grad_group_norm_g': 'grad_w', 'grad_w_out': 'grad_w', 'grad_norm3_g': 'grad_w', 'grad_ffn2_w_gate': 'grad_w', 'grad_ffn2_w_up': 'grad_w', 'grad_ffn2_w_down': 'grad_w', 'grad_final_g': 'grad_w', 'delta_ada_w': 'delta_w', 'delta_ada_b': 'delta_w', 'delta_norm1_g': 'delta_w', 'delta_ffn1_w_gate': 'delta_w', 'delta_ffn1_w_up': 'delta_w', 'delta_ffn1_w_down': 'delta_w', 'delta_norm2_g': 'delta_w', 'delta_w_in': 'delta_w', 'delta_forget_bias': 'delta_w', 'delta_conv_w': 'delta_w', 'delta_group_norm_g': 'delta_w', 'delta_w_out': 'delta_w', 'delta_norm3_g': 'delta_w', 'delta_ffn2_w_gate': 'delta_w', 'delta_ffn2_w_up': 'delta_w', 'delta_ffn2_w_down': 'delta_w', 'delta_final_g': 'delta_w', 'new_m_ada_w': 'new_m', 'new_m_ada_b': 'new_m', 'new_m_norm1_g': 'new_m', 'new_m_ffn1_w_gate': 'new_m', 'new_m_ffn1_w_up': 'new_m', 'new_m_ffn1_w_down': 'new_m', 'new_m_norm2_g': 'new_m', 'new_m_w_in': 'new_m', 'new_m_forget_bias': 'new_m', 'new_m_conv_w': 'new_m', 'new_m_group_norm_g': 'new_m', 'new_m_w_out': 'new_m', 'new_m_norm3_g': 'new_m', 'new_m_ffn2_w_gate': 'new_m', 'new_m_ffn2_w_up': 'new_m', 'new_m_ffn2_w_down': 'new_m', 'new_m_final_g': 'new_m', 'new_v_ada_w': 'new_v', 'new_v_ada_b': 'new_v', 'new_v_norm1_g': 'new_v', 'new_v_ffn1_w_gate': 'new_v', 'new_v_ffn1_w_up': 'new_v', 'new_v_ffn1_w_down': 'new_v', 'new_v_norm2_g': 'new_v', 'new_v_w_in': 'new_v', 'new_v_forget_bias': 'new_v', 'new_v_conv_w': 'new_v', 'new_v_group_norm_g': 'new_v', 'new_v_w_out': 'new_v', 'new_v_norm3_g': 'new_v', 'new_v_ffn2_w_gate': 'new_v', 'new_v_ffn2_w_up': 'new_v', 'new_v_ffn2_w_down': 'new_v', 'new_v_final_g': 'new_v'}


def _forward(args):
    return _fwd_reference(*[args[k] for k in FWD_PARAMS])


def _output_shape():
    out = _jax.eval_shape(lambda: _forward(_fwd_setup_inputs(0)))
    return out.shape, out.dtype

N_MICROBATCH = 1
ADAM_LR = 0.001
ADAM_B1 = 0.9
ADAM_B2 = 0.999
ADAM_EPS = 1e-08
ADAM_WD = 0.01
ADAM_STEP = 10
PER_EXAMPLE_BATCH_AXIS = {'x': 0, 'c': 0, 'loss_target': 0}
SHARED_INPUTS = []
_WEIGHT_DTYPES = {'ada_w': _jnp.float32, 'ada_b': _jnp.float32, 'norm1_g': _jnp.float32, 'ffn1_w_gate': _jnp.float32, 'ffn1_w_up': _jnp.float32, 'ffn1_w_down': _jnp.float32, 'norm2_g': _jnp.float32, 'w_in': _jnp.float32, 'forget_bias': _jnp.float32, 'conv_w': _jnp.float32, 'group_norm_g': _jnp.float32, 'w_out': _jnp.float32, 'norm3_g': _jnp.float32, 'ffn2_w_gate': _jnp.float32, 'ffn2_w_up': _jnp.float32, 'ffn2_w_down': _jnp.float32, 'final_g': _jnp.float32}
MOMENT_SCALE = {'ada_w': 6.998180e-02, 'ada_b': 1.154094e-01, 'norm1_g': 8.239244e-02, 'ffn1_w_gate': 3.904916e-02, 'ffn1_w_up': 3.780452e-02, 'ffn1_w_down': 6.288314e-02, 'norm2_g': 1.343652e-01, 'w_in': 8.919552e-02, 'forget_bias': 1.383217e-01, 'conv_w': 1.051313e-01, 'group_norm_g': 1.116061e-01, 'w_out': 1.055667e-01, 'norm3_g': 6.187898e-02, 'ffn2_w_gate': 2.958872e-02, 'ffn2_w_up': 2.899029e-02, 'ffn2_w_down': 4.859726e-02, 'final_g': 3.274505e+01}


def _to_microbatches(a, axis):
    t = _jnp.moveaxis(a, axis, 0)
    t = t.reshape((N_MICROBATCH, t.shape[0] // N_MICROBATCH) + t.shape[1:])
    return _jnp.moveaxis(t, 1, axis + 1)


def setup_inputs(seed: int = 0) -> dict:
    inp = _fwd_setup_inputs(seed)
    key = _jax.random.fold_in(_jax.random.key(seed), 7919)
    shape, _ = _output_shape()
    out = dict(inp)
    out["loss_target"] = _jax.random.normal(_jax.random.fold_in(key, 0), shape, _jnp.float32)
    for i, name in enumerate(TWIN_WEIGHTS):
        w = inp[name].astype(_jnp.float32)
        if MOMENT_SCALE is None:
            s = _jnp.sqrt(_jnp.mean(_jnp.square(w)) + 1e-30)
        else:
            s = MOMENT_SCALE[name]
        km, kv = _jax.random.split(_jax.random.fold_in(key, i + 1))
        out[name] = w
        out["m_" + name] = s * _jax.random.normal(km, w.shape, _jnp.float32)
        out["v_" + name] = (s * s) * _jax.random.uniform(kv, w.shape, _jnp.float32, 0.5, 1.5)
    if N_MICROBATCH > 1:
        for name, axis in PER_EXAMPLE_BATCH_AXIS.items():
            out[name] = _to_microbatches(out[name], axis)
    return {'x': out['x'], 'c': out['c'], 'ada_w': out['ada_w'], 'ada_b': out['ada_b'], 'norm1_g': out['norm1_g'], 'ffn1_w_gate': out['ffn1_w_gate'], 'ffn1_w_up': out['ffn1_w_up'], 'ffn1_w_down': out['ffn1_w_down'], 'norm2_g': out['norm2_g'], 'w_in': out['w_in'], 'forget_bias': out['forget_bias'], 'conv_w': out['conv_w'], 'group_norm_g': out['group_norm_g'], 'w_out': out['w_out'], 'norm3_g': out['norm3_g'], 'ffn2_w_gate': out['ffn2_w_gate'], 'ffn2_w_up': out['ffn2_w_up'], 'ffn2_w_down': out['ffn2_w_down'], 'final_g': out['final_g'], 'loss_target': out['loss_target'], 'm_ada_w': out['m_ada_w'], 'm_ada_b': out['m_ada_b'], 'm_norm1_g': out['m_norm1_g'], 'm_ffn1_w_gate': out['m_ffn1_w_gate'], 'm_ffn1_w_up': out['m_ffn1_w_up'], 'm_ffn1_w_down': out['m_ffn1_w_down'], 'm_norm2_g': out['m_norm2_g'], 'm_w_in': out['m_w_in'], 'm_forget_bias': out['m_forget_bias'], 'm_conv_w': out['m_conv_w'], 'm_group_norm_g': out['m_group_norm_g'], 'm_w_out': out['m_w_out'], 'm_norm3_g': out['m_norm3_g'], 'm_ffn2_w_gate': out['m_ffn2_w_gate'], 'm_ffn2_w_up': out['m_ffn2_w_up'], 'm_ffn2_w_down': out['m_ffn2_w_down'], 'm_final_g': out['m_final_g'], 'v_ada_w': out['v_ada_w'], 'v_ada_b': out['v_ada_b'], 'v_norm1_g': out['v_norm1_g'], 'v_ffn1_w_gate': out['v_ffn1_w_gate'], 'v_ffn1_w_up': out['v_ffn1_w_up'], 'v_ffn1_w_down': out['v_ffn1_w_down'], 'v_norm2_g': out['v_norm2_g'], 'v_w_in': out['v_w_in'], 'v_forget_bias': out['v_forget_bias'], 'v_conv_w': out['v_conv_w'], 'v_group_norm_g': out['v_group_norm_g'], 'v_w_out': out['v_w_out'], 'v_norm3_g': out['v_norm3_g'], 'v_ffn2_w_gate': out['v_ffn2_w_gate'], 'v_ffn2_w_up': out['v_ffn2_w_up'], 'v_ffn2_w_down': out['v_ffn2_w_down'], 'v_final_g': out['v_final_g']}


def _loss(weights, diff, rest, loss_target):
    with _jax.named_scope("forward"):
        args = {**rest, TWIN_DIFF_INPUT: diff, **{k: w.astype(_WEIGHT_DTYPES[k]) for k, w in weights.items()}}
        y = _forward(args)
    with _jax.named_scope("loss_head"):
        err = _jnp.square(y.astype(_jnp.float32) - loss_target)
        return 0.5 * _jnp.sum(_jnp.mean(err, axis=-1)) if err.ndim else 0.5 * err


def _adamw(w, g, m, v):
    m = ADAM_B1 * m + (1.0 - ADAM_B1) * g
    v = ADAM_B2 * v + (1.0 - ADAM_B2) * _jnp.square(g)
    m_hat = m / (1.0 - ADAM_B1 ** ADAM_STEP)
    v_hat = v / (1.0 - ADAM_B2 ** ADAM_STEP)
    delta = -ADAM_LR * (m_hat / (_jnp.sqrt(v_hat) + ADAM_EPS) + ADAM_WD * w)
    return delta, m, v


def reference(x, c, ada_w, ada_b, norm1_g, ffn1_w_gate, ffn1_w_up, ffn1_w_down, norm2_g, w_in, forget_bias, conv_w, group_norm_g, w_out, norm3_g, ffn2_w_gate, ffn2_w_up, ffn2_w_down, final_g, loss_target, m_ada_w, m_ada_b, m_norm1_g, m_ffn1_w_gate, m_ffn1_w_up, m_ffn1_w_down, m_norm2_g, m_w_in, m_forget_bias, m_conv_w, m_group_norm_g, m_w_out, m_norm3_g, m_ffn2_w_gate, m_ffn2_w_up, m_ffn2_w_down, m_final_g, v_ada_w, v_ada_b, v_norm1_g, v_ffn1_w_gate, v_ffn1_w_up, v_ffn1_w_down, v_norm2_g, v_w_in, v_forget_bias, v_conv_w, v_group_norm_g, v_w_out, v_norm3_g, v_ffn2_w_gate, v_ffn2_w_up, v_ffn2_w_down, v_final_g):
    given = dict(x=x, c=c, ada_w=ada_w, ada_b=ada_b, norm1_g=norm1_g, ffn1_w_gate=ffn1_w_gate, ffn1_w_up=ffn1_w_up, ffn1_w_down=ffn1_w_down, norm2_g=norm2_g, w_in=w_in, forget_bias=forget_bias, conv_w=conv_w, group_norm_g=group_norm_g, w_out=w_out, norm3_g=norm3_g, ffn2_w_gate=ffn2_w_gate, ffn2_w_up=ffn2_w_up, ffn2_w_down=ffn2_w_down, final_g=final_g, loss_target=loss_target, m_ada_w=m_ada_w, m_ada_b=m_ada_b, m_norm1_g=m_norm1_g, m_ffn1_w_gate=m_ffn1_w_gate, m_ffn1_w_up=m_ffn1_w_up, m_ffn1_w_down=m_ffn1_w_down, m_norm2_g=m_norm2_g, m_w_in=m_w_in, m_forget_bias=m_forget_bias, m_conv_w=m_conv_w, m_group_norm_g=m_group_norm_g, m_w_out=m_w_out, m_norm3_g=m_norm3_g, m_ffn2_w_gate=m_ffn2_w_gate, m_ffn2_w_up=m_ffn2_w_up, m_ffn2_w_down=m_ffn2_w_down, m_final_g=m_final_g, v_ada_w=v_ada_w, v_ada_b=v_ada_b, v_norm1_g=v_norm1_g, v_ffn1_w_gate=v_ffn1_w_gate, v_ffn1_w_up=v_ffn1_w_up, v_ffn1_w_down=v_ffn1_w_down, v_norm2_g=v_norm2_g, v_w_in=v_w_in, v_forget_bias=v_forget_bias, v_conv_w=v_conv_w, v_group_norm_g=v_group_norm_g, v_w_out=v_w_out, v_norm3_g=v_norm3_g, v_ffn2_w_gate=v_ffn2_w_gate, v_ffn2_w_up=v_ffn2_w_up, v_ffn2_w_down=v_ffn2_w_down, v_final_g=v_final_g)
    weights = {n: given[n] for n in TWIN_WEIGHTS}
    shared = {n: given[n] for n in SHARED_INPUTS}
    per_example = {n: given[n] for n in ['x', 'c']}
    grad_fn = _jax.value_and_grad(_loss, argnums=(0, 1))

    def one_microbatch(ex, loss_target):
        ex = dict(ex)
        diff = ex.pop(TWIN_DIFF_INPUT)
        return grad_fn(weights, diff, {**shared, **ex}, loss_target)

    if N_MICROBATCH == 1:
        loss, (grad_w, grad_x) = one_microbatch(per_example, given["loss_target"])
    else:
        def body(carry, xs):
            loss_sum, grad_sum = carry
            l_k, (gw_k, gx_k) = one_microbatch(xs[0], xs[1])
            with _jax.named_scope("update"):
                return (loss_sum + l_k, _jax.tree.map(_jnp.add, grad_sum, gw_k)), gx_k

        init = (_jnp.zeros((), _jnp.float32), _jax.tree.map(_jnp.zeros_like, weights))
        (loss, grad_w), grad_x = _jax.lax.scan(body, init, (per_example, given["loss_target"]))
    with _jax.named_scope("update"):
        delta_w, new_m, new_v = {}, {}, {}
        for n in TWIN_WEIGHTS:
            delta_w[n], new_m[n], new_v[n] = _adamw(weights[n], grad_w[n], given["m_" + n], given["v_" + n])
    return (loss, grad_x, *[grad_w[n] for n in TWIN_WEIGHTS], *[delta_w[n] for n in TWIN_WEIGHTS],
            *[new_m[n] for n in TWIN_WEIGHTS], *[new_v[n] for n in TWIN_WEIGHTS])
```

```python
import functools
import math

import jax
import jax.numpy as jnp
from jax import lax
from jax.experimental import pallas as pl
from jax.experimental.pallas import tpu as pltpu

F32 = jnp.float32
BF16 = jnp.bfloat16

HEAD_DIM = 64
CONV_K = 3
N_MOD = 9
EPS = 1e-6
ADAM_LR = 0.001
ADAM_B1 = 0.9
ADAM_B2 = 0.999
ADAM_EPS = 1e-08
ADAM_WD = 0.01
ADAM_STEP = 10

LANES = 128
N_CHIPS = 4
N_DEV = 8
VMEM_LIMIT_BYTES = 56 * 1024 * 1024
NEG_BIG = -1e30
MESH = pl.DeviceIdType.MESH

_NT = (((1,), (1,)), ((), ()))
_NN = (((1,), (0,)), ((), ()))
_TN = (((0,), (0,)), ((), ()))


def _pc(body, **kw):
    return pl.pallas_call(body, **kw)


def _params(*sem):
    return pltpu.CompilerParams(dimension_semantics=sem, vmem_limit_bytes=VMEM_LIMIT_BYTES)


def _tile(n, pref, mult):
    best = None
    t = mult
    while t <= min(n, pref):
        if n % t == 0:
            best = t
        t += mult
    return n if best is None else best


def _sds(shape, dtype):
    return jax.ShapeDtypeStruct(shape, dtype)


def _vec_spec(d):
    return pl.BlockSpec((1, d), lambda *_: (0, 0))


def _norm_mod_fwd(x, g, shift, scale, name):
    s, d = x.shape
    tr = _tile(s, 512, 16)

    def body(x_ref, g_ref, sh_ref, sc_ref, h_ref):
        xv = x_ref[...]
        rstd = lax.rsqrt(jnp.mean(xv * xv, axis=-1, keepdims=True) + EPS)
        n = xv * rstd * g_ref[...]
        h_ref[...] = (n * (1.0 + sc_ref[...]) + sh_ref[...]).astype(BF16)

    row = pl.BlockSpec((tr, d), lambda i: (i, 0))
    return _pc(body, out_shape=_sds((s, d), BF16), grid=(s // tr,),
               in_specs=[row, _vec_spec(d), _vec_spec(d), _vec_spec(d)], out_specs=row,
               compiler_params=_params("parallel"), name=name)(x, g, shift, scale)


def _norm_mod_bwd(dh, x, g, scale, dres, name):
    s, d = x.shape
    tr = _tile(s, 256, 8)

    def body(dh_ref, x_ref, g_ref, sc_ref, dres_ref, dx_ref, dsh_ref, dsc_ref, dg_ref):
        @pl.when(pl.program_id(0) == 0)
        def _():
            dsh_ref[...] = jnp.zeros_like(dsh_ref)
            dsc_ref[...] = jnp.zeros_like(dsc_ref)
            dg_ref[...] = jnp.zeros_like(dg_ref)

        xv = x_ref[...]
        dhv = dh_ref[...]
        gv = g_ref[...]
        rstd = lax.rsqrt(jnp.mean(xv * xv, axis=-1, keepdims=True) + EPS)
        xhat = xv * rstd
        dn = dhv * (1.0 + sc_ref[...])
        dsh_ref[...] += jnp.sum(dhv, axis=0, keepdims=True)
        dsc_ref[...] += jnp.sum(dhv * (xhat * gv), axis=0, keepdims=True)
        dg_ref[...] += jnp.sum(dn * xhat, axis=0, keepdims=True)
        dxh = dn * gv
        proj = jnp.mean(dxh * xhat, axis=-1, keepdims=True)
        dx_ref[...] = dres_ref[...] + rstd * (dxh - xhat * proj)

    row = pl.BlockSpec((tr, d), lambda i: (i, 0))
    vec = _vec_spec(d)
    return _pc(body, out_shape=(_sds((s, d), F32), _sds((1, d), F32), _sds((1, d), F32), _sds((1, d), F32)),
               grid=(s // tr,), in_specs=[row, row, vec, vec, row], out_specs=(row, vec, vec, vec),
               compiler_params=_params("arbitrary"), name=name)(dh, x, g, scale, dres)


def _gate_bwd(dx, f, gate, name):
    s, d = dx.shape
    tr = _tile(s, 512, 16)

    def body(dx_ref, f_ref, gate_ref, df_ref, dg_ref):
        @pl.when(pl.program_id(0) == 0)
        def _():
            dg_ref[...] = jnp.zeros_like(dg_ref)

        dxv = dx_ref[...]
        df_ref[...] = (dxv * gate_ref[...]).astype(BF16)
        dg_ref[...] += jnp.sum(dxv * f_ref[...].astype(F32), axis=0, keepdims=True)

    row = pl.BlockSpec((tr, d), lambda i: (i, 0))
    vec = _vec_spec(d)
    return _pc(body, out_shape=(_sds((s, d), BF16), _sds((1, d), F32)), grid=(s // tr,),
               in_specs=[row, row, vec], out_specs=(row, vec),
               compiler_params=_params("arbitrary"), name=name)(dx, f, gate)


def _final_loss(x, g, target, name):
    s, d = x.shape
    tr = _tile(s, 256, 8)
    nsteps = s // tr

    def body(x_ref, g_ref, t_ref, dx_ref, loss_ref, dg_ref):
        i = pl.program_id(0)

        @pl.when(i == 0)
        def _():
            loss_ref[...] = jnp.zeros_like(loss_ref)
            dg_ref[...] = jnp.zeros_like(dg_ref)

        xv = x_ref[...]
        gv = g_ref[...]
        rstd = lax.rsqrt(jnp.mean(xv * xv, axis=-1, keepdims=True) + EPS)
        xhat = xv * rstd
        err = xhat * gv - t_ref[...]
        dy = err * (1.0 / d)
        loss_ref[...] += jnp.sum(0.5 * err * dy, axis=0, keepdims=True)
        dg_ref[...] += jnp.sum(dy * xhat, axis=0, keepdims=True)
        dxh = dy * gv
        proj = jnp.mean(dxh * xhat, axis=-1, keepdims=True)
        dx_ref[...] = rstd * (dxh - xhat * proj)

        @pl.when(i == nsteps - 1)
        def _():
            loss_ref[...] = jnp.broadcast_to(jnp.sum(loss_ref[...], axis=-1, keepdims=True), loss_ref.shape)

    row = pl.BlockSpec((tr, d), lambda i: (i, 0))
    vec = _vec_spec(d)
    return _pc(body, out_shape=(_sds((s, d), F32), _sds((1, d), F32), _sds((1, d), F32)), grid=(nsteps,),
               in_specs=[row, vec, row], out_specs=(row, vec, vec),
               compiler_params=_params("arbitrary"), name=name)(x, g, target)


def _mm(lhs, rhs, dims, out_dtype, name, res=None, gate=None, aux_dtype=None):
    if dims == "nn":
        (m, k), (k2, n) = lhs.shape, rhs.shape
    elif dims == "nt":
        (m, k), (n, k2) = lhs.shape, rhs.shape
    else:
        (k, m), (k2, n) = lhs.shape, rhs.shape
    assert k == k2, (lhs.shape, rhs.shape, dims)
    tn = _tile(n, 1024, LANES)
    tm = _tile(m, 512, LANES if dims == "tn" else 16)
    tk = _tile(k, 1536, LANES)
    nk = k // tk
    dn = {"nn": _NN, "nt": _NT, "tn": _TN}[dims]
    lhs_spec = (pl.BlockSpec((tk, tm), lambda i, j, kk: (kk, i)) if dims == "tn"
                else pl.BlockSpec((tm, tk), lambda i, j, kk: (i, kk)))
    rhs_spec = (pl.BlockSpec((tn, tk), lambda i, j, kk: (j, kk)) if dims == "nt"
                else pl.BlockSpec((tk, tn), lambda i, j, kk: (kk, j)))
    out_spec = pl.BlockSpec((tm, tn), lambda i, j, kk: (i, j))
    has_res, has_gate, has_aux = res is not None, gate is not None, aux_dtype is not None

    def body(*refs):
        refs = list(refs)
        l_ref, r_ref = refs[0], refs[1]
        pos = 2
        res_ref = gate_ref = aux_ref = None
        if has_res:
            res_ref = refs[pos]; pos += 1
        if has_gate:
            gate_ref = refs[pos]; pos += 1
        out_ref = refs[pos]; pos += 1
        if has_aux:
            aux_ref = refs[pos]; pos += 1
        acc_ref = refs[pos]
        kk = pl.program_id(2)
        part = lax.dot_general(l_ref[...], r_ref[...], dn, preferred_element_type=F32)

        @pl.when(kk == 0)
        def _():
            acc_ref[...] = part

        @pl.when(kk > 0)
        def _():
            acc_ref[...] += part

        @pl.when(kk == nk - 1)
        def _():
            acc = acc_ref[...]
            if has_aux:
                aux_ref[...] = acc.astype(aux_dtype)
            if has_gate:
                acc = acc * gate_ref[...]
            if has_res:
                acc = res_ref[...] + acc
            out_ref[...] = acc.astype(out_dtype)

    in_specs = [lhs_spec, rhs_spec]
    args = [lhs, rhs]
    if has_res:
        in_specs.append(out_spec); args.append(res)
    if has_gate:
        in_specs.append(pl.BlockSpec((1, tn), lambda i, j, kk: (0, j))); args.append(gate)
    out_shape = [_sds((m, n), out_dtype)]
    out_specs = [out_spec]
    if has_aux:
        out_shape.append(_sds((m, n), aux_dtype)); out_specs.append(out_spec)
    outs = _pc(body, out_shape=tuple(out_shape), grid=(m // tm, n // tn, nk), in_specs=in_specs,
               out_specs=tuple(out_specs), scratch_shapes=[pltpu.VMEM((tm, tn), F32)],
               compiler_params=_params("parallel", "parallel", "arbitrary"), name=name)(*args)
    return outs if has_aux else outs[0]


def _ffn_up(h, wg_t, wu_t, name):
    s, d = h.shape
    f = wg_t.shape[0]
    tm = _tile(s, 1024, 16)
    tn = _tile(f, 256, LANES)

    def body(h_ref, wg_ref, wu_ref, a_ref, u_ref, hid_ref):
        hv = h_ref[...]
        a = lax.dot_general(hv, wg_ref[...], _NT, preferred_element_type=F32)
        u = lax.dot_general(hv, wu_ref[...], _NT, preferred_element_type=F32)
        a_ref[...] = a.astype(BF16)
        u_ref[...] = u.astype(BF16)
        hid_ref[...] = (a * jax.nn.sigmoid(a) * u).astype(BF16)

    hs = pl.BlockSpec((tm, d), lambda i, j: (i, 0))
    ws = pl.BlockSpec((tn, d), lambda i, j: (j, 0))
    os_ = pl.BlockSpec((tm, tn), lambda i, j: (i, j))
    return _pc(body, out_shape=(_sds((s, f), BF16),) * 3, grid=(s // tm, f // tn),
               in_specs=[hs, ws, ws], out_specs=(os_, os_, os_),
               compiler_params=_params("parallel", "parallel"), name=name)(h, wg_t, wu_t)


def _ffn_dact(df, wd, a, u, name):
    s, d = df.shape
    f = wd.shape[0]
    tm = _tile(s, 1024, 16)
    tn = _tile(f, 256, LANES)

    def body(df_ref, wd_ref, a_ref, u_ref, da_ref, du_ref):
        dhid = lax.dot_general(df_ref[...], wd_ref[...], _NT, preferred_element_type=F32)
        av = a_ref[...].astype(F32)
        uv = u_ref[...].astype(F32)
        sig = jax.nn.sigmoid(av)
        da_ref[...] = (dhid * uv * (sig * (1.0 + av * (1.0 - sig)))).astype(BF16)
        du_ref[...] = (dhid * (av * sig)).astype(BF16)

    ds_ = pl.BlockSpec((tm, d), lambda i, j: (i, 0))
    ws = pl.BlockSpec((tn, d), lambda i, j: (j, 0))
    os_ = pl.BlockSpec((tm, tn), lambda i, j: (i, j))
    return _pc(body, out_shape=(_sds((s, f), BF16),) * 2, grid=(s // tm, f // tn),
               in_specs=[ds_, ws, os_, os_], out_specs=(os_, os_),
               compiler_params=_params("parallel", "parallel"), name=name)(df, wd, a, u)


def _split3(v):
    hi = v.astype(BF16)
    r1 = v - hi.astype(F32)
    mid = r1.astype(BF16)
    lo = (r1 - mid.astype(F32)).astype(BF16)
    return hi, mid, lo


def _dot3(v, mat):
    hi, mid, lo = _split3(v)
    out = lax.dot_general(hi, mat, _NN, preferred_element_type=F32)
    out += lax.dot_general(mid, mat, _NN, preferred_element_type=F32)
    out += lax.dot_general(lo, mat, _NN, preferred_element_type=F32)
    return out


def _forget_fwd(flog_t, bias, name):
    h, s = flog_t.shape
    blk = _tile(s, 512, LANES)
    tri = (jnp.arange(blk)[:, None] <= jnp.arange(blk)[None, :]).astype(BF16)

    def body(z_ref, b_ref, tri_ref, f_ref, carry):
        @pl.when(pl.program_id(0) == 0)
        def _():
            carry[...] = jnp.zeros_like(carry)

        z = z_ref[...] + b_ref[...]
        e = jnp.exp(-jnp.abs(z))
        w = 1.0 + e
        log1p_e = jnp.where(w == 1.0, e, jnp.log(w) * (e / (w - 1.0)))
        lf = jnp.minimum(z, 0.0) - log1p_e
        out = carry[...] + _dot3(lf, tri_ref[...])
        f_ref[...] = out
        carry[...] = out[:, blk - 1:blk]

    zs = pl.BlockSpec((h, blk), lambda i: (0, i))
    return _pc(body, out_shape=_sds((h, s), F32), grid=(s // blk,),
               in_specs=[zs, pl.BlockSpec((h, 1), lambda i: (0, 0)), pl.BlockSpec((blk, blk), lambda i: (0, 0))],
               out_specs=zs, scratch_shapes=[pltpu.VMEM((h, 1), F32)],
               compiler_params=_params("arbitrary"), name=name)(flog_t, bias, tri)


def _forget_bwd(df_t, flog_t, bias, name):
    h, s = flog_t.shape
    blk = _tile(s, 512, LANES)
    nb = s // blk
    tri = (jnp.arange(blk)[:, None] >= jnp.arange(blk)[None, :]).astype(BF16)

    def body(df_ref, z_ref, b_ref, tri_ref, dz_ref, db_ref, carry):
        @pl.when(pl.program_id(0) == 0)
        def _():
            carry[...] = jnp.zeros_like(carry)
            db_ref[...] = jnp.zeros_like(db_ref)

        rc = carry[...] + _dot3(df_ref[...], tri_ref[...])
        carry[...] = rc[:, 0:1]
        dz = rc * jax.nn.sigmoid(-(z_ref[...] + b_ref[...]))
        dz_ref[...] = dz
        db_ref[...] += jnp.sum(dz, axis=-1, keepdims=True)

    rev = pl.BlockSpec((h, blk), lambda i: (0, nb - 1 - i))
    col = pl.BlockSpec((h, 1), lambda i: (0, 0))
    return _pc(body, out_shape=(_sds((h, s), F32), _sds((h, 1), F32)), grid=(nb,),
               in_specs=[rev, rev, col, pl.BlockSpec((blk, blk), lambda i: (0, 0))],
               out_specs=(rev, col), scratch_shapes=[pltpu.VMEM((h, 1), F32)],
               compiler_params=_params("arbitrary"), name=name)(df_t, flog_t, bias, tri)


def _attn_tiles(s):
    return _tile(s, 512, LANES)


def _attn_fwd(qkv, fcol, frow, name):
    s = qkv.shape[0]
    a_w = qkv.shape[1] // 3
    npair = a_w // LANES
    t = _attn_tiles(s)
    nq = s // t
    scale = 1.0 / math.sqrt(HEAD_DIM)

    def body(q_ref, k_ref, v_ref, fc_ref, fr_ref, o_ref, lse_ref, m_sc, l_sc, acc_sc):
        qi = pl.program_id(1)
        ki = pl.program_id(2)
        first = lax.broadcasted_iota(jnp.int32, (1, LANES), 1) < HEAD_DIM

        @pl.when(ki == 0)
        def _():
            m_sc[...] = jnp.full_like(m_sc, NEG_BIG)
            l_sc[...] = jnp.zeros_like(l_sc)
            acc_sc[...] = jnp.zeros_like(acc_sc)

        def step(diag):
            q2, k2, v2 = q_ref[...], k_ref[...], v_ref[...]
            fc = fc_ref[0]
            fr = fr_ref[0]
            m_old = m_sc[...]
            keep = None
            if diag:
                keep = (lax.broadcasted_iota(jnp.int32, (t, t), 0) >= lax.broadcasted_iota(jnp.int32, (t, t), 1))
            m_new, rs, pv = [], [], []
            for hh in range(2):
                sel = first if hh == 0 else jnp.logical_not(first)
                qm = jnp.where(sel, q2, jnp.zeros_like(q2))
                vm = jnp.where(sel, v2, jnp.zeros_like(v2))
                sc = lax.dot_general(qm, k2, _NT, preferred_element_type=F32) * scale
                sc = sc + (fc[:, hh:hh + 1] - fr[hh:hh + 1, :])
                if diag:
                    sc = jnp.where(keep, sc, NEG_BIG)
                mo = m_old[:, hh * HEAD_DIM:hh * HEAD_DIM + 1]
                mn = jnp.maximum(mo, jnp.max(sc, axis=1, keepdims=True))
                p = jnp.exp(sc - mn)
                m_new.append(mn)
                rs.append(jnp.sum(p, axis=1, keepdims=True))
                pv.append(lax.dot_general(p.astype(BF16), vm, _NN, preferred_element_type=F32))
            m2 = jnp.where(first, m_new[0], m_new[1])
            alpha = jnp.exp(m_old - m2)
            m_sc[...] = m2
            l_sc[...] = alpha * l_sc[...] + jnp.where(first, rs[0], rs[1])
            acc_sc[...] = alpha * acc_sc[...] + pv[0] + pv[1]

        @pl.when(ki < qi)
        def _():
            step(False)

        @pl.when(ki == qi)
        def _():
            step(True)
            l2 = l_sc[...]
            o_ref[...] = acc_sc[...] / l2
            lse_ref[...] = m_sc[...] + jnp.log(l2)

    qs = pl.BlockSpec((t, LANES), lambda p, qi, ki: (qi, p))
    ks = pl.BlockSpec((t, LANES), lambda p, qi, ki: (jnp.minimum(ki, qi), npair + p))
    vs = pl.BlockSpec((t, LANES), lambda p, qi, ki: (jnp.minimum(ki, qi), 2 * npair + p))
    fcs = pl.BlockSpec((1, t, 2), lambda p, qi, ki: (p, qi, 0))
    frs = pl.BlockSpec((1, 2, t), lambda p, qi, ki: (p, 0, jnp.minimum(ki, qi)))
    return _pc(body, out_shape=(_sds((s, a_w), F32), _sds((s, a_w), F32)), grid=(npair, nq, nq),
               in_specs=[qs, ks, vs, fcs, frs], out_specs=(qs, qs),
               scratch_shapes=[pltpu.VMEM((t, LANES), F32)] * 3,
               compiler_params=_params("parallel", "arbitrary", "arbitrary"), name=name)(qkv, qkv, qkv, fcol, frow)


def _attn_bwd(qkv, do, o, lse, fcol, frow, name):
    s = qkv.shape[0]
    a_w = qkv.shape[1] // 3
    npair = a_w // LANES
    t = _attn_tiles(s)
    nq = s // t
    scale = 1.0 / math.sqrt(HEAD_DIM)

    def body(q_ref, k_ref, v_ref, do_ref, o_ref, lse_ref, fc_ref, fr_ref,
             dq_ref, dk_ref, dv_ref, df_ref, dfq_ref, dk_sc, dv_sc):
        ki = pl.program_id(1)
        qi = pl.program_id(2)
        first = lax.broadcasted_iota(jnp.int32, (1, LANES), 1) < HEAD_DIM

        @pl.when(jnp.logical_and(ki == 0, qi == 0))
        def _():
            dq_ref[...] = jnp.zeros_like(dq_ref)
            dfq_ref[...] = jnp.zeros_like(dfq_ref)

        def step(diag):
            q2, k2, v2, do2 = q_ref[...], k_ref[...], v_ref[...], do_ref[...]
            fc = fc_ref[0]
            fr = fr_ref[0]
            lse2 = lse_ref[...]
            dd = do2.astype(F32) * o_ref[...]
            keep = None
            if diag:
                keep = (lax.broadcasted_iota(jnp.int32, (t, t), 0) >= lax.broadcasted_iota(jnp.int32, (t, t), 1))
            dq_part = jnp.zeros((t, LANES), F32)
            dk_part = jnp.zeros((t, LANES), F32)
            dv_part = jnp.zeros((t, LANES), F32)
            dfs, rsum = [], []
            for hh in range(2):
                sel = first if hh == 0 else jnp.logical_not(first)
                qm = jnp.where(sel, q2, jnp.zeros_like(q2))
                km = jnp.where(sel, k2, jnp.zeros_like(k2))
                dom = jnp.where(sel, do2, jnp.zeros_like(do2))
                delta = jnp.sum(jnp.where(sel, dd, 0.0), axis=1, keepdims=True)
                sc = lax.dot_general(qm, k2, _NT, preferred_element_type=F32) * scale
                sc = sc + (fc[:, hh:hh + 1] - fr[hh:hh + 1, :])
                if diag:
                    sc = jnp.where(keep, sc, NEG_BIG)
                p = jnp.exp(sc - lse2[:, hh * HEAD_DIM:hh * HEAD_DIM + 1])
                dp = lax.dot_general(dom, v2, _NT, preferred_element_type=F32)
                dsv = p * (dp - delta)
                dfs.append(-jnp.sum(dsv, axis=0, keepdims=True))
                rsum.append(jnp.sum(dsv, axis=1, keepdims=True))
                ds_b = dsv.astype(BF16)
                dv_part += lax.dot_general(p.astype(BF16), dom, _TN, preferred_element_type=F32)
                dk_part += lax.dot_general(ds_b, qm, _TN, preferred_element_type=F32)
                dq_part += lax.dot_general(ds_b, km, _NN, preferred_element_type=F32)
            rows = pl.ds(pl.multiple_of(qi * t, t), t)
            dq_ref[rows, :] += dq_part * scale
            dfq_ref[rows, :] += jnp.where(first, rsum[0], rsum[1])
            dfv = jnp.concatenate(dfs, axis=0)
            if diag:
                dk_sc[...] = dk_part * scale
                dv_sc[...] = dv_part
                df_ref[0] = dfv
            else:
                dk_sc[...] += dk_part * scale
                dv_sc[...] += dv_part
                df_ref[0] += dfv

        @pl.when(qi == ki)
        def _():
            step(True)

        @pl.when(qi > ki)
        def _():
            step(False)

        @pl.when(qi == nq - 1)
        def _():
            dk_ref[...] = dk_sc[...].astype(BF16)
            dv_ref[...] = dv_sc[...].astype(BF16)

    qs = pl.BlockSpec((t, LANES), lambda p, ki, qi: (jnp.maximum(qi, ki), p))
    ks = pl.BlockSpec((t, LANES), lambda p, ki, qi: (ki, npair + p))
    vs = pl.BlockSpec((t, LANES), lambda p, ki, qi: (ki, 2 * npair + p))
    kout = pl.BlockSpec((t, LANES), lambda p, ki, qi: (ki, p))
    fcs = pl.BlockSpec((1, t, 2), lambda p, ki, qi: (p, jnp.maximum(qi, ki), 0))
    frs = pl.BlockSpec((1, 2, t), lambda p, ki, qi: (p, 0, ki))
    dqs = pl.BlockSpec((s, LANES), lambda p, ki, qi: (0, p))
    return _pc(body,
               out_shape=(_sds((s, a_w), F32), _sds((s, a_w), BF16), _sds((s, a_w), BF16), _sds((npair, 2, s), F32),
                          _sds((s, a_w), F32)),
               grid=(npair, nq, nq), in_specs=[qs, ks, vs, qs, qs, qs, fcs, frs],
               out_specs=(dqs, kout, kout, frs, dqs),
               scratch_shapes=[pltpu.VMEM((t, LANES), F32)] * 2,
               compiler_params=_params("parallel", "arbitrary", "arbitrary"), name=name)(
                   qkv, qkv, qkv, do, o, lse, fcol, frow)


def _shift_down(z, k, rows):
    return jnp.where(rows >= k, pltpu.roll(z, k, 0), 0.0)


def _shift_up(z, k, rows, n):
    return jnp.where(rows < n - k, pltpu.roll(z, n - k, 0), 0.0)


def _conv_fwd(bcx, conv_w, name):
    s = bcx.shape[0]
    cw = bcx.shape[1] // 3
    nb = cw // LANES

    def body(b_ref, c_ref, x_ref, w_ref, cv_ref):
        rows = lax.broadcasted_iota(jnp.int32, (s, LANES), 0)
        z = c_ref[...] * x_ref[...]
        w = w_ref[...]
        y = w[2:3, :] * z + w[1:2, :] * _shift_down(z, 1, rows) + w[0:1, :] * _shift_down(z, 2, rows)
        cv_ref[...] = b_ref[...] * y

    def col(off):
        return pl.BlockSpec((s, LANES), lambda j: (0, j + off))

    return _pc(body, out_shape=_sds((s, cw), F32), grid=(nb,),
               in_specs=[col(0), col(nb), col(2 * nb), pl.BlockSpec((CONV_K, LANES), lambda j: (0, j))],
               out_specs=col(0), compiler_params=_params("parallel"), name=name)(bcx, bcx, bcx, conv_w)


def _conv_bwd(dcv, bcx, conv_w, name):
    s = bcx.shape[0]
    cw = bcx.shape[1] // 3
    nb = cw // LANES

    def body(dcv_ref, b_ref, c_ref, x_ref, w_ref, db_ref, dc_ref, dxc_ref, dw_ref):
        rows = lax.broadcasted_iota(jnp.int32, (s, LANES), 0)
        cv_, xv = c_ref[...], x_ref[...]
        z = cv_ * xv
        w = w_ref[...]
        z1 = _shift_down(z, 1, rows)
        z2 = _shift_down(z, 2, rows)
        y = w[2:3, :] * z + w[1:2, :] * z1 + w[0:1, :] * z2
        dcvv = dcv_ref[...]
        db_ref[...] = (dcvv * y).astype(BF16)
        dy = dcvv * b_ref[...]
        dw_ref[0:1, :] = jnp.sum(dy * z2, axis=0, keepdims=True)
        dw_ref[1:2, :] = jnp.sum(dy * z1, axis=0, keepdims=True)
        dw_ref[2:3, :] = jnp.sum(dy * z, axis=0, keepdims=True)
        dz = w[2:3, :] * dy + w[1:2, :] * _shift_up(dy, 1, rows, s) + w[0:1, :] * _shift_up(dy, 2, rows, s)
        dc_ref[...] = (dz * xv).astype(BF16)
        dxc_ref[...] = (dz * cv_).astype(BF16)

    def col(off):
        return pl.BlockSpec((s, LANES), lambda j: (0, j + off))

    wspec = pl.BlockSpec((CONV_K, LANES), lambda j: (0, j))
    db, dc, dxc, dw = _pc(body, out_shape=(_sds((s, cw), BF16),) * 3 + (_sds((CONV_K, cw), F32),), grid=(nb,),
                          in_specs=[col(0), col(0), col(nb), col(2 * nb), wspec],
                          out_specs=(col(0), col(0), col(0), wspec),
                          compiler_params=_params("parallel"), name=name)(dcv, bcx, bcx, bcx, conv_w)
    return db, dc, dxc, dw


def _group_matrix():
    idx = jnp.arange(LANES) // HEAD_DIM
    return (idx[:, None] == idx[None, :]).astype(BF16)


def _group_sum(v, gmat):
    return _dot3(v, gmat)


def _gnorm_fwd(att, cv, gg, name):
    s, a_w = att.shape
    cw = cv.shape[1]
    d = a_w + cw
    tr = _tile(s, 512, 16)
    gmat = _group_matrix()

    def body(att_ref, cv_ref, gg_ref, gm_ref, yn_ref):
        gm = gm_ref[...]
        for c0 in range(0, d, LANES):
            y = att_ref[:, c0:c0 + LANES] if c0 < a_w else cv_ref[:, c0 - a_w:c0 - a_w + LANES]
            ms = _group_sum(y * y, gm) * (1.0 / HEAD_DIM)
            yn_ref[:, c0:c0 + LANES] = (y * lax.rsqrt(ms + EPS) * gg_ref[:, c0:c0 + LANES]).astype(BF16)

    return _pc(body, out_shape=_sds((s, d), BF16), grid=(s // tr,),
               in_specs=[pl.BlockSpec((tr, a_w), lambda i: (i, 0)), pl.BlockSpec((tr, cw), lambda i: (i, 0)),
                         _vec_spec(d), pl.BlockSpec((LANES, LANES), lambda i: (0, 0))],
               out_specs=pl.BlockSpec((tr, d), lambda i: (i, 0)),
               compiler_params=_params("parallel"), name=name)(att, cv, gg, gmat)


def _gnorm_bwd(dyn, att, cv, gg, name):
    s, a_w = att.shape
    cw = cv.shape[1]
    d = a_w + cw
    tr = _tile(s, 256, 16)
    gmat = _group_matrix()

    def body(dyn_ref, att_ref, cv_ref, gg_ref, gm_ref, datt_ref, dcv_ref, dgg_ref):
        @pl.when(pl.program_id(0) == 0)
        def _():
            dgg_ref[...] = jnp.zeros_like(dgg_ref)

        gm = gm_ref[...]
        for c0 in range(0, d, LANES):
            y = att_ref[:, c0:c0 + LANES] if c0 < a_w else cv_ref[:, c0 - a_w:c0 - a_w + LANES]
            dv = dyn_ref[:, c0:c0 + LANES]
            r = lax.rsqrt(_group_sum(y * y, gm) * (1.0 / HEAD_DIM) + EPS)
            xhat = y * r
            dgg_ref[:, c0:c0 + LANES] += jnp.sum(dv * xhat, axis=0, keepdims=True)
            dxh = dv * gg_ref[:, c0:c0 + LANES]
            proj = _group_sum(dxh * xhat, gm) * (1.0 / HEAD_DIM)
            dy = r * (dxh - xhat * proj)
            if c0 < a_w:
                datt_ref[:, c0:c0 + LANES] = dy.astype(BF16)
            else:
                dcv_ref[:, c0 - a_w:c0 - a_w + LANES] = dy

    return _pc(body, out_shape=(_sds((s, a_w), BF16), _sds((s, cw), F32), _sds((1, d), F32)), grid=(s // tr,),
               in_specs=[pl.BlockSpec((tr, d), lambda i: (i, 0)), pl.BlockSpec((tr, a_w), lambda i: (i, 0)),
                         pl.BlockSpec((tr, cw), lambda i: (i, 0)), _vec_spec(d),
                         pl.BlockSpec((LANES, LANES), lambda i: (0, 0))],
               out_specs=(pl.BlockSpec((tr, a_w), lambda i: (i, 0)), pl.BlockSpec((tr, cw), lambda i: (i, 0)),
                          _vec_spec(d)),
               compiler_params=_params("arbitrary"), name=name)(dyn, att, cv, gg, gmat)


def _adamw_math(w, g, m, v):
    m_new = ADAM_B1 * m + (1.0 - ADAM_B1) * g
    v_new = ADAM_B2 * v + (1.0 - ADAM_B2) * (g * g)
    m_hat = m_new / (1.0 - ADAM_B1 ** ADAM_STEP)
    v_hat = v_new / (1.0 - ADAM_B2 ** ADAM_STEP)
    delta = -ADAM_LR * (m_hat / (jnp.sqrt(v_hat) + ADAM_EPS) + ADAM_WD * w)
    return delta, m_new, v_new


def _row_tile(r, c):
    return _tile(r, max(8, ((1 << 18) // c) // 8 * 8), 8)


def _adamw(w, g, m, v, name):
    r, c = w.shape
    tr = _row_tile(r, c)

    def body(w_ref, g_ref, m_ref, v_ref, d_ref, mo_ref, vo_ref):
        d, mn, vn = _adamw_math(w_ref[...], g_ref[...], m_ref[...], v_ref[...])
        d_ref[...] = d
        mo_ref[...] = mn
        vo_ref[...] = vn

    spec = pl.BlockSpec((tr, c), lambda i: (i, 0))
    return _pc(body, out_shape=(_sds((r, c), F32),) * 3, grid=(r // tr,), in_specs=[spec] * 4,
               out_specs=(spec,) * 3, compiler_params=_params("parallel"), name=name)(w, g, m, v)


def _ada_fwd(c16, ada_w, ada_b, name):
    d, n = ada_w.shape
    tn = _tile(n, 768, LANES)

    def body(c_ref, w_ref, b_ref, o_ref):
        cv = c_ref[...]
        sc = (cv * jax.nn.sigmoid(cv)).astype(BF16)
        o_ref[...] = lax.dot_general(sc, w_ref[...].astype(BF16), _NN, preferred_element_type=F32) + b_ref[...]

    return _pc(body, out_shape=_sds((16, n), F32), grid=(n // tn,),
               in_specs=[pl.BlockSpec((16, d), lambda j: (0, 0)), pl.BlockSpec((d, tn), lambda j: (0, j)),
                         pl.BlockSpec((1, tn), lambda j: (0, j))],
               out_specs=pl.BlockSpec((16, tn), lambda j: (0, j)),
               compiler_params=_params("parallel"), name=name)(c16, ada_w, ada_b)


def _ada_update(c16_t, dmod16, w, m, v, name):
    r, c = w.shape
    tr = _row_tile(r, c)

    def body(c_ref, dm_ref, w_ref, m_ref, v_ref, g_ref, d_ref, mo_ref, vo_ref):
        cv = c_ref[...]
        sc = (cv * jax.nn.sigmoid(cv)).astype(BF16)
        g = lax.dot_general(sc, dm_ref[...].astype(BF16), _NN, preferred_element_type=F32)
        d, mn, vn = _adamw_math(w_ref[...], g, m_ref[...], v_ref[...])
        g_ref[...] = g
        d_ref[...] = d
        mo_ref[...] = mn
        vo_ref[...] = vn

    spec = pl.BlockSpec((tr, c), lambda i: (i, 0))
    return _pc(body, out_shape=(_sds((r, c), F32),) * 4, grid=(r // tr,),
               in_specs=[pl.BlockSpec((tr, 16), lambda i: (i, 0)), pl.BlockSpec((16, c), lambda i: (0, 0)),
                         spec, spec, spec],
               out_specs=(spec,) * 4, compiler_params=_params("parallel"), name=name)(c16_t, dmod16, w, m, v)


def _add_half(dw, recv, core, name):
    _, _, r, w = dw.shape
    tr = _tile(r, 256, 16)

    def body(core_ref, a_ref, b_ref, o_ref):
        o_ref[...] = (a_ref[...].astype(F32) + b_ref[...].astype(F32)).astype(BF16)

    grid_spec = pltpu.PrefetchScalarGridSpec(
        num_scalar_prefetch=1, grid=(N_CHIPS, r // tr),
        in_specs=[pl.BlockSpec((None, None, tr, w), lambda s, i, core_ref: (s, core_ref[0], i, 0)),
                  pl.BlockSpec((None, tr, w), lambda s, i, core_ref: (s, i, 0))],
        out_specs=pl.BlockSpec((None, tr, w), lambda s, i, core_ref: (s, i, 0)))
    return _pc(body, out_shape=_sds((N_CHIPS, r, w), BF16), grid_spec=grid_spec,
               compiler_params=_params("parallel", "parallel"), name=name)(core, dw, recv)


def _sum_chips(parts, name):
    _, r, w = parts.shape
    tr = _tile(r, 256, 16)

    def body(p_ref, o_ref):
        acc = p_ref[0].astype(F32)
        for q in range(1, N_CHIPS):
            acc = acc + p_ref[q].astype(F32)
        o_ref[...] = acc

    return _pc(body, out_shape=_sds((r, w), F32), grid=(r // tr,),
               in_specs=[pl.BlockSpec((N_CHIPS, tr, w), lambda i: (0, i, 0))],
               out_specs=pl.BlockSpec((tr, w), lambda i: (i, 0)),
               compiler_params=_params("parallel"), name=name)(parts)


def _sum_devices(parts, name):
    nd, r, w = parts.shape

    def body(p_ref, o_ref):
        acc = p_ref[0]
        for q in range(1, nd):
            acc = acc + p_ref[q]
        o_ref[...] = acc

    return _pc(body, out_shape=_sds((r, w), F32), name=name)(parts)


def _place():
    x, y, c = lax.axis_index("x"), lax.axis_index("y"), lax.axis_index("c")
    chips = [(1 - x, y), (x, 1 - y), (1 - x, 1 - y)]
    return x, y, c, chips


_ANY = pl.BlockSpec(memory_space=pl.ANY)


def _all_gather_small(blk, name):
    r, w = blk.shape

    def body(x_ref, out_ref, send_sems, recv_sems, local_sem):
        x, y, c, chips = _place()
        me, sibling = (x, y, c), (x, y, 1 - c)

        def rows(px, py, pc):
            return out_ref.at[pl.ds((4 * px + 2 * py + pc) * r, r), :]

        def copy(k, block, to, src=None):
            return pltpu.make_async_remote_copy(
                src_ref=rows(*block) if src is None else src, dst_ref=rows(*block),
                send_sem=send_sems.at[k], recv_sem=recv_sems.at[k], device_id=to, device_id_type=MESH)

        mine = pltpu.make_async_copy(x_ref, rows(*me), local_sem)
        mine.start()
        first = [copy(0, me, sibling, src=x_ref)]
        first += [copy(1 + j, me, (*chip, c), src=x_ref) for j, chip in enumerate(chips)]
        for cp in first:
            cp.start()
        passed = [copy(4 + j, (*chip, c), sibling) for j, chip in enumerate(chips)]
        for j, chip in enumerate(chips):
            copy(1 + j, (*chip, c), me).wait_recv()
            passed[j].start()
        copy(0, sibling, me).wait_recv()
        for j, chip in enumerate(chips):
            copy(4 + j, (*chip, 1 - c), me).wait_recv()
        for cp in first + passed:
            cp.wait_send()
        mine.wait()

    return _pc(body, out_shape=_sds((N_DEV * r, w), blk.dtype),
               in_specs=[pl.BlockSpec(memory_space=pltpu.VMEM)], out_specs=pl.BlockSpec(memory_space=pltpu.VMEM),
               scratch_shapes=[pltpu.SemaphoreType.DMA((7,)), pltpu.SemaphoreType.DMA((7,)), pltpu.SemaphoreType.DMA],
               name=name)(blk)


def _gather_shards(shards, name):
    n = len(shards)

    def body(*refs):
        own, out = refs[:n], refs[n:2 * n]
        send_sems, recv_sems, local_sems = refs[2 * n:]
        x, y, c, chips = _place()
        me, sibling = (x, y, c), (x, y, 1 - c)
        my_chip = 2 * x + y

        def copy(k, i, chip_idx, half, to, src=None):
            dst = out[i].at[chip_idx, half]
            return pltpu.make_async_remote_copy(
                src_ref=dst if src is None else src, dst_ref=dst,
                send_sem=send_sems.at[k], recv_sem=recv_sems.at[k], device_id=to, device_id_type=MESH)

        local = [pltpu.make_async_copy(own[i], out[i].at[my_chip], local_sems.at[i]) for i in range(n)]
        for cp in local:
            cp.start()
        first = [copy(3 * i + j, i, my_chip, c, (*chip, c), src=own[i].at[c])
                 for i in range(n) for j, chip in enumerate(chips)]
        for cp in first:
            cp.start()
        passed = []
        for i in range(n):
            for j, chip in enumerate(chips):
                idx = 2 * chip[0] + chip[1]
                copy(3 * i + j, i, idx, c, me).wait_recv()
                fw = copy(3 * n + 3 * i + j, i, idx, c, sibling)
                fw.start()
                passed.append(fw)
        for i in range(n):
            for j, chip in enumerate(chips):
                copy(3 * n + 3 * i + j, i, 2 * chip[0] + chip[1], 1 - c, me).wait_recv()
        for cp in first + passed:
            cp.wait_send()
        for cp in local:
            cp.wait()

    out_shape = tuple(_sds((N_CHIPS,) + s.shape, s.dtype) for s in shards)
    return _pc(body, out_shape=out_shape, in_specs=[_ANY] * n, out_specs=(_ANY,) * n,
               scratch_shapes=[pltpu.SemaphoreType.DMA((6 * n,)), pltpu.SemaphoreType.DMA((6 * n,)),
                               pltpu.SemaphoreType.DMA((n,))],
               name=name)(*shards)


def _sibling_send_halves(grads, name):
    n = len(grads)

    def body(*refs):
        src, dst = refs[:n], refs[n:2 * n]
        send_sems, recv_sems = refs[2 * n:]
        x, y, c, _ = _place()
        copies = [pltpu.make_async_remote_copy(
            src_ref=src[i].at[s, 1 - c], dst_ref=dst[i].at[s], send_sem=send_sems.at[N_CHIPS * i + s],
            recv_sem=recv_sems.at[N_CHIPS * i + s], device_id=(x, y, 1 - c), device_id_type=MESH)
            for i in range(n) for s in range(N_CHIPS)]
        for cp in copies:
            cp.start()
        for cp in copies:
            cp.wait()

    out_shape = tuple(_sds((N_CHIPS,) + g.shape[2:], g.dtype) for g in grads)
    return _pc(body, out_shape=out_shape, in_specs=[_ANY] * n, out_specs=(_ANY,) * n,
               scratch_shapes=[pltpu.SemaphoreType.DMA((N_CHIPS * n,)), pltpu.SemaphoreType.DMA((N_CHIPS * n,))],
               name=name)(*grads)


def _chip_scatter(parts, name):
    n = len(parts)

    def body(*refs):
        src, dst = refs[:n], refs[n:2 * n]
        send_sems, recv_sems, local_sems = refs[2 * n:]
        x, y, c, chips = _place()
        my_chip = 2 * x + y
        local = [pltpu.make_async_copy(src[i].at[my_chip], dst[i].at[my_chip], local_sems.at[i]) for i in range(n)]
        for cp in local:
            cp.start()

        def copy(i, j, piece, slot, to):
            return pltpu.make_async_remote_copy(
                src_ref=src[i].at[piece], dst_ref=dst[i].at[slot], send_sem=send_sems.at[3 * i + j],
                recv_sem=recv_sems.at[3 * i + j], device_id=to, device_id_type=MESH)

        sends = [copy(i, j, 2 * chip[0] + chip[1], my_chip, (*chip, c))
                 for i in range(n) for j, chip in enumerate(chips)]
        for cp in sends:
            cp.start()
        for i in range(n):
            for j, chip in enumerate(chips):
                copy(i, j, my_chip, 2 * chip[0] + chip[1], (x, y, c)).wait_recv()
        for cp in sends:
            cp.wait_send()
        for cp in local:
            cp.wait()

    out_shape = tuple(_sds(p.shape, p.dtype) for p in parts)
    return _pc(body, out_shape=out_shape, in_specs=[_ANY] * n, out_specs=(_ANY,) * n,
               scratch_shapes=[pltpu.SemaphoreType.DMA((3 * n,)), pltpu.SemaphoreType.DMA((3 * n,)),
                               pltpu.SemaphoreType.DMA((n,))],
               name=name)(*parts)


def _sibling_swap(halves, name):
    n = len(halves)

    def body(*refs):
        src, dst = refs[:n], refs[n:2 * n]
        send_sems, recv_sems, local_sems = refs[2 * n:]
        x, y, c, _ = _place()
        local = [pltpu.make_async_copy(src[i], dst[i].at[c], local_sems.at[i]) for i in range(n)]
        for cp in local:
            cp.start()
        sends = [pltpu.make_async_remote_copy(
            src_ref=src[i], dst_ref=dst[i].at[c], send_sem=send_sems.at[i], recv_sem=recv_sems.at[i],
            device_id=(x, y, 1 - c), device_id_type=MESH) for i in range(n)]
        for cp in sends:
            cp.start()
        for i in range(n):
            pltpu.make_async_remote_copy(
                src_ref=src[i], dst_ref=dst[i].at[1 - c], send_sem=send_sems.at[i], recv_sem=recv_sems.at[i],
                device_id=(x, y, 1 - c), device_id_type=MESH).wait_recv()
        for cp in sends:
            cp.wait_send()
        for cp in local:
            cp.wait()

    out_shape = tuple(_sds((2,) + h.shape, h.dtype) for h in halves)
    return _pc(body, out_shape=out_shape, in_specs=[_ANY] * n, out_specs=(_ANY,) * n,
               scratch_shapes=[pltpu.SemaphoreType.DMA((n,)), pltpu.SemaphoreType.DMA((n,)),
                               pltpu.SemaphoreType.DMA((n,))],
               name=name)(*halves)


HEAD_ROWS = 16


def _ffn_fwd(x, norm_g, shift, scale, gate, wg_t, wu_t, wd, tag):
    h = _norm_mod_fwd(x, norm_g, shift, scale, tag + "_norm_fwd")
    a, u, hid = _ffn_up(h, wg_t, wu_t, tag + "_up")
    x_out, f = _mm(hid, wd, "nn", F32, tag + "_down", res=x, gate=gate, aux_dtype=BF16)
    return x_out, (h, a, u, hid, f)


def _ffn_bwd(dx_out, x, saved, norm_g, scale, gate, wg_t, wu_t, wd, tag):
    h, a, u, hid, f = saved
    df, dgate = _gate_bwd(dx_out, f, gate, tag + "_gate_bwd")
    da, du = _ffn_dact(df, wd, a, u, tag + "_dact")
    dwd = _mm(hid, df, "tn", BF16, tag + "_dwd")
    dwg_t = _mm(da, h, "tn", BF16, tag + "_dwg")
    dwu_t = _mm(du, h, "tn", BF16, tag + "_dwu")
    dh = _mm(da, wg_t, "nn", F32, tag + "_dh_a")
    dh = _mm(du, wu_t, "nn", F32, tag + "_dh_u", res=dh)
    dx, dshift, dscale, dnorm_g = _norm_mod_bwd(dh, x, norm_g, scale, dx_out, tag + "_norm_bwd")
    return dx, (dshift, dscale, dgate, dnorm_g), (dwg_t, dwu_t, dwd)


def _local_step(x, target, mod, gains, forget_bias, conv_w, wts):
    sh1, sc1, g1, sh2, sc2, g2, sh3, sc3, g3 = mod
    norm1_g, norm2_g, norm3_g, final_g, group_g = gains
    wg1_t, wu1_t, wd1, wqkv_t, wf_t, wbcx_t, w_out, wg2_t, wu2_t, wd2 = wts
    s, d = x.shape
    n_heads = wqkv_t.shape[0] // (3 * HEAD_DIM)
    npair = n_heads // 2
    gate1, gate3 = 0.5 * g1, 0.5 * g3

    x1, saved1 = _ffn_fwd(x, norm1_g, sh1, sc1, gate1, wg1_t, wu1_t, wd1, "ffn1")

    h2 = _norm_mod_fwd(x1, norm2_g, sh2, sc2, "mix_norm_fwd")
    qkv = _mm(h2, wqkv_t, "nt", BF16, "mix_proj_qkv")
    bcx = _mm(h2, wbcx_t, "nt", F32, "mix_proj_bcx")
    flog = _mm(h2, wf_t, "nt", F32, "mix_proj_f")
    flog_t = jnp.pad(flog[:, :n_heads].T, ((0, HEAD_ROWS - n_heads), (0, 0)))
    bias_col = jnp.pad(forget_bias, (0, HEAD_ROWS - n_heads))[:, None]
    f_rows = _forget_fwd(flog_t, bias_col, "forget_fwd")
    frow = f_rows[:n_heads].reshape(npair, 2, s)
    fcol = frow.transpose(0, 2, 1)
    att, lse = _attn_fwd(qkv, fcol, frow, "attn_fwd")
    cv = _conv_fwd(bcx, conv_w, "conv_fwd")
    yn = _gnorm_fwd(att, cv, group_g, "gnorm_fwd")
    x2, mix = _mm(yn, w_out, "nn", F32, "mix_out", res=x1, gate=g2, aux_dtype=BF16)

    x3, saved3 = _ffn_fwd(x2, norm3_g, sh3, sc3, gate3, wg2_t, wu2_t, wd2, "ffn2")

    dx3, loss_row, dfinal_g = _final_loss(x3, final_g, target, "final_loss")

    dx2, (dsh3, dsc3, dgate3, dnorm3_g), dw_ffn2 = _ffn_bwd(
        dx3, x2, saved3, norm3_g, sc3, gate3, wg2_t, wu2_t, wd2, "ffn2")

    dmix, dg2 = _gate_bwd(dx2, mix, g2, "mix_gate_bwd")
    dyn = _mm(dmix, w_out, "nt", F32, "mix_out_dyn")
    dw_out = _mm(yn, dmix, "tn", BF16, "mix_out_dw")
    datt, dcv, dgroup_g = _gnorm_bwd(dyn, att, cv, group_g, "gnorm_bwd")
    db, dc, dxc, dconv_w = _conv_bwd(dcv, bcx, conv_w, "conv_bwd")
    dbcx = jnp.concatenate([db, dc, dxc], axis=1)
    dq, dk, dv, df_key, df_query = _attn_bwd(qkv, datt, att, lse, fcol, frow, "attn_bwd")
    dqkv = jnp.concatenate([dq.astype(BF16), dk, dv], axis=1)
    df_heads = df_key.reshape(n_heads, s) + df_query[:, ::HEAD_DIM].T
    df_t = jnp.pad(df_heads, ((0, HEAD_ROWS - n_heads), (0, 0)))
    dflog_t, dbias_col = _forget_bwd(df_t, flog_t, bias_col, "forget_bwd")
    dflog = jnp.pad(dflog_t[:n_heads].T, ((0, 0), (0, LANES - n_heads))).astype(BF16)
    dh2 = _mm(dqkv, wqkv_t, "nn", F32, "mix_dh_qkv")
    dh2 = _mm(dbcx, wbcx_t, "nn", F32, "mix_dh_bcx", res=dh2)
    dh2 = _mm(dflog, wf_t, "nn", F32, "mix_dh_f", res=dh2)
    dwqkv_t = _mm(dqkv, h2, "tn", BF16, "mix_dw_qkv")
    dwbcx_t = _mm(dbcx, h2, "tn", BF16, "mix_dw_bcx")
    dwf_t = _mm(dflog, h2, "tn", BF16, "mix_dw_f")
    dx1, dsh2, dsc2, dnorm2_g = _norm_mod_bwd(dh2, x1, norm2_g, sc2, dx2, "mix_norm_bwd")

    dx, (dsh1, dsc1, dgate1, dnorm1_g), dw_ffn1 = _ffn_bwd(
        dx1, x, saved1, norm1_g, sc1, gate1, wg1_t, wu1_t, wd1, "ffn1")

    dmod = [dsh1, dsc1, 0.5 * dgate1, dsh2, dsc2, dg2, dsh3, dsc3, 0.5 * dgate3]
    dgains = [dnorm1_g, dnorm2_g, dnorm3_g, dfinal_g, dgroup_g]
    dbias = dbias_col[:n_heads, 0]
    dw = dw_ffn1 + (dwqkv_t, dwf_t[:n_heads], dwbcx_t, dw_out) + dw_ffn2
    return dx, loss_row, dmod, dgains, dbias, dconv_w, dw


SMALL_ROWS = 24
ROW_GAINS, ROW_LOSS, ROW_FORGET, ROW_CONV, ROW_MOD = 0, 5, 6, 7, 10
PROW_ADA_B, PROW_GAINS, PROW_FORGET, PROW_CONV = 0, 9, 14, 15


def _round_up(n, m):
    return -(-n // m) * m


def _pad_rows(a, rows):
    return jnp.pad(a, ((0, rows - a.shape[0]), (0, 0)))


def _halves(a):
    return a.reshape(2, a.shape[0] // 2, a.shape[1])


def _rows_at(a, r0, total, width):
    return jnp.pad(a, ((r0, total - r0 - a.shape[0]), (0, width - a.shape[1])))


def kernel(x, c, ada_w, ada_b, norm1_g, ffn1_w_gate, ffn1_w_up, ffn1_w_down, norm2_g, w_in, forget_bias, conv_w, group_norm_g, w_out, norm3_g, ffn2_w_gate, ffn2_w_up, ffn2_w_down, final_g, loss_target, m_ada_w, m_ada_b, m_norm1_g, m_ffn1_w_gate, m_ffn1_w_up, m_ffn1_w_down, m_norm2_g, m_w_in, m_forget_bias, m_conv_w, m_group_norm_g, m_w_out, m_norm3_g, m_ffn2_w_gate, m_ffn2_w_up, m_ffn2_w_down, m_final_g, v_ada_w, v_ada_b, v_norm1_g, v_ffn1_w_gate, v_ffn1_w_up, v_ffn1_w_down, v_norm2_g, v_w_in, v_forget_bias, v_conv_w, v_group_norm_g, v_w_out, v_norm3_g, v_ffn2_w_gate, v_ffn2_w_up, v_ffn2_w_down, v_final_g):
    xi, yi, ci = lax.axis_index("x"), lax.axis_index("y"), lax.axis_index("c")
    chip = 2 * xi + yi
    dev = 4 * xi + 2 * yi + ci
    _, s, d = x.shape
    att_w = d // 2
    conv_width = d - att_w
    n_heads = att_w // HEAD_DIM
    in_shard = w_in.shape[1]
    in_cols = in_shard * N_CHIPS
    in_rows = _round_up(in_shard, 32)
    cs = conv_w.shape[1]
    mod_shard = ada_w.shape[1]
    assert N_MOD * d == N_CHIPS * mod_shard and conv_width == N_CHIPS * cs and n_heads % 2 == 0

    pack0 = _rows_at(c, 0, 8, d) + _rows_at(conv_w, 1, 8, d)
    got0 = _all_gather_small(pack0, "gather_cond").reshape(N_DEV, 8, d)
    c16 = _pad_rows(got0[:, 0, :], 16)
    conv_full = got0[0::2, 1:1 + CONV_K, :cs].transpose(1, 0, 2).reshape(CONV_K, conv_width)

    ada_b_mine = lax.dynamic_slice(ada_b, (chip * mod_shard,), (mod_shard,))[None, :]
    mod_part = _ada_fwd(c16, ada_w, ada_b_mine, "ada_fwd")
    got1 = _all_gather_small(mod_part, "gather_mod").reshape(N_DEV, 16, mod_shard)
    mod_mine = lax.dynamic_index_in_dim(got1[0::2], dev, axis=1, keepdims=False).reshape(N_MOD, d)
    mod = [mod_mine[i:i + 1] for i in range(N_MOD)]

    def t_bf(w):
        return w.T.astype(BF16)

    shards = [_halves(t_bf(ffn1_w_gate)), _halves(t_bf(ffn1_w_up)), _halves(ffn1_w_down.astype(BF16)),
              _halves(_pad_rows(t_bf(w_in), in_rows)), _halves(w_out.astype(BF16)),
              _halves(t_bf(ffn2_w_gate)), _halves(t_bf(ffn2_w_up)), _halves(ffn2_w_down.astype(BF16))]
    gathered = _gather_shards(shards, "gather_weights")
    wg1_t, wu1_t, wd1, w_in_pad, w_out_full, wg2_t, wu2_t, wd2 = [g.reshape(-1, d) for g in gathered]
    w_in_t = w_in_pad.reshape(N_CHIPS, in_rows, d)[:, :in_shard].reshape(in_cols, d)
    wqkv_t = w_in_t[:3 * att_w]
    wf_t = _pad_rows(w_in_t[3 * att_w:3 * att_w + n_heads], LANES)
    wbcx_t = w_in_t[3 * att_w + n_heads:]
    wts = (wg1_t, wu1_t, wd1, wqkv_t, wf_t, wbcx_t, w_out_full, wg2_t, wu2_t, wd2)

    gains = [g[None, :] for g in (norm1_g, norm2_g, norm3_g, final_g, group_norm_g)]
    dx, loss_row, dmod, dgains, dbias, dconv_w, dw = _local_step(x[0], loss_target[0], mod, gains, forget_bias, conv_full, wts)

    pack = sum(_rows_at(g, ROW_GAINS + i, SMALL_ROWS, d) for i, g in enumerate(dgains))
    pack += _rows_at(loss_row, ROW_LOSS, SMALL_ROWS, d) + _rows_at(dbias[None, :], ROW_FORGET, SMALL_ROWS, d)
    pack += _rows_at(dconv_w, ROW_CONV, SMALL_ROWS, d)
    pack += sum(_rows_at(g, ROW_MOD + i, SMALL_ROWS, d) for i, g in enumerate(dmod))
    got2 = _all_gather_small(pack, "gather_small_grads").reshape(N_DEV, SMALL_ROWS, d)
    tot = _sum_devices(got2, "sum_small_grads")
    loss = tot[ROW_LOSS, 0]
    grad_ada_b = tot[ROW_MOD:ROW_MOD + N_MOD].reshape(N_MOD * d)
    grad_conv = lax.dynamic_slice(tot[ROW_CONV:ROW_CONV + CONV_K], (0, chip * cs), (CONV_K, cs))
    dmod_all = got2[:, ROW_MOD:ROW_MOD + N_MOD, :].reshape(N_DEV, N_MOD * d)
    dmod16 = _pad_rows(lax.dynamic_slice(dmod_all, (0, chip * mod_shard), (N_DEV, mod_shard)), 16)

    dwg1_t, dwu1_t, dwd1, dwqkv_t, dwf_t, dwbcx_t, dw_out, dwg2_t, dwu2_t, dwd2 = dw
    dw_in_t = jnp.concatenate([dwqkv_t, dwf_t, dwbcx_t], axis=0).reshape(N_CHIPS, in_shard, d)
    dw_in_t = jnp.pad(dw_in_t, ((0, 0), (0, in_rows - in_shard), (0, 0)))
    big = [dwg1_t, dwu1_t, dwd1, dw_in_t, dw_out, dwg2_t, dwu2_t, dwd2]
    big = [g.reshape(N_CHIPS, 2, g.size // (2 * N_CHIPS * d), d) for g in big]
    from_sibling = _sibling_send_halves(big, "reduce_sibling")
    core = ci.astype(jnp.int32).reshape(1)
    chip_sums = [_add_half(g, r, core, "add_half_%d" % i) for i, (g, r) in enumerate(zip(big, from_sibling))]
    from_chips = _chip_scatter(chip_sums, "reduce_chips")
    half_sums = [_sum_chips(p, "sum_chips_%d" % i) for i, p in enumerate(from_chips)]
    shared = _sibling_swap(half_sums, "share_sums")
    g_wg1, g_wu1, g_wd1, g_in, g_out, g_wg2, g_wu2, g_wd2 = [t.reshape(-1, d) for t in shared]
    grads = {
        "ffn1_w_gate": g_wg1.T, "ffn1_w_up": g_wu1.T, "ffn1_w_down": g_wd1, "w_in": g_in[:in_shard].T,
        "w_out": g_out, "ffn2_w_gate": g_wg2.T, "ffn2_w_up": g_wu2.T, "ffn2_w_down": g_wd2,
    }

    params = {"ffn1_w_gate": (ffn1_w_gate, m_ffn1_w_gate, v_ffn1_w_gate), "ffn1_w_up": (ffn1_w_up, m_ffn1_w_up, v_ffn1_w_up),
              "ffn1_w_down": (ffn1_w_down, m_ffn1_w_down, v_ffn1_w_down), "w_in": (w_in, m_w_in, v_w_in),
              "w_out": (w_out, m_w_out, v_w_out), "ffn2_w_gate": (ffn2_w_gate, m_ffn2_w_gate, v_ffn2_w_gate),
              "ffn2_w_up": (ffn2_w_up, m_ffn2_w_up, v_ffn2_w_up), "ffn2_w_down": (ffn2_w_down, m_ffn2_w_down, v_ffn2_w_down)}
    out = {}
    for name_, (w, m, v) in params.items():
        out[name_] = (grads[name_],) + tuple(_adamw(w, grads[name_], m, v, "adamw_" + name_))
    c16_t = c16.T
    out["ada_w"] = tuple(_ada_update(c16_t, dmod16, ada_w, m_ada_w, v_ada_w, "adamw_ada_w"))

    def small_pack(ada_b_, gains_, forget_, conv_):
        p = _rows_at(ada_b_.reshape(N_MOD, d), PROW_ADA_B, SMALL_ROWS, d)
        p += sum(_rows_at(g[None, :], PROW_GAINS + i, SMALL_ROWS, d) for i, g in enumerate(gains_))
        p += _rows_at(forget_[None, :], PROW_FORGET, SMALL_ROWS, d) + _rows_at(conv_, PROW_CONV, SMALL_ROWS, d)
        return p

    g_gains = [tot[ROW_GAINS + i] for i in range(5)]
    g_forget = tot[ROW_FORGET, :n_heads]
    sw = small_pack(ada_b, (norm1_g, norm2_g, norm3_g, final_g, group_norm_g), forget_bias, conv_w)
    sm = small_pack(m_ada_b, (m_norm1_g, m_norm2_g, m_norm3_g, m_final_g, m_group_norm_g), m_forget_bias, m_conv_w)
    sv = small_pack(v_ada_b, (v_norm1_g, v_norm2_g, v_norm3_g, v_final_g, v_group_norm_g), v_forget_bias, v_conv_w)
    sg = small_pack(grad_ada_b, g_gains, g_forget, grad_conv)
    small = (sg,) + tuple(_adamw(sw, sg, sm, sv, "adamw_small"))

    def unpack(p):
        r = {"ada_b": p[PROW_ADA_B:PROW_ADA_B + N_MOD].reshape(N_MOD * d), "forget_bias": p[PROW_FORGET, :n_heads],
             "conv_w": p[PROW_CONV:PROW_CONV + CONV_K, :cs]}
        for i, nm in enumerate(("norm1_g", "norm2_g", "norm3_g", "final_g", "group_norm_g")):
            r[nm] = p[PROW_GAINS + i]
        return r

    small = [unpack(p) for p in small]
    order = ("ada_w", "ada_b", "norm1_g", "ffn1_w_gate", "ffn1_w_up", "ffn1_w_down", "norm2_g", "w_in", "forget_bias",
             "conv_w", "group_norm_g", "w_out", "norm3_g", "ffn2_w_gate", "ffn2_w_up", "ffn2_w_down", "final_g")
    result = [loss, dx[None]]
    for k in range(4):
        result += [out[nm][k] if nm in out else small[k][nm] for nm in order]
    return tuple(result)
```

```python
import functools
import math

import jax
import jax.numpy as jnp
from jax import lax
from jax.experimental import pallas as pl
from jax.experimental.pallas import tpu as pltpu

F32 = jnp.float32
BF16 = jnp.bfloat16

HEAD_DIM = 64
CONV_K = 3
N_MOD = 9
EPS = 1e-6
ADAM_LR = 0.001
ADAM_B1 = 0.9
ADAM_B2 = 0.999
ADAM_EPS = 1e-08
ADAM_WD = 0.01
ADAM_STEP = 10

LANES = 128
N_CHIPS = 4
N_DEV = 8
VMEM_LIMIT_BYTES = 56 * 1024 * 1024
NEG_BIG = -1e30
MESH = pl.DeviceIdType.MESH

_NT = (((1,), (1,)), ((), ()))
_NN = (((1,), (0,)), ((), ()))
_TN = (((0,), (0,)), ((), ()))


def _pc(body, **kw):
    return pl.pallas_call(body, **kw)


def _params(*sem):
    return pltpu.CompilerParams(dimension_semantics=sem, vmem_limit_bytes=VMEM_LIMIT_BYTES)


def _tile(n, pref, mult):
    best = None
    t = mult
    while t <= min(n, pref):
        if n % t == 0:
            best = t
        t += mult
    return n if best is None else best


def _sds(shape, dtype):
    return jax.ShapeDtypeStruct(shape, dtype)


def _vec_spec(d):
    return pl.BlockSpec((1, d), lambda *_: (0, 0))


def _norm_mod_fwd(x, g, shift, scale, name):
    s, d = x.shape
    tr = _tile(s, 512, 16)

    def body(x_ref, g_ref, sh_ref, sc_ref, h_ref):
        xv = x_ref[...]
        rstd = lax.rsqrt(jnp.mean(xv * xv, axis=-1, keepdims=True) + EPS)
        n = xv * rstd * g_ref[...]
        h_ref[...] = (n * (1.0 + sc_ref[...]) + sh_ref[...]).astype(BF16)

    row = pl.BlockSpec((tr, d), lambda i: (i, 0))
    return _pc(body, out_shape=_sds((s, d), BF16), grid=(s // tr,),
               in_specs=[row, _vec_spec(d), _vec_spec(d), _vec_spec(d)], out_specs=row,
               compiler_params=_params("parallel"), name=name)(x, g, shift, scale)


def _norm_mod_bwd(dh, x, g, scale, dres, name):
    s, d = x.shape
    tr = _tile(s, 256, 8)

    def body(dh_ref, x_ref, g_ref, sc_ref, dres_ref, dx_ref, dsh_ref, dsc_ref, dg_ref):
        @pl.when(pl.program_id(0) == 0)
        def _():
            dsh_ref[...] = jnp.zeros_like(dsh_ref)
            dsc_ref[...] = jnp.zeros_like(dsc_ref)
            dg_ref[...] = jnp.zeros_like(dg_ref)

        xv = x_ref[...]
        dhv = dh_ref[...]
        gv = g_ref[...]
        rstd = lax.rsqrt(jnp.mean(xv * xv, axis=-1, keepdims=True) + EPS)
        xhat = xv * rstd
        dn = dhv * (1.0 + sc_ref[...])
        dsh_ref[...] += jnp.sum(dhv, axis=0, keepdims=True)
        dsc_ref[...] += jnp.sum(dhv * (xhat * gv), axis=0, keepdims=True)
        dg_ref[...] += jnp.sum(dn * xhat, axis=0, keepdims=True)
        dxh = dn * gv
        proj = jnp.mean(dxh * xhat, axis=-1, keepdims=True)
        dx_ref[...] = dres_ref[...] + rstd * (dxh - xhat * proj)

    row = pl.BlockSpec((tr, d), lambda i: (i, 0))
    vec = _vec_spec(d)
    return _pc(body, out_shape=(_sds((s, d), F32), _sds((1, d), F32), _sds((1, d), F32), _sds((1, d), F32)),
               grid=(s // tr,), in_specs=[row, row, vec, vec, row], out_specs=(row, vec, vec, vec),
               compiler_params=_params("arbitrary"), name=name)(dh, x, g, scale, dres)


def _gate_bwd(dx, f, gate, name):
    s, d = dx.shape
    tr = _tile(s, 512, 16)

    def body(dx_ref, f_ref, gate_ref, df_ref, dg_ref):
        @pl.when(pl.program_id(0) == 0)
        def _():
            dg_ref[...] = jnp.zeros_like(dg_ref)

        dxv = dx_ref[...]
        df_ref[...] = (dxv * gate_ref[...]).astype(BF16)
        dg_ref[...] += jnp.sum(dxv * f_ref[...].astype(F32), axis=0, keepdims=True)

    row = pl.BlockSpec((tr, d), lambda i: (i, 0))
    vec = _vec_spec(d)
    return _pc(body, out_shape=(_sds((s, d), BF16), _sds((1, d), F32)), grid=(s // tr,),
               in_specs=[row, row, vec], out_specs=(row, vec),
               compiler_params=_params("arbitrary"), name=name)(dx, f, gate)


def _final_loss(x, g, target, name):
    s, d = x.shape
    tr = _tile(s, 256, 8)
    nsteps = s // tr

    def body(x_ref, g_ref, t_ref, dx_ref, loss_ref, dg_ref):
        i = pl.program_id(0)

        @pl.when(i == 0)
        def _():
            loss_ref[...] = jnp.zeros_like(loss_ref)
            dg_ref[...] = jnp.zeros_like(dg_ref)

        xv = x_ref[...]
        gv = g_ref[...]
        rstd = lax.rsqrt(jnp.mean(xv * xv, axis=-1, keepdims=True) + EPS)
        xhat = xv * rstd
        err = xhat * gv - t_ref[...]
        dy = err * (1.0 / d)
        loss_ref[...] += jnp.sum(0.5 * err * dy, axis=0, keepdims=True)
        dg_ref[...] += jnp.sum(dy * xhat, axis=0, keepdims=True)
        dxh = dy * gv
        proj = jnp.mean(dxh * xhat, axis=-1, keepdims=True)
        dx_ref[...] = rstd * (dxh - xhat * proj)

        @pl.when(i == nsteps - 1)
        def _():
            loss_ref[...] = jnp.broadcast_to(jnp.sum(loss_ref[...], axis=-1, keepdims=True), loss_ref.shape)

    row = pl.BlockSpec((tr, d), lambda i: (i, 0))
    vec = _vec_spec(d)
    return _pc(body, out_shape=(_sds((s, d), F32), _sds((1, d), F32), _sds((1, d), F32)), grid=(nsteps,),
               in_specs=[row, vec, row], out_specs=(row, vec, vec),
               compiler_params=_params("arbitrary"), name=name)(x, g, target)


def _mm(lhs, rhs, dims, out_dtype, name, res=None, gate=None, aux_dtype=None):
    if dims == "nn":
        (m, k), (k2, n) = lhs.shape, rhs.shape
    elif dims == "nt":
        (m, k), (n, k2) = lhs.shape, rhs.shape
    else:
        (k, m), (k2, n) = lhs.shape, rhs.shape
    assert k == k2, (lhs.shape, rhs.shape, dims)
    tn = _tile(n, 1024, LANES)
    tm = _tile(m, 512, LANES if dims == "tn" else 16)
    tk = _tile(k, 1536, LANES)
    nk = k // tk
    dn = {"nn": _NN, "nt": _NT, "tn": _TN}[dims]
    lhs_spec = (pl.BlockSpec((tk, tm), lambda i, j, kk: (kk, i)) if dims == "tn"
                else pl.BlockSpec((tm, tk), lambda i, j, kk: (i, kk)))
    rhs_spec = (pl.BlockSpec((tn, tk), lambda i, j, kk: (j, kk)) if dims == "nt"
                else pl.BlockSpec((tk, tn), lambda i, j, kk: (kk, j)))
    out_spec = pl.BlockSpec((tm, tn), lambda i, j, kk: (i, j))
    has_res, has_gate, has_aux = res is not None, gate is not None, aux_dtype is not None

    def body(*refs):
        refs = list(refs)
        l_ref, r_ref = refs[0], refs[1]
        pos = 2
        res_ref = gate_ref = aux_ref = None
        if has_res:
            res_ref = refs[pos]; pos += 1
        if has_gate:
            gate_ref = refs[pos]; pos += 1
        out_ref = refs[pos]; pos += 1
        if has_aux:
            aux_ref = refs[pos]; pos += 1
        acc_ref = refs[pos]
        kk = pl.program_id(2)
        part = lax.dot_general(l_ref[...], r_ref[...], dn, preferred_element_type=F32)

        @pl.when(kk == 0)
        def _():
            acc_ref[...] = part

        @pl.when(kk > 0)
        def _():
            acc_ref[...] += part

        @pl.when(kk == nk - 1)
        def _():
            acc = acc_ref[...]
            if has_aux:
                aux_ref[...] = acc.astype(aux_dtype)
            if has_gate:
                acc = acc * gate_ref[...]
            if has_res:
                acc = res_ref[...] + acc
            out_ref[...] = acc.astype(out_dtype)

    in_specs = [lhs_spec, rhs_spec]
    args = [lhs, rhs]
    if has_res:
        in_specs.append(out_spec); args.append(res)
    if has_gate:
        in_specs.append(pl.BlockSpec((1, tn), lambda i, j, kk: (0, j))); args.append(gate)
    out_shape = [_sds((m, n), out_dtype)]
    out_specs = [out_spec]
    if has_aux:
        out_shape.append(_sds((m, n), aux_dtype)); out_specs.append(out_spec)
    outs = _pc(body, out_shape=tuple(out_shape), grid=(m // tm, n // tn, nk), in_specs=in_specs,
               out_specs=tuple(out_specs), scratch_shapes=[pltpu.VMEM((tm, tn), F32)],
               compiler_params=_params("parallel", "parallel", "arbitrary"), name=name)(*args)
    return outs if has_aux else outs[0]


def _ffn_up(h, wg_t, wu_t, name):
    s, d = h.shape
    f = wg_t.shape[0]
    tm = _tile(s, 1024, 16)
    tn = _tile(f, 256, LANES)

    def body(h_ref, wg_ref, wu_ref, a_ref, u_ref, hid_ref):
        hv = h_ref[...]
        a = lax.dot_general(hv, wg_ref[...], _NT, preferred_element_type=F32)
        u = lax.dot_general(hv, wu_ref[...], _NT, preferred_element_type=F32)
        a_ref[...] = a.astype(BF16)
        u_ref[...] = u.astype(BF16)
        hid_ref[...] = (a * jax.nn.sigmoid(a) * u).astype(BF16)

    hs = pl.BlockSpec((tm, d), lambda i, j: (i, 0))
    ws = pl.BlockSpec((tn, d), lambda i, j: (j, 0))
    os_ = pl.BlockSpec((tm, tn), lambda i, j: (i, j))
    return _pc(body, out_shape=(_sds((s, f), BF16),) * 3, grid=(s // tm, f // tn),
               in_specs=[hs, ws, ws], out_specs=(os_, os_, os_),
               compiler_params=_params("parallel", "parallel"), name=name)(h, wg_t, wu_t)


def _ffn_dact(df, wd, a, u, name):
    s, d = df.shape
    f = wd.shape[0]
    tm = _tile(s, 1024, 16)
    tn = _tile(f, 256, LANES)

    def body(df_ref, wd_ref, a_ref, u_ref, da_ref, du_ref):
        dhid = lax.dot_general(df_ref[...], wd_ref[...], _NT, preferred_element_type=F32)
        av = a_ref[...].astype(F32)
        uv = u_ref[...].astype(F32)
        sig = jax.nn.sigmoid(av)
        da_ref[...] = (dhid * uv * (sig * (1.0 + av * (1.0 - sig)))).astype(BF16)
        du_ref[...] = (dhid * (av * sig)).astype(BF16)

    ds_ = pl.BlockSpec((tm, d), lambda i, j: (i, 0))
    ws = pl.BlockSpec((tn, d), lambda i, j: (j, 0))
    os_ = pl.BlockSpec((tm, tn), lambda i, j: (i, j))
    return _pc(body, out_shape=(_sds((s, f), BF16),) * 2, grid=(s // tm, f // tn),
               in_specs=[ds_, ws, os_, os_], out_specs=(os_, os_),
               compiler_params=_params("parallel", "parallel"), name=name)(df, wd, a, u)


def _split3(v):
    hi = v.astype(BF16)
    r1 = v - hi.astype(F32)
    mid = r1.astype(BF16)
    lo = (r1 - mid.astype(F32)).astype(BF16)
    return hi, mid, lo


def _dot3(v, mat):
    hi, mid, lo = _split3(v)
    out = lax.dot_general(hi, mat, _NN, preferred_element_type=F32)
    out += lax.dot_general(mid, mat, _NN, preferred_element_type=F32)
    out += lax.dot_general(lo, mat, _NN, preferred_element_type=F32)
    return out


def _forget_fwd(flog_t, bias, name):
    h, s = flog_t.shape
    blk = _tile(s, 512, LANES)
    tri = (jnp.arange(blk)[:, None] <= jnp.arange(blk)[None, :]).astype(BF16)

    def body(z_ref, b_ref, tri_ref, f_ref, carry):
        @pl.when(pl.program_id(0) == 0)
        def _():
            carry[...] = jnp.zeros_like(carry)

        z = z_ref[...] + b_ref[...]
        e = jnp.exp(-jnp.abs(z))
        w = 1.0 + e
        log1p_e = jnp.where(w == 1.0, e, jnp.log(w) * (e / (w - 1.0)))
        lf = jnp.minimum(z, 0.0) - log1p_e
        out = carry[...] + _dot3(lf, tri_ref[...])
        f_ref[...] = out
        carry[...] = out[:, blk - 1:blk]

    zs = pl.BlockSpec((h, blk), lambda i: (0, i))
    return _pc(body, out_shape=_sds((h, s), F32), grid=(s // blk,),
               in_specs=[zs, pl.BlockSpec((h, 1), lambda i: (0, 0)), pl.BlockSpec((blk, blk), lambda i: (0, 0))],
               out_specs=zs, scratch_shapes=[pltpu.VMEM((h, 1), F32)],
               compiler_params=_params("arbitrary"), name=name)(flog_t, bias, tri)


def _forget_bwd(df_t, flog_t, bias, name):
    h, s = flog_t.shape
    blk = _tile(s, 512, LANES)
    nb = s // blk
    tri = (jnp.arange(blk)[:, None] >= jnp.arange(blk)[None, :]).astype(BF16)

    def body(df_ref, z_ref, b_ref, tri_ref, dz_ref, db_ref, carry):
        @pl.when(pl.program_id(0) == 0)
        def _():
            carry[...] = jnp.zeros_like(carry)
            db_ref[...] = jnp.zeros_like(db_ref)

        rc = carry[...] + _dot3(df_ref[...], tri_ref[...])
        carry[...] = rc[:, 0:1]
        dz = rc * jax.nn.sigmoid(-(z_ref[...] + b_ref[...]))
        dz_ref[...] = dz
        db_ref[...] += jnp.sum(dz, axis=-1, keepdims=True)

    rev = pl.BlockSpec((h, blk), lambda i: (0, nb - 1 - i))
    col = pl.BlockSpec((h, 1), lambda i: (0, 0))
    return _pc(body, out_shape=(_sds((h, s), F32), _sds((h, 1), F32)), grid=(nb,),
               in_specs=[rev, rev, col, pl.BlockSpec((blk, blk), lambda i: (0, 0))],
               out_specs=(rev, col), scratch_shapes=[pltpu.VMEM((h, 1), F32)],
               compiler_params=_params("arbitrary"), name=name)(df_t, flog_t, bias, tri)


def _attn_tiles(s):
    return _tile(s, 512, LANES)


def _attn_fwd(qkv, fcol, frow, name):
    s = qkv.shape[0]
    a_w = qkv.shape[1] // 3
    npair = a_w // LANES
    t = _attn_tiles(s)
    nq = s // t
    scale = 1.0 / math.sqrt(HEAD_DIM)

    def body(q_ref, k_ref, v_ref, fc_ref, fr_ref, o_ref, lse_ref, m_sc, l_sc, acc_sc):
        qi = pl.program_id(1)
        ki = pl.program_id(2)
        first = lax.broadcasted_iota(jnp.int32, (1, LANES), 1) < HEAD_DIM

        @pl.when(ki == 0)
        def _():
            m_sc[...] = jnp.full_like(m_sc, NEG_BIG)
            l_sc[...] = jnp.zeros_like(l_sc)
            acc_sc[...] = jnp.zeros_like(acc_sc)

        def step(diag):
            q2, k2, v2 = q_ref[...], k_ref[...], v_ref[...]
            fc = fc_ref[0]
            fr = fr_ref[0]
            m_old = m_sc[...]
            keep = None
            if diag:
                keep = (lax.broadcasted_iota(jnp.int32, (t, t), 0) >= lax.broadcasted_iota(jnp.int32, (t, t), 1))
            m_new, rs, pv = [], [], []
            for hh in range(2):
                sel = first if hh == 0 else jnp.logical_not(first)
                qm = jnp.where(sel, q2, jnp.zeros_like(q2))
                vm = jnp.where(sel, v2, jnp.zeros_like(v2))
                sc = lax.dot_general(qm, k2, _NT, preferred_element_type=F32) * scale
                sc = sc + (fc[:, hh:hh + 1] - fr[hh:hh + 1, :])
                if diag:
                    sc = jnp.where(keep, sc, NEG_BIG)
                mo = m_old[:, hh * HEAD_DIM:hh * HEAD_DIM + 1]
                mn = jnp.maximum(mo, jnp.max(sc, axis=1, keepdims=True))
                p = jnp.exp(sc - mn)
                m_new.append(mn)
                rs.append(jnp.sum(p, axis=1, keepdims=True))
                pv.append(lax.dot_general(p.astype(BF16), vm, _NN, preferred_element_type=F32))
            m2 = jnp.where(first, m_new[0], m_new[1])
            alpha = jnp.exp(m_old - m2)
            m_sc[...] = m2
            l_sc[...] = alpha * l_sc[...] + jnp.where(first, rs[0], rs[1])
            acc_sc[...] = alpha * acc_sc[...] + pv[0] + pv[1]

        @pl.when(ki < qi)
        def _():
            step(False)

        @pl.when(ki == qi)
        def _():
            step(True)
            l2 = l_sc[...]
            o_ref[...] = acc_sc[...] / l2
            lse_ref[...] = m_sc[...] + jnp.log(l2)

    qs = pl.BlockSpec((t, LANES), lambda p, qi, ki: (qi, p))
    ks = pl.BlockSpec((t, LANES), lambda p, qi, ki: (jnp.minimum(ki, qi), npair + p))
    vs = pl.BlockSpec((t, LANES), lambda p, qi, ki: (jnp.minimum(ki, qi), 2 * npair + p))
    fcs = pl.BlockSpec((1, t, 2), lambda p, qi, ki: (p, qi, 0))
    frs = pl.BlockSpec((1, 2, t), lambda p, qi, ki: (p, 0, jnp.minimum(ki, qi)))
    return _pc(body, out_shape=(_sds((s, a_w), F32), _sds((s, a_w), F32)), grid=(npair, nq, nq),
               in_specs=[qs, ks, vs, fcs, frs], out_specs=(qs, qs),
               scratch_shapes=[pltpu.VMEM((t, LANES), F32)] * 3,
               compiler_params=_params("parallel", "arbitrary", "arbitrary"), name=name)(qkv, qkv, qkv, fcol, frow)


def _attn_bwd(qkv, do, o, lse, fcol, frow, name):
    s = qkv.shape[0]
    a_w = qkv.shape[1] // 3
    npair = a_w // LANES
    t = _attn_tiles(s)
    nq = s // t
    scale = 1.0 / math.sqrt(HEAD_DIM)

    def body(q_ref, k_ref, v_ref, do_ref, o_ref, lse_ref, fc_ref, fr_ref,
             dq_ref, dk_ref, dv_ref, df_ref, dfq_ref, dk_sc, dv_sc):
        ki = pl.program_id(1)
        qi = pl.program_id(2)
        first = lax.broadcasted_iota(jnp.int32, (1, LANES), 1) < HEAD_DIM

        @pl.when(jnp.logical_and(ki == 0, qi == 0))
        def _():
            dq_ref[...] = jnp.zeros_like(dq_ref)
            dfq_ref[...] = jnp.zeros_like(dfq_ref)

        def step(diag):
            q2, k2, v2, do2 = q_ref[...], k_ref[...], v_ref[...], do_ref[...]
            fc = fc_ref[0]
            fr = fr_ref[0]
            lse2 = lse_ref[...]
            dd = do2.astype(F32) * o_ref[...]
            keep = None
            if diag:
                keep = (lax.broadcasted_iota(jnp.int32, (t, t), 0) >= lax.broadcasted_iota(jnp.int32, (t, t), 1))
            dq_part = jnp.zeros((t, LANES), F32)
            dk_part = jnp.zeros((t, LANES), F32)
            dv_part = jnp.zeros((t, LANES), F32)
            dfs, rsum = [], []
            for hh in range(2):
                sel = first if hh == 0 else jnp.logical_not(first)
                qm = jnp.where(sel, q2, jnp.zeros_like(q2))
                km = jnp.where(sel, k2, jnp.zeros_like(k2))
                dom = jnp.where(sel, do2, jnp.zeros_like(do2))
                delta = jnp.sum(jnp.where(sel, dd, 0.0), axis=1, keepdims=True)
                sc = lax.dot_general(qm, k2, _NT, preferred_element_type=F32) * scale
                sc = sc + (fc[:, hh:hh + 1] - fr[hh:hh + 1, :])
                if diag:
                    sc = jnp.where(keep, sc, NEG_BIG)
                p = jnp.exp(sc - lse2[:, hh * HEAD_DIM:hh * HEAD_DIM + 1])
                dp = lax.dot_general(dom, v2, _NT, preferred_element_type=F32)
                dsv = p * (dp - delta)
                dfs.append(-jnp.sum(dsv, axis=0, keepdims=True))
                rsum.append(jnp.sum(dsv, axis=1, keepdims=True))
                ds_b = dsv.astype(BF16)
                dv_part += lax.dot_general(p.astype(BF16), dom, _TN, preferred_element_type=F32)
                dk_part += lax.dot_general(ds_b, qm, _TN, preferred_element_type=F32)
                dq_part += lax.dot_general(ds_b, km, _NN, preferred_element_type=F32)
            rows = pl.ds(pl.multiple_of(qi * t, t), t)
            dq_ref[rows, :] += dq_part * scale
            dfq_ref[rows, :] += jnp.where(first, rsum[0], rsum[1])
            dfv = jnp.concatenate(dfs, axis=0)
            if diag:
                dk_sc[...] = dk_part * scale
                dv_sc[...] = dv_part
                df_ref[0] = dfv
            else:
                dk_sc[...] += dk_part * scale
                dv_sc[...] += dv_part
                df_ref[0] += dfv

        @pl.when(qi == ki)
        def _():
            step(True)

        @pl.when(qi > ki)
        def _():
            step(False)

        @pl.when(qi == nq - 1)
        def _():
            dk_ref[...] = dk_sc[...].astype(BF16)
            dv_ref[...] = dv_sc[...].astype(BF16)

    qs = pl.BlockSpec((t, LANES), lambda p, ki, qi: (jnp.maximum(qi, ki), p))
    ks = pl.BlockSpec((t, LANES), lambda p, ki, qi: (ki, npair + p))
    vs = pl.BlockSpec((t, LANES), lambda p, ki, qi: (ki, 2 * npair + p))
    kout = pl.BlockSpec((t, LANES), lambda p, ki, qi: (ki, p))
    fcs = pl.BlockSpec((1, t, 2), lambda p, ki, qi: (p, jnp.maximum(qi, ki), 0))
    frs = pl.BlockSpec((1, 2, t), lambda p, ki, qi: (p, 0, ki))
    dqs = pl.BlockSpec((s, LANES), lambda p, ki, qi: (0, p))
    return _pc(body,
               out_shape=(_sds((s, a_w), F32), _sds((s, a_w), BF16), _sds((s, a_w), BF16), _sds((npair, 2, s), F32),
                          _sds((s, a_w), F32)),
               grid=(npair, nq, nq), in_specs=[qs, ks, vs, qs, qs, qs, fcs, frs],
               out_specs=(dqs, kout, kout, frs, dqs),
               scratch_shapes=[pltpu.VMEM((t, LANES), F32)] * 2,
               compiler_params=_params("parallel", "arbitrary", "arbitrary"), name=name)(
                   qkv, qkv, qkv, do, o, lse, fcol, frow)


def _shift_down(z, k, rows):
    return jnp.where(rows >= k, pltpu.roll(z, k, 0), 0.0)


def _shift_up(z, k, rows, n):
    return jnp.where(rows < n - k, pltpu.roll(z, n - k, 0), 0.0)


def _conv_fwd(bcx, conv_w, name):
    s = bcx.shape[0]
    cw = bcx.shape[1] // 3
    nb = cw // LANES

    def body(b_ref, c_ref, x_ref, w_ref, cv_ref):
        rows = lax.broadcasted_iota(jnp.int32, (s, LANES), 0)
        z = c_ref[...] * x_ref[...]
        w = w_ref[...]
        y = w[2:3, :] * z + w[1:2, :] * _shift_down(z, 1, rows) + w[0:1, :] * _shift_down(z, 2, rows)
        cv_ref[...] = b_ref[...] * y

    def col(off):
        return pl.BlockSpec((s, LANES), lambda j: (0, j + off))

    return _pc(body, out_shape=_sds((s, cw), F32), grid=(nb,),
               in_specs=[col(0), col(nb), col(2 * nb), pl.BlockSpec((CONV_K, LANES), lambda j: (0, j))],
               out_specs=col(0), compiler_params=_params("parallel"), name=name)(bcx, bcx, bcx, conv_w)


def _conv_bwd(dcv, bcx, conv_w, name):
    s = bcx.shape[0]
    cw = bcx.shape[1] // 3
    nb = cw // LANES

    def body(dcv_ref, b_ref, c_ref, x_ref, w_ref, db_ref, dc_ref, dxc_ref, dw_ref):
        rows = lax.broadcasted_iota(jnp.int32, (s, LANES), 0)
        cv_, xv = c_ref[...], x_ref[...]
        z = cv_ * xv
        w = w_ref[...]
        z1 = _shift_down(z, 1, rows)
        z2 = _shift_down(z, 2, rows)
        y = w[2:3, :] * z + w[1:2, :] * z1 + w[0:1, :] * z2
        dcvv = dcv_ref[...]
        db_ref[...] = (dcvv * y).astype(BF16)
        dy = dcvv * b_ref[...]
        dw_ref[0:1, :] = jnp.sum(dy * z2, axis=0, keepdims=True)
        dw_ref[1:2, :] = jnp.sum(dy * z1, axis=0, keepdims=True)
        dw_ref[2:3, :] = jnp.sum(dy * z, axis=0, keepdims=True)
        dz = w[2:3, :] * dy + w[1:2, :] * _shift_up(dy, 1, rows, s) + w[0:1, :] * _shift_up(dy, 2, rows, s)
        dc_ref[...] = (dz * xv).astype(BF16)
        dxc_ref[...] = (dz * cv_).astype(BF16)

    def col(off):
        return pl.BlockSpec((s, LANES), lambda j: (0, j + off))

    wspec = pl.BlockSpec((CONV_K, LANES), lambda j: (0, j))
    db, dc, dxc, dw = _pc(body, out_shape=(_sds((s, cw), BF16),) * 3 + (_sds((CONV_K, cw), F32),), grid=(nb,),
                          in_specs=[col(0), col(0), col(nb), col(2 * nb), wspec],
                          out_specs=(col(0), col(0), col(0), wspec),
                          compiler_params=_params("parallel"), name=name)(dcv, bcx, bcx, bcx, conv_w)
    return db, dc, dxc, dw


def _group_matrix():
    idx = jnp.arange(LANES) // HEAD_DIM
    return (idx[:, None] == idx[None, :]).astype(BF16)


def _group_sum(v, gmat):
    return _dot3(v, gmat)


def _gnorm_fwd(att, cv, gg, name):
    s, a_w = att.shape
    cw = cv.shape[1]
    d = a_w + cw
    tr = _tile(s, 512, 16)
    gmat = _group_matrix()

    def body(att_ref, cv_ref, gg_ref, gm_ref, yn_ref):
        gm = gm_ref[...]
        for c0 in range(0, d, LANES):
            y = att_ref[:, c0:c0 + LANES] if c0 < a_w else cv_ref[:, c0 - a_w:c0 - a_w + LANES]
            ms = _group_sum(y * y, gm) * (1.0 / HEAD_DIM)
            yn_ref[:, c0:c0 + LANES] = (y * lax.rsqrt(ms + EPS) * gg_ref[:, c0:c0 + LANES]).astype(BF16)

    return _pc(body, out_shape=_sds((s, d), BF16), grid=(s // tr,),
               in_specs=[pl.BlockSpec((tr, a_w), lambda i: (i, 0)), pl.BlockSpec((tr, cw), lambda i: (i, 0)),
                         _vec_spec(d), pl.BlockSpec((LANES, LANES), lambda i: (0, 0))],
               out_specs=pl.BlockSpec((tr, d), lambda i: (i, 0)),
               compiler_params=_params("parallel"), name=name)(att, cv, gg, gmat)


def _gnorm_bwd(dyn, att, cv, gg, name):
    s, a_w = att.shape
    cw = cv.shape[1]
    d = a_w + cw
    tr = _tile(s, 256, 16)
    gmat = _group_matrix()

    def body(dyn_ref, att_ref, cv_ref, gg_ref, gm_ref, datt_ref, dcv_ref, dgg_ref):
        @pl.when(pl.program_id(0) == 0)
        def _():
            dgg_ref[...] = jnp.zeros_like(dgg_ref)

        gm = gm_ref[...]
        for c0 in range(0, d, LANES):
            y = att_ref[:, c0:c0 + LANES] if c0 < a_w else cv_ref[:, c0 - a_w:c0 - a_w + LANES]
            dv = dyn_ref[:, c0:c0 + LANES]
            r = lax.rsqrt(_group_sum(y * y, gm) * (1.0 / HEAD_DIM) + EPS)
            xhat = y * r
            dgg_ref[:, c0:c0 + LANES] += jnp.sum(dv * xhat, axis=0, keepdims=True)
            dxh = dv * gg_ref[:, c0:c0 + LANES]
            proj = _group_sum(dxh * xhat, gm) * (1.0 / HEAD_DIM)
            dy = r * (dxh - xhat * proj)
            if c0 < a_w:
                datt_ref[:, c0:c0 + LANES] = dy.astype(BF16)
            else:
                dcv_ref[:, c0 - a_w:c0 - a_w + LANES] = dy

    return _pc(body, out_shape=(_sds((s, a_w), BF16), _sds((s, cw), F32), _sds((1, d), F32)), grid=(s // tr,),
               in_specs=[pl.BlockSpec((tr, d), lambda i: (i, 0)), pl.BlockSpec((tr, a_w), lambda i: (i, 0)),
                         pl.BlockSpec((tr, cw), lambda i: (i, 0)), _vec_spec(d),
                         pl.BlockSpec((LANES, LANES), lambda i: (0, 0))],
               out_specs=(pl.BlockSpec((tr, a_w), lambda i: (i, 0)), pl.BlockSpec((tr, cw), lambda i: (i, 0)),
                          _vec_spec(d)),
               compiler_params=_params("arbitrary"), name=name)(dyn, att, cv, gg, gmat)


def _adamw_math(w, g, m, v):
    m_new = ADAM_B1 * m + (1.0 - ADAM_B1) * g
    v_new = ADAM_B2 * v + (1.0 - ADAM_B2) * (g * g)
    m_hat = m_new / (1.0 - ADAM_B1 ** ADAM_STEP)
    v_hat = v_new / (1.0 - ADAM_B2 ** ADAM_STEP)
    delta = -ADAM_LR * (m_hat / (jnp.sqrt(v_hat) + ADAM_EPS) + ADAM_WD * w)
    return delta, m_new, v_new


def _row_tile(r, c):
    return _tile(r, max(8, ((1 << 18) // c) // 8 * 8), 8)


def _adamw(w, g, m, v, name):
    r, c = w.shape
    tr = _row_tile(r, c)

    def body(w_ref, g_ref, m_ref, v_ref, d_ref, mo_ref, vo_ref):
        d, mn, vn = _adamw_math(w_ref[...], g_ref[...], m_ref[...], v_ref[...])
        d_ref[...] = d
        mo_ref[...] = mn
        vo_ref[...] = vn

    spec = pl.BlockSpec((tr, c), lambda i: (i, 0))
    return _pc(body, out_shape=(_sds((r, c), F32),) * 3, grid=(r // tr,), in_specs=[spec] * 4,
               out_specs=(spec,) * 3, compiler_params=_params("parallel"), name=name)(w, g, m, v)


def _adamw_halves(w, mine, theirs, m, v, core, name):
    r2, c = w.shape
    r = r2 // 2
    assert mine.shape == (r, c) and theirs.shape == (r, c)
    tr = _row_tile(r, c)
    nb = r // tr

    def body(core_ref, w_ref, a_ref, b_ref, m_ref, v_ref, g_ref, d_ref, mo_ref, vo_ref):
        g = jnp.where(pl.program_id(0) == core_ref[0], a_ref[...], b_ref[...])
        d, mn, vn = _adamw_math(w_ref[...], g, m_ref[...], v_ref[...])
        g_ref[...] = g
        d_ref[...] = d
        mo_ref[...] = mn
        vo_ref[...] = vn

    full = pl.BlockSpec((tr, c), lambda h, i, core_ref: (h * nb + i, 0))
    half = pl.BlockSpec((tr, c), lambda h, i, core_ref: (i, 0))
    grid_spec = pltpu.PrefetchScalarGridSpec(
        num_scalar_prefetch=1, grid=(2, nb), in_specs=[full, half, half, full, full], out_specs=(full,) * 4)
    return _pc(body, out_shape=(_sds((r2, c), F32),) * 4, grid_spec=grid_spec,
               compiler_params=_params("parallel", "parallel"), name=name)(core, w, mine, theirs, m, v)


def _ada_fwd(c16, ada_w, ada_b, name):
    d, n = ada_w.shape
    tn = _tile(n, 768, LANES)

    def body(c_ref, w_ref, b_ref, o_ref):
        cv = c_ref[...]
        sc = (cv * jax.nn.sigmoid(cv)).astype(BF16)
        o_ref[...] = lax.dot_general(sc, w_ref[...].astype(BF16), _NN, preferred_element_type=F32) + b_ref[...]

    return _pc(body, out_shape=_sds((16, n), F32), grid=(n // tn,),
               in_specs=[pl.BlockSpec((16, d), lambda j: (0, 0)), pl.BlockSpec((d, tn), lambda j: (0, j)),
                         pl.BlockSpec((1, tn), lambda j: (0, j))],
               out_specs=pl.BlockSpec((16, tn), lambda j: (0, j)),
               compiler_params=_params("parallel"), name=name)(c16, ada_w, ada_b)


def _ada_update(c16_t, dmod16, w, m, v, name):
    r, c = w.shape
    tr = _row_tile(r, c)

    def body(c_ref, dm_ref, w_ref, m_ref, v_ref, g_ref, d_ref, mo_ref, vo_ref):
        cv = c_ref[...]
        sc = (cv * jax.nn.sigmoid(cv)).astype(BF16)
        g = lax.dot_general(sc, dm_ref[...].astype(BF16), _NN, preferred_element_type=F32)
        d, mn, vn = _adamw_math(w_ref[...], g, m_ref[...], v_ref[...])
        g_ref[...] = g
        d_ref[...] = d
        mo_ref[...] = mn
        vo_ref[...] = vn

    spec = pl.BlockSpec((tr, c), lambda i: (i, 0))
    return _pc(body, out_shape=(_sds((r, c), F32),) * 4, grid=(r // tr,),
               in_specs=[pl.BlockSpec((tr, 16), lambda i: (i, 0)), pl.BlockSpec((16, c), lambda i: (0, 0)),
                         spec, spec, spec],
               out_specs=(spec,) * 4, compiler_params=_params("parallel"), name=name)(c16_t, dmod16, w, m, v)


def _add_half(dw, recv, core, name):
    _, _, r, w = dw.shape
    tr = _tile(r, 256, 16)

    def body(core_ref, a_ref, b_ref, o_ref):
        o_ref[...] = (a_ref[...].astype(F32) + b_ref[...].astype(F32)).astype(BF16)

    grid_spec = pltpu.PrefetchScalarGridSpec(
        num_scalar_prefetch=1, grid=(N_CHIPS, r // tr),
        in_specs=[pl.BlockSpec((None, None, tr, w), lambda s, i, core_ref: (s, core_ref[0], i, 0)),
                  pl.BlockSpec((None, tr, w), lambda s, i, core_ref: (s, i, 0))],
        out_specs=pl.BlockSpec((None, tr, w), lambda s, i, core_ref: (s, i, 0)))
    return _pc(body, out_shape=_sds((N_CHIPS, r, w), BF16), grid_spec=grid_spec,
               compiler_params=_params("parallel", "parallel"), name=name)(core, dw, recv)


def _sum_chips(own, recv, chip, name):
    _, r, w = own.shape
    tr = _tile(r, 256, 16)

    def body(chip_ref, own_ref, p_ref, o_ref):
        acc = own_ref[...].astype(F32)
        for q in range(N_CHIPS - 1):
            acc = acc + p_ref[q].astype(F32)
        o_ref[...] = acc

    grid_spec = pltpu.PrefetchScalarGridSpec(
        num_scalar_prefetch=1, grid=(r // tr,),
        in_specs=[pl.BlockSpec((None, tr, w), lambda i, chip_ref: (chip_ref[0], i, 0)),
                  pl.BlockSpec((N_CHIPS - 1, tr, w), lambda i, chip_ref: (0, i, 0))],
        out_specs=pl.BlockSpec((tr, w), lambda i, chip_ref: (i, 0)))
    return _pc(body, out_shape=_sds((r, w), F32), grid_spec=grid_spec,
               compiler_params=_params("parallel"), name=name)(chip, own, recv)


def _sum_devices(parts, name):
    nd, r, w = parts.shape

    def body(p_ref, o_ref):
        acc = p_ref[0]
        for q in range(1, nd):
            acc = acc + p_ref[q]
        o_ref[...] = acc

    return _pc(body, out_shape=_sds((r, w), F32), name=name)(parts)


def _place():
    x, y, c = lax.axis_index("x"), lax.axis_index("y"), lax.axis_index("c")
    chips = [(1 - x, y), (x, 1 - y), (1 - x, 1 - y)]
    return x, y, c, chips


_ANY = pl.BlockSpec(memory_space=pl.ANY)


def _all_gather_small(blk, name):
    r, w = blk.shape

    def body(x_ref, out_ref, send_sems, recv_sems, local_sem):
        x, y, c, chips = _place()
        me, sibling = (x, y, c), (x, y, 1 - c)

        def rows(px, py, pc):
            return out_ref.at[pl.ds((4 * px + 2 * py + pc) * r, r), :]

        def copy(k, block, to, src=None):
            return pltpu.make_async_remote_copy(
                src_ref=rows(*block) if src is None else src, dst_ref=rows(*block),
                send_sem=send_sems.at[k], recv_sem=recv_sems.at[k], device_id=to, device_id_type=MESH)

        mine = pltpu.make_async_copy(x_ref, rows(*me), local_sem)
        mine.start()
        first = [copy(0, me, sibling, src=x_ref)]
        first += [copy(1 + j, me, (*chip, c), src=x_ref) for j, chip in enumerate(chips)]
        for cp in first:
            cp.start()
        passed = [copy(4 + j, (*chip, c), sibling) for j, chip in enumerate(chips)]
        for j, chip in enumerate(chips):
            copy(1 + j, (*chip, c), me).wait_recv()
            passed[j].start()
        copy(0, sibling, me).wait_recv()
        for j, chip in enumerate(chips):
            copy(4 + j, (*chip, 1 - c), me).wait_recv()
        for cp in first + passed:
            cp.wait_send()
        mine.wait()

    return _pc(body, out_shape=_sds((N_DEV * r, w), blk.dtype),
               in_specs=[pl.BlockSpec(memory_space=pltpu.VMEM)], out_specs=pl.BlockSpec(memory_space=pltpu.VMEM),
               scratch_shapes=[pltpu.SemaphoreType.DMA((7,)), pltpu.SemaphoreType.DMA((7,)), pltpu.SemaphoreType.DMA],
               name=name)(blk)


def _gather_shards(shards, name):
    n = len(shards)

    def body(*refs):
        own, out = refs[:n], refs[n:2 * n]
        send_sems, recv_sems = refs[2 * n:]
        x, y, c, chips = _place()
        me, sibling = (x, y, c), (x, y, 1 - c)
        my_chip = 2 * x + y

        def copy(k, i, chip_idx, half, to, src=None):
            dst = out[i].at[chip_idx, half]
            return pltpu.make_async_remote_copy(
                src_ref=dst if src is None else src, dst_ref=dst,
                send_sem=send_sems.at[k], recv_sem=recv_sems.at[k], device_id=to, device_id_type=MESH)

        local = [pltpu.make_async_remote_copy(
            src_ref=own[i], dst_ref=out[i].at[my_chip], send_sem=send_sems.at[6 * n + i],
            recv_sem=recv_sems.at[6 * n + i], device_id=sibling, device_id_type=MESH) for i in range(n)]
        first = [copy(3 * i + j, i, my_chip, c, (*chip, c), src=own[i].at[c])
                 for i in range(n) for j, chip in enumerate(chips)]
        for cp in first + local:
            cp.start()
        passed = []
        for i in range(n):
            for j, chip in enumerate(chips):
                idx = 2 * chip[0] + chip[1]
                copy(3 * i + j, i, idx, c, me).wait_recv()
                fw = copy(3 * n + 3 * i + j, i, idx, c, sibling)
                fw.start()
                passed.append(fw)
        for i in range(n):
            for j, chip in enumerate(chips):
                copy(3 * n + 3 * i + j, i, 2 * chip[0] + chip[1], 1 - c, me).wait_recv()
        for cp in first + passed:
            cp.wait_send()
        for cp in local:
            cp.wait()

    out_shape = tuple(_sds((N_CHIPS,) + s.shape, s.dtype) for s in shards)
    return _pc(body, out_shape=out_shape, in_specs=[_ANY] * n, out_specs=(_ANY,) * n,
               scratch_shapes=[pltpu.SemaphoreType.DMA((7 * n,)), pltpu.SemaphoreType.DMA((7 * n,))],
               name=name)(*shards)


def _sibling_send_halves(grads, name):
    n = len(grads)

    def body(*refs):
        src, dst = refs[:n], refs[n:2 * n]
        send_sems, recv_sems = refs[2 * n:]
        x, y, c, _ = _place()
        copies = [pltpu.make_async_remote_copy(
            src_ref=src[i].at[s, 1 - c], dst_ref=dst[i].at[s], send_sem=send_sems.at[N_CHIPS * i + s],
            recv_sem=recv_sems.at[N_CHIPS * i + s], device_id=(x, y, 1 - c), device_id_type=MESH)
            for i in range(n) for s in range(N_CHIPS)]
        for cp in copies:
            cp.start()
        for cp in copies:
            cp.wait()

    out_shape = tuple(_sds((N_CHIPS,) + g.shape[2:], g.dtype) for g in grads)
    return _pc(body, out_shape=out_shape, in_specs=[_ANY] * n, out_specs=(_ANY,) * n,
               scratch_shapes=[pltpu.SemaphoreType.DMA((N_CHIPS * n,)), pltpu.SemaphoreType.DMA((N_CHIPS * n,))],
               name=name)(*grads)


def _chip_scatter(parts, name):
    n = len(parts)

    def body(*refs):
        src, dst = refs[:n], refs[n:2 * n]
        send_sems, recv_sems = refs[2 * n:]
        x, y, c, chips = _place()

        def copy(i, j, to):
            chip = chips[j]
            return pltpu.make_async_remote_copy(
                src_ref=src[i].at[2 * chip[0] + chip[1]], dst_ref=dst[i].at[j], send_sem=send_sems.at[3 * i + j],
                recv_sem=recv_sems.at[3 * i + j], device_id=to, device_id_type=MESH)

        sends = [copy(i, j, (*chips[j], c)) for i in range(n) for j in range(3)]
        for cp in sends:
            cp.start()
        for cp in sends:
            cp.wait()

    out_shape = tuple(_sds((N_CHIPS - 1,) + p.shape[1:], p.dtype) for p in parts)
    return _pc(body, out_shape=out_shape, in_specs=[_ANY] * n, out_specs=(_ANY,) * n,
               scratch_shapes=[pltpu.SemaphoreType.DMA((3 * n,)), pltpu.SemaphoreType.DMA((3 * n,))],
               name=name)(*parts)


def _sibling_swap(halves, name):
    n = len(halves)

    def body(*refs):
        src, dst = refs[:n], refs[n:2 * n]
        send_sems, recv_sems = refs[2 * n:]
        x, y, c, _ = _place()
        sends = [pltpu.make_async_remote_copy(
            src_ref=src[i], dst_ref=dst[i], send_sem=send_sems.at[i], recv_sem=recv_sems.at[i],
            device_id=(x, y, 1 - c), device_id_type=MESH) for i in range(n)]
        for cp in sends:
            cp.start()
        for cp in sends:
            cp.wait()

    out_shape = tuple(_sds(h.shape, h.dtype) for h in halves)
    return _pc(body, out_shape=out_shape, in_specs=[_ANY] * n, out_specs=(_ANY,) * n,
               scratch_shapes=[pltpu.SemaphoreType.DMA((n,)), pltpu.SemaphoreType.DMA((n,))],
               name=name)(*halves)


HEAD_ROWS = 16


def _ffn_fwd(x, norm_g, shift, scale, gate, wg_t, wu_t, wd, tag):
    h = _norm_mod_fwd(x, norm_g, shift, scale, tag + "_norm_fwd")
    a, u, hid = _ffn_up(h, wg_t, wu_t, tag + "_up")
    x_out, f = _mm(hid, wd, "nn", F32, tag + "_down", res=x, gate=gate, aux_dtype=BF16)
    return x_out, (h, a, u, hid, f)


def _ffn_bwd(dx_out, x, saved, norm_g, scale, gate, wg_t, wu_t, wd, tag):
    h, a, u, hid, f = saved
    df, dgate = _gate_bwd(dx_out, f, gate, tag + "_gate_bwd")
    da, du = _ffn_dact(df, wd, a, u, tag + "_dact")
    dwd = _mm(hid, df, "tn", BF16, tag + "_dwd")
    dwg_t = _mm(da, h, "tn", BF16, tag + "_dwg")
    dwu_t = _mm(du, h, "tn", BF16, tag + "_dwu")
    dh = _mm(da, wg_t, "nn", F32, tag + "_dh_a")
    dh = _mm(du, wu_t, "nn", F32, tag + "_dh_u", res=dh)
    dx, dshift, dscale, dnorm_g = _norm_mod_bwd(dh, x, norm_g, scale, dx_out, tag + "_norm_bwd")
    return dx, (dshift, dscale, dgate, dnorm_g), (dwg_t, dwu_t, dwd)


def _local_step(x, target, mod, gains, forget_bias, conv_w, wts):
    sh1, sc1, g1, sh2, sc2, g2, sh3, sc3, g3 = mod
    norm1_g, norm2_g, norm3_g, final_g, group_g = gains
    wg1_t, wu1_t, wd1, wqkv_t, wf_t, wbcx_t, w_out, wg2_t, wu2_t, wd2 = wts
    s, d = x.shape
    n_heads = wqkv_t.shape[0] // (3 * HEAD_DIM)
    npair = n_heads // 2
    gate1, gate3 = 0.5 * g1, 0.5 * g3

    x1, saved1 = _ffn_fwd(x, norm1_g, sh1, sc1, gate1, wg1_t, wu1_t, wd1, "ffn1")

    h2 = _norm_mod_fwd(x1, norm2_g, sh2, sc2, "mix_norm_fwd")
    qkv = _mm(h2, wqkv_t, "nt", BF16, "mix_proj_qkv")
    bcx = _mm(h2, wbcx_t, "nt", F32, "mix_proj_bcx")
    flog = _mm(h2, wf_t, "nt", F32, "mix_proj_f")
    flog_t = jnp.pad(flog[:, :n_heads].T, ((0, HEAD_ROWS - n_heads), (0, 0)))
    bias_col = jnp.pad(forget_bias, (0, HEAD_ROWS - n_heads))[:, None]
    f_rows = _forget_fwd(flog_t, bias_col, "forget_fwd")
    frow = f_rows[:n_heads].reshape(npair, 2, s)
    fcol = frow.transpose(0, 2, 1)
    att, lse = _attn_fwd(qkv, fcol, frow, "attn_fwd")
    cv = _conv_fwd(bcx, conv_w, "conv_fwd")
    yn = _gnorm_fwd(att, cv, group_g, "gnorm_fwd")
    x2, mix = _mm(yn, w_out, "nn", F32, "mix_out", res=x1, gate=g2, aux_dtype=BF16)

    x3, saved3 = _ffn_fwd(x2, norm3_g, sh3, sc3, gate3, wg2_t, wu2_t, wd2, "ffn2")

    dx3, loss_row, dfinal_g = _final_loss(x3, final_g, target, "final_loss")

    dx2, (dsh3, dsc3, dgate3, dnorm3_g), dw_ffn2 = _ffn_bwd(
        dx3, x2, saved3, norm3_g, sc3, gate3, wg2_t, wu2_t, wd2, "ffn2")

    dmix, dg2 = _gate_bwd(dx2, mix, g2, "mix_gate_bwd")
    dyn = _mm(dmix, w_out, "nt", F32, "mix_out_dyn")
    dw_out = _mm(yn, dmix, "tn", BF16, "mix_out_dw")
    datt, dcv, dgroup_g = _gnorm_bwd(dyn, att, cv, group_g, "gnorm_bwd")
    db, dc, dxc, dconv_w = _conv_bwd(dcv, bcx, conv_w, "conv_bwd")
    dbcx = jnp.concatenate([db, dc, dxc], axis=1)
    dq, dk, dv, df_key, df_query = _attn_bwd(qkv, datt, att, lse, fcol, frow, "attn_bwd")
    dqkv = jnp.concatenate([dq.astype(BF16), dk, dv], axis=1)
    df_heads = df_key.reshape(n_heads, s) + df_query[:, ::HEAD_DIM].T
    df_t = jnp.pad(df_heads, ((0, HEAD_ROWS - n_heads), (0, 0)))
    dflog_t, dbias_col = _forget_bwd(df_t, flog_t, bias_col, "forget_bwd")
    dflog = jnp.pad(dflog_t[:n_heads].T, ((0, 0), (0, LANES - n_heads))).astype(BF16)
    dh2 = _mm(dqkv, wqkv_t, "nn", F32, "mix_dh_qkv")
    dh2 = _mm(dbcx, wbcx_t, "nn", F32, "mix_dh_bcx", res=dh2)
    dh2 = _mm(dflog, wf_t, "nn", F32, "mix_dh_f", res=dh2)
    dwqkv_t = _mm(dqkv, h2, "tn", BF16, "mix_dw_qkv")
    dwbcx_t = _mm(dbcx, h2, "tn", BF16, "mix_dw_bcx")
    dwf_t = _mm(dflog, h2, "tn", BF16, "mix_dw_f")
    dx1, dsh2, dsc2, dnorm2_g = _norm_mod_bwd(dh2, x1, norm2_g, sc2, dx2, "mix_norm_bwd")

    dx, (dsh1, dsc1, dgate1, dnorm1_g), dw_ffn1 = _ffn_bwd(
        dx1, x, saved1, norm1_g, sc1, gate1, wg1_t, wu1_t, wd1, "ffn1")

    dmod = [dsh1, dsc1, 0.5 * dgate1, dsh2, dsc2, dg2, dsh3, dsc3, 0.5 * dgate3]
    dgains = [dnorm1_g, dnorm2_g, dnorm3_g, dfinal_g, dgroup_g]
    dbias = dbias_col[:n_heads, 0]
    dw = dw_ffn1 + (dwqkv_t, dwf_t[:n_heads], dwbcx_t, dw_out) + dw_ffn2
    return dx, loss_row, dmod, dgains, dbias, dconv_w, dw


SMALL_ROWS = 24
ROW_GAINS, ROW_LOSS, ROW_FORGET, ROW_CONV, ROW_MOD = 0, 5, 6, 7, 10
PROW_ADA_B, PROW_GAINS, PROW_FORGET, PROW_CONV = 0, 9, 14, 15


def _round_up(n, m):
    return -(-n // m) * m


def _pad_rows(a, rows):
    return jnp.pad(a, ((0, rows - a.shape[0]), (0, 0)))


def _halves(a):
    return a.reshape(2, a.shape[0] // 2, a.shape[1])


def _rows_at(a, r0, total, width):
    return jnp.pad(a, ((r0, total - r0 - a.shape[0]), (0, width - a.shape[1])))


def kernel(x, c, ada_w, ada_b, norm1_g, ffn1_w_gate, ffn1_w_up, ffn1_w_down, norm2_g, w_in, forget_bias, conv_w, group_norm_g, w_out, norm3_g, ffn2_w_gate, ffn2_w_up, ffn2_w_down, final_g, loss_target, m_ada_w, m_ada_b, m_norm1_g, m_ffn1_w_gate, m_ffn1_w_up, m_ffn1_w_down, m_norm2_g, m_w_in, m_forget_bias, m_conv_w, m_group_norm_g, m_w_out, m_norm3_g, m_ffn2_w_gate, m_ffn2_w_up, m_ffn2_w_down, m_final_g, v_ada_w, v_ada_b, v_norm1_g, v_ffn1_w_gate, v_ffn1_w_up, v_ffn1_w_down, v_norm2_g, v_w_in, v_forget_bias, v_conv_w, v_group_norm_g, v_w_out, v_norm3_g, v_ffn2_w_gate, v_ffn2_w_up, v_ffn2_w_down, v_final_g):
    xi, yi, ci = lax.axis_index("x"), lax.axis_index("y"), lax.axis_index("c")
    chip = 2 * xi + yi
    dev = 4 * xi + 2 * yi + ci
    _, s, d = x.shape
    att_w = d // 2
    conv_width = d - att_w
    n_heads = att_w // HEAD_DIM
    in_shard = w_in.shape[1]
    in_cols = in_shard * N_CHIPS
    in_rows = _round_up(in_shard, 32)
    cs = conv_w.shape[1]
    mod_shard = ada_w.shape[1]
    assert N_MOD * d == N_CHIPS * mod_shard and conv_width == N_CHIPS * cs and n_heads % 2 == 0

    pack0 = _rows_at(c, 0, 8, d) + _rows_at(conv_w, 1, 8, d)
    got0 = _all_gather_small(pack0, "gather_cond").reshape(N_DEV, 8, d)
    c16 = _pad_rows(got0[:, 0, :], 16)
    conv_full = got0[0::2, 1:1 + CONV_K, :cs].transpose(1, 0, 2).reshape(CONV_K, conv_width)

    ada_b_mine = lax.dynamic_slice(ada_b, (chip * mod_shard,), (mod_shard,))[None, :]
    mod_part = _ada_fwd(c16, ada_w, ada_b_mine, "ada_fwd")
    got1 = _all_gather_small(mod_part, "gather_mod").reshape(N_DEV, 16, mod_shard)
    mod_mine = lax.dynamic_index_in_dim(got1[0::2], dev, axis=1, keepdims=False).reshape(N_MOD, d)
    mod = [mod_mine[i:i + 1] for i in range(N_MOD)]

    def t_bf(w):
        return w.T.astype(BF16)

    shards = [_halves(t_bf(ffn1_w_gate)), _halves(t_bf(ffn1_w_up)), _halves(ffn1_w_down.astype(BF16)),
              _halves(_pad_rows(t_bf(w_in), in_rows)), _halves(w_out.astype(BF16)),
              _halves(t_bf(ffn2_w_gate)), _halves(t_bf(ffn2_w_up)), _halves(ffn2_w_down.astype(BF16))]
    gathered = _gather_shards(shards, "gather_weights")
    wg1_t, wu1_t, wd1, w_in_pad, w_out_full, wg2_t, wu2_t, wd2 = [g.reshape(-1, d) for g in gathered]
    w_in_t = w_in_pad.reshape(N_CHIPS, in_rows, d)[:, :in_shard].reshape(in_cols, d)
    wqkv_t = w_in_t[:3 * att_w]
    wf_t = _pad_rows(w_in_t[3 * att_w:3 * att_w + n_heads], LANES)
    wbcx_t = w_in_t[3 * att_w + n_heads:]
    wts = (wg1_t, wu1_t, wd1, wqkv_t, wf_t, wbcx_t, w_out_full, wg2_t, wu2_t, wd2)

    gains = [g[None, :] for g in (norm1_g, norm2_g, norm3_g, final_g, group_norm_g)]
    dx, loss_row, dmod, dgains, dbias, dconv_w, dw = _local_step(x[0], loss_target[0], mod, gains, forget_bias, conv_full, wts)

    pack = sum(_rows_at(g, ROW_GAINS + i, SMALL_ROWS, d) for i, g in enumerate(dgains))
    pack += _rows_at(loss_row, ROW_LOSS, SMALL_ROWS, d) + _rows_at(dbias[None, :], ROW_FORGET, SMALL_ROWS, d)
    pack += _rows_at(dconv_w, ROW_CONV, SMALL_ROWS, d)
    pack += sum(_rows_at(g, ROW_MOD + i, SMALL_ROWS, d) for i, g in enumerate(dmod))
    got2 = _all_gather_small(pack, "gather_small_grads").reshape(N_DEV, SMALL_ROWS, d)
    tot = _sum_devices(got2, "sum_small_grads")
    loss = tot[ROW_LOSS, 0]
    grad_ada_b = tot[ROW_MOD:ROW_MOD + N_MOD].reshape(N_MOD * d)
    grad_conv = lax.dynamic_slice(tot[ROW_CONV:ROW_CONV + CONV_K], (0, chip * cs), (CONV_K, cs))
    dmod_all = got2[:, ROW_MOD:ROW_MOD + N_MOD, :].reshape(N_DEV, N_MOD * d)
    dmod16 = _pad_rows(lax.dynamic_slice(dmod_all, (0, chip * mod_shard), (N_DEV, mod_shard)), 16)

    dwg1_t, dwu1_t, dwd1, dwqkv_t, dwf_t, dwbcx_t, dw_out, dwg2_t, dwu2_t, dwd2 = dw
    dw_in_t = jnp.concatenate([dwqkv_t, dwf_t, dwbcx_t], axis=0).reshape(N_CHIPS, in_shard, d)
    dw_in_t = jnp.pad(dw_in_t, ((0, 0), (0, in_rows - in_shard), (0, 0)))
    big = [dwg1_t, dwu1_t, dwd1, dw_in_t, dw_out, dwg2_t, dwu2_t, dwd2]
    big = [g.reshape(N_CHIPS, 2, g.size // (2 * N_CHIPS * d), d) for g in big]
    from_sibling = _sibling_send_halves(big, "reduce_sibling")
    core = ci.astype(jnp.int32).reshape(1)
    chip_sums = [_add_half(g, r, core, "add_half_%d" % i) for i, (g, r) in enumerate(zip(big, from_sibling))]
    from_chips = _chip_scatter(chip_sums, "reduce_chips")
    chip_arr = chip.astype(jnp.int32).reshape(1)
    half_sums = [_sum_chips(o, p, chip_arr, "sum_chips_%d" % i) for i, (o, p) in enumerate(zip(chip_sums, from_chips))]
    from_core = _sibling_swap(half_sums, "share_sums")

    names = ("ffn1_w_gate", "ffn1_w_up", "ffn1_w_down", "w_in", "w_out", "ffn2_w_gate", "ffn2_w_up", "ffn2_w_down")
    transposed = ("ffn1_w_gate", "ffn1_w_up", "w_in", "ffn2_w_gate", "ffn2_w_up")
    params = {"ffn1_w_gate": (ffn1_w_gate, m_ffn1_w_gate, v_ffn1_w_gate), "ffn1_w_up": (ffn1_w_up, m_ffn1_w_up, v_ffn1_w_up),
              "ffn1_w_down": (ffn1_w_down, m_ffn1_w_down, v_ffn1_w_down), "w_in": (w_in, m_w_in, v_w_in),
              "w_out": (w_out, m_w_out, v_w_out), "ffn2_w_gate": (ffn2_w_gate, m_ffn2_w_gate, v_ffn2_w_gate),
              "ffn2_w_up": (ffn2_w_up, m_ffn2_w_up, v_ffn2_w_up), "ffn2_w_down": (ffn2_w_down, m_ffn2_w_down, v_ffn2_w_down)}
    out = {}
    for name_, mine, theirs in zip(names, half_sums, from_core):
        w, m, v = params[name_]
        if name_ in transposed:
            w, m, v = w.T, m.T, v.T
        if name_ == "w_in":
            both = jnp.where(ci == 0, jnp.concatenate([mine, theirs]), jnp.concatenate([theirs, mine]))[:in_shard]
            res = (both,) + tuple(_adamw(w, both, m, v, "adamw_" + name_))
        else:
            res = _adamw_halves(w, mine, theirs, m, v, core, "adamw_" + name_)
        out[name_] = tuple(r.T for r in res) if name_ in transposed else tuple(res)
    c16_t = c16.T
    out["ada_w"] = tuple(_ada_update(c16_t, dmod16, ada_w, m_ada_w, v_ada_w, "adamw_ada_w"))

    def small_pack(ada_b_, gains_, forget_, conv_):
        p = _rows_at(ada_b_.reshape(N_MOD, d), PROW_ADA_B, SMALL_ROWS, d)
        p += sum(_rows_at(g[None, :], PROW_GAINS + i, SMALL_ROWS, d) for i, g in enumerate(gains_))
        p += _rows_at(forget_[None, :], PROW_FORGET, SMALL_ROWS, d) + _rows_at(conv_, PROW_CONV, SMALL_ROWS, d)
        return p

    g_gains = [tot[ROW_GAINS + i] for i in range(5)]
    g_forget = tot[ROW_FORGET, :n_heads]
    sw = small_pack(ada_b, (norm1_g, norm2_g, norm3_g, final_g, group_norm_g), forget_bias, conv_w)
    sm = small_pack(m_ada_b, (m_norm1_g, m_norm2_g, m_norm3_g, m_final_g, m_group_norm_g), m_forget_bias, m_conv_w)
    sv = small_pack(v_ada_b, (v_norm1_g, v_norm2_g, v_norm3_g, v_final_g, v_group_norm_g), v_forget_bias, v_conv_w)
    sg = small_pack(grad_ada_b, g_gains, g_forget, grad_conv)
    small = (sg,) + tuple(_adamw(sw, sg, sm, sv, "adamw_small"))

    def unpack(p):
        r = {"ada_b": p[PROW_ADA_B:PROW_ADA_B + N_MOD].reshape(N_MOD * d), "forget_bias": p[PROW_FORGET, :n_heads],
             "conv_w": p[PROW_CONV:PROW_CONV + CONV_K, :cs]}
        for i, nm in enumerate(("norm1_g", "norm2_g", "norm3_g", "final_g", "group_norm_g")):
            r[nm] = p[PROW_GAINS + i]
        return r

    small = [unpack(p) for p in small]
    order = ("ada_w", "ada_b", "norm1_g", "ffn1_w_gate", "ffn1_w_up", "ffn1_w_down", "norm2_g", "w_in", "forget_bias",
             "conv_w", "group_norm_g", "w_out", "norm3_g", "ffn2_w_gate", "ffn2_w_up", "ffn2_w_down", "final_g")
    result = [loss, dx[None]]
    for k in range(4):
        result += [out[nm][k] if nm in out else small[k][nm] for nm in order]
    return tuple(result)
```

```python
import functools
import math

import jax
import jax.numpy as jnp
from jax import lax
from jax.experimental import pallas as pl
from jax.experimental.pallas import tpu as pltpu

F32 = jnp.float32
BF16 = jnp.bfloat16

HEAD_DIM = 64
CONV_K = 3
N_MOD = 9
EPS = 1e-6
ADAM_LR = 0.001
ADAM_B1 = 0.9
ADAM_B2 = 0.999
ADAM_EPS = 1e-08
ADAM_WD = 0.01
ADAM_STEP = 10

LANES = 128
N_CHIPS = 4
N_DEV = 8
VMEM_LIMIT_BYTES = 56 * 1024 * 1024
NEG_BIG = -1e30
MESH = pl.DeviceIdType.MESH

_NT = (((1,), (1,)), ((), ()))
_NN = (((1,), (0,)), ((), ()))
_TN = (((0,), (0,)), ((), ()))


def _params(*sem):
    return pltpu.CompilerParams(dimension_semantics=sem, vmem_limit_bytes=VMEM_LIMIT_BYTES)


class _Exchange:
    def __init__(self, inputs, out_shapes, n_sems, start, finish, aliases=None):
        self.inputs, self.out_shapes, self.n_sems = list(inputs), list(out_shapes), n_sems
        self.start, self.finish, self.aliases = start, finish, dict(aliases or {})
        self.results = None


def _pc(body, exchange=None, **kw):
    if exchange is None:
        return pl.pallas_call(body, **kw)
    grid = kw["grid"]
    single = not isinstance(kw["out_shape"], (tuple, list))
    out_shape = [kw["out_shape"]] if single else list(kw["out_shape"])
    out_specs = [kw["out_specs"]] if single else list(kw["out_specs"])
    in_specs = list(kw["in_specs"])
    scratch = list(kw.get("scratch_shapes", ()))
    n_in, n_out, n_scr = len(in_specs), len(out_shape), len(scratch)
    n_xi, n_xo = len(exchange.inputs), len(exchange.out_shapes)

    def wrapped(*refs):
        pos = [n_in, n_in + n_xi, n_in + n_xi + n_out, n_in + n_xi + n_out + n_xo]
        ins, x_in, outs, x_out = refs[:pos[0]], refs[pos[0]:pos[1]], refs[pos[1]:pos[2]], refs[pos[2]:pos[3]]
        scr = refs[pos[3]:pos[3] + n_scr]
        send_sems, recv_sems = refs[pos[3] + n_scr:]
        ids = [pl.program_id(a) for a in range(len(grid))]
        first = functools.reduce(jnp.logical_and, [i == 0 for i in ids])
        last = functools.reduce(jnp.logical_and, [i == g - 1 for i, g in zip(ids, grid)])

        @pl.when(first)
        def _():
            exchange.start(x_in, x_out, send_sems, recv_sems)

        body(*ins, *outs, *scr)

        @pl.when(last)
        def _():
            exchange.finish(x_in, x_out, send_sems, recv_sems)

    call = pl.pallas_call(
        wrapped, out_shape=tuple(out_shape) + tuple(exchange.out_shapes), grid=grid,
        in_specs=in_specs + [_ANY] * n_xi, out_specs=tuple(out_specs) + (_ANY,) * n_xo,
        scratch_shapes=scratch + [pltpu.SemaphoreType.DMA((exchange.n_sems,)), pltpu.SemaphoreType.DMA((exchange.n_sems,))],
        input_output_aliases={n_in + a: n_out + b for a, b in exchange.aliases.items()},
        compiler_params=_params(*(["arbitrary"] * len(grid))), name=kw["name"])

    def run(*args):
        res = call(*args, *exchange.inputs)
        exchange.results = list(res[n_out:])
        return res[0] if single else tuple(res[:n_out])

    return run


_ANY = pl.BlockSpec(memory_space=pl.ANY)


def _tile(n, pref, mult):
    best = None
    t = mult
    while t <= min(n, pref):
        if n % t == 0:
            best = t
        t += mult
    return n if best is None else best


def _sds(shape, dtype):
    return jax.ShapeDtypeStruct(shape, dtype)


def _vec_spec(d):
    return pl.BlockSpec((1, d), lambda *_: (0, 0))


def _norm_mod_fwd(x, g, shift, scale, name):
    s, d = x.shape
    tr = _tile(s, 512, 16)

    def body(x_ref, g_ref, sh_ref, sc_ref, h_ref):
        xv = x_ref[...]
        rstd = lax.rsqrt(jnp.mean(xv * xv, axis=-1, keepdims=True) + EPS)
        n = xv * rstd * g_ref[...]
        h_ref[...] = (n * (1.0 + sc_ref[...]) + sh_ref[...]).astype(BF16)

    row = pl.BlockSpec((tr, d), lambda i: (i, 0))
    return _pc(body, out_shape=_sds((s, d), BF16), grid=(s // tr,),
               in_specs=[row, _vec_spec(d), _vec_spec(d), _vec_spec(d)], out_specs=row,
               compiler_params=_params("parallel"), name=name)(x, g, shift, scale)


def _norm_mod_bwd(dh, x, g, scale, dres, name, exchange=None):
    s, d = x.shape
    tr = _tile(s, 256, 8)

    def body(dh_ref, x_ref, g_ref, sc_ref, dres_ref, dx_ref, dsh_ref, dsc_ref, dg_ref):
        @pl.when(pl.program_id(0) == 0)
        def _():
            dsh_ref[...] = jnp.zeros_like(dsh_ref)
            dsc_ref[...] = jnp.zeros_like(dsc_ref)
            dg_ref[...] = jnp.zeros_like(dg_ref)

        xv = x_ref[...]
        dhv = dh_ref[...]
        gv = g_ref[...]
        rstd = lax.rsqrt(jnp.mean(xv * xv, axis=-1, keepdims=True) + EPS)
        xhat = xv * rstd
        dn = dhv * (1.0 + sc_ref[...])
        dsh_ref[...] += jnp.sum(dhv, axis=0, keepdims=True)
        dsc_ref[...] += jnp.sum(dhv * (xhat * gv), axis=0, keepdims=True)
        dg_ref[...] += jnp.sum(dn * xhat, axis=0, keepdims=True)
        dxh = dn * gv
        proj = jnp.mean(dxh * xhat, axis=-1, keepdims=True)
        dx_ref[...] = dres_ref[...] + rstd * (dxh - xhat * proj)

    row = pl.BlockSpec((tr, d), lambda i: (i, 0))
    vec = _vec_spec(d)
    return _pc(body, exchange, out_shape=(_sds((s, d), F32), _sds((1, d), F32), _sds((1, d), F32), _sds((1, d), F32)),
               grid=(s // tr,), in_specs=[row, row, vec, vec, row], out_specs=(row, vec, vec, vec),
               compiler_params=_params("arbitrary"), name=name)(dh, x, g, scale, dres)


def _gate_bwd(dx, f, gate, name):
    s, d = dx.shape
    tr = _tile(s, 512, 16)

    def body(dx_ref, f_ref, gate_ref, df_ref, dg_ref):
        @pl.when(pl.program_id(0) == 0)
        def _():
            dg_ref[...] = jnp.zeros_like(dg_ref)

        dxv = dx_ref[...]
        df_ref[...] = (dxv * gate_ref[...]).astype(BF16)
        dg_ref[...] += jnp.sum(dxv * f_ref[...].astype(F32), axis=0, keepdims=True)

    row = pl.BlockSpec((tr, d), lambda i: (i, 0))
    vec = _vec_spec(d)
    return _pc(body, out_shape=(_sds((s, d), BF16), _sds((1, d), F32)), grid=(s // tr,),
               in_specs=[row, row, vec], out_specs=(row, vec),
               compiler_params=_params("arbitrary"), name=name)(dx, f, gate)


def _final_loss(x, g, target, name):
    s, d = x.shape
    tr = _tile(s, 256, 8)
    nsteps = s // tr

    def body(x_ref, g_ref, t_ref, dx_ref, loss_ref, dg_ref):
        i = pl.program_id(0)

        @pl.when(i == 0)
        def _():
            loss_ref[...] = jnp.zeros_like(loss_ref)
            dg_ref[...] = jnp.zeros_like(dg_ref)

        xv = x_ref[...]
        gv = g_ref[...]
        rstd = lax.rsqrt(jnp.mean(xv * xv, axis=-1, keepdims=True) + EPS)
        xhat = xv * rstd
        err = xhat * gv - t_ref[...]
        dy = err * (1.0 / d)
        loss_ref[...] += jnp.sum(0.5 * err * dy, axis=0, keepdims=True)
        dg_ref[...] += jnp.sum(dy * xhat, axis=0, keepdims=True)
        dxh = dy * gv
        proj = jnp.mean(dxh * xhat, axis=-1, keepdims=True)
        dx_ref[...] = rstd * (dxh - xhat * proj)

        @pl.when(i == nsteps - 1)
        def _():
            loss_ref[...] = jnp.broadcast_to(jnp.sum(loss_ref[...], axis=-1, keepdims=True), loss_ref.shape)

    row = pl.BlockSpec((tr, d), lambda i: (i, 0))
    vec = _vec_spec(d)
    return _pc(body, out_shape=(_sds((s, d), F32), _sds((1, d), F32), _sds((1, d), F32)), grid=(nsteps,),
               in_specs=[row, vec, row], out_specs=(row, vec, vec),
               compiler_params=_params("arbitrary"), name=name)(x, g, target)


def _mm(lhs, rhs, dims, out_dtype, name, res=None, gate=None, aux_dtype=None, exchange=None):
    if dims == "nn":
        (m, k), (k2, n) = lhs.shape, rhs.shape
    elif dims == "nt":
        (m, k), (n, k2) = lhs.shape, rhs.shape
    else:
        (k, m), (k2, n) = lhs.shape, rhs.shape
    assert k == k2, (lhs.shape, rhs.shape, dims)
    tn = _tile(n, 1024, LANES)
    tm = _tile(m, 512, LANES if dims == "tn" else 16)
    tk = _tile(k, 1536, LANES)
    nk = k // tk
    dn = {"nn": _NN, "nt": _NT, "tn": _TN}[dims]
    lhs_spec = (pl.BlockSpec((tk, tm), lambda i, j, kk: (kk, i)) if dims == "tn"
                else pl.BlockSpec((tm, tk), lambda i, j, kk: (i, kk)))
    rhs_spec = (pl.BlockSpec((tn, tk), lambda i, j, kk: (j, kk)) if dims == "nt"
                else pl.BlockSpec((tk, tn), lambda i, j, kk: (kk, j)))
    out_spec = pl.BlockSpec((tm, tn), lambda i, j, kk: (i, j))
    has_res, has_gate, has_aux = res is not None, gate is not None, aux_dtype is not None

    def body(*refs):
        refs = list(refs)
        l_ref, r_ref = refs[0], refs[1]
        pos = 2
        res_ref = gate_ref = aux_ref = None
        if has_res:
            res_ref = refs[pos]; pos += 1
        if has_gate:
            gate_ref = refs[pos]; pos += 1
        out_ref = refs[pos]; pos += 1
        if has_aux:
            aux_ref = refs[pos]; pos += 1
        acc_ref = refs[pos]
        kk = pl.program_id(2)
        part = lax.dot_general(l_ref[...], r_ref[...], dn, preferred_element_type=F32)

        @pl.when(kk == 0)
        def _():
            acc_ref[...] = part

        @pl.when(kk > 0)
        def _():
            acc_ref[...] += part

        @pl.when(kk == nk - 1)
        def _():
            acc = acc_ref[...]
            if has_aux:
                aux_ref[...] = acc.astype(aux_dtype)
            if has_gate:
                acc = acc * gate_ref[...]
            if has_res:
                acc = res_ref[...] + acc
            out_ref[...] = acc.astype(out_dtype)

    in_specs = [lhs_spec, rhs_spec]
    args = [lhs, rhs]
    if has_res:
        in_specs.append(out_spec); args.append(res)
    if has_gate:
        in_specs.append(pl.BlockSpec((1, tn), lambda i, j, kk: (0, j))); args.append(gate)
    out_shape = [_sds((m, n), out_dtype)]
    out_specs = [out_spec]
    if has_aux:
        out_shape.append(_sds((m, n), aux_dtype)); out_specs.append(out_spec)
    outs = _pc(body, exchange, out_shape=tuple(out_shape), grid=(m // tm, n // tn, nk), in_specs=in_specs,
               out_specs=tuple(out_specs), scratch_shapes=[pltpu.VMEM((tm, tn), F32)],
               compiler_params=_params("parallel", "parallel", "arbitrary"), name=name)(*args)
    return outs if has_aux else outs[0]


def _ffn_up(h, wg_t, wu_t, name, exchange=None):
    s, d = h.shape
    f = wg_t.shape[0]
    tm = _tile(s, 1024, 16)
    tn = _tile(f, 256, LANES)

    def body(h_ref, wg_ref, wu_ref, a_ref, u_ref, hid_ref):
        hv = h_ref[...]
        a = lax.dot_general(hv, wg_ref[...], _NT, preferred_element_type=F32)
        u = lax.dot_general(hv, wu_ref[...], _NT, preferred_element_type=F32)
        a_ref[...] = a.astype(BF16)
        u_ref[...] = u.astype(BF16)
        hid_ref[...] = (a * jax.nn.sigmoid(a) * u).astype(BF16)

    hs = pl.BlockSpec((tm, d), lambda i, j: (i, 0))
    ws = pl.BlockSpec((tn, d), lambda i, j: (j, 0))
    os_ = pl.BlockSpec((tm, tn), lambda i, j: (i, j))
    return _pc(body, exchange, out_shape=(_sds((s, f), BF16),) * 3, grid=(s // tm, f // tn),
               in_specs=[hs, ws, ws], out_specs=(os_, os_, os_),
               compiler_params=_params("parallel", "parallel"), name=name)(h, wg_t, wu_t)


def _ffn_dact(df, wd, a, u, name, exchange=None):
    s, d = df.shape
    f = wd.shape[0]
    tm = _tile(s, 1024, 16)
    tn = _tile(f, 256, LANES)

    def body(df_ref, wd_ref, a_ref, u_ref, da_ref, du_ref):
        dhid = lax.dot_general(df_ref[...], wd_ref[...], _NT, preferred_element_type=F32)
        av = a_ref[...].astype(F32)
        uv = u_ref[...].astype(F32)
        sig = jax.nn.sigmoid(av)
        da_ref[...] = (dhid * uv * (sig * (1.0 + av * (1.0 - sig)))).astype(BF16)
        du_ref[...] = (dhid * (av * sig)).astype(BF16)

    ds_ = pl.BlockSpec((tm, d), lambda i, j: (i, 0))
    ws = pl.BlockSpec((tn, d), lambda i, j: (j, 0))
    os_ = pl.BlockSpec((tm, tn), lambda i, j: (i, j))
    return _pc(body, exchange, out_shape=(_sds((s, f), BF16),) * 2, grid=(s // tm, f // tn),
               in_specs=[ds_, ws, os_, os_], out_specs=(os_, os_),
               compiler_params=_params("parallel", "parallel"), name=name)(df, wd, a, u)


def _split3(v):
    hi = v.astype(BF16)
    r1 = v - hi.astype(F32)
    mid = r1.astype(BF16)
    lo = (r1 - mid.astype(F32)).astype(BF16)
    return hi, mid, lo


def _dot3(v, mat):
    hi, mid, lo = _split3(v)
    out = lax.dot_general(hi, mat, _NN, preferred_element_type=F32)
    out += lax.dot_general(mid, mat, _NN, preferred_element_type=F32)
    out += lax.dot_general(lo, mat, _NN, preferred_element_type=F32)
    return out


def _forget_fwd(flog_t, bias, name):
    h, s = flog_t.shape
    blk = _tile(s, 512, LANES)
    tri = (jnp.arange(blk)[:, None] <= jnp.arange(blk)[None, :]).astype(BF16)

    def body(z_ref, b_ref, tri_ref, f_ref, carry):
        @pl.when(pl.program_id(0) == 0)
        def _():
            carry[...] = jnp.zeros_like(carry)

        z = z_ref[...] + b_ref[...]
        e = jnp.exp(-jnp.abs(z))
        w = 1.0 + e
        log1p_e = jnp.where(w == 1.0, e, jnp.log(w) * (e / (w - 1.0)))
        lf = jnp.minimum(z, 0.0) - log1p_e
        out = carry[...] + _dot3(lf, tri_ref[...])
        f_ref[...] = out
        carry[...] = out[:, blk - 1:blk]

    zs = pl.BlockSpec((h, blk), lambda i: (0, i))
    return _pc(body, out_shape=_sds((h, s), F32), grid=(s // blk,),
               in_specs=[zs, pl.BlockSpec((h, 1), lambda i: (0, 0)), pl.BlockSpec((blk, blk), lambda i: (0, 0))],
               out_specs=zs, scratch_shapes=[pltpu.VMEM((h, 1), F32)],
               compiler_params=_params("arbitrary"), name=name)(flog_t, bias, tri)


def _forget_bwd(df_t, flog_t, bias, name):
    h, s = flog_t.shape
    blk = _tile(s, 512, LANES)
    nb = s // blk
    tri = (jnp.arange(blk)[:, None] >= jnp.arange(blk)[None, :]).astype(BF16)

    def body(df_ref, z_ref, b_ref, tri_ref, dz_ref, db_ref, carry):
        @pl.when(pl.program_id(0) == 0)
        def _():
            carry[...] = jnp.zeros_like(carry)
            db_ref[...] = jnp.zeros_like(db_ref)

        rc = carry[...] + _dot3(df_ref[...], tri_ref[...])
        carry[...] = rc[:, 0:1]
        dz = rc * jax.nn.sigmoid(-(z_ref[...] + b_ref[...]))
        dz_ref[...] = dz
        db_ref[...] += jnp.sum(dz, axis=-1, keepdims=True)

    rev = pl.BlockSpec((h, blk), lambda i: (0, nb - 1 - i))
    col = pl.BlockSpec((h, 1), lambda i: (0, 0))
    return _pc(body, out_shape=(_sds((h, s), F32), _sds((h, 1), F32)), grid=(nb,),
               in_specs=[rev, rev, col, pl.BlockSpec((blk, blk), lambda i: (0, 0))],
               out_specs=(rev, col), scratch_shapes=[pltpu.VMEM((h, 1), F32)],
               compiler_params=_params("arbitrary"), name=name)(df_t, flog_t, bias, tri)


def _attn_tiles(s):
    return _tile(s, 512, LANES)


def _attn_fwd(qkv, fcol, frow, name, exchange=None):
    s = qkv.shape[0]
    a_w = qkv.shape[1] // 3
    npair = a_w // LANES
    t = _attn_tiles(s)
    nq = s // t
    scale = 1.0 / math.sqrt(HEAD_DIM)

    def body(q_ref, k_ref, v_ref, fc_ref, fr_ref, o_ref, lse_ref, m_sc, l_sc, acc_sc):
        qi = pl.program_id(1)
        ki = pl.program_id(2)
        first = lax.broadcasted_iota(jnp.int32, (1, LANES), 1) < HEAD_DIM

        @pl.when(ki == 0)
        def _():
            m_sc[...] = jnp.full_like(m_sc, NEG_BIG)
            l_sc[...] = jnp.zeros_like(l_sc)
            acc_sc[...] = jnp.zeros_like(acc_sc)

        def step(diag):
            q2, k2, v2 = q_ref[...], k_ref[...], v_ref[...]
            fc = fc_ref[0]
            fr = fr_ref[0]
            m_old = m_sc[...]
            keep = None
            if diag:
                keep = (lax.broadcasted_iota(jnp.int32, (t, t), 0) >= lax.broadcasted_iota(jnp.int32, (t, t), 1))
            m_new, rs, pv = [], [], []
            for hh in range(2):
                sel = first if hh == 0 else jnp.logical_not(first)
                qm = jnp.where(sel, q2, jnp.zeros_like(q2))
                vm = jnp.where(sel, v2, jnp.zeros_like(v2))
                sc = lax.dot_general(qm, k2, _NT, preferred_element_type=F32) * scale
                sc = sc + (fc[:, hh:hh + 1] - fr[hh:hh + 1, :])
                if diag:
                    sc = jnp.where(keep, sc, NEG_BIG)
                mo = m_old[:, hh * HEAD_DIM:hh * HEAD_DIM + 1]
                mn = jnp.maximum(mo, jnp.max(sc, axis=1, keepdims=True))
                p = jnp.exp(sc - mn)
                m_new.append(mn)
                rs.append(jnp.sum(p, axis=1, keepdims=True))
                pv.append(lax.dot_general(p.astype(BF16), vm, _NN, preferred_element_type=F32))
            m2 = jnp.where(first, m_new[0], m_new[1])
            alpha = jnp.exp(m_old - m2)
            m_sc[...] = m2
            l_sc[...] = alpha * l_sc[...] + jnp.where(first, rs[0], rs[1])
            acc_sc[...] = alpha * acc_sc[...] + pv[0] + pv[1]

        @pl.when(ki < qi)
        def _():
            step(False)

        @pl.when(ki == qi)
        def _():
            step(True)
            l2 = l_sc[...]
            o_ref[...] = acc_sc[...] / l2
            lse_ref[...] = m_sc[...] + jnp.log(l2)

    qs = pl.BlockSpec((t, LANES), lambda p, qi, ki: (qi, p))
    ks = pl.BlockSpec((t, LANES), lambda p, qi, ki: (jnp.minimum(ki, qi), npair + p))
    vs = pl.BlockSpec((t, LANES), lambda p, qi, ki: (jnp.minimum(ki, qi), 2 * npair + p))
    fcs = pl.BlockSpec((1, t, 2), lambda p, qi, ki: (p, qi, 0))
    frs = pl.BlockSpec((1, 2, t), lambda p, qi, ki: (p, 0, jnp.minimum(ki, qi)))
    return _pc(body, exchange, out_shape=(_sds((s, a_w), F32), _sds((s, a_w), F32)), grid=(npair, nq, nq),
               in_specs=[qs, ks, vs, fcs, frs], out_specs=(qs, qs),
               scratch_shapes=[pltpu.VMEM((t, LANES), F32)] * 3,
               compiler_params=_params("parallel", "arbitrary", "arbitrary"), name=name)(qkv, qkv, qkv, fcol, frow)


def _attn_bwd(qkv, do, o, lse, fcol, frow, name, exchange=None):
    s = qkv.shape[0]
    a_w = qkv.shape[1] // 3
    npair = a_w // LANES
    t = _attn_tiles(s)
    nq = s // t
    scale = 1.0 / math.sqrt(HEAD_DIM)

    def body(q_ref, k_ref, v_ref, do_ref, o_ref, lse_ref, fc_ref, fr_ref,
             dq_ref, dk_ref, dv_ref, df_ref, dfq_ref, dk_sc, dv_sc):
        ki = pl.program_id(1)
        qi = pl.program_id(2)
        first = lax.broadcasted_iota(jnp.int32, (1, LANES), 1) < HEAD_DIM

        @pl.when(jnp.logical_and(ki == 0, qi == 0))
        def _():
            dq_ref[...] = jnp.zeros_like(dq_ref)
            dfq_ref[...] = jnp.zeros_like(dfq_ref)

        def step(diag):
            q2, k2, v2, do2 = q_ref[...], k_ref[...], v_ref[...], do_ref[...]
            fc = fc_ref[0]
            fr = fr_ref[0]
            lse2 = lse_ref[...]
            dd = do2.astype(F32) * o_ref[...]
            keep = None
            if diag:
                keep = (lax.broadcasted_iota(jnp.int32, (t, t), 0) >= lax.broadcasted_iota(jnp.int32, (t, t), 1))
            dq_part = jnp.zeros((t, LANES), F32)
            dk_part = jnp.zeros((t, LANES), F32)
            dv_part = jnp.zeros((t, LANES), F32)
            dfs, rsum = [], []
            for hh in range(2):
                sel = first if hh == 0 else jnp.logical_not(first)
                qm = jnp.where(sel, q2, jnp.zeros_like(q2))
                km = jnp.where(sel, k2, jnp.zeros_like(k2))
                dom = jnp.where(sel, do2, jnp.zeros_like(do2))
                delta = jnp.sum(jnp.where(sel, dd, 0.0), axis=1, keepdims=True)
                sc = lax.dot_general(qm, k2, _NT, preferred_element_type=F32) * scale
                sc = sc + (fc[:, hh:hh + 1] - fr[hh:hh + 1, :])
                if diag:
                    sc = jnp.where(keep, sc, NEG_BIG)
                p = jnp.exp(sc - lse2[:, hh * HEAD_DIM:hh * HEAD_DIM + 1])
                dp = lax.dot_general(dom, v2, _NT, preferred_element_type=F32)
                dsv = p * (dp - delta)
                dfs.append(-jnp.sum(dsv, axis=0, keepdims=True))
                rsum.append(jnp.sum(dsv, axis=1, keepdims=True))
                ds_b = dsv.astype(BF16)
                dv_part += lax.dot_general(p.astype(BF16), dom, _TN, preferred_element_type=F32)
                dk_part += lax.dot_general(ds_b, qm, _TN, preferred_element_type=F32)
                dq_part += lax.dot_general(ds_b, km, _NN, preferred_element_type=F32)
            rows = pl.ds(pl.multiple_of(qi * t, t), t)
            dq_ref[rows, :] += dq_part * scale
            dfq_ref[rows, :] += jnp.where(first, rsum[0], rsum[1])
            dfv = jnp.concatenate(dfs, axis=0)
            if diag:
                dk_sc[...] = dk_part * scale
                dv_sc[...] = dv_part
                df_ref[0] = dfv
            else:
                dk_sc[...] += dk_part * scale
                dv_sc[...] += dv_part
                df_ref[0] += dfv

        @pl.when(qi == ki)
        def _():
            step(True)

        @pl.when(qi > ki)
        def _():
            step(False)

        @pl.when(qi == nq - 1)
        def _():
            dk_ref[...] = dk_sc[...].astype(BF16)
            dv_ref[...] = dv_sc[...].astype(BF16)

    qs = pl.BlockSpec((t, LANES), lambda p, ki, qi: (jnp.maximum(qi, ki), p))
    ks = pl.BlockSpec((t, LANES), lambda p, ki, qi: (ki, npair + p))
    vs = pl.BlockSpec((t, LANES), lambda p, ki, qi: (ki, 2 * npair + p))
    kout = pl.BlockSpec((t, LANES), lambda p, ki, qi: (ki, p))
    fcs = pl.BlockSpec((1, t, 2), lambda p, ki, qi: (p, jnp.maximum(qi, ki), 0))
    frs = pl.BlockSpec((1, 2, t), lambda p, ki, qi: (p, 0, ki))
    dqs = pl.BlockSpec((s, LANES), lambda p, ki, qi: (0, p))
    return _pc(body, exchange,
               out_shape=(_sds((s, a_w), F32), _sds((s, a_w), BF16), _sds((s, a_w), BF16), _sds((npair, 2, s), F32),
                          _sds((s, a_w), F32)),
               grid=(npair, nq, nq), in_specs=[qs, ks, vs, qs, qs, qs, fcs, frs],
               out_specs=(dqs, kout, kout, frs, dqs),
               scratch_shapes=[pltpu.VMEM((t, LANES), F32)] * 2,
               compiler_params=_params("parallel", "arbitrary", "arbitrary"), name=name)(
                   qkv, qkv, qkv, do, o, lse, fcol, frow)


def _shift_down(z, k, rows):
    return jnp.where(rows >= k, pltpu.roll(z, k, 0), 0.0)


def _shift_up(z, k, rows, n):
    return jnp.where(rows < n - k, pltpu.roll(z, n - k, 0), 0.0)


def _conv_fwd(bcx, conv_w, name):
    s = bcx.shape[0]
    cw = bcx.shape[1] // 3
    nb = cw // LANES

    def body(b_ref, c_ref, x_ref, w_ref, cv_ref):
        rows = lax.broadcasted_iota(jnp.int32, (s, LANES), 0)
        z = c_ref[...] * x_ref[...]
        w = w_ref[...]
        y = w[2:3, :] * z + w[1:2, :] * _shift_down(z, 1, rows) + w[0:1, :] * _shift_down(z, 2, rows)
        cv_ref[...] = b_ref[...] * y

    def col(off):
        return pl.BlockSpec((s, LANES), lambda j: (0, j + off))

    return _pc(body, out_shape=_sds((s, cw), F32), grid=(nb,),
               in_specs=[col(0), col(nb), col(2 * nb), pl.BlockSpec((CONV_K, LANES), lambda j: (0, j))],
               out_specs=col(0), compiler_params=_params("parallel"), name=name)(bcx, bcx, bcx, conv_w)


def _conv_bwd(dcv, bcx, conv_w, name):
    s = bcx.shape[0]
    cw = bcx.shape[1] // 3
    nb = cw // LANES

    def body(dcv_ref, b_ref, c_ref, x_ref, w_ref, db_ref, dc_ref, dxc_ref, dw_ref):
        rows = lax.broadcasted_iota(jnp.int32, (s, LANES), 0)
        cv_, xv = c_ref[...], x_ref[...]
        z = cv_ * xv
        w = w_ref[...]
        z1 = _shift_down(z, 1, rows)
        z2 = _shift_down(z, 2, rows)
        y = w[2:3, :] * z + w[1:2, :] * z1 + w[0:1, :] * z2
        dcvv = dcv_ref[...]
        db_ref[...] = (dcvv * y).astype(BF16)
        dy = dcvv * b_ref[...]
        dw_ref[0:1, :] = jnp.sum(dy * z2, axis=0, keepdims=True)
        dw_ref[1:2, :] = jnp.sum(dy * z1, axis=0, keepdims=True)
        dw_ref[2:3, :] = jnp.sum(dy * z, axis=0, keepdims=True)
        dz = w[2:3, :] * dy + w[1:2, :] * _shift_up(dy, 1, rows, s) + w[0:1, :] * _shift_up(dy, 2, rows, s)
        dc_ref[...] = (dz * xv).astype(BF16)
        dxc_ref[...] = (dz * cv_).astype(BF16)

    def col(off):
        return pl.BlockSpec((s, LANES), lambda j: (0, j + off))

    wspec = pl.BlockSpec((CONV_K, LANES), lambda j: (0, j))
    db, dc, dxc, dw = _pc(body, out_shape=(_sds((s, cw), BF16),) * 3 + (_sds((CONV_K, cw), F32),), grid=(nb,),
                          in_specs=[col(0), col(0), col(nb), col(2 * nb), wspec],
                          out_specs=(col(0), col(0), col(0), wspec),
                          compiler_params=_params("parallel"), name=name)(dcv, bcx, bcx, bcx, conv_w)
    return db, dc, dxc, dw


def _group_matrix():
    idx = jnp.arange(LANES) // HEAD_DIM
    return (idx[:, None] == idx[None, :]).astype(BF16)


def _group_sum(v, gmat):
    return _dot3(v, gmat)


def _gnorm_fwd(att, cv, gg, name):
    s, a_w = att.shape
    cw = cv.shape[1]
    d = a_w + cw
    tr = _tile(s, 512, 16)
    gmat = _group_matrix()

    def body(att_ref, cv_ref, gg_ref, gm_ref, yn_ref):
        gm = gm_ref[...]
        for c0 in range(0, d, LANES):
            y = att_ref[:, c0:c0 + LANES] if c0 < a_w else cv_ref[:, c0 - a_w:c0 - a_w + LANES]
            ms = _group_sum(y * y, gm) * (1.0 / HEAD_DIM)
            yn_ref[:, c0:c0 + LANES] = (y * lax.rsqrt(ms + EPS) * gg_ref[:, c0:c0 + LANES]).astype(BF16)

    return _pc(body, out_shape=_sds((s, d), BF16), grid=(s // tr,),
               in_specs=[pl.BlockSpec((tr, a_w), lambda i: (i, 0)), pl.BlockSpec((tr, cw), lambda i: (i, 0)),
                         _vec_spec(d), pl.BlockSpec((LANES, LANES), lambda i: (0, 0))],
               out_specs=pl.BlockSpec((tr, d), lambda i: (i, 0)),
               compiler_params=_params("parallel"), name=name)(att, cv, gg, gmat)


def _gnorm_bwd(dyn, att, cv, gg, name):
    s, a_w = att.shape
    cw = cv.shape[1]
    d = a_w + cw
    tr = _tile(s, 256, 16)
    gmat = _group_matrix()

    def body(dyn_ref, att_ref, cv_ref, gg_ref, gm_ref, datt_ref, dcv_ref, dgg_ref):
        @pl.when(pl.program_id(0) == 0)
        def _():
            dgg_ref[...] = jnp.zeros_like(dgg_ref)

        gm = gm_ref[...]
        for c0 in range(0, d, LANES):
            y = att_ref[:, c0:c0 + LANES] if c0 < a_w else cv_ref[:, c0 - a_w:c0 - a_w + LANES]
            dv = dyn_ref[:, c0:c0 + LANES]
            r = lax.rsqrt(_group_sum(y * y, gm) * (1.0 / HEAD_DIM) + EPS)
            xhat = y * r
            dgg_ref[:, c0:c0 + LANES] += jnp.sum(dv * xhat, axis=0, keepdims=True)
            dxh = dv * gg_ref[:, c0:c0 + LANES]
            proj = _group_sum(dxh * xhat, gm) * (1.0 / HEAD_DIM)
            dy = r * (dxh - xhat * proj)
            if c0 < a_w:
                datt_ref[:, c0:c0 + LANES] = dy.astype(BF16)
            else:
                dcv_ref[:, c0 - a_w:c0 - a_w + LANES] = dy

    return _pc(body, out_shape=(_sds((s, a_w), BF16), _sds((s, cw), F32), _sds((1, d), F32)), grid=(s // tr,),
               in_specs=[pl.BlockSpec((tr, d), lambda i: (i, 0)), pl.BlockSpec((tr, a_w), lambda i: (i, 0)),
                         pl.BlockSpec((tr, cw), lambda i: (i, 0)), _vec_spec(d),
                         pl.BlockSpec((LANES, LANES), lambda i: (0, 0))],
               out_specs=(pl.BlockSpec((tr, a_w), lambda i: (i, 0)), pl.BlockSpec((tr, cw), lambda i: (i, 0)),
                          _vec_spec(d)),
               compiler_params=_params("arbitrary"), name=name)(dyn, att, cv, gg, gmat)


def _adamw_math(w, g, m, v):
    m_new = ADAM_B1 * m + (1.0 - ADAM_B1) * g
    v_new = ADAM_B2 * v + (1.0 - ADAM_B2) * (g * g)
    m_hat = m_new / (1.0 - ADAM_B1 ** ADAM_STEP)
    v_hat = v_new / (1.0 - ADAM_B2 ** ADAM_STEP)
    delta = -ADAM_LR * (m_hat / (jnp.sqrt(v_hat) + ADAM_EPS) + ADAM_WD * w)
    return delta, m_new, v_new


def _row_tile(r, c):
    return _tile(r, max(8, ((1 << 18) // c) // 8 * 8), 8)


def _adamw(w, g, m, v, name):
    r, c = w.shape
    tr = _row_tile(r, c)

    def body(w_ref, g_ref, m_ref, v_ref, d_ref, mo_ref, vo_ref):
        d, mn, vn = _adamw_math(w_ref[...], g_ref[...], m_ref[...], v_ref[...])
        d_ref[...] = d
        mo_ref[...] = mn
        vo_ref[...] = vn

    spec = pl.BlockSpec((tr, c), lambda i: (i, 0))
    return _pc(body, out_shape=(_sds((r, c), F32),) * 3, grid=(r // tr,), in_specs=[spec] * 4,
               out_specs=(spec,) * 3, compiler_params=_params("parallel"), name=name)(w, g, m, v)


def _adamw_halves(w, mine, theirs, m, v, core, name):
    r2, c = w.shape
    r = r2 // 2
    assert mine.shape == (r, c) and theirs.shape == (r, c)
    tr = _row_tile(r, c)
    nb = r // tr

    def body(core_ref, w_ref, a_ref, b_ref, m_ref, v_ref, g_ref, d_ref, mo_ref, vo_ref):
        g = jnp.where(pl.program_id(0) == core_ref[0], a_ref[...], b_ref[...])
        d, mn, vn = _adamw_math(w_ref[...], g, m_ref[...], v_ref[...])
        g_ref[...] = g
        d_ref[...] = d
        mo_ref[...] = mn
        vo_ref[...] = vn

    full = pl.BlockSpec((tr, c), lambda h, i, core_ref: (h * nb + i, 0))
    half = pl.BlockSpec((tr, c), lambda h, i, core_ref: (i, 0))
    grid_spec = pltpu.PrefetchScalarGridSpec(
        num_scalar_prefetch=1, grid=(2, nb), in_specs=[full, half, half, full, full], out_specs=(full,) * 4)
    return _pc(body, out_shape=(_sds((r2, c), F32),) * 4, grid_spec=grid_spec,
               compiler_params=_params("parallel", "parallel"), name=name)(core, w, mine, theirs, m, v)


def _ada_fwd(c16, ada_w, ada_b, name):
    d, n = ada_w.shape
    tn = _tile(n, 768, LANES)

    def body(c_ref, w_ref, b_ref, o_ref):
        cv = c_ref[...]
        sc = (cv * jax.nn.sigmoid(cv)).astype(BF16)
        o_ref[...] = lax.dot_general(sc, w_ref[...].astype(BF16), _NN, preferred_element_type=F32) + b_ref[...]

    return _pc(body, out_shape=_sds((16, n), F32), grid=(n // tn,),
               in_specs=[pl.BlockSpec((16, d), lambda j: (0, 0)), pl.BlockSpec((d, tn), lambda j: (0, j)),
                         pl.BlockSpec((1, tn), lambda j: (0, j))],
               out_specs=pl.BlockSpec((16, tn), lambda j: (0, j)),
               compiler_params=_params("parallel"), name=name)(c16, ada_w, ada_b)


def _ada_update(c16_t, dmod16, w, m, v, name):
    r, c = w.shape
    tr = _row_tile(r, c)

    def body(c_ref, dm_ref, w_ref, m_ref, v_ref, g_ref, d_ref, mo_ref, vo_ref):
        cv = c_ref[...]
        sc = (cv * jax.nn.sigmoid(cv)).astype(BF16)
        g = lax.dot_general(sc, dm_ref[...].astype(BF16), _NN, preferred_element_type=F32)
        d, mn, vn = _adamw_math(w_ref[...], g, m_ref[...], v_ref[...])
        g_ref[...] = g
        d_ref[...] = d
        mo_ref[...] = mn
        vo_ref[...] = vn

    spec = pl.BlockSpec((tr, c), lambda i: (i, 0))
    return _pc(body, out_shape=(_sds((r, c), F32),) * 4, grid=(r // tr,),
               in_specs=[pl.BlockSpec((tr, 16), lambda i: (i, 0)), pl.BlockSpec((16, c), lambda i: (0, 0)),
                         spec, spec, spec],
               out_specs=(spec,) * 4, compiler_params=_params("parallel"), name=name)(c16_t, dmod16, w, m, v)


def _add_half(dw, recv, core, name):
    _, _, r, w = dw.shape
    tr = _tile(r, 256, 16)

    def body(core_ref, a_ref, b_ref, o_ref):
        o_ref[...] = (a_ref[...].astype(F32) + b_ref[...].astype(F32)).astype(BF16)

    grid_spec = pltpu.PrefetchScalarGridSpec(
        num_scalar_prefetch=1, grid=(N_CHIPS, r // tr),
        in_specs=[pl.BlockSpec((None, None, tr, w), lambda s, i, core_ref: (s, core_ref[0], i, 0)),
                  pl.BlockSpec((None, tr, w), lambda s, i, core_ref: (s, i, 0))],
        out_specs=pl.BlockSpec((None, tr, w), lambda s, i, core_ref: (s, i, 0)))
    return _pc(body, out_shape=_sds((N_CHIPS, r, w), BF16), grid_spec=grid_spec,
               compiler_params=_params("parallel", "parallel"), name=name)(core, dw, recv)


def _sum_chips(own, recv, chip, name):
    _, r, w = own.shape
    tr = _tile(r, 256, 16)

    def body(chip_ref, own_ref, p_ref, o_ref):
        acc = own_ref[...].astype(F32)
        for q in range(N_CHIPS - 1):
            acc = acc + p_ref[q].astype(F32)
        o_ref[...] = acc

    grid_spec = pltpu.PrefetchScalarGridSpec(
        num_scalar_prefetch=1, grid=(r // tr,),
        in_specs=[pl.BlockSpec((None, tr, w), lambda i, chip_ref: (chip_ref[0], i, 0)),
                  pl.BlockSpec((N_CHIPS - 1, tr, w), lambda i, chip_ref: (0, i, 0))],
        out_specs=pl.BlockSpec((tr, w), lambda i, chip_ref: (i, 0)))
    return _pc(body, out_shape=_sds((r, w), F32), grid_spec=grid_spec,
               compiler_params=_params("parallel"), name=name)(chip, own, recv)


def _sum_devices(parts, name):
    nd, r, w = parts.shape

    def body(p_ref, o_ref):
        acc = p_ref[0]
        for q in range(1, nd):
            acc = acc + p_ref[q]
        o_ref[...] = acc

    return _pc(body, out_shape=_sds((r, w), F32), name=name)(parts)


def _place():
    x, y, c = lax.axis_index("x"), lax.axis_index("y"), lax.axis_index("c")
    chips = [(1 - x, y), (x, 1 - y), (1 - x, 1 - y)]
    return x, y, c, chips


def _all_gather_small(blk, name):
    r, w = blk.shape

    def body(x_ref, out_ref, send_sems, recv_sems, local_sem):
        x, y, c, chips = _place()
        me, sibling = (x, y, c), (x, y, 1 - c)

        def rows(px, py, pc):
            return out_ref.at[pl.ds((4 * px + 2 * py + pc) * r, r), :]

        def copy(k, block, to, src=None):
            return pltpu.make_async_remote_copy(
                src_ref=rows(*block) if src is None else src, dst_ref=rows(*block),
                send_sem=send_sems.at[k], recv_sem=recv_sems.at[k], device_id=to, device_id_type=MESH)

        mine = pltpu.make_async_copy(x_ref, rows(*me), local_sem)
        mine.start()
        first = [copy(0, me, sibling, src=x_ref)]
        first += [copy(1 + j, me, (*chip, c), src=x_ref) for j, chip in enumerate(chips)]
        for cp in first:
            cp.start()
        passed = [copy(4 + j, (*chip, c), sibling) for j, chip in enumerate(chips)]
        for j, chip in enumerate(chips):
            copy(1 + j, (*chip, c), me).wait_recv()
            passed[j].start()
        copy(0, sibling, me).wait_recv()
        for j, chip in enumerate(chips):
            copy(4 + j, (*chip, 1 - c), me).wait_recv()
        for cp in first + passed:
            cp.wait_send()
        mine.wait()

    return _pc(body, out_shape=_sds((N_DEV * r, w), blk.dtype),
               in_specs=[pl.BlockSpec(memory_space=pltpu.VMEM)], out_specs=pl.BlockSpec(memory_space=pltpu.VMEM),
               scratch_shapes=[pltpu.SemaphoreType.DMA((7,)), pltpu.SemaphoreType.DMA((7,)), pltpu.SemaphoreType.DMA],
               name=name)(blk)


def _remote(src, dst, send_sems, recv_sems, k, to):
    return pltpu.make_async_remote_copy(src_ref=src, dst_ref=dst, send_sem=send_sems.at[k], recv_sem=recv_sems.at[k],
                                        device_id=to, device_id_type=MESH)


def _exchange_of(inputs, out_shapes, n_sems, copies, aliases=None):
    def start(src, dst, send_sems, recv_sems):
        for cp in copies(src, dst, send_sems, recv_sems)[0]:
            cp.start()

    def finish(src, dst, send_sems, recv_sems):
        sends, arrivals = copies(src, dst, send_sems, recv_sems)
        for cp in arrivals:
            cp.wait_recv()
        for cp in sends:
            cp.wait_send()

    return _Exchange(inputs, out_shapes, n_sems, start, finish, aliases)


def _run_exchange(ex, name):
    n_in, n_out = len(ex.inputs), len(ex.out_shapes)

    def body(*refs):
        src, dst = refs[:n_in], refs[n_in:n_in + n_out]
        send_sems, recv_sems = refs[n_in + n_out:]
        ex.start(src, dst, send_sems, recv_sems)
        ex.finish(src, dst, send_sems, recv_sems)

    ex.results = list(pl.pallas_call(
        body, out_shape=tuple(ex.out_shapes), in_specs=[_ANY] * n_in, out_specs=(_ANY,) * n_out,
        scratch_shapes=[pltpu.SemaphoreType.DMA((ex.n_sems,)), pltpu.SemaphoreType.DMA((ex.n_sems,))],
        input_output_aliases=ex.aliases, name=name)(*ex.inputs))
    return ex.results


def _gather_ici_exchange(shards):
    n = len(shards)

    def copies(own, out, send_sems, recv_sems):
        x, y, c, chips = _place()
        my_chip = 2 * x + y
        sends, arrivals = [], []
        for i in range(n):
            for j, chip in enumerate(chips):
                to = (*chip, c)
                sends.append(_remote(own[i].at[c], out[i].at[my_chip, c], send_sems, recv_sems, 4 * i + j, to))
                arrivals.append(_remote(own[i].at[c], out[i].at[2 * chip[0] + chip[1], c], send_sems, recv_sems, 4 * i + j, to))
            whole = _remote(own[i], out[i].at[my_chip], send_sems, recv_sems, 4 * i + 3, (x, y, 1 - c))
            sends.append(whole)
            arrivals.append(whole)
        return sends, arrivals

    return _exchange_of(shards, [_sds((N_CHIPS,) + s.shape, s.dtype) for s in shards], 4 * n, copies)


def _gather_pass_exchange(gathered):
    n = len(gathered)

    def copies(src, dst, send_sems, recv_sems):
        x, y, c, chips = _place()
        sends, arrivals = [], []
        for i in range(n):
            for j, chip in enumerate(chips):
                idx = 2 * chip[0] + chip[1]
                sends.append(_remote(src[i].at[idx, c], dst[i].at[idx, c], send_sems, recv_sems, 3 * i + j, (x, y, 1 - c)))
                arrivals.append(_remote(src[i].at[idx, c], dst[i].at[idx, 1 - c], send_sems, recv_sems, 3 * i + j, (x, y, 1 - c)))
        return sends, arrivals

    return _exchange_of(gathered, [_sds(g.shape, g.dtype) for g in gathered], 3 * n, copies,
                        aliases={i: i for i in range(n)})


def _reduce_sibling_exchange(grads):
    n = len(grads)

    def copies(src, dst, send_sems, recv_sems):
        x, y, c, _ = _place()
        both = [_remote(src[i].at[s, 1 - c], dst[i].at[s], send_sems, recv_sems, N_CHIPS * i + s, (x, y, 1 - c))
                for i in range(n) for s in range(N_CHIPS)]
        return both, both

    return _exchange_of(grads, [_sds((N_CHIPS,) + g.shape[2:], g.dtype) for g in grads], N_CHIPS * n, copies)


def _reduce_chips_exchange(parts):
    n = len(parts)

    def copies(src, dst, send_sems, recv_sems):
        x, y, c, chips = _place()
        both = [_remote(src[i].at[2 * chip[0] + chip[1]], dst[i].at[j], send_sems, recv_sems, 3 * i + j, (*chip, c))
                for i in range(n) for j, chip in enumerate(chips)]
        return both, both

    return _exchange_of(parts, [_sds((N_CHIPS - 1,) + p.shape[1:], p.dtype) for p in parts], 3 * n, copies)


def _share_exchange(halves):
    n = len(halves)

    def copies(src, dst, send_sems, recv_sems):
        x, y, c, _ = _place()
        both = [_remote(src[i], dst[i], send_sems, recv_sems, i, (x, y, 1 - c)) for i in range(n)]
        return both, both

    return _exchange_of(halves, [_sds(h.shape, h.dtype) for h in halves], n, copies)


HEAD_ROWS = 16


class _WeightTraffic:
    def __init__(self, shards, core, chip):
        self.shards, self.core, self.chip = shards, core, chip
        self.gather, self.grads, self.reduce, self.chip_sums, self.half_sums, self.shared = {}, {}, {}, {}, {}, {}

    def gather_ici(self, grp):
        self.gather[grp] = _gather_ici_exchange(self.shards[grp])
        return self.gather[grp]

    def gather_pass(self, grp):
        self.gather[grp] = _gather_pass_exchange(self.gather[grp].results)
        return self.gather[grp]

    def weights(self, grp):
        return [g.reshape(-1, g.shape[-1]) for g in self.gather[grp].results]

    def reduce_sibling(self, grp, grads):
        self.grads[grp] = [g.reshape(N_CHIPS, 2, g.shape[0] // (2 * N_CHIPS), g.shape[1]) for g in grads]
        self.reduce[grp] = _reduce_sibling_exchange(self.grads[grp])
        return self.reduce[grp]

    def add_halves(self, grp):
        self.chip_sums[grp] = [_add_half(g, r, self.core, "add_half_%s%d" % (grp, i))
                               for i, (g, r) in enumerate(zip(self.grads[grp], self.reduce[grp].results))]

    def reduce_chips(self, grp):
        self.reduce[grp] = _reduce_chips_exchange(self.chip_sums[grp])
        return self.reduce[grp]

    def sum_chips(self, grp):
        self.half_sums[grp] = [_sum_chips(o, p, self.chip, "sum_chips_%s%d" % (grp, i))
                               for i, (o, p) in enumerate(zip(self.chip_sums[grp], self.reduce[grp].results))]

    def share(self, grp):
        self.shared[grp] = _share_exchange(self.half_sums[grp])
        return self.shared[grp]

    def totals(self, grp):
        return list(zip(self.half_sums[grp], self.shared[grp].results))


def _ffn_fwd(x, norm_g, shift, scale, gate, wg_t, wu_t, wd, tag, up_exchange=None, down_exchange=None):
    h = _norm_mod_fwd(x, norm_g, shift, scale, tag + "_norm_fwd")
    a, u, hid = _ffn_up(h, wg_t, wu_t, tag + "_up", exchange=up_exchange)
    x_out, f = _mm(hid, wd, "nn", F32, tag + "_down", res=x, gate=gate, aux_dtype=BF16,
                   exchange=down_exchange() if down_exchange else None)
    return x_out, (h, a, u, hid, f)


def _ffn_bwd(dx_out, x, saved, norm_g, scale, gate, wg_t, wu_t, wd, tag, traffic, dact_exchange=None, dw_exchange=None,
             finish_reduction=False):
    h, a, u, hid, f = saved
    df, dgate = _gate_bwd(dx_out, f, gate, tag + "_gate_bwd")
    da, du = _ffn_dact(df, wd, a, u, tag + "_dact", exchange=dact_exchange)
    dwd = _mm(hid, df, "tn", BF16, tag + "_dwd", exchange=dw_exchange() if dw_exchange else None)
    dwg_t = _mm(da, h, "tn", BF16, tag + "_dwg")
    dwu_t = _mm(du, h, "tn", BF16, tag + "_dwu")
    dh = _mm(da, wg_t, "nn", F32, tag + "_dh_a", exchange=traffic.reduce_sibling(tag, [dwg_t, dwu_t, dwd]))
    traffic.add_halves(tag)
    dh = _mm(du, wu_t, "nn", F32, tag + "_dh_u", res=dh, exchange=traffic.reduce_chips(tag) if finish_reduction else None)
    if finish_reduction:
        traffic.sum_chips(tag)
    dx, dshift, dscale, dnorm_g = _norm_mod_bwd(dh, x, norm_g, scale, dx_out, tag + "_norm_bwd",
                                                exchange=traffic.share(tag) if finish_reduction else None)
    return dx, (dshift, dscale, dgate, dnorm_g)


def _layer_step(x, target, mod, gains, forget_bias, conv_w, traffic, att_w, in_shard, in_rows):
    sh1, sc1, g1, sh2, sc2, g2, sh3, sc3, g3 = mod
    norm1_g, norm2_g, norm3_g, final_g, group_g = gains
    s, d = x.shape
    n_heads = att_w // HEAD_DIM
    npair = n_heads // 2
    gate1, gate3 = 0.5 * g1, 0.5 * g3

    def split_w_in(w_in_pad):
        w_in_t = w_in_pad.reshape(N_CHIPS, in_rows, d)[:, :in_shard].reshape(N_CHIPS * in_shard, d)
        return (w_in_t[:3 * att_w], _pad_rows(w_in_t[3 * att_w:3 * att_w + n_heads], LANES), w_in_t[3 * att_w + n_heads:])

    _run_exchange(traffic.gather_ici("ffn1"), "gather_ffn1_ici")
    _run_exchange(traffic.gather_pass("ffn1"), "gather_ffn1_pass")
    wg1_t, wu1_t, wd1 = traffic.weights("ffn1")
    x1, saved1 = _ffn_fwd(x, norm1_g, sh1, sc1, gate1, wg1_t, wu1_t, wd1, "ffn1",
                          up_exchange=traffic.gather_ici("mix"), down_exchange=lambda: traffic.gather_pass("mix"))
    w_in_pad, w_out = traffic.weights("mix")
    wqkv_t, wf_t, wbcx_t = split_w_in(w_in_pad)

    h2 = _norm_mod_fwd(x1, norm2_g, sh2, sc2, "mix_norm_fwd")
    qkv = _mm(h2, wqkv_t, "nt", BF16, "mix_proj_qkv")
    bcx = _mm(h2, wbcx_t, "nt", F32, "mix_proj_bcx")
    flog = _mm(h2, wf_t, "nt", F32, "mix_proj_f")
    flog_t = jnp.pad(flog[:, :n_heads].T, ((0, HEAD_ROWS - n_heads), (0, 0)))
    bias_col = jnp.pad(forget_bias, (0, HEAD_ROWS - n_heads))[:, None]
    f_rows = _forget_fwd(flog_t, bias_col, "forget_fwd")
    frow = f_rows[:n_heads].reshape(npair, 2, s)
    fcol = frow.transpose(0, 2, 1)
    att, lse = _attn_fwd(qkv, fcol, frow, "attn_fwd", exchange=traffic.gather_ici("ffn2"))
    cv = _conv_fwd(bcx, conv_w, "conv_fwd")
    yn = _gnorm_fwd(att, cv, group_g, "gnorm_fwd")
    x2, mix = _mm(yn, w_out, "nn", F32, "mix_out", res=x1, gate=g2, aux_dtype=BF16, exchange=traffic.gather_pass("ffn2"))
    wg2_t, wu2_t, wd2 = traffic.weights("ffn2")

    x3, saved3 = _ffn_fwd(x2, norm3_g, sh3, sc3, gate3, wg2_t, wu2_t, wd2, "ffn2")

    dx3, loss_row, dfinal_g = _final_loss(x3, final_g, target, "final_loss")

    dx2, (dsh3, dsc3, dgate3, dnorm3_g) = _ffn_bwd(
        dx3, x2, saved3, norm3_g, sc3, gate3, wg2_t, wu2_t, wd2, "ffn2", traffic)

    dmix, dg2 = _gate_bwd(dx2, mix, g2, "mix_gate_bwd")
    dyn = _mm(dmix, w_out, "nt", F32, "mix_out_dyn")
    dw_out = _mm(yn, dmix, "tn", BF16, "mix_out_dw")
    datt, dcv, dgroup_g = _gnorm_bwd(dyn, att, cv, group_g, "gnorm_bwd")
    db, dc, dxc, dconv_w = _conv_bwd(dcv, bcx, conv_w, "conv_bwd")
    dbcx = jnp.concatenate([db, dc, dxc], axis=1)
    dq, dk, dv, df_key, df_query = _attn_bwd(qkv, datt, att, lse, fcol, frow, "attn_bwd",
                                             exchange=traffic.reduce_chips("ffn2"))
    traffic.sum_chips("ffn2")
    dqkv = jnp.concatenate([dq.astype(BF16), dk, dv], axis=1)
    df_heads = df_key.reshape(n_heads, s) + df_query[:, ::HEAD_DIM].T
    df_t = jnp.pad(df_heads, ((0, HEAD_ROWS - n_heads), (0, 0)))
    dflog_t, dbias_col = _forget_bwd(df_t, flog_t, bias_col, "forget_bwd")
    dflog = jnp.pad(dflog_t[:n_heads].T, ((0, 0), (0, LANES - n_heads))).astype(BF16)
    dh2 = _mm(dqkv, wqkv_t, "nn", F32, "mix_dh_qkv", exchange=traffic.share("ffn2"))
    dh2 = _mm(dbcx, wbcx_t, "nn", F32, "mix_dh_bcx", res=dh2)
    dh2 = _mm(dflog, wf_t, "nn", F32, "mix_dh_f", res=dh2)
    dwqkv_t = _mm(dqkv, h2, "tn", BF16, "mix_dw_qkv")
    dwbcx_t = _mm(dbcx, h2, "tn", BF16, "mix_dw_bcx")
    dwf_t = _mm(dflog, h2, "tn", BF16, "mix_dw_f")
    dw_in_t = jnp.concatenate([dwqkv_t, dwf_t[:n_heads], dwbcx_t], axis=0).reshape(N_CHIPS, in_shard, d)
    dw_in_t = jnp.pad(dw_in_t, ((0, 0), (0, in_rows - in_shard), (0, 0))).reshape(N_CHIPS * in_rows, d)
    dx1, dsh2, dsc2, dnorm2_g = _norm_mod_bwd(dh2, x1, norm2_g, sc2, dx2, "mix_norm_bwd",
                                              exchange=traffic.reduce_sibling("mix", [dw_in_t, dw_out]))
    traffic.add_halves("mix")

    def share_mix():
        traffic.sum_chips("mix")
        return traffic.share("mix")

    dx, (dsh1, dsc1, dgate1, dnorm1_g) = _ffn_bwd(
        dx1, x, saved1, norm1_g, sc1, gate1, wg1_t, wu1_t, wd1, "ffn1", traffic,
        dact_exchange=traffic.reduce_chips("mix"), dw_exchange=share_mix, finish_reduction=True)

    dmod = [dsh1, dsc1, 0.5 * dgate1, dsh2, dsc2, dg2, dsh3, dsc3, 0.5 * dgate3]
    dgains = [dnorm1_g, dnorm2_g, dnorm3_g, dfinal_g, dgroup_g]
    dbias = dbias_col[:n_heads, 0]
    return dx, loss_row, dmod, dgains, dbias, dconv_w


SMALL_ROWS = 24
ROW_GAINS, ROW_LOSS, ROW_FORGET, ROW_CONV, ROW_MOD = 0, 5, 6, 7, 10
PROW_ADA_B, PROW_GAINS, PROW_FORGET, PROW_CONV = 0, 9, 14, 15


def _round_up(n, m):
    return -(-n // m) * m


def _pad_rows(a, rows):
    return jnp.pad(a, ((0, rows - a.shape[0]), (0, 0)))


def _halves(a):
    return a.reshape(2, a.shape[0] // 2, a.shape[1])


def _rows_at(a, r0, total, width):
    return jnp.pad(a, ((r0, total - r0 - a.shape[0]), (0, width - a.shape[1])))


def kernel(x, c, ada_w, ada_b, norm1_g, ffn1_w_gate, ffn1_w_up, ffn1_w_down, norm2_g, w_in, forget_bias, conv_w, group_norm_g, w_out, norm3_g, ffn2_w_gate, ffn2_w_up, ffn2_w_down, final_g, loss_target, m_ada_w, m_ada_b, m_norm1_g, m_ffn1_w_gate, m_ffn1_w_up, m_ffn1_w_down, m_norm2_g, m_w_in, m_forget_bias, m_conv_w, m_group_norm_g, m_w_out, m_norm3_g, m_ffn2_w_gate, m_ffn2_w_up, m_ffn2_w_down, m_final_g, v_ada_w, v_ada_b, v_norm1_g, v_ffn1_w_gate, v_ffn1_w_up, v_ffn1_w_down, v_norm2_g, v_w_in, v_forget_bias, v_conv_w, v_group_norm_g, v_w_out, v_norm3_g, v_ffn2_w_gate, v_ffn2_w_up, v_ffn2_w_down, v_final_g):
    xi, yi, ci = lax.axis_index("x"), lax.axis_index("y"), lax.axis_index("c")
    chip = 2 * xi + yi
    dev = 4 * xi + 2 * yi + ci
    _, s, d = x.shape
    att_w = d // 2
    conv_width = d - att_w
    n_heads = att_w // HEAD_DIM
    in_shard = w_in.shape[1]
    in_rows = _round_up(in_shard, 32)
    cs = conv_w.shape[1]
    mod_shard = ada_w.shape[1]
    assert N_MOD * d == N_CHIPS * mod_shard and conv_width == N_CHIPS * cs and n_heads % 2 == 0

    pack0 = _rows_at(c, 0, 8, d) + _rows_at(conv_w, 1, 8, d)
    got0 = _all_gather_small(pack0, "gather_cond").reshape(N_DEV, 8, d)
    c16 = _pad_rows(got0[:, 0, :], 16)
    conv_full = got0[0::2, 1:1 + CONV_K, :cs].transpose(1, 0, 2).reshape(CONV_K, conv_width)

    ada_b_mine = lax.dynamic_slice(ada_b, (chip * mod_shard,), (mod_shard,))[None, :]
    mod_part = _ada_fwd(c16, ada_w, ada_b_mine, "ada_fwd")
    got1 = _all_gather_small(mod_part, "gather_mod").reshape(N_DEV, 16, mod_shard)
    mod_mine = lax.dynamic_index_in_dim(got1[0::2], dev, axis=1, keepdims=False).reshape(N_MOD, d)
    mod = [mod_mine[i:i + 1] for i in range(N_MOD)]

    def t_bf(w):
        return w.T.astype(BF16)

    shards = {"ffn1": [_halves(t_bf(ffn1_w_gate)), _halves(t_bf(ffn1_w_up)), _halves(ffn1_w_down.astype(BF16))],
              "mix": [_halves(_pad_rows(t_bf(w_in), in_rows)), _halves(w_out.astype(BF16))],
              "ffn2": [_halves(t_bf(ffn2_w_gate)), _halves(t_bf(ffn2_w_up)), _halves(ffn2_w_down.astype(BF16))]}
    core = ci.astype(jnp.int32).reshape(1)
    chip_arr = chip.astype(jnp.int32).reshape(1)
    traffic = _WeightTraffic(shards, core, chip_arr)

    gains = [g[None, :] for g in (norm1_g, norm2_g, norm3_g, final_g, group_norm_g)]
    dx, loss_row, dmod, dgains, dbias, dconv_w = _layer_step(
        x[0], loss_target[0], mod, gains, forget_bias, conv_full, traffic, att_w, in_shard, in_rows)

    pack = sum(_rows_at(g, ROW_GAINS + i, SMALL_ROWS, d) for i, g in enumerate(dgains))
    pack += _rows_at(loss_row, ROW_LOSS, SMALL_ROWS, d) + _rows_at(dbias[None, :], ROW_FORGET, SMALL_ROWS, d)
    pack += _rows_at(dconv_w, ROW_CONV, SMALL_ROWS, d)
    pack += sum(_rows_at(g, ROW_MOD + i, SMALL_ROWS, d) for i, g in enumerate(dmod))
    got2 = _all_gather_small(pack, "gather_small_grads").reshape(N_DEV, SMALL_ROWS, d)
    tot = _sum_devices(got2, "sum_small_grads")
    loss = tot[ROW_LOSS, 0]
    grad_ada_b = tot[ROW_MOD:ROW_MOD + N_MOD].reshape(N_MOD * d)
    grad_conv = lax.dynamic_slice(tot[ROW_CONV:ROW_CONV + CONV_K], (0, chip * cs), (CONV_K, cs))
    dmod_all = got2[:, ROW_MOD:ROW_MOD + N_MOD, :].reshape(N_DEV, N_MOD * d)
    dmod16 = _pad_rows(lax.dynamic_slice(dmod_all, (0, chip * mod_shard), (N_DEV, mod_shard)), 16)

    totals = traffic.totals("ffn1") + traffic.totals("mix") + traffic.totals("ffn2")

    names = ("ffn1_w_gate", "ffn1_w_up", "ffn1_w_down", "w_in", "w_out", "ffn2_w_gate", "ffn2_w_up", "ffn2_w_down")
    transposed = ("ffn1_w_gate", "ffn1_w_up", "w_in", "ffn2_w_gate", "ffn2_w_up")
    params = {"ffn1_w_gate": (ffn1_w_gate, m_ffn1_w_gate, v_ffn1_w_gate), "ffn1_w_up": (ffn1_w_up, m_ffn1_w_up, v_ffn1_w_up),
              "ffn1_w_down": (ffn1_w_down, m_ffn1_w_down, v_ffn1_w_down), "w_in": (w_in, m_w_in, v_w_in),
              "w_out": (w_out, m_w_out, v_w_out), "ffn2_w_gate": (ffn2_w_gate, m_ffn2_w_gate, v_ffn2_w_gate),
              "ffn2_w_up": (ffn2_w_up, m_ffn2_w_up, v_ffn2_w_up), "ffn2_w_down": (ffn2_w_down, m_ffn2_w_down, v_ffn2_w_down)}
    out = {}
    for name_, (mine, theirs) in zip(names, totals):
        w, m, v = params[name_]
        if name_ in transposed:
            w, m, v = w.T, m.T, v.T
        if name_ == "w_in":
            both = jnp.where(ci == 0, jnp.concatenate([mine, theirs]), jnp.concatenate([theirs, mine]))[:in_shard]
            res = (both,) + tuple(_adamw(w, both, m, v, "adamw_" + name_))
        else:
            res = _adamw_halves(w, mine, theirs, m, v, core, "adamw_" + name_)
        out[name_] = tuple(r.T for r in res) if name_ in transposed else tuple(res)
    c16_t = c16.T
    out["ada_w"] = tuple(_ada_update(c16_t, dmod16, ada_w, m_ada_w, v_ada_w, "adamw_ada_w"))

    def small_pack(ada_b_, gains_, forget_, conv_):
        p = _rows_at(ada_b_.reshape(N_MOD, d), PROW_ADA_B, SMALL_ROWS, d)
        p += sum(_rows_at(g[None, :], PROW_GAINS + i, SMALL_ROWS, d) for i, g in enumerate(gains_))
        p += _rows_at(forget_[None, :], PROW_FORGET, SMALL_ROWS, d) + _rows_at(conv_, PROW_CONV, SMALL_ROWS, d)
        return p

    g_gains = [tot[ROW_GAINS + i] for i in range(5)]
    g_forget = tot[ROW_FORGET, :n_heads]
    sw = small_pack(ada_b, (norm1_g, norm2_g, norm3_g, final_g, group_norm_g), forget_bias, conv_w)
    sm = small_pack(m_ada_b, (m_norm1_g, m_norm2_g, m_norm3_g, m_final_g, m_group_norm_g), m_forget_bias, m_conv_w)
    sv = small_pack(v_ada_b, (v_norm1_g, v_norm2_g, v_norm3_g, v_final_g, v_group_norm_g), v_forget_bias, v_conv_w)
    sg = small_pack(grad_ada_b, g_gains, g_forget, grad_conv)
    small = (sg,) + tuple(_adamw(sw, sg, sm, sv, "adamw_small"))

    def unpack(p):
        r = {"ada_b": p[PROW_ADA_B:PROW_ADA_B + N_MOD].reshape(N_MOD * d), "forget_bias": p[PROW_FORGET, :n_heads],
             "conv_w": p[PROW_CONV:PROW_CONV + CONV_K, :cs]}
        for i, nm in enumerate(("norm1_g", "norm2_g", "norm3_g", "final_g", "group_norm_g")):
            r[nm] = p[PROW_GAINS + i]
        return r

    small = [unpack(p) for p in small]
    order = ("ada_w", "ada_b", "norm1_g", "ffn1_w_gate", "ffn1_w_up", "ffn1_w_down", "norm2_g", "w_in", "forget_bias",
             "conv_w", "group_norm_g", "w_out", "norm3_g", "ffn2_w_gate", "ffn2_w_up", "ffn2_w_down", "final_g")
    result = [loss, dx[None]]
    for k in range(4):
        result += [out[nm][k] if nm in out else small[k][nm] for nm in order]
    return tuple(result)
```

```python
import functools
import math

import jax
import jax.numpy as jnp
from jax import lax
from jax.experimental import pallas as pl
from jax.experimental.pallas import tpu as pltpu

F32 = jnp.float32
BF16 = jnp.bfloat16

HEAD_DIM = 64
CONV_K = 3
N_MOD = 9
EPS = 1e-6
ADAM_LR = 0.001
ADAM_B1 = 0.9
ADAM_B2 = 0.999
ADAM_EPS = 1e-08
ADAM_WD = 0.01
ADAM_STEP = 10

LANES = 128
N_CHIPS = 4
N_DEV = 8
VMEM_LIMIT_BYTES = 56 * 1024 * 1024
NEG_BIG = -1e30
MESH = pl.DeviceIdType.MESH

_NT = (((1,), (1,)), ((), ()))
_NN = (((1,), (0,)), ((), ()))
_TN = (((0,), (0,)), ((), ()))


def _params(*sem):
    return pltpu.CompilerParams(dimension_semantics=sem, vmem_limit_bytes=VMEM_LIMIT_BYTES)


class _Exchange:
    def __init__(self, inputs, out_shapes, n_sems, start, finish, aliases=None):
        self.inputs, self.out_shapes, self.n_sems = list(inputs), list(out_shapes), n_sems
        self.start, self.finish, self.aliases = start, finish, dict(aliases or {})
        self.results = None


def _pc(body, exchange=None, **kw):
    if exchange is None:
        return pl.pallas_call(body, **kw)
    grid = kw["grid"]
    single = not isinstance(kw["out_shape"], (tuple, list))
    out_shape = [kw["out_shape"]] if single else list(kw["out_shape"])
    out_specs = [kw["out_specs"]] if single else list(kw["out_specs"])
    in_specs = list(kw["in_specs"])
    scratch = list(kw.get("scratch_shapes", ()))
    n_in, n_out, n_scr = len(in_specs), len(out_shape), len(scratch)
    n_xi, n_xo = len(exchange.inputs), len(exchange.out_shapes)

    def wrapped(*refs):
        pos = [n_in, n_in + n_xi, n_in + n_xi + n_out, n_in + n_xi + n_out + n_xo]
        ins, x_in, outs, x_out = refs[:pos[0]], refs[pos[0]:pos[1]], refs[pos[1]:pos[2]], refs[pos[2]:pos[3]]
        scr = refs[pos[3]:pos[3] + n_scr]
        send_sems, recv_sems = refs[pos[3] + n_scr:]
        ids = [pl.program_id(a) for a in range(len(grid))]
        first = functools.reduce(jnp.logical_and, [i == 0 for i in ids])
        last = functools.reduce(jnp.logical_and, [i == g - 1 for i, g in zip(ids, grid)])

        @pl.when(first)
        def _():
            exchange.start(x_in, x_out, send_sems, recv_sems)

        body(*ins, *outs, *scr)

        @pl.when(last)
        def _():
            exchange.finish(x_in, x_out, send_sems, recv_sems)

    call = pl.pallas_call(
        wrapped, out_shape=tuple(out_shape) + tuple(exchange.out_shapes), grid=grid,
        in_specs=in_specs + [_ANY] * n_xi, out_specs=tuple(out_specs) + (_ANY,) * n_xo,
        scratch_shapes=scratch + [pltpu.SemaphoreType.DMA((exchange.n_sems,)), pltpu.SemaphoreType.DMA((exchange.n_sems,))],
        input_output_aliases={n_in + a: n_out + b for a, b in exchange.aliases.items()},
        compiler_params=_params(*(["arbitrary"] * len(grid))), name=kw["name"])

    def run(*args):
        res = call(*args, *exchange.inputs)
        exchange.results = list(res[n_out:])
        return res[0] if single else tuple(res[:n_out])

    return run


_ANY = pl.BlockSpec(memory_space=pl.ANY)


def _tile(n, pref, mult):
    best = None
    t = mult
    while t <= min(n, pref):
        if n % t == 0:
            best = t
        t += mult
    return n if best is None else best


def _sds(shape, dtype):
    return jax.ShapeDtypeStruct(shape, dtype)


def _vec_spec(d):
    return pl.BlockSpec((1, d), lambda *_: (0, 0))


def _norm_mod_fwd(x, g, shift, scale, name):
    s, d = x.shape
    tr = _tile(s, 512, 16)

    def body(x_ref, g_ref, sh_ref, sc_ref, h_ref):
        xv = x_ref[...]
        rstd = lax.rsqrt(jnp.mean(xv * xv, axis=-1, keepdims=True) + EPS)
        n = xv * rstd * g_ref[...]
        h_ref[...] = (n * (1.0 + sc_ref[...]) + sh_ref[...]).astype(BF16)

    row = pl.BlockSpec((tr, d), lambda i: (i, 0))
    return _pc(body, out_shape=_sds((s, d), BF16), grid=(s // tr,),
               in_specs=[row, _vec_spec(d), _vec_spec(d), _vec_spec(d)], out_specs=row,
               compiler_params=_params("parallel"), name=name)(x, g, shift, scale)


def _norm_mod_bwd(dh, x, g, scale, dres, name, exchange=None):
    s, d = x.shape
    tr = _tile(s, 256, 8)

    def body(dh_ref, x_ref, g_ref, sc_ref, dres_ref, dx_ref, dsh_ref, dsc_ref, dg_ref):
        @pl.when(pl.program_id(0) == 0)
        def _():
            dsh_ref[...] = jnp.zeros_like(dsh_ref)
            dsc_ref[...] = jnp.zeros_like(dsc_ref)
            dg_ref[...] = jnp.zeros_like(dg_ref)

        xv = x_ref[...]
        dhv = dh_ref[...]
        gv = g_ref[...]
        rstd = lax.rsqrt(jnp.mean(xv * xv, axis=-1, keepdims=True) + EPS)
        xhat = xv * rstd
        dn = dhv * (1.0 + sc_ref[...])
        dsh_ref[...] += jnp.sum(dhv, axis=0, keepdims=True)
        dsc_ref[...] += jnp.sum(dhv * (xhat * gv), axis=0, keepdims=True)
        dg_ref[...] += jnp.sum(dn * xhat, axis=0, keepdims=True)
        dxh = dn * gv
        proj = jnp.mean(dxh * xhat, axis=-1, keepdims=True)
        dx_ref[...] = dres_ref[...] + rstd * (dxh - xhat * proj)

    row = pl.BlockSpec((tr, d), lambda i: (i, 0))
    vec = _vec_spec(d)
    return _pc(body, exchange, out_shape=(_sds((s, d), F32), _sds((1, d), F32), _sds((1, d), F32), _sds((1, d), F32)),
               grid=(s // tr,), in_specs=[row, row, vec, vec, row], out_specs=(row, vec, vec, vec),
               compiler_params=_params("arbitrary"), name=name)(dh, x, g, scale, dres)


def _gate_bwd(dx, f, gate, name):
    s, d = dx.shape
    tr = _tile(s, 512, 16)

    def body(dx_ref, f_ref, gate_ref, df_ref, dg_ref):
        @pl.when(pl.program_id(0) == 0)
        def _():
            dg_ref[...] = jnp.zeros_like(dg_ref)

        dxv = dx_ref[...]
        df_ref[...] = (dxv * gate_ref[...]).astype(BF16)
        dg_ref[...] += jnp.sum(dxv * f_ref[...].astype(F32), axis=0, keepdims=True)

    row = pl.BlockSpec((tr, d), lambda i: (i, 0))
    vec = _vec_spec(d)
    return _pc(body, out_shape=(_sds((s, d), BF16), _sds((1, d), F32)), grid=(s // tr,),
               in_specs=[row, row, vec], out_specs=(row, vec),
               compiler_params=_params("arbitrary"), name=name)(dx, f, gate)


def _final_loss(x, g, target, name):
    s, d = x.shape
    tr = _tile(s, 256, 8)
    nsteps = s // tr

    def body(x_ref, g_ref, t_ref, dx_ref, loss_ref, dg_ref):
        i = pl.program_id(0)

        @pl.when(i == 0)
        def _():
            loss_ref[...] = jnp.zeros_like(loss_ref)
            dg_ref[...] = jnp.zeros_like(dg_ref)

        xv = x_ref[...]
        gv = g_ref[...]
        rstd = lax.rsqrt(jnp.mean(xv * xv, axis=-1, keepdims=True) + EPS)
        xhat = xv * rstd
        err = xhat * gv - t_ref[...]
        dy = err * (1.0 / d)
        loss_ref[...] += jnp.sum(0.5 * err * dy, axis=0, keepdims=True)
        dg_ref[...] += jnp.sum(dy * xhat, axis=0, keepdims=True)
        dxh = dy * gv
        proj = jnp.mean(dxh * xhat, axis=-1, keepdims=True)
        dx_ref[...] = rstd * (dxh - xhat * proj)

        @pl.when(i == nsteps - 1)
        def _():
            loss_ref[...] = jnp.broadcast_to(jnp.sum(loss_ref[...], axis=-1, keepdims=True), loss_ref.shape)

    row = pl.BlockSpec((tr, d), lambda i: (i, 0))
    vec = _vec_spec(d)
    return _pc(body, out_shape=(_sds((s, d), F32), _sds((1, d), F32), _sds((1, d), F32)), grid=(nsteps,),
               in_specs=[row, vec, row], out_specs=(row, vec, vec),
               compiler_params=_params("arbitrary"), name=name)(x, g, target)


def _mm(lhs, rhs, dims, out_dtype, name, res=None, gate=None, aux_dtype=None, exchange=None):
    if dims == "nn":
        (m, k), (k2, n) = lhs.shape, rhs.shape
    elif dims == "nt":
        (m, k), (n, k2) = lhs.shape, rhs.shape
    else:
        (k, m), (k2, n) = lhs.shape, rhs.shape
    assert k == k2, (lhs.shape, rhs.shape, dims)
    tn = _tile(n, 1024, LANES)
    tm = _tile(m, 512, LANES if dims == "tn" else 16)
    tk = _tile(k, 4096, LANES)
    nk = k // tk
    dn = {"nn": _NN, "nt": _NT, "tn": _TN}[dims]
    lhs_spec = (pl.BlockSpec((tk, tm), lambda i, j, kk: (kk, i)) if dims == "tn"
                else pl.BlockSpec((tm, tk), lambda i, j, kk: (i, kk)))
    rhs_spec = (pl.BlockSpec((tn, tk), lambda i, j, kk: (j, kk)) if dims == "nt"
                else pl.BlockSpec((tk, tn), lambda i, j, kk: (kk, j)))
    out_spec = pl.BlockSpec((tm, tn), lambda i, j, kk: (i, j))
    has_res, has_gate, has_aux = res is not None, gate is not None, aux_dtype is not None

    def body(*refs):
        refs = list(refs)
        l_ref, r_ref = refs[0], refs[1]
        pos = 2
        res_ref = gate_ref = aux_ref = None
        if has_res:
            res_ref = refs[pos]; pos += 1
        if has_gate:
            gate_ref = refs[pos]; pos += 1
        out_ref = refs[pos]; pos += 1
        if has_aux:
            aux_ref = refs[pos]; pos += 1
        acc_ref = refs[pos]
        kk = pl.program_id(2)
        part = lax.dot_general(l_ref[...], r_ref[...], dn, preferred_element_type=F32)

        @pl.when(kk == 0)
        def _():
            acc_ref[...] = part

        @pl.when(kk > 0)
        def _():
            acc_ref[...] += part

        @pl.when(kk == nk - 1)
        def _():
            acc = acc_ref[...]
            if has_aux:
                aux_ref[...] = acc.astype(aux_dtype)
            if has_gate:
                acc = acc * gate_ref[...]
            if has_res:
                acc = res_ref[...] + acc
            out_ref[...] = acc.astype(out_dtype)

    in_specs = [lhs_spec, rhs_spec]
    args = [lhs, rhs]
    if has_res:
        in_specs.append(out_spec); args.append(res)
    if has_gate:
        in_specs.append(pl.BlockSpec((1, tn), lambda i, j, kk: (0, j))); args.append(gate)
    out_shape = [_sds((m, n), out_dtype)]
    out_specs = [out_spec]
    if has_aux:
        out_shape.append(_sds((m, n), aux_dtype)); out_specs.append(out_spec)
    outs = _pc(body, exchange, out_shape=tuple(out_shape), grid=(m // tm, n // tn, nk), in_specs=in_specs,
               out_specs=tuple(out_specs), scratch_shapes=[pltpu.VMEM((tm, tn), F32)],
               compiler_params=_params("parallel", "parallel", "arbitrary"), name=name)(*args)
    return outs if has_aux else outs[0]


def _ffn_up(h, wg_t, wu_t, name, exchange=None):
    s, d = h.shape
    f = wg_t.shape[0]
    tm = _tile(s, 1024, 16)
    tn = _tile(f, 256, LANES)

    def body(h_ref, wg_ref, wu_ref, a_ref, u_ref, hid_ref):
        hv = h_ref[...]
        a = lax.dot_general(hv, wg_ref[...], _NT, preferred_element_type=F32)
        u = lax.dot_general(hv, wu_ref[...], _NT, preferred_element_type=F32)
        a_ref[...] = a.astype(BF16)
        u_ref[...] = u.astype(BF16)
        hid_ref[...] = (a * jax.nn.sigmoid(a) * u).astype(BF16)

    hs = pl.BlockSpec((tm, d), lambda i, j: (i, 0))
    ws = pl.BlockSpec((tn, d), lambda i, j: (j, 0))
    os_ = pl.BlockSpec((tm, tn), lambda i, j: (i, j))
    return _pc(body, exchange, out_shape=(_sds((s, f), BF16),) * 3, grid=(s // tm, f // tn),
               in_specs=[hs, ws, ws], out_specs=(os_, os_, os_),
               compiler_params=_params("parallel", "parallel"), name=name)(h, wg_t, wu_t)


def _ffn_dact(df, wd, a, u, name, exchange=None):
    s, d = df.shape
    f = wd.shape[0]
    tm = _tile(s, 1024, 16)
    tn = _tile(f, 256, LANES)

    def body(df_ref, wd_ref, a_ref, u_ref, da_ref, du_ref):
        dhid = lax.dot_general(df_ref[...], wd_ref[...], _NT, preferred_element_type=F32)
        av = a_ref[...].astype(F32)
        uv = u_ref[...].astype(F32)
        sig = jax.nn.sigmoid(av)
        da_ref[...] = (dhid * uv * (sig * (1.0 + av * (1.0 - sig)))).astype(BF16)
        du_ref[...] = (dhid * (av * sig)).astype(BF16)

    ds_ = pl.BlockSpec((tm, d), lambda i, j: (i, 0))
    ws = pl.BlockSpec((tn, d), lambda i, j: (j, 0))
    os_ = pl.BlockSpec((tm, tn), lambda i, j: (i, j))
    return _pc(body, exchange, out_shape=(_sds((s, f), BF16),) * 2, grid=(s // tm, f // tn),
               in_specs=[ds_, ws, os_, os_], out_specs=(os_, os_),
               compiler_params=_params("parallel", "parallel"), name=name)(df, wd, a, u)


def _split3(v):
    hi = v.astype(BF16)
    r1 = v - hi.astype(F32)
    mid = r1.astype(BF16)
    lo = (r1 - mid.astype(F32)).astype(BF16)
    return hi, mid, lo


def _dot3(v, mat):
    hi, mid, lo = _split3(v)
    out = lax.dot_general(hi, mat, _NN, preferred_element_type=F32)
    out += lax.dot_general(mid, mat, _NN, preferred_element_type=F32)
    out += lax.dot_general(lo, mat, _NN, preferred_element_type=F32)
    return out


def _forget_fwd(flog_t, bias, name):
    h, s = flog_t.shape
    blk = _tile(s, 512, LANES)
    tri = (jnp.arange(blk)[:, None] <= jnp.arange(blk)[None, :]).astype(BF16)

    def body(z_ref, b_ref, tri_ref, f_ref, carry):
        @pl.when(pl.program_id(0) == 0)
        def _():
            carry[...] = jnp.zeros_like(carry)

        z = z_ref[...] + b_ref[...]
        e = jnp.exp(-jnp.abs(z))
        w = 1.0 + e
        log1p_e = jnp.where(w == 1.0, e, jnp.log(w) * (e / (w - 1.0)))
        lf = jnp.minimum(z, 0.0) - log1p_e
        out = carry[...] + _dot3(lf, tri_ref[...])
        f_ref[...] = out
        carry[...] = out[:, blk - 1:blk]

    zs = pl.BlockSpec((h, blk), lambda i: (0, i))
    return _pc(body, out_shape=_sds((h, s), F32), grid=(s // blk,),
               in_specs=[zs, pl.BlockSpec((h, 1), lambda i: (0, 0)), pl.BlockSpec((blk, blk), lambda i: (0, 0))],
               out_specs=zs, scratch_shapes=[pltpu.VMEM((h, 1), F32)],
               compiler_params=_params("arbitrary"), name=name)(flog_t, bias, tri)


def _forget_bwd(df_t, flog_t, bias, name):
    h, s = flog_t.shape
    blk = _tile(s, 512, LANES)
    nb = s // blk
    tri = (jnp.arange(blk)[:, None] >= jnp.arange(blk)[None, :]).astype(BF16)

    def body(df_ref, z_ref, b_ref, tri_ref, dz_ref, db_ref, carry):
        @pl.when(pl.program_id(0) == 0)
        def _():
            carry[...] = jnp.zeros_like(carry)
            db_ref[...] = jnp.zeros_like(db_ref)

        rc = carry[...] + _dot3(df_ref[...], tri_ref[...])
        carry[...] = rc[:, 0:1]
        dz = rc * jax.nn.sigmoid(-(z_ref[...] + b_ref[...]))
        dz_ref[...] = dz
        db_ref[...] += jnp.sum(dz, axis=-1, keepdims=True)

    rev = pl.BlockSpec((h, blk), lambda i: (0, nb - 1 - i))
    col = pl.BlockSpec((h, 1), lambda i: (0, 0))
    return _pc(body, out_shape=(_sds((h, s), F32), _sds((h, 1), F32)), grid=(nb,),
               in_specs=[rev, rev, col, pl.BlockSpec((blk, blk), lambda i: (0, 0))],
               out_specs=(rev, col), scratch_shapes=[pltpu.VMEM((h, 1), F32)],
               compiler_params=_params("arbitrary"), name=name)(df_t, flog_t, bias, tri)


def _attn_tiles(s):
    return _tile(s, 512, LANES)


def _attn_fwd(qkv, fcol, frow, name, exchange=None):
    s = qkv.shape[0]
    a_w = qkv.shape[1] // 3
    npair = a_w // LANES
    t = _attn_tiles(s)
    nq = s // t
    scale = 1.0 / math.sqrt(HEAD_DIM)

    def body(q_ref, k_ref, v_ref, fc_ref, fr_ref, o_ref, lse_ref, m_sc, l_sc, acc_sc):
        qi = pl.program_id(1)
        ki = pl.program_id(2)
        first = lax.broadcasted_iota(jnp.int32, (1, LANES), 1) < HEAD_DIM

        @pl.when(ki == 0)
        def _():
            m_sc[...] = jnp.full_like(m_sc, NEG_BIG)
            l_sc[...] = jnp.zeros_like(l_sc)
            acc_sc[...] = jnp.zeros_like(acc_sc)

        def step(diag):
            q2, k2, v2 = q_ref[...], k_ref[...], v_ref[...]
            fc = fc_ref[0]
            fr = fr_ref[0]
            m_old = m_sc[...]
            keep = None
            if diag:
                keep = (lax.broadcasted_iota(jnp.int32, (t, t), 0) >= lax.broadcasted_iota(jnp.int32, (t, t), 1))
            m_new, rs, pv = [], [], []
            for hh in range(2):
                sel = first if hh == 0 else jnp.logical_not(first)
                qm = jnp.where(sel, q2, jnp.zeros_like(q2))
                vm = jnp.where(sel, v2, jnp.zeros_like(v2))
                sc = lax.dot_general(qm, k2, _NT, preferred_element_type=F32) * scale
                sc = sc + (fc[:, hh:hh + 1] - fr[hh:hh + 1, :])
                if diag:
                    sc = jnp.where(keep, sc, NEG_BIG)
                mo = m_old[:, hh * HEAD_DIM:hh * HEAD_DIM + 1]
                mn = jnp.maximum(mo, jnp.max(sc, axis=1, keepdims=True))
                p = jnp.exp(sc - mn)
                m_new.append(mn)
                rs.append(jnp.sum(p, axis=1, keepdims=True))
                pv.append(lax.dot_general(p.astype(BF16), vm, _NN, preferred_element_type=F32))
            m2 = jnp.where(first, m_new[0], m_new[1])
            alpha = jnp.exp(m_old - m2)
            m_sc[...] = m2
            l_sc[...] = alpha * l_sc[...] + jnp.where(first, rs[0], rs[1])
            acc_sc[...] = alpha * acc_sc[...] + pv[0] + pv[1]

        @pl.when(ki < qi)
        def _():
            step(False)

        @pl.when(ki == qi)
        def _():
            step(True)
            l2 = l_sc[...]
            o_ref[...] = acc_sc[...] / l2
            lse_ref[...] = m_sc[...] + jnp.log(l2)

    qs = pl.BlockSpec((t, LANES), lambda p, qi, ki: (qi, p))
    ks = pl.BlockSpec((t, LANES), lambda p, qi, ki: (jnp.minimum(ki, qi), npair + p))
    vs = pl.BlockSpec((t, LANES), lambda p, qi, ki: (jnp.minimum(ki, qi), 2 * npair + p))
    fcs = pl.BlockSpec((1, t, 2), lambda p, qi, ki: (p, qi, 0))
    frs = pl.BlockSpec((1, 2, t), lambda p, qi, ki: (p, 0, jnp.minimum(ki, qi)))
    return _pc(body, exchange, out_shape=(_sds((s, a_w), F32), _sds((s, a_w), F32)), grid=(npair, nq, nq),
               in_specs=[qs, ks, vs, fcs, frs], out_specs=(qs, qs),
               scratch_shapes=[pltpu.VMEM((t, LANES), F32)] * 3,
               compiler_params=_params("parallel", "arbitrary", "arbitrary"), name=name)(qkv, qkv, qkv, fcol, frow)


def _attn_bwd(qkv, do, o, lse, fcol, frow, name, exchange=None):
    s = qkv.shape[0]
    a_w = qkv.shape[1] // 3
    npair = a_w // LANES
    t = _attn_tiles(s)
    nq = s // t
    scale = 1.0 / math.sqrt(HEAD_DIM)

    def body(q_ref, k_ref, v_ref, do_ref, o_ref, lse_ref, fc_ref, fr_ref,
             dq_ref, dk_ref, dv_ref, df_ref, dfq_ref, dk_sc, dv_sc):
        ki = pl.program_id(1)
        qi = pl.program_id(2)
        first = lax.broadcasted_iota(jnp.int32, (1, LANES), 1) < HEAD_DIM

        @pl.when(jnp.logical_and(ki == 0, qi == 0))
        def _():
            dq_ref[...] = jnp.zeros_like(dq_ref)
            dfq_ref[...] = jnp.zeros_like(dfq_ref)

        def step(diag):
            q2, k2, v2, do2 = q_ref[...], k_ref[...], v_ref[...], do_ref[...]
            fc = fc_ref[0]
            fr = fr_ref[0]
            lse2 = lse_ref[...]
            dd = do2.astype(F32) * o_ref[...]
            keep = None
            if diag:
                keep = (lax.broadcasted_iota(jnp.int32, (t, t), 0) >= lax.broadcasted_iota(jnp.int32, (t, t), 1))
            dq_part = jnp.zeros((t, LANES), F32)
            dk_part = jnp.zeros((t, LANES), F32)
            dv_part = jnp.zeros((t, LANES), F32)
            dfs, rsum = [], []
            for hh in range(2):
                sel = first if hh == 0 else jnp.logical_not(first)
                qm = jnp.where(sel, q2, jnp.zeros_like(q2))
                km = jnp.where(sel, k2, jnp.zeros_like(k2))
                dom = jnp.where(sel, do2, jnp.zeros_like(do2))
                delta = jnp.sum(jnp.where(sel, dd, 0.0), axis=1, keepdims=True)
                sc = lax.dot_general(qm, k2, _NT, preferred_element_type=F32) * scale
                sc = sc + (fc[:, hh:hh + 1] - fr[hh:hh + 1, :])
                if diag:
                    sc = jnp.where(keep, sc, NEG_BIG)
                p = jnp.exp(sc - lse2[:, hh * HEAD_DIM:hh * HEAD_DIM + 1])
                dp = lax.dot_general(dom, v2, _NT, preferred_element_type=F32)
                dsv = p * (dp - delta)
                dfs.append(-jnp.sum(dsv, axis=0, keepdims=True))
                rsum.append(jnp.sum(dsv, axis=1, keepdims=True))
                ds_b = dsv.astype(BF16)
                dv_part += lax.dot_general(p.astype(BF16), dom, _TN, preferred_element_type=F32)
                dk_part += lax.dot_general(ds_b, qm, _TN, preferred_element_type=F32)
                dq_part += lax.dot_general(ds_b, km, _NN, preferred_element_type=F32)
            rows = pl.ds(pl.multiple_of(qi * t, t), t)
            dq_ref[rows, :] += dq_part * scale
            dfq_ref[rows, :] += jnp.where(first, rsum[0], rsum[1])
            dfv = jnp.concatenate(dfs, axis=0)
            if diag:
                dk_sc[...] = dk_part * scale
                dv_sc[...] = dv_part
                df_ref[0] = dfv
            else:
                dk_sc[...] += dk_part * scale
                dv_sc[...] += dv_part
                df_ref[0] += dfv

        @pl.when(qi == ki)
        def _():
            step(True)

        @pl.when(qi > ki)
        def _():
            step(False)

        @pl.when(qi == nq - 1)
        def _():
            dk_ref[...] = dk_sc[...].astype(BF16)
            dv_ref[...] = dv_sc[...].astype(BF16)

    qs = pl.BlockSpec((t, LANES), lambda p, ki, qi: (jnp.maximum(qi, ki), p))
    ks = pl.BlockSpec((t, LANES), lambda p, ki, qi: (ki, npair + p))
    vs = pl.BlockSpec((t, LANES), lambda p, ki, qi: (ki, 2 * npair + p))
    kout = pl.BlockSpec((t, LANES), lambda p, ki, qi: (ki, p))
    fcs = pl.BlockSpec((1, t, 2), lambda p, ki, qi: (p, jnp.maximum(qi, ki), 0))
    frs = pl.BlockSpec((1, 2, t), lambda p, ki, qi: (p, 0, ki))
    dqs = pl.BlockSpec((s, LANES), lambda p, ki, qi: (0, p))
    return _pc(body, exchange,
               out_shape=(_sds((s, a_w), F32), _sds((s, a_w), BF16), _sds((s, a_w), BF16), _sds((npair, 2, s), F32),
                          _sds((s, a_w), F32)),
               grid=(npair, nq, nq), in_specs=[qs, ks, vs, qs, qs, qs, fcs, frs],
               out_specs=(dqs, kout, kout, frs, dqs),
               scratch_shapes=[pltpu.VMEM((t, LANES), F32)] * 2,
               compiler_params=_params("parallel", "arbitrary", "arbitrary"), name=name)(
                   qkv, qkv, qkv, do, o, lse, fcol, frow)


def _shift_down(z, k, rows):
    return jnp.where(rows >= k, pltpu.roll(z, k, 0), 0.0)


def _shift_up(z, k, rows, n):
    return jnp.where(rows < n - k, pltpu.roll(z, n - k, 0), 0.0)


def _conv_fwd(bcx, conv_w, name):
    s = bcx.shape[0]
    cw = bcx.shape[1] // 3
    nb = cw // LANES

    def body(b_ref, c_ref, x_ref, w_ref, cv_ref):
        rows = lax.broadcasted_iota(jnp.int32, (s, LANES), 0)
        z = c_ref[...] * x_ref[...]
        w = w_ref[...]
        y = w[2:3, :] * z + w[1:2, :] * _shift_down(z, 1, rows) + w[0:1, :] * _shift_down(z, 2, rows)
        cv_ref[...] = b_ref[...] * y

    def col(off):
        return pl.BlockSpec((s, LANES), lambda j: (0, j + off))

    return _pc(body, out_shape=_sds((s, cw), F32), grid=(nb,),
               in_specs=[col(0), col(nb), col(2 * nb), pl.BlockSpec((CONV_K, LANES), lambda j: (0, j))],
               out_specs=col(0), compiler_params=_params("parallel"), name=name)(bcx, bcx, bcx, conv_w)


def _conv_bwd(dcv, bcx, conv_w, name):
    s = bcx.shape[0]
    cw = bcx.shape[1] // 3
    nb = cw // LANES

    def body(dcv_ref, b_ref, c_ref, x_ref, w_ref, db_ref, dc_ref, dxc_ref, dw_ref):
        rows = lax.broadcasted_iota(jnp.int32, (s, LANES), 0)
        cv_, xv = c_ref[...], x_ref[...]
        z = cv_ * xv
        w = w_ref[...]
        z1 = _shift_down(z, 1, rows)
        z2 = _shift_down(z, 2, rows)
        y = w[2:3, :] * z + w[1:2, :] * z1 + w[0:1, :] * z2
        dcvv = dcv_ref[...]
        db_ref[...] = (dcvv * y).astype(BF16)
        dy = dcvv * b_ref[...]
        dw_ref[0:1, :] = jnp.sum(dy * z2, axis=0, keepdims=True)
        dw_ref[1:2, :] = jnp.sum(dy * z1, axis=0, keepdims=True)
        dw_ref[2:3, :] = jnp.sum(dy * z, axis=0, keepdims=True)
        dz = w[2:3, :] * dy + w[1:2, :] * _shift_up(dy, 1, rows, s) + w[0:1, :] * _shift_up(dy, 2, rows, s)
        dc_ref[...] = (dz * xv).astype(BF16)
        dxc_ref[...] = (dz * cv_).astype(BF16)

    def col(off):
        return pl.BlockSpec((s, LANES), lambda j: (0, j + off))

    wspec = pl.BlockSpec((CONV_K, LANES), lambda j: (0, j))
    db, dc, dxc, dw = _pc(body, out_shape=(_sds((s, cw), BF16),) * 3 + (_sds((CONV_K, cw), F32),), grid=(nb,),
                          in_specs=[col(0), col(0), col(nb), col(2 * nb), wspec],
                          out_specs=(col(0), col(0), col(0), wspec),
                          compiler_params=_params("parallel"), name=name)(dcv, bcx, bcx, bcx, conv_w)
    return db, dc, dxc, dw


def _group_matrix():
    idx = jnp.arange(LANES) // HEAD_DIM
    return (idx[:, None] == idx[None, :]).astype(BF16)


def _group_sum(v, gmat):
    return _dot3(v, gmat)


def _gnorm_fwd(att, cv, gg, name):
    s, a_w = att.shape
    cw = cv.shape[1]
    d = a_w + cw
    tr = _tile(s, 512, 16)
    gmat = _group_matrix()

    def body(att_ref, cv_ref, gg_ref, gm_ref, yn_ref):
        gm = gm_ref[...]
        for c0 in range(0, d, LANES):
            y = att_ref[:, c0:c0 + LANES] if c0 < a_w else cv_ref[:, c0 - a_w:c0 - a_w + LANES]
            ms = _group_sum(y * y, gm) * (1.0 / HEAD_DIM)
            yn_ref[:, c0:c0 + LANES] = (y * lax.rsqrt(ms + EPS) * gg_ref[:, c0:c0 + LANES]).astype(BF16)

    return _pc(body, out_shape=_sds((s, d), BF16), grid=(s // tr,),
               in_specs=[pl.BlockSpec((tr, a_w), lambda i: (i, 0)), pl.BlockSpec((tr, cw), lambda i: (i, 0)),
                         _vec_spec(d), pl.BlockSpec((LANES, LANES), lambda i: (0, 0))],
               out_specs=pl.BlockSpec((tr, d), lambda i: (i, 0)),
               compiler_params=_params("parallel"), name=name)(att, cv, gg, gmat)


def _gnorm_bwd(dyn, att, cv, gg, name):
    s, a_w = att.shape
    cw = cv.shape[1]
    d = a_w + cw
    tr = _tile(s, 256, 16)
    gmat = _group_matrix()

    def body(dyn_ref, att_ref, cv_ref, gg_ref, gm_ref, datt_ref, dcv_ref, dgg_ref):
        @pl.when(pl.program_id(0) == 0)
        def _():
            dgg_ref[...] = jnp.zeros_like(dgg_ref)

        gm = gm_ref[...]
        for c0 in range(0, d, LANES):
            y = att_ref[:, c0:c0 + LANES] if c0 < a_w else cv_ref[:, c0 - a_w:c0 - a_w + LANES]
            dv = dyn_ref[:, c0:c0 + LANES]
            r = lax.rsqrt(_group_sum(y * y, gm) * (1.0 / HEAD_DIM) + EPS)
            xhat = y * r
            dgg_ref[:, c0:c0 + LANES] += jnp.sum(dv * xhat, axis=0, keepdims=True)
            dxh = dv * gg_ref[:, c0:c0 + LANES]
            proj = _group_sum(dxh * xhat, gm) * (1.0 / HEAD_DIM)
            dy = r * (dxh - xhat * proj)
            if c0 < a_w:
                datt_ref[:, c0:c0 + LANES] = dy.astype(BF16)
            else:
                dcv_ref[:, c0 - a_w:c0 - a_w + LANES] = dy

    return _pc(body, out_shape=(_sds((s, a_w), BF16), _sds((s, cw), F32), _sds((1, d), F32)), grid=(s // tr,),
               in_specs=[pl.BlockSpec((tr, d), lambda i: (i, 0)), pl.BlockSpec((tr, a_w), lambda i: (i, 0)),
                         pl.BlockSpec((tr, cw), lambda i: (i, 0)), _vec_spec(d),
                         pl.BlockSpec((LANES, LANES), lambda i: (0, 0))],
               out_specs=(pl.BlockSpec((tr, a_w), lambda i: (i, 0)), pl.BlockSpec((tr, cw), lambda i: (i, 0)),
                          _vec_spec(d)),
               compiler_params=_params("arbitrary"), name=name)(dyn, att, cv, gg, gmat)


def _adamw_math(w, g, m, v):
    m_new = ADAM_B1 * m + (1.0 - ADAM_B1) * g
    v_new = ADAM_B2 * v + (1.0 - ADAM_B2) * (g * g)
    m_hat = m_new / (1.0 - ADAM_B1 ** ADAM_STEP)
    v_hat = v_new / (1.0 - ADAM_B2 ** ADAM_STEP)
    delta = -ADAM_LR * (m_hat / (jnp.sqrt(v_hat) + ADAM_EPS) + ADAM_WD * w)
    return delta, m_new, v_new


def _row_tile(r, c):
    return _tile(r, max(8, ((1 << 18) // c) // 8 * 8), 8)


def _adamw(w, g, m, v, name):
    r, c = w.shape
    tr = _row_tile(r, c)

    def body(w_ref, g_ref, m_ref, v_ref, d_ref, mo_ref, vo_ref):
        d, mn, vn = _adamw_math(w_ref[...], g_ref[...], m_ref[...], v_ref[...])
        d_ref[...] = d
        mo_ref[...] = mn
        vo_ref[...] = vn

    spec = pl.BlockSpec((tr, c), lambda i: (i, 0))
    return _pc(body, out_shape=(_sds((r, c), F32),) * 3, grid=(r // tr,), in_specs=[spec] * 4,
               out_specs=(spec,) * 3, compiler_params=_params("parallel"), name=name)(w, g, m, v)


def _adamw_halves(w, mine, theirs, m, v, core, name):
    r2, c = w.shape
    r = r2 // 2
    assert mine.shape == (r, c) and theirs.shape == (r, c)
    tr = _row_tile(r, c)
    nb = r // tr

    def body(core_ref, w_ref, a_ref, b_ref, m_ref, v_ref, g_ref, d_ref, mo_ref, vo_ref):
        g = jnp.where(pl.program_id(0) == core_ref[0], a_ref[...], b_ref[...])
        d, mn, vn = _adamw_math(w_ref[...], g, m_ref[...], v_ref[...])
        g_ref[...] = g
        d_ref[...] = d
        mo_ref[...] = mn
        vo_ref[...] = vn

    full = pl.BlockSpec((tr, c), lambda h, i, core_ref: (h * nb + i, 0))
    half = pl.BlockSpec((tr, c), lambda h, i, core_ref: (i, 0))
    grid_spec = pltpu.PrefetchScalarGridSpec(
        num_scalar_prefetch=1, grid=(2, nb), in_specs=[full, half, half, full, full], out_specs=(full,) * 4)
    return _pc(body, out_shape=(_sds((r2, c), F32),) * 4, grid_spec=grid_spec,
               compiler_params=_params("parallel", "parallel"), name=name)(core, w, mine, theirs, m, v)


def _ada_fwd(c16, ada_w, ada_b, name):
    d, n = ada_w.shape
    tn = _tile(n, 768, LANES)

    def body(c_ref, w_ref, b_ref, o_ref):
        cv = c_ref[...]
        sc = (cv * jax.nn.sigmoid(cv)).astype(BF16)
        o_ref[...] = lax.dot_general(sc, w_ref[...].astype(BF16), _NN, preferred_element_type=F32) + b_ref[...]

    return _pc(body, out_shape=_sds((16, n), F32), grid=(n // tn,),
               in_specs=[pl.BlockSpec((16, d), lambda j: (0, 0)), pl.BlockSpec((d, tn), lambda j: (0, j)),
                         pl.BlockSpec((1, tn), lambda j: (0, j))],
               out_specs=pl.BlockSpec((16, tn), lambda j: (0, j)),
               compiler_params=_params("parallel"), name=name)(c16, ada_w, ada_b)


def _ada_update(c16_t, dmod16, w, m, v, name):
    r, c = w.shape
    tr = _row_tile(r, c)

    def body(c_ref, dm_ref, w_ref, m_ref, v_ref, g_ref, d_ref, mo_ref, vo_ref):
        cv = c_ref[...]
        sc = (cv * jax.nn.sigmoid(cv)).astype(BF16)
        g = lax.dot_general(sc, dm_ref[...].astype(BF16), _NN, preferred_element_type=F32)
        d, mn, vn = _adamw_math(w_ref[...], g, m_ref[...], v_ref[...])
        g_ref[...] = g
        d_ref[...] = d
        mo_ref[...] = mn
        vo_ref[...] = vn

    spec = pl.BlockSpec((tr, c), lambda i: (i, 0))
    return _pc(body, out_shape=(_sds((r, c), F32),) * 4, grid=(r // tr,),
               in_specs=[pl.BlockSpec((tr, 16), lambda i: (i, 0)), pl.BlockSpec((16, c), lambda i: (0, 0)),
                         spec, spec, spec],
               out_specs=(spec,) * 4, compiler_params=_params("parallel"), name=name)(c16_t, dmod16, w, m, v)


def _add_half(dw, recv, core, name):
    _, _, r, w = dw.shape
    tr = _tile(r, 256, 16)

    def body(core_ref, a_ref, b_ref, o_ref):
        o_ref[...] = (a_ref[...].astype(F32) + b_ref[...].astype(F32)).astype(BF16)

    grid_spec = pltpu.PrefetchScalarGridSpec(
        num_scalar_prefetch=1, grid=(N_CHIPS, r // tr),
        in_specs=[pl.BlockSpec((None, None, tr, w), lambda s, i, core_ref: (s, core_ref[0], i, 0)),
                  pl.BlockSpec((None, tr, w), lambda s, i, core_ref: (s, i, 0))],
        out_specs=pl.BlockSpec((None, tr, w), lambda s, i, core_ref: (s, i, 0)))
    return _pc(body, out_shape=_sds((N_CHIPS, r, w), BF16), grid_spec=grid_spec,
               compiler_params=_params("parallel", "parallel"), name=name)(core, dw, recv)


def _sum_chips(own, recv, chip, name):
    _, r, w = own.shape
    tr = _tile(r, 256, 16)

    def body(chip_ref, own_ref, p_ref, o_ref):
        acc = own_ref[...].astype(F32)
        for q in range(N_CHIPS - 1):
            acc = acc + p_ref[q].astype(F32)
        o_ref[...] = acc

    grid_spec = pltpu.PrefetchScalarGridSpec(
        num_scalar_prefetch=1, grid=(r // tr,),
        in_specs=[pl.BlockSpec((None, tr, w), lambda i, chip_ref: (chip_ref[0], i, 0)),
                  pl.BlockSpec((N_CHIPS - 1, tr, w), lambda i, chip_ref: (0, i, 0))],
        out_specs=pl.BlockSpec((tr, w), lambda i, chip_ref: (i, 0)))
    return _pc(body, out_shape=_sds((r, w), F32), grid_spec=grid_spec,
               compiler_params=_params("parallel"), name=name)(chip, own, recv)


def _sum_devices(parts, name):
    nd, r, w = parts.shape

    def body(p_ref, o_ref):
        acc = p_ref[0]
        for q in range(1, nd):
            acc = acc + p_ref[q]
        o_ref[...] = acc

    return _pc(body, out_shape=_sds((r, w), F32), name=name)(parts)


def _place():
    x, y, c = lax.axis_index("x"), lax.axis_index("y"), lax.axis_index("c")
    chips = [(1 - x, y), (x, 1 - y), (1 - x, 1 - y)]
    return x, y, c, chips


def _all_gather_small(blk, name):
    r, w = blk.shape

    def body(x_ref, out_ref, send_sems, recv_sems, local_sem):
        x, y, c, chips = _place()
        me, sibling = (x, y, c), (x, y, 1 - c)

        def rows(px, py, pc):
            return out_ref.at[pl.ds((4 * px + 2 * py + pc) * r, r), :]

        def copy(k, block, to, src=None):
            return pltpu.make_async_remote_copy(
                src_ref=rows(*block) if src is None else src, dst_ref=rows(*block),
                send_sem=send_sems.at[k], recv_sem=recv_sems.at[k], device_id=to, device_id_type=MESH)

        mine = pltpu.make_async_copy(x_ref, rows(*me), local_sem)
        mine.start()
        first = [copy(0, me, sibling, src=x_ref)]
        first += [copy(1 + j, me, (*chip, c), src=x_ref) for j, chip in enumerate(chips)]
        for cp in first:
            cp.start()
        passed = [copy(4 + j, (*chip, c), sibling) for j, chip in enumerate(chips)]
        for j, chip in enumerate(chips):
            copy(1 + j, (*chip, c), me).wait_recv()
            passed[j].start()
        copy(0, sibling, me).wait_recv()
        for j, chip in enumerate(chips):
            copy(4 + j, (*chip, 1 - c), me).wait_recv()
        for cp in first + passed:
            cp.wait_send()
        mine.wait()

    return _pc(body, out_shape=_sds((N_DEV * r, w), blk.dtype),
               in_specs=[pl.BlockSpec(memory_space=pltpu.VMEM)], out_specs=pl.BlockSpec(memory_space=pltpu.VMEM),
               scratch_shapes=[pltpu.SemaphoreType.DMA((7,)), pltpu.SemaphoreType.DMA((7,)), pltpu.SemaphoreType.DMA],
               name=name)(blk)


def _remote(src, dst, send_sems, recv_sems, k, to):
    return pltpu.make_async_remote_copy(src_ref=src, dst_ref=dst, send_sem=send_sems.at[k], recv_sem=recv_sems.at[k],
                                        device_id=to, device_id_type=MESH)


def _exchange_of(inputs, out_shapes, n_sems, copies, aliases=None):
    def start(src, dst, send_sems, recv_sems):
        for cp in copies(src, dst, send_sems, recv_sems)[0]:
            cp.start()

    def finish(src, dst, send_sems, recv_sems):
        sends, arrivals = copies(src, dst, send_sems, recv_sems)
        for cp in arrivals:
            cp.wait_recv()
        for cp in sends:
            cp.wait_send()

    return _Exchange(inputs, out_shapes, n_sems, start, finish, aliases)


def _run_exchange(ex, name):
    n_in, n_out = len(ex.inputs), len(ex.out_shapes)

    def body(*refs):
        src, dst = refs[:n_in], refs[n_in:n_in + n_out]
        send_sems, recv_sems = refs[n_in + n_out:]
        ex.start(src, dst, send_sems, recv_sems)
        ex.finish(src, dst, send_sems, recv_sems)

    ex.results = list(pl.pallas_call(
        body, out_shape=tuple(ex.out_shapes), in_specs=[_ANY] * n_in, out_specs=(_ANY,) * n_out,
        scratch_shapes=[pltpu.SemaphoreType.DMA((ex.n_sems,)), pltpu.SemaphoreType.DMA((ex.n_sems,))],
        input_output_aliases=ex.aliases, name=name)(*ex.inputs))
    return ex.results


def _gather_ici_exchange(shards):
    n = len(shards)

    def copies(own, out, send_sems, recv_sems):
        x, y, c, chips = _place()
        my_chip = 2 * x + y
        sends, arrivals = [], []
        for i in range(n):
            for j, chip in enumerate(chips):
                to = (*chip, c)
                sends.append(_remote(own[i].at[c], out[i].at[my_chip, c], send_sems, recv_sems, 4 * i + j, to))
                arrivals.append(_remote(own[i].at[c], out[i].at[2 * chip[0] + chip[1], c], send_sems, recv_sems, 4 * i + j, to))
            whole = _remote(own[i], out[i].at[my_chip], send_sems, recv_sems, 4 * i + 3, (x, y, 1 - c))
            sends.append(whole)
            arrivals.append(whole)
        return sends, arrivals

    return _exchange_of(shards, [_sds((N_CHIPS,) + s.shape, s.dtype) for s in shards], 4 * n, copies)


def _gather_pass_exchange(gathered):
    n = len(gathered)

    def copies(src, dst, send_sems, recv_sems):
        x, y, c, chips = _place()
        sends, arrivals = [], []
        for i in range(n):
            for j, chip in enumerate(chips):
                idx = 2 * chip[0] + chip[1]
                sends.append(_remote(src[i].at[idx, c], dst[i].at[idx, c], send_sems, recv_sems, 3 * i + j, (x, y, 1 - c)))
                arrivals.append(_remote(src[i].at[idx, c], dst[i].at[idx, 1 - c], send_sems, recv_sems, 3 * i + j, (x, y, 1 - c)))
        return sends, arrivals

    return _exchange_of(gathered, [_sds(g.shape, g.dtype) for g in gathered], 3 * n, copies,
                        aliases={i: i for i in range(n)})


def _reduce_sibling_exchange(grads):
    n = len(grads)

    def copies(src, dst, send_sems, recv_sems):
        x, y, c, _ = _place()
        both = [_remote(src[i].at[s, 1 - c], dst[i].at[s], send_sems, recv_sems, N_CHIPS * i + s, (x, y, 1 - c))
                for i in range(n) for s in range(N_CHIPS)]
        return both, both

    return _exchange_of(grads, [_sds((N_CHIPS,) + g.shape[2:], g.dtype) for g in grads], N_CHIPS * n, copies)


def _reduce_chips_exchange(parts):
    n = len(parts)

    def copies(src, dst, send_sems, recv_sems):
        x, y, c, chips = _place()
        both = [_remote(src[i].at[2 * chip[0] + chip[1]], dst[i].at[j], send_sems, recv_sems, 3 * i + j, (*chip, c))
                for i in range(n) for j, chip in enumerate(chips)]
        return both, both

    return _exchange_of(parts, [_sds((N_CHIPS - 1,) + p.shape[1:], p.dtype) for p in parts], 3 * n, copies)


def _share_exchange(halves):
    n = len(halves)

    def copies(src, dst, send_sems, recv_sems):
        x, y, c, _ = _place()
        both = [_remote(src[i], dst[i], send_sems, recv_sems, i, (x, y, 1 - c)) for i in range(n)]
        return both, both

    return _exchange_of(halves, [_sds(h.shape, h.dtype) for h in halves], n, copies)


HEAD_ROWS = 16


class _WeightTraffic:
    def __init__(self, shards, core, chip):
        self.shards, self.core, self.chip = shards, core, chip
        self.gather, self.grads, self.reduce, self.chip_sums, self.half_sums, self.shared = {}, {}, {}, {}, {}, {}

    def gather_ici(self, grp):
        self.gather[grp] = _gather_ici_exchange(self.shards[grp])
        return self.gather[grp]

    def gather_pass(self, grp):
        self.gather[grp] = _gather_pass_exchange(self.gather[grp].results)
        return self.gather[grp]

    def weights(self, grp):
        return [g.reshape(-1, g.shape[-1]) for g in self.gather[grp].results]

    def reduce_sibling(self, grp, grads):
        self.grads[grp] = [g.reshape(N_CHIPS, 2, g.shape[0] // (2 * N_CHIPS), g.shape[1]) for g in grads]
        self.reduce[grp] = _reduce_sibling_exchange(self.grads[grp])
        return self.reduce[grp]

    def add_halves(self, grp):
        self.chip_sums[grp] = [_add_half(g, r, self.core, "add_half_%s%d" % (grp, i))
                               for i, (g, r) in enumerate(zip(self.grads[grp], self.reduce[grp].results))]

    def reduce_chips(self, grp):
        self.reduce[grp] = _reduce_chips_exchange(self.chip_sums[grp])
        return self.reduce[grp]

    def sum_chips(self, grp):
        self.half_sums[grp] = [_sum_chips(o, p, self.chip, "sum_chips_%s%d" % (grp, i))
                               for i, (o, p) in enumerate(zip(self.chip_sums[grp], self.reduce[grp].results))]

    def share(self, grp):
        self.shared[grp] = _share_exchange(self.half_sums[grp])
        return self.shared[grp]

    def totals(self, grp):
        return list(zip(self.half_sums[grp], self.shared[grp].results))


def _ffn_fwd(x, norm_g, shift, scale, gate, wg_t, wu_t, wd, tag, up_exchange=None, down_exchange=None):
    h = _norm_mod_fwd(x, norm_g, shift, scale, tag + "_norm_fwd")
    a, u, hid = _ffn_up(h, wg_t, wu_t, tag + "_up", exchange=up_exchange)
    x_out, f = _mm(hid, wd, "nn", F32, tag + "_down", res=x, gate=gate, aux_dtype=BF16,
                   exchange=down_exchange() if down_exchange else None)
    return x_out, (h, a, u, hid, f)


def _ffn_bwd(dx_out, x, saved, norm_g, scale, gate, wg_t, wu_t, wd, tag, traffic, dact_exchange=None, dw_exchange=None,
             finish_reduction=False):
    h, a, u, hid, f = saved
    df, dgate = _gate_bwd(dx_out, f, gate, tag + "_gate_bwd")
    da, du = _ffn_dact(df, wd, a, u, tag + "_dact", exchange=dact_exchange)
    dwd = _mm(hid, df, "tn", BF16, tag + "_dwd", exchange=dw_exchange() if dw_exchange else None)
    dwg_t = _mm(da, h, "tn", BF16, tag + "_dwg")
    dwu_t = _mm(du, h, "tn", BF16, tag + "_dwu")
    dh = _mm(da, wg_t, "nn", F32, tag + "_dh_a", exchange=traffic.reduce_sibling(tag, [dwg_t, dwu_t, dwd]))
    traffic.add_halves(tag)
    dh = _mm(du, wu_t, "nn", F32, tag + "_dh_u", res=dh, exchange=traffic.reduce_chips(tag) if finish_reduction else None)
    if finish_reduction:
        traffic.sum_chips(tag)
    dx, dshift, dscale, dnorm_g = _norm_mod_bwd(dh, x, norm_g, scale, dx_out, tag + "_norm_bwd",
                                                exchange=traffic.share(tag) if finish_reduction else None)
    return dx, (dshift, dscale, dgate, dnorm_g)


def _layer_step(x, target, mod, gains, forget_bias, conv_w, traffic, att_w, in_shard, in_rows):
    sh1, sc1, g1, sh2, sc2, g2, sh3, sc3, g3 = mod
    norm1_g, norm2_g, norm3_g, final_g, group_g = gains
    s, d = x.shape
    n_heads = att_w // HEAD_DIM
    npair = n_heads // 2
    gate1, gate3 = 0.5 * g1, 0.5 * g3

    def split_w_in(w_in_pad):
        w_in_t = w_in_pad.reshape(N_CHIPS, in_rows, d)[:, :in_shard].reshape(N_CHIPS * in_shard, d)
        return (w_in_t[:3 * att_w], _pad_rows(w_in_t[3 * att_w:3 * att_w + n_heads], LANES), w_in_t[3 * att_w + n_heads:])

    _run_exchange(traffic.gather_ici("ffn1"), "gather_ffn1_ici")
    _run_exchange(traffic.gather_pass("ffn1"), "gather_ffn1_pass")
    wg1_t, wu1_t, wd1 = traffic.weights("ffn1")
    x1, saved1 = _ffn_fwd(x, norm1_g, sh1, sc1, gate1, wg1_t, wu1_t, wd1, "ffn1",
                          up_exchange=traffic.gather_ici("mix"), down_exchange=lambda: traffic.gather_pass("mix"))
    w_in_pad, w_out = traffic.weights("mix")
    wqkv_t, wf_t, wbcx_t = split_w_in(w_in_pad)

    h2 = _norm_mod_fwd(x1, norm2_g, sh2, sc2, "mix_norm_fwd")
    qkv = _mm(h2, wqkv_t, "nt", BF16, "mix_proj_qkv")
    bcx = _mm(h2, wbcx_t, "nt", F32, "mix_proj_bcx")
    flog = _mm(h2, wf_t, "nt", F32, "mix_proj_f")
    flog_t = jnp.pad(flog[:, :n_heads].T, ((0, HEAD_ROWS - n_heads), (0, 0)))
    bias_col = jnp.pad(forget_bias, (0, HEAD_ROWS - n_heads))[:, None]
    f_rows = _forget_fwd(flog_t, bias_col, "forget_fwd")
    frow = f_rows[:n_heads].reshape(npair, 2, s)
    fcol = frow.transpose(0, 2, 1)
    att, lse = _attn_fwd(qkv, fcol, frow, "attn_fwd", exchange=traffic.gather_ici("ffn2"))
    cv = _conv_fwd(bcx, conv_w, "conv_fwd")
    yn = _gnorm_fwd(att, cv, group_g, "gnorm_fwd")
    x2, mix = _mm(yn, w_out, "nn", F32, "mix_out", res=x1, gate=g2, aux_dtype=BF16, exchange=traffic.gather_pass("ffn2"))
    wg2_t, wu2_t, wd2 = traffic.weights("ffn2")

    x3, saved3 = _ffn_fwd(x2, norm3_g, sh3, sc3, gate3, wg2_t, wu2_t, wd2, "ffn2")

    dx3, loss_row, dfinal_g = _final_loss(x3, final_g, target, "final_loss")

    dx2, (dsh3, dsc3, dgate3, dnorm3_g) = _ffn_bwd(
        dx3, x2, saved3, norm3_g, sc3, gate3, wg2_t, wu2_t, wd2, "ffn2", traffic)

    dmix, dg2 = _gate_bwd(dx2, mix, g2, "mix_gate_bwd")
    dyn = _mm(dmix, w_out, "nt", F32, "mix_out_dyn")
    dw_out = _mm(yn, dmix, "tn", BF16, "mix_out_dw")
    datt, dcv, dgroup_g = _gnorm_bwd(dyn, att, cv, group_g, "gnorm_bwd")
    db, dc, dxc, dconv_w = _conv_bwd(dcv, bcx, conv_w, "conv_bwd")
    dbcx = jnp.concatenate([db, dc, dxc], axis=1)
    dq, dk, dv, df_key, df_query = _attn_bwd(qkv, datt, att, lse, fcol, frow, "attn_bwd",
                                             exchange=traffic.reduce_chips("ffn2"))
    traffic.sum_chips("ffn2")
    dqkv = jnp.concatenate([dq.astype(BF16), dk, dv], axis=1)
    df_heads = df_key.reshape(n_heads, s) + df_query[:, ::HEAD_DIM].T
    df_t = jnp.pad(df_heads, ((0, HEAD_ROWS - n_heads), (0, 0)))
    dflog_t, dbias_col = _forget_bwd(df_t, flog_t, bias_col, "forget_bwd")
    dflog = jnp.pad(dflog_t[:n_heads].T, ((0, 0), (0, LANES - n_heads))).astype(BF16)
    dh2 = _mm(dqkv, wqkv_t, "nn", F32, "mix_dh_qkv", exchange=traffic.share("ffn2"))
    dh2 = _mm(dbcx, wbcx_t, "nn", F32, "mix_dh_bcx", res=dh2)
    dh2 = _mm(dflog, wf_t, "nn", F32, "mix_dh_f", res=dh2)
    dwqkv_t = _mm(dqkv, h2, "tn", BF16, "mix_dw_qkv")
    dwbcx_t = _mm(dbcx, h2, "tn", BF16, "mix_dw_bcx")
    dwf_t = _mm(dflog, h2, "tn", BF16, "mix_dw_f")
    dw_in_t = jnp.concatenate([dwqkv_t, dwf_t[:n_heads], dwbcx_t], axis=0).reshape(N_CHIPS, in_shard, d)
    dw_in_t = jnp.pad(dw_in_t, ((0, 0), (0, in_rows - in_shard), (0, 0))).reshape(N_CHIPS * in_rows, d)
    dx1, dsh2, dsc2, dnorm2_g = _norm_mod_bwd(dh2, x1, norm2_g, sc2, dx2, "mix_norm_bwd",
                                              exchange=traffic.reduce_sibling("mix", [dw_in_t, dw_out]))
    traffic.add_halves("mix")

    def share_mix():
        traffic.sum_chips("mix")
        return traffic.share("mix")

    dx, (dsh1, dsc1, dgate1, dnorm1_g) = _ffn_bwd(
        dx1, x, saved1, norm1_g, sc1, gate1, wg1_t, wu1_t, wd1, "ffn1", traffic,
        dact_exchange=traffic.reduce_chips("mix"), dw_exchange=share_mix, finish_reduction=True)

    dmod = [dsh1, dsc1, 0.5 * dgate1, dsh2, dsc2, dg2, dsh3, dsc3, 0.5 * dgate3]
    dgains = [dnorm1_g, dnorm2_g, dnorm3_g, dfinal_g, dgroup_g]
    dbias = dbias_col[:n_heads, 0]
    return dx, loss_row, dmod, dgains, dbias, dconv_w


SMALL_ROWS = 24
ROW_GAINS, ROW_LOSS, ROW_FORGET, ROW_CONV, ROW_MOD = 0, 5, 6, 7, 10
PROW_ADA_B, PROW_GAINS, PROW_FORGET, PROW_CONV = 0, 9, 14, 15


def _round_up(n, m):
    return -(-n // m) * m


def _pad_rows(a, rows):
    return jnp.pad(a, ((0, rows - a.shape[0]), (0, 0)))


def _halves(a):
    return a.reshape(2, a.shape[0] // 2, a.shape[1])


def _rows_at(a, r0, total, width):
    return jnp.pad(a, ((r0, total - r0 - a.shape[0]), (0, width - a.shape[1])))


def kernel(x, c, ada_w, ada_b, norm1_g, ffn1_w_gate, ffn1_w_up, ffn1_w_down, norm2_g, w_in, forget_bias, conv_w, group_norm_g, w_out, norm3_g, ffn2_w_gate, ffn2_w_up, ffn2_w_down, final_g, loss_target, m_ada_w, m_ada_b, m_norm1_g, m_ffn1_w_gate, m_ffn1_w_up, m_ffn1_w_down, m_norm2_g, m_w_in, m_forget_bias, m_conv_w, m_group_norm_g, m_w_out, m_norm3_g, m_ffn2_w_gate, m_ffn2_w_up, m_ffn2_w_down, m_final_g, v_ada_w, v_ada_b, v_norm1_g, v_ffn1_w_gate, v_ffn1_w_up, v_ffn1_w_down, v_norm2_g, v_w_in, v_forget_bias, v_conv_w, v_group_norm_g, v_w_out, v_norm3_g, v_ffn2_w_gate, v_ffn2_w_up, v_ffn2_w_down, v_final_g):
    xi, yi, ci = lax.axis_index("x"), lax.axis_index("y"), lax.axis_index("c")
    chip = 2 * xi + yi
    dev = 4 * xi + 2 * yi + ci
    _, s, d = x.shape
    att_w = d // 2
    conv_width = d - att_w
    n_heads = att_w // HEAD_DIM
    in_shard = w_in.shape[1]
    in_rows = _round_up(in_shard, 32)
    cs = conv_w.shape[1]
    mod_shard = ada_w.shape[1]
    assert N_MOD * d == N_CHIPS * mod_shard and conv_width == N_CHIPS * cs and n_heads % 2 == 0

    pack0 = _rows_at(c, 0, 8, d) + _rows_at(conv_w, 1, 8, d)
    got0 = _all_gather_small(pack0, "gather_cond").reshape(N_DEV, 8, d)
    c16 = _pad_rows(got0[:, 0, :], 16)
    conv_full = got0[0::2, 1:1 + CONV_K, :cs].transpose(1, 0, 2).reshape(CONV_K, conv_width)

    ada_b_mine = lax.dynamic_slice(ada_b, (chip * mod_shard,), (mod_shard,))[None, :]
    mod_part = _ada_fwd(c16, ada_w, ada_b_mine, "ada_fwd")
    got1 = _all_gather_small(mod_part, "gather_mod").reshape(N_DEV, 16, mod_shard)
    mod_mine = lax.dynamic_index_in_dim(got1[0::2], dev, axis=1, keepdims=False).reshape(N_MOD, d)
    mod = [mod_mine[i:i + 1] for i in range(N_MOD)]

    def t_bf(w):
        return w.T.astype(BF16)

    shards = {"ffn1": [_halves(t_bf(ffn1_w_gate)), _halves(t_bf(ffn1_w_up)), _halves(ffn1_w_down.astype(BF16))],
              "mix": [_halves(_pad_rows(t_bf(w_in), in_rows)), _halves(w_out.astype(BF16))],
              "ffn2": [_halves(t_bf(ffn2_w_gate)), _halves(t_bf(ffn2_w_up)), _halves(ffn2_w_down.astype(BF16))]}
    core = ci.astype(jnp.int32).reshape(1)
    chip_arr = chip.astype(jnp.int32).reshape(1)
    traffic = _WeightTraffic(shards, core, chip_arr)

    gains = [g[None, :] for g in (norm1_g, norm2_g, norm3_g, final_g, group_norm_g)]
    dx, loss_row, dmod, dgains, dbias, dconv_w = _layer_step(
        x[0], loss_target[0], mod, gains, forget_bias, conv_full, traffic, att_w, in_shard, in_rows)

    pack = sum(_rows_at(g, ROW_GAINS + i, SMALL_ROWS, d) for i, g in enumerate(dgains))
    pack += _rows_at(loss_row, ROW_LOSS, SMALL_ROWS, d) + _rows_at(dbias[None, :], ROW_FORGET, SMALL_ROWS, d)
    pack += _rows_at(dconv_w, ROW_CONV, SMALL_ROWS, d)
    pack += sum(_rows_at(g, ROW_MOD + i, SMALL_ROWS, d) for i, g in enumerate(dmod))
    got2 = _all_gather_small(pack, "gather_small_grads").reshape(N_DEV, SMALL_ROWS, d)
    tot = _sum_devices(got2, "sum_small_grads")
    loss = tot[ROW_LOSS, 0]
    grad_ada_b = tot[ROW_MOD:ROW_MOD + N_MOD].reshape(N_MOD * d)
    grad_conv = lax.dynamic_slice(tot[ROW_CONV:ROW_CONV + CONV_K], (0, chip * cs), (CONV_K, cs))
    dmod_all = got2[:, ROW_MOD:ROW_MOD + N_MOD, :].reshape(N_DEV, N_MOD * d)
    dmod16 = _pad_rows(lax.dynamic_slice(dmod_all, (0, chip * mod_shard), (N_DEV, mod_shard)), 16)

    totals = traffic.totals("ffn1") + traffic.totals("mix") + traffic.totals("ffn2")

    names = ("ffn1_w_gate", "ffn1_w_up", "ffn1_w_down", "w_in", "w_out", "ffn2_w_gate", "ffn2_w_up", "ffn2_w_down")
    transposed = ("ffn1_w_gate", "ffn1_w_up", "w_in", "ffn2_w_gate", "ffn2_w_up")
    params = {"ffn1_w_gate": (ffn1_w_gate, m_ffn1_w_gate, v_ffn1_w_gate), "ffn1_w_up": (ffn1_w_up, m_ffn1_w_up, v_ffn1_w_up),
              "ffn1_w_down": (ffn1_w_down, m_ffn1_w_down, v_ffn1_w_down), "w_in": (w_in, m_w_in, v_w_in),
              "w_out": (w_out, m_w_out, v_w_out), "ffn2_w_gate": (ffn2_w_gate, m_ffn2_w_gate, v_ffn2_w_gate),
              "ffn2_w_up": (ffn2_w_up, m_ffn2_w_up, v_ffn2_w_up), "ffn2_w_down": (ffn2_w_down, m_ffn2_w_down, v_ffn2_w_down)}
    out = {}
    for name_, (mine, theirs) in zip(names, totals):
        w, m, v = params[name_]
        if name_ in transposed:
            w, m, v = w.T, m.T, v.T
        if name_ == "w_in":
            both = jnp.where(ci == 0, jnp.concatenate([mine, theirs]), jnp.concatenate([theirs, mine]))[:in_shard]
            res = (both,) + tuple(_adamw(w, both, m, v, "adamw_" + name_))
        else:
            res = _adamw_halves(w, mine, theirs, m, v, core, "adamw_" + name_)
        out[name_] = tuple(r.T for r in res) if name_ in transposed else tuple(res)
    c16_t = c16.T
    out["ada_w"] = tuple(_ada_update(c16_t, dmod16, ada_w, m_ada_w, v_ada_w, "adamw_ada_w"))

    def small_pack(ada_b_, gains_, forget_, conv_):
        p = _rows_at(ada_b_.reshape(N_MOD, d), PROW_ADA_B, SMALL_ROWS, d)
        p += sum(_rows_at(g[None, :], PROW_GAINS + i, SMALL_ROWS, d) for i, g in enumerate(gains_))
        p += _rows_at(forget_[None, :], PROW_FORGET, SMALL_ROWS, d) + _rows_at(conv_, PROW_CONV, SMALL_ROWS, d)
        return p

    g_gains = [tot[ROW_GAINS + i] for i in range(5)]
    g_forget = tot[ROW_FORGET, :n_heads]
    sw = small_pack(ada_b, (norm1_g, norm2_g, norm3_g, final_g, group_norm_g), forget_bias, conv_w)
    sm = small_pack(m_ada_b, (m_norm1_g, m_norm2_g, m_norm3_g, m_final_g, m_group_norm_g), m_forget_bias, m_conv_w)
    sv = small_pack(v_ada_b, (v_norm1_g, v_norm2_g, v_norm3_g, v_final_g, v_group_norm_g), v_forget_bias, v_conv_w)
    sg = small_pack(grad_ada_b, g_gains, g_forget, grad_conv)
    small = (sg,) + tuple(_adamw(sw, sg, sm, sv, "adamw_small"))

    def unpack(p):
        r = {"ada_b": p[PROW_ADA_B:PROW_ADA_B + N_MOD].reshape(N_MOD * d), "forget_bias": p[PROW_FORGET, :n_heads],
             "conv_w": p[PROW_CONV:PROW_CONV + CONV_K, :cs]}
        for i, nm in enumerate(("norm1_g", "norm2_g", "norm3_g", "final_g", "group_norm_g")):
            r[nm] = p[PROW_GAINS + i]
        return r

    small = [unpack(p) for p in small]
    order = ("ada_w", "ada_b", "norm1_g", "ffn1_w_gate", "ffn1_w_up", "ffn1_w_down", "norm2_g", "w_in", "forget_bias",
             "conv_w", "group_norm_g", "w_out", "norm3_g", "ffn2_w_gate", "ffn2_w_up", "ffn2_w_down", "final_g")
    result = [loss, dx[None]]
    for k in range(4):
        result += [out[nm][k] if nm in out else small[k][nm] for nm in order]
    return tuple(result)
```

```python
import functools
import math

import jax
import jax.numpy as jnp
from jax import lax
from jax.experimental import pallas as pl
from jax.experimental.pallas import tpu as pltpu

F32 = jnp.float32
BF16 = jnp.bfloat16

HEAD_DIM = 64
CONV_K = 3
N_MOD = 9
EPS = 1e-6
ADAM_LR = 0.001
ADAM_B1 = 0.9
ADAM_B2 = 0.999
ADAM_EPS = 1e-08
ADAM_WD = 0.01
ADAM_STEP = 10

LANES = 128
N_CHIPS = 4
N_DEV = 8
VMEM_LIMIT_BYTES = 56 * 1024 * 1024
NEG_BIG = -1e30
MESH = pl.DeviceIdType.MESH

_NT = (((1,), (1,)), ((), ()))
_NN = (((1,), (0,)), ((), ()))
_TN = (((0,), (0,)), ((), ()))


def _params(*sem):
    return pltpu.CompilerParams(dimension_semantics=sem, vmem_limit_bytes=VMEM_LIMIT_BYTES)


class _Exchange:
    def __init__(self, inputs, out_shapes, n_sems, start, finish, aliases=None):
        self.inputs, self.out_shapes, self.n_sems = list(inputs), list(out_shapes), n_sems
        self.start, self.finish, self.aliases = start, finish, dict(aliases or {})
        self.results = None


def _pc(body, exchange=None, **kw):
    if exchange is None:
        return pl.pallas_call(body, **kw)
    grid = kw["grid"]
    single = not isinstance(kw["out_shape"], (tuple, list))
    out_shape = [kw["out_shape"]] if single else list(kw["out_shape"])
    out_specs = [kw["out_specs"]] if single else list(kw["out_specs"])
    in_specs = list(kw["in_specs"])
    scratch = list(kw.get("scratch_shapes", ()))
    n_in, n_out, n_scr = len(in_specs), len(out_shape), len(scratch)
    n_xi, n_xo = len(exchange.inputs), len(exchange.out_shapes)

    def wrapped(*refs):
        pos = [n_in, n_in + n_xi, n_in + n_xi + n_out, n_in + n_xi + n_out + n_xo]
        ins, x_in, outs, x_out = refs[:pos[0]], refs[pos[0]:pos[1]], refs[pos[1]:pos[2]], refs[pos[2]:pos[3]]
        scr = refs[pos[3]:pos[3] + n_scr]
        send_sems, recv_sems = refs[pos[3] + n_scr:]
        ids = [pl.program_id(a) for a in range(len(grid))]
        first = functools.reduce(jnp.logical_and, [i == 0 for i in ids])
        last = functools.reduce(jnp.logical_and, [i == g - 1 for i, g in zip(ids, grid)])

        @pl.when(first)
        def _():
            exchange.start(x_in, x_out, send_sems, recv_sems)

        body(*ins, *outs, *scr)

        @pl.when(last)
        def _():
            exchange.finish(x_in, x_out, send_sems, recv_sems)

    call = pl.pallas_call(
        wrapped, out_shape=tuple(out_shape) + tuple(exchange.out_shapes), grid=grid,
        in_specs=in_specs + [_ANY] * n_xi, out_specs=tuple(out_specs) + (_ANY,) * n_xo,
        scratch_shapes=scratch + [pltpu.SemaphoreType.DMA((exchange.n_sems,)), pltpu.SemaphoreType.DMA((exchange.n_sems,))],
        input_output_aliases={n_in + a: n_out + b for a, b in exchange.aliases.items()},
        compiler_params=_params(*(["arbitrary"] * len(grid))), name=kw["name"])

    def run(*args):
        res = call(*args, *exchange.inputs)
        exchange.results = list(res[n_out:])
        return res[0] if single else tuple(res[:n_out])

    return run


_ANY = pl.BlockSpec(memory_space=pl.ANY)


def _tile(n, pref, mult):
    best = None
    t = mult
    while t <= min(n, pref):
        if n % t == 0:
            best = t
        t += mult
    return n if best is None else best


def _sds(shape, dtype):
    return jax.ShapeDtypeStruct(shape, dtype)


def _vec_spec(d):
    return pl.BlockSpec((1, d), lambda *_: (0, 0))


def _norm_mod_fwd(x, g, shift, scale, name):
    s, d = x.shape
    tr = _tile(s, 512, 16)

    def body(x_ref, g_ref, sh_ref, sc_ref, h_ref):
        xv = x_ref[...]
        rstd = lax.rsqrt(jnp.mean(xv * xv, axis=-1, keepdims=True) + EPS)
        n = xv * rstd * g_ref[...]
        h_ref[...] = (n * (1.0 + sc_ref[...]) + sh_ref[...]).astype(BF16)

    row = pl.BlockSpec((tr, d), lambda i: (i, 0))
    return _pc(body, out_shape=_sds((s, d), BF16), grid=(s // tr,),
               in_specs=[row, _vec_spec(d), _vec_spec(d), _vec_spec(d)], out_specs=row,
               compiler_params=_params("parallel"), name=name)(x, g, shift, scale)


def _norm_mod_bwd(dh, x, g, scale, dres, name, exchange=None):
    s, d = x.shape
    tr = _tile(s, 256, 8)

    def body(dh_ref, x_ref, g_ref, sc_ref, dres_ref, dx_ref, dsh_ref, dsc_ref, dg_ref):
        @pl.when(pl.program_id(0) == 0)
        def _():
            dsh_ref[...] = jnp.zeros_like(dsh_ref)
            dsc_ref[...] = jnp.zeros_like(dsc_ref)
            dg_ref[...] = jnp.zeros_like(dg_ref)

        xv = x_ref[...]
        dhv = dh_ref[...]
        gv = g_ref[...]
        rstd = lax.rsqrt(jnp.mean(xv * xv, axis=-1, keepdims=True) + EPS)
        xhat = xv * rstd
        dn = dhv * (1.0 + sc_ref[...])
        dsh_ref[...] += jnp.sum(dhv, axis=0, keepdims=True)
        dsc_ref[...] += jnp.sum(dhv * (xhat * gv), axis=0, keepdims=True)
        dg_ref[...] += jnp.sum(dn * xhat, axis=0, keepdims=True)
        dxh = dn * gv
        proj = jnp.mean(dxh * xhat, axis=-1, keepdims=True)
        dx_ref[...] = dres_ref[...] + rstd * (dxh - xhat * proj)

    row = pl.BlockSpec((tr, d), lambda i: (i, 0))
    vec = _vec_spec(d)
    return _pc(body, exchange, out_shape=(_sds((s, d), F32), _sds((1, d), F32), _sds((1, d), F32), _sds((1, d), F32)),
               grid=(s // tr,), in_specs=[row, row, vec, vec, row], out_specs=(row, vec, vec, vec),
               compiler_params=_params("arbitrary"), name=name)(dh, x, g, scale, dres)


def _gate_bwd(dx, f, gate, name):
    s, d = dx.shape
    tr = _tile(s, 512, 16)

    def body(dx_ref, f_ref, gate_ref, df_ref, dg_ref):
        @pl.when(pl.program_id(0) == 0)
        def _():
            dg_ref[...] = jnp.zeros_like(dg_ref)

        dxv = dx_ref[...]
        df_ref[...] = (dxv * gate_ref[...]).astype(BF16)
        dg_ref[...] += jnp.sum(dxv * f_ref[...].astype(F32), axis=0, keepdims=True)

    row = pl.BlockSpec((tr, d), lambda i: (i, 0))
    vec = _vec_spec(d)
    return _pc(body, out_shape=(_sds((s, d), BF16), _sds((1, d), F32)), grid=(s // tr,),
               in_specs=[row, row, vec], out_specs=(row, vec),
               compiler_params=_params("arbitrary"), name=name)(dx, f, gate)


def _final_loss(x, g, target, name):
    s, d = x.shape
    tr = _tile(s, 256, 8)
    nsteps = s // tr

    def body(x_ref, g_ref, t_ref, dx_ref, loss_ref, dg_ref):
        i = pl.program_id(0)

        @pl.when(i == 0)
        def _():
            loss_ref[...] = jnp.zeros_like(loss_ref)
            dg_ref[...] = jnp.zeros_like(dg_ref)

        xv = x_ref[...]
        gv = g_ref[...]
        rstd = lax.rsqrt(jnp.mean(xv * xv, axis=-1, keepdims=True) + EPS)
        xhat = xv * rstd
        err = xhat * gv - t_ref[...]
        dy = err * (1.0 / d)
        loss_ref[...] += jnp.sum(0.5 * err * dy, axis=0, keepdims=True)
        dg_ref[...] += jnp.sum(dy * xhat, axis=0, keepdims=True)
        dxh = dy * gv
        proj = jnp.mean(dxh * xhat, axis=-1, keepdims=True)
        dx_ref[...] = rstd * (dxh - xhat * proj)

        @pl.when(i == nsteps - 1)
        def _():
            loss_ref[...] = jnp.broadcast_to(jnp.sum(loss_ref[...], axis=-1, keepdims=True), loss_ref.shape)

    row = pl.BlockSpec((tr, d), lambda i: (i, 0))
    vec = _vec_spec(d)
    return _pc(body, out_shape=(_sds((s, d), F32), _sds((1, d), F32), _sds((1, d), F32)), grid=(nsteps,),
               in_specs=[row, vec, row], out_specs=(row, vec, vec),
               compiler_params=_params("arbitrary"), name=name)(x, g, target)


def _mm(lhs, rhs, dims, out_dtype, name, res=None, gate=None, aux_dtype=None, exchange=None):
    if dims == "nn":
        (m, k), (k2, n) = lhs.shape, rhs.shape
    elif dims == "nt":
        (m, k), (n, k2) = lhs.shape, rhs.shape
    else:
        (k, m), (k2, n) = lhs.shape, rhs.shape
    assert k == k2, (lhs.shape, rhs.shape, dims)
    tn = _tile(n, 1024, LANES)
    tm = _tile(m, 512, LANES if dims == "tn" else 16)
    tk = _tile(k, 4096, LANES)
    nk = k // tk
    dn = {"nn": _NN, "nt": _NT, "tn": _TN}[dims]
    lhs_spec = (pl.BlockSpec((tk, tm), lambda i, j, kk: (kk, i)) if dims == "tn"
                else pl.BlockSpec((tm, tk), lambda i, j, kk: (i, kk)))
    rhs_spec = (pl.BlockSpec((tn, tk), lambda i, j, kk: (j, kk)) if dims == "nt"
                else pl.BlockSpec((tk, tn), lambda i, j, kk: (kk, j)))
    out_spec = pl.BlockSpec((tm, tn), lambda i, j, kk: (i, j))
    has_res, has_gate, has_aux = res is not None, gate is not None, aux_dtype is not None

    def body(*refs):
        refs = list(refs)
        l_ref, r_ref = refs[0], refs[1]
        pos = 2
        res_ref = gate_ref = aux_ref = None
        if has_res:
            res_ref = refs[pos]; pos += 1
        if has_gate:
            gate_ref = refs[pos]; pos += 1
        out_ref = refs[pos]; pos += 1
        if has_aux:
            aux_ref = refs[pos]; pos += 1
        acc_ref = refs[pos]
        kk = pl.program_id(2)
        part = lax.dot_general(l_ref[...], r_ref[...], dn, preferred_element_type=F32)

        @pl.when(kk == 0)
        def _():
            acc_ref[...] = part

        @pl.when(kk > 0)
        def _():
            acc_ref[...] += part

        @pl.when(kk == nk - 1)
        def _():
            acc = acc_ref[...]
            if has_aux:
                aux_ref[...] = acc.astype(aux_dtype)
            if has_gate:
                acc = acc * gate_ref[...]
            if has_res:
                acc = res_ref[...] + acc
            out_ref[...] = acc.astype(out_dtype)

    in_specs = [lhs_spec, rhs_spec]
    args = [lhs, rhs]
    if has_res:
        in_specs.append(out_spec); args.append(res)
    if has_gate:
        in_specs.append(pl.BlockSpec((1, tn), lambda i, j, kk: (0, j))); args.append(gate)
    out_shape = [_sds((m, n), out_dtype)]
    out_specs = [out_spec]
    if has_aux:
        out_shape.append(_sds((m, n), aux_dtype)); out_specs.append(out_spec)
    outs = _pc(body, exchange, out_shape=tuple(out_shape), grid=(m // tm, n // tn, nk), in_specs=in_specs,
               out_specs=tuple(out_specs), scratch_shapes=[pltpu.VMEM((tm, tn), F32)],
               compiler_params=_params("parallel", "parallel", "arbitrary"), name=name)(*args)
    return outs if has_aux else outs[0]


def _ffn_up(h, wg_t, wu_t, name, exchange=None):
    s, d = h.shape
    f = wg_t.shape[0]
    tm = _tile(s, 1024, 16)
    tn = _tile(f, 256, LANES)

    def body(h_ref, wg_ref, wu_ref, a_ref, u_ref, hid_ref):
        hv = h_ref[...]
        a = lax.dot_general(hv, wg_ref[...], _NT, preferred_element_type=F32)
        u = lax.dot_general(hv, wu_ref[...], _NT, preferred_element_type=F32)
        a_ref[...] = a.astype(BF16)
        u_ref[...] = u.astype(BF16)
        hid_ref[...] = (a * jax.nn.sigmoid(a) * u).astype(BF16)

    hs = pl.BlockSpec((tm, d), lambda i, j: (i, 0))
    ws = pl.BlockSpec((tn, d), lambda i, j: (j, 0))
    os_ = pl.BlockSpec((tm, tn), lambda i, j: (i, j))
    return _pc(body, exchange, out_shape=(_sds((s, f), BF16),) * 3, grid=(s // tm, f // tn),
               in_specs=[hs, ws, ws], out_specs=(os_, os_, os_),
               compiler_params=_params("parallel", "parallel"), name=name)(h, wg_t, wu_t)


def _ffn_dact(df, wd, a, u, name, exchange=None):
    s, d = df.shape
    f = wd.shape[0]
    tm = _tile(s, 1024, 16)
    tn = _tile(f, 256, LANES)

    def body(df_ref, wd_ref, a_ref, u_ref, da_ref, du_ref):
        dhid = lax.dot_general(df_ref[...], wd_ref[...], _NT, preferred_element_type=F32)
        av = a_ref[...].astype(F32)
        uv = u_ref[...].astype(F32)
        sig = jax.nn.sigmoid(av)
        da_ref[...] = (dhid * uv * (sig * (1.0 + av * (1.0 - sig)))).astype(BF16)
        du_ref[...] = (dhid * (av * sig)).astype(BF16)

    ds_ = pl.BlockSpec((tm, d), lambda i, j: (i, 0))
    ws = pl.BlockSpec((tn, d), lambda i, j: (j, 0))
    os_ = pl.BlockSpec((tm, tn), lambda i, j: (i, j))
    return _pc(body, exchange, out_shape=(_sds((s, f), BF16),) * 2, grid=(s // tm, f // tn),
               in_specs=[ds_, ws, os_, os_], out_specs=(os_, os_),
               compiler_params=_params("parallel", "parallel"), name=name)(df, wd, a, u)


def _split3(v):
    hi = v.astype(BF16)
    r1 = v - hi.astype(F32)
    mid = r1.astype(BF16)
    lo = (r1 - mid.astype(F32)).astype(BF16)
    return hi, mid, lo


def _dot3(v, mat):
    hi, mid, lo = _split3(v)
    out = lax.dot_general(hi, mat, _NN, preferred_element_type=F32)
    out += lax.dot_general(mid, mat, _NN, preferred_element_type=F32)
    out += lax.dot_general(lo, mat, _NN, preferred_element_type=F32)
    return out


def _forget_fwd(flog_t, bias, name):
    h, s = flog_t.shape
    blk = _tile(s, 512, LANES)
    tri = (jnp.arange(blk)[:, None] <= jnp.arange(blk)[None, :]).astype(BF16)

    def body(z_ref, b_ref, tri_ref, f_ref, carry):
        @pl.when(pl.program_id(0) == 0)
        def _():
            carry[...] = jnp.zeros_like(carry)

        z = z_ref[...] + b_ref[...]
        e = jnp.exp(-jnp.abs(z))
        w = 1.0 + e
        log1p_e = jnp.where(w == 1.0, e, jnp.log(w) * (e / (w - 1.0)))
        lf = jnp.minimum(z, 0.0) - log1p_e
        out = carry[...] + _dot3(lf, tri_ref[...])
        f_ref[...] = out
        carry[...] = out[:, blk - 1:blk]

    zs = pl.BlockSpec((h, blk), lambda i: (0, i))
    return _pc(body, out_shape=_sds((h, s), F32), grid=(s // blk,),
               in_specs=[zs, pl.BlockSpec((h, 1), lambda i: (0, 0)), pl.BlockSpec((blk, blk), lambda i: (0, 0))],
               out_specs=zs, scratch_shapes=[pltpu.VMEM((h, 1), F32)],
               compiler_params=_params("arbitrary"), name=name)(flog_t, bias, tri)


def _forget_bwd(df_t, flog_t, bias, name):
    h, s = flog_t.shape
    blk = _tile(s, 512, LANES)
    nb = s // blk
    tri = (jnp.arange(blk)[:, None] >= jnp.arange(blk)[None, :]).astype(BF16)

    def body(df_ref, z_ref, b_ref, tri_ref, dz_ref, db_ref, carry):
        @pl.when(pl.program_id(0) == 0)
        def _():
            carry[...] = jnp.zeros_like(carry)
            db_ref[...] = jnp.zeros_like(db_ref)

        rc = carry[...] + _dot3(df_ref[...], tri_ref[...])
        carry[...] = rc[:, 0:1]
        dz = rc * jax.nn.sigmoid(-(z_ref[...] + b_ref[...]))
        dz_ref[...] = dz
        db_ref[...] += jnp.sum(dz, axis=-1, keepdims=True)

    rev = pl.BlockSpec((h, blk), lambda i: (0, nb - 1 - i))
    col = pl.BlockSpec((h, 1), lambda i: (0, 0))
    return _pc(body, out_shape=(_sds((h, s), F32), _sds((h, 1), F32)), grid=(nb,),
               in_specs=[rev, rev, col, pl.BlockSpec((blk, blk), lambda i: (0, 0))],
               out_specs=(rev, col), scratch_shapes=[pltpu.VMEM((h, 1), F32)],
               compiler_params=_params("arbitrary"), name=name)(df_t, flog_t, bias, tri)


def _attn_tiles(s):
    return _tile(s, 512, LANES)


BIAS_LANES = 6


def _attn_prep(qkv, fcol, name):
    s = qkv.shape[0]
    a_w = qkv.shape[1] // 3
    npair = a_w // LANES
    t = _attn_tiles(s)
    scale = 1.0 / math.sqrt(HEAD_DIM)

    def body(q_ref, k_ref, v_ref, fc_ref, qa_ref, ka_ref, va_ref):
        lane = lax.broadcasted_iota(jnp.int32, (1, LANES), 1)
        q2 = (q_ref[...].astype(F32) * scale).astype(BF16)
        k2, v2 = k_ref[...], v_ref[...]
        fc = fc_ref[0]
        one = jnp.ones((1, 1), BF16)
        zero = jnp.zeros((1, 1), BF16)
        for hh in range(2):
            real = (lane < HEAD_DIM) if hh == 0 else (lane >= HEAD_DIM)
            b0 = HEAD_DIM if hh == 0 else 0
            hi, mid, lo = _split3(fc[:, hh:hh + 1])
            qx = jnp.where(lane == b0, hi, jnp.where(lane == b0 + 1, mid, jnp.where(lane == b0 + 2, lo, one)))
            kx = jnp.where(lane == b0 + 3, -hi, jnp.where(lane == b0 + 4, -mid, jnp.where(lane == b0 + 5, -lo, one)))
            bias = jnp.logical_and(lane >= b0, lane < b0 + BIAS_LANES)
            cols = slice(hh * LANES, (hh + 1) * LANES)
            qa_ref[:, cols] = jnp.where(real, q2, jnp.where(bias, qx, zero))
            ka_ref[:, cols] = jnp.where(real, k2, jnp.where(bias, kx, zero))
            va_ref[:, cols] = jnp.where(real, v2, zero)

    def col(off):
        return pl.BlockSpec((t, LANES), lambda p, i: (i, off + p))

    out = pl.BlockSpec((t, 2 * LANES), lambda p, i: (i, p))
    return _pc(body, out_shape=(_sds((s, 2 * a_w), BF16),) * 3, grid=(npair, s // t),
               in_specs=[col(0), col(npair), col(2 * npair), pl.BlockSpec((1, t, 2), lambda p, i: (p, i, 0))],
               out_specs=(out, out, out), compiler_params=_params("parallel", "parallel"), name=name)(qkv, qkv, qkv, fcol)


def _attn_fwd(qa, ka, va, name, exchange=None):
    s = qa.shape[0]
    a_w = qa.shape[1] // 2
    npair = a_w // LANES
    t = _attn_tiles(s)
    nq = s // t

    def body(q_ref, k_ref, v_ref, o_ref, lse_ref, m_sc, l_sc, acc_sc):
        qi = pl.program_id(1)
        first = lax.broadcasted_iota(jnp.int32, (1, LANES), 1) < HEAD_DIM
        m_sc[...] = jnp.full_like(m_sc, NEG_BIG)
        l_sc[...] = jnp.zeros_like(l_sc)
        acc_sc[...] = jnp.zeros_like(acc_sc)

        def step(ki, diag):
            k_rows = pl.ds(pl.multiple_of(ki * t, t), t)
            m_old = m_sc[...]
            keep = None
            if diag:
                keep = (lax.broadcasted_iota(jnp.int32, (t, t), 0) >= lax.broadcasted_iota(jnp.int32, (t, t), 1))
            m_new, rs, pv = [], [], []
            for hh in range(2):
                cols = slice(hh * LANES, (hh + 1) * LANES)
                sc = lax.dot_general(q_ref[:, cols], k_ref[k_rows, cols], _NT, preferred_element_type=F32)
                if diag:
                    sc = jnp.where(keep, sc, NEG_BIG)
                mo = m_old[:, hh * HEAD_DIM:hh * HEAD_DIM + 1]
                mn = jnp.maximum(mo, jnp.max(sc, axis=1, keepdims=True))
                p = jnp.exp(sc - mn)
                m_new.append(mn)
                rs.append(jnp.sum(p, axis=1, keepdims=True))
                pv.append(lax.dot_general(p.astype(BF16), v_ref[k_rows, cols], _NN, preferred_element_type=F32))
            m2 = jnp.where(first, m_new[0], m_new[1])
            alpha = jnp.exp(m_old - m2)
            m_sc[...] = m2
            l_sc[...] = alpha * l_sc[...] + jnp.where(first, rs[0], rs[1])
            acc_sc[...] = alpha * acc_sc[...] + pv[0] + pv[1]

        def below_diagonal(ki, carry):
            step(ki, False)
            return carry

        lax.fori_loop(0, qi, below_diagonal, 0)
        step(qi, True)
        l2 = l_sc[...]
        o_ref[...] = acc_sc[...] / l2
        lse_ref[...] = m_sc[...] + jnp.log(l2)

    qs = pl.BlockSpec((t, 2 * LANES), lambda p, qi: (qi, p))
    ks = pl.BlockSpec((s, 2 * LANES), lambda p, qi: (0, p))
    os_ = pl.BlockSpec((t, LANES), lambda p, qi: (qi, p))
    return _pc(body, exchange, out_shape=(_sds((s, a_w), F32), _sds((s, a_w), F32)), grid=(npair, nq),
               in_specs=[qs, ks, ks], out_specs=(os_, os_),
               scratch_shapes=[pltpu.VMEM((t, LANES), F32)] * 3,
               compiler_params=_params("parallel", "arbitrary"), name=name)(qa, ka, va)


def _attn_bwd(qa, ka, va, do, o, lse, name, exchange=None):
    s = qa.shape[0]
    a_w = qa.shape[1] // 2
    npair = a_w // LANES
    t = _attn_tiles(s)
    nq = s // t
    scale = 1.0 / math.sqrt(HEAD_DIM)

    def body(q_ref, k_ref, v_ref, do_ref, o_ref, lse_ref, dq_ref, dk_ref, dv_ref, qx_ref, kx_ref, dk_sc, dv_sc, kx_sc):
        ki = pl.program_id(1)
        first = lax.broadcasted_iota(jnp.int32, (1, LANES), 1) < HEAD_DIM

        @pl.when(ki == 0)
        def _():
            dq_ref[...] = jnp.zeros_like(dq_ref)
            qx_ref[...] = jnp.zeros_like(qx_ref)

        def step(qi, diag):
            rows = pl.ds(pl.multiple_of(qi * t, t), t)
            do2 = do_ref[rows, :]
            lse2 = lse_ref[rows, :]
            dd = do2.astype(F32) * o_ref[rows, :]
            keep = None
            if diag:
                keep = (lax.broadcasted_iota(jnp.int32, (t, t), 0) >= lax.broadcasted_iota(jnp.int32, (t, t), 1))
            dq_h, dk_h, dv_h = [], [], []
            for hh in range(2):
                sel = first if hh == 0 else jnp.logical_not(first)
                cols = slice(hh * LANES, (hh + 1) * LANES)
                qh, kh, vh = q_ref[rows, cols], k_ref[:, cols], v_ref[:, cols]
                delta = jnp.sum(jnp.where(sel, dd, 0.0), axis=1, keepdims=True)
                sc = lax.dot_general(qh, kh, _NT, preferred_element_type=F32)
                if diag:
                    sc = jnp.where(keep, sc, NEG_BIG)
                p = jnp.exp(sc - lse2[:, hh * HEAD_DIM:hh * HEAD_DIM + 1])
                dp = lax.dot_general(do2, vh, _NT, preferred_element_type=F32)
                ds_b = (p * (dp - delta)).astype(BF16)
                dv_h.append(lax.dot_general(p.astype(BF16), do2, _TN, preferred_element_type=F32))
                dk_h.append(lax.dot_general(ds_b, qh, _TN, preferred_element_type=F32))
                dq_h.append(lax.dot_general(ds_b, kh, _NN, preferred_element_type=F32))
            dq_ref[rows, :] += jnp.where(first, dq_h[0], dq_h[1]) * scale
            qx_ref[rows, :] += jnp.where(first, dq_h[1], dq_h[0])
            dk_new = jnp.where(first, dk_h[0], dk_h[1])
            kx_new = jnp.where(first, dk_h[1], dk_h[0])
            dv_new = jnp.where(first, dv_h[0], dv_h[1])
            if diag:
                dk_sc[...] = dk_new
                kx_sc[...] = kx_new
                dv_sc[...] = dv_new
            else:
                dk_sc[...] += dk_new
                kx_sc[...] += kx_new
                dv_sc[...] += dv_new

        def below_diagonal(qi, carry):
            step(qi, False)
            return carry

        step(ki, True)
        lax.fori_loop(ki + 1, nq, below_diagonal, 0)
        dk_ref[...] = dk_sc[...].astype(BF16)
        dv_ref[...] = dv_sc[...].astype(BF16)
        kx_ref[...] = kx_sc[...]

    ks2 = pl.BlockSpec((t, 2 * LANES), lambda p, ki: (ki, p))
    qs2 = pl.BlockSpec((s, 2 * LANES), lambda p, ki: (0, p))
    whole = pl.BlockSpec((s, LANES), lambda p, ki: (0, p))
    kout = pl.BlockSpec((t, LANES), lambda p, ki: (ki, p))
    return _pc(body, exchange,
               out_shape=(_sds((s, a_w), F32), _sds((s, a_w), BF16), _sds((s, a_w), BF16), _sds((s, a_w), F32),
                          _sds((s, a_w), F32)),
               grid=(npair, nq), in_specs=[qs2, ks2, ks2, whole, whole, whole],
               out_specs=(whole, kout, kout, whole, kout),
               scratch_shapes=[pltpu.VMEM((t, LANES), F32)] * 3,
               compiler_params=_params("parallel", "arbitrary"), name=name)(qa, ka, va, do, o, lse)


def _shift_down(z, k, rows):
    return jnp.where(rows >= k, pltpu.roll(z, k, 0), 0.0)


def _shift_up(z, k, rows, n):
    return jnp.where(rows < n - k, pltpu.roll(z, n - k, 0), 0.0)


def _conv_fwd(bcx, conv_w, name):
    s = bcx.shape[0]
    cw = bcx.shape[1] // 3
    nb = cw // LANES

    def body(b_ref, c_ref, x_ref, w_ref, cv_ref):
        rows = lax.broadcasted_iota(jnp.int32, (s, LANES), 0)
        z = c_ref[...] * x_ref[...]
        w = w_ref[...]
        y = w[2:3, :] * z + w[1:2, :] * _shift_down(z, 1, rows) + w[0:1, :] * _shift_down(z, 2, rows)
        cv_ref[...] = b_ref[...] * y

    def col(off):
        return pl.BlockSpec((s, LANES), lambda j: (0, j + off))

    return _pc(body, out_shape=_sds((s, cw), F32), grid=(nb,),
               in_specs=[col(0), col(nb), col(2 * nb), pl.BlockSpec((CONV_K, LANES), lambda j: (0, j))],
               out_specs=col(0), compiler_params=_params("parallel"), name=name)(bcx, bcx, bcx, conv_w)


def _conv_bwd(dcv, bcx, conv_w, name):
    s = bcx.shape[0]
    cw = bcx.shape[1] // 3
    nb = cw // LANES

    def body(dcv_ref, b_ref, c_ref, x_ref, w_ref, db_ref, dc_ref, dxc_ref, dw_ref):
        rows = lax.broadcasted_iota(jnp.int32, (s, LANES), 0)
        cv_, xv = c_ref[...], x_ref[...]
        z = cv_ * xv
        w = w_ref[...]
        z1 = _shift_down(z, 1, rows)
        z2 = _shift_down(z, 2, rows)
        y = w[2:3, :] * z + w[1:2, :] * z1 + w[0:1, :] * z2
        dcvv = dcv_ref[...]
        db_ref[...] = (dcvv * y).astype(BF16)
        dy = dcvv * b_ref[...]
        dw_ref[0:1, :] = jnp.sum(dy * z2, axis=0, keepdims=True)
        dw_ref[1:2, :] = jnp.sum(dy * z1, axis=0, keepdims=True)
        dw_ref[2:3, :] = jnp.sum(dy * z, axis=0, keepdims=True)
        dz = w[2:3, :] * dy + w[1:2, :] * _shift_up(dy, 1, rows, s) + w[0:1, :] * _shift_up(dy, 2, rows, s)
        dc_ref[...] = (dz * xv).astype(BF16)
        dxc_ref[...] = (dz * cv_).astype(BF16)

    def col(off):
        return pl.BlockSpec((s, LANES), lambda j: (0, j + off))

    wspec = pl.BlockSpec((CONV_K, LANES), lambda j: (0, j))
    db, dc, dxc, dw = _pc(body, out_shape=(_sds((s, cw), BF16),) * 3 + (_sds((CONV_K, cw), F32),), grid=(nb,),
                          in_specs=[col(0), col(0), col(nb), col(2 * nb), wspec],
                          out_specs=(col(0), col(0), col(0), wspec),
                          compiler_params=_params("parallel"), name=name)(dcv, bcx, bcx, bcx, conv_w)
    return db, dc, dxc, dw


def _group_matrix():
    idx = jnp.arange(LANES) // HEAD_DIM
    return (idx[:, None] == idx[None, :]).astype(BF16)


def _group_sum(v, gmat):
    return _dot3(v, gmat)


def _gnorm_fwd(att, cv, gg, name):
    s, a_w = att.shape
    cw = cv.shape[1]
    d = a_w + cw
    tr = _tile(s, 512, 16)
    gmat = _group_matrix()

    def body(att_ref, cv_ref, gg_ref, gm_ref, yn_ref):
        gm = gm_ref[...]
        for c0 in range(0, d, LANES):
            y = att_ref[:, c0:c0 + LANES] if c0 < a_w else cv_ref[:, c0 - a_w:c0 - a_w + LANES]
            ms = _group_sum(y * y, gm) * (1.0 / HEAD_DIM)
            yn_ref[:, c0:c0 + LANES] = (y * lax.rsqrt(ms + EPS) * gg_ref[:, c0:c0 + LANES]).astype(BF16)

    return _pc(body, out_shape=_sds((s, d), BF16), grid=(s // tr,),
               in_specs=[pl.BlockSpec((tr, a_w), lambda i: (i, 0)), pl.BlockSpec((tr, cw), lambda i: (i, 0)),
                         _vec_spec(d), pl.BlockSpec((LANES, LANES), lambda i: (0, 0))],
               out_specs=pl.BlockSpec((tr, d), lambda i: (i, 0)),
               compiler_params=_params("parallel"), name=name)(att, cv, gg, gmat)


def _gnorm_bwd(dyn, att, cv, gg, name):
    s, a_w = att.shape
    cw = cv.shape[1]
    d = a_w + cw
    tr = _tile(s, 256, 16)
    gmat = _group_matrix()

    def body(dyn_ref, att_ref, cv_ref, gg_ref, gm_ref, datt_ref, dcv_ref, dgg_ref):
        @pl.when(pl.program_id(0) == 0)
        def _():
            dgg_ref[...] = jnp.zeros_like(dgg_ref)

        gm = gm_ref[...]
        for c0 in range(0, d, LANES):
            y = att_ref[:, c0:c0 + LANES] if c0 < a_w else cv_ref[:, c0 - a_w:c0 - a_w + LANES]
            dv = dyn_ref[:, c0:c0 + LANES]
            r = lax.rsqrt(_group_sum(y * y, gm) * (1.0 / HEAD_DIM) + EPS)
            xhat = y * r
            dgg_ref[:, c0:c0 + LANES] += jnp.sum(dv * xhat, axis=0, keepdims=True)
            dxh = dv * gg_ref[:, c0:c0 + LANES]
            proj = _group_sum(dxh * xhat, gm) * (1.0 / HEAD_DIM)
            dy = r * (dxh - xhat * proj)
            if c0 < a_w:
                datt_ref[:, c0:c0 + LANES] = dy.astype(BF16)
            else:
                dcv_ref[:, c0 - a_w:c0 - a_w + LANES] = dy

    return _pc(body, out_shape=(_sds((s, a_w), BF16), _sds((s, cw), F32), _sds((1, d), F32)), grid=(s // tr,),
               in_specs=[pl.BlockSpec((tr, d), lambda i: (i, 0)), pl.BlockSpec((tr, a_w), lambda i: (i, 0)),
                         pl.BlockSpec((tr, cw), lambda i: (i, 0)), _vec_spec(d),
                         pl.BlockSpec((LANES, LANES), lambda i: (0, 0))],
               out_specs=(pl.BlockSpec((tr, a_w), lambda i: (i, 0)), pl.BlockSpec((tr, cw), lambda i: (i, 0)),
                          _vec_spec(d)),
               compiler_params=_params("arbitrary"), name=name)(dyn, att, cv, gg, gmat)


def _adamw_math(w, g, m, v):
    m_new = ADAM_B1 * m + (1.0 - ADAM_B1) * g
    v_new = ADAM_B2 * v + (1.0 - ADAM_B2) * (g * g)
    m_hat = m_new / (1.0 - ADAM_B1 ** ADAM_STEP)
    v_hat = v_new / (1.0 - ADAM_B2 ** ADAM_STEP)
    delta = -ADAM_LR * (m_hat / (jnp.sqrt(v_hat) + ADAM_EPS) + ADAM_WD * w)
    return delta, m_new, v_new


def _row_tile(r, c):
    return _tile(r, max(8, ((1 << 18) // c) // 8 * 8), 8)


def _adamw(w, g, m, v, name):
    r, c = w.shape
    tr = _row_tile(r, c)

    def body(w_ref, g_ref, m_ref, v_ref, d_ref, mo_ref, vo_ref):
        d, mn, vn = _adamw_math(w_ref[...], g_ref[...], m_ref[...], v_ref[...])
        d_ref[...] = d
        mo_ref[...] = mn
        vo_ref[...] = vn

    spec = pl.BlockSpec((tr, c), lambda i: (i, 0))
    return _pc(body, out_shape=(_sds((r, c), F32),) * 3, grid=(r // tr,), in_specs=[spec] * 4,
               out_specs=(spec,) * 3, compiler_params=_params("parallel"), name=name)(w, g, m, v)


def _adamw_halves(w, mine, theirs, m, v, core, name):
    r2, c = w.shape
    r = r2 // 2
    assert mine.shape == (r, c) and theirs.shape == (r, c)
    tr = _row_tile(r, c)
    nb = r // tr

    def body(core_ref, w_ref, a_ref, b_ref, m_ref, v_ref, g_ref, d_ref, mo_ref, vo_ref):
        g = jnp.where(pl.program_id(0) == core_ref[0], a_ref[...], b_ref[...])
        d, mn, vn = _adamw_math(w_ref[...], g, m_ref[...], v_ref[...])
        g_ref[...] = g
        d_ref[...] = d
        mo_ref[...] = mn
        vo_ref[...] = vn

    full = pl.BlockSpec((tr, c), lambda h, i, core_ref: (h * nb + i, 0))
    half = pl.BlockSpec((tr, c), lambda h, i, core_ref: (i, 0))
    grid_spec = pltpu.PrefetchScalarGridSpec(
        num_scalar_prefetch=1, grid=(2, nb), in_specs=[full, half, half, full, full], out_specs=(full,) * 4)
    return _pc(body, out_shape=(_sds((r2, c), F32),) * 4, grid_spec=grid_spec,
               compiler_params=_params("parallel", "parallel"), name=name)(core, w, mine, theirs, m, v)


def _ada_fwd(c16, ada_w, ada_b, name):
    d, n = ada_w.shape
    tn = _tile(n, 768, LANES)

    def body(c_ref, w_ref, b_ref, o_ref):
        cv = c_ref[...]
        sc = (cv * jax.nn.sigmoid(cv)).astype(BF16)
        o_ref[...] = lax.dot_general(sc, w_ref[...].astype(BF16), _NN, preferred_element_type=F32) + b_ref[...]

    return _pc(body, out_shape=_sds((16, n), F32), grid=(n // tn,),
               in_specs=[pl.BlockSpec((16, d), lambda j: (0, 0)), pl.BlockSpec((d, tn), lambda j: (0, j)),
                         pl.BlockSpec((1, tn), lambda j: (0, j))],
               out_specs=pl.BlockSpec((16, tn), lambda j: (0, j)),
               compiler_params=_params("parallel"), name=name)(c16, ada_w, ada_b)


def _ada_update(c16_t, dmod16, w, m, v, name):
    r, c = w.shape
    tr = _row_tile(r, c)

    def body(c_ref, dm_ref, w_ref, m_ref, v_ref, g_ref, d_ref, mo_ref, vo_ref):
        cv = c_ref[...]
        sc = (cv * jax.nn.sigmoid(cv)).astype(BF16)
        g = lax.dot_general(sc, dm_ref[...].astype(BF16), _NN, preferred_element_type=F32)
        d, mn, vn = _adamw_math(w_ref[...], g, m_ref[...], v_ref[...])
        g_ref[...] = g
        d_ref[...] = d
        mo_ref[...] = mn
        vo_ref[...] = vn

    spec = pl.BlockSpec((tr, c), lambda i: (i, 0))
    return _pc(body, out_shape=(_sds((r, c), F32),) * 4, grid=(r // tr,),
               in_specs=[pl.BlockSpec((tr, 16), lambda i: (i, 0)), pl.BlockSpec((16, c), lambda i: (0, 0)),
                         spec, spec, spec],
               out_specs=(spec,) * 4, compiler_params=_params("parallel"), name=name)(c16_t, dmod16, w, m, v)


def _add_half(dw, recv, core, name):
    _, _, r, w = dw.shape
    tr = _tile(r, 256, 16)

    def body(core_ref, a_ref, b_ref, o_ref):
        o_ref[...] = (a_ref[...].astype(F32) + b_ref[...].astype(F32)).astype(BF16)

    grid_spec = pltpu.PrefetchScalarGridSpec(
        num_scalar_prefetch=1, grid=(N_CHIPS, r // tr),
        in_specs=[pl.BlockSpec((None, None, tr, w), lambda s, i, core_ref: (s, core_ref[0], i, 0)),
                  pl.BlockSpec((None, tr, w), lambda s, i, core_ref: (s, i, 0))],
        out_specs=pl.BlockSpec((None, tr, w), lambda s, i, core_ref: (s, i, 0)))
    return _pc(body, out_shape=_sds((N_CHIPS, r, w), BF16), grid_spec=grid_spec,
               compiler_params=_params("parallel", "parallel"), name=name)(core, dw, recv)


def _sum_chips(own, recv, chip, name):
    _, r, w = own.shape
    tr = _tile(r, 256, 16)

    def body(chip_ref, own_ref, p_ref, o_ref):
        acc = own_ref[...].astype(F32)
        for q in range(N_CHIPS - 1):
            acc = acc + p_ref[q].astype(F32)
        o_ref[...] = acc

    grid_spec = pltpu.PrefetchScalarGridSpec(
        num_scalar_prefetch=1, grid=(r // tr,),
        in_specs=[pl.BlockSpec((None, tr, w), lambda i, chip_ref: (chip_ref[0], i, 0)),
                  pl.BlockSpec((N_CHIPS - 1, tr, w), lambda i, chip_ref: (0, i, 0))],
        out_specs=pl.BlockSpec((tr, w), lambda i, chip_ref: (i, 0)))
    return _pc(body, out_shape=_sds((r, w), F32), grid_spec=grid_spec,
               compiler_params=_params("parallel"), name=name)(chip, own, recv)


def _sum_devices(parts, name):
    nd, r, w = parts.shape

    def body(p_ref, o_ref):
        acc = p_ref[0]
        for q in range(1, nd):
            acc = acc + p_ref[q]
        o_ref[...] = acc

    return _pc(body, out_shape=_sds((r, w), F32), name=name)(parts)


def _place():
    x, y, c = lax.axis_index("x"), lax.axis_index("y"), lax.axis_index("c")
    chips = [(1 - x, y), (x, 1 - y), (1 - x, 1 - y)]
    return x, y, c, chips


def _all_gather_small(blk, name):
    r, w = blk.shape

    def body(x_ref, out_ref, send_sems, recv_sems, local_sem):
        x, y, c, chips = _place()
        me, sibling = (x, y, c), (x, y, 1 - c)

        def rows(px, py, pc):
            return out_ref.at[pl.ds((4 * px + 2 * py + pc) * r, r), :]

        def copy(k, block, to, src=None):
            return pltpu.make_async_remote_copy(
                src_ref=rows(*block) if src is None else src, dst_ref=rows(*block),
                send_sem=send_sems.at[k], recv_sem=recv_sems.at[k], device_id=to, device_id_type=MESH)

        mine = pltpu.make_async_copy(x_ref, rows(*me), local_sem)
        mine.start()
        first = [copy(0, me, sibling, src=x_ref)]
        first += [copy(1 + j, me, (*chip, c), src=x_ref) for j, chip in enumerate(chips)]
        for cp in first:
            cp.start()
        passed = [copy(4 + j, (*chip, c), sibling) for j, chip in enumerate(chips)]
        for j, chip in enumerate(chips):
            copy(1 + j, (*chip, c), me).wait_recv()
            passed[j].start()
        copy(0, sibling, me).wait_recv()
        for j, chip in enumerate(chips):
            copy(4 + j, (*chip, 1 - c), me).wait_recv()
        for cp in first + passed:
            cp.wait_send()
        mine.wait()

    return _pc(body, out_shape=_sds((N_DEV * r, w), blk.dtype),
               in_specs=[pl.BlockSpec(memory_space=pltpu.VMEM)], out_specs=pl.BlockSpec(memory_space=pltpu.VMEM),
               scratch_shapes=[pltpu.SemaphoreType.DMA((7,)), pltpu.SemaphoreType.DMA((7,)), pltpu.SemaphoreType.DMA],
               name=name)(blk)


def _remote(src, dst, send_sems, recv_sems, k, to):
    return pltpu.make_async_remote_copy(src_ref=src, dst_ref=dst, send_sem=send_sems.at[k], recv_sem=recv_sems.at[k],
                                        device_id=to, device_id_type=MESH)


def _exchange_of(inputs, out_shapes, n_sems, copies, aliases=None):
    def start(src, dst, send_sems, recv_sems):
        for cp in copies(src, dst, send_sems, recv_sems)[0]:
            cp.start()

    def finish(src, dst, send_sems, recv_sems):
        sends, arrivals = copies(src, dst, send_sems, recv_sems)
        for cp in arrivals:
            cp.wait_recv()
        for cp in sends:
            cp.wait_send()

    return _Exchange(inputs, out_shapes, n_sems, start, finish, aliases)


def _run_exchange(ex, name):
    n_in, n_out = len(ex.inputs), len(ex.out_shapes)

    def body(*refs):
        src, dst = refs[:n_in], refs[n_in:n_in + n_out]
        send_sems, recv_sems = refs[n_in + n_out:]
        ex.start(src, dst, send_sems, recv_sems)
        ex.finish(src, dst, send_sems, recv_sems)

    ex.results = list(pl.pallas_call(
        body, out_shape=tuple(ex.out_shapes), in_specs=[_ANY] * n_in, out_specs=(_ANY,) * n_out,
        scratch_shapes=[pltpu.SemaphoreType.DMA((ex.n_sems,)), pltpu.SemaphoreType.DMA((ex.n_sems,))],
        input_output_aliases=ex.aliases, name=name)(*ex.inputs))
    return ex.results


def _gather_ici_exchange(shards):
    n = len(shards)

    def copies(own, out, send_sems, recv_sems):
        x, y, c, chips = _place()
        my_chip = 2 * x + y
        sends, arrivals = [], []
        for i in range(n):
            for j, chip in enumerate(chips):
                to = (*chip, c)
                sends.append(_remote(own[i].at[c], out[i].at[my_chip, c], send_sems, recv_sems, 4 * i + j, to))
                arrivals.append(_remote(own[i].at[c], out[i].at[2 * chip[0] + chip[1], c], send_sems, recv_sems, 4 * i + j, to))
            whole = _remote(own[i], out[i].at[my_chip], send_sems, recv_sems, 4 * i + 3, (x, y, 1 - c))
            sends.append(whole)
            arrivals.append(whole)
        return sends, arrivals

    return _exchange_of(shards, [_sds((N_CHIPS,) + s.shape, s.dtype) for s in shards], 4 * n, copies)


def _gather_pass_exchange(gathered):
    n = len(gathered)

    def copies(src, dst, send_sems, recv_sems):
        x, y, c, chips = _place()
        sends, arrivals = [], []
        for i in range(n):
            for j, chip in enumerate(chips):
                idx = 2 * chip[0] + chip[1]
                sends.append(_remote(src[i].at[idx, c], dst[i].at[idx, c], send_sems, recv_sems, 3 * i + j, (x, y, 1 - c)))
                arrivals.append(_remote(src[i].at[idx, c], dst[i].at[idx, 1 - c], send_sems, recv_sems, 3 * i + j, (x, y, 1 - c)))
        return sends, arrivals

    return _exchange_of(gathered, [_sds(g.shape, g.dtype) for g in gathered], 3 * n, copies,
                        aliases={i: i for i in range(n)})


def _reduce_sibling_exchange(grads):
    n = len(grads)

    def copies(src, dst, send_sems, recv_sems):
        x, y, c, _ = _place()
        both = [_remote(src[i].at[s, 1 - c], dst[i].at[s], send_sems, recv_sems, N_CHIPS * i + s, (x, y, 1 - c))
                for i in range(n) for s in range(N_CHIPS)]
        return both, both

    return _exchange_of(grads, [_sds((N_CHIPS,) + g.shape[2:], g.dtype) for g in grads], N_CHIPS * n, copies)


def _reduce_chips_exchange(parts):
    n = len(parts)

    def copies(src, dst, send_sems, recv_sems):
        x, y, c, chips = _place()
        both = [_remote(src[i].at[2 * chip[0] + chip[1]], dst[i].at[j], send_sems, recv_sems, 3 * i + j, (*chip, c))
                for i in range(n) for j, chip in enumerate(chips)]
        return both, both

    return _exchange_of(parts, [_sds((N_CHIPS - 1,) + p.shape[1:], p.dtype) for p in parts], 3 * n, copies)


def _share_exchange(halves):
    n = len(halves)

    def copies(src, dst, send_sems, recv_sems):
        x, y, c, _ = _place()
        both = [_remote(src[i], dst[i], send_sems, recv_sems, i, (x, y, 1 - c)) for i in range(n)]
        return both, both

    return _exchange_of(halves, [_sds(h.shape, h.dtype) for h in halves], n, copies)


HEAD_ROWS = 16


class _WeightTraffic:
    def __init__(self, shards, core, chip):
        self.shards, self.core, self.chip = shards, core, chip
        self.gather, self.grads, self.reduce, self.chip_sums, self.half_sums, self.shared = {}, {}, {}, {}, {}, {}

    def gather_ici(self, grp):
        self.gather[grp] = _gather_ici_exchange(self.shards[grp])
        return self.gather[grp]

    def gather_pass(self, grp):
        self.gather[grp] = _gather_pass_exchange(self.gather[grp].results)
        return self.gather[grp]

    def weights(self, grp):
        return [g.reshape(-1, g.shape[-1]) for g in self.gather[grp].results]

    def reduce_sibling(self, grp, grads):
        self.grads[grp] = [g.reshape(N_CHIPS, 2, g.shape[0] // (2 * N_CHIPS), g.shape[1]) for g in grads]
        self.reduce[grp] = _reduce_sibling_exchange(self.grads[grp])
        return self.reduce[grp]

    def add_halves(self, grp):
        self.chip_sums[grp] = [_add_half(g, r, self.core, "add_half_%s%d" % (grp, i))
                               for i, (g, r) in enumerate(zip(self.grads[grp], self.reduce[grp].results))]

    def reduce_chips(self, grp):
        self.reduce[grp] = _reduce_chips_exchange(self.chip_sums[grp])
        return self.reduce[grp]

    def sum_chips(self, grp):
        self.half_sums[grp] = [_sum_chips(o, p, self.chip, "sum_chips_%s%d" % (grp, i))
                               for i, (o, p) in enumerate(zip(self.chip_sums[grp], self.reduce[grp].results))]

    def share(self, grp):
        self.shared[grp] = _share_exchange(self.half_sums[grp])
        return self.shared[grp]

    def totals(self, grp):
        return list(zip(self.half_sums[grp], self.shared[grp].results))


def _ffn_fwd(x, norm_g, shift, scale, gate, wg_t, wu_t, wd, tag, up_exchange=None, down_exchange=None):
    h = _norm_mod_fwd(x, norm_g, shift, scale, tag + "_norm_fwd")
    a, u, hid = _ffn_up(h, wg_t, wu_t, tag + "_up", exchange=up_exchange)
    x_out, f = _mm(hid, wd, "nn", F32, tag + "_down", res=x, gate=gate, aux_dtype=BF16,
                   exchange=down_exchange() if down_exchange else None)
    return x_out, (h, a, u, hid, f)


def _ffn_bwd(dx_out, x, saved, norm_g, scale, gate, wg_t, wu_t, wd, tag, traffic, dact_exchange=None, dw_exchange=None,
             finish_reduction=False):
    h, a, u, hid, f = saved
    df, dgate = _gate_bwd(dx_out, f, gate, tag + "_gate_bwd")
    da, du = _ffn_dact(df, wd, a, u, tag + "_dact", exchange=dact_exchange)
    dwd = _mm(hid, df, "tn", BF16, tag + "_dwd", exchange=dw_exchange() if dw_exchange else None)
    dwg_t = _mm(da, h, "tn", BF16, tag + "_dwg")
    dwu_t = _mm(du, h, "tn", BF16, tag + "_dwu")
    dh = _mm(da, wg_t, "nn", F32, tag + "_dh_a", exchange=traffic.reduce_sibling(tag, [dwg_t, dwu_t, dwd]))
    traffic.add_halves(tag)
    dh = _mm(du, wu_t, "nn", F32, tag + "_dh_u", res=dh, exchange=traffic.reduce_chips(tag) if finish_reduction else None)
    if finish_reduction:
        traffic.sum_chips(tag)
    dx, dshift, dscale, dnorm_g = _norm_mod_bwd(dh, x, norm_g, scale, dx_out, tag + "_norm_bwd",
                                                exchange=traffic.share(tag) if finish_reduction else None)
    return dx, (dshift, dscale, dgate, dnorm_g)


def _layer_step(x, target, mod, gains, forget_bias, conv_w, traffic, att_w, in_shard, in_rows):
    sh1, sc1, g1, sh2, sc2, g2, sh3, sc3, g3 = mod
    norm1_g, norm2_g, norm3_g, final_g, group_g = gains
    s, d = x.shape
    n_heads = att_w // HEAD_DIM
    npair = n_heads // 2
    gate1, gate3 = 0.5 * g1, 0.5 * g3

    def split_w_in(w_in_pad):
        w_in_t = w_in_pad.reshape(N_CHIPS, in_rows, d)[:, :in_shard].reshape(N_CHIPS * in_shard, d)
        return (w_in_t[:3 * att_w], _pad_rows(w_in_t[3 * att_w:3 * att_w + n_heads], LANES), w_in_t[3 * att_w + n_heads:])

    _run_exchange(traffic.gather_ici("ffn1"), "gather_ffn1_ici")
    _run_exchange(traffic.gather_pass("ffn1"), "gather_ffn1_pass")
    wg1_t, wu1_t, wd1 = traffic.weights("ffn1")
    x1, saved1 = _ffn_fwd(x, norm1_g, sh1, sc1, gate1, wg1_t, wu1_t, wd1, "ffn1",
                          up_exchange=traffic.gather_ici("mix"), down_exchange=lambda: traffic.gather_pass("mix"))
    w_in_pad, w_out = traffic.weights("mix")
    wqkv_t, wf_t, wbcx_t = split_w_in(w_in_pad)

    h2 = _norm_mod_fwd(x1, norm2_g, sh2, sc2, "mix_norm_fwd")
    qkv = _mm(h2, wqkv_t, "nt", BF16, "mix_proj_qkv")
    bcx = _mm(h2, wbcx_t, "nt", F32, "mix_proj_bcx")
    flog = _mm(h2, wf_t, "nt", F32, "mix_proj_f")
    flog_t = jnp.pad(flog[:, :n_heads].T, ((0, HEAD_ROWS - n_heads), (0, 0)))
    bias_col = jnp.pad(forget_bias, (0, HEAD_ROWS - n_heads))[:, None]
    f_rows = _forget_fwd(flog_t, bias_col, "forget_fwd")
    fcol = f_rows[:n_heads].reshape(npair, 2, s).transpose(0, 2, 1)
    qa, ka, va = _attn_prep(qkv, fcol, "attn_prep")
    att, lse = _attn_fwd(qa, ka, va, "attn_fwd", exchange=traffic.gather_ici("ffn2"))
    cv = _conv_fwd(bcx, conv_w, "conv_fwd")
    yn = _gnorm_fwd(att, cv, group_g, "gnorm_fwd")
    x2, mix = _mm(yn, w_out, "nn", F32, "mix_out", res=x1, gate=g2, aux_dtype=BF16, exchange=traffic.gather_pass("ffn2"))
    wg2_t, wu2_t, wd2 = traffic.weights("ffn2")

    x3, saved3 = _ffn_fwd(x2, norm3_g, sh3, sc3, gate3, wg2_t, wu2_t, wd2, "ffn2")

    dx3, loss_row, dfinal_g = _final_loss(x3, final_g, target, "final_loss")

    dx2, (dsh3, dsc3, dgate3, dnorm3_g) = _ffn_bwd(
        dx3, x2, saved3, norm3_g, sc3, gate3, wg2_t, wu2_t, wd2, "ffn2", traffic)

    dmix, dg2 = _gate_bwd(dx2, mix, g2, "mix_gate_bwd")
    dyn = _mm(dmix, w_out, "nt", F32, "mix_out_dyn")
    dw_out = _mm(yn, dmix, "tn", BF16, "mix_out_dw")
    datt, dcv, dgroup_g = _gnorm_bwd(dyn, att, cv, group_g, "gnorm_bwd")
    db, dc, dxc, dconv_w = _conv_bwd(dcv, bcx, conv_w, "conv_bwd")
    dbcx = jnp.concatenate([db, dc, dxc], axis=1)
    dq, dk, dv, qx, kx = _attn_bwd(qa, ka, va, datt, att, lse, "attn_bwd", exchange=traffic.reduce_chips("ffn2"))
    traffic.sum_chips("ffn2")
    dqkv = jnp.concatenate([dq.astype(BF16), dk, dv], axis=1)
    df_pairs = jnp.stack([qx[:, HEAD_DIM::LANES] - kx[:, HEAD_DIM + 3::LANES], qx[:, 0::LANES] - kx[:, 3::LANES]], axis=-1)
    df_t = jnp.pad(df_pairs.reshape(s, n_heads).T, ((0, HEAD_ROWS - n_heads), (0, 0)))
    dflog_t, dbias_col = _forget_bwd(df_t, flog_t, bias_col, "forget_bwd")
    dflog = jnp.pad(dflog_t[:n_heads].T, ((0, 0), (0, LANES - n_heads))).astype(BF16)
    dh2 = _mm(dqkv, wqkv_t, "nn", F32, "mix_dh_qkv", exchange=traffic.share("ffn2"))
    dh2 = _mm(dbcx, wbcx_t, "nn", F32, "mix_dh_bcx", res=dh2)
    dh2 = _mm(dflog, wf_t, "nn", F32, "mix_dh_f", res=dh2)
    dwqkv_t = _mm(dqkv, h2, "tn", BF16, "mix_dw_qkv")
    dwbcx_t = _mm(dbcx, h2, "tn", BF16, "mix_dw_bcx")
    dwf_t = _mm(dflog, h2, "tn", BF16, "mix_dw_f")
    dw_in_t = jnp.concatenate([dwqkv_t, dwf_t[:n_heads], dwbcx_t], axis=0).reshape(N_CHIPS, in_shard, d)
    dw_in_t = jnp.pad(dw_in_t, ((0, 0), (0, in_rows - in_shard), (0, 0))).reshape(N_CHIPS * in_rows, d)
    dx1, dsh2, dsc2, dnorm2_g = _norm_mod_bwd(dh2, x1, norm2_g, sc2, dx2, "mix_norm_bwd",
                                              exchange=traffic.reduce_sibling("mix", [dw_in_t, dw_out]))
    traffic.add_halves("mix")

    def share_mix():
        traffic.sum_chips("mix")
        return traffic.share("mix")

    dx, (dsh1, dsc1, dgate1, dnorm1_g) = _ffn_bwd(
        dx1, x, saved1, norm1_g, sc1, gate1, wg1_t, wu1_t, wd1, "ffn1", traffic,
        dact_exchange=traffic.reduce_chips("mix"), dw_exchange=share_mix, finish_reduction=True)

    dmod = [dsh1, dsc1, 0.5 * dgate1, dsh2, dsc2, dg2, dsh3, dsc3, 0.5 * dgate3]
    dgains = [dnorm1_g, dnorm2_g, dnorm3_g, dfinal_g, dgroup_g]
    dbias = dbias_col[:n_heads, 0]
    return dx, loss_row, dmod, dgains, dbias, dconv_w


SMALL_ROWS = 24
ROW_GAINS, ROW_LOSS, ROW_FORGET, ROW_CONV, ROW_MOD = 0, 5, 6, 7, 10
PROW_ADA_B, PROW_GAINS, PROW_FORGET, PROW_CONV = 0, 9, 14, 15


def _round_up(n, m):
    return -(-n // m) * m


def _pad_rows(a, rows):
    return jnp.pad(a, ((0, rows - a.shape[0]), (0, 0)))


def _halves(a):
    return a.reshape(2, a.shape[0] // 2, a.shape[1])


def _rows_at(a, r0, total, width):
    return jnp.pad(a, ((r0, total - r0 - a.shape[0]), (0, width - a.shape[1])))


def kernel(x, c, ada_w, ada_b, norm1_g, ffn1_w_gate, ffn1_w_up, ffn1_w_down, norm2_g, w_in, forget_bias, conv_w, group_norm_g, w_out, norm3_g, ffn2_w_gate, ffn2_w_up, ffn2_w_down, final_g, loss_target, m_ada_w, m_ada_b, m_norm1_g, m_ffn1_w_gate, m_ffn1_w_up, m_ffn1_w_down, m_norm2_g, m_w_in, m_forget_bias, m_conv_w, m_group_norm_g, m_w_out, m_norm3_g, m_ffn2_w_gate, m_ffn2_w_up, m_ffn2_w_down, m_final_g, v_ada_w, v_ada_b, v_norm1_g, v_ffn1_w_gate, v_ffn1_w_up, v_ffn1_w_down, v_norm2_g, v_w_in, v_forget_bias, v_conv_w, v_group_norm_g, v_w_out, v_norm3_g, v_ffn2_w_gate, v_ffn2_w_up, v_ffn2_w_down, v_final_g):
    xi, yi, ci = lax.axis_index("x"), lax.axis_index("y"), lax.axis_index("c")
    chip = 2 * xi + yi
    dev = 4 * xi + 2 * yi + ci
    _, s, d = x.shape
    att_w = d // 2
    conv_width = d - att_w
    n_heads = att_w // HEAD_DIM
    in_shard = w_in.shape[1]
    in_rows = _round_up(in_shard, 32)
    cs = conv_w.shape[1]
    mod_shard = ada_w.shape[1]
    assert N_MOD * d == N_CHIPS * mod_shard and conv_width == N_CHIPS * cs and n_heads % 2 == 0

    pack0 = _rows_at(c, 0, 8, d) + _rows_at(conv_w, 1, 8, d)
    got0 = _all_gather_small(pack0, "gather_cond").reshape(N_DEV, 8, d)
    c16 = _pad_rows(got0[:, 0, :], 16)
    conv_full = got0[0::2, 1:1 + CONV_K, :cs].transpose(1, 0, 2).reshape(CONV_K, conv_width)

    ada_b_mine = lax.dynamic_slice(ada_b, (chip * mod_shard,), (mod_shard,))[None, :]
    mod_part = _ada_fwd(c16, ada_w, ada_b_mine, "ada_fwd")
    got1 = _all_gather_small(mod_part, "gather_mod").reshape(N_DEV, 16, mod_shard)
    mod_mine = lax.dynamic_index_in_dim(got1[0::2], dev, axis=1, keepdims=False).reshape(N_MOD, d)
    mod = [mod_mine[i:i + 1] for i in range(N_MOD)]

    def t_bf(w):
        return w.T.astype(BF16)

    shards = {"ffn1": [_halves(t_bf(ffn1_w_gate)), _halves(t_bf(ffn1_w_up)), _halves(ffn1_w_down.astype(BF16))],
              "mix": [_halves(_pad_rows(t_bf(w_in), in_rows)), _halves(w_out.astype(BF16))],
              "ffn2": [_halves(t_bf(ffn2_w_gate)), _halves(t_bf(ffn2_w_up)), _halves(ffn2_w_down.astype(BF16))]}
    core = ci.astype(jnp.int32).reshape(1)
    chip_arr = chip.astype(jnp.int32).reshape(1)
    traffic = _WeightTraffic(shards, core, chip_arr)

    gains = [g[None, :] for g in (norm1_g, norm2_g, norm3_g, final_g, group_norm_g)]
    dx, loss_row, dmod, dgains, dbias, dconv_w = _layer_step(
        x[0], loss_target[0], mod, gains, forget_bias, conv_full, traffic, att_w, in_shard, in_rows)

    pack = sum(_rows_at(g, ROW_GAINS + i, SMALL_ROWS, d) for i, g in enumerate(dgains))
    pack += _rows_at(loss_row, ROW_LOSS, SMALL_ROWS, d) + _rows_at(dbias[None, :], ROW_FORGET, SMALL_ROWS, d)
    pack += _rows_at(dconv_w, ROW_CONV, SMALL_ROWS, d)
    pack += sum(_rows_at(g, ROW_MOD + i, SMALL_ROWS, d) for i, g in enumerate(dmod))
    got2 = _all_gather_small(pack, "gather_small_grads").reshape(N_DEV, SMALL_ROWS, d)
    tot = _sum_devices(got2, "sum_small_grads")
    loss = tot[ROW_LOSS, 0]
    grad_ada_b = tot[ROW_MOD:ROW_MOD + N_MOD].reshape(N_MOD * d)
    grad_conv = lax.dynamic_slice(tot[ROW_CONV:ROW_CONV + CONV_K], (0, chip * cs), (CONV_K, cs))
    dmod_all = got2[:, ROW_MOD:ROW_MOD + N_MOD, :].reshape(N_DEV, N_MOD * d)
    dmod16 = _pad_rows(lax.dynamic_slice(dmod_all, (0, chip * mod_shard), (N_DEV, mod_shard)), 16)

    totals = traffic.totals("ffn1") + traffic.totals("mix") + traffic.totals("ffn2")

    names = ("ffn1_w_gate", "ffn1_w_up", "ffn1_w_down", "w_in", "w_out", "ffn2_w_gate", "ffn2_w_up", "ffn2_w_down")
    transposed = ("ffn1_w_gate", "ffn1_w_up", "w_in", "ffn2_w_gate", "ffn2_w_up")
    params = {"ffn1_w_gate": (ffn1_w_gate, m_ffn1_w_gate, v_ffn1_w_gate), "ffn1_w_up": (ffn1_w_up, m_ffn1_w_up, v_ffn1_w_up),
              "ffn1_w_down": (ffn1_w_down, m_ffn1_w_down, v_ffn1_w_down), "w_in": (w_in, m_w_in, v_w_in),
              "w_out": (w_out, m_w_out, v_w_out), "ffn2_w_gate": (ffn2_w_gate, m_ffn2_w_gate, v_ffn2_w_gate),
              "ffn2_w_up": (ffn2_w_up, m_ffn2_w_up, v_ffn2_w_up), "ffn2_w_down": (ffn2_w_down, m_ffn2_w_down, v_ffn2_w_down)}
    out = {}
    for name_, (mine, theirs) in zip(names, totals):
        w, m, v = params[name_]
        if name_ in transposed:
            w, m, v = w.T, m.T, v.T
        if name_ == "w_in":
            both = jnp.where(ci == 0, jnp.concatenate([mine, theirs]), jnp.concatenate([theirs, mine]))[:in_shard]
            res = (both,) + tuple(_adamw(w, both, m, v, "adamw_" + name_))
        else:
            res = _adamw_halves(w, mine, theirs, m, v, core, "adamw_" + name_)
        out[name_] = tuple(r.T for r in res) if name_ in transposed else tuple(res)
    c16_t = c16.T
    out["ada_w"] = tuple(_ada_update(c16_t, dmod16, ada_w, m_ada_w, v_ada_w, "adamw_ada_w"))

    def small_pack(ada_b_, gains_, forget_, conv_):
        p = _rows_at(ada_b_.reshape(N_MOD, d), PROW_ADA_B, SMALL_ROWS, d)
        p += sum(_rows_at(g[None, :], PROW_GAINS + i, SMALL_ROWS, d) for i, g in enumerate(gains_))
        p += _rows_at(forget_[None, :], PROW_FORGET, SMALL_ROWS, d) + _rows_at(conv_, PROW_CONV, SMALL_ROWS, d)
        return p

    g_gains = [tot[ROW_GAINS + i] for i in range(5)]
    g_forget = tot[ROW_FORGET, :n_heads]
    sw = small_pack(ada_b, (norm1_g, norm2_g, norm3_g, final_g, group_norm_g), forget_bias, conv_w)
    sm = small_pack(m_ada_b, (m_norm1_g, m_norm2_g, m_norm3_g, m_final_g, m_group_norm_g), m_forget_bias, m_conv_w)
    sv = small_pack(v_ada_b, (v_norm1_g, v_norm2_g, v_norm3_g, v_final_g, v_group_norm_g), v_forget_bias, v_conv_w)
    sg = small_pack(grad_ada_b, g_gains, g_forget, grad_conv)
    small = (sg,) + tuple(_adamw(sw, sg, sm, sv, "adamw_small"))

    def unpack(p):
        r = {"ada_b": p[PROW_ADA_B:PROW_ADA_B + N_MOD].reshape(N_MOD * d), "forget_bias": p[PROW_FORGET, :n_heads],
             "conv_w": p[PROW_CONV:PROW_CONV + CONV_K, :cs]}
        for i, nm in enumerate(("norm1_g", "norm2_g", "norm3_g", "final_g", "group_norm_g")):
            r[nm] = p[PROW_GAINS + i]
        return r

    small = [unpack(p) for p in small]
    order = ("ada_w", "ada_b", "norm1_g", "ffn1_w_gate", "ffn1_w_up", "ffn1_w_down", "norm2_g", "w_in", "forget_bias",
             "conv_w", "group_norm_g", "w_out", "norm3_g", "ffn2_w_gate", "ffn2_w_up", "ffn2_w_down", "final_g")
    result = [loss, dx[None]]
    for k in range(4):
        result += [out[nm][k] if nm in out else small[k][nm] for nm in order]
    return tuple(result)
```

```python
import functools
import math

import jax
import jax.numpy as jnp
from jax import lax
from jax.experimental import pallas as pl
from jax.experimental.pallas import tpu as pltpu

F32 = jnp.float32
BF16 = jnp.bfloat16

HEAD_DIM = 64
CONV_K = 3
N_MOD = 9
EPS = 1e-6
ADAM_LR = 0.001
ADAM_B1 = 0.9
ADAM_B2 = 0.999
ADAM_EPS = 1e-08
ADAM_WD = 0.01
ADAM_STEP = 10

LANES = 128
N_CHIPS = 4
N_DEV = 8
VMEM_LIMIT_BYTES = 56 * 1024 * 1024
NEG_BIG = -1e30
MESH = pl.DeviceIdType.MESH

_NT = (((1,), (1,)), ((), ()))
_NN = (((1,), (0,)), ((), ()))
_TN = (((0,), (0,)), ((), ()))


def _params(*sem):
    return pltpu.CompilerParams(dimension_semantics=sem, vmem_limit_bytes=VMEM_LIMIT_BYTES)


class _Exchange:
    def __init__(self, inputs, out_shapes, n_sems, start, finish, aliases=None):
        self.inputs, self.out_shapes, self.n_sems = list(inputs), list(out_shapes), n_sems
        self.start, self.finish, self.aliases = start, finish, dict(aliases or {})
        self.results = None

    def set_results(self, results):
        self.results = list(results)


class _SemaphoreWindow:
    def __init__(self, sems, base):
        self.sems, self.base = sems, base
        self.at = self

    def __getitem__(self, k):
        return self.sems.at[self.base + k]


class _JoinedExchange(_Exchange):
    def __init__(self, parts):
        self.parts = parts
        assert all(not p.aliases for p in parts)

        def each(method, src, dst, send_sems, recv_sems):
            i0 = o0 = s0 = 0
            for p in parts:
                i1, o1 = i0 + len(p.inputs), o0 + len(p.out_shapes)
                getattr(p, method)(src[i0:i1], dst[o0:o1], _SemaphoreWindow(send_sems, s0), _SemaphoreWindow(recv_sems, s0))
                i0, o0, s0 = i1, o1, s0 + p.n_sems

        super().__init__([a for p in parts for a in p.inputs], [o for p in parts for o in p.out_shapes],
                         sum(p.n_sems for p in parts), functools.partial(each, "start"), functools.partial(each, "finish"))

    def set_results(self, results):
        o0 = 0
        for p in self.parts:
            p.set_results(results[o0:o0 + len(p.out_shapes)])
            o0 += len(p.out_shapes)


def _join(*parts):
    return parts[0] if len(parts) == 1 else _JoinedExchange(list(parts))


def _pc(body, exchange=None, **kw):
    if exchange is None:
        return pl.pallas_call(body, **kw)
    grid = kw["grid"]
    single = not isinstance(kw["out_shape"], (tuple, list))
    out_shape = [kw["out_shape"]] if single else list(kw["out_shape"])
    out_specs = [kw["out_specs"]] if single else list(kw["out_specs"])
    in_specs = list(kw["in_specs"])
    scratch = list(kw.get("scratch_shapes", ()))
    n_in, n_out, n_scr = len(in_specs), len(out_shape), len(scratch)
    n_xi, n_xo = len(exchange.inputs), len(exchange.out_shapes)

    def wrapped(*refs):
        pos = [n_in, n_in + n_xi, n_in + n_xi + n_out, n_in + n_xi + n_out + n_xo]
        ins, x_in, outs, x_out = refs[:pos[0]], refs[pos[0]:pos[1]], refs[pos[1]:pos[2]], refs[pos[2]:pos[3]]
        scr = refs[pos[3]:pos[3] + n_scr]
        send_sems, recv_sems = refs[pos[3] + n_scr:]
        ids = [pl.program_id(a) for a in range(len(grid))]
        first = functools.reduce(jnp.logical_and, [i == 0 for i in ids])
        last = functools.reduce(jnp.logical_and, [i == g - 1 for i, g in zip(ids, grid)])

        @pl.when(first)
        def _():
            exchange.start(x_in, x_out, send_sems, recv_sems)

        body(*ins, *outs, *scr)

        @pl.when(last)
        def _():
            exchange.finish(x_in, x_out, send_sems, recv_sems)

    call = pl.pallas_call(
        wrapped, out_shape=tuple(out_shape) + tuple(exchange.out_shapes), grid=grid,
        in_specs=in_specs + [_ANY] * n_xi, out_specs=tuple(out_specs) + (_ANY,) * n_xo,
        scratch_shapes=scratch + [pltpu.SemaphoreType.DMA((exchange.n_sems,)), pltpu.SemaphoreType.DMA((exchange.n_sems,))],
        input_output_aliases={n_in + a: n_out + b for a, b in exchange.aliases.items()},
        compiler_params=_params(*(["arbitrary"] * len(grid))), name=kw["name"])

    def run(*args):
        res = call(*args, *exchange.inputs)
        exchange.set_results(res[n_out:])
        return res[0] if single else tuple(res[:n_out])

    return run


_ANY = pl.BlockSpec(memory_space=pl.ANY)


def _tile(n, pref, mult):
    best = None
    t = mult
    while t <= min(n, pref):
        if n % t == 0:
            best = t
        t += mult
    return n if best is None else best


def _sds(shape, dtype):
    return jax.ShapeDtypeStruct(shape, dtype)


def _vec_spec(d):
    return pl.BlockSpec((1, d), lambda *_: (0, 0))


def _norm_mod_fwd(x, g, shift, scale, name):
    s, d = x.shape
    tr = _tile(s, 512, 16)

    def body(x_ref, g_ref, sh_ref, sc_ref, h_ref):
        xv = x_ref[...]
        rstd = lax.rsqrt(jnp.mean(xv * xv, axis=-1, keepdims=True) + EPS)
        n = xv * rstd * g_ref[...]
        h_ref[...] = (n * (1.0 + sc_ref[...]) + sh_ref[...]).astype(BF16)

    row = pl.BlockSpec((tr, d), lambda i: (i, 0))
    return _pc(body, out_shape=_sds((s, d), BF16), grid=(s // tr,),
               in_specs=[row, _vec_spec(d), _vec_spec(d), _vec_spec(d)], out_specs=row,
               compiler_params=_params("parallel"), name=name)(x, g, shift, scale)


def _norm_mod_bwd(dh, x, g, scale, dres, name, exchange=None):
    s, d = x.shape
    tr = _tile(s, 256, 8)

    def body(dh_ref, x_ref, g_ref, sc_ref, dres_ref, dx_ref, dsh_ref, dsc_ref, dg_ref):
        @pl.when(pl.program_id(0) == 0)
        def _():
            dsh_ref[...] = jnp.zeros_like(dsh_ref)
            dsc_ref[...] = jnp.zeros_like(dsc_ref)
            dg_ref[...] = jnp.zeros_like(dg_ref)

        xv = x_ref[...]
        dhv = dh_ref[...]
        gv = g_ref[...]
        rstd = lax.rsqrt(jnp.mean(xv * xv, axis=-1, keepdims=True) + EPS)
        xhat = xv * rstd
        dn = dhv * (1.0 + sc_ref[...])
        dsh_ref[...] += jnp.sum(dhv, axis=0, keepdims=True)
        dsc_ref[...] += jnp.sum(dhv * (xhat * gv), axis=0, keepdims=True)
        dg_ref[...] += jnp.sum(dn * xhat, axis=0, keepdims=True)
        dxh = dn * gv
        proj = jnp.mean(dxh * xhat, axis=-1, keepdims=True)
        dx_ref[...] = dres_ref[...] + rstd * (dxh - xhat * proj)

    row = pl.BlockSpec((tr, d), lambda i: (i, 0))
    vec = _vec_spec(d)
    return _pc(body, exchange, out_shape=(_sds((s, d), F32), _sds((1, d), F32), _sds((1, d), F32), _sds((1, d), F32)),
               grid=(s // tr,), in_specs=[row, row, vec, vec, row], out_specs=(row, vec, vec, vec),
               compiler_params=_params("arbitrary"), name=name)(dh, x, g, scale, dres)


def _gate_bwd(dx, f, gate, name):
    s, d = dx.shape
    tr = _tile(s, 512, 16)

    def body(dx_ref, f_ref, gate_ref, df_ref, dg_ref):
        @pl.when(pl.program_id(0) == 0)
        def _():
            dg_ref[...] = jnp.zeros_like(dg_ref)

        dxv = dx_ref[...]
        df_ref[...] = (dxv * gate_ref[...]).astype(BF16)
        dg_ref[...] += jnp.sum(dxv * f_ref[...].astype(F32), axis=0, keepdims=True)

    row = pl.BlockSpec((tr, d), lambda i: (i, 0))
    vec = _vec_spec(d)
    return _pc(body, out_shape=(_sds((s, d), BF16), _sds((1, d), F32)), grid=(s // tr,),
               in_specs=[row, row, vec], out_specs=(row, vec),
               compiler_params=_params("arbitrary"), name=name)(dx, f, gate)


def _final_loss(x, g, target, name):
    s, d = x.shape
    tr = _tile(s, 256, 8)
    nsteps = s // tr

    def body(x_ref, g_ref, t_ref, dx_ref, loss_ref, dg_ref):
        i = pl.program_id(0)

        @pl.when(i == 0)
        def _():
            loss_ref[...] = jnp.zeros_like(loss_ref)
            dg_ref[...] = jnp.zeros_like(dg_ref)

        xv = x_ref[...]
        gv = g_ref[...]
        rstd = lax.rsqrt(jnp.mean(xv * xv, axis=-1, keepdims=True) + EPS)
        xhat = xv * rstd
        err = xhat * gv - t_ref[...]
        dy = err * (1.0 / d)
        loss_ref[...] += jnp.sum(0.5 * err * dy, axis=0, keepdims=True)
        dg_ref[...] += jnp.sum(dy * xhat, axis=0, keepdims=True)
        dxh = dy * gv
        proj = jnp.mean(dxh * xhat, axis=-1, keepdims=True)
        dx_ref[...] = rstd * (dxh - xhat * proj)

        @pl.when(i == nsteps - 1)
        def _():
            loss_ref[...] = jnp.broadcast_to(jnp.sum(loss_ref[...], axis=-1, keepdims=True), loss_ref.shape)

    row = pl.BlockSpec((tr, d), lambda i: (i, 0))
    vec = _vec_spec(d)
    return _pc(body, out_shape=(_sds((s, d), F32), _sds((1, d), F32), _sds((1, d), F32)), grid=(nsteps,),
               in_specs=[row, vec, row], out_specs=(row, vec, vec),
               compiler_params=_params("arbitrary"), name=name)(x, g, target)


def _mm(lhs, rhs, dims, out_dtype, name, res=None, gate=None, aux_dtype=None, exchange=None):
    if dims == "nn":
        (m, k), (k2, n) = lhs.shape, rhs.shape
    elif dims == "nt":
        (m, k), (n, k2) = lhs.shape, rhs.shape
    else:
        (k, m), (k2, n) = lhs.shape, rhs.shape
    assert k == k2, (lhs.shape, rhs.shape, dims)
    tn = _tile(n, 1024, LANES)
    tm = _tile(m, 512, LANES if dims == "tn" else 16)
    tk = _tile(k, 4096, LANES)
    nk = k // tk
    dn = {"nn": _NN, "nt": _NT, "tn": _TN}[dims]
    lhs_spec = (pl.BlockSpec((tk, tm), lambda i, j, kk: (kk, i)) if dims == "tn"
                else pl.BlockSpec((tm, tk), lambda i, j, kk: (i, kk)))
    rhs_spec = (pl.BlockSpec((tn, tk), lambda i, j, kk: (j, kk)) if dims == "nt"
                else pl.BlockSpec((tk, tn), lambda i, j, kk: (kk, j)))
    out_spec = pl.BlockSpec((tm, tn), lambda i, j, kk: (i, j))
    has_res, has_gate, has_aux = res is not None, gate is not None, aux_dtype is not None

    def body(*refs):
        refs = list(refs)
        l_ref, r_ref = refs[0], refs[1]
        pos = 2
        res_ref = gate_ref = aux_ref = None
        if has_res:
            res_ref = refs[pos]; pos += 1
        if has_gate:
            gate_ref = refs[pos]; pos += 1
        out_ref = refs[pos]; pos += 1
        if has_aux:
            aux_ref = refs[pos]; pos += 1
        acc_ref = refs[pos]
        kk = pl.program_id(2)
        part = lax.dot_general(l_ref[...], r_ref[...], dn, preferred_element_type=F32)

        @pl.when(kk == 0)
        def _():
            acc_ref[...] = part

        @pl.when(kk > 0)
        def _():
            acc_ref[...] += part

        @pl.when(kk == nk - 1)
        def _():
            acc = acc_ref[...]
            if has_aux:
                aux_ref[...] = acc.astype(aux_dtype)
            if has_gate:
                acc = acc * gate_ref[...]
            if has_res:
                acc = res_ref[...] + acc
            out_ref[...] = acc.astype(out_dtype)

    in_specs = [lhs_spec, rhs_spec]
    args = [lhs, rhs]
    if has_res:
        in_specs.append(out_spec); args.append(res)
    if has_gate:
        in_specs.append(pl.BlockSpec((1, tn), lambda i, j, kk: (0, j))); args.append(gate)
    out_shape = [_sds((m, n), out_dtype)]
    out_specs = [out_spec]
    if has_aux:
        out_shape.append(_sds((m, n), aux_dtype)); out_specs.append(out_spec)
    outs = _pc(body, exchange, out_shape=tuple(out_shape), grid=(m // tm, n // tn, nk), in_specs=in_specs,
               out_specs=tuple(out_specs), scratch_shapes=[pltpu.VMEM((tm, tn), F32)],
               compiler_params=_params("parallel", "parallel", "arbitrary"), name=name)(*args)
    return outs if has_aux else outs[0]


def _ffn_up(h, wg_t, wu_t, name, exchange=None):
    s, d = h.shape
    f = wg_t.shape[0]
    tm = _tile(s, 1024, 16)
    tn = _tile(f, 256, LANES)

    def body(h_ref, wg_ref, wu_ref, a_ref, u_ref, hid_ref):
        hv = h_ref[...]
        a = lax.dot_general(hv, wg_ref[...], _NT, preferred_element_type=F32)
        u = lax.dot_general(hv, wu_ref[...], _NT, preferred_element_type=F32)
        a_ref[...] = a.astype(BF16)
        u_ref[...] = u.astype(BF16)
        hid_ref[...] = (a * jax.nn.sigmoid(a) * u).astype(BF16)

    hs = pl.BlockSpec((tm, d), lambda i, j: (i, 0))
    ws = pl.BlockSpec((tn, d), lambda i, j: (j, 0))
    os_ = pl.BlockSpec((tm, tn), lambda i, j: (i, j))
    return _pc(body, exchange, out_shape=(_sds((s, f), BF16),) * 3, grid=(s // tm, f // tn),
               in_specs=[hs, ws, ws], out_specs=(os_, os_, os_),
               compiler_params=_params("parallel", "parallel"), name=name)(h, wg_t, wu_t)


def _ffn_dact(df, wd, a, u, name, exchange=None):
    s, d = df.shape
    f = wd.shape[0]
    tm = _tile(s, 1024, 16)
    tn = _tile(f, 256, LANES)

    def body(df_ref, wd_ref, a_ref, u_ref, da_ref, du_ref):
        dhid = lax.dot_general(df_ref[...], wd_ref[...], _NT, preferred_element_type=F32)
        av = a_ref[...].astype(F32)
        uv = u_ref[...].astype(F32)
        sig = jax.nn.sigmoid(av)
        da_ref[...] = (dhid * uv * (sig * (1.0 + av * (1.0 - sig)))).astype(BF16)
        du_ref[...] = (dhid * (av * sig)).astype(BF16)

    ds_ = pl.BlockSpec((tm, d), lambda i, j: (i, 0))
    ws = pl.BlockSpec((tn, d), lambda i, j: (j, 0))
    os_ = pl.BlockSpec((tm, tn), lambda i, j: (i, j))
    return _pc(body, exchange, out_shape=(_sds((s, f), BF16),) * 2, grid=(s // tm, f // tn),
               in_specs=[ds_, ws, os_, os_], out_specs=(os_, os_),
               compiler_params=_params("parallel", "parallel"), name=name)(df, wd, a, u)


def _split3(v):
    hi = v.astype(BF16)
    r1 = v - hi.astype(F32)
    mid = r1.astype(BF16)
    lo = (r1 - mid.astype(F32)).astype(BF16)
    return hi, mid, lo


def _dot3(v, mat):
    hi, mid, lo = _split3(v)
    out = lax.dot_general(hi, mat, _NN, preferred_element_type=F32)
    out += lax.dot_general(mid, mat, _NN, preferred_element_type=F32)
    out += lax.dot_general(lo, mat, _NN, preferred_element_type=F32)
    return out


def _forget_fwd(flog_t, bias, name):
    h, s = flog_t.shape
    blk = _tile(s, 512, LANES)
    tri = (jnp.arange(blk)[:, None] <= jnp.arange(blk)[None, :]).astype(BF16)

    def body(z_ref, b_ref, tri_ref, f_ref, carry):
        @pl.when(pl.program_id(0) == 0)
        def _():
            carry[...] = jnp.zeros_like(carry)

        z = z_ref[...] + b_ref[...]
        e = jnp.exp(-jnp.abs(z))
        w = 1.0 + e
        log1p_e = jnp.where(w == 1.0, e, jnp.log(w) * (e / (w - 1.0)))
        lf = jnp.minimum(z, 0.0) - log1p_e
        out = carry[...] + _dot3(lf, tri_ref[...])
        for j, piece in enumerate(_split3(out)):
            f_ref[j] = piece
        carry[...] = out[:, blk - 1:blk]

    zs = pl.BlockSpec((h, blk), lambda i: (0, i))
    return _pc(body, out_shape=_sds((3, h, s), BF16), grid=(s // blk,),
               in_specs=[zs, pl.BlockSpec((h, 1), lambda i: (0, 0)), pl.BlockSpec((blk, blk), lambda i: (0, 0))],
               out_specs=pl.BlockSpec((3, h, blk), lambda i: (0, 0, i)), scratch_shapes=[pltpu.VMEM((h, 1), F32)],
               compiler_params=_params("arbitrary"), name=name)(flog_t, bias, tri)


def _forget_bwd(df_t, flog_t, bias, name):
    h, s = flog_t.shape
    blk = _tile(s, 512, LANES)
    nb = s // blk
    tri = (jnp.arange(blk)[:, None] >= jnp.arange(blk)[None, :]).astype(BF16)

    def body(df_ref, z_ref, b_ref, tri_ref, dz_ref, db_ref, carry):
        @pl.when(pl.program_id(0) == 0)
        def _():
            carry[...] = jnp.zeros_like(carry)
            db_ref[...] = jnp.zeros_like(db_ref)

        rc = carry[...] + _dot3(df_ref[...], tri_ref[...])
        carry[...] = rc[:, 0:1]
        dz = rc * jax.nn.sigmoid(-(z_ref[...] + b_ref[...]))
        dz_ref[...] = dz
        db_ref[...] += jnp.sum(dz, axis=-1, keepdims=True)

    rev = pl.BlockSpec((h, blk), lambda i: (0, nb - 1 - i))
    col = pl.BlockSpec((h, 1), lambda i: (0, 0))
    return _pc(body, out_shape=(_sds((h, s), F32), _sds((h, 1), F32)), grid=(nb,),
               in_specs=[rev, rev, col, pl.BlockSpec((blk, blk), lambda i: (0, 0))],
               out_specs=(rev, col), scratch_shapes=[pltpu.VMEM((h, 1), F32)],
               compiler_params=_params("arbitrary"), name=name)(df_t, flog_t, bias, tri)


def _attn_tiles(s):
    return _tile(s, 512, LANES)


BIAS_ROWS = 16


def _attn_prep(qkv, f_pieces, name):
    s = qkv.shape[0]
    a_w = qkv.shape[1] // 3
    npair = a_w // LANES
    t = _attn_tiles(s)
    scale = 1.0 / math.sqrt(HEAD_DIM)

    six = f_pieces[:, :2 * npair].reshape(3, npair, 2, s).transpose(1, 3, 2, 0).reshape(npair, s, 6)
    feat = jnp.concatenate([six, jnp.ones((npair, s, 1), BF16), jnp.zeros((npair, s, BIAS_ROWS - 7), BF16)], axis=-1)
    place_q = [[0.0] * (2 * LANES) for _ in range(BIAS_ROWS)]
    place_k = [[0.0] * (2 * LANES) for _ in range(BIAS_ROWS)]
    for hh in range(2):
        b0 = hh * LANES + (HEAD_DIM if hh == 0 else 0)
        for j in range(3):
            place_q[3 * hh + j][b0 + j] = 1.0
            place_q[6][b0 + 3 + j] = 1.0
            place_k[6][b0 + j] = 1.0
            place_k[3 * hh + j][b0 + 3 + j] = -1.0
    place_q = jnp.array(place_q, BF16)
    place_k = jnp.array(place_k, BF16)

    def body(q_ref, k_ref, v_ref, f_ref, pq_ref, pk_ref, qa_ref, ka_ref, va_ref):
        lane = lax.broadcasted_iota(jnp.int32, (1, LANES), 1)
        q2 = (q_ref[...].astype(F32) * scale).astype(BF16)
        k2, v2 = k_ref[...], v_ref[...]
        qx = lax.dot_general(f_ref[0], pq_ref[...], _NN, preferred_element_type=F32).astype(BF16)
        kx = lax.dot_general(f_ref[0], pk_ref[...], _NN, preferred_element_type=F32).astype(BF16)
        for hh in range(2):
            real = (lane < HEAD_DIM) if hh == 0 else (lane >= HEAD_DIM)
            cols = slice(hh * LANES, (hh + 1) * LANES)
            qa_ref[:, cols] = jnp.where(real, q2, qx[:, cols])
            ka_ref[:, cols] = jnp.where(real, k2, kx[:, cols])
            va_ref[:, cols] = jnp.where(real, v2, jnp.zeros_like(v2))

    def col(off):
        return pl.BlockSpec((t, LANES), lambda p, i: (i, off + p))

    out = pl.BlockSpec((t, 2 * LANES), lambda p, i: (i, p))
    place = pl.BlockSpec((BIAS_ROWS, 2 * LANES), lambda p, i: (0, 0))
    return _pc(body, out_shape=(_sds((s, 2 * a_w), BF16),) * 3, grid=(npair, s // t),
               in_specs=[col(0), col(npair), col(2 * npair), pl.BlockSpec((1, t, BIAS_ROWS), lambda p, i: (p, i, 0)),
                         place, place],
               out_specs=(out, out, out), compiler_params=_params("parallel", "parallel"), name=name)(
                   qkv, qkv, qkv, feat, place_q, place_k)


def _attn_fwd(qa, ka, va, name, exchange=None):
    s = qa.shape[0]
    a_w = qa.shape[1] // 2
    npair = a_w // LANES
    t = _attn_tiles(s)
    nq = s // t

    def body(q_ref, k_ref, v_ref, o_ref, lse_ref, m_sc, l_sc, acc_sc):
        qi = pl.program_id(1)
        first = lax.broadcasted_iota(jnp.int32, (1, LANES), 1) < HEAD_DIM
        m_sc[...] = jnp.full_like(m_sc, NEG_BIG)
        l_sc[...] = jnp.zeros_like(l_sc)
        acc_sc[...] = jnp.zeros_like(acc_sc)

        def step(ki, diag):
            k_rows = pl.ds(pl.multiple_of(ki * t, t), t)
            m_old = m_sc[...]
            keep = None
            if diag:
                keep = (lax.broadcasted_iota(jnp.int32, (t, t), 0) >= lax.broadcasted_iota(jnp.int32, (t, t), 1))
            m_new, rs, pv = [], [], []
            for hh in range(2):
                cols = slice(hh * LANES, (hh + 1) * LANES)
                sc = lax.dot_general(q_ref[:, cols], k_ref[k_rows, cols], _NT, preferred_element_type=F32)
                if diag:
                    sc = jnp.where(keep, sc, NEG_BIG)
                mo = m_old[:, hh * HEAD_DIM:hh * HEAD_DIM + 1]
                mn = jnp.maximum(mo, jnp.max(sc, axis=1, keepdims=True))
                p = jnp.exp(sc - mn)
                m_new.append(mn)
                rs.append(jnp.sum(p, axis=1, keepdims=True))
                pv.append(lax.dot_general(p.astype(BF16), v_ref[k_rows, cols], _NN, preferred_element_type=F32))
            m2 = jnp.where(first, m_new[0], m_new[1])
            alpha = jnp.exp(m_old - m2)
            m_sc[...] = m2
            l_sc[...] = alpha * l_sc[...] + jnp.where(first, rs[0], rs[1])
            acc_sc[...] = alpha * acc_sc[...] + pv[0] + pv[1]

        def below_diagonal(ki, carry):
            step(ki, False)
            return carry

        lax.fori_loop(0, qi, below_diagonal, 0)
        step(qi, True)
        l2 = l_sc[...]
        o_ref[...] = acc_sc[...] / l2
        lse_ref[...] = m_sc[...] + jnp.log(l2)

    qs = pl.BlockSpec((t, 2 * LANES), lambda p, qi: (qi, p))
    ks = pl.BlockSpec((s, 2 * LANES), lambda p, qi: (0, p))
    os_ = pl.BlockSpec((t, LANES), lambda p, qi: (qi, p))
    return _pc(body, exchange, out_shape=(_sds((s, a_w), F32), _sds((s, a_w), F32)), grid=(npair, nq),
               in_specs=[qs, ks, ks], out_specs=(os_, os_),
               scratch_shapes=[pltpu.VMEM((t, LANES), F32)] * 3,
               compiler_params=_params("parallel", "arbitrary"), name=name)(qa, ka, va)


def _attn_bwd(qa, ka, va, do, o, lse, name, exchange=None):
    s = qa.shape[0]
    a_w = qa.shape[1] // 2
    npair = a_w // LANES
    t = _attn_tiles(s)
    nq = s // t
    scale = 1.0 / math.sqrt(HEAD_DIM)

    def body(q_ref, k_ref, v_ref, do_ref, o_ref, lse_ref, dq_ref, dk_ref, dv_ref, qx_ref, kx_ref, dk_sc, dv_sc, kx_sc):
        ki = pl.program_id(1)
        first = lax.broadcasted_iota(jnp.int32, (1, LANES), 1) < HEAD_DIM

        @pl.when(ki == 0)
        def _():
            dq_ref[...] = jnp.zeros_like(dq_ref)
            qx_ref[...] = jnp.zeros_like(qx_ref)

        def step(qi, diag):
            rows = pl.ds(pl.multiple_of(qi * t, t), t)
            do2 = do_ref[rows, :]
            lse2 = lse_ref[rows, :]
            dd = do2.astype(F32) * o_ref[rows, :]
            keep = None
            if diag:
                keep = (lax.broadcasted_iota(jnp.int32, (t, t), 0) >= lax.broadcasted_iota(jnp.int32, (t, t), 1))
            dq_h, dk_h, dv_h = [], [], []
            for hh in range(2):
                sel = first if hh == 0 else jnp.logical_not(first)
                cols = slice(hh * LANES, (hh + 1) * LANES)
                qh, kh, vh = q_ref[rows, cols], k_ref[:, cols], v_ref[:, cols]
                delta = jnp.sum(jnp.where(sel, dd, 0.0), axis=1, keepdims=True)
                sc = lax.dot_general(qh, kh, _NT, preferred_element_type=F32)
                if diag:
                    sc = jnp.where(keep, sc, NEG_BIG)
                p = jnp.exp(sc - lse2[:, hh * HEAD_DIM:hh * HEAD_DIM + 1])
                dp = lax.dot_general(do2, vh, _NT, preferred_element_type=F32)
                ds_b = (p * (dp - delta)).astype(BF16)
                dv_h.append(lax.dot_general(p.astype(BF16), do2, _TN, preferred_element_type=F32))
                dk_h.append(lax.dot_general(ds_b, qh, _TN, preferred_element_type=F32))
                dq_h.append(lax.dot_general(ds_b, kh, _NN, preferred_element_type=F32))
            dq_ref[rows, :] += jnp.where(first, dq_h[0], dq_h[1]) * scale
            qx_ref[rows, :] += jnp.where(first, dq_h[1], dq_h[0])
            dk_new = jnp.where(first, dk_h[0], dk_h[1])
            kx_new = jnp.where(first, dk_h[1], dk_h[0])
            dv_new = jnp.where(first, dv_h[0], dv_h[1])
            if diag:
                dk_sc[...] = dk_new
                kx_sc[...] = kx_new
                dv_sc[...] = dv_new
            else:
                dk_sc[...] += dk_new
                kx_sc[...] += kx_new
                dv_sc[...] += dv_new

        def below_diagonal(qi, carry):
            step(qi, False)
            return carry

        step(ki, True)
        lax.fori_loop(ki + 1, nq, below_diagonal, 0)
        dk_ref[...] = dk_sc[...].astype(BF16)
        dv_ref[...] = dv_sc[...].astype(BF16)
        kx_ref[...] = kx_sc[...]

    ks2 = pl.BlockSpec((t, 2 * LANES), lambda p, ki: (ki, p))
    qs2 = pl.BlockSpec((s, 2 * LANES), lambda p, ki: (0, p))
    whole = pl.BlockSpec((s, LANES), lambda p, ki: (0, p))
    kout = pl.BlockSpec((t, LANES), lambda p, ki: (ki, p))
    return _pc(body, exchange,
               out_shape=(_sds((s, a_w), F32), _sds((s, a_w), BF16), _sds((s, a_w), BF16), _sds((s, a_w), F32),
                          _sds((s, a_w), F32)),
               grid=(npair, nq), in_specs=[qs2, ks2, ks2, whole, whole, whole],
               out_specs=(whole, kout, kout, whole, kout),
               scratch_shapes=[pltpu.VMEM((t, LANES), F32)] * 3,
               compiler_params=_params("parallel", "arbitrary"), name=name)(qa, ka, va, do, o, lse)

def _decay_grads(qx, kx, name):
    s, a_w = qx.shape
    n_heads = a_w // HEAD_DIM
    tr = _tile(s, 512, 8)
    pick_q = [[0.0] * LANES for _ in range(a_w)]
    pick_k = [[0.0] * LANES for _ in range(a_w)]
    for h in range(n_heads):
        b0 = (h // 2) * LANES + (HEAD_DIM if h % 2 == 0 else 0)
        pick_q[b0][h] = 1.0
        pick_k[b0 + 3][h] = 1.0
    pick_q = jnp.array(pick_q, BF16)
    pick_k = jnp.array(pick_k, BF16)

    def body(qx_ref, kx_ref, pq_ref, pk_ref, o_ref):
        o_ref[...] = _dot3(qx_ref[...], pq_ref[...]) - _dot3(kx_ref[...], pk_ref[...])

    row = pl.BlockSpec((tr, a_w), lambda i: (i, 0))
    pick = pl.BlockSpec((a_w, LANES), lambda i: (0, 0))
    return _pc(body, out_shape=_sds((s, LANES), F32), grid=(s // tr,), in_specs=[row, row, pick, pick],
               out_specs=pl.BlockSpec((tr, LANES), lambda i: (i, 0)),
               compiler_params=_params("parallel"), name=name)(qx, kx, pick_q, pick_k)


def _shift_down(z, k, rows):
    return jnp.where(rows >= k, pltpu.roll(z, k, 0), 0.0)


def _shift_up(z, k, rows, n):
    return jnp.where(rows < n - k, pltpu.roll(z, n - k, 0), 0.0)


def _conv_fwd(bcx, conv_w, name):
    s = bcx.shape[0]
    cw = bcx.shape[1] // 3
    nb = cw // LANES

    def body(b_ref, c_ref, x_ref, w_ref, cv_ref):
        rows = lax.broadcasted_iota(jnp.int32, (s, LANES), 0)
        z = c_ref[...] * x_ref[...]
        w = w_ref[...]
        y = w[2:3, :] * z + w[1:2, :] * _shift_down(z, 1, rows) + w[0:1, :] * _shift_down(z, 2, rows)
        cv_ref[...] = b_ref[...] * y

    def col(off):
        return pl.BlockSpec((s, LANES), lambda j: (0, j + off))

    return _pc(body, out_shape=_sds((s, cw), F32), grid=(nb,),
               in_specs=[col(0), col(nb), col(2 * nb), pl.BlockSpec((CONV_K, LANES), lambda j: (0, j))],
               out_specs=col(0), compiler_params=_params("parallel"), name=name)(bcx, bcx, bcx, conv_w)


def _conv_bwd(dcv, bcx, conv_w, name):
    s = bcx.shape[0]
    cw = bcx.shape[1] // 3
    nb = cw // LANES

    def body(dcv_ref, b_ref, c_ref, x_ref, w_ref, db_ref, dc_ref, dxc_ref, dw_ref):
        rows = lax.broadcasted_iota(jnp.int32, (s, LANES), 0)
        cv_, xv = c_ref[...], x_ref[...]
        z = cv_ * xv
        w = w_ref[...]
        z1 = _shift_down(z, 1, rows)
        z2 = _shift_down(z, 2, rows)
        y = w[2:3, :] * z + w[1:2, :] * z1 + w[0:1, :] * z2
        dcvv = dcv_ref[...]
        db_ref[...] = (dcvv * y).astype(BF16)
        dy = dcvv * b_ref[...]
        dw_ref[0:1, :] = jnp.sum(dy * z2, axis=0, keepdims=True)
        dw_ref[1:2, :] = jnp.sum(dy * z1, axis=0, keepdims=True)
        dw_ref[2:3, :] = jnp.sum(dy * z, axis=0, keepdims=True)
        dz = w[2:3, :] * dy + w[1:2, :] * _shift_up(dy, 1, rows, s) + w[0:1, :] * _shift_up(dy, 2, rows, s)
        dc_ref[...] = (dz * xv).astype(BF16)
        dxc_ref[...] = (dz * cv_).astype(BF16)

    def col(off):
        return pl.BlockSpec((s, LANES), lambda j: (0, j + off))

    wspec = pl.BlockSpec((CONV_K, LANES), lambda j: (0, j))
    db, dc, dxc, dw = _pc(body, out_shape=(_sds((s, cw), BF16),) * 3 + (_sds((CONV_K, cw), F32),), grid=(nb,),
                          in_specs=[col(0), col(0), col(nb), col(2 * nb), wspec],
                          out_specs=(col(0), col(0), col(0), wspec),
                          compiler_params=_params("parallel"), name=name)(dcv, bcx, bcx, bcx, conv_w)
    return db, dc, dxc, dw


def _group_matrix():
    idx = jnp.arange(LANES) // HEAD_DIM
    return (idx[:, None] == idx[None, :]).astype(BF16)


def _group_sum(v, gmat):
    return _dot3(v, gmat)


def _gnorm_fwd(att, cv, gg, name):
    s, a_w = att.shape
    cw = cv.shape[1]
    d = a_w + cw
    tr = _tile(s, 512, 16)
    gmat = _group_matrix()

    def body(att_ref, cv_ref, gg_ref, gm_ref, yn_ref):
        gm = gm_ref[...]
        for c0 in range(0, d, LANES):
            y = att_ref[:, c0:c0 + LANES] if c0 < a_w else cv_ref[:, c0 - a_w:c0 - a_w + LANES]
            ms = _group_sum(y * y, gm) * (1.0 / HEAD_DIM)
            yn_ref[:, c0:c0 + LANES] = (y * lax.rsqrt(ms + EPS) * gg_ref[:, c0:c0 + LANES]).astype(BF16)

    return _pc(body, out_shape=_sds((s, d), BF16), grid=(s // tr,),
               in_specs=[pl.BlockSpec((tr, a_w), lambda i: (i, 0)), pl.BlockSpec((tr, cw), lambda i: (i, 0)),
                         _vec_spec(d), pl.BlockSpec((LANES, LANES), lambda i: (0, 0))],
               out_specs=pl.BlockSpec((tr, d), lambda i: (i, 0)),
               compiler_params=_params("parallel"), name=name)(att, cv, gg, gmat)


def _gnorm_bwd(dyn, att, cv, gg, name):
    s, a_w = att.shape
    cw = cv.shape[1]
    d = a_w + cw
    tr = _tile(s, 256, 16)
    gmat = _group_matrix()

    def body(dyn_ref, att_ref, cv_ref, gg_ref, gm_ref, datt_ref, dcv_ref, dgg_ref):
        @pl.when(pl.program_id(0) == 0)
        def _():
            dgg_ref[...] = jnp.zeros_like(dgg_ref)

        gm = gm_ref[...]
        for c0 in range(0, d, LANES):
            y = att_ref[:, c0:c0 + LANES] if c0 < a_w else cv_ref[:, c0 - a_w:c0 - a_w + LANES]
            dv = dyn_ref[:, c0:c0 + LANES]
            r = lax.rsqrt(_group_sum(y * y, gm) * (1.0 / HEAD_DIM) + EPS)
            xhat = y * r
            dgg_ref[:, c0:c0 + LANES] += jnp.sum(dv * xhat, axis=0, keepdims=True)
            dxh = dv * gg_ref[:, c0:c0 + LANES]
            proj = _group_sum(dxh * xhat, gm) * (1.0 / HEAD_DIM)
            dy = r * (dxh - xhat * proj)
            if c0 < a_w:
                datt_ref[:, c0:c0 + LANES] = dy.astype(BF16)
            else:
                dcv_ref[:, c0 - a_w:c0 - a_w + LANES] = dy

    return _pc(body, out_shape=(_sds((s, a_w), BF16), _sds((s, cw), F32), _sds((1, d), F32)), grid=(s // tr,),
               in_specs=[pl.BlockSpec((tr, d), lambda i: (i, 0)), pl.BlockSpec((tr, a_w), lambda i: (i, 0)),
                         pl.BlockSpec((tr, cw), lambda i: (i, 0)), _vec_spec(d),
                         pl.BlockSpec((LANES, LANES), lambda i: (0, 0))],
               out_specs=(pl.BlockSpec((tr, a_w), lambda i: (i, 0)), pl.BlockSpec((tr, cw), lambda i: (i, 0)),
                          _vec_spec(d)),
               compiler_params=_params("arbitrary"), name=name)(dyn, att, cv, gg, gmat)


def _adamw_math(w, g, m, v):
    m_new = ADAM_B1 * m + (1.0 - ADAM_B1) * g
    v_new = ADAM_B2 * v + (1.0 - ADAM_B2) * (g * g)
    m_hat = m_new / (1.0 - ADAM_B1 ** ADAM_STEP)
    v_hat = v_new / (1.0 - ADAM_B2 ** ADAM_STEP)
    delta = -ADAM_LR * (m_hat / (jnp.sqrt(v_hat) + ADAM_EPS) + ADAM_WD * w)
    return delta, m_new, v_new


def _row_tile(r, c):
    return _tile(r, max(8, ((1 << 18) // c) // 8 * 8), 8)


def _adamw(w, g, m, v, name):
    r, c = w.shape
    tr = _row_tile(r, c)

    def body(w_ref, g_ref, m_ref, v_ref, d_ref, mo_ref, vo_ref):
        d, mn, vn = _adamw_math(w_ref[...], g_ref[...], m_ref[...], v_ref[...])
        d_ref[...] = d
        mo_ref[...] = mn
        vo_ref[...] = vn

    spec = pl.BlockSpec((tr, c), lambda i: (i, 0))
    return _pc(body, out_shape=(_sds((r, c), F32),) * 3, grid=(r // tr,), in_specs=[spec] * 4,
               out_specs=(spec,) * 3, compiler_params=_params("parallel"), name=name)(w, g, m, v)


def _adamw_halves(w, mine, theirs, m, v, core, name):
    r2, c = w.shape
    r = r2 // 2
    assert mine.shape == (r, c) and theirs.shape == (r, c)
    tr = _row_tile(r, c)
    nb = r // tr

    def body(core_ref, w_ref, a_ref, b_ref, m_ref, v_ref, g_ref, d_ref, mo_ref, vo_ref):
        g = jnp.where(pl.program_id(0) == core_ref[0], a_ref[...], b_ref[...])
        d, mn, vn = _adamw_math(w_ref[...], g, m_ref[...], v_ref[...])
        g_ref[...] = g
        d_ref[...] = d
        mo_ref[...] = mn
        vo_ref[...] = vn

    full = pl.BlockSpec((tr, c), lambda h, i, core_ref: (h * nb + i, 0))
    half = pl.BlockSpec((tr, c), lambda h, i, core_ref: (i, 0))
    grid_spec = pltpu.PrefetchScalarGridSpec(
        num_scalar_prefetch=1, grid=(2, nb), in_specs=[full, half, half, full, full], out_specs=(full,) * 4)
    return _pc(body, out_shape=(_sds((r2, c), F32),) * 4, grid_spec=grid_spec,
               compiler_params=_params("parallel", "parallel"), name=name)(core, w, mine, theirs, m, v)


def _ada_fwd(c16, ada_w, ada_b, name):
    d, n = ada_w.shape
    tn = _tile(n, 768, LANES)

    def body(c_ref, w_ref, b_ref, o_ref):
        cv = c_ref[...]
        sc = (cv * jax.nn.sigmoid(cv)).astype(BF16)
        o_ref[...] = lax.dot_general(sc, w_ref[...].astype(BF16), _NN, preferred_element_type=F32) + b_ref[...]

    return _pc(body, out_shape=_sds((16, n), F32), grid=(n // tn,),
               in_specs=[pl.BlockSpec((16, d), lambda j: (0, 0)), pl.BlockSpec((d, tn), lambda j: (0, j)),
                         pl.BlockSpec((1, tn), lambda j: (0, j))],
               out_specs=pl.BlockSpec((16, tn), lambda j: (0, j)),
               compiler_params=_params("parallel"), name=name)(c16, ada_w, ada_b)


def _ada_update(c16_t, dmod16, w, m, v, name):
    r, c = w.shape
    tr = _row_tile(r, c)

    def body(c_ref, dm_ref, w_ref, m_ref, v_ref, g_ref, d_ref, mo_ref, vo_ref):
        cv = c_ref[...]
        sc = (cv * jax.nn.sigmoid(cv)).astype(BF16)
        g = lax.dot_general(sc, dm_ref[...].astype(BF16), _NN, preferred_element_type=F32)
        d, mn, vn = _adamw_math(w_ref[...], g, m_ref[...], v_ref[...])
        g_ref[...] = g
        d_ref[...] = d
        mo_ref[...] = mn
        vo_ref[...] = vn

    spec = pl.BlockSpec((tr, c), lambda i: (i, 0))
    return _pc(body, out_shape=(_sds((r, c), F32),) * 4, grid=(r // tr,),
               in_specs=[pl.BlockSpec((tr, 16), lambda i: (i, 0)), pl.BlockSpec((16, c), lambda i: (0, 0)),
                         spec, spec, spec],
               out_specs=(spec,) * 4, compiler_params=_params("parallel"), name=name)(c16_t, dmod16, w, m, v)


def _add_half(dw, recv, core, name):
    _, _, r, w = dw.shape
    tr = _tile(r, 256, 16)

    def body(core_ref, a_ref, b_ref, o_ref):
        o_ref[...] = (a_ref[...].astype(F32) + b_ref[...].astype(F32)).astype(BF16)

    grid_spec = pltpu.PrefetchScalarGridSpec(
        num_scalar_prefetch=1, grid=(N_CHIPS, r // tr),
        in_specs=[pl.BlockSpec((None, None, tr, w), lambda s, i, core_ref: (s, core_ref[0], i, 0)),
                  pl.BlockSpec((None, tr, w), lambda s, i, core_ref: (s, i, 0))],
        out_specs=pl.BlockSpec((None, tr, w), lambda s, i, core_ref: (s, i, 0)))
    return _pc(body, out_shape=_sds((N_CHIPS, r, w), BF16), grid_spec=grid_spec,
               compiler_params=_params("parallel", "parallel"), name=name)(core, dw, recv)


def _sum_chips(own, recv, chip, name):
    _, r, w = own.shape
    tr = _tile(r, 256, 16)

    def body(chip_ref, own_ref, p_ref, o_ref):
        acc = own_ref[...].astype(F32)
        for q in range(N_CHIPS - 1):
            acc = acc + p_ref[q].astype(F32)
        o_ref[...] = acc

    grid_spec = pltpu.PrefetchScalarGridSpec(
        num_scalar_prefetch=1, grid=(r // tr,),
        in_specs=[pl.BlockSpec((None, tr, w), lambda i, chip_ref: (chip_ref[0], i, 0)),
                  pl.BlockSpec((N_CHIPS - 1, tr, w), lambda i, chip_ref: (0, i, 0))],
        out_specs=pl.BlockSpec((tr, w), lambda i, chip_ref: (i, 0)))
    return _pc(body, out_shape=_sds((r, w), F32), grid_spec=grid_spec,
               compiler_params=_params("parallel"), name=name)(chip, own, recv)


def _sum_devices(parts, name):
    nd, r, w = parts.shape

    def body(p_ref, o_ref):
        acc = p_ref[0]
        for q in range(1, nd):
            acc = acc + p_ref[q]
        o_ref[...] = acc

    return _pc(body, out_shape=_sds((r, w), F32), name=name)(parts)


def _place():
    x, y, c = lax.axis_index("x"), lax.axis_index("y"), lax.axis_index("c")
    chips = [(1 - x, y), (x, 1 - y), (1 - x, 1 - y)]
    return x, y, c, chips


def _all_gather_small(blk, name):
    r, w = blk.shape

    def body(x_ref, out_ref, send_sems, recv_sems, local_sem):
        x, y, c, chips = _place()
        me, sibling = (x, y, c), (x, y, 1 - c)

        def rows(px, py, pc):
            return out_ref.at[pl.ds((4 * px + 2 * py + pc) * r, r), :]

        def copy(k, block, to, src=None):
            return pltpu.make_async_remote_copy(
                src_ref=rows(*block) if src is None else src, dst_ref=rows(*block),
                send_sem=send_sems.at[k], recv_sem=recv_sems.at[k], device_id=to, device_id_type=MESH)

        mine = pltpu.make_async_copy(x_ref, rows(*me), local_sem)
        mine.start()
        first = [copy(0, me, sibling, src=x_ref)]
        first += [copy(1 + j, me, (*chip, c), src=x_ref) for j, chip in enumerate(chips)]
        for cp in first:
            cp.start()
        passed = [copy(4 + j, (*chip, c), sibling) for j, chip in enumerate(chips)]
        for j, chip in enumerate(chips):
            copy(1 + j, (*chip, c), me).wait_recv()
            passed[j].start()
        copy(0, sibling, me).wait_recv()
        for j, chip in enumerate(chips):
            copy(4 + j, (*chip, 1 - c), me).wait_recv()
        for cp in first + passed:
            cp.wait_send()
        mine.wait()

    return _pc(body, out_shape=_sds((N_DEV * r, w), blk.dtype),
               in_specs=[pl.BlockSpec(memory_space=pltpu.VMEM)], out_specs=pl.BlockSpec(memory_space=pltpu.VMEM),
               scratch_shapes=[pltpu.SemaphoreType.DMA((7,)), pltpu.SemaphoreType.DMA((7,)), pltpu.SemaphoreType.DMA],
               name=name)(blk)


def _remote(src, dst, send_sems, recv_sems, k, to):
    return pltpu.make_async_remote_copy(src_ref=src, dst_ref=dst, send_sem=send_sems.at[k], recv_sem=recv_sems.at[k],
                                        device_id=to, device_id_type=MESH)


def _exchange_of(inputs, out_shapes, n_sems, copies, aliases=None):
    def start(src, dst, send_sems, recv_sems):
        for cp in copies(src, dst, send_sems, recv_sems)[0]:
            cp.start()

    def finish(src, dst, send_sems, recv_sems):
        sends, arrivals = copies(src, dst, send_sems, recv_sems)
        for cp in arrivals:
            cp.wait_recv()
        for cp in sends:
            cp.wait_send()

    return _Exchange(inputs, out_shapes, n_sems, start, finish, aliases)


def _run_exchange(ex, name):
    n_in, n_out = len(ex.inputs), len(ex.out_shapes)

    def body(*refs):
        src, dst = refs[:n_in], refs[n_in:n_in + n_out]
        send_sems, recv_sems = refs[n_in + n_out:]
        ex.start(src, dst, send_sems, recv_sems)
        ex.finish(src, dst, send_sems, recv_sems)

    ex.set_results(pl.pallas_call(
        body, out_shape=tuple(ex.out_shapes), in_specs=[_ANY] * n_in, out_specs=(_ANY,) * n_out,
        scratch_shapes=[pltpu.SemaphoreType.DMA((ex.n_sems,)), pltpu.SemaphoreType.DMA((ex.n_sems,))],
        input_output_aliases=ex.aliases, name=name)(*ex.inputs))


def _gather_ici_exchange(shards):
    n = len(shards)

    def copies(own, out, send_sems, recv_sems):
        x, y, c, chips = _place()
        my_chip = 2 * x + y
        sends, arrivals = [], []
        for i in range(n):
            for j, chip in enumerate(chips):
                to = (*chip, c)
                sends.append(_remote(own[i].at[c], out[i].at[my_chip, c], send_sems, recv_sems, 4 * i + j, to))
                arrivals.append(_remote(own[i].at[c], out[i].at[2 * chip[0] + chip[1], c], send_sems, recv_sems, 4 * i + j, to))
            whole = _remote(own[i], out[i].at[my_chip], send_sems, recv_sems, 4 * i + 3, (x, y, 1 - c))
            sends.append(whole)
            arrivals.append(whole)
        return sends, arrivals

    return _exchange_of(shards, [_sds((N_CHIPS,) + s.shape, s.dtype) for s in shards], 4 * n, copies)


def _gather_pass_exchange(gathered):
    n = len(gathered)

    def copies(src, dst, send_sems, recv_sems):
        x, y, c, chips = _place()
        sends, arrivals = [], []
        for i in range(n):
            for j, chip in enumerate(chips):
                idx = 2 * chip[0] + chip[1]
                sends.append(_remote(src[i].at[idx, c], dst[i].at[idx, c], send_sems, recv_sems, 3 * i + j, (x, y, 1 - c)))
                arrivals.append(_remote(src[i].at[idx, c], dst[i].at[idx, 1 - c], send_sems, recv_sems, 3 * i + j, (x, y, 1 - c)))
        return sends, arrivals

    return _exchange_of(gathered, [_sds(g.shape, g.dtype) for g in gathered], 3 * n, copies,
                        aliases={i: i for i in range(n)})


def _reduce_sibling_exchange(grads):
    n = len(grads)

    def copies(src, dst, send_sems, recv_sems):
        x, y, c, _ = _place()
        both = [_remote(src[i].at[s, 1 - c], dst[i].at[s], send_sems, recv_sems, N_CHIPS * i + s, (x, y, 1 - c))
                for i in range(n) for s in range(N_CHIPS)]
        return both, both

    return _exchange_of(grads, [_sds((N_CHIPS,) + g.shape[2:], g.dtype) for g in grads], N_CHIPS * n, copies)


def _reduce_chips_exchange(parts):
    n = len(parts)

    def copies(src, dst, send_sems, recv_sems):
        x, y, c, chips = _place()
        both = [_remote(src[i].at[2 * chip[0] + chip[1]], dst[i].at[j], send_sems, recv_sems, 3 * i + j, (*chip, c))
                for i in range(n) for j, chip in enumerate(chips)]
        return both, both

    return _exchange_of(parts, [_sds((N_CHIPS - 1,) + p.shape[1:], p.dtype) for p in parts], 3 * n, copies)


def _share_exchange(halves):
    n = len(halves)

    def copies(src, dst, send_sems, recv_sems):
        x, y, c, _ = _place()
        both = [_remote(src[i], dst[i], send_sems, recv_sems, i, (x, y, 1 - c)) for i in range(n)]
        return both, both

    return _exchange_of(halves, [_sds(h.shape, h.dtype) for h in halves], n, copies)


HEAD_ROWS = 16


class _WeightTraffic:
    def __init__(self, shards, core, chip):
        self.shards, self.core, self.chip = shards, core, chip
        self.gather, self.grads, self.reduce, self.chip_sums, self.half_sums, self.shared = {}, {}, {}, {}, {}, {}

    def gather_ici(self, grp):
        self.gather[grp] = _gather_ici_exchange(self.shards[grp])
        return self.gather[grp]

    def gather_pass(self, grp):
        self.gather[grp] = _gather_pass_exchange(self.gather[grp].results)
        return self.gather[grp]

    def weights(self, grp):
        return [g.reshape(-1, g.shape[-1]) for g in self.gather[grp].results]

    def reduce_sibling(self, grp, grads):
        self.grads[grp] = [g.reshape(N_CHIPS, 2, g.shape[0] // (2 * N_CHIPS), g.shape[1]) for g in grads]
        self.reduce[grp] = _reduce_sibling_exchange(self.grads[grp])
        return self.reduce[grp]

    def add_halves(self, grp):
        self.chip_sums[grp] = [_add_half(g, r, self.core, "add_half_%s%d" % (grp, i))
                               for i, (g, r) in enumerate(zip(self.grads[grp], self.reduce[grp].results))]

    def reduce_chips(self, grp):
        self.reduce[grp] = _reduce_chips_exchange(self.chip_sums[grp])
        return self.reduce[grp]

    def sum_chips(self, grp):
        self.half_sums[grp] = [_sum_chips(o, p, self.chip, "sum_chips_%s%d" % (grp, i))
                               for i, (o, p) in enumerate(zip(self.chip_sums[grp], self.reduce[grp].results))]

    def share(self, grp):
        self.shared[grp] = _share_exchange(self.half_sums[grp])
        return self.shared[grp]

    def totals(self, grp):
        return list(zip(self.half_sums[grp], self.shared[grp].results))


def _ffn_fwd(x, norm_g, shift, scale, gate, wg_t, wu_t, wd, tag, up_exchange=None, down_exchange=None):
    h = _norm_mod_fwd(x, norm_g, shift, scale, tag + "_norm_fwd")
    a, u, hid = _ffn_up(h, wg_t, wu_t, tag + "_up", exchange=up_exchange)
    x_out, f = _mm(hid, wd, "nn", F32, tag + "_down", res=x, gate=gate, aux_dtype=BF16,
                   exchange=down_exchange() if down_exchange else None)
    return x_out, (h, a, u, hid, f)


def _ffn_bwd(dx_out, x, saved, norm_g, scale, gate, wg_t, wu_t, wd, tag, traffic, dact_exchange=None, dw_exchange=None,
             finish_reduction=False):
    h, a, u, hid, f = saved
    df, dgate = _gate_bwd(dx_out, f, gate, tag + "_gate_bwd")
    da, du = _ffn_dact(df, wd, a, u, tag + "_dact", exchange=dact_exchange)
    dwd = _mm(hid, df, "tn", BF16, tag + "_dwd", exchange=dw_exchange() if dw_exchange else None)
    if not finish_reduction:
        dwg_t = _mm(da, h, "tn", BF16, tag + "_dwg")
        dwu_t = _mm(du, h, "tn", BF16, tag + "_dwu")
        dh = _mm(da, wg_t, "nn", F32, tag + "_dh_a", exchange=traffic.reduce_sibling(tag, [dwg_t, dwu_t, dwd]))
        traffic.add_halves(tag)
        dh = _mm(du, wu_t, "nn", F32, tag + "_dh_u", res=dh)
        dx, dshift, dscale, dnorm_g = _norm_mod_bwd(dh, x, norm_g, scale, dx_out, tag + "_norm_bwd")
        return dx, (dshift, dscale, dgate, dnorm_g)
    kd, kg, ku = tag + "_wd", tag + "_wg", tag + "_wu"
    dwg_t = _mm(da, h, "tn", BF16, tag + "_dwg", exchange=traffic.reduce_sibling(kd, [dwd]))
    traffic.add_halves(kd)
    dwu_t = _mm(du, h, "tn", BF16, tag + "_dwu",
                exchange=_join(traffic.reduce_chips(kd), traffic.reduce_sibling(kg, [dwg_t])))
    traffic.add_halves(kg)
    dh = _mm(da, wg_t, "nn", F32, tag + "_dh_a",
             exchange=_join(traffic.reduce_chips(kg), traffic.reduce_sibling(ku, [dwu_t])))
    traffic.add_halves(ku)
    traffic.sum_chips(kd)
    dh = _mm(du, wu_t, "nn", F32, tag + "_dh_u", res=dh, exchange=_join(traffic.reduce_chips(ku), traffic.share(kd)))
    traffic.sum_chips(kg)
    traffic.sum_chips(ku)
    dx, dshift, dscale, dnorm_g = _norm_mod_bwd(dh, x, norm_g, scale, dx_out, tag + "_norm_bwd",
                                                exchange=_join(traffic.share(kg), traffic.share(ku)))
    return dx, (dshift, dscale, dgate, dnorm_g)


def _layer_step(x, target, mod, gains, forget_bias, conv_w, traffic, att_w, in_shard, in_rows):
    sh1, sc1, g1, sh2, sc2, g2, sh3, sc3, g3 = mod
    norm1_g, norm2_g, norm3_g, final_g, group_g = gains
    s, d = x.shape
    n_heads = att_w // HEAD_DIM
    npair = n_heads // 2
    gate1, gate3 = 0.5 * g1, 0.5 * g3

    def split_w_in(w_in_pad):
        w_in_t = w_in_pad.reshape(N_CHIPS, in_rows, d)[:, :in_shard].reshape(N_CHIPS * in_shard, d)
        return (w_in_t[:3 * att_w], _pad_rows(w_in_t[3 * att_w:3 * att_w + n_heads], LANES), w_in_t[3 * att_w + n_heads:])

    _run_exchange(traffic.gather_ici("ffn1"), "gather_ffn1_ici")
    _run_exchange(traffic.gather_pass("ffn1"), "gather_ffn1_pass")
    wg1_t, wu1_t, wd1 = traffic.weights("ffn1")
    x1, saved1 = _ffn_fwd(x, norm1_g, sh1, sc1, gate1, wg1_t, wu1_t, wd1, "ffn1",
                          up_exchange=traffic.gather_ici("mix"), down_exchange=lambda: traffic.gather_pass("mix"))
    w_in_pad, w_out = traffic.weights("mix")
    wqkv_t, wf_t, wbcx_t = split_w_in(w_in_pad)

    h2 = _norm_mod_fwd(x1, norm2_g, sh2, sc2, "mix_norm_fwd")
    qkv = _mm(h2, wqkv_t, "nt", BF16, "mix_proj_qkv")
    bcx = _mm(h2, wbcx_t, "nt", F32, "mix_proj_bcx")
    flog = _mm(h2, wf_t, "nt", F32, "mix_proj_f")
    flog_t = jnp.pad(flog[:, :n_heads].T, ((0, HEAD_ROWS - n_heads), (0, 0)))
    bias_col = jnp.pad(forget_bias, (0, HEAD_ROWS - n_heads))[:, None]
    f_pieces = _forget_fwd(flog_t, bias_col, "forget_fwd")
    qa, ka, va = _attn_prep(qkv, f_pieces, "attn_prep")
    att, lse = _attn_fwd(qa, ka, va, "attn_fwd", exchange=traffic.gather_ici("ffn2"))
    cv = _conv_fwd(bcx, conv_w, "conv_fwd")
    yn = _gnorm_fwd(att, cv, group_g, "gnorm_fwd")
    x2, mix = _mm(yn, w_out, "nn", F32, "mix_out", res=x1, gate=g2, aux_dtype=BF16, exchange=traffic.gather_pass("ffn2"))
    wg2_t, wu2_t, wd2 = traffic.weights("ffn2")

    x3, saved3 = _ffn_fwd(x2, norm3_g, sh3, sc3, gate3, wg2_t, wu2_t, wd2, "ffn2")

    dx3, loss_row, dfinal_g = _final_loss(x3, final_g, target, "final_loss")

    dx2, (dsh3, dsc3, dgate3, dnorm3_g) = _ffn_bwd(
        dx3, x2, saved3, norm3_g, sc3, gate3, wg2_t, wu2_t, wd2, "ffn2", traffic)

    dmix, dg2 = _gate_bwd(dx2, mix, g2, "mix_gate_bwd")
    dyn = _mm(dmix, w_out, "nt", F32, "mix_out_dyn")
    dw_out = _mm(yn, dmix, "tn", BF16, "mix_out_dw")
    datt, dcv, dgroup_g = _gnorm_bwd(dyn, att, cv, group_g, "gnorm_bwd")
    db, dc, dxc, dconv_w = _conv_bwd(dcv, bcx, conv_w, "conv_bwd")
    dbcx = jnp.concatenate([db, dc, dxc], axis=1)
    dq, dk, dv, qx, kx = _attn_bwd(qa, ka, va, datt, att, lse, "attn_bwd", exchange=traffic.reduce_chips("ffn2"))
    traffic.sum_chips("ffn2")
    dqkv = jnp.concatenate([dq.astype(BF16), dk, dv], axis=1)
    df_t = _decay_grads(qx, kx, "decay_grads")[:, :HEAD_ROWS].T
    dflog_t, dbias_col = _forget_bwd(df_t, flog_t, bias_col, "forget_bwd")
    dflog = jnp.pad(dflog_t[:n_heads].T, ((0, 0), (0, LANES - n_heads))).astype(BF16)
    dh2 = _mm(dqkv, wqkv_t, "nn", F32, "mix_dh_qkv", exchange=traffic.share("ffn2"))
    dh2 = _mm(dbcx, wbcx_t, "nn", F32, "mix_dh_bcx", res=dh2)
    dh2 = _mm(dflog, wf_t, "nn", F32, "mix_dh_f", res=dh2)
    dwqkv_t = _mm(dqkv, h2, "tn", BF16, "mix_dw_qkv")
    dwbcx_t = _mm(dbcx, h2, "tn", BF16, "mix_dw_bcx")
    dwf_t = _mm(dflog, h2, "tn", BF16, "mix_dw_f")
    dw_in_t = jnp.concatenate([dwqkv_t, dwf_t[:n_heads], dwbcx_t], axis=0).reshape(N_CHIPS, in_shard, d)
    dw_in_t = jnp.pad(dw_in_t, ((0, 0), (0, in_rows - in_shard), (0, 0))).reshape(N_CHIPS * in_rows, d)
    dx1, dsh2, dsc2, dnorm2_g = _norm_mod_bwd(dh2, x1, norm2_g, sc2, dx2, "mix_norm_bwd",
                                              exchange=traffic.reduce_sibling("mix", [dw_in_t, dw_out]))
    traffic.add_halves("mix")

    def share_mix():
        traffic.sum_chips("mix")
        return traffic.share("mix")

    dx, (dsh1, dsc1, dgate1, dnorm1_g) = _ffn_bwd(
        dx1, x, saved1, norm1_g, sc1, gate1, wg1_t, wu1_t, wd1, "ffn1", traffic,
        dact_exchange=traffic.reduce_chips("mix"), dw_exchange=share_mix, finish_reduction=True)

    dmod = [dsh1, dsc1, 0.5 * dgate1, dsh2, dsc2, dg2, dsh3, dsc3, 0.5 * dgate3]
    dgains = [dnorm1_g, dnorm2_g, dnorm3_g, dfinal_g, dgroup_g]
    dbias = dbias_col[:n_heads, 0]
    return dx, loss_row, dmod, dgains, dbias, dconv_w


SMALL_ROWS = 24
ROW_GAINS, ROW_LOSS, ROW_FORGET, ROW_CONV, ROW_MOD = 0, 5, 6, 7, 10
PROW_ADA_B, PROW_GAINS, PROW_FORGET, PROW_CONV = 0, 9, 14, 15


def _round_up(n, m):
    return -(-n // m) * m


def _pad_rows(a, rows):
    return jnp.pad(a, ((0, rows - a.shape[0]), (0, 0)))


def _halves(a):
    return a.reshape(2, a.shape[0] // 2, a.shape[1])


def _rows_at(a, r0, total, width):
    return jnp.pad(a, ((r0, total - r0 - a.shape[0]), (0, width - a.shape[1])))


def kernel(x, c, ada_w, ada_b, norm1_g, ffn1_w_gate, ffn1_w_up, ffn1_w_down, norm2_g, w_in, forget_bias, conv_w, group_norm_g, w_out, norm3_g, ffn2_w_gate, ffn2_w_up, ffn2_w_down, final_g, loss_target, m_ada_w, m_ada_b, m_norm1_g, m_ffn1_w_gate, m_ffn1_w_up, m_ffn1_w_down, m_norm2_g, m_w_in, m_forget_bias, m_conv_w, m_group_norm_g, m_w_out, m_norm3_g, m_ffn2_w_gate, m_ffn2_w_up, m_ffn2_w_down, m_final_g, v_ada_w, v_ada_b, v_norm1_g, v_ffn1_w_gate, v_ffn1_w_up, v_ffn1_w_down, v_norm2_g, v_w_in, v_forget_bias, v_conv_w, v_group_norm_g, v_w_out, v_norm3_g, v_ffn2_w_gate, v_ffn2_w_up, v_ffn2_w_down, v_final_g):
    xi, yi, ci = lax.axis_index("x"), lax.axis_index("y"), lax.axis_index("c")
    chip = 2 * xi + yi
    dev = 4 * xi + 2 * yi + ci
    _, s, d = x.shape
    att_w = d // 2
    conv_width = d - att_w
    n_heads = att_w // HEAD_DIM
    in_shard = w_in.shape[1]
    in_rows = _round_up(in_shard, 32)
    cs = conv_w.shape[1]
    mod_shard = ada_w.shape[1]
    assert N_MOD * d == N_CHIPS * mod_shard and conv_width == N_CHIPS * cs and n_heads % 2 == 0

    pack0 = _rows_at(c, 0, 8, d) + _rows_at(conv_w, 1, 8, d)
    got0 = _all_gather_small(pack0, "gather_cond").reshape(N_DEV, 8, d)
    c16 = _pad_rows(got0[:, 0, :], 16)
    conv_full = got0[0::2, 1:1 + CONV_K, :cs].transpose(1, 0, 2).reshape(CONV_K, conv_width)

    ada_b_mine = lax.dynamic_slice(ada_b, (chip * mod_shard,), (mod_shard,))[None, :]
    mod_part = _ada_fwd(c16, ada_w, ada_b_mine, "ada_fwd")
    got1 = _all_gather_small(mod_part, "gather_mod").reshape(N_DEV, 16, mod_shard)
    mod_mine = lax.dynamic_index_in_dim(got1[0::2], dev, axis=1, keepdims=False).reshape(N_MOD, d)
    mod = [mod_mine[i:i + 1] for i in range(N_MOD)]

    def t_bf(w):
        return w.T.astype(BF16)

    shards = {"ffn1": [_halves(t_bf(ffn1_w_gate)), _halves(t_bf(ffn1_w_up)), _halves(ffn1_w_down.astype(BF16))],
              "mix": [_halves(_pad_rows(t_bf(w_in), in_rows)), _halves(w_out.astype(BF16))],
              "ffn2": [_halves(t_bf(ffn2_w_gate)), _halves(t_bf(ffn2_w_up)), _halves(ffn2_w_down.astype(BF16))]}
    core = ci.astype(jnp.int32).reshape(1)
    chip_arr = chip.astype(jnp.int32).reshape(1)
    traffic = _WeightTraffic(shards, core, chip_arr)

    gains = [g[None, :] for g in (norm1_g, norm2_g, norm3_g, final_g, group_norm_g)]
    dx, loss_row, dmod, dgains, dbias, dconv_w = _layer_step(
        x[0], loss_target[0], mod, gains, forget_bias, conv_full, traffic, att_w, in_shard, in_rows)

    pack = sum(_rows_at(g, ROW_GAINS + i, SMALL_ROWS, d) for i, g in enumerate(dgains))
    pack += _rows_at(loss_row, ROW_LOSS, SMALL_ROWS, d) + _rows_at(dbias[None, :], ROW_FORGET, SMALL_ROWS, d)
    pack += _rows_at(dconv_w, ROW_CONV, SMALL_ROWS, d)
    pack += sum(_rows_at(g, ROW_MOD + i, SMALL_ROWS, d) for i, g in enumerate(dmod))
    got2 = _all_gather_small(pack, "gather_small_grads").reshape(N_DEV, SMALL_ROWS, d)
    tot = _sum_devices(got2, "sum_small_grads")
    loss = tot[ROW_LOSS, 0]
    grad_ada_b = tot[ROW_MOD:ROW_MOD + N_MOD].reshape(N_MOD * d)
    grad_conv = lax.dynamic_slice(tot[ROW_CONV:ROW_CONV + CONV_K], (0, chip * cs), (CONV_K, cs))
    dmod_all = got2[:, ROW_MOD:ROW_MOD + N_MOD, :].reshape(N_DEV, N_MOD * d)
    dmod16 = _pad_rows(lax.dynamic_slice(dmod_all, (0, chip * mod_shard), (N_DEV, mod_shard)), 16)

    totals = (traffic.totals("ffn1_wg") + traffic.totals("ffn1_wu") + traffic.totals("ffn1_wd")
              + traffic.totals("mix") + traffic.totals("ffn2"))

    names = ("ffn1_w_gate", "ffn1_w_up", "ffn1_w_down", "w_in", "w_out", "ffn2_w_gate", "ffn2_w_up", "ffn2_w_down")
    transposed = ("ffn1_w_gate", "ffn1_w_up", "w_in", "ffn2_w_gate", "ffn2_w_up")
    params = {"ffn1_w_gate": (ffn1_w_gate, m_ffn1_w_gate, v_ffn1_w_gate), "ffn1_w_up": (ffn1_w_up, m_ffn1_w_up, v_ffn1_w_up),
              "ffn1_w_down": (ffn1_w_down, m_ffn1_w_down, v_ffn1_w_down), "w_in": (w_in, m_w_in, v_w_in),
              "w_out": (w_out, m_w_out, v_w_out), "ffn2_w_gate": (ffn2_w_gate, m_ffn2_w_gate, v_ffn2_w_gate),
              "ffn2_w_up": (ffn2_w_up, m_ffn2_w_up, v_ffn2_w_up), "ffn2_w_down": (ffn2_w_down, m_ffn2_w_down, v_ffn2_w_down)}
    out = {}
    for name_, (mine, theirs) in zip(names, totals):
        w, m, v = params[name_]
        if name_ in transposed:
            w, m, v = w.T, m.T, v.T
        if name_ == "w_in":
            both = jnp.where(ci == 0, jnp.concatenate([mine, theirs]), jnp.concatenate([theirs, mine]))[:in_shard]
            res = (both,) + tuple(_adamw(w, both, m, v, "adamw_" + name_))
        else:
            res = _adamw_halves(w, mine, theirs, m, v, core, "adamw_" + name_)
        out[name_] = tuple(r.T for r in res) if name_ in transposed else tuple(res)
    c16_t = c16.T
    out["ada_w"] = tuple(_ada_update(c16_t, dmod16, ada_w, m_ada_w, v_ada_w, "adamw_ada_w"))

    def small_pack(ada_b_, gains_, forget_, conv_):
        p = _rows_at(ada_b_.reshape(N_MOD, d), PROW_ADA_B, SMALL_ROWS, d)
        p += sum(_rows_at(g[None, :], PROW_GAINS + i, SMALL_ROWS, d) for i, g in enumerate(gains_))
        p += _rows_at(forget_[None, :], PROW_FORGET, SMALL_ROWS, d) + _rows_at(conv_, PROW_CONV, SMALL_ROWS, d)
        return p

    g_gains = [tot[ROW_GAINS + i] for i in range(5)]
    g_forget = tot[ROW_FORGET, :n_heads]
    sw = small_pack(ada_b, (norm1_g, norm2_g, norm3_g, final_g, group_norm_g), forget_bias, conv_w)
    sm = small_pack(m_ada_b, (m_norm1_g, m_norm2_g, m_norm3_g, m_final_g, m_group_norm_g), m_forget_bias, m_conv_w)
    sv = small_pack(v_ada_b, (v_norm1_g, v_norm2_g, v_norm3_g, v_final_g, v_group_norm_g), v_forget_bias, v_conv_w)
    sg = small_pack(grad_ada_b, g_gains, g_forget, grad_conv)
    small = (sg,) + tuple(_adamw(sw, sg, sm, sv, "adamw_small"))

    def unpack(p):
        r = {"ada_b": p[PROW_ADA_B:PROW_ADA_B + N_MOD].reshape(N_MOD * d), "forget_bias": p[PROW_FORGET, :n_heads],
             "conv_w": p[PROW_CONV:PROW_CONV + CONV_K, :cs]}
        for i, nm in enumerate(("norm1_g", "norm2_g", "norm3_g", "final_g", "group_norm_g")):
            r[nm] = p[PROW_GAINS + i]
        return r

    small = [unpack(p) for p in small]
    order = ("ada_w", "ada_b", "norm1_g", "ffn1_w_gate", "ffn1_w_up", "ffn1_w_down", "norm2_g", "w_in", "forget_bias",
             "conv_w", "group_norm_g", "w_out", "norm3_g", "ffn2_w_gate", "ffn2_w_up", "ffn2_w_down", "final_g")
    result = [loss, dx[None]]
    for k in range(4):
        result += [out[nm][k] if nm in out else small[k][nm] for nm in order]
    return tuple(result)
```

```python
import functools
import math

import jax
import jax.numpy as jnp
from jax import lax
from jax.experimental import pallas as pl
from jax.experimental.pallas import tpu as pltpu

F32 = jnp.float32
BF16 = jnp.bfloat16

HEAD_DIM = 64
CONV_K = 3
N_MOD = 9
EPS = 1e-6
ADAM_LR = 0.001
ADAM_B1 = 0.9
ADAM_B2 = 0.999
ADAM_EPS = 1e-08
ADAM_WD = 0.01
ADAM_STEP = 10

LANES = 128
N_CHIPS = 4
N_DEV = 8
VMEM_LIMIT_BYTES = 56 * 1024 * 1024
NEG_BIG = -1e30
MESH = pl.DeviceIdType.MESH

_NT = (((1,), (1,)), ((), ()))
_NN = (((1,), (0,)), ((), ()))
_TN = (((0,), (0,)), ((), ()))


def _params(*sem):
    return pltpu.CompilerParams(dimension_semantics=sem, vmem_limit_bytes=VMEM_LIMIT_BYTES)


class _Exchange:
    def __init__(self, inputs, out_shapes, n_sems, start, finish, aliases=None):
        self.inputs, self.out_shapes, self.n_sems = list(inputs), list(out_shapes), n_sems
        self.start, self.finish, self.aliases = start, finish, dict(aliases or {})
        self.results = None

    def set_results(self, results):
        self.results = list(results)


class _SemaphoreWindow:
    def __init__(self, sems, base):
        self.sems, self.base = sems, base
        self.at = self

    def __getitem__(self, k):
        return self.sems.at[self.base + k]


class _JoinedExchange(_Exchange):
    def __init__(self, parts):
        self.parts = parts
        assert all(not p.aliases for p in parts)

        def each(method, src, dst, send_sems, recv_sems):
            i0 = o0 = s0 = 0
            for p in parts:
                i1, o1 = i0 + len(p.inputs), o0 + len(p.out_shapes)
                getattr(p, method)(src[i0:i1], dst[o0:o1], _SemaphoreWindow(send_sems, s0), _SemaphoreWindow(recv_sems, s0))
                i0, o0, s0 = i1, o1, s0 + p.n_sems

        super().__init__([a for p in parts for a in p.inputs], [o for p in parts for o in p.out_shapes],
                         sum(p.n_sems for p in parts), functools.partial(each, "start"), functools.partial(each, "finish"))

    def set_results(self, results):
        o0 = 0
        for p in self.parts:
            p.set_results(results[o0:o0 + len(p.out_shapes)])
            o0 += len(p.out_shapes)


def _join(*parts):
    return parts[0] if len(parts) == 1 else _JoinedExchange(list(parts))


def _pc(body, exchange=None, **kw):
    if exchange is None:
        return pl.pallas_call(body, **kw)
    grid = kw["grid"]
    single = not isinstance(kw["out_shape"], (tuple, list))
    out_shape = [kw["out_shape"]] if single else list(kw["out_shape"])
    out_specs = [kw["out_specs"]] if single else list(kw["out_specs"])
    in_specs = list(kw["in_specs"])
    scratch = list(kw.get("scratch_shapes", ()))
    n_in, n_out, n_scr = len(in_specs), len(out_shape), len(scratch)
    n_xi, n_xo = len(exchange.inputs), len(exchange.out_shapes)

    def wrapped(*refs):
        pos = [n_in, n_in + n_xi, n_in + n_xi + n_out, n_in + n_xi + n_out + n_xo]
        ins, x_in, outs, x_out = refs[:pos[0]], refs[pos[0]:pos[1]], refs[pos[1]:pos[2]], refs[pos[2]:pos[3]]
        scr = refs[pos[3]:pos[3] + n_scr]
        send_sems, recv_sems = refs[pos[3] + n_scr:]
        ids = [pl.program_id(a) for a in range(len(grid))]
        first = functools.reduce(jnp.logical_and, [i == 0 for i in ids])
        last = functools.reduce(jnp.logical_and, [i == g - 1 for i, g in zip(ids, grid)])

        @pl.when(first)
        def _():
            exchange.start(x_in, x_out, send_sems, recv_sems)

        body(*ins, *outs, *scr)

        @pl.when(last)
        def _():
            exchange.finish(x_in, x_out, send_sems, recv_sems)

    call = pl.pallas_call(
        wrapped, out_shape=tuple(out_shape) + tuple(exchange.out_shapes), grid=grid,
        in_specs=in_specs + [_ANY] * n_xi, out_specs=tuple(out_specs) + (_ANY,) * n_xo,
        scratch_shapes=scratch + [pltpu.SemaphoreType.DMA((exchange.n_sems,)), pltpu.SemaphoreType.DMA((exchange.n_sems,))],
        input_output_aliases={n_in + a: n_out + b for a, b in exchange.aliases.items()},
        compiler_params=_params(*(["arbitrary"] * len(grid))), name=kw["name"])

    def run(*args):
        res = call(*args, *exchange.inputs)
        exchange.set_results(res[n_out:])
        return res[0] if single else tuple(res[:n_out])

    return run


_ANY = pl.BlockSpec(memory_space=pl.ANY)


def _tile(n, pref, mult):
    best = None
    t = mult
    while t <= min(n, pref):
        if n % t == 0:
            best = t
        t += mult
    return n if best is None else best


def _sds(shape, dtype):
    return jax.ShapeDtypeStruct(shape, dtype)


def _vec_spec(d):
    return pl.BlockSpec((1, d), lambda *_: (0, 0))


def _norm_mod_fwd(x, g, shift, scale, name):
    s, d = x.shape
    tr = _tile(s, 512, 16)

    def body(x_ref, g_ref, sh_ref, sc_ref, h_ref):
        xv = x_ref[...]
        rstd = lax.rsqrt(jnp.mean(xv * xv, axis=-1, keepdims=True) + EPS)
        n = xv * rstd * g_ref[...]
        h_ref[...] = (n * (1.0 + sc_ref[...]) + sh_ref[...]).astype(BF16)

    row = pl.BlockSpec((tr, d), lambda i: (i, 0))
    return _pc(body, out_shape=_sds((s, d), BF16), grid=(s // tr,),
               in_specs=[row, _vec_spec(d), _vec_spec(d), _vec_spec(d)], out_specs=row,
               compiler_params=_params("parallel"), name=name)(x, g, shift, scale)


def _through_gate(dx, f_ref, gate_ref, df_ref, dgate_ref):
    df_ref[...] = (dx * gate_ref[...]).astype(BF16)
    dgate_ref[...] += jnp.sum(dx * f_ref[...].astype(F32), axis=0, keepdims=True)


def _norm_mod_bwd(dh, x, g, scale, dres, name, f=None, gate=None, exchange=None):
    s, d = x.shape
    tr = _tile(s, 256, 16)
    gated = f is not None

    def body(dh_ref, x_ref, g_ref, sc_ref, dres_ref, *rest):
        f_ref, gate_ref = rest[:2] if gated else (None, None)
        dx_ref, dsh_ref, dsc_ref, dg_ref = rest[2:6] if gated else rest[:4]
        df_ref, dgate_ref = rest[6:8] if gated else (None, None)

        @pl.when(pl.program_id(0) == 0)
        def _():
            for ref in (dsh_ref, dsc_ref, dg_ref) + ((dgate_ref,) if gated else ()):
                ref[...] = jnp.zeros_like(ref)

        xv = x_ref[...]
        dhv = dh_ref[...]
        gv = g_ref[...]
        rstd = lax.rsqrt(jnp.mean(xv * xv, axis=-1, keepdims=True) + EPS)
        xhat = xv * rstd
        dn = dhv * (1.0 + sc_ref[...])
        dsh_ref[...] += jnp.sum(dhv, axis=0, keepdims=True)
        dsc_ref[...] += jnp.sum(dhv * (xhat * gv), axis=0, keepdims=True)
        dg_ref[...] += jnp.sum(dn * xhat, axis=0, keepdims=True)
        dxh = dn * gv
        proj = jnp.mean(dxh * xhat, axis=-1, keepdims=True)
        dx = dres_ref[...] + rstd * (dxh - xhat * proj)
        dx_ref[...] = dx
        if gated:
            _through_gate(dx, f_ref, gate_ref, df_ref, dgate_ref)

    row = pl.BlockSpec((tr, d), lambda i: (i, 0))
    vec = _vec_spec(d)
    out_shape = [_sds((s, d), F32), _sds((1, d), F32), _sds((1, d), F32), _sds((1, d), F32)]
    out_specs, in_specs, args = [row, vec, vec, vec], [row, row, vec, vec, row], [dh, x, g, scale, dres]
    if gated:
        out_shape += [_sds((s, d), BF16), _sds((1, d), F32)]
        out_specs += [row, vec]
        in_specs += [row, vec]
        args += [f, gate]
    return _pc(body, exchange, out_shape=tuple(out_shape), grid=(s // tr,), in_specs=in_specs,
               out_specs=tuple(out_specs), compiler_params=_params("arbitrary"), name=name)(*args)


def _final_loss(x, g, target, f, gate, name):
    s, d = x.shape
    tr = _tile(s, 256, 16)
    nsteps = s // tr

    def body(x_ref, g_ref, t_ref, f_ref, gate_ref, dx_ref, loss_ref, dg_ref, df_ref, dgate_ref):
        i = pl.program_id(0)

        @pl.when(i == 0)
        def _():
            loss_ref[...] = jnp.zeros_like(loss_ref)
            dg_ref[...] = jnp.zeros_like(dg_ref)
            dgate_ref[...] = jnp.zeros_like(dgate_ref)

        xv = x_ref[...]
        gv = g_ref[...]
        rstd = lax.rsqrt(jnp.mean(xv * xv, axis=-1, keepdims=True) + EPS)
        xhat = xv * rstd
        err = xhat * gv - t_ref[...]
        dy = err * (1.0 / d)
        loss_ref[...] += jnp.sum(0.5 * err * dy, axis=0, keepdims=True)
        dg_ref[...] += jnp.sum(dy * xhat, axis=0, keepdims=True)
        dxh = dy * gv
        proj = jnp.mean(dxh * xhat, axis=-1, keepdims=True)
        dx = rstd * (dxh - xhat * proj)
        dx_ref[...] = dx
        _through_gate(dx, f_ref, gate_ref, df_ref, dgate_ref)

        @pl.when(i == nsteps - 1)
        def _():
            loss_ref[...] = jnp.broadcast_to(jnp.sum(loss_ref[...], axis=-1, keepdims=True), loss_ref.shape)

    row = pl.BlockSpec((tr, d), lambda i: (i, 0))
    vec = _vec_spec(d)
    return _pc(body, out_shape=(_sds((s, d), F32), _sds((1, d), F32), _sds((1, d), F32), _sds((s, d), BF16), _sds((1, d), F32)),
               grid=(nsteps,), in_specs=[row, vec, row, row, vec], out_specs=(row, vec, vec, row, vec),
               compiler_params=_params("arbitrary"), name=name)(x, g, target, f, gate)


def _mm(lhs, rhs, dims, out_dtype, name, res=None, gate=None, aux_dtype=None, exchange=None):
    if dims == "nn":
        (m, k), (k2, n) = lhs.shape, rhs.shape
    elif dims == "nt":
        (m, k), (n, k2) = lhs.shape, rhs.shape
    else:
        (k, m), (k2, n) = lhs.shape, rhs.shape
    assert k == k2, (lhs.shape, rhs.shape, dims)
    tn = _tile(n, 1024, LANES)
    tm = _tile(m, 512, LANES if dims == "tn" else 16)
    tk = _tile(k, 4096, LANES)
    nk = k // tk
    dn = {"nn": _NN, "nt": _NT, "tn": _TN}[dims]
    lhs_spec = (pl.BlockSpec((tk, tm), lambda i, j, kk: (kk, i)) if dims == "tn"
                else pl.BlockSpec((tm, tk), lambda i, j, kk: (i, kk)))
    rhs_spec = (pl.BlockSpec((tn, tk), lambda i, j, kk: (j, kk)) if dims == "nt"
                else pl.BlockSpec((tk, tn), lambda i, j, kk: (kk, j)))
    out_spec = pl.BlockSpec((tm, tn), lambda i, j, kk: (i, j))
    has_res, has_gate, has_aux = res is not None, gate is not None, aux_dtype is not None

    def body(*refs):
        refs = list(refs)
        l_ref, r_ref = refs[0], refs[1]
        pos = 2
        res_ref = gate_ref = aux_ref = None
        if has_res:
            res_ref = refs[pos]; pos += 1
        if has_gate:
            gate_ref = refs[pos]; pos += 1
        out_ref = refs[pos]; pos += 1
        if has_aux:
            aux_ref = refs[pos]; pos += 1
        acc_ref = refs[pos]
        kk = pl.program_id(2)
        part = lax.dot_general(l_ref[...], r_ref[...], dn, preferred_element_type=F32)

        @pl.when(kk == 0)
        def _():
            acc_ref[...] = part

        @pl.when(kk > 0)
        def _():
            acc_ref[...] += part

        @pl.when(kk == nk - 1)
        def _():
            acc = acc_ref[...]
            if has_aux:
                aux_ref[...] = acc.astype(aux_dtype)
            if has_gate:
                acc = acc * gate_ref[...]
            if has_res:
                acc = res_ref[...] + acc
            out_ref[...] = acc.astype(out_dtype)

    in_specs = [lhs_spec, rhs_spec]
    args = [lhs, rhs]
    if has_res:
        in_specs.append(out_spec); args.append(res)
    if has_gate:
        in_specs.append(pl.BlockSpec((1, tn), lambda i, j, kk: (0, j))); args.append(gate)
    out_shape = [_sds((m, n), out_dtype)]
    out_specs = [out_spec]
    if has_aux:
        out_shape.append(_sds((m, n), aux_dtype)); out_specs.append(out_spec)
    outs = _pc(body, exchange, out_shape=tuple(out_shape), grid=(m // tm, n // tn, nk), in_specs=in_specs,
               out_specs=tuple(out_specs), scratch_shapes=[pltpu.VMEM((tm, tn), F32)],
               compiler_params=_params("parallel", "parallel", "arbitrary"), name=name)(*args)
    return outs if has_aux else outs[0]


def _ffn_up(h, wg_t, wu_t, name, exchange=None):
    s, d = h.shape
    f = wg_t.shape[0]
    tm = _tile(s, 1024, 16)
    tn = _tile(f, 256, LANES)

    def body(h_ref, wg_ref, wu_ref, a_ref, u_ref, hid_ref):
        hv = h_ref[...]
        a = lax.dot_general(hv, wg_ref[...], _NT, preferred_element_type=F32)
        u = lax.dot_general(hv, wu_ref[...], _NT, preferred_element_type=F32)
        a_ref[...] = a.astype(BF16)
        u_ref[...] = u.astype(BF16)
        hid_ref[...] = (a * jax.nn.sigmoid(a) * u).astype(BF16)

    hs = pl.BlockSpec((tm, d), lambda i, j: (i, 0))
    ws = pl.BlockSpec((tn, d), lambda i, j: (j, 0))
    os_ = pl.BlockSpec((tm, tn), lambda i, j: (i, j))
    return _pc(body, exchange, out_shape=(_sds((s, f), BF16),) * 3, grid=(s // tm, f // tn),
               in_specs=[hs, ws, ws], out_specs=(os_, os_, os_),
               compiler_params=_params("parallel", "parallel"), name=name)(h, wg_t, wu_t)


def _ffn_dact(df, wd, a, u, name, exchange=None):
    s, d = df.shape
    f = wd.shape[0]
    tm = _tile(s, 1024, 16)
    tn = _tile(f, 256, LANES)

    def body(df_ref, wd_ref, a_ref, u_ref, da_ref, du_ref):
        dhid = lax.dot_general(df_ref[...], wd_ref[...], _NT, preferred_element_type=F32)
        av = a_ref[...].astype(F32)
        uv = u_ref[...].astype(F32)
        sig = jax.nn.sigmoid(av)
        da_ref[...] = (dhid * uv * (sig * (1.0 + av * (1.0 - sig)))).astype(BF16)
        du_ref[...] = (dhid * (av * sig)).astype(BF16)

    ds_ = pl.BlockSpec((tm, d), lambda i, j: (i, 0))
    ws = pl.BlockSpec((tn, d), lambda i, j: (j, 0))
    os_ = pl.BlockSpec((tm, tn), lambda i, j: (i, j))
    return _pc(body, exchange, out_shape=(_sds((s, f), BF16),) * 2, grid=(s // tm, f // tn),
               in_specs=[ds_, ws, os_, os_], out_specs=(os_, os_),
               compiler_params=_params("parallel", "parallel"), name=name)(df, wd, a, u)


def _split3(v):
    hi = v.astype(BF16)
    r1 = v - hi.astype(F32)
    mid = r1.astype(BF16)
    lo = (r1 - mid.astype(F32)).astype(BF16)
    return hi, mid, lo


def _dot3(v, mat):
    hi, mid, lo = _split3(v)
    out = lax.dot_general(hi, mat, _NN, preferred_element_type=F32)
    out += lax.dot_general(mid, mat, _NN, preferred_element_type=F32)
    out += lax.dot_general(lo, mat, _NN, preferred_element_type=F32)
    return out


def _forget_fwd(flog_t, bias, name):
    h, s = flog_t.shape
    blk = _tile(s, 512, LANES)
    tri = (jnp.arange(blk)[:, None] <= jnp.arange(blk)[None, :]).astype(BF16)

    def body(z_ref, b_ref, tri_ref, f_ref, carry):
        @pl.when(pl.program_id(0) == 0)
        def _():
            carry[...] = jnp.zeros_like(carry)

        z = z_ref[...] + b_ref[...]
        e = jnp.exp(-jnp.abs(z))
        w = 1.0 + e
        log1p_e = jnp.where(w == 1.0, e, jnp.log(w) * (e / (w - 1.0)))
        lf = jnp.minimum(z, 0.0) - log1p_e
        out = carry[...] + _dot3(lf, tri_ref[...])
        for j, piece in enumerate(_split3(out)):
            f_ref[j] = piece
        carry[...] = out[:, blk - 1:blk]

    zs = pl.BlockSpec((h, blk), lambda i: (0, i))
    return _pc(body, out_shape=_sds((3, h, s), BF16), grid=(s // blk,),
               in_specs=[zs, pl.BlockSpec((h, 1), lambda i: (0, 0)), pl.BlockSpec((blk, blk), lambda i: (0, 0))],
               out_specs=pl.BlockSpec((3, h, blk), lambda i: (0, 0, i)), scratch_shapes=[pltpu.VMEM((h, 1), F32)],
               compiler_params=_params("arbitrary"), name=name)(flog_t, bias, tri)


def _forget_bwd(df_t, flog_t, bias, name):
    h, s = flog_t.shape
    blk = _tile(s, 512, LANES)
    nb = s // blk
    tri = (jnp.arange(blk)[:, None] >= jnp.arange(blk)[None, :]).astype(BF16)

    def body(df_ref, z_ref, b_ref, tri_ref, dz_ref, db_ref, carry):
        @pl.when(pl.program_id(0) == 0)
        def _():
            carry[...] = jnp.zeros_like(carry)
            db_ref[...] = jnp.zeros_like(db_ref)

        rc = carry[...] + _dot3(df_ref[...], tri_ref[...])
        carry[...] = rc[:, 0:1]
        dz = rc * jax.nn.sigmoid(-(z_ref[...] + b_ref[...]))
        dz_ref[...] = dz
        db_ref[...] += jnp.sum(dz, axis=-1, keepdims=True)

    rev = pl.BlockSpec((h, blk), lambda i: (0, nb - 1 - i))
    col = pl.BlockSpec((h, 1), lambda i: (0, 0))
    return _pc(body, out_shape=(_sds((h, s), F32), _sds((h, 1), F32)), grid=(nb,),
               in_specs=[rev, rev, col, pl.BlockSpec((blk, blk), lambda i: (0, 0))],
               out_specs=(rev, col), scratch_shapes=[pltpu.VMEM((h, 1), F32)],
               compiler_params=_params("arbitrary"), name=name)(df_t, flog_t, bias, tri)


def _attn_tiles(s):
    return _tile(s, 1024, LANES)


BIAS_ROWS = 16


def _attn_prep(qkv, f_pieces, name):
    s = qkv.shape[0]
    a_w = qkv.shape[1] // 3
    npair = a_w // LANES
    t = _attn_tiles(s)
    scale = 1.0 / math.sqrt(HEAD_DIM)

    six = f_pieces[:, :2 * npair].reshape(3, npair, 2, s).transpose(1, 3, 2, 0).reshape(npair, s, 6)
    feat = jnp.concatenate([six, jnp.ones((npair, s, 1), BF16), jnp.zeros((npair, s, BIAS_ROWS - 7), BF16)], axis=-1)
    place_q = [[0.0] * (2 * LANES) for _ in range(BIAS_ROWS)]
    place_k = [[0.0] * (2 * LANES) for _ in range(BIAS_ROWS)]
    for hh in range(2):
        b0 = hh * LANES + (HEAD_DIM if hh == 0 else 0)
        for j in range(3):
            place_q[3 * hh + j][b0 + j] = 1.0
            place_q[6][b0 + 3 + j] = 1.0
            place_k[6][b0 + j] = 1.0
            place_k[3 * hh + j][b0 + 3 + j] = -1.0
    place_q = jnp.array(place_q, BF16)
    place_k = jnp.array(place_k, BF16)

    def body(q_ref, k_ref, v_ref, f_ref, pq_ref, pk_ref, qa_ref, ka_ref, va_ref):
        lane = lax.broadcasted_iota(jnp.int32, (1, LANES), 1)
        q2 = (q_ref[...].astype(F32) * scale).astype(BF16)
        k2, v2 = k_ref[...], v_ref[...]
        qx = lax.dot_general(f_ref[0], pq_ref[...], _NN, preferred_element_type=F32).astype(BF16)
        kx = lax.dot_general(f_ref[0], pk_ref[...], _NN, preferred_element_type=F32).astype(BF16)
        for hh in range(2):
            real = (lane < HEAD_DIM) if hh == 0 else (lane >= HEAD_DIM)
            cols = slice(hh * LANES, (hh + 1) * LANES)
            qa_ref[:, cols] = jnp.where(real, q2, qx[:, cols])
            ka_ref[:, cols] = jnp.where(real, k2, kx[:, cols])
            va_ref[:, cols] = jnp.where(real, v2, jnp.zeros_like(v2))

    def col(off):
        return pl.BlockSpec((t, LANES), lambda p, i: (i, off + p))

    out = pl.BlockSpec((t, 2 * LANES), lambda p, i: (i, p))
    place = pl.BlockSpec((BIAS_ROWS, 2 * LANES), lambda p, i: (0, 0))
    return _pc(body, out_shape=(_sds((s, 2 * a_w), BF16),) * 3, grid=(npair, s // t),
               in_specs=[col(0), col(npair), col(2 * npair), pl.BlockSpec((1, t, BIAS_ROWS), lambda p, i: (p, i, 0)),
                         place, place],
               out_specs=(out, out, out), compiler_params=_params("parallel", "parallel"), name=name)(
                   qkv, qkv, qkv, feat, place_q, place_k)


def _attn_fwd(qa, ka, va, name, exchange=None):
    s = qa.shape[0]
    a_w = qa.shape[1] // 2
    npair = a_w // LANES
    t = _attn_tiles(s)
    nq = s // t

    def body(q_ref, k_ref, v_ref, o_ref, lse_ref, m_sc, l_sc, acc_sc):
        qi = pl.program_id(1)
        first = lax.broadcasted_iota(jnp.int32, (1, LANES), 1) < HEAD_DIM
        m_sc[...] = jnp.full_like(m_sc, NEG_BIG)
        l_sc[...] = jnp.zeros_like(l_sc)
        acc_sc[...] = jnp.zeros_like(acc_sc)

        def step(ki, diag):
            k_rows = pl.ds(pl.multiple_of(ki * t, t), t)
            m_old = m_sc[...]
            keep = None
            if diag:
                keep = (lax.broadcasted_iota(jnp.int32, (t, t), 0) >= lax.broadcasted_iota(jnp.int32, (t, t), 1))
            m_new, rs, pv = [], [], []
            for hh in range(2):
                cols = slice(hh * LANES, (hh + 1) * LANES)
                sc = lax.dot_general(q_ref[:, cols], k_ref[k_rows, cols], _NT, preferred_element_type=F32)
                if diag:
                    sc = jnp.where(keep, sc, NEG_BIG)
                mo = m_old[:, hh * HEAD_DIM:hh * HEAD_DIM + 1]
                mn = jnp.maximum(mo, jnp.max(sc, axis=1, keepdims=True))
                p = jnp.exp(sc - mn)
                m_new.append(mn)
                rs.append(jnp.sum(p, axis=1, keepdims=True))
                pv.append(lax.dot_general(p.astype(BF16), v_ref[k_rows, cols], _NN, preferred_element_type=F32))
            m2 = jnp.where(first, m_new[0], m_new[1])
            alpha = jnp.exp(m_old - m2)
            m_sc[...] = m2
            l_sc[...] = alpha * l_sc[...] + jnp.where(first, rs[0], rs[1])
            acc_sc[...] = alpha * acc_sc[...] + pv[0] + pv[1]

        def below_diagonal(ki, carry):
            step(ki, False)
            return carry

        lax.fori_loop(0, qi, below_diagonal, 0)
        step(qi, True)
        l2 = l_sc[...]
        o_ref[...] = acc_sc[...] / l2
        lse_ref[...] = m_sc[...] + jnp.log(l2)

    qs = pl.BlockSpec((t, 2 * LANES), lambda p, qi: (qi, p))
    ks = pl.BlockSpec((s, 2 * LANES), lambda p, qi: (0, p))
    os_ = pl.BlockSpec((t, LANES), lambda p, qi: (qi, p))
    return _pc(body, exchange, out_shape=(_sds((s, a_w), F32), _sds((s, a_w), F32)), grid=(npair, nq),
               in_specs=[qs, ks, ks], out_specs=(os_, os_),
               scratch_shapes=[pltpu.VMEM((t, LANES), F32)] * 3,
               compiler_params=_params("parallel", "arbitrary"), name=name)(qa, ka, va)


def _attn_bwd(qa, ka, va, do, o, lse, name, exchange=None):
    s = qa.shape[0]
    a_w = qa.shape[1] // 2
    npair = a_w // LANES
    t = _attn_tiles(s)
    nq = s // t
    scale = 1.0 / math.sqrt(HEAD_DIM)

    def body(q_ref, k_ref, v_ref, do_ref, o_ref, lse_ref, dq_ref, dk_ref, dv_ref, qx_ref, kx_ref, dk_sc, dv_sc, kx_sc):
        ki = pl.program_id(1)
        first = lax.broadcasted_iota(jnp.int32, (1, LANES), 1) < HEAD_DIM

        @pl.when(ki == 0)
        def _():
            dq_ref[...] = jnp.zeros_like(dq_ref)
            qx_ref[...] = jnp.zeros_like(qx_ref)

        def step(qi, diag):
            rows = pl.ds(pl.multiple_of(qi * t, t), t)
            do2 = do_ref[rows, :]
            lse2 = lse_ref[rows, :]
            dd = do2.astype(F32) * o_ref[rows, :]
            keep = None
            if diag:
                keep = (lax.broadcasted_iota(jnp.int32, (t, t), 0) >= lax.broadcasted_iota(jnp.int32, (t, t), 1))
            dq_h, dk_h, dv_h = [], [], []
            for hh in range(2):
                sel = first if hh == 0 else jnp.logical_not(first)
                cols = slice(hh * LANES, (hh + 1) * LANES)
                qh, kh, vh = q_ref[rows, cols], k_ref[:, cols], v_ref[:, cols]
                delta = jnp.sum(jnp.where(sel, dd, 0.0), axis=1, keepdims=True)
                sc = lax.dot_general(qh, kh, _NT, preferred_element_type=F32)
                if diag:
                    sc = jnp.where(keep, sc, NEG_BIG)
                p = jnp.exp(sc - lse2[:, hh * HEAD_DIM:hh * HEAD_DIM + 1])
                dp = lax.dot_general(do2, vh, _NT, preferred_element_type=F32)
                ds_b = (p * (dp - delta)).astype(BF16)
                dv_h.append(lax.dot_general(p.astype(BF16), do2, _TN, preferred_element_type=F32))
                dk_h.append(lax.dot_general(ds_b, qh, _TN, preferred_element_type=F32))
                dq_h.append(lax.dot_general(ds_b, kh, _NN, preferred_element_type=F32))
            dq_ref[rows, :] += jnp.where(first, dq_h[0], dq_h[1]) * scale
            qx_ref[rows, :] += jnp.where(first, dq_h[1], dq_h[0])
            dk_new = jnp.where(first, dk_h[0], dk_h[1])
            kx_new = jnp.where(first, dk_h[1], dk_h[0])
            dv_new = jnp.where(first, dv_h[0], dv_h[1])
            if diag:
                dk_sc[...] = dk_new
                kx_sc[...] = kx_new
                dv_sc[...] = dv_new
            else:
                dk_sc[...] += dk_new
                kx_sc[...] += kx_new
                dv_sc[...] += dv_new

        def below_diagonal(qi, carry):
            step(qi, False)
            return carry

        step(ki, True)
        lax.fori_loop(ki + 1, nq, below_diagonal, 0)
        dk_ref[...] = dk_sc[...].astype(BF16)
        dv_ref[...] = dv_sc[...].astype(BF16)
        kx_ref[...] = kx_sc[...]

    ks2 = pl.BlockSpec((t, 2 * LANES), lambda p, ki: (ki, p))
    qs2 = pl.BlockSpec((s, 2 * LANES), lambda p, ki: (0, p))
    whole = pl.BlockSpec((s, LANES), lambda p, ki: (0, p))
    kout = pl.BlockSpec((t, LANES), lambda p, ki: (ki, p))
    return _pc(body, exchange,
               out_shape=(_sds((s, a_w), F32), _sds((s, a_w), BF16), _sds((s, a_w), BF16), _sds((s, a_w), F32),
                          _sds((s, a_w), F32)),
               grid=(npair, nq), in_specs=[qs2, ks2, ks2, whole, whole, whole],
               out_specs=(whole, kout, kout, whole, kout),
               scratch_shapes=[pltpu.VMEM((t, LANES), F32)] * 3,
               compiler_params=_params("parallel", "arbitrary"), name=name)(qa, ka, va, do, o, lse)

def _decay_grads(qx, kx, name):
    s, a_w = qx.shape
    n_heads = a_w // HEAD_DIM
    tr = _tile(s, 512, 8)
    pick_q = [[0.0] * LANES for _ in range(a_w)]
    pick_k = [[0.0] * LANES for _ in range(a_w)]
    for h in range(n_heads):
        b0 = (h // 2) * LANES + (HEAD_DIM if h % 2 == 0 else 0)
        pick_q[b0][h] = 1.0
        pick_k[b0 + 3][h] = 1.0
    pick_q = jnp.array(pick_q, BF16)
    pick_k = jnp.array(pick_k, BF16)

    def body(qx_ref, kx_ref, pq_ref, pk_ref, o_ref):
        o_ref[...] = _dot3(qx_ref[...], pq_ref[...]) - _dot3(kx_ref[...], pk_ref[...])

    row = pl.BlockSpec((tr, a_w), lambda i: (i, 0))
    pick = pl.BlockSpec((a_w, LANES), lambda i: (0, 0))
    return _pc(body, out_shape=_sds((s, LANES), F32), grid=(s // tr,), in_specs=[row, row, pick, pick],
               out_specs=pl.BlockSpec((tr, LANES), lambda i: (i, 0)),
               compiler_params=_params("parallel"), name=name)(qx, kx, pick_q, pick_k)


def _shift_down(z, k, rows):
    return jnp.where(rows >= k, pltpu.roll(z, k, 0), 0.0)


def _shift_up(z, k, rows, n):
    return jnp.where(rows < n - k, pltpu.roll(z, n - k, 0), 0.0)


def _conv_fwd(bcx, conv_w, name):
    s = bcx.shape[0]
    cw = bcx.shape[1] // 3
    nb = cw // LANES

    def body(b_ref, c_ref, x_ref, w_ref, cv_ref):
        rows = lax.broadcasted_iota(jnp.int32, (s, LANES), 0)
        z = c_ref[...] * x_ref[...]
        w = w_ref[...]
        y = w[2:3, :] * z + w[1:2, :] * _shift_down(z, 1, rows) + w[0:1, :] * _shift_down(z, 2, rows)
        cv_ref[...] = b_ref[...] * y

    def col(off):
        return pl.BlockSpec((s, LANES), lambda j: (0, j + off))

    return _pc(body, out_shape=_sds((s, cw), F32), grid=(nb,),
               in_specs=[col(0), col(nb), col(2 * nb), pl.BlockSpec((CONV_K, LANES), lambda j: (0, j))],
               out_specs=col(0), compiler_params=_params("parallel"), name=name)(bcx, bcx, bcx, conv_w)


def _conv_bwd(dcv, bcx, conv_w, name):
    s = bcx.shape[0]
    cw = bcx.shape[1] // 3
    nb = cw // LANES

    def body(dcv_ref, b_ref, c_ref, x_ref, w_ref, db_ref, dc_ref, dxc_ref, dw_ref):
        rows = lax.broadcasted_iota(jnp.int32, (s, LANES), 0)
        cv_, xv = c_ref[...], x_ref[...]
        z = cv_ * xv
        w = w_ref[...]
        z1 = _shift_down(z, 1, rows)
        z2 = _shift_down(z, 2, rows)
        y = w[2:3, :] * z + w[1:2, :] * z1 + w[0:1, :] * z2
        dcvv = dcv_ref[...]
        db_ref[...] = (dcvv * y).astype(BF16)
        dy = dcvv * b_ref[...]
        dw_ref[0:1, :] = jnp.sum(dy * z2, axis=0, keepdims=True)
        dw_ref[1:2, :] = jnp.sum(dy * z1, axis=0, keepdims=True)
        dw_ref[2:3, :] = jnp.sum(dy * z, axis=0, keepdims=True)
        dz = w[2:3, :] * dy + w[1:2, :] * _shift_up(dy, 1, rows, s) + w[0:1, :] * _shift_up(dy, 2, rows, s)
        dc_ref[...] = (dz * xv).astype(BF16)
        dxc_ref[...] = (dz * cv_).astype(BF16)

    def col(off):
        return pl.BlockSpec((s, LANES), lambda j: (0, j + off))

    wspec = pl.BlockSpec((CONV_K, LANES), lambda j: (0, j))
    db, dc, dxc, dw = _pc(body, out_shape=(_sds((s, cw), BF16),) * 3 + (_sds((CONV_K, cw), F32),), grid=(nb,),
                          in_specs=[col(0), col(0), col(nb), col(2 * nb), wspec],
                          out_specs=(col(0), col(0), col(0), wspec),
                          compiler_params=_params("parallel"), name=name)(dcv, bcx, bcx, bcx, conv_w)
    return db, dc, dxc, dw


def _group_matrix():
    idx = jnp.arange(LANES) // HEAD_DIM
    return (idx[:, None] == idx[None, :]).astype(BF16)


def _group_sum(v, gmat):
    return _dot3(v, gmat)


def _gnorm_fwd(att, cv, gg, name):
    s, a_w = att.shape
    cw = cv.shape[1]
    d = a_w + cw
    tr = _tile(s, 512, 16)
    gmat = _group_matrix()

    def body(att_ref, cv_ref, gg_ref, gm_ref, yn_ref):
        gm = gm_ref[...]
        for c0 in range(0, d, LANES):
            y = att_ref[:, c0:c0 + LANES] if c0 < a_w else cv_ref[:, c0 - a_w:c0 - a_w + LANES]
            ms = _group_sum(y * y, gm) * (1.0 / HEAD_DIM)
            yn_ref[:, c0:c0 + LANES] = (y * lax.rsqrt(ms + EPS) * gg_ref[:, c0:c0 + LANES]).astype(BF16)

    return _pc(body, out_shape=_sds((s, d), BF16), grid=(s // tr,),
               in_specs=[pl.BlockSpec((tr, a_w), lambda i: (i, 0)), pl.BlockSpec((tr, cw), lambda i: (i, 0)),
                         _vec_spec(d), pl.BlockSpec((LANES, LANES), lambda i: (0, 0))],
               out_specs=pl.BlockSpec((tr, d), lambda i: (i, 0)),
               compiler_params=_params("parallel"), name=name)(att, cv, gg, gmat)


def _gnorm_bwd(dyn, att, cv, gg, name):
    s, a_w = att.shape
    cw = cv.shape[1]
    d = a_w + cw
    tr = _tile(s, 256, 16)
    gmat = _group_matrix()

    def body(dyn_ref, att_ref, cv_ref, gg_ref, gm_ref, datt_ref, dcv_ref, dgg_ref):
        @pl.when(pl.program_id(0) == 0)
        def _():
            dgg_ref[...] = jnp.zeros_like(dgg_ref)

        gm = gm_ref[...]
        for c0 in range(0, d, LANES):
            y = att_ref[:, c0:c0 + LANES] if c0 < a_w else cv_ref[:, c0 - a_w:c0 - a_w + LANES]
            dv = dyn_ref[:, c0:c0 + LANES]
            r = lax.rsqrt(_group_sum(y * y, gm) * (1.0 / HEAD_DIM) + EPS)
            xhat = y * r
            dgg_ref[:, c0:c0 + LANES] += jnp.sum(dv * xhat, axis=0, keepdims=True)
            dxh = dv * gg_ref[:, c0:c0 + LANES]
            proj = _group_sum(dxh * xhat, gm) * (1.0 / HEAD_DIM)
            dy = r * (dxh - xhat * proj)
            if c0 < a_w:
                datt_ref[:, c0:c0 + LANES] = dy.astype(BF16)
            else:
                dcv_ref[:, c0 - a_w:c0 - a_w + LANES] = dy

    return _pc(body, out_shape=(_sds((s, a_w), BF16), _sds((s, cw), F32), _sds((1, d), F32)), grid=(s // tr,),
               in_specs=[pl.BlockSpec((tr, d), lambda i: (i, 0)), pl.BlockSpec((tr, a_w), lambda i: (i, 0)),
                         pl.BlockSpec((tr, cw), lambda i: (i, 0)), _vec_spec(d),
                         pl.BlockSpec((LANES, LANES), lambda i: (0, 0))],
               out_specs=(pl.BlockSpec((tr, a_w), lambda i: (i, 0)), pl.BlockSpec((tr, cw), lambda i: (i, 0)),
                          _vec_spec(d)),
               compiler_params=_params("arbitrary"), name=name)(dyn, att, cv, gg, gmat)


def _adamw_math(w, g, m, v):
    m_new = ADAM_B1 * m + (1.0 - ADAM_B1) * g
    v_new = ADAM_B2 * v + (1.0 - ADAM_B2) * (g * g)
    m_hat = m_new / (1.0 - ADAM_B1 ** ADAM_STEP)
    v_hat = v_new / (1.0 - ADAM_B2 ** ADAM_STEP)
    delta = -ADAM_LR * (m_hat / (jnp.sqrt(v_hat) + ADAM_EPS) + ADAM_WD * w)
    return delta, m_new, v_new


def _row_tile(r, c):
    return _tile(r, max(8, ((1 << 18) // c) // 8 * 8), 8)


def _adamw(w, g, m, v, name):
    r, c = w.shape
    tr = _row_tile(r, c)

    def body(w_ref, g_ref, m_ref, v_ref, d_ref, mo_ref, vo_ref):
        d, mn, vn = _adamw_math(w_ref[...], g_ref[...], m_ref[...], v_ref[...])
        d_ref[...] = d
        mo_ref[...] = mn
        vo_ref[...] = vn

    spec = pl.BlockSpec((tr, c), lambda i: (i, 0))
    return _pc(body, out_shape=(_sds((r, c), F32),) * 3, grid=(r // tr,), in_specs=[spec] * 4,
               out_specs=(spec,) * 3, compiler_params=_params("parallel"), name=name)(w, g, m, v)


def _adamw_halves(w, mine, theirs, m, v, core, name):
    r2, c = w.shape
    r = r2 // 2
    assert mine.shape == (r, c) and theirs.shape == (r, c)
    tr = _row_tile(r, c)
    nb = r // tr

    def body(core_ref, w_ref, a_ref, b_ref, m_ref, v_ref, g_ref, d_ref, mo_ref, vo_ref):
        g = jnp.where(pl.program_id(0) == core_ref[0], a_ref[...], b_ref[...])
        d, mn, vn = _adamw_math(w_ref[...], g, m_ref[...], v_ref[...])
        g_ref[...] = g
        d_ref[...] = d
        mo_ref[...] = mn
        vo_ref[...] = vn

    full = pl.BlockSpec((tr, c), lambda h, i, core_ref: (h * nb + i, 0))
    half = pl.BlockSpec((tr, c), lambda h, i, core_ref: (i, 0))
    grid_spec = pltpu.PrefetchScalarGridSpec(
        num_scalar_prefetch=1, grid=(2, nb), in_specs=[full, half, half, full, full], out_specs=(full,) * 4)
    return _pc(body, out_shape=(_sds((r2, c), F32),) * 4, grid_spec=grid_spec,
               compiler_params=_params("parallel", "parallel"), name=name)(core, w, mine, theirs, m, v)


def _ada_fwd(c16, ada_w, ada_b, name):
    d, n = ada_w.shape
    tn = _tile(n, 768, LANES)

    def body(c_ref, w_ref, b_ref, o_ref):
        cv = c_ref[...]
        sc = (cv * jax.nn.sigmoid(cv)).astype(BF16)
        o_ref[...] = lax.dot_general(sc, w_ref[...].astype(BF16), _NN, preferred_element_type=F32) + b_ref[...]

    return _pc(body, out_shape=_sds((16, n), F32), grid=(n // tn,),
               in_specs=[pl.BlockSpec((16, d), lambda j: (0, 0)), pl.BlockSpec((d, tn), lambda j: (0, j)),
                         pl.BlockSpec((1, tn), lambda j: (0, j))],
               out_specs=pl.BlockSpec((16, tn), lambda j: (0, j)),
               compiler_params=_params("parallel"), name=name)(c16, ada_w, ada_b)


def _ada_update(c16_t, dmod16, w, m, v, name):
    r, c = w.shape
    tr = _row_tile(r, c)

    def body(c_ref, dm_ref, w_ref, m_ref, v_ref, g_ref, d_ref, mo_ref, vo_ref):
        cv = c_ref[...]
        sc = (cv * jax.nn.sigmoid(cv)).astype(BF16)
        g = lax.dot_general(sc, dm_ref[...].astype(BF16), _NN, preferred_element_type=F32)
        d, mn, vn = _adamw_math(w_ref[...], g, m_ref[...], v_ref[...])
        g_ref[...] = g
        d_ref[...] = d
        mo_ref[...] = mn
        vo_ref[...] = vn

    spec = pl.BlockSpec((tr, c), lambda i: (i, 0))
    return _pc(body, out_shape=(_sds((r, c), F32),) * 4, grid=(r // tr,),
               in_specs=[pl.BlockSpec((tr, 16), lambda i: (i, 0)), pl.BlockSpec((16, c), lambda i: (0, 0)),
                         spec, spec, spec],
               out_specs=(spec,) * 4, compiler_params=_params("parallel"), name=name)(c16_t, dmod16, w, m, v)


def _add_half(dw, recv, core, name):
    _, _, r, w = dw.shape
    tr = _tile(r, 512, 16)

    def body(core_ref, a_ref, b_ref, o_ref):
        o_ref[...] = (a_ref[...].astype(F32) + b_ref[...].astype(F32)).astype(BF16)

    grid_spec = pltpu.PrefetchScalarGridSpec(
        num_scalar_prefetch=1, grid=(N_CHIPS, r // tr),
        in_specs=[pl.BlockSpec((None, None, tr, w), lambda s, i, core_ref: (s, core_ref[0], i, 0)),
                  pl.BlockSpec((None, tr, w), lambda s, i, core_ref: (s, i, 0))],
        out_specs=pl.BlockSpec((None, tr, w), lambda s, i, core_ref: (s, i, 0)))
    return _pc(body, out_shape=_sds((N_CHIPS, r, w), BF16), grid_spec=grid_spec,
               compiler_params=_params("parallel", "parallel"), name=name)(core, dw, recv)


def _sum_chips(own, recv, chip, name):
    _, r, w = own.shape
    tr = _tile(r, 512, 16)

    def body(chip_ref, own_ref, p_ref, o_ref):
        acc = own_ref[...].astype(F32)
        for q in range(N_CHIPS - 1):
            acc = acc + p_ref[q].astype(F32)
        o_ref[...] = acc

    grid_spec = pltpu.PrefetchScalarGridSpec(
        num_scalar_prefetch=1, grid=(r // tr,),
        in_specs=[pl.BlockSpec((None, tr, w), lambda i, chip_ref: (chip_ref[0], i, 0)),
                  pl.BlockSpec((N_CHIPS - 1, tr, w), lambda i, chip_ref: (0, i, 0))],
        out_specs=pl.BlockSpec((tr, w), lambda i, chip_ref: (i, 0)))
    return _pc(body, out_shape=_sds((r, w), F32), grid_spec=grid_spec,
               compiler_params=_params("parallel"), name=name)(chip, own, recv)


def _sum_devices(parts, name):
    nd, r, w = parts.shape

    def body(p_ref, o_ref):
        acc = p_ref[0]
        for q in range(1, nd):
            acc = acc + p_ref[q]
        o_ref[...] = acc

    return _pc(body, out_shape=_sds((r, w), F32), name=name)(parts)


def _place():
    x, y, c = lax.axis_index("x"), lax.axis_index("y"), lax.axis_index("c")
    chips = [(1 - x, y), (x, 1 - y), (1 - x, 1 - y)]
    return x, y, c, chips


def _all_gather_small(blk, name):
    r, w = blk.shape

    def body(x_ref, out_ref, send_sems, recv_sems, local_sem):
        x, y, c, chips = _place()
        me, sibling = (x, y, c), (x, y, 1 - c)

        def rows(px, py, pc):
            return out_ref.at[pl.ds((4 * px + 2 * py + pc) * r, r), :]

        def copy(k, block, to, src=None):
            return pltpu.make_async_remote_copy(
                src_ref=rows(*block) if src is None else src, dst_ref=rows(*block),
                send_sem=send_sems.at[k], recv_sem=recv_sems.at[k], device_id=to, device_id_type=MESH)

        mine = pltpu.make_async_copy(x_ref, rows(*me), local_sem)
        mine.start()
        first = [copy(0, me, sibling, src=x_ref)]
        first += [copy(1 + j, me, (*chip, c), src=x_ref) for j, chip in enumerate(chips)]
        for cp in first:
            cp.start()
        passed = [copy(4 + j, (*chip, c), sibling) for j, chip in enumerate(chips)]
        for j, chip in enumerate(chips):
            copy(1 + j, (*chip, c), me).wait_recv()
            passed[j].start()
        copy(0, sibling, me).wait_recv()
        for j, chip in enumerate(chips):
            copy(4 + j, (*chip, 1 - c), me).wait_recv()
        for cp in first + passed:
            cp.wait_send()
        mine.wait()

    return _pc(body, out_shape=_sds((N_DEV * r, w), blk.dtype),
               in_specs=[pl.BlockSpec(memory_space=pltpu.VMEM)], out_specs=pl.BlockSpec(memory_space=pltpu.VMEM),
               scratch_shapes=[pltpu.SemaphoreType.DMA((7,)), pltpu.SemaphoreType.DMA((7,)), pltpu.SemaphoreType.DMA],
               name=name)(blk)


def _remote(src, dst, send_sems, recv_sems, k, to):
    return pltpu.make_async_remote_copy(src_ref=src, dst_ref=dst, send_sem=send_sems.at[k], recv_sem=recv_sems.at[k],
                                        device_id=to, device_id_type=MESH)


def _exchange_of(inputs, out_shapes, n_sems, copies, aliases=None):
    def start(src, dst, send_sems, recv_sems):
        for cp in copies(src, dst, send_sems, recv_sems)[0]:
            cp.start()

    def finish(src, dst, send_sems, recv_sems):
        sends, arrivals = copies(src, dst, send_sems, recv_sems)
        for cp in arrivals:
            cp.wait_recv()
        for cp in sends:
            cp.wait_send()

    return _Exchange(inputs, out_shapes, n_sems, start, finish, aliases)


def _run_exchange(ex, name):
    n_in, n_out = len(ex.inputs), len(ex.out_shapes)

    def body(*refs):
        src, dst = refs[:n_in], refs[n_in:n_in + n_out]
        send_sems, recv_sems = refs[n_in + n_out:]
        ex.start(src, dst, send_sems, recv_sems)
        ex.finish(src, dst, send_sems, recv_sems)

    ex.set_results(pl.pallas_call(
        body, out_shape=tuple(ex.out_shapes), in_specs=[_ANY] * n_in, out_specs=(_ANY,) * n_out,
        scratch_shapes=[pltpu.SemaphoreType.DMA((ex.n_sems,)), pltpu.SemaphoreType.DMA((ex.n_sems,))],
        input_output_aliases=ex.aliases, name=name)(*ex.inputs))


def _gather_ici_exchange(shards):
    n = len(shards)

    def copies(own, out, send_sems, recv_sems):
        x, y, c, chips = _place()
        my_chip = 2 * x + y
        sends, arrivals = [], []
        for i in range(n):
            for j, chip in enumerate(chips):
                to = (*chip, c)
                sends.append(_remote(own[i].at[c], out[i].at[my_chip, c], send_sems, recv_sems, 4 * i + j, to))
                arrivals.append(_remote(own[i].at[c], out[i].at[2 * chip[0] + chip[1], c], send_sems, recv_sems, 4 * i + j, to))
            whole = _remote(own[i], out[i].at[my_chip], send_sems, recv_sems, 4 * i + 3, (x, y, 1 - c))
            sends.append(whole)
            arrivals.append(whole)
        return sends, arrivals

    return _exchange_of(shards, [_sds((N_CHIPS,) + s.shape, s.dtype) for s in shards], 4 * n, copies)


def _gather_pass_exchange(gathered):
    n = len(gathered)

    def copies(src, dst, send_sems, recv_sems):
        x, y, c, chips = _place()
        sends, arrivals = [], []
        for i in range(n):
            for j, chip in enumerate(chips):
                idx = 2 * chip[0] + chip[1]
                sends.append(_remote(src[i].at[idx, c], dst[i].at[idx, c], send_sems, recv_sems, 3 * i + j, (x, y, 1 - c)))
                arrivals.append(_remote(src[i].at[idx, c], dst[i].at[idx, 1 - c], send_sems, recv_sems, 3 * i + j, (x, y, 1 - c)))
        return sends, arrivals

    return _exchange_of(gathered, [_sds(g.shape, g.dtype) for g in gathered], 3 * n, copies,
                        aliases={i: i for i in range(n)})


def _reduce_sibling_exchange(grads):
    n = len(grads)

    def copies(src, dst, send_sems, recv_sems):
        x, y, c, _ = _place()
        both = [_remote(src[i].at[s, 1 - c], dst[i].at[s], send_sems, recv_sems, N_CHIPS * i + s, (x, y, 1 - c))
                for i in range(n) for s in range(N_CHIPS)]
        return both, both

    return _exchange_of(grads, [_sds((N_CHIPS,) + g.shape[2:], g.dtype) for g in grads], N_CHIPS * n, copies)


def _reduce_chips_exchange(parts):
    n = len(parts)

    def copies(src, dst, send_sems, recv_sems):
        x, y, c, chips = _place()
        both = [_remote(src[i].at[2 * chip[0] + chip[1]], dst[i].at[j], send_sems, recv_sems, 3 * i + j, (*chip, c))
                for i in range(n) for j, chip in enumerate(chips)]
        return both, both

    return _exchange_of(parts, [_sds((N_CHIPS - 1,) + p.shape[1:], p.dtype) for p in parts], 3 * n, copies)


def _share_exchange(halves):
    n = len(halves)

    def copies(src, dst, send_sems, recv_sems):
        x, y, c, _ = _place()
        both = [_remote(src[i], dst[i], send_sems, recv_sems, i, (x, y, 1 - c)) for i in range(n)]
        return both, both

    return _exchange_of(halves, [_sds(h.shape, h.dtype) for h in halves], n, copies)


HEAD_ROWS = 16


class _WeightTraffic:
    def __init__(self, shards, core, chip):
        self.shards, self.core, self.chip = shards, core, chip
        self.gather, self.grads, self.reduce, self.chip_sums, self.half_sums, self.shared = {}, {}, {}, {}, {}, {}

    def gather_ici(self, grp):
        self.gather[grp] = _gather_ici_exchange(self.shards[grp])
        return self.gather[grp]

    def gather_pass(self, grp):
        self.gather[grp] = _gather_pass_exchange(self.gather[grp].results)
        return self.gather[grp]

    def weights(self, grp):
        return [g.reshape(-1, g.shape[-1]) for g in self.gather[grp].results]

    def reduce_sibling(self, grp, grads):
        self.grads[grp] = [g.reshape(N_CHIPS, 2, g.shape[0] // (2 * N_CHIPS), g.shape[1]) for g in grads]
        self.reduce[grp] = _reduce_sibling_exchange(self.grads[grp])
        return self.reduce[grp]

    def add_halves(self, grp):
        self.chip_sums[grp] = [_add_half(g, r, self.core, "add_half_%s%d" % (grp, i))
                               for i, (g, r) in enumerate(zip(self.grads[grp], self.reduce[grp].results))]

    def reduce_chips(self, grp):
        self.reduce[grp] = _reduce_chips_exchange(self.chip_sums[grp])
        return self.reduce[grp]

    def sum_chips(self, grp):
        self.half_sums[grp] = [_sum_chips(o, p, self.chip, "sum_chips_%s%d" % (grp, i))
                               for i, (o, p) in enumerate(zip(self.chip_sums[grp], self.reduce[grp].results))]

    def share(self, grp):
        self.shared[grp] = _share_exchange(self.half_sums[grp])
        return self.shared[grp]

    def totals(self, grp):
        return list(zip(self.half_sums[grp], self.shared[grp].results))


def _ffn_fwd(x, norm_g, shift, scale, gate, wg_t, wu_t, wd, tag, up_exchange=None, down_exchange=None):
    h = _norm_mod_fwd(x, norm_g, shift, scale, tag + "_norm_fwd")
    a, u, hid = _ffn_up(h, wg_t, wu_t, tag + "_up", exchange=up_exchange)
    x_out, f = _mm(hid, wd, "nn", F32, tag + "_down", res=x, gate=gate, aux_dtype=BF16,
                   exchange=down_exchange() if down_exchange else None)
    return x_out, (h, a, u, hid, f)


def _ffn_bwd(dx_out, df, x, saved, norm_g, scale, wg_t, wu_t, wd, tag, traffic, below=None, dact_exchange=None,
             dw_exchange=None, finish_reduction=False):
    h, a, u, hid, _ = saved
    f_below, gate_below = below if below else (None, None)
    da, du = _ffn_dact(df, wd, a, u, tag + "_dact", exchange=dact_exchange)
    dwd = _mm(hid, df, "tn", BF16, tag + "_dwd", exchange=dw_exchange() if dw_exchange else None)
    if not finish_reduction:
        dwg_t = _mm(da, h, "tn", BF16, tag + "_dwg")
        dwu_t = _mm(du, h, "tn", BF16, tag + "_dwu")
        dh = _mm(da, wg_t, "nn", F32, tag + "_dh_a", exchange=traffic.reduce_sibling(tag, [dwg_t, dwu_t, dwd]))
        traffic.add_halves(tag)
        dh = _mm(du, wu_t, "nn", F32, tag + "_dh_u", res=dh)
        dx, dshift, dscale, dnorm_g, *gated = _norm_mod_bwd(dh, x, norm_g, scale, dx_out, tag + "_norm_bwd",
                                                            f=f_below, gate=gate_below)
        return dx, (dshift, dscale, dnorm_g), gated
    kd, kg, ku = tag + "_wd", tag + "_wg", tag + "_wu"
    dwg_t = _mm(da, h, "tn", BF16, tag + "_dwg", exchange=traffic.reduce_sibling(kd, [dwd]))
    traffic.add_halves(kd)
    dwu_t = _mm(du, h, "tn", BF16, tag + "_dwu",
                exchange=_join(traffic.reduce_chips(kd), traffic.reduce_sibling(kg, [dwg_t])))
    traffic.add_halves(kg)
    dh = _mm(da, wg_t, "nn", F32, tag + "_dh_a",
             exchange=_join(traffic.reduce_chips(kg), traffic.reduce_sibling(ku, [dwu_t])))
    traffic.add_halves(ku)
    traffic.sum_chips(kd)
    dh = _mm(du, wu_t, "nn", F32, tag + "_dh_u", res=dh, exchange=_join(traffic.reduce_chips(ku), traffic.share(kd)))
    traffic.sum_chips(kg)
    traffic.sum_chips(ku)
    dx, dshift, dscale, dnorm_g, *gated = _norm_mod_bwd(dh, x, norm_g, scale, dx_out, tag + "_norm_bwd", f=f_below,
                                                        gate=gate_below, exchange=_join(traffic.share(kg), traffic.share(ku)))
    return dx, (dshift, dscale, dnorm_g), gated


def _layer_step(x, target, mod, gains, forget_bias, conv_w, traffic, att_w, in_shard, in_rows):
    sh1, sc1, g1, sh2, sc2, g2, sh3, sc3, g3 = mod
    norm1_g, norm2_g, norm3_g, final_g, group_g = gains
    s, d = x.shape
    n_heads = att_w // HEAD_DIM
    npair = n_heads // 2
    gate1, gate3 = 0.5 * g1, 0.5 * g3

    def split_w_in(w_in_pad):
        w_in_t = w_in_pad.reshape(N_CHIPS, in_rows, d)[:, :in_shard].reshape(N_CHIPS * in_shard, d)
        return (w_in_t[:3 * att_w], _pad_rows(w_in_t[3 * att_w:3 * att_w + n_heads], LANES), w_in_t[3 * att_w + n_heads:])

    _run_exchange(traffic.gather_ici("ffn1"), "gather_ffn1_ici")
    _run_exchange(traffic.gather_pass("ffn1"), "gather_ffn1_pass")
    wg1_t, wu1_t, wd1 = traffic.weights("ffn1")
    x1, saved1 = _ffn_fwd(x, norm1_g, sh1, sc1, gate1, wg1_t, wu1_t, wd1, "ffn1",
                          up_exchange=traffic.gather_ici("mix"), down_exchange=lambda: traffic.gather_pass("mix"))
    w_in_pad, w_out = traffic.weights("mix")
    wqkv_t, wf_t, wbcx_t = split_w_in(w_in_pad)

    h2 = _norm_mod_fwd(x1, norm2_g, sh2, sc2, "mix_norm_fwd")
    qkv = _mm(h2, wqkv_t, "nt", BF16, "mix_proj_qkv")
    bcx = _mm(h2, wbcx_t, "nt", F32, "mix_proj_bcx")
    flog = _mm(h2, wf_t, "nt", F32, "mix_proj_f")
    flog_t = jnp.pad(flog[:, :n_heads].T, ((0, HEAD_ROWS - n_heads), (0, 0)))
    bias_col = jnp.pad(forget_bias, (0, HEAD_ROWS - n_heads))[:, None]
    f_pieces = _forget_fwd(flog_t, bias_col, "forget_fwd")
    qa, ka, va = _attn_prep(qkv, f_pieces, "attn_prep")
    att, lse = _attn_fwd(qa, ka, va, "attn_fwd", exchange=traffic.gather_ici("ffn2"))
    cv = _conv_fwd(bcx, conv_w, "conv_fwd")
    yn = _gnorm_fwd(att, cv, group_g, "gnorm_fwd")
    x2, mix = _mm(yn, w_out, "nn", F32, "mix_out", res=x1, gate=g2, aux_dtype=BF16, exchange=traffic.gather_pass("ffn2"))
    wg2_t, wu2_t, wd2 = traffic.weights("ffn2")

    x3, saved3 = _ffn_fwd(x2, norm3_g, sh3, sc3, gate3, wg2_t, wu2_t, wd2, "ffn2")

    dx3, loss_row, dfinal_g, df2, dgate3 = _final_loss(x3, final_g, target, saved3[4], gate3, "final_loss")

    dx2, (dsh3, dsc3, dnorm3_g), (dmix, dg2) = _ffn_bwd(
        dx3, df2, x2, saved3, norm3_g, sc3, wg2_t, wu2_t, wd2, "ffn2", traffic, below=(mix, g2))
    dyn = _mm(dmix, w_out, "nt", F32, "mix_out_dyn")
    dw_out = _mm(yn, dmix, "tn", BF16, "mix_out_dw")
    datt, dcv, dgroup_g = _gnorm_bwd(dyn, att, cv, group_g, "gnorm_bwd")
    db, dc, dxc, dconv_w = _conv_bwd(dcv, bcx, conv_w, "conv_bwd")
    dbcx = jnp.concatenate([db, dc, dxc], axis=1)
    dq, dk, dv, qx, kx = _attn_bwd(qa, ka, va, datt, att, lse, "attn_bwd", exchange=traffic.reduce_chips("ffn2"))
    traffic.sum_chips("ffn2")
    dqkv = jnp.concatenate([dq.astype(BF16), dk, dv], axis=1)
    df_t = _decay_grads(qx, kx, "decay_grads")[:, :HEAD_ROWS].T
    dflog_t, dbias_col = _forget_bwd(df_t, flog_t, bias_col, "forget_bwd")
    dflog = jnp.pad(dflog_t[:n_heads].T, ((0, 0), (0, LANES - n_heads))).astype(BF16)
    dh2 = _mm(dqkv, wqkv_t, "nn", F32, "mix_dh_qkv", exchange=traffic.share("ffn2"))
    dh2 = _mm(dbcx, wbcx_t, "nn", F32, "mix_dh_bcx", res=dh2)
    dh2 = _mm(dflog, wf_t, "nn", F32, "mix_dh_f", res=dh2)
    dwqkv_t = _mm(dqkv, h2, "tn", BF16, "mix_dw_qkv")
    dwbcx_t = _mm(dbcx, h2, "tn", BF16, "mix_dw_bcx")
    dwf_t = _mm(dflog, h2, "tn", BF16, "mix_dw_f")
    dw_in_t = jnp.concatenate([dwqkv_t, dwf_t[:n_heads], dwbcx_t], axis=0).reshape(N_CHIPS, in_shard, d)
    dw_in_t = jnp.pad(dw_in_t, ((0, 0), (0, in_rows - in_shard), (0, 0))).reshape(N_CHIPS * in_rows, d)
    dx1, dsh2, dsc2, dnorm2_g, df1, dgate1 = _norm_mod_bwd(
        dh2, x1, norm2_g, sc2, dx2, "mix_norm_bwd", f=saved1[4], gate=gate1,
        exchange=traffic.reduce_sibling("mix", [dw_in_t, dw_out]))
    traffic.add_halves("mix")

    def share_mix():
        traffic.sum_chips("mix")
        return traffic.share("mix")

    dx, (dsh1, dsc1, dnorm1_g), _ = _ffn_bwd(
        dx1, df1, x, saved1, norm1_g, sc1, wg1_t, wu1_t, wd1, "ffn1", traffic,
        dact_exchange=traffic.reduce_chips("mix"), dw_exchange=share_mix, finish_reduction=True)

    dmod = [dsh1, dsc1, 0.5 * dgate1, dsh2, dsc2, dg2, dsh3, dsc3, 0.5 * dgate3]
    dgains = [dnorm1_g, dnorm2_g, dnorm3_g, dfinal_g, dgroup_g]
    dbias = dbias_col[:n_heads, 0]
    return dx, loss_row, dmod, dgains, dbias, dconv_w


SMALL_ROWS = 24
ROW_GAINS, ROW_LOSS, ROW_FORGET, ROW_CONV, ROW_MOD = 0, 5, 6, 7, 10
PROW_ADA_B, PROW_GAINS, PROW_FORGET, PROW_CONV = 0, 9, 14, 15


def _round_up(n, m):
    return -(-n // m) * m


def _pad_rows(a, rows):
    return jnp.pad(a, ((0, rows - a.shape[0]), (0, 0)))


def _halves(a):
    return a.reshape(2, a.shape[0] // 2, a.shape[1])


def _rows_at(a, r0, total, width):
    return jnp.pad(a, ((r0, total - r0 - a.shape[0]), (0, width - a.shape[1])))


def kernel(x, c, ada_w, ada_b, norm1_g, ffn1_w_gate, ffn1_w_up, ffn1_w_down, norm2_g, w_in, forget_bias, conv_w, group_norm_g, w_out, norm3_g, ffn2_w_gate, ffn2_w_up, ffn2_w_down, final_g, loss_target, m_ada_w, m_ada_b, m_norm1_g, m_ffn1_w_gate, m_ffn1_w_up, m_ffn1_w_down, m_norm2_g, m_w_in, m_forget_bias, m_conv_w, m_group_norm_g, m_w_out, m_norm3_g, m_ffn2_w_gate, m_ffn2_w_up, m_ffn2_w_down, m_final_g, v_ada_w, v_ada_b, v_norm1_g, v_ffn1_w_gate, v_ffn1_w_up, v_ffn1_w_down, v_norm2_g, v_w_in, v_forget_bias, v_conv_w, v_group_norm_g, v_w_out, v_norm3_g, v_ffn2_w_gate, v_ffn2_w_up, v_ffn2_w_down, v_final_g):
    xi, yi, ci = lax.axis_index("x"), lax.axis_index("y"), lax.axis_index("c")
    chip = 2 * xi + yi
    dev = 4 * xi + 2 * yi + ci
    _, s, d = x.shape
    att_w = d // 2
    conv_width = d - att_w
    n_heads = att_w // HEAD_DIM
    in_shard = w_in.shape[1]
    in_rows = _round_up(in_shard, 32)
    cs = conv_w.shape[1]
    mod_shard = ada_w.shape[1]
    assert N_MOD * d == N_CHIPS * mod_shard and conv_width == N_CHIPS * cs and n_heads % 2 == 0

    pack0 = _rows_at(c, 0, 8, d) + _rows_at(conv_w, 1, 8, d)
    got0 = _all_gather_small(pack0, "gather_cond").reshape(N_DEV, 8, d)
    c16 = _pad_rows(got0[:, 0, :], 16)
    conv_full = got0[0::2, 1:1 + CONV_K, :cs].transpose(1, 0, 2).reshape(CONV_K, conv_width)

    ada_b_mine = lax.dynamic_slice(ada_b, (chip * mod_shard,), (mod_shard,))[None, :]
    mod_part = _ada_fwd(c16, ada_w, ada_b_mine, "ada_fwd")
    got1 = _all_gather_small(mod_part, "gather_mod").reshape(N_DEV, 16, mod_shard)
    mod_mine = lax.dynamic_index_in_dim(got1[0::2], dev, axis=1, keepdims=False).reshape(N_MOD, d)
    mod = [mod_mine[i:i + 1] for i in range(N_MOD)]

    def t_bf(w):
        return w.T.astype(BF16)

    shards = {"ffn1": [_halves(t_bf(ffn1_w_gate)), _halves(t_bf(ffn1_w_up)), _halves(ffn1_w_down.astype(BF16))],
              "mix": [_halves(_pad_rows(t_bf(w_in), in_rows)), _halves(w_out.astype(BF16))],
              "ffn2": [_halves(t_bf(ffn2_w_gate)), _halves(t_bf(ffn2_w_up)), _halves(ffn2_w_down.astype(BF16))]}
    core = ci.astype(jnp.int32).reshape(1)
    chip_arr = chip.astype(jnp.int32).reshape(1)
    traffic = _WeightTraffic(shards, core, chip_arr)

    gains = [g[None, :] for g in (norm1_g, norm2_g, norm3_g, final_g, group_norm_g)]
    dx, loss_row, dmod, dgains, dbias, dconv_w = _layer_step(
        x[0], loss_target[0], mod, gains, forget_bias, conv_full, traffic, att_w, in_shard, in_rows)

    pack = sum(_rows_at(g, ROW_GAINS + i, SMALL_ROWS, d) for i, g in enumerate(dgains))
    pack += _rows_at(loss_row, ROW_LOSS, SMALL_ROWS, d) + _rows_at(dbias[None, :], ROW_FORGET, SMALL_ROWS, d)
    pack += _rows_at(dconv_w, ROW_CONV, SMALL_ROWS, d)
    pack += sum(_rows_at(g, ROW_MOD + i, SMALL_ROWS, d) for i, g in enumerate(dmod))
    got2 = _all_gather_small(pack, "gather_small_grads").reshape(N_DEV, SMALL_ROWS, d)
    tot = _sum_devices(got2, "sum_small_grads")
    loss = tot[ROW_LOSS, 0]
    grad_ada_b = tot[ROW_MOD:ROW_MOD + N_MOD].reshape(N_MOD * d)
    grad_conv = lax.dynamic_slice(tot[ROW_CONV:ROW_CONV + CONV_K], (0, chip * cs), (CONV_K, cs))
    dmod_all = got2[:, ROW_MOD:ROW_MOD + N_MOD, :].reshape(N_DEV, N_MOD * d)
    dmod16 = _pad_rows(lax.dynamic_slice(dmod_all, (0, chip * mod_shard), (N_DEV, mod_shard)), 16)

    totals = (traffic.totals("ffn1_wg") + traffic.totals("ffn1_wu") + traffic.totals("ffn1_wd")
              + traffic.totals("mix") + traffic.totals("ffn2"))

    names = ("ffn1_w_gate", "ffn1_w_up", "ffn1_w_down", "w_in", "w_out", "ffn2_w_gate", "ffn2_w_up", "ffn2_w_down")
    transposed = ("ffn1_w_gate", "ffn1_w_up", "w_in", "ffn2_w_gate", "ffn2_w_up")
    params = {"ffn1_w_gate": (ffn1_w_gate, m_ffn1_w_gate, v_ffn1_w_gate), "ffn1_w_up": (ffn1_w_up, m_ffn1_w_up, v_ffn1_w_up),
              "ffn1_w_down": (ffn1_w_down, m_ffn1_w_down, v_ffn1_w_down), "w_in": (w_in, m_w_in, v_w_in),
              "w_out": (w_out, m_w_out, v_w_out), "ffn2_w_gate": (ffn2_w_gate, m_ffn2_w_gate, v_ffn2_w_gate),
              "ffn2_w_up": (ffn2_w_up, m_ffn2_w_up, v_ffn2_w_up), "ffn2_w_down": (ffn2_w_down, m_ffn2_w_down, v_ffn2_w_down)}
    out = {}
    for name_, (mine, theirs) in zip(names, totals):
        w, m, v = params[name_]
        if name_ in transposed:
            w, m, v = w.T, m.T, v.T
        if name_ == "w_in":
            both = jnp.where(ci == 0, jnp.concatenate([mine, theirs]), jnp.concatenate([theirs, mine]))[:in_shard]
            res = (both,) + tuple(_adamw(w, both, m, v, "adamw_" + name_))
        else:
            res = _adamw_halves(w, mine, theirs, m, v, core, "adamw_" + name_)
        out[name_] = tuple(r.T for r in res) if name_ in transposed else tuple(res)
    c16_t = c16.T
    out["ada_w"] = tuple(_ada_update(c16_t, dmod16, ada_w, m_ada_w, v_ada_w, "adamw_ada_w"))

    def small_pack(ada_b_, gains_, forget_, conv_):
        p = _rows_at(ada_b_.reshape(N_MOD, d), PROW_ADA_B, SMALL_ROWS, d)
        p += sum(_rows_at(g[None, :], PROW_GAINS + i, SMALL_ROWS, d) for i, g in enumerate(gains_))
        p += _rows_at(forget_[None, :], PROW_FORGET, SMALL_ROWS, d) + _rows_at(conv_, PROW_CONV, SMALL_ROWS, d)
        return p

    g_gains = [tot[ROW_GAINS + i] for i in range(5)]
    g_forget = tot[ROW_FORGET, :n_heads]
    sw = small_pack(ada_b, (norm1_g, norm2_g, norm3_g, final_g, group_norm_g), forget_bias, conv_w)
    sm = small_pack(m_ada_b, (m_norm1_g, m_norm2_g, m_norm3_g, m_final_g, m_group_norm_g), m_forget_bias, m_conv_w)
    sv = small_pack(v_ada_b, (v_norm1_g, v_norm2_g, v_norm3_g, v_final_g, v_group_norm_g), v_forget_bias, v_conv_w)
    sg = small_pack(grad_ada_b, g_gains, g_forget, grad_conv)
    small = (sg,) + tuple(_adamw(sw, sg, sm, sv, "adamw_small"))

    def unpack(p):
        r = {"ada_b": p[PROW_ADA_B:PROW_ADA_B + N_MOD].reshape(N_MOD * d), "forget_bias": p[PROW_FORGET, :n_heads],
             "conv_w": p[PROW_CONV:PROW_CONV + CONV_K, :cs]}
        for i, nm in enumerate(("norm1_g", "norm2_g", "norm3_g", "final_g", "group_norm_g")):
            r[nm] = p[PROW_GAINS + i]
        return r

    small = [unpack(p) for p in small]
    order = ("ada_w", "ada_b", "norm1_g", "ffn1_w_gate", "ffn1_w_up", "ffn1_w_down", "norm2_g", "w_in", "forget_bias",
             "conv_w", "group_norm_g", "w_out", "norm3_g", "ffn2_w_gate", "ffn2_w_up", "ffn2_w_down", "final_g")
    result = [loss, dx[None]]
    for k in range(4):
        result += [out[nm][k] if nm in out else small[k][nm] for nm in order]
    return tuple(result)
```

```python
import functools
import math

import jax
import jax.numpy as jnp
from jax import lax
from jax.experimental import pallas as pl
from jax.experimental.pallas import tpu as pltpu

F32 = jnp.float32
BF16 = jnp.bfloat16

HEAD_DIM = 64
CONV_K = 3
N_MOD = 9
EPS = 1e-6
ADAM_LR = 0.001
ADAM_B1 = 0.9
ADAM_B2 = 0.999
ADAM_EPS = 1e-08
ADAM_WD = 0.01
ADAM_STEP = 10

LANES = 128
N_CHIPS = 4
N_DEV = 8
VMEM_LIMIT_BYTES = 56 * 1024 * 1024
NEG_BIG = -1e30
MESH = pl.DeviceIdType.MESH

_NT = (((1,), (1,)), ((), ()))
_NN = (((1,), (0,)), ((), ()))
_TN = (((0,), (0,)), ((), ()))


def _params(*sem):
    return pltpu.CompilerParams(dimension_semantics=sem, vmem_limit_bytes=VMEM_LIMIT_BYTES)


class _Exchange:
    def __init__(self, inputs, out_shapes, n_sems, start, finish, aliases=None):
        self.inputs, self.out_shapes, self.n_sems = list(inputs), list(out_shapes), n_sems
        self.start, self.finish, self.aliases = start, finish, dict(aliases or {})
        self.results = None

    def set_results(self, results):
        self.results = list(results)


class _SemaphoreWindow:
    def __init__(self, sems, base):
        self.sems, self.base = sems, base
        self.at = self

    def __getitem__(self, k):
        return self.sems.at[self.base + k]


class _JoinedExchange(_Exchange):
    def __init__(self, parts):
        self.parts = parts
        assert all(not p.aliases for p in parts)

        def each(method, src, dst, send_sems, recv_sems):
            i0 = o0 = s0 = 0
            for p in parts:
                i1, o1 = i0 + len(p.inputs), o0 + len(p.out_shapes)
                getattr(p, method)(src[i0:i1], dst[o0:o1], _SemaphoreWindow(send_sems, s0), _SemaphoreWindow(recv_sems, s0))
                i0, o0, s0 = i1, o1, s0 + p.n_sems

        super().__init__([a for p in parts for a in p.inputs], [o for p in parts for o in p.out_shapes],
                         sum(p.n_sems for p in parts), functools.partial(each, "start"), functools.partial(each, "finish"))

    def set_results(self, results):
        o0 = 0
        for p in self.parts:
            p.set_results(results[o0:o0 + len(p.out_shapes)])
            o0 += len(p.out_shapes)


def _join(*parts):
    return parts[0] if len(parts) == 1 else _JoinedExchange(list(parts))


def _pc(body, exchange=None, **kw):
    if exchange is None:
        return pl.pallas_call(body, **kw)
    grid = kw["grid"]
    single = not isinstance(kw["out_shape"], (tuple, list))
    out_shape = [kw["out_shape"]] if single else list(kw["out_shape"])
    out_specs = [kw["out_specs"]] if single else list(kw["out_specs"])
    in_specs = list(kw["in_specs"])
    scratch = list(kw.get("scratch_shapes", ()))
    n_in, n_out, n_scr = len(in_specs), len(out_shape), len(scratch)
    n_xi, n_xo = len(exchange.inputs), len(exchange.out_shapes)

    def wrapped(*refs):
        pos = [n_in, n_in + n_xi, n_in + n_xi + n_out, n_in + n_xi + n_out + n_xo]
        ins, x_in, outs, x_out = refs[:pos[0]], refs[pos[0]:pos[1]], refs[pos[1]:pos[2]], refs[pos[2]:pos[3]]
        scr = refs[pos[3]:pos[3] + n_scr]
        send_sems, recv_sems = refs[pos[3] + n_scr:]
        ids = [pl.program_id(a) for a in range(len(grid))]
        first = functools.reduce(jnp.logical_and, [i == 0 for i in ids])
        last = functools.reduce(jnp.logical_and, [i == g - 1 for i, g in zip(ids, grid)])

        @pl.when(first)
        def _():
            exchange.start(x_in, x_out, send_sems, recv_sems)

        body(*ins, *outs, *scr)

        @pl.when(last)
        def _():
            exchange.finish(x_in, x_out, send_sems, recv_sems)

    call = pl.pallas_call(
        wrapped, out_shape=tuple(out_shape) + tuple(exchange.out_shapes), grid=grid,
        in_specs=in_specs + [_ANY] * n_xi, out_specs=tuple(out_specs) + (_ANY,) * n_xo,
        scratch_shapes=scratch + [pltpu.SemaphoreType.DMA((exchange.n_sems,)), pltpu.SemaphoreType.DMA((exchange.n_sems,))],
        input_output_aliases={n_in + a: n_out + b for a, b in exchange.aliases.items()},
        compiler_params=_params(*(["arbitrary"] * len(grid))), name=kw["name"])

    def run(*args):
        res = call(*args, *exchange.inputs)
        exchange.set_results(res[n_out:])
        return res[0] if single else tuple(res[:n_out])

    return run


_ANY = pl.BlockSpec(memory_space=pl.ANY)


def _tile(n, pref, mult):
    best = None
    t = mult
    while t <= min(n, pref):
        if n % t == 0:
            best = t
        t += mult
    return n if best is None else best


def _sds(shape, dtype):
    return jax.ShapeDtypeStruct(shape, dtype)


def _vec_spec(d):
    return pl.BlockSpec((1, d), lambda *_: (0, 0))


def _norm_mod_fwd(x, g, shift, scale, name):
    s, d = x.shape
    tr = _tile(s, 512, 16)

    def body(x_ref, g_ref, sh_ref, sc_ref, h_ref):
        xv = x_ref[...]
        rstd = lax.rsqrt(jnp.mean(xv * xv, axis=-1, keepdims=True) + EPS)
        n = xv * rstd * g_ref[...]
        h_ref[...] = (n * (1.0 + sc_ref[...]) + sh_ref[...]).astype(BF16)

    row = pl.BlockSpec((tr, d), lambda i: (i, 0))
    return _pc(body, out_shape=_sds((s, d), BF16), grid=(s // tr,),
               in_specs=[row, _vec_spec(d), _vec_spec(d), _vec_spec(d)], out_specs=row,
               compiler_params=_params("parallel"), name=name)(x, g, shift, scale)


def _through_gate(dx, f_ref, gate_ref, df_ref, dgate_ref):
    df_ref[...] = (dx * gate_ref[...]).astype(BF16)
    dgate_ref[...] += jnp.sum(dx * f_ref[...].astype(F32), axis=0, keepdims=True)


def _norm_mod_bwd(dh, x, g, scale, dres, name, f=None, gate=None, exchange=None):
    s, d = x.shape
    tr = _tile(s, 256, 16)
    gated = f is not None

    def body(dh_ref, x_ref, g_ref, sc_ref, dres_ref, *rest):
        f_ref, gate_ref = rest[:2] if gated else (None, None)
        dx_ref, dsh_ref, dsc_ref, dg_ref = rest[2:6] if gated else rest[:4]
        df_ref, dgate_ref = rest[6:8] if gated else (None, None)

        @pl.when(pl.program_id(0) == 0)
        def _():
            for ref in (dsh_ref, dsc_ref, dg_ref) + ((dgate_ref,) if gated else ()):
                ref[...] = jnp.zeros_like(ref)

        xv = x_ref[...]
        dhv = dh_ref[...]
        gv = g_ref[...]
        rstd = lax.rsqrt(jnp.mean(xv * xv, axis=-1, keepdims=True) + EPS)
        xhat = xv * rstd
        dn = dhv * (1.0 + sc_ref[...])
        dsh_ref[...] += jnp.sum(dhv, axis=0, keepdims=True)
        dsc_ref[...] += jnp.sum(dhv * (xhat * gv), axis=0, keepdims=True)
        dg_ref[...] += jnp.sum(dn * xhat, axis=0, keepdims=True)
        dxh = dn * gv
        proj = jnp.mean(dxh * xhat, axis=-1, keepdims=True)
        dx = dres_ref[...] + rstd * (dxh - xhat * proj)
        dx_ref[...] = dx
        if gated:
            _through_gate(dx, f_ref, gate_ref, df_ref, dgate_ref)

    row = pl.BlockSpec((tr, d), lambda i: (i, 0))
    vec = _vec_spec(d)
    out_shape = [_sds((s, d), F32), _sds((1, d), F32), _sds((1, d), F32), _sds((1, d), F32)]
    out_specs, in_specs, args = [row, vec, vec, vec], [row, row, vec, vec, row], [dh, x, g, scale, dres]
    if gated:
        out_shape += [_sds((s, d), BF16), _sds((1, d), F32)]
        out_specs += [row, vec]
        in_specs += [row, vec]
        args += [f, gate]
    return _pc(body, exchange, out_shape=tuple(out_shape), grid=(s // tr,), in_specs=in_specs,
               out_specs=tuple(out_specs), compiler_params=_params("arbitrary"), name=name)(*args)


def _final_loss(x, g, target, f, gate, name):
    s, d = x.shape
    tr = _tile(s, 256, 16)
    nsteps = s // tr

    def body(x_ref, g_ref, t_ref, f_ref, gate_ref, dx_ref, loss_ref, dg_ref, df_ref, dgate_ref):
        i = pl.program_id(0)

        @pl.when(i == 0)
        def _():
            loss_ref[...] = jnp.zeros_like(loss_ref)
            dg_ref[...] = jnp.zeros_like(dg_ref)
            dgate_ref[...] = jnp.zeros_like(dgate_ref)

        xv = x_ref[...]
        gv = g_ref[...]
        rstd = lax.rsqrt(jnp.mean(xv * xv, axis=-1, keepdims=True) + EPS)
        xhat = xv * rstd
        err = xhat * gv - t_ref[...]
        dy = err * (1.0 / d)
        loss_ref[...] += jnp.sum(0.5 * err * dy, axis=0, keepdims=True)
        dg_ref[...] += jnp.sum(dy * xhat, axis=0, keepdims=True)
        dxh = dy * gv
        proj = jnp.mean(dxh * xhat, axis=-1, keepdims=True)
        dx = rstd * (dxh - xhat * proj)
        dx_ref[...] = dx
        _through_gate(dx, f_ref, gate_ref, df_ref, dgate_ref)

        @pl.when(i == nsteps - 1)
        def _():
            loss_ref[...] = jnp.broadcast_to(jnp.sum(loss_ref[...], axis=-1, keepdims=True), loss_ref.shape)

    row = pl.BlockSpec((tr, d), lambda i: (i, 0))
    vec = _vec_spec(d)
    return _pc(body, out_shape=(_sds((s, d), F32), _sds((1, d), F32), _sds((1, d), F32), _sds((s, d), BF16), _sds((1, d), F32)),
               grid=(nsteps,), in_specs=[row, vec, row, row, vec], out_specs=(row, vec, vec, row, vec),
               compiler_params=_params("arbitrary"), name=name)(x, g, target, f, gate)


def _mm(lhs, rhs, dims, out_dtype, name, res=None, gate=None, aux_dtype=None, exchange=None):
    if dims == "nn":
        (m, k), (k2, n) = lhs.shape, rhs.shape
    elif dims == "nt":
        (m, k), (n, k2) = lhs.shape, rhs.shape
    else:
        (k, m), (k2, n) = lhs.shape, rhs.shape
    assert k == k2, (lhs.shape, rhs.shape, dims)
    tn = _tile(n, 1024, LANES)
    tm = _tile(m, 512, LANES if dims == "tn" else 16)
    tk = _tile(k, 4096, LANES)
    nk = k // tk
    dn = {"nn": _NN, "nt": _NT, "tn": _TN}[dims]
    lhs_spec = (pl.BlockSpec((tk, tm), lambda i, j, kk: (kk, i)) if dims == "tn"
                else pl.BlockSpec((tm, tk), lambda i, j, kk: (i, kk)))
    rhs_spec = (pl.BlockSpec((tn, tk), lambda i, j, kk: (j, kk)) if dims == "nt"
                else pl.BlockSpec((tk, tn), lambda i, j, kk: (kk, j)))
    out_spec = pl.BlockSpec((tm, tn), lambda i, j, kk: (i, j))
    has_res, has_gate, has_aux = res is not None, gate is not None, aux_dtype is not None

    def body(*refs):
        refs = list(refs)
        l_ref, r_ref = refs[0], refs[1]
        pos = 2
        res_ref = gate_ref = aux_ref = None
        if has_res:
            res_ref = refs[pos]; pos += 1
        if has_gate:
            gate_ref = refs[pos]; pos += 1
        out_ref = refs[pos]; pos += 1
        if has_aux:
            aux_ref = refs[pos]; pos += 1
        acc_ref = refs[pos]
        kk = pl.program_id(2)
        part = lax.dot_general(l_ref[...], r_ref[...], dn, preferred_element_type=F32)

        @pl.when(kk == 0)
        def _():
            acc_ref[...] = part

        @pl.when(kk > 0)
        def _():
            acc_ref[...] += part

        @pl.when(kk == nk - 1)
        def _():
            acc = acc_ref[...]
            if has_aux:
                aux_ref[...] = acc.astype(aux_dtype)
            if has_gate:
                acc = acc * gate_ref[...]
            if has_res:
                acc = res_ref[...] + acc
            out_ref[...] = acc.astype(out_dtype)

    in_specs = [lhs_spec, rhs_spec]
    args = [lhs, rhs]
    if has_res:
        in_specs.append(out_spec); args.append(res)
    if has_gate:
        in_specs.append(pl.BlockSpec((1, tn), lambda i, j, kk: (0, j))); args.append(gate)
    out_shape = [_sds((m, n), out_dtype)]
    out_specs = [out_spec]
    if has_aux:
        out_shape.append(_sds((m, n), aux_dtype)); out_specs.append(out_spec)
    outs = _pc(body, exchange, out_shape=tuple(out_shape), grid=(m // tm, n // tn, nk), in_specs=in_specs,
               out_specs=tuple(out_specs), scratch_shapes=[pltpu.VMEM((tm, tn), F32)],
               compiler_params=_params("parallel", "parallel", "arbitrary"), name=name)(*args)
    return outs if has_aux else outs[0]


def _ffn_up(h, wg_t, wu_t, name, exchange=None):
    s, d = h.shape
    f = wg_t.shape[0]
    tm = _tile(s, 1024, 16)
    tn = _tile(f, 256, LANES)

    def body(h_ref, wg_ref, wu_ref, a_ref, u_ref, hid_ref):
        hv = h_ref[...]
        a = lax.dot_general(hv, wg_ref[...], _NT, preferred_element_type=F32)
        u = lax.dot_general(hv, wu_ref[...], _NT, preferred_element_type=F32)
        a_ref[...] = a.astype(BF16)
        u_ref[...] = u.astype(BF16)
        hid_ref[...] = (a * jax.nn.sigmoid(a) * u).astype(BF16)

    hs = pl.BlockSpec((tm, d), lambda i, j: (i, 0))
    ws = pl.BlockSpec((tn, d), lambda i, j: (j, 0))
    os_ = pl.BlockSpec((tm, tn), lambda i, j: (i, j))
    return _pc(body, exchange, out_shape=(_sds((s, f), BF16),) * 3, grid=(s // tm, f // tn),
               in_specs=[hs, ws, ws], out_specs=(os_, os_, os_),
               compiler_params=_params("parallel", "parallel"), name=name)(h, wg_t, wu_t)


def _ffn_dact(df, wd, a, u, name, exchange=None):
    s, d = df.shape
    f = wd.shape[0]
    tm = _tile(s, 1024, 16)
    tn = _tile(f, 256, LANES)

    def body(df_ref, wd_ref, a_ref, u_ref, da_ref, du_ref):
        dhid = lax.dot_general(df_ref[...], wd_ref[...], _NT, preferred_element_type=F32)
        av = a_ref[...].astype(F32)
        uv = u_ref[...].astype(F32)
        sig = jax.nn.sigmoid(av)
        da_ref[...] = (dhid * uv * (sig * (1.0 + av * (1.0 - sig)))).astype(BF16)
        du_ref[...] = (dhid * (av * sig)).astype(BF16)

    ds_ = pl.BlockSpec((tm, d), lambda i, j: (i, 0))
    ws = pl.BlockSpec((tn, d), lambda i, j: (j, 0))
    os_ = pl.BlockSpec((tm, tn), lambda i, j: (i, j))
    return _pc(body, exchange, out_shape=(_sds((s, f), BF16),) * 2, grid=(s // tm, f // tn),
               in_specs=[ds_, ws, os_, os_], out_specs=(os_, os_),
               compiler_params=_params("parallel", "parallel"), name=name)(df, wd, a, u)


def _split3(v):
    hi = v.astype(BF16)
    r1 = v - hi.astype(F32)
    mid = r1.astype(BF16)
    lo = (r1 - mid.astype(F32)).astype(BF16)
    return hi, mid, lo


def _dot3(v, mat):
    hi, mid, lo = _split3(v)
    out = lax.dot_general(hi, mat, _NN, preferred_element_type=F32)
    out += lax.dot_general(mid, mat, _NN, preferred_element_type=F32)
    out += lax.dot_general(lo, mat, _NN, preferred_element_type=F32)
    return out


def _forget_fwd(flog_t, bias, name):
    h, s = flog_t.shape
    blk = _tile(s, 512, LANES)
    tri = (jnp.arange(blk)[:, None] <= jnp.arange(blk)[None, :]).astype(BF16)

    def body(z_ref, b_ref, tri_ref, f_ref, carry):
        @pl.when(pl.program_id(0) == 0)
        def _():
            carry[...] = jnp.zeros_like(carry)

        z = z_ref[...] + b_ref[...]
        e = jnp.exp(-jnp.abs(z))
        w = 1.0 + e
        log1p_e = jnp.where(w == 1.0, e, jnp.log(w) * (e / (w - 1.0)))
        lf = jnp.minimum(z, 0.0) - log1p_e
        out = carry[...] + _dot3(lf, tri_ref[...])
        for j, piece in enumerate(_split3(out)):
            f_ref[j] = piece
        carry[...] = out[:, blk - 1:blk]

    zs = pl.BlockSpec((h, blk), lambda i: (0, i))
    return _pc(body, out_shape=_sds((3, h, s), BF16), grid=(s // blk,),
               in_specs=[zs, pl.BlockSpec((h, 1), lambda i: (0, 0)), pl.BlockSpec((blk, blk), lambda i: (0, 0))],
               out_specs=pl.BlockSpec((3, h, blk), lambda i: (0, 0, i)), scratch_shapes=[pltpu.VMEM((h, 1), F32)],
               compiler_params=_params("arbitrary"), name=name)(flog_t, bias, tri)


def _forget_bwd(df_t, flog_t, bias, name):
    h, s = flog_t.shape
    blk = _tile(s, 512, LANES)
    nb = s // blk
    tri = (jnp.arange(blk)[:, None] >= jnp.arange(blk)[None, :]).astype(BF16)

    def body(df_ref, z_ref, b_ref, tri_ref, dz_ref, db_ref, carry):
        @pl.when(pl.program_id(0) == 0)
        def _():
            carry[...] = jnp.zeros_like(carry)
            db_ref[...] = jnp.zeros_like(db_ref)

        rc = carry[...] + _dot3(df_ref[...], tri_ref[...])
        carry[...] = rc[:, 0:1]
        dz = rc * jax.nn.sigmoid(-(z_ref[...] + b_ref[...]))
        dz_ref[...] = dz
        db_ref[...] += jnp.sum(dz, axis=-1, keepdims=True)

    rev = pl.BlockSpec((h, blk), lambda i: (0, nb - 1 - i))
    col = pl.BlockSpec((h, 1), lambda i: (0, 0))
    return _pc(body, out_shape=(_sds((h, s), F32), _sds((h, 1), F32)), grid=(nb,),
               in_specs=[rev, rev, col, pl.BlockSpec((blk, blk), lambda i: (0, 0))],
               out_specs=(rev, col), scratch_shapes=[pltpu.VMEM((h, 1), F32)],
               compiler_params=_params("arbitrary"), name=name)(df_t, flog_t, bias, tri)


def _attn_tiles(s):
    return _tile(s, 1024, LANES)


BIAS_ROWS = 16


def _attn_prep(qkv, f_pieces, name):
    s = qkv.shape[0]
    a_w = qkv.shape[1] // 3
    npair = a_w // LANES
    t = _attn_tiles(s)
    scale = 1.0 / math.sqrt(HEAD_DIM)

    six = f_pieces[:, :2 * npair].reshape(3, npair, 2, s).transpose(1, 3, 2, 0).reshape(npair, s, 6)
    feat = jnp.concatenate([six, jnp.ones((npair, s, 1), BF16), jnp.zeros((npair, s, BIAS_ROWS - 7), BF16)], axis=-1)
    place_q = [[0.0] * (2 * LANES) for _ in range(BIAS_ROWS)]
    place_k = [[0.0] * (2 * LANES) for _ in range(BIAS_ROWS)]
    for hh in range(2):
        b0 = hh * LANES + (HEAD_DIM if hh == 0 else 0)
        for j in range(3):
            place_q[3 * hh + j][b0 + j] = 1.0
            place_q[6][b0 + 3 + j] = 1.0
            place_k[6][b0 + j] = 1.0
            place_k[3 * hh + j][b0 + 3 + j] = -1.0
    place_q = jnp.array(place_q, BF16)
    place_k = jnp.array(place_k, BF16)

    def body(q_ref, k_ref, v_ref, f_ref, pq_ref, pk_ref, qa_ref, ka_ref, va_ref):
        lane = lax.broadcasted_iota(jnp.int32, (1, LANES), 1)
        q2 = (q_ref[...].astype(F32) * scale).astype(BF16)
        k2, v2 = k_ref[...], v_ref[...]
        qx = lax.dot_general(f_ref[0], pq_ref[...], _NN, preferred_element_type=F32).astype(BF16)
        kx = lax.dot_general(f_ref[0], pk_ref[...], _NN, preferred_element_type=F32).astype(BF16)
        for hh in range(2):
            real = (lane < HEAD_DIM) if hh == 0 else (lane >= HEAD_DIM)
            cols = slice(hh * LANES, (hh + 1) * LANES)
            qa_ref[:, cols] = jnp.where(real, q2, qx[:, cols])
            ka_ref[:, cols] = jnp.where(real, k2, kx[:, cols])
            va_ref[:, cols] = jnp.where(real, v2, jnp.zeros_like(v2))

    def col(off):
        return pl.BlockSpec((t, LANES), lambda p, i: (i, off + p))

    out = pl.BlockSpec((t, 2 * LANES), lambda p, i: (i, p))
    place = pl.BlockSpec((BIAS_ROWS, 2 * LANES), lambda p, i: (0, 0))
    return _pc(body, out_shape=(_sds((s, 2 * a_w), BF16),) * 3, grid=(npair, s // t),
               in_specs=[col(0), col(npair), col(2 * npair), pl.BlockSpec((1, t, BIAS_ROWS), lambda p, i: (p, i, 0)),
                         place, place],
               out_specs=(out, out, out), compiler_params=_params("parallel", "parallel"), name=name)(
                   qkv, qkv, qkv, feat, place_q, place_k)


def _attn_fwd(qa, ka, va, name, exchange=None):
    s = qa.shape[0]
    a_w = qa.shape[1] // 2
    npair = a_w // LANES
    t = _attn_tiles(s)
    nq = s // t

    def body(q_ref, k_ref, v_ref, o_ref, lse_ref, m_sc, l_sc, acc_sc):
        qi = pl.program_id(1)
        first = lax.broadcasted_iota(jnp.int32, (1, LANES), 1) < HEAD_DIM
        m_sc[...] = jnp.full_like(m_sc, NEG_BIG)
        l_sc[...] = jnp.zeros_like(l_sc)
        acc_sc[...] = jnp.zeros_like(acc_sc)

        def step(ki, diag):
            k_rows = pl.ds(pl.multiple_of(ki * t, t), t)
            m_old = m_sc[...]
            keep = None
            if diag:
                keep = (lax.broadcasted_iota(jnp.int32, (t, t), 0) >= lax.broadcasted_iota(jnp.int32, (t, t), 1))
            m_new, rs, pv = [], [], []
            for hh in range(2):
                cols = slice(hh * LANES, (hh + 1) * LANES)
                sc = lax.dot_general(q_ref[:, cols], k_ref[k_rows, cols], _NT, preferred_element_type=F32)
                if diag:
                    sc = jnp.where(keep, sc, NEG_BIG)
                mo = m_old[:, hh * HEAD_DIM:hh * HEAD_DIM + 1]
                mn = jnp.maximum(mo, jnp.max(sc, axis=1, keepdims=True))
                p = jnp.exp(sc - mn)
                m_new.append(mn)
                rs.append(jnp.sum(p, axis=1, keepdims=True))
                pv.append(lax.dot_general(p.astype(BF16), v_ref[k_rows, cols], _NN, preferred_element_type=F32))
            m2 = jnp.where(first, m_new[0], m_new[1])
            alpha = jnp.exp(m_old - m2)
            m_sc[...] = m2
            l_sc[...] = alpha * l_sc[...] + jnp.where(first, rs[0], rs[1])
            acc_sc[...] = alpha * acc_sc[...] + pv[0] + pv[1]

        def below_diagonal(ki, carry):
            step(ki, False)
            return carry

        lax.fori_loop(0, qi, below_diagonal, 0)
        step(qi, True)
        l2 = l_sc[...]
        o_ref[...] = acc_sc[...] / l2
        lse_ref[...] = m_sc[...] + jnp.log(l2)

    qs = pl.BlockSpec((t, 2 * LANES), lambda p, qi: (qi, p))
    ks = pl.BlockSpec((s, 2 * LANES), lambda p, qi: (0, p))
    os_ = pl.BlockSpec((t, LANES), lambda p, qi: (qi, p))
    return _pc(body, exchange, out_shape=(_sds((s, a_w), F32), _sds((s, a_w), F32)), grid=(npair, nq),
               in_specs=[qs, ks, ks], out_specs=(os_, os_),
               scratch_shapes=[pltpu.VMEM((t, LANES), F32)] * 3,
               compiler_params=_params("parallel", "arbitrary"), name=name)(qa, ka, va)


def _attn_bwd(qa, ka, va, do, o, lse, name, exchange=None):
    s = qa.shape[0]
    a_w = qa.shape[1] // 2
    npair = a_w // LANES
    t = _attn_tiles(s)
    nq = s // t
    scale = 1.0 / math.sqrt(HEAD_DIM)

    def body(q_ref, k_ref, v_ref, do_ref, o_ref, lse_ref, dq_ref, dk_ref, dv_ref, qx_ref, kx_ref, dk_sc, dv_sc, kx_sc):
        ki = pl.program_id(1)
        first = lax.broadcasted_iota(jnp.int32, (1, LANES), 1) < HEAD_DIM

        @pl.when(ki == 0)
        def _():
            dq_ref[...] = jnp.zeros_like(dq_ref)
            qx_ref[...] = jnp.zeros_like(qx_ref)

        def step(qi, diag):
            rows = pl.ds(pl.multiple_of(qi * t, t), t)
            do2 = do_ref[rows, :]
            lse2 = lse_ref[rows, :]
            dd = do2.astype(F32) * o_ref[rows, :]
            keep = None
            if diag:
                keep = (lax.broadcasted_iota(jnp.int32, (t, t), 0) >= lax.broadcasted_iota(jnp.int32, (t, t), 1))
            dq_h, dk_h, dv_h = [], [], []
            for hh in range(2):
                sel = first if hh == 0 else jnp.logical_not(first)
                cols = slice(hh * LANES, (hh + 1) * LANES)
                qh, kh, vh = q_ref[rows, cols], k_ref[:, cols], v_ref[:, cols]
                delta = jnp.sum(jnp.where(sel, dd, 0.0), axis=1, keepdims=True)
                sc = lax.dot_general(qh, kh, _NT, preferred_element_type=F32)
                if diag:
                    sc = jnp.where(keep, sc, NEG_BIG)
                p = jnp.exp(sc - lse2[:, hh * HEAD_DIM:hh * HEAD_DIM + 1])
                dp = lax.dot_general(do2, vh, _NT, preferred_element_type=F32)
                ds_b = (p * (dp - delta)).astype(BF16)
                dv_h.append(lax.dot_general(p.astype(BF16), do2, _TN, preferred_element_type=F32))
                dk_h.append(lax.dot_general(ds_b, qh, _TN, preferred_element_type=F32))
                dq_h.append(lax.dot_general(ds_b, kh, _NN, preferred_element_type=F32))
            dq_ref[rows, :] += jnp.where(first, dq_h[0], dq_h[1]) * scale
            qx_ref[rows, :] += jnp.where(first, dq_h[1], dq_h[0])
            dk_new = jnp.where(first, dk_h[0], dk_h[1])
            kx_new = jnp.where(first, dk_h[1], dk_h[0])
            dv_new = jnp.where(first, dv_h[0], dv_h[1])
            if diag:
                dk_sc[...] = dk_new
                kx_sc[...] = kx_new
                dv_sc[...] = dv_new
            else:
                dk_sc[...] += dk_new
                kx_sc[...] += kx_new
                dv_sc[...] += dv_new

        def below_diagonal(qi, carry):
            step(qi, False)
            return carry

        step(ki, True)
        lax.fori_loop(ki + 1, nq, below_diagonal, 0)
        dk_ref[...] = dk_sc[...].astype(BF16)
        dv_ref[...] = dv_sc[...].astype(BF16)
        kx_ref[...] = kx_sc[...]

    ks2 = pl.BlockSpec((t, 2 * LANES), lambda p, ki: (ki, p))
    qs2 = pl.BlockSpec((s, 2 * LANES), lambda p, ki: (0, p))
    whole = pl.BlockSpec((s, LANES), lambda p, ki: (0, p))
    kout = pl.BlockSpec((t, LANES), lambda p, ki: (ki, p))
    return _pc(body, exchange,
               out_shape=(_sds((s, a_w), F32), _sds((s, a_w), BF16), _sds((s, a_w), BF16), _sds((s, a_w), F32),
                          _sds((s, a_w), F32)),
               grid=(npair, nq), in_specs=[qs2, ks2, ks2, whole, whole, whole],
               out_specs=(whole, kout, kout, whole, kout),
               scratch_shapes=[pltpu.VMEM((t, LANES), F32)] * 3,
               compiler_params=_params("parallel", "arbitrary"), name=name)(qa, ka, va, do, o, lse)

def _decay_grads(qx, kx, name):
    s, a_w = qx.shape
    n_heads = a_w // HEAD_DIM
    tr = _tile(s, 512, 8)
    pick_q = [[0.0] * LANES for _ in range(a_w)]
    pick_k = [[0.0] * LANES for _ in range(a_w)]
    for h in range(n_heads):
        b0 = (h // 2) * LANES + (HEAD_DIM if h % 2 == 0 else 0)
        pick_q[b0][h] = 1.0
        pick_k[b0 + 3][h] = 1.0
    pick_q = jnp.array(pick_q, BF16)
    pick_k = jnp.array(pick_k, BF16)

    def body(qx_ref, kx_ref, pq_ref, pk_ref, o_ref):
        o_ref[...] = _dot3(qx_ref[...], pq_ref[...]) - _dot3(kx_ref[...], pk_ref[...])

    row = pl.BlockSpec((tr, a_w), lambda i: (i, 0))
    pick = pl.BlockSpec((a_w, LANES), lambda i: (0, 0))
    return _pc(body, out_shape=_sds((s, LANES), F32), grid=(s // tr,), in_specs=[row, row, pick, pick],
               out_specs=pl.BlockSpec((tr, LANES), lambda i: (i, 0)),
               compiler_params=_params("parallel"), name=name)(qx, kx, pick_q, pick_k)


def _shift_down(z, k, rows):
    return jnp.where(rows >= k, pltpu.roll(z, k, 0), 0.0)


def _shift_up(z, k, rows, n):
    return jnp.where(rows < n - k, pltpu.roll(z, n - k, 0), 0.0)


def _conv_fwd(bcx, conv_w, name):
    s = bcx.shape[0]
    cw = bcx.shape[1] // 3
    nb = cw // LANES

    def body(b_ref, c_ref, x_ref, w_ref, cv_ref):
        rows = lax.broadcasted_iota(jnp.int32, (s, LANES), 0)
        z = c_ref[...] * x_ref[...]
        w = w_ref[...]
        y = w[2:3, :] * z + w[1:2, :] * _shift_down(z, 1, rows) + w[0:1, :] * _shift_down(z, 2, rows)
        cv_ref[...] = b_ref[...] * y

    def col(off):
        return pl.BlockSpec((s, LANES), lambda j: (0, j + off))

    return _pc(body, out_shape=_sds((s, cw), F32), grid=(nb,),
               in_specs=[col(0), col(nb), col(2 * nb), pl.BlockSpec((CONV_K, LANES), lambda j: (0, j))],
               out_specs=col(0), compiler_params=_params("parallel"), name=name)(bcx, bcx, bcx, conv_w)


def _conv_bwd(dcv, bcx, conv_w, name):
    s = bcx.shape[0]
    cw = bcx.shape[1] // 3
    nb = cw // LANES

    def body(dcv_ref, b_ref, c_ref, x_ref, w_ref, db_ref, dc_ref, dxc_ref, dw_ref):
        rows = lax.broadcasted_iota(jnp.int32, (s, LANES), 0)
        cv_, xv = c_ref[...], x_ref[...]
        z = cv_ * xv
        w = w_ref[...]
        z1 = _shift_down(z, 1, rows)
        z2 = _shift_down(z, 2, rows)
        y = w[2:3, :] * z + w[1:2, :] * z1 + w[0:1, :] * z2
        dcvv = dcv_ref[...]
        db_ref[...] = (dcvv * y).astype(BF16)
        dy = dcvv * b_ref[...]
        dw_ref[0:1, :] = jnp.sum(dy * z2, axis=0, keepdims=True)
        dw_ref[1:2, :] = jnp.sum(dy * z1, axis=0, keepdims=True)
        dw_ref[2:3, :] = jnp.sum(dy * z, axis=0, keepdims=True)
        dz = w[2:3, :] * dy + w[1:2, :] * _shift_up(dy, 1, rows, s) + w[0:1, :] * _shift_up(dy, 2, rows, s)
        dc_ref[...] = (dz * xv).astype(BF16)
        dxc_ref[...] = (dz * cv_).astype(BF16)

    def col(off):
        return pl.BlockSpec((s, LANES), lambda j: (0, j + off))

    wspec = pl.BlockSpec((CONV_K, LANES), lambda j: (0, j))
    db, dc, dxc, dw = _pc(body, out_shape=(_sds((s, cw), BF16),) * 3 + (_sds((CONV_K, cw), F32),), grid=(nb,),
                          in_specs=[col(0), col(0), col(nb), col(2 * nb), wspec],
                          out_specs=(col(0), col(0), col(0), wspec),
                          compiler_params=_params("parallel"), name=name)(dcv, bcx, bcx, bcx, conv_w)
    return db, dc, dxc, dw


def _group_matrix():
    idx = jnp.arange(LANES) // HEAD_DIM
    return (idx[:, None] == idx[None, :]).astype(BF16)


def _group_sum(v, gmat):
    return _dot3(v, gmat)


def _gnorm_fwd(att, cv, gg, name):
    s, a_w = att.shape
    cw = cv.shape[1]
    d = a_w + cw
    tr = _tile(s, 512, 16)
    gmat = _group_matrix()

    def body(att_ref, cv_ref, gg_ref, gm_ref, yn_ref):
        gm = gm_ref[...]
        for c0 in range(0, d, LANES):
            y = att_ref[:, c0:c0 + LANES] if c0 < a_w else cv_ref[:, c0 - a_w:c0 - a_w + LANES]
            ms = _group_sum(y * y, gm) * (1.0 / HEAD_DIM)
            yn_ref[:, c0:c0 + LANES] = (y * lax.rsqrt(ms + EPS) * gg_ref[:, c0:c0 + LANES]).astype(BF16)

    return _pc(body, out_shape=_sds((s, d), BF16), grid=(s // tr,),
               in_specs=[pl.BlockSpec((tr, a_w), lambda i: (i, 0)), pl.BlockSpec((tr, cw), lambda i: (i, 0)),
                         _vec_spec(d), pl.BlockSpec((LANES, LANES), lambda i: (0, 0))],
               out_specs=pl.BlockSpec((tr, d), lambda i: (i, 0)),
               compiler_params=_params("parallel"), name=name)(att, cv, gg, gmat)


def _gnorm_bwd(dyn, att, cv, gg, name):
    s, a_w = att.shape
    cw = cv.shape[1]
    d = a_w + cw
    tr = _tile(s, 256, 16)
    gmat = _group_matrix()

    def body(dyn_ref, att_ref, cv_ref, gg_ref, gm_ref, datt_ref, dcv_ref, dgg_ref):
        @pl.when(pl.program_id(0) == 0)
        def _():
            dgg_ref[...] = jnp.zeros_like(dgg_ref)

        gm = gm_ref[...]
        for c0 in range(0, d, LANES):
            y = att_ref[:, c0:c0 + LANES] if c0 < a_w else cv_ref[:, c0 - a_w:c0 - a_w + LANES]
            dv = dyn_ref[:, c0:c0 + LANES]
            r = lax.rsqrt(_group_sum(y * y, gm) * (1.0 / HEAD_DIM) + EPS)
            xhat = y * r
            dgg_ref[:, c0:c0 + LANES] += jnp.sum(dv * xhat, axis=0, keepdims=True)
            dxh = dv * gg_ref[:, c0:c0 + LANES]
            proj = _group_sum(dxh * xhat, gm) * (1.0 / HEAD_DIM)
            dy = r * (dxh - xhat * proj)
            if c0 < a_w:
                datt_ref[:, c0:c0 + LANES] = dy.astype(BF16)
            else:
                dcv_ref[:, c0 - a_w:c0 - a_w + LANES] = dy

    return _pc(body, out_shape=(_sds((s, a_w), BF16), _sds((s, cw), F32), _sds((1, d), F32)), grid=(s // tr,),
               in_specs=[pl.BlockSpec((tr, d), lambda i: (i, 0)), pl.BlockSpec((tr, a_w), lambda i: (i, 0)),
                         pl.BlockSpec((tr, cw), lambda i: (i, 0)), _vec_spec(d),
                         pl.BlockSpec((LANES, LANES), lambda i: (0, 0))],
               out_specs=(pl.BlockSpec((tr, a_w), lambda i: (i, 0)), pl.BlockSpec((tr, cw), lambda i: (i, 0)),
                          _vec_spec(d)),
               compiler_params=_params("arbitrary"), name=name)(dyn, att, cv, gg, gmat)


def _adamw_math(w, g, m, v):
    m_new = ADAM_B1 * m + (1.0 - ADAM_B1) * g
    v_new = ADAM_B2 * v + (1.0 - ADAM_B2) * (g * g)
    m_hat = m_new / (1.0 - ADAM_B1 ** ADAM_STEP)
    v_hat = v_new / (1.0 - ADAM_B2 ** ADAM_STEP)
    delta = -ADAM_LR * (m_hat / (jnp.sqrt(v_hat) + ADAM_EPS) + ADAM_WD * w)
    return delta, m_new, v_new


def _row_tile(r, c):
    return _tile(r, max(8, ((1 << 18) // c) // 8 * 8), 8)


def _adamw(w, g, m, v, name):
    r, c = w.shape
    tr = _row_tile(r, c)

    def body(w_ref, g_ref, m_ref, v_ref, d_ref, mo_ref, vo_ref):
        d, mn, vn = _adamw_math(w_ref[...], g_ref[...], m_ref[...], v_ref[...])
        d_ref[...] = d
        mo_ref[...] = mn
        vo_ref[...] = vn

    spec = pl.BlockSpec((tr, c), lambda i: (i, 0))
    return _pc(body, out_shape=(_sds((r, c), F32),) * 3, grid=(r // tr,), in_specs=[spec] * 4,
               out_specs=(spec,) * 3, compiler_params=_params("parallel"), name=name)(w, g, m, v)


def _adamw_halves(w, mine, theirs, m, v, core, name):
    r2, c = w.shape
    r = r2 // 2
    assert mine.shape == (r, c) and theirs.shape == (r, c)
    tr = _row_tile(r, c)
    nb = r // tr

    def body(core_ref, w_ref, a_ref, b_ref, m_ref, v_ref, g_ref, d_ref, mo_ref, vo_ref):
        g = jnp.where(pl.program_id(0) == core_ref[0], a_ref[...], b_ref[...])
        d, mn, vn = _adamw_math(w_ref[...], g, m_ref[...], v_ref[...])
        g_ref[...] = g
        d_ref[...] = d
        mo_ref[...] = mn
        vo_ref[...] = vn

    full = pl.BlockSpec((tr, c), lambda h, i, core_ref: (h * nb + i, 0))
    half = pl.BlockSpec((tr, c), lambda h, i, core_ref: (i, 0))
    grid_spec = pltpu.PrefetchScalarGridSpec(
        num_scalar_prefetch=1, grid=(2, nb), in_specs=[full, half, half, full, full], out_specs=(full,) * 4)
    return _pc(body, out_shape=(_sds((r2, c), F32),) * 4, grid_spec=grid_spec,
               compiler_params=_params("parallel", "parallel"), name=name)(core, w, mine, theirs, m, v)


def _ada_fwd(c16, ada_w, ada_b, name):
    d, n = ada_w.shape
    tn = _tile(n, 768, LANES)

    def body(c_ref, w_ref, b_ref, o_ref):
        cv = c_ref[...]
        sc = (cv * jax.nn.sigmoid(cv)).astype(BF16)
        o_ref[...] = lax.dot_general(sc, w_ref[...].astype(BF16), _NN, preferred_element_type=F32) + b_ref[...]

    return _pc(body, out_shape=_sds((16, n), F32), grid=(n // tn,),
               in_specs=[pl.BlockSpec((16, d), lambda j: (0, 0)), pl.BlockSpec((d, tn), lambda j: (0, j)),
                         pl.BlockSpec((1, tn), lambda j: (0, j))],
               out_specs=pl.BlockSpec((16, tn), lambda j: (0, j)),
               compiler_params=_params("parallel"), name=name)(c16, ada_w, ada_b)


def _ada_update(c16_t, dmod16, w, m, v, name):
    r, c = w.shape
    tr = _row_tile(r, c)

    def body(c_ref, dm_ref, w_ref, m_ref, v_ref, g_ref, d_ref, mo_ref, vo_ref):
        cv = c_ref[...]
        sc = (cv * jax.nn.sigmoid(cv)).astype(BF16)
        g = lax.dot_general(sc, dm_ref[...].astype(BF16), _NN, preferred_element_type=F32)
        d, mn, vn = _adamw_math(w_ref[...], g, m_ref[...], v_ref[...])
        g_ref[...] = g
        d_ref[...] = d
        mo_ref[...] = mn
        vo_ref[...] = vn

    spec = pl.BlockSpec((tr, c), lambda i: (i, 0))
    return _pc(body, out_shape=(_sds((r, c), F32),) * 4, grid=(r // tr,),
               in_specs=[pl.BlockSpec((tr, 16), lambda i: (i, 0)), pl.BlockSpec((16, c), lambda i: (0, 0)),
                         spec, spec, spec],
               out_specs=(spec,) * 4, compiler_params=_params("parallel"), name=name)(c16_t, dmod16, w, m, v)


def _add_half(dw, recv, core, name):
    _, _, r, w = dw.shape
    tr = _tile(r, 512, 16)

    def body(core_ref, a_ref, b_ref, o_ref):
        o_ref[...] = (a_ref[...].astype(F32) + b_ref[...].astype(F32)).astype(BF16)

    grid_spec = pltpu.PrefetchScalarGridSpec(
        num_scalar_prefetch=1, grid=(N_CHIPS, r // tr),
        in_specs=[pl.BlockSpec((None, None, tr, w), lambda s, i, core_ref: (s, core_ref[0], i, 0)),
                  pl.BlockSpec((None, tr, w), lambda s, i, core_ref: (s, i, 0))],
        out_specs=pl.BlockSpec((None, tr, w), lambda s, i, core_ref: (s, i, 0)))
    return _pc(body, out_shape=_sds((N_CHIPS, r, w), BF16), grid_spec=grid_spec,
               compiler_params=_params("parallel", "parallel"), name=name)(core, dw, recv)


def _sum_chips(own, recv, chip, name):
    _, r, w = own.shape
    tr = _tile(r, 512, 16)

    def body(chip_ref, own_ref, p_ref, o_ref):
        acc = own_ref[...].astype(F32)
        for q in range(N_CHIPS - 1):
            acc = acc + p_ref[q].astype(F32)
        o_ref[...] = acc

    grid_spec = pltpu.PrefetchScalarGridSpec(
        num_scalar_prefetch=1, grid=(r // tr,),
        in_specs=[pl.BlockSpec((None, tr, w), lambda i, chip_ref: (chip_ref[0], i, 0)),
                  pl.BlockSpec((N_CHIPS - 1, tr, w), lambda i, chip_ref: (0, i, 0))],
        out_specs=pl.BlockSpec((tr, w), lambda i, chip_ref: (i, 0)))
    return _pc(body, out_shape=_sds((r, w), F32), grid_spec=grid_spec,
               compiler_params=_params("parallel"), name=name)(chip, own, recv)


def _sum_devices(parts, name):
    nd, r, w = parts.shape

    def body(p_ref, o_ref):
        acc = p_ref[0]
        for q in range(1, nd):
            acc = acc + p_ref[q]
        o_ref[...] = acc

    return _pc(body, out_shape=_sds((r, w), F32), name=name)(parts)


def _place():
    x, y, c = lax.axis_index("x"), lax.axis_index("y"), lax.axis_index("c")
    chips = [(1 - x, y), (x, 1 - y), (1 - x, 1 - y)]
    return x, y, c, chips


def _all_gather_small(blk, name):
    r, w = blk.shape

    def body(x_ref, out_ref, send_sems, recv_sems, local_sem):
        x, y, c, chips = _place()
        me, sibling = (x, y, c), (x, y, 1 - c)

        def rows(px, py, pc):
            return out_ref.at[pl.ds((4 * px + 2 * py + pc) * r, r), :]

        def copy(k, block, to, src=None):
            return pltpu.make_async_remote_copy(
                src_ref=rows(*block) if src is None else src, dst_ref=rows(*block),
                send_sem=send_sems.at[k], recv_sem=recv_sems.at[k], device_id=to, device_id_type=MESH)

        mine = pltpu.make_async_copy(x_ref, rows(*me), local_sem)
        mine.start()
        first = [copy(0, me, sibling, src=x_ref)]
        first += [copy(1 + j, me, (*chip, c), src=x_ref) for j, chip in enumerate(chips)]
        for cp in first:
            cp.start()
        passed = [copy(4 + j, (*chip, c), sibling) for j, chip in enumerate(chips)]
        for j, chip in enumerate(chips):
            copy(1 + j, (*chip, c), me).wait_recv()
            passed[j].start()
        copy(0, sibling, me).wait_recv()
        for j, chip in enumerate(chips):
            copy(4 + j, (*chip, 1 - c), me).wait_recv()
        for cp in first + passed:
            cp.wait_send()
        mine.wait()

    return _pc(body, out_shape=_sds((N_DEV * r, w), blk.dtype),
               in_specs=[pl.BlockSpec(memory_space=pltpu.VMEM)], out_specs=pl.BlockSpec(memory_space=pltpu.VMEM),
               scratch_shapes=[pltpu.SemaphoreType.DMA((7,)), pltpu.SemaphoreType.DMA((7,)), pltpu.SemaphoreType.DMA],
               name=name)(blk)


def _remote(src, dst, send_sems, recv_sems, k, to):
    return pltpu.make_async_remote_copy(src_ref=src, dst_ref=dst, send_sem=send_sems.at[k], recv_sem=recv_sems.at[k],
                                        device_id=to, device_id_type=MESH)


def _exchange_of(inputs, out_shapes, n_sems, copies, aliases=None):
    def start(src, dst, send_sems, recv_sems):
        for cp in copies(src, dst, send_sems, recv_sems)[0]:
            cp.start()

    def finish(src, dst, send_sems, recv_sems):
        sends, arrivals = copies(src, dst, send_sems, recv_sems)
        for cp in arrivals:
            cp.wait_recv()
        for cp in sends:
            cp.wait_send()

    return _Exchange(inputs, out_shapes, n_sems, start, finish, aliases)


def _run_exchange(ex, name):
    n_in, n_out = len(ex.inputs), len(ex.out_shapes)

    def body(*refs):
        src, dst = refs[:n_in], refs[n_in:n_in + n_out]
        send_sems, recv_sems = refs[n_in + n_out:]
        ex.start(src, dst, send_sems, recv_sems)
        ex.finish(src, dst, send_sems, recv_sems)

    ex.set_results(pl.pallas_call(
        body, out_shape=tuple(ex.out_shapes), in_specs=[_ANY] * n_in, out_specs=(_ANY,) * n_out,
        scratch_shapes=[pltpu.SemaphoreType.DMA((ex.n_sems,)), pltpu.SemaphoreType.DMA((ex.n_sems,))],
        input_output_aliases=ex.aliases, name=name)(*ex.inputs))


def _gather_ici_exchange(shards):
    n = len(shards)

    def copies(own, out, send_sems, recv_sems):
        x, y, c, chips = _place()
        my_chip = 2 * x + y
        sends, arrivals = [], []
        for i in range(n):
            for j, chip in enumerate(chips):
                to = (*chip, c)
                sends.append(_remote(own[i].at[c], out[i].at[my_chip, c], send_sems, recv_sems, 4 * i + j, to))
                arrivals.append(_remote(own[i].at[c], out[i].at[2 * chip[0] + chip[1], c], send_sems, recv_sems, 4 * i + j, to))
            whole = _remote(own[i], out[i].at[my_chip], send_sems, recv_sems, 4 * i + 3, (x, y, 1 - c))
            sends.append(whole)
            arrivals.append(whole)
        return sends, arrivals

    return _exchange_of(shards, [_sds((N_CHIPS,) + s.shape, s.dtype) for s in shards], 4 * n, copies)


def _gather_pass_exchange(gathered):
    n = len(gathered)

    def copies(src, dst, send_sems, recv_sems):
        x, y, c, chips = _place()
        sends, arrivals = [], []
        for i in range(n):
            for j, chip in enumerate(chips):
                idx = 2 * chip[0] + chip[1]
                sends.append(_remote(src[i].at[idx, c], dst[i].at[idx, c], send_sems, recv_sems, 3 * i + j, (x, y, 1 - c)))
                arrivals.append(_remote(src[i].at[idx, c], dst[i].at[idx, 1 - c], send_sems, recv_sems, 3 * i + j, (x, y, 1 - c)))
        return sends, arrivals

    return _exchange_of(gathered, [_sds(g.shape, g.dtype) for g in gathered], 3 * n, copies,
                        aliases={i: i for i in range(n)})


def _reduce_sibling_exchange(grads):
    n = len(grads)

    def copies(src, dst, send_sems, recv_sems):
        x, y, c, _ = _place()
        both = [_remote(src[i].at[s, 1 - c], dst[i].at[s], send_sems, recv_sems, N_CHIPS * i + s, (x, y, 1 - c))
                for i in range(n) for s in range(N_CHIPS)]
        return both, both

    return _exchange_of(grads, [_sds((N_CHIPS,) + g.shape[2:], g.dtype) for g in grads], N_CHIPS * n, copies)


def _reduce_chips_exchange(parts):
    n = len(parts)

    def copies(src, dst, send_sems, recv_sems):
        x, y, c, chips = _place()
        both = [_remote(src[i].at[2 * chip[0] + chip[1]], dst[i].at[j], send_sems, recv_sems, 3 * i + j, (*chip, c))
                for i in range(n) for j, chip in enumerate(chips)]
        return both, both

    return _exchange_of(parts, [_sds((N_CHIPS - 1,) + p.shape[1:], p.dtype) for p in parts], 3 * n, copies)


def _share_exchange(halves):
    n = len(halves)

    def copies(src, dst, send_sems, recv_sems):
        x, y, c, _ = _place()
        both = [_remote(src[i], dst[i], send_sems, recv_sems, i, (x, y, 1 - c)) for i in range(n)]
        return both, both

    return _exchange_of(halves, [_sds(h.shape, h.dtype) for h in halves], n, copies)


HEAD_ROWS = 16


class _WeightTraffic:
    def __init__(self, shards, core, chip):
        self.shards, self.core, self.chip = shards, core, chip
        self.gather, self.grads, self.reduce, self.chip_sums, self.half_sums, self.shared = {}, {}, {}, {}, {}, {}

    def gather_ici(self, grp):
        self.gather[grp] = _gather_ici_exchange(self.shards[grp])
        return self.gather[grp]

    def gather_pass(self, grp):
        self.gather[grp] = _gather_pass_exchange(self.gather[grp].results)
        return self.gather[grp]

    def weights(self, grp):
        return [g.reshape(-1, g.shape[-1]) for g in self.gather[grp].results]

    def reduce_sibling(self, grp, grads):
        self.grads[grp] = [g.reshape(N_CHIPS, 2, g.shape[0] // (2 * N_CHIPS), g.shape[1]) for g in grads]
        self.reduce[grp] = _reduce_sibling_exchange(self.grads[grp])
        return self.reduce[grp]

    def add_halves(self, grp):
        self.chip_sums[grp] = [_add_half(g, r, self.core, "add_half_%s%d" % (grp, i))
                               for i, (g, r) in enumerate(zip(self.grads[grp], self.reduce[grp].results))]

    def reduce_chips(self, grp):
        self.reduce[grp] = _reduce_chips_exchange(self.chip_sums[grp])
        return self.reduce[grp]

    def sum_chips(self, grp):
        self.half_sums[grp] = [_sum_chips(o, p, self.chip, "sum_chips_%s%d" % (grp, i))
                               for i, (o, p) in enumerate(zip(self.chip_sums[grp], self.reduce[grp].results))]

    def share(self, grp):
        self.shared[grp] = _share_exchange(self.half_sums[grp])
        return self.shared[grp]

    def totals(self, grp):
        return list(zip(self.half_sums[grp], self.shared[grp].results))


def _ffn_fwd(x, norm_g, shift, scale, gate, wg_t, wu_t, wd, tag, up_exchange=None, down_exchange=None):
    h = _norm_mod_fwd(x, norm_g, shift, scale, tag + "_norm_fwd")
    a, u, hid = _ffn_up(h, wg_t, wu_t, tag + "_up", exchange=up_exchange)
    wd = wd() if callable(wd) else wd
    x_out, f = _mm(hid, wd, "nn", F32, tag + "_down", res=x, gate=gate, aux_dtype=BF16,
                   exchange=down_exchange() if down_exchange else None)
    return x_out, (h, a, u, hid, f)


def _ffn_bwd(dx_out, df, x, saved, norm_g, scale, wg_t, wu_t, wd, tag, traffic, below=None, dact_exchange=None,
             dw_exchange=None, finish_reduction=False):
    h, a, u, hid, _ = saved
    f_below, gate_below = below if below else (None, None)
    da, du = _ffn_dact(df, wd, a, u, tag + "_dact", exchange=dact_exchange)
    dwd = _mm(hid, df, "tn", BF16, tag + "_dwd", exchange=dw_exchange() if dw_exchange else None)
    if not finish_reduction:
        dwg_t = _mm(da, h, "tn", BF16, tag + "_dwg")
        dwu_t = _mm(du, h, "tn", BF16, tag + "_dwu")
        dh = _mm(da, wg_t, "nn", F32, tag + "_dh_a", exchange=traffic.reduce_sibling(tag, [dwg_t, dwu_t, dwd]))
        traffic.add_halves(tag)
        dh = _mm(du, wu_t, "nn", F32, tag + "_dh_u", res=dh)
        dx, dshift, dscale, dnorm_g, *gated = _norm_mod_bwd(dh, x, norm_g, scale, dx_out, tag + "_norm_bwd",
                                                            f=f_below, gate=gate_below)
        return dx, (dshift, dscale, dnorm_g), gated
    kd, kg, ku = tag + "_wd", tag + "_wg", tag + "_wu"
    dwg_t = _mm(da, h, "tn", BF16, tag + "_dwg", exchange=traffic.reduce_sibling(kd, [dwd]))
    traffic.add_halves(kd)
    dwu_t = _mm(du, h, "tn", BF16, tag + "_dwu",
                exchange=_join(traffic.reduce_chips(kd), traffic.reduce_sibling(kg, [dwg_t])))
    traffic.add_halves(kg)
    dh = _mm(da, wg_t, "nn", F32, tag + "_dh_a",
             exchange=_join(traffic.reduce_chips(kg), traffic.reduce_sibling(ku, [dwu_t])))
    traffic.add_halves(ku)
    traffic.sum_chips(kd)
    dh = _mm(du, wu_t, "nn", F32, tag + "_dh_u", res=dh, exchange=_join(traffic.reduce_chips(ku), traffic.share(kd)))
    traffic.sum_chips(kg)
    traffic.sum_chips(ku)
    dx, dshift, dscale, dnorm_g, *gated = _norm_mod_bwd(dh, x, norm_g, scale, dx_out, tag + "_norm_bwd", f=f_below,
                                                        gate=gate_below, exchange=_join(traffic.share(kg), traffic.share(ku)))
    return dx, (dshift, dscale, dnorm_g), gated


def _layer_step(x, target, mod, gains, forget_bias, conv_w, traffic, att_w, in_shard, in_rows):
    sh1, sc1, g1, sh2, sc2, g2, sh3, sc3, g3 = mod
    norm1_g, norm2_g, norm3_g, final_g, group_g = gains
    s, d = x.shape
    n_heads = att_w // HEAD_DIM
    npair = n_heads // 2
    gate1, gate3 = 0.5 * g1, 0.5 * g3

    def split_w_in(w_in_pad):
        w_in_t = w_in_pad.reshape(N_CHIPS, in_rows, d)[:, :in_shard].reshape(N_CHIPS * in_shard, d)
        return (w_in_t[:3 * att_w], _pad_rows(w_in_t[3 * att_w:3 * att_w + n_heads], LANES), w_in_t[3 * att_w + n_heads:])

    _run_exchange(traffic.gather_ici("ffn1_gu"), "gather_ffn1_ici")
    _run_exchange(traffic.gather_pass("ffn1_gu"), "gather_ffn1_pass")
    wg1_t, wu1_t = traffic.weights("ffn1_gu")

    def wd1_ready():
        _run_exchange(traffic.gather_pass("ffn1_d"), "gather_ffn1_down_pass")
        return traffic.weights("ffn1_d")[0]

    x1, saved1 = _ffn_fwd(x, norm1_g, sh1, sc1, gate1, wg1_t, wu1_t, wd1_ready, "ffn1",
                          up_exchange=_join(traffic.gather_ici("ffn1_d"), traffic.gather_ici("mix")),
                          down_exchange=lambda: traffic.gather_pass("mix"))
    wd1 = traffic.weights("ffn1_d")[0]
    w_in_pad, w_out = traffic.weights("mix")
    wqkv_t, wf_t, wbcx_t = split_w_in(w_in_pad)

    h2 = _norm_mod_fwd(x1, norm2_g, sh2, sc2, "mix_norm_fwd")
    qkv = _mm(h2, wqkv_t, "nt", BF16, "mix_proj_qkv")
    bcx = _mm(h2, wbcx_t, "nt", F32, "mix_proj_bcx")
    flog = _mm(h2, wf_t, "nt", F32, "mix_proj_f")
    flog_t = jnp.pad(flog[:, :n_heads].T, ((0, HEAD_ROWS - n_heads), (0, 0)))
    bias_col = jnp.pad(forget_bias, (0, HEAD_ROWS - n_heads))[:, None]
    f_pieces = _forget_fwd(flog_t, bias_col, "forget_fwd")
    qa, ka, va = _attn_prep(qkv, f_pieces, "attn_prep")
    att, lse = _attn_fwd(qa, ka, va, "attn_fwd", exchange=traffic.gather_ici("ffn2"))
    cv = _conv_fwd(bcx, conv_w, "conv_fwd")
    yn = _gnorm_fwd(att, cv, group_g, "gnorm_fwd")
    x2, mix = _mm(yn, w_out, "nn", F32, "mix_out", res=x1, gate=g2, aux_dtype=BF16, exchange=traffic.gather_pass("ffn2"))
    wg2_t, wu2_t, wd2 = traffic.weights("ffn2")

    x3, saved3 = _ffn_fwd(x2, norm3_g, sh3, sc3, gate3, wg2_t, wu2_t, wd2, "ffn2")

    dx3, loss_row, dfinal_g, df2, dgate3 = _final_loss(x3, final_g, target, saved3[4], gate3, "final_loss")

    dx2, (dsh3, dsc3, dnorm3_g), (dmix, dg2) = _ffn_bwd(
        dx3, df2, x2, saved3, norm3_g, sc3, wg2_t, wu2_t, wd2, "ffn2", traffic, below=(mix, g2))
    dyn = _mm(dmix, w_out, "nt", F32, "mix_out_dyn")
    dw_out = _mm(yn, dmix, "tn", BF16, "mix_out_dw")
    datt, dcv, dgroup_g = _gnorm_bwd(dyn, att, cv, group_g, "gnorm_bwd")
    db, dc, dxc, dconv_w = _conv_bwd(dcv, bcx, conv_w, "conv_bwd")
    dbcx = jnp.concatenate([db, dc, dxc], axis=1)
    dq, dk, dv, qx, kx = _attn_bwd(qa, ka, va, datt, att, lse, "attn_bwd", exchange=traffic.reduce_chips("ffn2"))
    traffic.sum_chips("ffn2")
    dqkv = jnp.concatenate([dq.astype(BF16), dk, dv], axis=1)
    df_t = _decay_grads(qx, kx, "decay_grads")[:, :HEAD_ROWS].T
    dflog_t, dbias_col = _forget_bwd(df_t, flog_t, bias_col, "forget_bwd")
    dflog = jnp.pad(dflog_t[:n_heads].T, ((0, 0), (0, LANES - n_heads))).astype(BF16)
    dh2 = _mm(dqkv, wqkv_t, "nn", F32, "mix_dh_qkv", exchange=traffic.share("ffn2"))
    dh2 = _mm(dbcx, wbcx_t, "nn", F32, "mix_dh_bcx", res=dh2)
    dh2 = _mm(dflog, wf_t, "nn", F32, "mix_dh_f", res=dh2)
    dwqkv_t = _mm(dqkv, h2, "tn", BF16, "mix_dw_qkv")
    dwbcx_t = _mm(dbcx, h2, "tn", BF16, "mix_dw_bcx")
    dwf_t = _mm(dflog, h2, "tn", BF16, "mix_dw_f")
    dw_in_t = jnp.concatenate([dwqkv_t, dwf_t[:n_heads], dwbcx_t], axis=0).reshape(N_CHIPS, in_shard, d)
    dw_in_t = jnp.pad(dw_in_t, ((0, 0), (0, in_rows - in_shard), (0, 0))).reshape(N_CHIPS * in_rows, d)
    dx1, dsh2, dsc2, dnorm2_g, df1, dgate1 = _norm_mod_bwd(
        dh2, x1, norm2_g, sc2, dx2, "mix_norm_bwd", f=saved1[4], gate=gate1,
        exchange=traffic.reduce_sibling("mix", [dw_in_t, dw_out]))
    traffic.add_halves("mix")

    def share_mix():
        traffic.sum_chips("mix")
        return traffic.share("mix")

    dx, (dsh1, dsc1, dnorm1_g), _ = _ffn_bwd(
        dx1, df1, x, saved1, norm1_g, sc1, wg1_t, wu1_t, wd1, "ffn1", traffic,
        dact_exchange=traffic.reduce_chips("mix"), dw_exchange=share_mix, finish_reduction=True)

    dmod = [dsh1, dsc1, 0.5 * dgate1, dsh2, dsc2, dg2, dsh3, dsc3, 0.5 * dgate3]
    dgains = [dnorm1_g, dnorm2_g, dnorm3_g, dfinal_g, dgroup_g]
    dbias = dbias_col[:n_heads, 0]
    return dx, loss_row, dmod, dgains, dbias, dconv_w


SMALL_ROWS = 24
ROW_GAINS, ROW_LOSS, ROW_FORGET, ROW_CONV, ROW_MOD = 0, 5, 6, 7, 10
PROW_ADA_B, PROW_GAINS, PROW_FORGET, PROW_CONV = 0, 9, 14, 15


def _round_up(n, m):
    return -(-n // m) * m


def _pad_rows(a, rows):
    return jnp.pad(a, ((0, rows - a.shape[0]), (0, 0)))


def _halves(a):
    return a.reshape(2, a.shape[0] // 2, a.shape[1])


def _rows_at(a, r0, total, width):
    return jnp.pad(a, ((r0, total - r0 - a.shape[0]), (0, width - a.shape[1])))


def kernel(x, c, ada_w, ada_b, norm1_g, ffn1_w_gate, ffn1_w_up, ffn1_w_down, norm2_g, w_in, forget_bias, conv_w, group_norm_g, w_out, norm3_g, ffn2_w_gate, ffn2_w_up, ffn2_w_down, final_g, loss_target, m_ada_w, m_ada_b, m_norm1_g, m_ffn1_w_gate, m_ffn1_w_up, m_ffn1_w_down, m_norm2_g, m_w_in, m_forget_bias, m_conv_w, m_group_norm_g, m_w_out, m_norm3_g, m_ffn2_w_gate, m_ffn2_w_up, m_ffn2_w_down, m_final_g, v_ada_w, v_ada_b, v_norm1_g, v_ffn1_w_gate, v_ffn1_w_up, v_ffn1_w_down, v_norm2_g, v_w_in, v_forget_bias, v_conv_w, v_group_norm_g, v_w_out, v_norm3_g, v_ffn2_w_gate, v_ffn2_w_up, v_ffn2_w_down, v_final_g):
    xi, yi, ci = lax.axis_index("x"), lax.axis_index("y"), lax.axis_index("c")
    chip = 2 * xi + yi
    dev = 4 * xi + 2 * yi + ci
    _, s, d = x.shape
    att_w = d // 2
    conv_width = d - att_w
    n_heads = att_w // HEAD_DIM
    in_shard = w_in.shape[1]
    in_rows = _round_up(in_shard, 32)
    cs = conv_w.shape[1]
    mod_shard = ada_w.shape[1]
    assert N_MOD * d == N_CHIPS * mod_shard and conv_width == N_CHIPS * cs and n_heads % 2 == 0

    pack0 = _rows_at(c, 0, 8, d) + _rows_at(conv_w, 1, 8, d)
    got0 = _all_gather_small(pack0, "gather_cond").reshape(N_DEV, 8, d)
    c16 = _pad_rows(got0[:, 0, :], 16)
    conv_full = got0[0::2, 1:1 + CONV_K, :cs].transpose(1, 0, 2).reshape(CONV_K, conv_width)

    ada_b_mine = lax.dynamic_slice(ada_b, (chip * mod_shard,), (mod_shard,))[None, :]
    mod_part = _ada_fwd(c16, ada_w, ada_b_mine, "ada_fwd")
    got1 = _all_gather_small(mod_part, "gather_mod").reshape(N_DEV, 16, mod_shard)
    mod_mine = lax.dynamic_index_in_dim(got1[0::2], dev, axis=1, keepdims=False).reshape(N_MOD, d)
    mod = [mod_mine[i:i + 1] for i in range(N_MOD)]

    def t_bf(w):
        return w.T.astype(BF16)

    shards = {"ffn1_gu": [_halves(t_bf(ffn1_w_gate)), _halves(t_bf(ffn1_w_up))], "ffn1_d": [_halves(ffn1_w_down.astype(BF16))],
              "mix": [_halves(_pad_rows(t_bf(w_in), in_rows)), _halves(w_out.astype(BF16))],
              "ffn2": [_halves(t_bf(ffn2_w_gate)), _halves(t_bf(ffn2_w_up)), _halves(ffn2_w_down.astype(BF16))]}
    core = ci.astype(jnp.int32).reshape(1)
    chip_arr = chip.astype(jnp.int32).reshape(1)
    traffic = _WeightTraffic(shards, core, chip_arr)

    gains = [g[None, :] for g in (norm1_g, norm2_g, norm3_g, final_g, group_norm_g)]
    dx, loss_row, dmod, dgains, dbias, dconv_w = _layer_step(
        x[0], loss_target[0], mod, gains, forget_bias, conv_full, traffic, att_w, in_shard, in_rows)

    pack = sum(_rows_at(g, ROW_GAINS + i, SMALL_ROWS, d) for i, g in enumerate(dgains))
    pack += _rows_at(loss_row, ROW_LOSS, SMALL_ROWS, d) + _rows_at(dbias[None, :], ROW_FORGET, SMALL_ROWS, d)
    pack += _rows_at(dconv_w, ROW_CONV, SMALL_ROWS, d)
    pack += sum(_rows_at(g, ROW_MOD + i, SMALL_ROWS, d) for i, g in enumerate(dmod))
    got2 = _all_gather_small(pack, "gather_small_grads").reshape(N_DEV, SMALL_ROWS, d)
    tot = _sum_devices(got2, "sum_small_grads")
    loss = tot[ROW_LOSS, 0]
    grad_ada_b = tot[ROW_MOD:ROW_MOD + N_MOD].reshape(N_MOD * d)
    grad_conv = lax.dynamic_slice(tot[ROW_CONV:ROW_CONV + CONV_K], (0, chip * cs), (CONV_K, cs))
    dmod_all = got2[:, ROW_MOD:ROW_MOD + N_MOD, :].reshape(N_DEV, N_MOD * d)
    dmod16 = _pad_rows(lax.dynamic_slice(dmod_all, (0, chip * mod_shard), (N_DEV, mod_shard)), 16)

    totals = (traffic.totals("ffn1_wg") + traffic.totals("ffn1_wu") + traffic.totals("ffn1_wd")
              + traffic.totals("mix") + traffic.totals("ffn2"))

    names = ("ffn1_w_gate", "ffn1_w_up", "ffn1_w_down", "w_in", "w_out", "ffn2_w_gate", "ffn2_w_up", "ffn2_w_down")
    transposed = ("ffn1_w_gate", "ffn1_w_up", "w_in", "ffn2_w_gate", "ffn2_w_up")
    params = {"ffn1_w_gate": (ffn1_w_gate, m_ffn1_w_gate, v_ffn1_w_gate), "ffn1_w_up": (ffn1_w_up, m_ffn1_w_up, v_ffn1_w_up),
              "ffn1_w_down": (ffn1_w_down, m_ffn1_w_down, v_ffn1_w_down), "w_in": (w_in, m_w_in, v_w_in),
              "w_out": (w_out, m_w_out, v_w_out), "ffn2_w_gate": (ffn2_w_gate, m_ffn2_w_gate, v_ffn2_w_gate),
              "ffn2_w_up": (ffn2_w_up, m_ffn2_w_up, v_ffn2_w_up), "ffn2_w_down": (ffn2_w_down, m_ffn2_w_down, v_ffn2_w_down)}
    out = {}
    for name_, (mine, theirs) in zip(names, totals):
        w, m, v = params[name_]
        if name_ in transposed:
            w, m, v = w.T, m.T, v.T
        if name_ == "w_in":
            both = jnp.where(ci == 0, jnp.concatenate([mine, theirs]), jnp.concatenate([theirs, mine]))[:in_shard]
            res = (both,) + tuple(_adamw(w, both, m, v, "adamw_" + name_))
        else:
            res = _adamw_halves(w, mine, theirs, m, v, core, "adamw_" + name_)
        out[name_] = tuple(r.T for r in res) if name_ in transposed else tuple(res)
    c16_t = c16.T
    out["ada_w"] = tuple(_ada_update(c16_t, dmod16, ada_w, m_ada_w, v_ada_w, "adamw_ada_w"))

    def small_pack(ada_b_, gains_, forget_, conv_):
        p = _rows_at(ada_b_.reshape(N_MOD, d), PROW_ADA_B, SMALL_ROWS, d)
        p += sum(_rows_at(g[None, :], PROW_GAINS + i, SMALL_ROWS, d) for i, g in enumerate(gains_))
        p += _rows_at(forget_[None, :], PROW_FORGET, SMALL_ROWS, d) + _rows_at(conv_, PROW_CONV, SMALL_ROWS, d)
        return p

    g_gains = [tot[ROW_GAINS + i] for i in range(5)]
    g_forget = tot[ROW_FORGET, :n_heads]
    sw = small_pack(ada_b, (norm1_g, norm2_g, norm3_g, final_g, group_norm_g), forget_bias, conv_w)
    sm = small_pack(m_ada_b, (m_norm1_g, m_norm2_g, m_norm3_g, m_final_g, m_group_norm_g), m_forget_bias, m_conv_w)
    sv = small_pack(v_ada_b, (v_norm1_g, v_norm2_g, v_norm3_g, v_final_g, v_group_norm_g), v_forget_bias, v_conv_w)
    sg = small_pack(grad_ada_b, g_gains, g_forget, grad_conv)
    small = (sg,) + tuple(_adamw(sw, sg, sm, sv, "adamw_small"))

    def unpack(p):
        r = {"ada_b": p[PROW_ADA_B:PROW_ADA_B + N_MOD].reshape(N_MOD * d), "forget_bias": p[PROW_FORGET, :n_heads],
             "conv_w": p[PROW_CONV:PROW_CONV + CONV_K, :cs]}
        for i, nm in enumerate(("norm1_g", "norm2_g", "norm3_g", "final_g", "group_norm_g")):
            r[nm] = p[PROW_GAINS + i]
        return r

    small = [unpack(p) for p in small]
    order = ("ada_w", "ada_b", "norm1_g", "ffn1_w_gate", "ffn1_w_up", "ffn1_w_down", "norm2_g", "w_in", "forget_bias",
             "conv_w", "group_norm_g", "w_out", "norm3_g", "ffn2_w_gate", "ffn2_w_up", "ffn2_w_down", "final_g")
    result = [loss, dx[None]]
    for k in range(4):
        result += [out[nm][k] if nm in out else small[k][nm] for nm in order]
    return tuple(result)
```

```python
import functools
import math

import jax
import jax.numpy as jnp
from jax import lax
from jax.experimental import pallas as pl
from jax.experimental.pallas import tpu as pltpu

F32 = jnp.float32
BF16 = jnp.bfloat16

HEAD_DIM = 64
CONV_K = 3
N_MOD = 9
EPS = 1e-6
ADAM_LR = 0.001
ADAM_B1 = 0.9
ADAM_B2 = 0.999
ADAM_EPS = 1e-08
ADAM_WD = 0.01
ADAM_STEP = 10

LANES = 128
N_CHIPS = 4
N_DEV = 8
VMEM_LIMIT_BYTES = 56 * 1024 * 1024
NEG_BIG = -1e30
MESH = pl.DeviceIdType.MESH

_NT = (((1,), (1,)), ((), ()))
_NN = (((1,), (0,)), ((), ()))
_TN = (((0,), (0,)), ((), ()))


def _params(*sem):
    return pltpu.CompilerParams(dimension_semantics=sem, vmem_limit_bytes=VMEM_LIMIT_BYTES)


class _Exchange:
    def __init__(self, inputs, out_shapes, n_sems, start, finish, aliases=None):
        self.inputs, self.out_shapes, self.n_sems = list(inputs), list(out_shapes), n_sems
        self.start, self.finish, self.aliases = start, finish, dict(aliases or {})
        self.results = None

    def set_results(self, results):
        self.results = list(results)


class _SemaphoreWindow:
    def __init__(self, sems, base):
        self.sems, self.base = sems, base
        self.at = self

    def __getitem__(self, k):
        return self.sems.at[self.base + k]


class _JoinedExchange(_Exchange):
    def __init__(self, parts):
        self.parts = parts
        assert all(not p.aliases for p in parts)

        def each(method, src, dst, send_sems, recv_sems):
            i0 = o0 = s0 = 0
            for p in parts:
                i1, o1 = i0 + len(p.inputs), o0 + len(p.out_shapes)
                getattr(p, method)(src[i0:i1], dst[o0:o1], _SemaphoreWindow(send_sems, s0), _SemaphoreWindow(recv_sems, s0))
                i0, o0, s0 = i1, o1, s0 + p.n_sems

        super().__init__([a for p in parts for a in p.inputs], [o for p in parts for o in p.out_shapes],
                         sum(p.n_sems for p in parts), functools.partial(each, "start"), functools.partial(each, "finish"))

    def set_results(self, results):
        o0 = 0
        for p in self.parts:
            p.set_results(results[o0:o0 + len(p.out_shapes)])
            o0 += len(p.out_shapes)


def _join(*parts):
    return parts[0] if len(parts) == 1 else _JoinedExchange(list(parts))


def _pc(body, exchange=None, **kw):
    if exchange is None:
        return pl.pallas_call(body, **kw)
    grid = kw["grid"]
    single = not isinstance(kw["out_shape"], (tuple, list))
    out_shape = [kw["out_shape"]] if single else list(kw["out_shape"])
    out_specs = [kw["out_specs"]] if single else list(kw["out_specs"])
    in_specs = list(kw["in_specs"])
    scratch = list(kw.get("scratch_shapes", ()))
    n_in, n_out, n_scr = len(in_specs), len(out_shape), len(scratch)
    n_xi, n_xo = len(exchange.inputs), len(exchange.out_shapes)

    def wrapped(*refs):
        pos = [n_in, n_in + n_xi, n_in + n_xi + n_out, n_in + n_xi + n_out + n_xo]
        ins, x_in, outs, x_out = refs[:pos[0]], refs[pos[0]:pos[1]], refs[pos[1]:pos[2]], refs[pos[2]:pos[3]]
        scr = refs[pos[3]:pos[3] + n_scr]
        send_sems, recv_sems = refs[pos[3] + n_scr:]
        ids = [pl.program_id(a) for a in range(len(grid))]
        first = functools.reduce(jnp.logical_and, [i == 0 for i in ids])
        last = functools.reduce(jnp.logical_and, [i == g - 1 for i, g in zip(ids, grid)])

        @pl.when(first)
        def _():
            exchange.start(x_in, x_out, send_sems, recv_sems)

        body(*ins, *outs, *scr)

        @pl.when(last)
        def _():
            exchange.finish(x_in, x_out, send_sems, recv_sems)

    call = pl.pallas_call(
        wrapped, out_shape=tuple(out_shape) + tuple(exchange.out_shapes), grid=grid,
        in_specs=in_specs + [_ANY] * n_xi, out_specs=tuple(out_specs) + (_ANY,) * n_xo,
        scratch_shapes=scratch + [pltpu.SemaphoreType.DMA((exchange.n_sems,)), pltpu.SemaphoreType.DMA((exchange.n_sems,))],
        input_output_aliases={n_in + a: n_out + b for a, b in exchange.aliases.items()},
        compiler_params=_params(*(["arbitrary"] * len(grid))), name=kw["name"])

    def run(*args):
        res = call(*args, *exchange.inputs)
        exchange.set_results(res[n_out:])
        return res[0] if single else tuple(res[:n_out])

    return run


_ANY = pl.BlockSpec(memory_space=pl.ANY)


def _tile(n, pref, mult):
    best = None
    t = mult
    while t <= min(n, pref):
        if n % t == 0:
            best = t
        t += mult
    return n if best is None else best


def _sds(shape, dtype):
    return jax.ShapeDtypeStruct(shape, dtype)


def _vec_spec(d):
    return pl.BlockSpec((1, d), lambda *_: (0, 0))


def _norm_mod_fwd(x, g, shift, scale, name):
    s, d = x.shape
    tr = _tile(s, 512, 16)

    def body(x_ref, g_ref, sh_ref, sc_ref, h_ref):
        xv = x_ref[...]
        rstd = lax.rsqrt(jnp.mean(xv * xv, axis=-1, keepdims=True) + EPS)
        n = xv * rstd * g_ref[...]
        h_ref[...] = (n * (1.0 + sc_ref[...]) + sh_ref[...]).astype(BF16)

    row = pl.BlockSpec((tr, d), lambda i: (i, 0))
    return _pc(body, out_shape=_sds((s, d), BF16), grid=(s // tr,),
               in_specs=[row, _vec_spec(d), _vec_spec(d), _vec_spec(d)], out_specs=row,
               compiler_params=_params("parallel"), name=name)(x, g, shift, scale)


def _through_gate(dx, f_ref, gate_ref, df_ref, dgate_ref):
    df_ref[...] = (dx * gate_ref[...]).astype(BF16)
    dgate_ref[...] += jnp.sum(dx * f_ref[...].astype(F32), axis=0, keepdims=True)


def _norm_mod_bwd(dh, x, g, scale, dres, name, f=None, gate=None, exchange=None):
    s, d = x.shape
    tr = _tile(s, 256, 16)
    gated = f is not None

    def body(dh_ref, x_ref, g_ref, sc_ref, dres_ref, *rest):
        f_ref, gate_ref = rest[:2] if gated else (None, None)
        dx_ref, dsh_ref, dsc_ref, dg_ref = rest[2:6] if gated else rest[:4]
        df_ref, dgate_ref = rest[6:8] if gated else (None, None)

        @pl.when(pl.program_id(0) == 0)
        def _():
            for ref in (dsh_ref, dsc_ref, dg_ref) + ((dgate_ref,) if gated else ()):
                ref[...] = jnp.zeros_like(ref)

        xv = x_ref[...]
        dhv = dh_ref[...]
        gv = g_ref[...]
        rstd = lax.rsqrt(jnp.mean(xv * xv, axis=-1, keepdims=True) + EPS)
        xhat = xv * rstd
        dn = dhv * (1.0 + sc_ref[...])
        dsh_ref[...] += jnp.sum(dhv, axis=0, keepdims=True)
        dsc_ref[...] += jnp.sum(dhv * (xhat * gv), axis=0, keepdims=True)
        dg_ref[...] += jnp.sum(dn * xhat, axis=0, keepdims=True)
        dxh = dn * gv
        proj = jnp.mean(dxh * xhat, axis=-1, keepdims=True)
        dx = dres_ref[...] + rstd * (dxh - xhat * proj)
        dx_ref[...] = dx
        if gated:
            _through_gate(dx, f_ref, gate_ref, df_ref, dgate_ref)

    row = pl.BlockSpec((tr, d), lambda i: (i, 0))
    vec = _vec_spec(d)
    out_shape = [_sds((s, d), F32), _sds((1, d), F32), _sds((1, d), F32), _sds((1, d), F32)]
    out_specs, in_specs, args = [row, vec, vec, vec], [row, row, vec, vec, row], [dh, x, g, scale, dres]
    if gated:
        out_shape += [_sds((s, d), BF16), _sds((1, d), F32)]
        out_specs += [row, vec]
        in_specs += [row, vec]
        args += [f, gate]
    return _pc(body, exchange, out_shape=tuple(out_shape), grid=(s // tr,), in_specs=in_specs,
               out_specs=tuple(out_specs), compiler_params=_params("arbitrary"), name=name)(*args)


def _final_loss(x, g, target, f, gate, name):
    s, d = x.shape
    tr = _tile(s, 256, 16)
    nsteps = s // tr

    def body(x_ref, g_ref, t_ref, f_ref, gate_ref, dx_ref, loss_ref, dg_ref, df_ref, dgate_ref):
        i = pl.program_id(0)

        @pl.when(i == 0)
        def _():
            loss_ref[...] = jnp.zeros_like(loss_ref)
            dg_ref[...] = jnp.zeros_like(dg_ref)
            dgate_ref[...] = jnp.zeros_like(dgate_ref)

        xv = x_ref[...]
        gv = g_ref[...]
        rstd = lax.rsqrt(jnp.mean(xv * xv, axis=-1, keepdims=True) + EPS)
        xhat = xv * rstd
        err = xhat * gv - t_ref[...]
        dy = err * (1.0 / d)
        loss_ref[...] += jnp.sum(0.5 * err * dy, axis=0, keepdims=True)
        dg_ref[...] += jnp.sum(dy * xhat, axis=0, keepdims=True)
        dxh = dy * gv
        proj = jnp.mean(dxh * xhat, axis=-1, keepdims=True)
        dx = rstd * (dxh - xhat * proj)
        dx_ref[...] = dx
        _through_gate(dx, f_ref, gate_ref, df_ref, dgate_ref)

        @pl.when(i == nsteps - 1)
        def _():
            loss_ref[...] = jnp.broadcast_to(jnp.sum(loss_ref[...], axis=-1, keepdims=True), loss_ref.shape)

    row = pl.BlockSpec((tr, d), lambda i: (i, 0))
    vec = _vec_spec(d)
    return _pc(body, out_shape=(_sds((s, d), F32), _sds((1, d), F32), _sds((1, d), F32), _sds((s, d), BF16), _sds((1, d), F32)),
               grid=(nsteps,), in_specs=[row, vec, row, row, vec], out_specs=(row, vec, vec, row, vec),
               compiler_params=_params("arbitrary"), name=name)(x, g, target, f, gate)


def _mm(lhs, rhs, dims, out_dtype, name, res=None, gate=None, aux_dtype=None, exchange=None):
    if dims == "nn":
        (m, k), (k2, n) = lhs.shape, rhs.shape
    elif dims == "nt":
        (m, k), (n, k2) = lhs.shape, rhs.shape
    else:
        (k, m), (k2, n) = lhs.shape, rhs.shape
    assert k == k2, (lhs.shape, rhs.shape, dims)
    tn = _tile(n, 1024, LANES)
    tm = _tile(m, 512, LANES if dims == "tn" else 16)
    tk = _tile(k, 4096, LANES)
    nk = k // tk
    dn = {"nn": _NN, "nt": _NT, "tn": _TN}[dims]
    lhs_spec = (pl.BlockSpec((tk, tm), lambda i, j, kk: (kk, i)) if dims == "tn"
                else pl.BlockSpec((tm, tk), lambda i, j, kk: (i, kk)))
    rhs_spec = (pl.BlockSpec((tn, tk), lambda i, j, kk: (j, kk)) if dims == "nt"
                else pl.BlockSpec((tk, tn), lambda i, j, kk: (kk, j)))
    out_spec = pl.BlockSpec((tm, tn), lambda i, j, kk: (i, j))
    has_res, has_gate, has_aux = res is not None, gate is not None, aux_dtype is not None

    def body(*refs):
        refs = list(refs)
        l_ref, r_ref = refs[0], refs[1]
        pos = 2
        res_ref = gate_ref = aux_ref = None
        if has_res:
            res_ref = refs[pos]; pos += 1
        if has_gate:
            gate_ref = refs[pos]; pos += 1
        out_ref = refs[pos]; pos += 1
        if has_aux:
            aux_ref = refs[pos]; pos += 1
        acc_ref = refs[pos]
        kk = pl.program_id(2)
        part = lax.dot_general(l_ref[...], r_ref[...], dn, preferred_element_type=F32)

        @pl.when(kk == 0)
        def _():
            acc_ref[...] = part

        @pl.when(kk > 0)
        def _():
            acc_ref[...] += part

        @pl.when(kk == nk - 1)
        def _():
            acc = acc_ref[...]
            if has_aux:
                aux_ref[...] = acc.astype(aux_dtype)
            if has_gate:
                acc = acc * gate_ref[...]
            if has_res:
                acc = res_ref[...] + acc
            out_ref[...] = acc.astype(out_dtype)

    in_specs = [lhs_spec, rhs_spec]
    args = [lhs, rhs]
    if has_res:
        in_specs.append(out_spec); args.append(res)
    if has_gate:
        in_specs.append(pl.BlockSpec((1, tn), lambda i, j, kk: (0, j))); args.append(gate)
    out_shape = [_sds((m, n), out_dtype)]
    out_specs = [out_spec]
    if has_aux:
        out_shape.append(_sds((m, n), aux_dtype)); out_specs.append(out_spec)
    outs = _pc(body, exchange, out_shape=tuple(out_shape), grid=(m // tm, n // tn, nk), in_specs=in_specs,
               out_specs=tuple(out_specs), scratch_shapes=[pltpu.VMEM((tm, tn), F32)],
               compiler_params=_params("parallel", "parallel", "arbitrary"), name=name)(*args)
    return outs if has_aux else outs[0]


def _ffn_up(h, wg_t, wu_t, name, exchange=None):
    s, d = h.shape
    f = wg_t.shape[0]
    tm = _tile(s, 1024, 16)
    tn = _tile(f, 256, LANES)

    def body(h_ref, wg_ref, wu_ref, a_ref, u_ref, hid_ref):
        hv = h_ref[...]
        a = lax.dot_general(hv, wg_ref[...], _NT, preferred_element_type=F32)
        u = lax.dot_general(hv, wu_ref[...], _NT, preferred_element_type=F32)
        a_ref[...] = a.astype(BF16)
        u_ref[...] = u.astype(BF16)
        hid_ref[...] = (a * jax.nn.sigmoid(a) * u).astype(BF16)

    hs = pl.BlockSpec((tm, d), lambda i, j: (i, 0))
    ws = pl.BlockSpec((tn, d), lambda i, j: (j, 0))
    os_ = pl.BlockSpec((tm, tn), lambda i, j: (i, j))
    return _pc(body, exchange, out_shape=(_sds((s, f), BF16),) * 3, grid=(s // tm, f // tn),
               in_specs=[hs, ws, ws], out_specs=(os_, os_, os_),
               compiler_params=_params("parallel", "parallel"), name=name)(h, wg_t, wu_t)


def _ffn_dact(df, wd, a, u, name, exchange=None):
    s, d = df.shape
    f = wd.shape[0]
    tm = _tile(s, 1024, 16)
    tn = _tile(f, 256, LANES)

    def body(df_ref, wd_ref, a_ref, u_ref, da_ref, du_ref):
        dhid = lax.dot_general(df_ref[...], wd_ref[...], _NT, preferred_element_type=F32)
        av = a_ref[...].astype(F32)
        uv = u_ref[...].astype(F32)
        sig = jax.nn.sigmoid(av)
        da_ref[...] = (dhid * uv * (sig * (1.0 + av * (1.0 - sig)))).astype(BF16)
        du_ref[...] = (dhid * (av * sig)).astype(BF16)

    ds_ = pl.BlockSpec((tm, d), lambda i, j: (i, 0))
    ws = pl.BlockSpec((tn, d), lambda i, j: (j, 0))
    os_ = pl.BlockSpec((tm, tn), lambda i, j: (i, j))
    return _pc(body, exchange, out_shape=(_sds((s, f), BF16),) * 2, grid=(s // tm, f // tn),
               in_specs=[ds_, ws, os_, os_], out_specs=(os_, os_),
               compiler_params=_params("parallel", "parallel"), name=name)(df, wd, a, u)


def _split3(v):
    hi = v.astype(BF16)
    r1 = v - hi.astype(F32)
    mid = r1.astype(BF16)
    lo = (r1 - mid.astype(F32)).astype(BF16)
    return hi, mid, lo


def _dot3(v, mat):
    hi, mid, lo = _split3(v)
    out = lax.dot_general(hi, mat, _NN, preferred_element_type=F32)
    out += lax.dot_general(mid, mat, _NN, preferred_element_type=F32)
    out += lax.dot_general(lo, mat, _NN, preferred_element_type=F32)
    return out


def _forget_fwd(flog_t, bias, name):
    h, s = flog_t.shape
    blk = _tile(s, 512, LANES)
    tri = (jnp.arange(blk)[:, None] <= jnp.arange(blk)[None, :]).astype(BF16)

    def body(z_ref, b_ref, tri_ref, f_ref, carry):
        @pl.when(pl.program_id(0) == 0)
        def _():
            carry[...] = jnp.zeros_like(carry)

        z = z_ref[...] + b_ref[...]
        e = jnp.exp(-jnp.abs(z))
        w = 1.0 + e
        log1p_e = jnp.where(w == 1.0, e, jnp.log(w) * (e / (w - 1.0)))
        lf = jnp.minimum(z, 0.0) - log1p_e
        out = carry[...] + _dot3(lf, tri_ref[...])
        for j, piece in enumerate(_split3(out)):
            f_ref[j] = piece
        carry[...] = out[:, blk - 1:blk]

    zs = pl.BlockSpec((h, blk), lambda i: (0, i))
    return _pc(body, out_shape=_sds((3, h, s), BF16), grid=(s // blk,),
               in_specs=[zs, pl.BlockSpec((h, 1), lambda i: (0, 0)), pl.BlockSpec((blk, blk), lambda i: (0, 0))],
               out_specs=pl.BlockSpec((3, h, blk), lambda i: (0, 0, i)), scratch_shapes=[pltpu.VMEM((h, 1), F32)],
               compiler_params=_params("arbitrary"), name=name)(flog_t, bias, tri)


def _forget_bwd(df_t, flog_t, bias, name):
    h, s = flog_t.shape
    blk = _tile(s, 512, LANES)
    nb = s // blk
    tri = (jnp.arange(blk)[:, None] >= jnp.arange(blk)[None, :]).astype(BF16)

    def body(df_ref, z_ref, b_ref, tri_ref, dz_ref, db_ref, carry):
        @pl.when(pl.program_id(0) == 0)
        def _():
            carry[...] = jnp.zeros_like(carry)
            db_ref[...] = jnp.zeros_like(db_ref)

        rc = carry[...] + _dot3(df_ref[...], tri_ref[...])
        carry[...] = rc[:, 0:1]
        dz = rc * jax.nn.sigmoid(-(z_ref[...] + b_ref[...]))
        dz_ref[...] = dz
        db_ref[...] += jnp.sum(dz, axis=-1, keepdims=True)

    rev = pl.BlockSpec((h, blk), lambda i: (0, nb - 1 - i))
    col = pl.BlockSpec((h, 1), lambda i: (0, 0))
    return _pc(body, out_shape=(_sds((h, s), F32), _sds((h, 1), F32)), grid=(nb,),
               in_specs=[rev, rev, col, pl.BlockSpec((blk, blk), lambda i: (0, 0))],
               out_specs=(rev, col), scratch_shapes=[pltpu.VMEM((h, 1), F32)],
               compiler_params=_params("arbitrary"), name=name)(df_t, flog_t, bias, tri)


def _attn_tiles(s):
    return _tile(s, 1024, LANES)


def _attn_half(t):
    return t // 2 if t >= 4 * LANES else t


BIAS_ROWS = 16


def _attn_prep(qkv, f_pieces, name):
    s = qkv.shape[0]
    a_w = qkv.shape[1] // 3
    npair = a_w // LANES
    t = _attn_tiles(s)
    scale = 1.0 / math.sqrt(HEAD_DIM)

    six = f_pieces[:, :2 * npair].reshape(3, npair, 2, s).transpose(1, 3, 2, 0).reshape(npair, s, 6)
    feat = jnp.concatenate([six, jnp.ones((npair, s, 1), BF16), jnp.zeros((npair, s, BIAS_ROWS - 7), BF16)], axis=-1)
    place_q = [[0.0] * (2 * LANES) for _ in range(BIAS_ROWS)]
    place_k = [[0.0] * (2 * LANES) for _ in range(BIAS_ROWS)]
    for hh in range(2):
        b0 = hh * LANES + (HEAD_DIM if hh == 0 else 0)
        for j in range(3):
            place_q[3 * hh + j][b0 + j] = 1.0
            place_q[6][b0 + 3 + j] = 1.0
            place_k[6][b0 + j] = 1.0
            place_k[3 * hh + j][b0 + 3 + j] = -1.0
    place_q = jnp.array(place_q, BF16)
    place_k = jnp.array(place_k, BF16)

    def body(q_ref, k_ref, v_ref, f_ref, pq_ref, pk_ref, qa_ref, ka_ref, va_ref):
        lane = lax.broadcasted_iota(jnp.int32, (1, LANES), 1)
        q2 = (q_ref[...].astype(F32) * scale).astype(BF16)
        k2, v2 = k_ref[...], v_ref[...]
        qx = lax.dot_general(f_ref[0], pq_ref[...], _NN, preferred_element_type=F32).astype(BF16)
        kx = lax.dot_general(f_ref[0], pk_ref[...], _NN, preferred_element_type=F32).astype(BF16)
        for hh in range(2):
            real = (lane < HEAD_DIM) if hh == 0 else (lane >= HEAD_DIM)
            cols = slice(hh * LANES, (hh + 1) * LANES)
            qa_ref[:, cols] = jnp.where(real, q2, qx[:, cols])
            ka_ref[:, cols] = jnp.where(real, k2, kx[:, cols])
            va_ref[:, cols] = jnp.where(real, v2, jnp.zeros_like(v2))

    def col(off):
        return pl.BlockSpec((t, LANES), lambda p, i: (i, off + p))

    out = pl.BlockSpec((t, 2 * LANES), lambda p, i: (i, p))
    place = pl.BlockSpec((BIAS_ROWS, 2 * LANES), lambda p, i: (0, 0))
    return _pc(body, out_shape=(_sds((s, 2 * a_w), BF16),) * 3, grid=(npair, s // t),
               in_specs=[col(0), col(npair), col(2 * npair), pl.BlockSpec((1, t, BIAS_ROWS), lambda p, i: (p, i, 0)),
                         place, place],
               out_specs=(out, out, out), compiler_params=_params("parallel", "parallel"), name=name)(
                   qkv, qkv, qkv, feat, place_q, place_k)


def _attn_fwd(qa, ka, va, name, exchange=None):
    s = qa.shape[0]
    a_w = qa.shape[1] // 2
    npair = a_w // LANES
    t = _attn_tiles(s)
    nq = s // t
    half = _attn_half(t)

    def body(q_ref, k_ref, v_ref, o_ref, lse_ref, m_sc, l_sc, acc_sc):
        qi = pl.program_id(1)
        first = lax.broadcasted_iota(jnp.int32, (1, LANES), 1) < HEAD_DIM
        m_sc[...] = jnp.full_like(m_sc, NEG_BIG)
        l_sc[...] = jnp.zeros_like(l_sc)
        acc_sc[...] = jnp.zeros_like(acc_sc)

        def step(q0, k_start, size, diag):
            q_sl = slice(q0, q0 + size)
            k_rows = pl.ds(pl.multiple_of(k_start, size), size)
            m_old = m_sc[q_sl, :]
            keep = None
            if diag:
                keep = (lax.broadcasted_iota(jnp.int32, (size, size), 0) >= lax.broadcasted_iota(jnp.int32, (size, size), 1))
            m_new, rs, pv = [], [], []
            for hh in range(2):
                cols = slice(hh * LANES, (hh + 1) * LANES)
                sc = lax.dot_general(q_ref[q_sl, cols], k_ref[k_rows, cols], _NT, preferred_element_type=F32)
                if diag:
                    sc = jnp.where(keep, sc, NEG_BIG)
                mo = m_old[:, hh * HEAD_DIM:hh * HEAD_DIM + 1]
                mn = jnp.maximum(mo, jnp.max(sc, axis=1, keepdims=True))
                p = jnp.exp(sc - mn)
                m_new.append(mn)
                rs.append(jnp.sum(p, axis=1, keepdims=True))
                pv.append(lax.dot_general(p.astype(BF16), v_ref[k_rows, cols], _NN, preferred_element_type=F32))
            m2 = jnp.where(first, m_new[0], m_new[1])
            alpha = jnp.exp(m_old - m2)
            m_sc[q_sl, :] = m2
            l_sc[q_sl, :] = alpha * l_sc[q_sl, :] + jnp.where(first, rs[0], rs[1])
            acc_sc[q_sl, :] = alpha * acc_sc[q_sl, :] + pv[0] + pv[1]

        def below_diagonal(ki, carry):
            step(0, ki * t, t, False)
            return carry

        lax.fori_loop(0, qi, below_diagonal, 0)
        step(0, qi * t, half, True)
        if half < t:
            step(half, qi * t, half, False)
            step(half, qi * t + half, half, True)
        l2 = l_sc[...]
        o_ref[...] = acc_sc[...] / l2
        lse_ref[...] = m_sc[...] + jnp.log(l2)

    qs = pl.BlockSpec((t, 2 * LANES), lambda p, qi: (qi, p))
    ks = pl.BlockSpec((s, 2 * LANES), lambda p, qi: (0, p))
    os_ = pl.BlockSpec((t, LANES), lambda p, qi: (qi, p))
    return _pc(body, exchange, out_shape=(_sds((s, a_w), F32), _sds((s, a_w), F32)), grid=(npair, nq),
               in_specs=[qs, ks, ks], out_specs=(os_, os_),
               scratch_shapes=[pltpu.VMEM((t, LANES), F32)] * 3,
               compiler_params=_params("parallel", "arbitrary"), name=name)(qa, ka, va)


def _attn_bwd(qa, ka, va, do, o, lse, name, exchange=None):
    s = qa.shape[0]
    a_w = qa.shape[1] // 2
    npair = a_w // LANES
    t = _attn_tiles(s)
    nq = s // t
    half = _attn_half(t)
    scale = 1.0 / math.sqrt(HEAD_DIM)

    def body(q_ref, k_ref, v_ref, do_ref, o_ref, lse_ref, dq_ref, dk_ref, dv_ref, qx_ref, kx_ref, dk_sc, dv_sc, kx_sc):
        ki = pl.program_id(1)
        first = lax.broadcasted_iota(jnp.int32, (1, LANES), 1) < HEAD_DIM

        @pl.when(ki == 0)
        def _():
            dq_ref[...] = jnp.zeros_like(dq_ref)
            qx_ref[...] = jnp.zeros_like(qx_ref)

        def step(q_start, k0, size, diag, assign):
            rows = pl.ds(pl.multiple_of(q_start, size), size)
            k_sl = slice(k0, k0 + size)
            do2 = do_ref[rows, :]
            lse2 = lse_ref[rows, :]
            dd = do2.astype(F32) * o_ref[rows, :]
            keep = None
            if diag:
                keep = (lax.broadcasted_iota(jnp.int32, (size, size), 0) >= lax.broadcasted_iota(jnp.int32, (size, size), 1))
            dq_h, dk_h, dv_h = [], [], []
            for hh in range(2):
                sel = first if hh == 0 else jnp.logical_not(first)
                cols = slice(hh * LANES, (hh + 1) * LANES)
                qh, kh, vh = q_ref[rows, cols], k_ref[k_sl, cols], v_ref[k_sl, cols]
                delta = jnp.sum(jnp.where(sel, dd, 0.0), axis=1, keepdims=True)
                sc = lax.dot_general(qh, kh, _NT, preferred_element_type=F32)
                if diag:
                    sc = jnp.where(keep, sc, NEG_BIG)
                p = jnp.exp(sc - lse2[:, hh * HEAD_DIM:hh * HEAD_DIM + 1])
                dp = lax.dot_general(do2, vh, _NT, preferred_element_type=F32)
                ds_b = (p * (dp - delta)).astype(BF16)
                dv_h.append(lax.dot_general(p.astype(BF16), do2, _TN, preferred_element_type=F32))
                dk_h.append(lax.dot_general(ds_b, qh, _TN, preferred_element_type=F32))
                dq_h.append(lax.dot_general(ds_b, kh, _NN, preferred_element_type=F32))
            dq_ref[rows, :] += jnp.where(first, dq_h[0], dq_h[1]) * scale
            qx_ref[rows, :] += jnp.where(first, dq_h[1], dq_h[0])
            dk_new = jnp.where(first, dk_h[0], dk_h[1])
            kx_new = jnp.where(first, dk_h[1], dk_h[0])
            dv_new = jnp.where(first, dv_h[0], dv_h[1])
            if assign:
                dk_sc[k_sl, :] = dk_new
                kx_sc[k_sl, :] = kx_new
                dv_sc[k_sl, :] = dv_new
            else:
                dk_sc[k_sl, :] += dk_new
                kx_sc[k_sl, :] += kx_new
                dv_sc[k_sl, :] += dv_new

        def below_diagonal(qi, carry):
            step(qi * t, 0, t, False, False)
            return carry

        step(ki * t, 0, half, True, True)
        if half < t:
            step(ki * t + half, 0, half, False, False)
            step(ki * t + half, half, half, True, True)
        lax.fori_loop(ki + 1, nq, below_diagonal, 0)
        dk_ref[...] = dk_sc[...].astype(BF16)
        dv_ref[...] = dv_sc[...].astype(BF16)
        kx_ref[...] = kx_sc[...]

    ks2 = pl.BlockSpec((t, 2 * LANES), lambda p, ki: (ki, p))
    qs2 = pl.BlockSpec((s, 2 * LANES), lambda p, ki: (0, p))
    whole = pl.BlockSpec((s, LANES), lambda p, ki: (0, p))
    kout = pl.BlockSpec((t, LANES), lambda p, ki: (ki, p))
    return _pc(body, exchange,
               out_shape=(_sds((s, a_w), F32), _sds((s, a_w), BF16), _sds((s, a_w), BF16), _sds((s, a_w), F32),
                          _sds((s, a_w), F32)),
               grid=(npair, nq), in_specs=[qs2, ks2, ks2, whole, whole, whole],
               out_specs=(whole, kout, kout, whole, kout),
               scratch_shapes=[pltpu.VMEM((t, LANES), F32)] * 3,
               compiler_params=_params("parallel", "arbitrary"), name=name)(qa, ka, va, do, o, lse)

def _decay_grads(qx, kx, name):
    s, a_w = qx.shape
    n_heads = a_w // HEAD_DIM
    tr = _tile(s, 512, 8)
    pick_q = [[0.0] * LANES for _ in range(a_w)]
    pick_k = [[0.0] * LANES for _ in range(a_w)]
    for h in range(n_heads):
        b0 = (h // 2) * LANES + (HEAD_DIM if h % 2 == 0 else 0)
        pick_q[b0][h] = 1.0
        pick_k[b0 + 3][h] = 1.0
    pick_q = jnp.array(pick_q, BF16)
    pick_k = jnp.array(pick_k, BF16)

    def body(qx_ref, kx_ref, pq_ref, pk_ref, o_ref):
        o_ref[...] = _dot3(qx_ref[...], pq_ref[...]) - _dot3(kx_ref[...], pk_ref[...])

    row = pl.BlockSpec((tr, a_w), lambda i: (i, 0))
    pick = pl.BlockSpec((a_w, LANES), lambda i: (0, 0))
    return _pc(body, out_shape=_sds((s, LANES), F32), grid=(s // tr,), in_specs=[row, row, pick, pick],
               out_specs=pl.BlockSpec((tr, LANES), lambda i: (i, 0)),
               compiler_params=_params("parallel"), name=name)(qx, kx, pick_q, pick_k)


def _shift_down(z, k, rows):
    return jnp.where(rows >= k, pltpu.roll(z, k, 0), 0.0)


def _shift_up(z, k, rows, n):
    return jnp.where(rows < n - k, pltpu.roll(z, n - k, 0), 0.0)


def _conv_fwd(bcx, conv_w, name):
    s = bcx.shape[0]
    cw = bcx.shape[1] // 3
    nb = cw // LANES

    def body(b_ref, c_ref, x_ref, w_ref, cv_ref):
        rows = lax.broadcasted_iota(jnp.int32, (s, LANES), 0)
        z = c_ref[...] * x_ref[...]
        w = w_ref[...]
        y = w[2:3, :] * z + w[1:2, :] * _shift_down(z, 1, rows) + w[0:1, :] * _shift_down(z, 2, rows)
        cv_ref[...] = b_ref[...] * y

    def col(off):
        return pl.BlockSpec((s, LANES), lambda j: (0, j + off))

    return _pc(body, out_shape=_sds((s, cw), F32), grid=(nb,),
               in_specs=[col(0), col(nb), col(2 * nb), pl.BlockSpec((CONV_K, LANES), lambda j: (0, j))],
               out_specs=col(0), compiler_params=_params("parallel"), name=name)(bcx, bcx, bcx, conv_w)


def _conv_bwd(dcv, bcx, conv_w, name):
    s = bcx.shape[0]
    cw = bcx.shape[1] // 3
    nb = cw // LANES

    def body(dcv_ref, b_ref, c_ref, x_ref, w_ref, db_ref, dc_ref, dxc_ref, dw_ref):
        rows = lax.broadcasted_iota(jnp.int32, (s, LANES), 0)
        cv_, xv = c_ref[...], x_ref[...]
        z = cv_ * xv
        w = w_ref[...]
        z1 = _shift_down(z, 1, rows)
        z2 = _shift_down(z, 2, rows)
        y = w[2:3, :] * z + w[1:2, :] * z1 + w[0:1, :] * z2
        dcvv = dcv_ref[...]
        db_ref[...] = (dcvv * y).astype(BF16)
        dy = dcvv * b_ref[...]
        dw_ref[0:1, :] = jnp.sum(dy * z2, axis=0, keepdims=True)
        dw_ref[1:2, :] = jnp.sum(dy * z1, axis=0, keepdims=True)
        dw_ref[2:3, :] = jnp.sum(dy * z, axis=0, keepdims=True)
        dz = w[2:3, :] * dy + w[1:2, :] * _shift_up(dy, 1, rows, s) + w[0:1, :] * _shift_up(dy, 2, rows, s)
        dc_ref[...] = (dz * xv).astype(BF16)
        dxc_ref[...] = (dz * cv_).astype(BF16)

    def col(off):
        return pl.BlockSpec((s, LANES), lambda j: (0, j + off))

    wspec = pl.BlockSpec((CONV_K, LANES), lambda j: (0, j))
    db, dc, dxc, dw = _pc(body, out_shape=(_sds((s, cw), BF16),) * 3 + (_sds((CONV_K, cw), F32),), grid=(nb,),
                          in_specs=[col(0), col(0), col(nb), col(2 * nb), wspec],
                          out_specs=(col(0), col(0), col(0), wspec),
                          compiler_params=_params("parallel"), name=name)(dcv, bcx, bcx, bcx, conv_w)
    return db, dc, dxc, dw


def _group_matrix():
    idx = jnp.arange(LANES) // HEAD_DIM
    return (idx[:, None] == idx[None, :]).astype(BF16)


def _group_sum(v, gmat):
    return _dot3(v, gmat)


def _gnorm_fwd(att, cv, gg, name):
    s, a_w = att.shape
    cw = cv.shape[1]
    d = a_w + cw
    tr = _tile(s, 512, 16)
    gmat = _group_matrix()

    def body(att_ref, cv_ref, gg_ref, gm_ref, yn_ref):
        gm = gm_ref[...]
        for c0 in range(0, d, LANES):
            y = att_ref[:, c0:c0 + LANES] if c0 < a_w else cv_ref[:, c0 - a_w:c0 - a_w + LANES]
            ms = _group_sum(y * y, gm) * (1.0 / HEAD_DIM)
            yn_ref[:, c0:c0 + LANES] = (y * lax.rsqrt(ms + EPS) * gg_ref[:, c0:c0 + LANES]).astype(BF16)

    return _pc(body, out_shape=_sds((s, d), BF16), grid=(s // tr,),
               in_specs=[pl.BlockSpec((tr, a_w), lambda i: (i, 0)), pl.BlockSpec((tr, cw), lambda i: (i, 0)),
                         _vec_spec(d), pl.BlockSpec((LANES, LANES), lambda i: (0, 0))],
               out_specs=pl.BlockSpec((tr, d), lambda i: (i, 0)),
               compiler_params=_params("parallel"), name=name)(att, cv, gg, gmat)


def _gnorm_bwd(dyn, att, cv, gg, name):
    s, a_w = att.shape
    cw = cv.shape[1]
    d = a_w + cw
    tr = _tile(s, 256, 16)
    gmat = _group_matrix()

    def body(dyn_ref, att_ref, cv_ref, gg_ref, gm_ref, datt_ref, dcv_ref, dgg_ref):
        @pl.when(pl.program_id(0) == 0)
        def _():
            dgg_ref[...] = jnp.zeros_like(dgg_ref)

        gm = gm_ref[...]
        for c0 in range(0, d, LANES):
            y = att_ref[:, c0:c0 + LANES] if c0 < a_w else cv_ref[:, c0 - a_w:c0 - a_w + LANES]
            dv = dyn_ref[:, c0:c0 + LANES]
            r = lax.rsqrt(_group_sum(y * y, gm) * (1.0 / HEAD_DIM) + EPS)
            xhat = y * r
            dgg_ref[:, c0:c0 + LANES] += jnp.sum(dv * xhat, axis=0, keepdims=True)
            dxh = dv * gg_ref[:, c0:c0 + LANES]
            proj = _group_sum(dxh * xhat, gm) * (1.0 / HEAD_DIM)
            dy = r * (dxh - xhat * proj)
            if c0 < a_w:
                datt_ref[:, c0:c0 + LANES] = dy.astype(BF16)
            else:
                dcv_ref[:, c0 - a_w:c0 - a_w + LANES] = dy

    return _pc(body, out_shape=(_sds((s, a_w), BF16), _sds((s, cw), F32), _sds((1, d), F32)), grid=(s // tr,),
               in_specs=[pl.BlockSpec((tr, d), lambda i: (i, 0)), pl.BlockSpec((tr, a_w), lambda i: (i, 0)),
                         pl.BlockSpec((tr, cw), lambda i: (i, 0)), _vec_spec(d),
                         pl.BlockSpec((LANES, LANES), lambda i: (0, 0))],
               out_specs=(pl.BlockSpec((tr, a_w), lambda i: (i, 0)), pl.BlockSpec((tr, cw), lambda i: (i, 0)),
                          _vec_spec(d)),
               compiler_params=_params("arbitrary"), name=name)(dyn, att, cv, gg, gmat)


def _adamw_math(w, g, m, v):
    m_new = ADAM_B1 * m + (1.0 - ADAM_B1) * g
    v_new = ADAM_B2 * v + (1.0 - ADAM_B2) * (g * g)
    m_hat = m_new / (1.0 - ADAM_B1 ** ADAM_STEP)
    v_hat = v_new / (1.0 - ADAM_B2 ** ADAM_STEP)
    delta = -ADAM_LR * (m_hat / (jnp.sqrt(v_hat) + ADAM_EPS) + ADAM_WD * w)
    return delta, m_new, v_new


def _row_tile(r, c):
    return _tile(r, max(8, ((1 << 18) // c) // 8 * 8), 8)


def _adamw(w, g, m, v, name):
    r, c = w.shape
    tr = _row_tile(r, c)

    def body(w_ref, g_ref, m_ref, v_ref, d_ref, mo_ref, vo_ref):
        d, mn, vn = _adamw_math(w_ref[...], g_ref[...], m_ref[...], v_ref[...])
        d_ref[...] = d
        mo_ref[...] = mn
        vo_ref[...] = vn

    spec = pl.BlockSpec((tr, c), lambda i: (i, 0))
    return _pc(body, out_shape=(_sds((r, c), F32),) * 3, grid=(r // tr,), in_specs=[spec] * 4,
               out_specs=(spec,) * 3, compiler_params=_params("parallel"), name=name)(w, g, m, v)


def _adamw_halves(w, mine, theirs, m, v, core, name):
    r2, c = w.shape
    r = r2 // 2
    assert mine.shape == (r, c) and theirs.shape == (r, c)
    tr = _row_tile(r, c)
    nb = r // tr

    def body(core_ref, w_ref, a_ref, b_ref, m_ref, v_ref, g_ref, d_ref, mo_ref, vo_ref):
        g = jnp.where(pl.program_id(0) == core_ref[0], a_ref[...], b_ref[...])
        d, mn, vn = _adamw_math(w_ref[...], g, m_ref[...], v_ref[...])
        g_ref[...] = g
        d_ref[...] = d
        mo_ref[...] = mn
        vo_ref[...] = vn

    full = pl.BlockSpec((tr, c), lambda h, i, core_ref: (h * nb + i, 0))
    half = pl.BlockSpec((tr, c), lambda h, i, core_ref: (i, 0))
    grid_spec = pltpu.PrefetchScalarGridSpec(
        num_scalar_prefetch=1, grid=(2, nb), in_specs=[full, half, half, full, full], out_specs=(full,) * 4)
    return _pc(body, out_shape=(_sds((r2, c), F32),) * 4, grid_spec=grid_spec,
               compiler_params=_params("parallel", "parallel"), name=name)(core, w, mine, theirs, m, v)


def _ada_fwd(c16, ada_w, ada_b, name):
    d, n = ada_w.shape
    tn = _tile(n, 768, LANES)

    def body(c_ref, w_ref, b_ref, o_ref):
        cv = c_ref[...]
        sc = (cv * jax.nn.sigmoid(cv)).astype(BF16)
        o_ref[...] = lax.dot_general(sc, w_ref[...].astype(BF16), _NN, preferred_element_type=F32) + b_ref[...]

    return _pc(body, out_shape=_sds((16, n), F32), grid=(n // tn,),
               in_specs=[pl.BlockSpec((16, d), lambda j: (0, 0)), pl.BlockSpec((d, tn), lambda j: (0, j)),
                         pl.BlockSpec((1, tn), lambda j: (0, j))],
               out_specs=pl.BlockSpec((16, tn), lambda j: (0, j)),
               compiler_params=_params("parallel"), name=name)(c16, ada_w, ada_b)


def _ada_update(c16_t, dmod16, w, m, v, name):
    r, c = w.shape
    tr = _row_tile(r, c)

    def body(c_ref, dm_ref, w_ref, m_ref, v_ref, g_ref, d_ref, mo_ref, vo_ref):
        cv = c_ref[...]
        sc = (cv * jax.nn.sigmoid(cv)).astype(BF16)
        g = lax.dot_general(sc, dm_ref[...].astype(BF16), _NN, preferred_element_type=F32)
        d, mn, vn = _adamw_math(w_ref[...], g, m_ref[...], v_ref[...])
        g_ref[...] = g
        d_ref[...] = d
        mo_ref[...] = mn
        vo_ref[...] = vn

    spec = pl.BlockSpec((tr, c), lambda i: (i, 0))
    return _pc(body, out_shape=(_sds((r, c), F32),) * 4, grid=(r // tr,),
               in_specs=[pl.BlockSpec((tr, 16), lambda i: (i, 0)), pl.BlockSpec((16, c), lambda i: (0, 0)),
                         spec, spec, spec],
               out_specs=(spec,) * 4, compiler_params=_params("parallel"), name=name)(c16_t, dmod16, w, m, v)


def _add_half(dw, recv, core, name):
    _, _, r, w = dw.shape
    tr = _tile(r, 512, 16)

    def body(core_ref, a_ref, b_ref, o_ref):
        o_ref[...] = (a_ref[...].astype(F32) + b_ref[...].astype(F32)).astype(BF16)

    grid_spec = pltpu.PrefetchScalarGridSpec(
        num_scalar_prefetch=1, grid=(N_CHIPS, r // tr),
        in_specs=[pl.BlockSpec((None, None, tr, w), lambda s, i, core_ref: (s, core_ref[0], i, 0)),
                  pl.BlockSpec((None, tr, w), lambda s, i, core_ref: (s, i, 0))],
        out_specs=pl.BlockSpec((None, tr, w), lambda s, i, core_ref: (s, i, 0)))
    return _pc(body, out_shape=_sds((N_CHIPS, r, w), BF16), grid_spec=grid_spec,
               compiler_params=_params("parallel", "parallel"), name=name)(core, dw, recv)


def _sum_chips(own, recv, chip, name):
    _, r, w = own.shape
    tr = _tile(r, 512, 16)

    def body(chip_ref, own_ref, p_ref, o_ref):
        acc = own_ref[...].astype(F32)
        for q in range(N_CHIPS - 1):
            acc = acc + p_ref[q].astype(F32)
        o_ref[...] = acc

    grid_spec = pltpu.PrefetchScalarGridSpec(
        num_scalar_prefetch=1, grid=(r // tr,),
        in_specs=[pl.BlockSpec((None, tr, w), lambda i, chip_ref: (chip_ref[0], i, 0)),
                  pl.BlockSpec((N_CHIPS - 1, tr, w), lambda i, chip_ref: (0, i, 0))],
        out_specs=pl.BlockSpec((tr, w), lambda i, chip_ref: (i, 0)))
    return _pc(body, out_shape=_sds((r, w), F32), grid_spec=grid_spec,
               compiler_params=_params("parallel"), name=name)(chip, own, recv)


def _sum_devices(parts, name):
    nd, r, w = parts.shape

    def body(p_ref, o_ref):
        acc = p_ref[0]
        for q in range(1, nd):
            acc = acc + p_ref[q]
        o_ref[...] = acc

    return _pc(body, out_shape=_sds((r, w), F32), name=name)(parts)


def _place():
    x, y, c = lax.axis_index("x"), lax.axis_index("y"), lax.axis_index("c")
    chips = [(1 - x, y), (x, 1 - y), (1 - x, 1 - y)]
    return x, y, c, chips


def _all_gather_small(blk, name):
    r, w = blk.shape

    def body(x_ref, out_ref, send_sems, recv_sems, local_sem):
        x, y, c, chips = _place()
        me, sibling = (x, y, c), (x, y, 1 - c)

        def rows(px, py, pc):
            return out_ref.at[pl.ds((4 * px + 2 * py + pc) * r, r), :]

        def copy(k, block, to, src=None):
            return pltpu.make_async_remote_copy(
                src_ref=rows(*block) if src is None else src, dst_ref=rows(*block),
                send_sem=send_sems.at[k], recv_sem=recv_sems.at[k], device_id=to, device_id_type=MESH)

        mine = pltpu.make_async_copy(x_ref, rows(*me), local_sem)
        mine.start()
        first = [copy(0, me, sibling, src=x_ref)]
        first += [copy(1 + j, me, (*chip, c), src=x_ref) for j, chip in enumerate(chips)]
        for cp in first:
            cp.start()
        passed = [copy(4 + j, (*chip, c), sibling) for j, chip in enumerate(chips)]
        for j, chip in enumerate(chips):
            copy(1 + j, (*chip, c), me).wait_recv()
            passed[j].start()
        copy(0, sibling, me).wait_recv()
        for j, chip in enumerate(chips):
            copy(4 + j, (*chip, 1 - c), me).wait_recv()
        for cp in first + passed:
            cp.wait_send()
        mine.wait()

    return _pc(body, out_shape=_sds((N_DEV * r, w), blk.dtype),
               in_specs=[pl.BlockSpec(memory_space=pltpu.VMEM)], out_specs=pl.BlockSpec(memory_space=pltpu.VMEM),
               scratch_shapes=[pltpu.SemaphoreType.DMA((7,)), pltpu.SemaphoreType.DMA((7,)), pltpu.SemaphoreType.DMA],
               name=name)(blk)


def _remote(src, dst, send_sems, recv_sems, k, to):
    return pltpu.make_async_remote_copy(src_ref=src, dst_ref=dst, send_sem=send_sems.at[k], recv_sem=recv_sems.at[k],
                                        device_id=to, device_id_type=MESH)


def _exchange_of(inputs, out_shapes, n_sems, copies, aliases=None):
    def start(src, dst, send_sems, recv_sems):
        for cp in copies(src, dst, send_sems, recv_sems)[0]:
            cp.start()

    def finish(src, dst, send_sems, recv_sems):
        sends, arrivals = copies(src, dst, send_sems, recv_sems)
        for cp in arrivals:
            cp.wait_recv()
        for cp in sends:
            cp.wait_send()

    return _Exchange(inputs, out_shapes, n_sems, start, finish, aliases)


def _run_exchange(ex, name):
    n_in, n_out = len(ex.inputs), len(ex.out_shapes)

    def body(*refs):
        src, dst = refs[:n_in], refs[n_in:n_in + n_out]
        send_sems, recv_sems = refs[n_in + n_out:]
        ex.start(src, dst, send_sems, recv_sems)
        ex.finish(src, dst, send_sems, recv_sems)

    ex.set_results(pl.pallas_call(
        body, out_shape=tuple(ex.out_shapes), in_specs=[_ANY] * n_in, out_specs=(_ANY,) * n_out,
        scratch_shapes=[pltpu.SemaphoreType.DMA((ex.n_sems,)), pltpu.SemaphoreType.DMA((ex.n_sems,))],
        input_output_aliases=ex.aliases, name=name)(*ex.inputs))


def _gather_ici_exchange(shards):
    n = len(shards)

    def copies(own, out, send_sems, recv_sems):
        x, y, c, chips = _place()
        my_chip = 2 * x + y
        sends, arrivals = [], []
        for i in range(n):
            for j, chip in enumerate(chips):
                to = (*chip, c)
                sends.append(_remote(own[i].at[c], out[i].at[my_chip, c], send_sems, recv_sems, 4 * i + j, to))
                arrivals.append(_remote(own[i].at[c], out[i].at[2 * chip[0] + chip[1], c], send_sems, recv_sems, 4 * i + j, to))
            whole = _remote(own[i], out[i].at[my_chip], send_sems, recv_sems, 4 * i + 3, (x, y, 1 - c))
            sends.append(whole)
            arrivals.append(whole)
        return sends, arrivals

    return _exchange_of(shards, [_sds((N_CHIPS,) + s.shape, s.dtype) for s in shards], 4 * n, copies)


def _gather_pass_exchange(gathered):
    n = len(gathered)

    def copies(src, dst, send_sems, recv_sems):
        x, y, c, chips = _place()
        sends, arrivals = [], []
        for i in range(n):
            for j, chip in enumerate(chips):
                idx = 2 * chip[0] + chip[1]
                sends.append(_remote(src[i].at[idx, c], dst[i].at[idx, c], send_sems, recv_sems, 3 * i + j, (x, y, 1 - c)))
                arrivals.append(_remote(src[i].at[idx, c], dst[i].at[idx, 1 - c], send_sems, recv_sems, 3 * i + j, (x, y, 1 - c)))
        return sends, arrivals

    return _exchange_of(gathered, [_sds(g.shape, g.dtype) for g in gathered], 3 * n, copies,
                        aliases={i: i for i in range(n)})


def _reduce_sibling_exchange(grads):
    n = len(grads)

    def copies(src, dst, send_sems, recv_sems):
        x, y, c, _ = _place()
        both = [_remote(src[i].at[s, 1 - c], dst[i].at[s], send_sems, recv_sems, N_CHIPS * i + s, (x, y, 1 - c))
                for i in range(n) for s in range(N_CHIPS)]
        return both, both

    return _exchange_of(grads, [_sds((N_CHIPS,) + g.shape[2:], g.dtype) for g in grads], N_CHIPS * n, copies)


def _reduce_chips_exchange(parts):
    n = len(parts)

    def copies(src, dst, send_sems, recv_sems):
        x, y, c, chips = _place()
        both = [_remote(src[i].at[2 * chip[0] + chip[1]], dst[i].at[j], send_sems, recv_sems, 3 * i + j, (*chip, c))
                for i in range(n) for j, chip in enumerate(chips)]
        return both, both

    return _exchange_of(parts, [_sds((N_CHIPS - 1,) + p.shape[1:], p.dtype) for p in parts], 3 * n, copies)


def _share_exchange(halves):
    n = len(halves)

    def copies(src, dst, send_sems, recv_sems):
        x, y, c, _ = _place()
        both = [_remote(src[i], dst[i], send_sems, recv_sems, i, (x, y, 1 - c)) for i in range(n)]
        return both, both

    return _exchange_of(halves, [_sds(h.shape, h.dtype) for h in halves], n, copies)


HEAD_ROWS = 16


class _WeightTraffic:
    def __init__(self, shards, core, chip):
        self.shards, self.core, self.chip = shards, core, chip
        self.gather, self.grads, self.reduce, self.chip_sums, self.half_sums, self.shared = {}, {}, {}, {}, {}, {}

    def gather_ici(self, grp):
        self.gather[grp] = _gather_ici_exchange(self.shards[grp])
        return self.gather[grp]

    def gather_pass(self, grp):
        self.gather[grp] = _gather_pass_exchange(self.gather[grp].results)
        return self.gather[grp]

    def weights(self, grp):
        return [g.reshape(-1, g.shape[-1]) for g in self.gather[grp].results]

    def reduce_sibling(self, grp, grads):
        self.grads[grp] = [g.reshape(N_CHIPS, 2, g.shape[0] // (2 * N_CHIPS), g.shape[1]) for g in grads]
        self.reduce[grp] = _reduce_sibling_exchange(self.grads[grp])
        return self.reduce[grp]

    def add_halves(self, grp):
        self.chip_sums[grp] = [_add_half(g, r, self.core, "add_half_%s%d" % (grp, i))
                               for i, (g, r) in enumerate(zip(self.grads[grp], self.reduce[grp].results))]

    def reduce_chips(self, grp):
        self.reduce[grp] = _reduce_chips_exchange(self.chip_sums[grp])
        return self.reduce[grp]

    def sum_chips(self, grp):
        self.half_sums[grp] = [_sum_chips(o, p, self.chip, "sum_chips_%s%d" % (grp, i))
                               for i, (o, p) in enumerate(zip(self.chip_sums[grp], self.reduce[grp].results))]

    def share(self, grp):
        self.shared[grp] = _share_exchange(self.half_sums[grp])
        return self.shared[grp]

    def totals(self, grp):
        return list(zip(self.half_sums[grp], self.shared[grp].results))


def _ffn_fwd(x, norm_g, shift, scale, gate, wg_t, wu_t, wd, tag, up_exchange=None, down_exchange=None):
    h = _norm_mod_fwd(x, norm_g, shift, scale, tag + "_norm_fwd")
    a, u, hid = _ffn_up(h, wg_t, wu_t, tag + "_up", exchange=up_exchange)
    wd = wd() if callable(wd) else wd
    x_out, f = _mm(hid, wd, "nn", F32, tag + "_down", res=x, gate=gate, aux_dtype=BF16,
                   exchange=down_exchange() if down_exchange else None)
    return x_out, (h, a, u, hid, f)


def _ffn_bwd(dx_out, df, x, saved, norm_g, scale, wg_t, wu_t, wd, tag, traffic, below=None, dact_exchange=None,
             dw_exchange=None, finish_reduction=False):
    h, a, u, hid, _ = saved
    f_below, gate_below = below if below else (None, None)
    da, du = _ffn_dact(df, wd, a, u, tag + "_dact", exchange=dact_exchange)
    dwd = _mm(hid, df, "tn", BF16, tag + "_dwd", exchange=dw_exchange() if dw_exchange else None)
    if not finish_reduction:
        dwg_t = _mm(da, h, "tn", BF16, tag + "_dwg")
        dwu_t = _mm(du, h, "tn", BF16, tag + "_dwu")
        dh = _mm(da, wg_t, "nn", F32, tag + "_dh_a", exchange=traffic.reduce_sibling(tag, [dwg_t, dwu_t, dwd]))
        traffic.add_halves(tag)
        dh = _mm(du, wu_t, "nn", F32, tag + "_dh_u", res=dh)
        dx, dshift, dscale, dnorm_g, *gated = _norm_mod_bwd(dh, x, norm_g, scale, dx_out, tag + "_norm_bwd",
                                                            f=f_below, gate=gate_below)
        return dx, (dshift, dscale, dnorm_g), gated
    kd, kg, ku = tag + "_wd", tag + "_wg", tag + "_wu"
    dwg_t = _mm(da, h, "tn", BF16, tag + "_dwg", exchange=traffic.reduce_sibling(kd, [dwd]))
    traffic.add_halves(kd)
    dwu_t = _mm(du, h, "tn", BF16, tag + "_dwu",
                exchange=_join(traffic.reduce_chips(kd), traffic.reduce_sibling(kg, [dwg_t])))
    traffic.add_halves(kg)
    dh = _mm(da, wg_t, "nn", F32, tag + "_dh_a",
             exchange=_join(traffic.reduce_chips(kg), traffic.reduce_sibling(ku, [dwu_t])))
    traffic.add_halves(ku)
    traffic.sum_chips(kd)
    dh = _mm(du, wu_t, "nn", F32, tag + "_dh_u", res=dh, exchange=_join(traffic.reduce_chips(ku), traffic.share(kd)))
    traffic.sum_chips(kg)
    traffic.sum_chips(ku)
    dx, dshift, dscale, dnorm_g, *gated = _norm_mod_bwd(dh, x, norm_g, scale, dx_out, tag + "_norm_bwd", f=f_below,
                                                        gate=gate_below, exchange=_join(traffic.share(kg), traffic.share(ku)))
    return dx, (dshift, dscale, dnorm_g), gated


def _layer_step(x, target, mod, gains, forget_bias, conv_w, traffic, att_w, in_shard, in_rows):
    sh1, sc1, g1, sh2, sc2, g2, sh3, sc3, g3 = mod
    norm1_g, norm2_g, norm3_g, final_g, group_g = gains
    s, d = x.shape
    n_heads = att_w // HEAD_DIM
    npair = n_heads // 2
    gate1, gate3 = 0.5 * g1, 0.5 * g3

    def split_w_in(w_in_pad):
        w_in_t = w_in_pad.reshape(N_CHIPS, in_rows, d)[:, :in_shard].reshape(N_CHIPS * in_shard, d)
        return (w_in_t[:3 * att_w], _pad_rows(w_in_t[3 * att_w:3 * att_w + n_heads], LANES), w_in_t[3 * att_w + n_heads:])

    _run_exchange(traffic.gather_ici("ffn1_gu"), "gather_ffn1_ici")
    _run_exchange(traffic.gather_pass("ffn1_gu"), "gather_ffn1_pass")
    wg1_t, wu1_t = traffic.weights("ffn1_gu")

    def wd1_ready():
        _run_exchange(traffic.gather_pass("ffn1_d"), "gather_ffn1_down_pass")
        return traffic.weights("ffn1_d")[0]

    x1, saved1 = _ffn_fwd(x, norm1_g, sh1, sc1, gate1, wg1_t, wu1_t, wd1_ready, "ffn1",
                          up_exchange=_join(traffic.gather_ici("ffn1_d"), traffic.gather_ici("mix")),
                          down_exchange=lambda: traffic.gather_pass("mix"))
    wd1 = traffic.weights("ffn1_d")[0]
    w_in_pad, w_out = traffic.weights("mix")
    wqkv_t, wf_t, wbcx_t = split_w_in(w_in_pad)

    h2 = _norm_mod_fwd(x1, norm2_g, sh2, sc2, "mix_norm_fwd")
    qkv = _mm(h2, wqkv_t, "nt", BF16, "mix_proj_qkv")
    bcx = _mm(h2, wbcx_t, "nt", F32, "mix_proj_bcx")
    flog = _mm(h2, wf_t, "nt", F32, "mix_proj_f")
    flog_t = jnp.pad(flog[:, :n_heads].T, ((0, HEAD_ROWS - n_heads), (0, 0)))
    bias_col = jnp.pad(forget_bias, (0, HEAD_ROWS - n_heads))[:, None]
    f_pieces = _forget_fwd(flog_t, bias_col, "forget_fwd")
    qa, ka, va = _attn_prep(qkv, f_pieces, "attn_prep")
    att, lse = _attn_fwd(qa, ka, va, "attn_fwd", exchange=traffic.gather_ici("ffn2"))
    cv = _conv_fwd(bcx, conv_w, "conv_fwd")
    yn = _gnorm_fwd(att, cv, group_g, "gnorm_fwd")
    x2, mix = _mm(yn, w_out, "nn", F32, "mix_out", res=x1, gate=g2, aux_dtype=BF16, exchange=traffic.gather_pass("ffn2"))
    wg2_t, wu2_t, wd2 = traffic.weights("ffn2")

    x3, saved3 = _ffn_fwd(x2, norm3_g, sh3, sc3, gate3, wg2_t, wu2_t, wd2, "ffn2")

    dx3, loss_row, dfinal_g, df2, dgate3 = _final_loss(x3, final_g, target, saved3[4], gate3, "final_loss")

    dx2, (dsh3, dsc3, dnorm3_g), (dmix, dg2) = _ffn_bwd(
        dx3, df2, x2, saved3, norm3_g, sc3, wg2_t, wu2_t, wd2, "ffn2", traffic, below=(mix, g2))
    dyn = _mm(dmix, w_out, "nt", F32, "mix_out_dyn")
    dw_out = _mm(yn, dmix, "tn", BF16, "mix_out_dw")
    datt, dcv, dgroup_g = _gnorm_bwd(dyn, att, cv, group_g, "gnorm_bwd")
    db, dc, dxc, dconv_w = _conv_bwd(dcv, bcx, conv_w, "conv_bwd")
    dbcx = jnp.concatenate([db, dc, dxc], axis=1)
    dq, dk, dv, qx, kx = _attn_bwd(qa, ka, va, datt, att, lse, "attn_bwd", exchange=traffic.reduce_chips("ffn2"))
    traffic.sum_chips("ffn2")
    dqkv = jnp.concatenate([dq.astype(BF16), dk, dv], axis=1)
    df_t = _decay_grads(qx, kx, "decay_grads")[:, :HEAD_ROWS].T
    dflog_t, dbias_col = _forget_bwd(df_t, flog_t, bias_col, "forget_bwd")
    dflog = jnp.pad(dflog_t[:n_heads].T, ((0, 0), (0, LANES - n_heads))).astype(BF16)
    dh2 = _mm(dqkv, wqkv_t, "nn", F32, "mix_dh_qkv", exchange=traffic.share("ffn2"))
    dh2 = _mm(dbcx, wbcx_t, "nn", F32, "mix_dh_bcx", res=dh2)
    dh2 = _mm(dflog, wf_t, "nn", F32, "mix_dh_f", res=dh2)
    dwqkv_t = _mm(dqkv, h2, "tn", BF16, "mix_dw_qkv")
    dwbcx_t = _mm(dbcx, h2, "tn", BF16, "mix_dw_bcx")
    dwf_t = _mm(dflog, h2, "tn", BF16, "mix_dw_f")
    dw_in_t = jnp.concatenate([dwqkv_t, dwf_t[:n_heads], dwbcx_t], axis=0).reshape(N_CHIPS, in_shard, d)
    dw_in_t = jnp.pad(dw_in_t, ((0, 0), (0, in_rows - in_shard), (0, 0))).reshape(N_CHIPS * in_rows, d)
    dx1, dsh2, dsc2, dnorm2_g, df1, dgate1 = _norm_mod_bwd(
        dh2, x1, norm2_g, sc2, dx2, "mix_norm_bwd", f=saved1[4], gate=gate1,
        exchange=traffic.reduce_sibling("mix", [dw_in_t, dw_out]))
    traffic.add_halves("mix")

    def share_mix():
        traffic.sum_chips("mix")
        return traffic.share("mix")

    dx, (dsh1, dsc1, dnorm1_g), _ = _ffn_bwd(
        dx1, df1, x, saved1, norm1_g, sc1, wg1_t, wu1_t, wd1, "ffn1", traffic,
        dact_exchange=traffic.reduce_chips("mix"), dw_exchange=share_mix, finish_reduction=True)

    dmod = [dsh1, dsc1, 0.5 * dgate1, dsh2, dsc2, dg2, dsh3, dsc3, 0.5 * dgate3]
    dgains = [dnorm1_g, dnorm2_g, dnorm3_g, dfinal_g, dgroup_g]
    dbias = dbias_col[:n_heads, 0]
    return dx, loss_row, dmod, dgains, dbias, dconv_w


SMALL_ROWS = 24
ROW_GAINS, ROW_LOSS, ROW_FORGET, ROW_CONV, ROW_MOD = 0, 5, 6, 7, 10
PROW_ADA_B, PROW_GAINS, PROW_FORGET, PROW_CONV = 0, 9, 14, 15


def _round_up(n, m):
    return -(-n // m) * m


def _pad_rows(a, rows):
    return jnp.pad(a, ((0, rows - a.shape[0]), (0, 0)))


def _halves(a):
    return a.reshape(2, a.shape[0] // 2, a.shape[1])


def _rows_at(a, r0, total, width):
    return jnp.pad(a, ((r0, total - r0 - a.shape[0]), (0, width - a.shape[1])))


def kernel(x, c, ada_w, ada_b, norm1_g, ffn1_w_gate, ffn1_w_up, ffn1_w_down, norm2_g, w_in, forget_bias, conv_w, group_norm_g, w_out, norm3_g, ffn2_w_gate, ffn2_w_up, ffn2_w_down, final_g, loss_target, m_ada_w, m_ada_b, m_norm1_g, m_ffn1_w_gate, m_ffn1_w_up, m_ffn1_w_down, m_norm2_g, m_w_in, m_forget_bias, m_conv_w, m_group_norm_g, m_w_out, m_norm3_g, m_ffn2_w_gate, m_ffn2_w_up, m_ffn2_w_down, m_final_g, v_ada_w, v_ada_b, v_norm1_g, v_ffn1_w_gate, v_ffn1_w_up, v_ffn1_w_down, v_norm2_g, v_w_in, v_forget_bias, v_conv_w, v_group_norm_g, v_w_out, v_norm3_g, v_ffn2_w_gate, v_ffn2_w_up, v_ffn2_w_down, v_final_g):
    xi, yi, ci = lax.axis_index("x"), lax.axis_index("y"), lax.axis_index("c")
    chip = 2 * xi + yi
    dev = 4 * xi + 2 * yi + ci
    _, s, d = x.shape
    att_w = d // 2
    conv_width = d - att_w
    n_heads = att_w // HEAD_DIM
    in_shard = w_in.shape[1]
    in_rows = _round_up(in_shard, 32)
    cs = conv_w.shape[1]
    mod_shard = ada_w.shape[1]
    assert N_MOD * d == N_CHIPS * mod_shard and conv_width == N_CHIPS * cs and n_heads % 2 == 0

    pack0 = _rows_at(c, 0, 8, d) + _rows_at(conv_w, 1, 8, d)
    got0 = _all_gather_small(pack0, "gather_cond").reshape(N_DEV, 8, d)
    c16 = _pad_rows(got0[:, 0, :], 16)
    conv_full = got0[0::2, 1:1 + CONV_K, :cs].transpose(1, 0, 2).reshape(CONV_K, conv_width)

    ada_b_mine = lax.dynamic_slice(ada_b, (chip * mod_shard,), (mod_shard,))[None, :]
    mod_part = _ada_fwd(c16, ada_w, ada_b_mine, "ada_fwd")
    got1 = _all_gather_small(mod_part, "gather_mod").reshape(N_DEV, 16, mod_shard)
    mod_mine = lax.dynamic_index_in_dim(got1[0::2], dev, axis=1, keepdims=False).reshape(N_MOD, d)
    mod = [mod_mine[i:i + 1] for i in range(N_MOD)]

    def t_bf(w):
        return w.T.astype(BF16)

    shards = {"ffn1_gu": [_halves(t_bf(ffn1_w_gate)), _halves(t_bf(ffn1_w_up))], "ffn1_d": [_halves(ffn1_w_down.astype(BF16))],
              "mix": [_halves(_pad_rows(t_bf(w_in), in_rows)), _halves(w_out.astype(BF16))],
              "ffn2": [_halves(t_bf(ffn2_w_gate)), _halves(t_bf(ffn2_w_up)), _halves(ffn2_w_down.astype(BF16))]}
    core = ci.astype(jnp.int32).reshape(1)
    chip_arr = chip.astype(jnp.int32).reshape(1)
    traffic = _WeightTraffic(shards, core, chip_arr)

    gains = [g[None, :] for g in (norm1_g, norm2_g, norm3_g, final_g, group_norm_g)]
    dx, loss_row, dmod, dgains, dbias, dconv_w = _layer_step(
        x[0], loss_target[0], mod, gains, forget_bias, conv_full, traffic, att_w, in_shard, in_rows)

    pack = sum(_rows_at(g, ROW_GAINS + i, SMALL_ROWS, d) for i, g in enumerate(dgains))
    pack += _rows_at(loss_row, ROW_LOSS, SMALL_ROWS, d) + _rows_at(dbias[None, :], ROW_FORGET, SMALL_ROWS, d)
    pack += _rows_at(dconv_w, ROW_CONV, SMALL_ROWS, d)
    pack += sum(_rows_at(g, ROW_MOD + i, SMALL_ROWS, d) for i, g in enumerate(dmod))
    got2 = _all_gather_small(pack, "gather_small_grads").reshape(N_DEV, SMALL_ROWS, d)
    tot = _sum_devices(got2, "sum_small_grads")
    loss = tot[ROW_LOSS, 0]
    grad_ada_b = tot[ROW_MOD:ROW_MOD + N_MOD].reshape(N_MOD * d)
    grad_conv = lax.dynamic_slice(tot[ROW_CONV:ROW_CONV + CONV_K], (0, chip * cs), (CONV_K, cs))
    dmod_all = got2[:, ROW_MOD:ROW_MOD + N_MOD, :].reshape(N_DEV, N_MOD * d)
    dmod16 = _pad_rows(lax.dynamic_slice(dmod_all, (0, chip * mod_shard), (N_DEV, mod_shard)), 16)

    totals = (traffic.totals("ffn1_wg") + traffic.totals("ffn1_wu") + traffic.totals("ffn1_wd")
              + traffic.totals("mix") + traffic.totals("ffn2"))

    names = ("ffn1_w_gate", "ffn1_w_up", "ffn1_w_down", "w_in", "w_out", "ffn2_w_gate", "ffn2_w_up", "ffn2_w_down")
    transposed = ("ffn1_w_gate", "ffn1_w_up", "w_in", "ffn2_w_gate", "ffn2_w_up")
    params = {"ffn1_w_gate": (ffn1_w_gate, m_ffn1_w_gate, v_ffn1_w_gate), "ffn1_w_up": (ffn1_w_up, m_ffn1_w_up, v_ffn1_w_up),
              "ffn1_w_down": (ffn1_w_down, m_ffn1_w_down, v_ffn1_w_down), "w_in": (w_in, m_w_in, v_w_in),
              "w_out": (w_out, m_w_out, v_w_out), "ffn2_w_gate": (ffn2_w_gate, m_ffn2_w_gate, v_ffn2_w_gate),
              "ffn2_w_up": (ffn2_w_up, m_ffn2_w_up, v_ffn2_w_up), "ffn2_w_down": (ffn2_w_down, m_ffn2_w_down, v_ffn2_w_down)}
    out = {}
    for name_, (mine, theirs) in zip(names, totals):
        w, m, v = params[name_]
        if name_ in transposed:
            w, m, v = w.T, m.T, v.T
        if name_ == "w_in":
            both = jnp.where(ci == 0, jnp.concatenate([mine, theirs]), jnp.concatenate([theirs, mine]))[:in_shard]
            res = (both,) + tuple(_adamw(w, both, m, v, "adamw_" + name_))
        else:
            res = _adamw_halves(w, mine, theirs, m, v, core, "adamw_" + name_)
        out[name_] = tuple(r.T for r in res) if name_ in transposed else tuple(res)
    c16_t = c16.T
    out["ada_w"] = tuple(_ada_update(c16_t, dmod16, ada_w, m_ada_w, v_ada_w, "adamw_ada_w"))

    def small_pack(ada_b_, gains_, forget_, conv_):
        p = _rows_at(ada_b_.reshape(N_MOD, d), PROW_ADA_B, SMALL_ROWS, d)
        p += sum(_rows_at(g[None, :], PROW_GAINS + i, SMALL_ROWS, d) for i, g in enumerate(gains_))
        p += _rows_at(forget_[None, :], PROW_FORGET, SMALL_ROWS, d) + _rows_at(conv_, PROW_CONV, SMALL_ROWS, d)
        return p

    g_gains = [tot[ROW_GAINS + i] for i in range(5)]
    g_forget = tot[ROW_FORGET, :n_heads]
    sw = small_pack(ada_b, (norm1_g, norm2_g, norm3_g, final_g, group_norm_g), forget_bias, conv_w)
    sm = small_pack(m_ada_b, (m_norm1_g, m_norm2_g, m_norm3_g, m_final_g, m_group_norm_g), m_forget_bias, m_conv_w)
    sv = small_pack(v_ada_b, (v_norm1_g, v_norm2_g, v_norm3_g, v_final_g, v_group_norm_g), v_forget_bias, v_conv_w)
    sg = small_pack(grad_ada_b, g_gains, g_forget, grad_conv)
    small = (sg,) + tuple(_adamw(sw, sg, sm, sv, "adamw_small"))

    def unpack(p):
        r = {"ada_b": p[PROW_ADA_B:PROW_ADA_B + N_MOD].reshape(N_MOD * d), "forget_bias": p[PROW_FORGET, :n_heads],
             "conv_w": p[PROW_CONV:PROW_CONV + CONV_K, :cs]}
        for i, nm in enumerate(("norm1_g", "norm2_g", "norm3_g", "final_g", "group_norm_g")):
            r[nm] = p[PROW_GAINS + i]
        return r

    small = [unpack(p) for p in small]
    order = ("ada_w", "ada_b", "norm1_g", "ffn1_w_gate", "ffn1_w_up", "ffn1_w_down", "norm2_g", "w_in", "forget_bias",
             "conv_w", "group_norm_g", "w_out", "norm3_g", "ffn2_w_gate", "ffn2_w_up", "ffn2_w_down", "final_g")
    result = [loss, dx[None]]
    for k in range(4):
        result += [out[nm][k] if nm in out else small[k][nm] for nm in order]
    return tuple(result)
```

```python
import functools
import math

import jax
import jax.numpy as jnp
from jax import lax
from jax.experimental import pallas as pl
from jax.experimental.pallas import tpu as pltpu

F32 = jnp.float32
BF16 = jnp.bfloat16

HEAD_DIM = 64
CONV_K = 3
N_MOD = 9
EPS = 1e-6
ADAM_LR = 0.001
ADAM_B1 = 0.9
ADAM_B2 = 0.999
ADAM_EPS = 1e-08
ADAM_WD = 0.01
ADAM_STEP = 10

LANES = 128
N_CHIPS = 4
N_DEV = 8
VMEM_LIMIT_BYTES = 56 * 1024 * 1024
MAX_CONTRACTION = 4096
NEG_BIG = -1e30
MESH = pl.DeviceIdType.MESH

_NT = (((1,), (1,)), ((), ()))
_NN = (((1,), (0,)), ((), ()))
_TN = (((0,), (0,)), ((), ()))


def _params(*sem):
    return pltpu.CompilerParams(dimension_semantics=sem, vmem_limit_bytes=VMEM_LIMIT_BYTES)


class _Exchange:
    def __init__(self, inputs, out_shapes, n_sems, start, finish, aliases=None):
        self.inputs, self.out_shapes, self.n_sems = list(inputs), list(out_shapes), n_sems
        self.start, self.finish, self.aliases = start, finish, dict(aliases or {})
        self.results = None

    def set_results(self, results):
        self.results = list(results)


class _SemaphoreWindow:
    def __init__(self, sems, base):
        self.sems, self.base = sems, base
        self.at = self

    def __getitem__(self, k):
        return self.sems.at[self.base + k]


class _JoinedExchange(_Exchange):
    def __init__(self, parts):
        self.parts = parts
        assert all(not p.aliases for p in parts)

        def each(method, src, dst, send_sems, recv_sems):
            i0 = o0 = s0 = 0
            for p in parts:
                i1, o1 = i0 + len(p.inputs), o0 + len(p.out_shapes)
                getattr(p, method)(src[i0:i1], dst[o0:o1], _SemaphoreWindow(send_sems, s0), _SemaphoreWindow(recv_sems, s0))
                i0, o0, s0 = i1, o1, s0 + p.n_sems

        super().__init__([a for p in parts for a in p.inputs], [o for p in parts for o in p.out_shapes],
                         sum(p.n_sems for p in parts), functools.partial(each, "start"), functools.partial(each, "finish"))

    def set_results(self, results):
        o0 = 0
        for p in self.parts:
            p.set_results(results[o0:o0 + len(p.out_shapes)])
            o0 += len(p.out_shapes)


def _join(*parts):
    return parts[0] if len(parts) == 1 else _JoinedExchange(list(parts))


def _pc(body, exchange=None, **kw):
    if exchange is None:
        return pl.pallas_call(body, **kw)
    grid = kw["grid"]
    single = not isinstance(kw["out_shape"], (tuple, list))
    out_shape = [kw["out_shape"]] if single else list(kw["out_shape"])
    out_specs = [kw["out_specs"]] if single else list(kw["out_specs"])
    in_specs = list(kw["in_specs"])
    scratch = list(kw.get("scratch_shapes", ()))
    n_in, n_out, n_scr = len(in_specs), len(out_shape), len(scratch)
    n_xi, n_xo = len(exchange.inputs), len(exchange.out_shapes)

    def wrapped(*refs):
        pos = [n_in, n_in + n_xi, n_in + n_xi + n_out, n_in + n_xi + n_out + n_xo]
        ins, x_in, outs, x_out = refs[:pos[0]], refs[pos[0]:pos[1]], refs[pos[1]:pos[2]], refs[pos[2]:pos[3]]
        scr = refs[pos[3]:pos[3] + n_scr]
        send_sems, recv_sems = refs[pos[3] + n_scr:]
        ids = [pl.program_id(a) for a in range(len(grid))]
        first = functools.reduce(jnp.logical_and, [i == 0 for i in ids])
        last = functools.reduce(jnp.logical_and, [i == g - 1 for i, g in zip(ids, grid)])

        @pl.when(first)
        def _():
            exchange.start(x_in, x_out, send_sems, recv_sems)

        body(*ins, *outs, *scr)

        @pl.when(last)
        def _():
            exchange.finish(x_in, x_out, send_sems, recv_sems)

    call = pl.pallas_call(
        wrapped, out_shape=tuple(out_shape) + tuple(exchange.out_shapes), grid=grid,
        in_specs=in_specs + [_ANY] * n_xi, out_specs=tuple(out_specs) + (_ANY,) * n_xo,
        scratch_shapes=scratch + [pltpu.SemaphoreType.DMA((exchange.n_sems,)), pltpu.SemaphoreType.DMA((exchange.n_sems,))],
        input_output_aliases={n_in + a: n_out + b for a, b in exchange.aliases.items()},
        compiler_params=_params(*(["arbitrary"] * len(grid))), name=kw["name"])

    def run(*args):
        res = call(*args, *exchange.inputs)
        exchange.set_results(res[n_out:])
        return res[0] if single else tuple(res[:n_out])

    return run


_ANY = pl.BlockSpec(memory_space=pl.ANY)


def _tile(n, pref, mult):
    best = None
    t = mult
    while t <= min(n, pref):
        if n % t == 0:
            best = t
        t += mult
    return n if best is None else best


def _sds(shape, dtype):
    return jax.ShapeDtypeStruct(shape, dtype)


def _vec_spec(d):
    return pl.BlockSpec((1, d), lambda *_: (0, 0))


def _norm_mod_fwd(x, g, shift, scale, name):
    s, d = x.shape
    tr = _tile(s, 512, 16)

    def body(x_ref, g_ref, sh_ref, sc_ref, h_ref):
        xv = x_ref[...]
        rstd = lax.rsqrt(jnp.mean(xv * xv, axis=-1, keepdims=True) + EPS)
        n = xv * rstd * g_ref[...]
        h_ref[...] = (n * (1.0 + sc_ref[...]) + sh_ref[...]).astype(BF16)

    row = pl.BlockSpec((tr, d), lambda i: (i, 0))
    return _pc(body, out_shape=_sds((s, d), BF16), grid=(s // tr,),
               in_specs=[row, _vec_spec(d), _vec_spec(d), _vec_spec(d)], out_specs=row,
               compiler_params=_params("parallel"), name=name)(x, g, shift, scale)


def _through_gate(dx, f_ref, gate_ref, df_ref, dgate_ref):
    df_ref[...] = (dx * gate_ref[...]).astype(BF16)
    dgate_ref[...] += jnp.sum(dx * f_ref[...].astype(F32), axis=0, keepdims=True)


def _norm_mod_bwd(dh, x, g, scale, dres, name, f=None, gate=None, exchange=None):
    s, d = x.shape
    tr = _tile(s, 256, 16)
    gated = f is not None

    def body(dh_ref, x_ref, g_ref, sc_ref, dres_ref, *rest):
        f_ref, gate_ref = rest[:2] if gated else (None, None)
        dx_ref, dsh_ref, dsc_ref, dg_ref = rest[2:6] if gated else rest[:4]
        df_ref, dgate_ref = rest[6:8] if gated else (None, None)

        @pl.when(pl.program_id(0) == 0)
        def _():
            for ref in (dsh_ref, dsc_ref, dg_ref) + ((dgate_ref,) if gated else ()):
                ref[...] = jnp.zeros_like(ref)

        xv = x_ref[...]
        dhv = dh_ref[...]
        gv = g_ref[...]
        rstd = lax.rsqrt(jnp.mean(xv * xv, axis=-1, keepdims=True) + EPS)
        xhat = xv * rstd
        dn = dhv * (1.0 + sc_ref[...])
        dsh_ref[...] += jnp.sum(dhv, axis=0, keepdims=True)
        dsc_ref[...] += jnp.sum(dhv * (xhat * gv), axis=0, keepdims=True)
        dg_ref[...] += jnp.sum(dn * xhat, axis=0, keepdims=True)
        dxh = dn * gv
        proj = jnp.mean(dxh * xhat, axis=-1, keepdims=True)
        dx = dres_ref[...] + rstd * (dxh - xhat * proj)
        dx_ref[...] = dx
        if gated:
            _through_gate(dx, f_ref, gate_ref, df_ref, dgate_ref)

    row = pl.BlockSpec((tr, d), lambda i: (i, 0))
    vec = _vec_spec(d)
    out_shape = [_sds((s, d), F32), _sds((1, d), F32), _sds((1, d), F32), _sds((1, d), F32)]
    out_specs, in_specs, args = [row, vec, vec, vec], [row, row, vec, vec, row], [dh, x, g, scale, dres]
    if gated:
        out_shape += [_sds((s, d), BF16), _sds((1, d), F32)]
        out_specs += [row, vec]
        in_specs += [row, vec]
        args += [f, gate]
    return _pc(body, exchange, out_shape=tuple(out_shape), grid=(s // tr,), in_specs=in_specs,
               out_specs=tuple(out_specs), compiler_params=_params("arbitrary"), name=name)(*args)


def _final_loss(x, g, target, f, gate, name):
    s, d = x.shape
    tr = _tile(s, 256, 16)
    nsteps = s // tr

    def body(x_ref, g_ref, t_ref, f_ref, gate_ref, dx_ref, loss_ref, dg_ref, df_ref, dgate_ref):
        i = pl.program_id(0)

        @pl.when(i == 0)
        def _():
            loss_ref[...] = jnp.zeros_like(loss_ref)
            dg_ref[...] = jnp.zeros_like(dg_ref)
            dgate_ref[...] = jnp.zeros_like(dgate_ref)

        xv = x_ref[...]
        gv = g_ref[...]
        rstd = lax.rsqrt(jnp.mean(xv * xv, axis=-1, keepdims=True) + EPS)
        xhat = xv * rstd
        err = xhat * gv - t_ref[...]
        dy = err * (1.0 / d)
        loss_ref[...] += jnp.sum(0.5 * err * dy, axis=0, keepdims=True)
        dg_ref[...] += jnp.sum(dy * xhat, axis=0, keepdims=True)
        dxh = dy * gv
        proj = jnp.mean(dxh * xhat, axis=-1, keepdims=True)
        dx = rstd * (dxh - xhat * proj)
        dx_ref[...] = dx
        _through_gate(dx, f_ref, gate_ref, df_ref, dgate_ref)

        @pl.when(i == nsteps - 1)
        def _():
            loss_ref[...] = jnp.broadcast_to(jnp.sum(loss_ref[...], axis=-1, keepdims=True), loss_ref.shape)

    row = pl.BlockSpec((tr, d), lambda i: (i, 0))
    vec = _vec_spec(d)
    return _pc(body, out_shape=(_sds((s, d), F32), _sds((1, d), F32), _sds((1, d), F32), _sds((s, d), BF16), _sds((1, d), F32)),
               grid=(nsteps,), in_specs=[row, vec, row, row, vec], out_specs=(row, vec, vec, row, vec),
               compiler_params=_params("arbitrary"), name=name)(x, g, target, f, gate)


def _mm(lhs, rhs, dims, out_dtype, name, res=None, gate=None, aux_dtype=None, exchange=None):
    lhs_list = list(lhs) if isinstance(lhs, (list, tuple)) else [lhs]
    rhs_list = list(rhs) if isinstance(rhs, (list, tuple)) else [rhs]
    n_terms = len(lhs_list)
    assert n_terms == len(rhs_list)
    m = lhs_list[0].shape[1 if dims == "tn" else 0]
    n = rhs_list[0].shape[0 if dims == "nt" else 1]
    tn = _tile(n, 1024, LANES)
    tm = _tile(m, 512, LANES if dims == "tn" else 16)
    dn = {"nn": _NN, "nt": _NT, "tn": _TN}[dims]
    in_specs, args = [], []
    for l, r in zip(lhs_list, rhs_list):
        k = l.shape[0 if dims == "tn" else 1]
        assert k == r.shape[1 if dims == "nt" else 0] and k <= MAX_CONTRACTION, (l.shape, r.shape, dims)
        in_specs.append(pl.BlockSpec((k, tm), lambda i, j: (0, i)) if dims == "tn" else pl.BlockSpec((tm, k), lambda i, j: (i, 0)))
        in_specs.append(pl.BlockSpec((tn, k), lambda i, j: (j, 0)) if dims == "nt" else pl.BlockSpec((k, tn), lambda i, j: (0, j)))
        args += [l, r]
    out_spec = pl.BlockSpec((tm, tn), lambda i, j: (i, j))
    has_res, has_gate, has_aux = res is not None, gate is not None, aux_dtype is not None

    def body(*refs):
        refs = list(refs)
        pos = 2 * n_terms
        res_ref = gate_ref = aux_ref = None
        if has_res:
            res_ref = refs[pos]; pos += 1
        if has_gate:
            gate_ref = refs[pos]; pos += 1
        out_ref = refs[pos]; pos += 1
        if has_aux:
            aux_ref = refs[pos]
        acc = lax.dot_general(refs[0][...], refs[1][...], dn, preferred_element_type=F32)
        for p in range(1, n_terms):
            acc += lax.dot_general(refs[2 * p][...], refs[2 * p + 1][...], dn, preferred_element_type=F32)
        if has_aux:
            aux_ref[...] = acc.astype(aux_dtype)
        if has_gate:
            acc = acc * gate_ref[...]
        if has_res:
            acc = res_ref[...] + acc
        out_ref[...] = acc.astype(out_dtype)

    if has_res:
        in_specs.append(out_spec); args.append(res)
    if has_gate:
        in_specs.append(pl.BlockSpec((1, tn), lambda i, j: (0, j))); args.append(gate)
    out_shape = [_sds((m, n), out_dtype)]
    out_specs = [out_spec]
    if has_aux:
        out_shape.append(_sds((m, n), aux_dtype)); out_specs.append(out_spec)
    outs = _pc(body, exchange, out_shape=tuple(out_shape), grid=(m // tm, n // tn), in_specs=in_specs,
               out_specs=tuple(out_specs), compiler_params=_params("parallel", "parallel"), name=name)(*args)
    return outs if has_aux else outs[0]


def _ffn_up(h, wg_t, wu_t, name, exchange=None):
    s, d = h.shape
    f = wg_t.shape[0]
    tm = _tile(s, 1024, 16)
    tn = _tile(f, 256, LANES)

    def body(h_ref, wg_ref, wu_ref, a_ref, u_ref, hid_ref):
        hv = h_ref[...]
        a = lax.dot_general(hv, wg_ref[...], _NT, preferred_element_type=F32)
        u = lax.dot_general(hv, wu_ref[...], _NT, preferred_element_type=F32)
        a_ref[...] = a.astype(BF16)
        u_ref[...] = u.astype(BF16)
        hid_ref[...] = (a * jax.nn.sigmoid(a) * u).astype(BF16)

    hs = pl.BlockSpec((tm, d), lambda i, j: (i, 0))
    ws = pl.BlockSpec((tn, d), lambda i, j: (j, 0))
    os_ = pl.BlockSpec((tm, tn), lambda i, j: (i, j))
    return _pc(body, exchange, out_shape=(_sds((s, f), BF16),) * 3, grid=(s // tm, f // tn),
               in_specs=[hs, ws, ws], out_specs=(os_, os_, os_),
               compiler_params=_params("parallel", "parallel"), name=name)(h, wg_t, wu_t)


def _ffn_dact(df, wd, a, u, name, exchange=None):
    s, d = df.shape
    f = wd.shape[0]
    tm = _tile(s, 1024, 16)
    tn = _tile(f, 256, LANES)

    def body(df_ref, wd_ref, a_ref, u_ref, da_ref, du_ref):
        dhid = lax.dot_general(df_ref[...], wd_ref[...], _NT, preferred_element_type=F32)
        av = a_ref[...].astype(F32)
        uv = u_ref[...].astype(F32)
        sig = jax.nn.sigmoid(av)
        da_ref[...] = (dhid * uv * (sig * (1.0 + av * (1.0 - sig)))).astype(BF16)
        du_ref[...] = (dhid * (av * sig)).astype(BF16)

    ds_ = pl.BlockSpec((tm, d), lambda i, j: (i, 0))
    ws = pl.BlockSpec((tn, d), lambda i, j: (j, 0))
    os_ = pl.BlockSpec((tm, tn), lambda i, j: (i, j))
    return _pc(body, exchange, out_shape=(_sds((s, f), BF16),) * 2, grid=(s // tm, f // tn),
               in_specs=[ds_, ws, os_, os_], out_specs=(os_, os_),
               compiler_params=_params("parallel", "parallel"), name=name)(df, wd, a, u)


def _split3(v):
    hi = v.astype(BF16)
    r1 = v - hi.astype(F32)
    mid = r1.astype(BF16)
    lo = (r1 - mid.astype(F32)).astype(BF16)
    return hi, mid, lo


def _dot3(v, mat):
    hi, mid, lo = _split3(v)
    out = lax.dot_general(hi, mat, _NN, preferred_element_type=F32)
    out += lax.dot_general(mid, mat, _NN, preferred_element_type=F32)
    out += lax.dot_general(lo, mat, _NN, preferred_element_type=F32)
    return out


def _forget_fwd(flog_t, bias, name):
    h, s = flog_t.shape
    blk = _tile(s, 512, LANES)
    tri = (jnp.arange(blk)[:, None] <= jnp.arange(blk)[None, :]).astype(BF16)

    def body(z_ref, b_ref, tri_ref, f_ref, carry):
        @pl.when(pl.program_id(0) == 0)
        def _():
            carry[...] = jnp.zeros_like(carry)

        z = z_ref[...] + b_ref[...]
        e = jnp.exp(-jnp.abs(z))
        w = 1.0 + e
        log1p_e = jnp.where(w == 1.0, e, jnp.log(w) * (e / (w - 1.0)))
        lf = jnp.minimum(z, 0.0) - log1p_e
        out = carry[...] + _dot3(lf, tri_ref[...])
        for j, piece in enumerate(_split3(out)):
            f_ref[j] = piece
        carry[...] = out[:, blk - 1:blk]

    zs = pl.BlockSpec((h, blk), lambda i: (0, i))
    return _pc(body, out_shape=_sds((3, h, s), BF16), grid=(s // blk,),
               in_specs=[zs, pl.BlockSpec((h, 1), lambda i: (0, 0)), pl.BlockSpec((blk, blk), lambda i: (0, 0))],
               out_specs=pl.BlockSpec((3, h, blk), lambda i: (0, 0, i)), scratch_shapes=[pltpu.VMEM((h, 1), F32)],
               compiler_params=_params("arbitrary"), name=name)(flog_t, bias, tri)


def _forget_bwd(df_t, flog_t, bias, name):
    h, s = flog_t.shape
    blk = _tile(s, 512, LANES)
    nb = s // blk
    tri = (jnp.arange(blk)[:, None] >= jnp.arange(blk)[None, :]).astype(BF16)

    def body(df_ref, z_ref, b_ref, tri_ref, dz_ref, db_ref, carry):
        @pl.when(pl.program_id(0) == 0)
        def _():
            carry[...] = jnp.zeros_like(carry)
            db_ref[...] = jnp.zeros_like(db_ref)

        rc = carry[...] + _dot3(df_ref[...], tri_ref[...])
        carry[...] = rc[:, 0:1]
        dz = rc * jax.nn.sigmoid(-(z_ref[...] + b_ref[...]))
        dz_ref[...] = dz
        db_ref[...] += jnp.sum(dz, axis=-1, keepdims=True)

    rev = pl.BlockSpec((h, blk), lambda i: (0, nb - 1 - i))
    col = pl.BlockSpec((h, 1), lambda i: (0, 0))
    return _pc(body, out_shape=(_sds((h, s), F32), _sds((h, 1), F32)), grid=(nb,),
               in_specs=[rev, rev, col, pl.BlockSpec((blk, blk), lambda i: (0, 0))],
               out_specs=(rev, col), scratch_shapes=[pltpu.VMEM((h, 1), F32)],
               compiler_params=_params("arbitrary"), name=name)(df_t, flog_t, bias, tri)


def _attn_tiles(s):
    return _tile(s, 1024, LANES)


def _attn_half(t):
    return t // 2 if t >= 4 * LANES else t


BIAS_ROWS = 16


def _attn_prep(qkv, f_pieces, name):
    s = qkv.shape[0]
    a_w = qkv.shape[1] // 3
    npair = a_w // LANES
    t = _attn_tiles(s)
    scale = 1.0 / math.sqrt(HEAD_DIM)

    six = f_pieces[:, :2 * npair].reshape(3, npair, 2, s).transpose(1, 3, 2, 0).reshape(npair, s, 6)
    feat = jnp.concatenate([six, jnp.ones((npair, s, 1), BF16), jnp.zeros((npair, s, BIAS_ROWS - 7), BF16)], axis=-1)
    place_q = [[0.0] * (2 * LANES) for _ in range(BIAS_ROWS)]
    place_k = [[0.0] * (2 * LANES) for _ in range(BIAS_ROWS)]
    for hh in range(2):
        b0 = hh * LANES + (HEAD_DIM if hh == 0 else 0)
        for j in range(3):
            place_q[3 * hh + j][b0 + j] = 1.0
            place_q[6][b0 + 3 + j] = 1.0
            place_k[6][b0 + j] = 1.0
            place_k[3 * hh + j][b0 + 3 + j] = -1.0
    place_q = jnp.array(place_q, BF16)
    place_k = jnp.array(place_k, BF16)

    def body(q_ref, k_ref, v_ref, f_ref, pq_ref, pk_ref, qa_ref, ka_ref, va_ref):
        lane = lax.broadcasted_iota(jnp.int32, (1, LANES), 1)
        q2 = (q_ref[...].astype(F32) * scale).astype(BF16)
        k2, v2 = k_ref[...], v_ref[...]
        qx = lax.dot_general(f_ref[0], pq_ref[...], _NN, preferred_element_type=F32).astype(BF16)
        kx = lax.dot_general(f_ref[0], pk_ref[...], _NN, preferred_element_type=F32).astype(BF16)
        for hh in range(2):
            real = (lane < HEAD_DIM) if hh == 0 else (lane >= HEAD_DIM)
            cols = slice(hh * LANES, (hh + 1) * LANES)
            qa_ref[:, cols] = jnp.where(real, q2, qx[:, cols])
            ka_ref[:, cols] = jnp.where(real, k2, kx[:, cols])
            va_ref[:, cols] = jnp.where(real, v2, jnp.zeros_like(v2))

    def col(off):
        return pl.BlockSpec((t, LANES), lambda p, i: (i, off + p))

    out = pl.BlockSpec((t, 2 * LANES), lambda p, i: (i, p))
    place = pl.BlockSpec((BIAS_ROWS, 2 * LANES), lambda p, i: (0, 0))
    return _pc(body, out_shape=(_sds((s, 2 * a_w), BF16),) * 3, grid=(npair, s // t),
               in_specs=[col(0), col(npair), col(2 * npair), pl.BlockSpec((1, t, BIAS_ROWS), lambda p, i: (p, i, 0)),
                         place, place],
               out_specs=(out, out, out), compiler_params=_params("parallel", "parallel"), name=name)(
                   qkv, qkv, qkv, feat, place_q, place_k)


def _attn_fwd(qa, ka, va, name, exchange=None):
    s = qa.shape[0]
    a_w = qa.shape[1] // 2
    npair = a_w // LANES
    t = _attn_tiles(s)
    nq = s // t
    half = _attn_half(t)

    def body(q_ref, k_ref, v_ref, o_ref, lse_ref, m_sc, l_sc, acc_sc):
        qi = pl.program_id(1)
        first = lax.broadcasted_iota(jnp.int32, (1, LANES), 1) < HEAD_DIM
        m_sc[...] = jnp.full_like(m_sc, NEG_BIG)
        l_sc[...] = jnp.zeros_like(l_sc)
        acc_sc[...] = jnp.zeros_like(acc_sc)

        def step(q0, k_start, size, diag):
            q_sl = slice(q0, q0 + size)
            k_rows = pl.ds(pl.multiple_of(k_start, size), size)
            m_old = m_sc[q_sl, :]
            keep = None
            if diag:
                keep = (lax.broadcasted_iota(jnp.int32, (size, size), 0) >= lax.broadcasted_iota(jnp.int32, (size, size), 1))
            m_new, rs, pv = [], [], []
            for hh in range(2):
                cols = slice(hh * LANES, (hh + 1) * LANES)
                sc = lax.dot_general(q_ref[q_sl, cols], k_ref[k_rows, cols], _NT, preferred_element_type=F32)
                if diag:
                    sc = jnp.where(keep, sc, NEG_BIG)
                mo = m_old[:, hh * HEAD_DIM:hh * HEAD_DIM + 1]
                mn = jnp.maximum(mo, jnp.max(sc, axis=1, keepdims=True))
                p = jnp.exp(sc - mn)
                m_new.append(mn)
                rs.append(jnp.sum(p, axis=1, keepdims=True))
                pv.append(lax.dot_general(p.astype(BF16), v_ref[k_rows, cols], _NN, preferred_element_type=F32))
            m2 = jnp.where(first, m_new[0], m_new[1])
            alpha = jnp.exp(m_old - m2)
            m_sc[q_sl, :] = m2
            l_sc[q_sl, :] = alpha * l_sc[q_sl, :] + jnp.where(first, rs[0], rs[1])
            acc_sc[q_sl, :] = alpha * acc_sc[q_sl, :] + pv[0] + pv[1]

        def below_diagonal(ki, carry):
            step(0, ki * t, t, False)
            return carry

        lax.fori_loop(0, qi, below_diagonal, 0)
        step(0, qi * t, half, True)
        if half < t:
            step(half, qi * t, half, False)
            step(half, qi * t + half, half, True)
        l2 = l_sc[...]
        o_ref[...] = acc_sc[...] / l2
        lse_ref[...] = m_sc[...] + jnp.log(l2)

    qs = pl.BlockSpec((t, 2 * LANES), lambda p, qi: (qi, p))
    ks = pl.BlockSpec((s, 2 * LANES), lambda p, qi: (0, p))
    os_ = pl.BlockSpec((t, LANES), lambda p, qi: (qi, p))
    return _pc(body, exchange, out_shape=(_sds((s, a_w), F32), _sds((s, a_w), F32)), grid=(npair, nq),
               in_specs=[qs, ks, ks], out_specs=(os_, os_),
               scratch_shapes=[pltpu.VMEM((t, LANES), F32)] * 3,
               compiler_params=_params("parallel", "arbitrary"), name=name)(qa, ka, va)


def _attn_bwd(qa, ka, va, do, o, lse, name, exchange=None):
    s = qa.shape[0]
    a_w = qa.shape[1] // 2
    npair = a_w // LANES
    t = _attn_tiles(s)
    nq = s // t
    half = _attn_half(t)
    scale = 1.0 / math.sqrt(HEAD_DIM)

    def body(q_ref, k_ref, v_ref, do_ref, o_ref, lse_ref, dq_ref, dk_ref, dv_ref, qx_ref, kx_ref, dk_sc, dv_sc, kx_sc):
        ki = pl.program_id(1)
        first = lax.broadcasted_iota(jnp.int32, (1, LANES), 1) < HEAD_DIM

        @pl.when(ki == 0)
        def _():
            dq_ref[...] = jnp.zeros_like(dq_ref)
            qx_ref[...] = jnp.zeros_like(qx_ref)

        def step(q_start, k0, size, diag, assign):
            rows = pl.ds(pl.multiple_of(q_start, size), size)
            k_sl = slice(k0, k0 + size)
            do2 = do_ref[rows, :]
            lse2 = lse_ref[rows, :]
            dd = do2.astype(F32) * o_ref[rows, :]
            keep = None
            if diag:
                keep = (lax.broadcasted_iota(jnp.int32, (size, size), 0) >= lax.broadcasted_iota(jnp.int32, (size, size), 1))
            dq_h, dk_h, dv_h = [], [], []
            for hh in range(2):
                sel = first if hh == 0 else jnp.logical_not(first)
                cols = slice(hh * LANES, (hh + 1) * LANES)
                qh, kh, vh = q_ref[rows, cols], k_ref[k_sl, cols], v_ref[k_sl, cols]
                delta = jnp.sum(jnp.where(sel, dd, 0.0), axis=1, keepdims=True)
                sc = lax.dot_general(qh, kh, _NT, preferred_element_type=F32)
                if diag:
                    sc = jnp.where(keep, sc, NEG_BIG)
                p = jnp.exp(sc - lse2[:, hh * HEAD_DIM:hh * HEAD_DIM + 1])
                dp = lax.dot_general(do2, vh, _NT, preferred_element_type=F32)
                ds_b = (p * (dp - delta)).astype(BF16)
                dv_h.append(lax.dot_general(p.astype(BF16), do2, _TN, preferred_element_type=F32))
                dk_h.append(lax.dot_general(ds_b, qh, _TN, preferred_element_type=F32))
                dq_h.append(lax.dot_general(ds_b, kh, _NN, preferred_element_type=F32))
            dq_ref[rows, :] += jnp.where(first, dq_h[0], dq_h[1]) * scale
            qx_ref[rows, :] += jnp.where(first, dq_h[1], dq_h[0])
            dk_new = jnp.where(first, dk_h[0], dk_h[1])
            kx_new = jnp.where(first, dk_h[1], dk_h[0])
            dv_new = jnp.where(first, dv_h[0], dv_h[1])
            if assign:
                dk_sc[k_sl, :] = dk_new
                kx_sc[k_sl, :] = kx_new
                dv_sc[k_sl, :] = dv_new
            else:
                dk_sc[k_sl, :] += dk_new
                kx_sc[k_sl, :] += kx_new
                dv_sc[k_sl, :] += dv_new

        def below_diagonal(qi, carry):
            step(qi * t, 0, t, False, False)
            return carry

        step(ki * t, 0, half, True, True)
        if half < t:
            step(ki * t + half, 0, half, False, False)
            step(ki * t + half, half, half, True, True)
        lax.fori_loop(ki + 1, nq, below_diagonal, 0)
        dk_ref[...] = dk_sc[...].astype(BF16)
        dv_ref[...] = dv_sc[...].astype(BF16)
        kx_ref[...] = kx_sc[...]

    ks2 = pl.BlockSpec((t, 2 * LANES), lambda p, ki: (ki, p))
    qs2 = pl.BlockSpec((s, 2 * LANES), lambda p, ki: (0, p))
    whole = pl.BlockSpec((s, LANES), lambda p, ki: (0, p))
    kout = pl.BlockSpec((t, LANES), lambda p, ki: (ki, p))
    return _pc(body, exchange,
               out_shape=(_sds((s, a_w), F32), _sds((s, a_w), BF16), _sds((s, a_w), BF16), _sds((s, a_w), F32),
                          _sds((s, a_w), F32)),
               grid=(npair, nq), in_specs=[qs2, ks2, ks2, whole, whole, whole],
               out_specs=(whole, kout, kout, whole, kout),
               scratch_shapes=[pltpu.VMEM((t, LANES), F32)] * 3,
               compiler_params=_params("parallel", "arbitrary"), name=name)(qa, ka, va, do, o, lse)

def _decay_grads(qx, kx, name):
    s, a_w = qx.shape
    n_heads = a_w // HEAD_DIM
    tr = _tile(s, 512, 8)
    pick_q = [[0.0] * LANES for _ in range(a_w)]
    pick_k = [[0.0] * LANES for _ in range(a_w)]
    for h in range(n_heads):
        b0 = (h // 2) * LANES + (HEAD_DIM if h % 2 == 0 else 0)
        pick_q[b0][h] = 1.0
        pick_k[b0 + 3][h] = 1.0
    pick_q = jnp.array(pick_q, BF16)
    pick_k = jnp.array(pick_k, BF16)

    def body(qx_ref, kx_ref, pq_ref, pk_ref, o_ref):
        o_ref[...] = _dot3(qx_ref[...], pq_ref[...]) - _dot3(kx_ref[...], pk_ref[...])

    row = pl.BlockSpec((tr, a_w), lambda i: (i, 0))
    pick = pl.BlockSpec((a_w, LANES), lambda i: (0, 0))
    return _pc(body, out_shape=_sds((s, LANES), F32), grid=(s // tr,), in_specs=[row, row, pick, pick],
               out_specs=pl.BlockSpec((tr, LANES), lambda i: (i, 0)),
               compiler_params=_params("parallel"), name=name)(qx, kx, pick_q, pick_k)


def _shift_down(z, k, rows):
    return jnp.where(rows >= k, pltpu.roll(z, k, 0), 0.0)


def _shift_up(z, k, rows, n):
    return jnp.where(rows < n - k, pltpu.roll(z, n - k, 0), 0.0)


def _conv_fwd(bcx, conv_w, name):
    s = bcx.shape[0]
    cw = bcx.shape[1] // 3
    nb = cw // LANES

    def body(b_ref, c_ref, x_ref, w_ref, cv_ref):
        rows = lax.broadcasted_iota(jnp.int32, (s, LANES), 0)
        z = c_ref[...] * x_ref[...]
        w = w_ref[...]
        y = w[2:3, :] * z + w[1:2, :] * _shift_down(z, 1, rows) + w[0:1, :] * _shift_down(z, 2, rows)
        cv_ref[...] = b_ref[...] * y

    def col(off):
        return pl.BlockSpec((s, LANES), lambda j: (0, j + off))

    return _pc(body, out_shape=_sds((s, cw), F32), grid=(nb,),
               in_specs=[col(0), col(nb), col(2 * nb), pl.BlockSpec((CONV_K, LANES), lambda j: (0, j))],
               out_specs=col(0), compiler_params=_params("parallel"), name=name)(bcx, bcx, bcx, conv_w)


def _conv_bwd(dcv, bcx, conv_w, name):
    s = bcx.shape[0]
    cw = bcx.shape[1] // 3
    nb = cw // LANES

    def body(dcv_ref, b_ref, c_ref, x_ref, w_ref, db_ref, dc_ref, dxc_ref, dw_ref):
        rows = lax.broadcasted_iota(jnp.int32, (s, LANES), 0)
        cv_, xv = c_ref[...], x_ref[...]
        z = cv_ * xv
        w = w_ref[...]
        z1 = _shift_down(z, 1, rows)
        z2 = _shift_down(z, 2, rows)
        y = w[2:3, :] * z + w[1:2, :] * z1 + w[0:1, :] * z2
        dcvv = dcv_ref[...]
        db_ref[...] = (dcvv * y).astype(BF16)
        dy = dcvv * b_ref[...]
        dw_ref[0:1, :] = jnp.sum(dy * z2, axis=0, keepdims=True)
        dw_ref[1:2, :] = jnp.sum(dy * z1, axis=0, keepdims=True)
        dw_ref[2:3, :] = jnp.sum(dy * z, axis=0, keepdims=True)
        dz = w[2:3, :] * dy + w[1:2, :] * _shift_up(dy, 1, rows, s) + w[0:1, :] * _shift_up(dy, 2, rows, s)
        dc_ref[...] = (dz * xv).astype(BF16)
        dxc_ref[...] = (dz * cv_).astype(BF16)

    def col(off):
        return pl.BlockSpec((s, LANES), lambda j: (0, j + off))

    wspec = pl.BlockSpec((CONV_K, LANES), lambda j: (0, j))
    db, dc, dxc, dw = _pc(body, out_shape=(_sds((s, cw), BF16),) * 3 + (_sds((CONV_K, cw), F32),), grid=(nb,),
                          in_specs=[col(0), col(0), col(nb), col(2 * nb), wspec],
                          out_specs=(col(0), col(0), col(0), wspec),
                          compiler_params=_params("parallel"), name=name)(dcv, bcx, bcx, bcx, conv_w)
    return db, dc, dxc, dw


def _group_matrix():
    idx = jnp.arange(LANES) // HEAD_DIM
    return (idx[:, None] == idx[None, :]).astype(BF16)


def _group_sum(v, gmat):
    return _dot3(v, gmat)


def _gnorm_fwd(att, cv, gg, name):
    s, a_w = att.shape
    cw = cv.shape[1]
    d = a_w + cw
    tr = _tile(s, 512, 16)
    gmat = _group_matrix()

    def body(att_ref, cv_ref, gg_ref, gm_ref, yn_ref):
        gm = gm_ref[...]
        for c0 in range(0, d, LANES):
            y = att_ref[:, c0:c0 + LANES] if c0 < a_w else cv_ref[:, c0 - a_w:c0 - a_w + LANES]
            ms = _group_sum(y * y, gm) * (1.0 / HEAD_DIM)
            yn_ref[:, c0:c0 + LANES] = (y * lax.rsqrt(ms + EPS) * gg_ref[:, c0:c0 + LANES]).astype(BF16)

    return _pc(body, out_shape=_sds((s, d), BF16), grid=(s // tr,),
               in_specs=[pl.BlockSpec((tr, a_w), lambda i: (i, 0)), pl.BlockSpec((tr, cw), lambda i: (i, 0)),
                         _vec_spec(d), pl.BlockSpec((LANES, LANES), lambda i: (0, 0))],
               out_specs=pl.BlockSpec((tr, d), lambda i: (i, 0)),
               compiler_params=_params("parallel"), name=name)(att, cv, gg, gmat)


def _gnorm_bwd(dyn, att, cv, gg, name):
    s, a_w = att.shape
    cw = cv.shape[1]
    d = a_w + cw
    tr = _tile(s, 256, 16)
    gmat = _group_matrix()

    def body(dyn_ref, att_ref, cv_ref, gg_ref, gm_ref, datt_ref, dcv_ref, dgg_ref):
        @pl.when(pl.program_id(0) == 0)
        def _():
            dgg_ref[...] = jnp.zeros_like(dgg_ref)

        gm = gm_ref[...]
        for c0 in range(0, d, LANES):
            y = att_ref[:, c0:c0 + LANES] if c0 < a_w else cv_ref[:, c0 - a_w:c0 - a_w + LANES]
            dv = dyn_ref[:, c0:c0 + LANES]
            r = lax.rsqrt(_group_sum(y * y, gm) * (1.0 / HEAD_DIM) + EPS)
            xhat = y * r
            dgg_ref[:, c0:c0 + LANES] += jnp.sum(dv * xhat, axis=0, keepdims=True)
            dxh = dv * gg_ref[:, c0:c0 + LANES]
            proj = _group_sum(dxh * xhat, gm) * (1.0 / HEAD_DIM)
            dy = r * (dxh - xhat * proj)
            if c0 < a_w:
                datt_ref[:, c0:c0 + LANES] = dy.astype(BF16)
            else:
                dcv_ref[:, c0 - a_w:c0 - a_w + LANES] = dy

    return _pc(body, out_shape=(_sds((s, a_w), BF16), _sds((s, cw), F32), _sds((1, d), F32)), grid=(s // tr,),
               in_specs=[pl.BlockSpec((tr, d), lambda i: (i, 0)), pl.BlockSpec((tr, a_w), lambda i: (i, 0)),
                         pl.BlockSpec((tr, cw), lambda i: (i, 0)), _vec_spec(d),
                         pl.BlockSpec((LANES, LANES), lambda i: (0, 0))],
               out_specs=(pl.BlockSpec((tr, a_w), lambda i: (i, 0)), pl.BlockSpec((tr, cw), lambda i: (i, 0)),
                          _vec_spec(d)),
               compiler_params=_params("arbitrary"), name=name)(dyn, att, cv, gg, gmat)


def _adamw_math(w, g, m, v):
    m_new = ADAM_B1 * m + (1.0 - ADAM_B1) * g
    v_new = ADAM_B2 * v + (1.0 - ADAM_B2) * (g * g)
    m_hat = m_new / (1.0 - ADAM_B1 ** ADAM_STEP)
    v_hat = v_new / (1.0 - ADAM_B2 ** ADAM_STEP)
    delta = -ADAM_LR * (m_hat / (jnp.sqrt(v_hat) + ADAM_EPS) + ADAM_WD * w)
    return delta, m_new, v_new


def _row_tile(r, c):
    return _tile(r, max(8, ((1 << 18) // c) // 8 * 8), 8)


def _adamw(w, g, m, v, name):
    r, c = w.shape
    tr = _row_tile(r, c)

    def body(w_ref, g_ref, m_ref, v_ref, d_ref, mo_ref, vo_ref):
        d, mn, vn = _adamw_math(w_ref[...], g_ref[...], m_ref[...], v_ref[...])
        d_ref[...] = d
        mo_ref[...] = mn
        vo_ref[...] = vn

    spec = pl.BlockSpec((tr, c), lambda i: (i, 0))
    return _pc(body, out_shape=(_sds((r, c), F32),) * 3, grid=(r // tr,), in_specs=[spec] * 4,
               out_specs=(spec,) * 3, compiler_params=_params("parallel"), name=name)(w, g, m, v)


def _adamw_halves(w, mine, theirs, m, v, core, name):
    r2, c = w.shape
    r = r2 // 2
    assert mine.shape == (r, c) and theirs.shape == (r, c)
    tr = _row_tile(r, c)
    nb = r // tr

    def body(core_ref, w_ref, a_ref, b_ref, m_ref, v_ref, g_ref, d_ref, mo_ref, vo_ref):
        g = jnp.where(pl.program_id(0) == core_ref[0], a_ref[...], b_ref[...])
        d, mn, vn = _adamw_math(w_ref[...], g, m_ref[...], v_ref[...])
        g_ref[...] = g
        d_ref[...] = d
        mo_ref[...] = mn
        vo_ref[...] = vn

    full = pl.BlockSpec((tr, c), lambda h, i, core_ref: (h * nb + i, 0))
    half = pl.BlockSpec((tr, c), lambda h, i, core_ref: (i, 0))
    grid_spec = pltpu.PrefetchScalarGridSpec(
        num_scalar_prefetch=1, grid=(2, nb), in_specs=[full, half, half, full, full], out_specs=(full,) * 4)
    return _pc(body, out_shape=(_sds((r2, c), F32),) * 4, grid_spec=grid_spec,
               compiler_params=_params("parallel", "parallel"), name=name)(core, w, mine, theirs, m, v)


def _ada_fwd(c16, ada_w, ada_b, name):
    d, n = ada_w.shape
    tn = _tile(n, 768, LANES)

    def body(c_ref, w_ref, b_ref, o_ref):
        cv = c_ref[...]
        sc = (cv * jax.nn.sigmoid(cv)).astype(BF16)
        o_ref[...] = lax.dot_general(sc, w_ref[...].astype(BF16), _NN, preferred_element_type=F32) + b_ref[...]

    return _pc(body, out_shape=_sds((16, n), F32), grid=(n // tn,),
               in_specs=[pl.BlockSpec((16, d), lambda j: (0, 0)), pl.BlockSpec((d, tn), lambda j: (0, j)),
                         pl.BlockSpec((1, tn), lambda j: (0, j))],
               out_specs=pl.BlockSpec((16, tn), lambda j: (0, j)),
               compiler_params=_params("parallel"), name=name)(c16, ada_w, ada_b)


def _ada_update(c16_t, dmod16, w, m, v, name, exchange=None):
    r, c = w.shape
    tr = _row_tile(r, c)

    def body(c_ref, dm_ref, w_ref, m_ref, v_ref, g_ref, d_ref, mo_ref, vo_ref):
        cv = c_ref[...]
        sc = (cv * jax.nn.sigmoid(cv)).astype(BF16)
        g = lax.dot_general(sc, dm_ref[...].astype(BF16), _NN, preferred_element_type=F32)
        d, mn, vn = _adamw_math(w_ref[...], g, m_ref[...], v_ref[...])
        g_ref[...] = g
        d_ref[...] = d
        mo_ref[...] = mn
        vo_ref[...] = vn

    spec = pl.BlockSpec((tr, c), lambda i: (i, 0))
    return _pc(body, exchange, out_shape=(_sds((r, c), F32),) * 4, grid=(r // tr,),
               in_specs=[pl.BlockSpec((tr, 16), lambda i: (i, 0)), pl.BlockSpec((16, c), lambda i: (0, 0)),
                         spec, spec, spec],
               out_specs=(spec,) * 4, compiler_params=_params("parallel"), name=name)(c16_t, dmod16, w, m, v)


def _add_half(dw, recv, core, name):
    _, _, r, w = dw.shape
    tr = _tile(r, 512, 16)

    def body(core_ref, a_ref, b_ref, o_ref):
        o_ref[...] = (a_ref[...].astype(F32) + b_ref[...].astype(F32)).astype(BF16)

    grid_spec = pltpu.PrefetchScalarGridSpec(
        num_scalar_prefetch=1, grid=(N_CHIPS, r // tr),
        in_specs=[pl.BlockSpec((None, None, tr, w), lambda s, i, core_ref: (s, core_ref[0], i, 0)),
                  pl.BlockSpec((None, tr, w), lambda s, i, core_ref: (s, i, 0))],
        out_specs=pl.BlockSpec((None, tr, w), lambda s, i, core_ref: (s, i, 0)))
    return _pc(body, out_shape=_sds((N_CHIPS, r, w), BF16), grid_spec=grid_spec,
               compiler_params=_params("parallel", "parallel"), name=name)(core, dw, recv)


def _sum_chips(own, recv, chip, name):
    _, r, w = own.shape
    tr = _tile(r, 512, 16)

    def body(chip_ref, own_ref, p_ref, o_ref):
        acc = own_ref[...].astype(F32)
        for q in range(N_CHIPS - 1):
            acc = acc + p_ref[q].astype(F32)
        o_ref[...] = acc

    grid_spec = pltpu.PrefetchScalarGridSpec(
        num_scalar_prefetch=1, grid=(r // tr,),
        in_specs=[pl.BlockSpec((None, tr, w), lambda i, chip_ref: (chip_ref[0], i, 0)),
                  pl.BlockSpec((N_CHIPS - 1, tr, w), lambda i, chip_ref: (0, i, 0))],
        out_specs=pl.BlockSpec((tr, w), lambda i, chip_ref: (i, 0)))
    return _pc(body, out_shape=_sds((r, w), F32), grid_spec=grid_spec,
               compiler_params=_params("parallel"), name=name)(chip, own, recv)


def _sum_devices(parts, name):
    nd, r, w = parts.shape

    def body(p_ref, o_ref):
        acc = p_ref[0]
        for q in range(1, nd):
            acc = acc + p_ref[q]
        o_ref[...] = acc

    return _pc(body, out_shape=_sds((r, w), F32), name=name)(parts)


def _place():
    x, y, c = lax.axis_index("x"), lax.axis_index("y"), lax.axis_index("c")
    chips = [(1 - x, y), (x, 1 - y), (1 - x, 1 - y)]
    return x, y, c, chips


def _all_gather_small(blk, name):
    r, w = blk.shape

    def body(x_ref, out_ref, send_sems, recv_sems, local_sem):
        x, y, c, chips = _place()
        me, sibling = (x, y, c), (x, y, 1 - c)

        def rows(px, py, pc):
            return out_ref.at[pl.ds((4 * px + 2 * py + pc) * r, r), :]

        def copy(k, block, to, src=None):
            return pltpu.make_async_remote_copy(
                src_ref=rows(*block) if src is None else src, dst_ref=rows(*block),
                send_sem=send_sems.at[k], recv_sem=recv_sems.at[k], device_id=to, device_id_type=MESH)

        mine = pltpu.make_async_copy(x_ref, rows(*me), local_sem)
        mine.start()
        first = [copy(0, me, sibling, src=x_ref)]
        first += [copy(1 + j, me, (*chip, c), src=x_ref) for j, chip in enumerate(chips)]
        for cp in first:
            cp.start()
        passed = [copy(4 + j, (*chip, c), sibling) for j, chip in enumerate(chips)]
        for j, chip in enumerate(chips):
            copy(1 + j, (*chip, c), me).wait_recv()
            passed[j].start()
        copy(0, sibling, me).wait_recv()
        for j, chip in enumerate(chips):
            copy(4 + j, (*chip, 1 - c), me).wait_recv()
        for cp in first + passed:
            cp.wait_send()
        mine.wait()

    return _pc(body, out_shape=_sds((N_DEV * r, w), blk.dtype),
               in_specs=[pl.BlockSpec(memory_space=pltpu.VMEM)], out_specs=pl.BlockSpec(memory_space=pltpu.VMEM),
               scratch_shapes=[pltpu.SemaphoreType.DMA((7,)), pltpu.SemaphoreType.DMA((7,)), pltpu.SemaphoreType.DMA],
               name=name)(blk)


def _remote(src, dst, send_sems, recv_sems, k, to):
    return pltpu.make_async_remote_copy(src_ref=src, dst_ref=dst, send_sem=send_sems.at[k], recv_sem=recv_sems.at[k],
                                        device_id=to, device_id_type=MESH)


def _exchange_of(inputs, out_shapes, n_sems, copies, aliases=None):
    def start(src, dst, send_sems, recv_sems):
        for cp in copies(src, dst, send_sems, recv_sems)[0]:
            cp.start()

    def finish(src, dst, send_sems, recv_sems):
        sends, arrivals = copies(src, dst, send_sems, recv_sems)
        for cp in arrivals:
            cp.wait_recv()
        for cp in sends:
            cp.wait_send()

    return _Exchange(inputs, out_shapes, n_sems, start, finish, aliases)


def _run_exchange(ex, name):
    n_in, n_out = len(ex.inputs), len(ex.out_shapes)

    def body(*refs):
        src, dst = refs[:n_in], refs[n_in:n_in + n_out]
        send_sems, recv_sems = refs[n_in + n_out:]
        ex.start(src, dst, send_sems, recv_sems)
        ex.finish(src, dst, send_sems, recv_sems)

    ex.set_results(pl.pallas_call(
        body, out_shape=tuple(ex.out_shapes), in_specs=[_ANY] * n_in, out_specs=(_ANY,) * n_out,
        scratch_shapes=[pltpu.SemaphoreType.DMA((ex.n_sems,)), pltpu.SemaphoreType.DMA((ex.n_sems,))],
        input_output_aliases=ex.aliases, name=name)(*ex.inputs))


def _gather_ici_exchange(shards):
    n = len(shards)

    def copies(own, out, send_sems, recv_sems):
        x, y, c, chips = _place()
        my_chip = 2 * x + y
        sends, arrivals = [], []
        for i in range(n):
            for j, chip in enumerate(chips):
                to = (*chip, c)
                sends.append(_remote(own[i].at[c], out[i].at[my_chip, c], send_sems, recv_sems, 4 * i + j, to))
                arrivals.append(_remote(own[i].at[c], out[i].at[2 * chip[0] + chip[1], c], send_sems, recv_sems, 4 * i + j, to))
            whole = _remote(own[i], out[i].at[my_chip], send_sems, recv_sems, 4 * i + 3, (x, y, 1 - c))
            sends.append(whole)
            arrivals.append(whole)
        return sends, arrivals

    return _exchange_of(shards, [_sds((N_CHIPS,) + s.shape, s.dtype) for s in shards], 4 * n, copies)


def _gather_pass_exchange(gathered):
    n = len(gathered)

    def copies(src, dst, send_sems, recv_sems):
        x, y, c, chips = _place()
        sends, arrivals = [], []
        for i in range(n):
            for j, chip in enumerate(chips):
                idx = 2 * chip[0] + chip[1]
                sends.append(_remote(src[i].at[idx, c], dst[i].at[idx, c], send_sems, recv_sems, 3 * i + j, (x, y, 1 - c)))
                arrivals.append(_remote(src[i].at[idx, c], dst[i].at[idx, 1 - c], send_sems, recv_sems, 3 * i + j, (x, y, 1 - c)))
        return sends, arrivals

    return _exchange_of(gathered, [_sds(g.shape, g.dtype) for g in gathered], 3 * n, copies,
                        aliases={i: i for i in range(n)})


def _reduce_sibling_exchange(grads):
    n = len(grads)

    def copies(src, dst, send_sems, recv_sems):
        x, y, c, _ = _place()
        both = [_remote(src[i].at[s, 1 - c], dst[i].at[s], send_sems, recv_sems, N_CHIPS * i + s, (x, y, 1 - c))
                for i in range(n) for s in range(N_CHIPS)]
        return both, both

    return _exchange_of(grads, [_sds((N_CHIPS,) + g.shape[2:], g.dtype) for g in grads], N_CHIPS * n, copies)


def _reduce_chips_exchange(parts):
    n = len(parts)

    def copies(src, dst, send_sems, recv_sems):
        x, y, c, chips = _place()
        both = [_remote(src[i].at[2 * chip[0] + chip[1]], dst[i].at[j], send_sems, recv_sems, 3 * i + j, (*chip, c))
                for i in range(n) for j, chip in enumerate(chips)]
        return both, both

    return _exchange_of(parts, [_sds((N_CHIPS - 1,) + p.shape[1:], p.dtype) for p in parts], 3 * n, copies)


def _share_exchange(halves):
    n = len(halves)

    def copies(src, dst, send_sems, recv_sems):
        x, y, c, _ = _place()
        both = [_remote(src[i], dst[i], send_sems, recv_sems, i, (x, y, 1 - c)) for i in range(n)]
        return both, both

    return _exchange_of(halves, [_sds(h.shape, h.dtype) for h in halves], n, copies)


HEAD_ROWS = 16


class _WeightTraffic:
    def __init__(self, shards, core, chip):
        self.shards, self.core, self.chip = shards, core, chip
        self.gather, self.grads, self.reduce, self.chip_sums, self.half_sums, self.shared = {}, {}, {}, {}, {}, {}

    def gather_ici(self, grp):
        self.gather[grp] = _gather_ici_exchange(self.shards[grp])
        return self.gather[grp]

    def gather_pass(self, grp):
        self.gather[grp] = _gather_pass_exchange(self.gather[grp].results)
        return self.gather[grp]

    def weights(self, grp):
        return [g.reshape(-1, g.shape[-1]) for g in self.gather[grp].results]

    def reduce_sibling(self, grp, grads):
        self.grads[grp] = [g.reshape(N_CHIPS, 2, g.shape[0] // (2 * N_CHIPS), g.shape[1]) for g in grads]
        self.reduce[grp] = _reduce_sibling_exchange(self.grads[grp])
        return self.reduce[grp]

    def add_halves(self, grp):
        self.chip_sums[grp] = [_add_half(g, r, self.core, "add_half_%s%d" % (grp, i))
                               for i, (g, r) in enumerate(zip(self.grads[grp], self.reduce[grp].results))]

    def reduce_chips(self, grp):
        self.reduce[grp] = _reduce_chips_exchange(self.chip_sums[grp])
        return self.reduce[grp]

    def sum_chips(self, grp):
        self.half_sums[grp] = [_sum_chips(o, p, self.chip, "sum_chips_%s%d" % (grp, i))
                               for i, (o, p) in enumerate(zip(self.chip_sums[grp], self.reduce[grp].results))]

    def share(self, grp):
        self.shared[grp] = _share_exchange(self.half_sums[grp])
        return self.shared[grp]

    def totals(self, grp):
        return list(zip(self.half_sums[grp], self.shared[grp].results))


def _ffn_fwd(x, norm_g, shift, scale, gate, wg_t, wu_t, wd, tag, up_exchange=None, down_exchange=None):
    h = _norm_mod_fwd(x, norm_g, shift, scale, tag + "_norm_fwd")
    a, u, hid = _ffn_up(h, wg_t, wu_t, tag + "_up", exchange=up_exchange)
    wd = wd() if callable(wd) else wd
    x_out, f = _mm(hid, wd, "nn", F32, tag + "_down", res=x, gate=gate, aux_dtype=BF16,
                   exchange=down_exchange() if down_exchange else None)
    return x_out, (h, a, u, hid, f)


def _ffn_bwd(dx_out, df, x, saved, norm_g, scale, wg_t, wu_t, wd, tag, traffic, below=None, dact_exchange=None,
             dw_exchange=None, finish_reduction=False):
    h, a, u, hid, _ = saved
    f_below, gate_below = below if below else (None, None)
    da, du = _ffn_dact(df, wd, a, u, tag + "_dact", exchange=dact_exchange)
    dwd = _mm(hid, df, "tn", BF16, tag + "_dwd", exchange=dw_exchange() if dw_exchange else None)
    if not finish_reduction:
        dwg_t = _mm(da, h, "tn", BF16, tag + "_dwg")
        dwu_t = _mm(du, h, "tn", BF16, tag + "_dwu")
        dh = _mm([da, du], [wg_t, wu_t], "nn", F32, tag + "_dh", exchange=traffic.reduce_sibling(tag, [dwg_t, dwu_t, dwd]))
        traffic.add_halves(tag)
        dx, dshift, dscale, dnorm_g, *gated = _norm_mod_bwd(dh, x, norm_g, scale, dx_out, tag + "_norm_bwd",
                                                            f=f_below, gate=gate_below)
        return dx, (dshift, dscale, dnorm_g), gated
    kd, kg, ku = tag + "_wd", tag + "_wg", tag + "_wu"
    dwg_t = _mm(da, h, "tn", BF16, tag + "_dwg", exchange=traffic.reduce_sibling(kd, [dwd]))
    traffic.add_halves(kd)
    dwu_t = _mm(du, h, "tn", BF16, tag + "_dwu",
                exchange=_join(traffic.reduce_chips(kd), traffic.reduce_sibling(kg, [dwg_t])))
    traffic.add_halves(kg)
    dh = _mm([da, du], [wg_t, wu_t], "nn", F32, tag + "_dh",
             exchange=_join(traffic.reduce_chips(kg), traffic.reduce_sibling(ku, [dwu_t])))
    traffic.add_halves(ku)
    traffic.sum_chips(kd)
    traffic.sum_chips(kg)
    dx, dshift, dscale, dnorm_g, *gated = _norm_mod_bwd(
        dh, x, norm_g, scale, dx_out, tag + "_norm_bwd", f=f_below, gate=gate_below,
        exchange=_join(traffic.reduce_chips(ku), traffic.share(kd), traffic.share(kg)))
    traffic.sum_chips(ku)
    return dx, (dshift, dscale, dnorm_g), gated


def _layer_step(x, target, mod, gains, forget_bias, conv_w, traffic, att_w, in_shard, in_rows):
    sh1, sc1, g1, sh2, sc2, g2, sh3, sc3, g3 = mod
    norm1_g, norm2_g, norm3_g, final_g, group_g = gains
    s, d = x.shape
    n_heads = att_w // HEAD_DIM
    npair = n_heads // 2
    gate1, gate3 = 0.5 * g1, 0.5 * g3

    def split_w_in(w_in_pad):
        w_in_t = w_in_pad.reshape(N_CHIPS, in_rows, d)[:, :in_shard].reshape(N_CHIPS * in_shard, d)
        return (w_in_t[:3 * att_w], _pad_rows(w_in_t[3 * att_w:3 * att_w + n_heads], LANES), w_in_t[3 * att_w + n_heads:])

    _run_exchange(traffic.gather_ici("ffn1_gu"), "gather_ffn1_ici")
    _run_exchange(traffic.gather_pass("ffn1_gu"), "gather_ffn1_pass")
    wg1_t, wu1_t = traffic.weights("ffn1_gu")

    def wd1_ready():
        _run_exchange(traffic.gather_pass("ffn1_d"), "gather_ffn1_down_pass")
        return traffic.weights("ffn1_d")[0]

    x1, saved1 = _ffn_fwd(x, norm1_g, sh1, sc1, gate1, wg1_t, wu1_t, wd1_ready, "ffn1",
                          up_exchange=_join(traffic.gather_ici("ffn1_d"), traffic.gather_ici("mix")),
                          down_exchange=lambda: traffic.gather_pass("mix"))
    wd1 = traffic.weights("ffn1_d")[0]
    w_in_pad, w_out = traffic.weights("mix")
    wqkv_t, wf_t, wbcx_t = split_w_in(w_in_pad)

    h2 = _norm_mod_fwd(x1, norm2_g, sh2, sc2, "mix_norm_fwd")
    qkv = _mm(h2, wqkv_t, "nt", BF16, "mix_proj_qkv")
    bcx = _mm(h2, wbcx_t, "nt", F32, "mix_proj_bcx")
    flog = _mm(h2, wf_t, "nt", F32, "mix_proj_f")
    flog_t = jnp.pad(flog[:, :n_heads].T, ((0, HEAD_ROWS - n_heads), (0, 0)))
    bias_col = jnp.pad(forget_bias, (0, HEAD_ROWS - n_heads))[:, None]
    f_pieces = _forget_fwd(flog_t, bias_col, "forget_fwd")
    qa, ka, va = _attn_prep(qkv, f_pieces, "attn_prep")
    att, lse = _attn_fwd(qa, ka, va, "attn_fwd", exchange=traffic.gather_ici("ffn2"))
    cv = _conv_fwd(bcx, conv_w, "conv_fwd")
    yn = _gnorm_fwd(att, cv, group_g, "gnorm_fwd")
    x2, mix = _mm(yn, w_out, "nn", F32, "mix_out", res=x1, gate=g2, aux_dtype=BF16, exchange=traffic.gather_pass("ffn2"))
    wg2_t, wu2_t, wd2 = traffic.weights("ffn2")

    x3, saved3 = _ffn_fwd(x2, norm3_g, sh3, sc3, gate3, wg2_t, wu2_t, wd2, "ffn2")

    dx3, loss_row, dfinal_g, df2, dgate3 = _final_loss(x3, final_g, target, saved3[4], gate3, "final_loss")

    dx2, (dsh3, dsc3, dnorm3_g), (dmix, dg2) = _ffn_bwd(
        dx3, df2, x2, saved3, norm3_g, sc3, wg2_t, wu2_t, wd2, "ffn2", traffic, below=(mix, g2))
    dyn = _mm(dmix, w_out, "nt", F32, "mix_out_dyn")
    dw_out = _mm(yn, dmix, "tn", BF16, "mix_out_dw")
    datt, dcv, dgroup_g = _gnorm_bwd(dyn, att, cv, group_g, "gnorm_bwd")
    db, dc, dxc, dconv_w = _conv_bwd(dcv, bcx, conv_w, "conv_bwd")
    dbcx = jnp.concatenate([db, dc, dxc], axis=1)
    dq, dk, dv, qx, kx = _attn_bwd(qa, ka, va, datt, att, lse, "attn_bwd", exchange=traffic.reduce_chips("ffn2"))
    traffic.sum_chips("ffn2")
    dqkv = jnp.concatenate([dq.astype(BF16), dk, dv], axis=1)
    df_t = _decay_grads(qx, kx, "decay_grads")[:, :HEAD_ROWS].T
    dflog_t, dbias_col = _forget_bwd(df_t, flog_t, bias_col, "forget_bwd")
    dflog = jnp.pad(dflog_t[:n_heads].T, ((0, 0), (0, LANES - n_heads))).astype(BF16)
    dh2 = _mm([dqkv, dbcx, dflog], [wqkv_t, wbcx_t, wf_t], "nn", F32, "mix_dh", exchange=traffic.share("ffn2"))
    dwqkv_t = _mm(dqkv, h2, "tn", BF16, "mix_dw_qkv")
    dwbcx_t = _mm(dbcx, h2, "tn", BF16, "mix_dw_bcx")
    dwf_t = _mm(dflog, h2, "tn", BF16, "mix_dw_f")
    dw_in_t = jnp.concatenate([dwqkv_t, dwf_t[:n_heads], dwbcx_t], axis=0).reshape(N_CHIPS, in_shard, d)
    dw_in_t = jnp.pad(dw_in_t, ((0, 0), (0, in_rows - in_shard), (0, 0))).reshape(N_CHIPS * in_rows, d)
    dx1, dsh2, dsc2, dnorm2_g, df1, dgate1 = _norm_mod_bwd(
        dh2, x1, norm2_g, sc2, dx2, "mix_norm_bwd", f=saved1[4], gate=gate1,
        exchange=traffic.reduce_sibling("mix", [dw_in_t, dw_out]))
    traffic.add_halves("mix")

    def share_mix():
        traffic.sum_chips("mix")
        return traffic.share("mix")

    dx, (dsh1, dsc1, dnorm1_g), _ = _ffn_bwd(
        dx1, df1, x, saved1, norm1_g, sc1, wg1_t, wu1_t, wd1, "ffn1", traffic,
        dact_exchange=traffic.reduce_chips("mix"), dw_exchange=share_mix, finish_reduction=True)

    dmod = [dsh1, dsc1, 0.5 * dgate1, dsh2, dsc2, dg2, dsh3, dsc3, 0.5 * dgate3]
    dgains = [dnorm1_g, dnorm2_g, dnorm3_g, dfinal_g, dgroup_g]
    dbias = dbias_col[:n_heads, 0]
    return dx, loss_row, dmod, dgains, dbias, dconv_w


SMALL_ROWS = 24
ROW_GAINS, ROW_LOSS, ROW_FORGET, ROW_CONV, ROW_MOD = 0, 5, 6, 7, 10
PROW_ADA_B, PROW_GAINS, PROW_FORGET, PROW_CONV = 0, 9, 14, 15


def _round_up(n, m):
    return -(-n // m) * m


def _pad_rows(a, rows):
    return jnp.pad(a, ((0, rows - a.shape[0]), (0, 0)))


def _halves(a):
    return a.reshape(2, a.shape[0] // 2, a.shape[1])


def _rows_at(a, r0, total, width):
    return jnp.pad(a, ((r0, total - r0 - a.shape[0]), (0, width - a.shape[1])))


def kernel(x, c, ada_w, ada_b, norm1_g, ffn1_w_gate, ffn1_w_up, ffn1_w_down, norm2_g, w_in, forget_bias, conv_w, group_norm_g, w_out, norm3_g, ffn2_w_gate, ffn2_w_up, ffn2_w_down, final_g, loss_target, m_ada_w, m_ada_b, m_norm1_g, m_ffn1_w_gate, m_ffn1_w_up, m_ffn1_w_down, m_norm2_g, m_w_in, m_forget_bias, m_conv_w, m_group_norm_g, m_w_out, m_norm3_g, m_ffn2_w_gate, m_ffn2_w_up, m_ffn2_w_down, m_final_g, v_ada_w, v_ada_b, v_norm1_g, v_ffn1_w_gate, v_ffn1_w_up, v_ffn1_w_down, v_norm2_g, v_w_in, v_forget_bias, v_conv_w, v_group_norm_g, v_w_out, v_norm3_g, v_ffn2_w_gate, v_ffn2_w_up, v_ffn2_w_down, v_final_g):
    xi, yi, ci = lax.axis_index("x"), lax.axis_index("y"), lax.axis_index("c")
    chip = 2 * xi + yi
    dev = 4 * xi + 2 * yi + ci
    _, s, d = x.shape
    att_w = d // 2
    conv_width = d - att_w
    n_heads = att_w // HEAD_DIM
    in_shard = w_in.shape[1]
    in_rows = _round_up(in_shard, 32)
    cs = conv_w.shape[1]
    mod_shard = ada_w.shape[1]
    assert N_MOD * d == N_CHIPS * mod_shard and conv_width == N_CHIPS * cs and n_heads % 2 == 0

    pack0 = _rows_at(c, 0, 8, d) + _rows_at(conv_w, 1, 8, d)
    got0 = _all_gather_small(pack0, "gather_cond").reshape(N_DEV, 8, d)
    c16 = _pad_rows(got0[:, 0, :], 16)
    conv_full = got0[0::2, 1:1 + CONV_K, :cs].transpose(1, 0, 2).reshape(CONV_K, conv_width)

    ada_b_mine = lax.dynamic_slice(ada_b, (chip * mod_shard,), (mod_shard,))[None, :]
    mod_part = _ada_fwd(c16, ada_w, ada_b_mine, "ada_fwd")
    got1 = _all_gather_small(mod_part, "gather_mod").reshape(N_DEV, 16, mod_shard)
    mod_mine = lax.dynamic_index_in_dim(got1[0::2], dev, axis=1, keepdims=False).reshape(N_MOD, d)
    mod = [mod_mine[i:i + 1] for i in range(N_MOD)]

    def t_bf(w):
        return w.T.astype(BF16)

    shards = {"ffn1_gu": [_halves(t_bf(ffn1_w_gate)), _halves(t_bf(ffn1_w_up))], "ffn1_d": [_halves(ffn1_w_down.astype(BF16))],
              "mix": [_halves(_pad_rows(t_bf(w_in), in_rows)), _halves(w_out.astype(BF16))],
              "ffn2": [_halves(t_bf(ffn2_w_gate)), _halves(t_bf(ffn2_w_up)), _halves(ffn2_w_down.astype(BF16))]}
    core = ci.astype(jnp.int32).reshape(1)
    chip_arr = chip.astype(jnp.int32).reshape(1)
    traffic = _WeightTraffic(shards, core, chip_arr)

    gains = [g[None, :] for g in (norm1_g, norm2_g, norm3_g, final_g, group_norm_g)]
    dx, loss_row, dmod, dgains, dbias, dconv_w = _layer_step(
        x[0], loss_target[0], mod, gains, forget_bias, conv_full, traffic, att_w, in_shard, in_rows)

    pack = sum(_rows_at(g, ROW_GAINS + i, SMALL_ROWS, d) for i, g in enumerate(dgains))
    pack += _rows_at(loss_row, ROW_LOSS, SMALL_ROWS, d) + _rows_at(dbias[None, :], ROW_FORGET, SMALL_ROWS, d)
    pack += _rows_at(dconv_w, ROW_CONV, SMALL_ROWS, d)
    pack += sum(_rows_at(g, ROW_MOD + i, SMALL_ROWS, d) for i, g in enumerate(dmod))
    got2 = _all_gather_small(pack, "gather_small_grads").reshape(N_DEV, SMALL_ROWS, d)
    tot = _sum_devices(got2, "sum_small_grads")
    loss = tot[ROW_LOSS, 0]
    grad_ada_b = tot[ROW_MOD:ROW_MOD + N_MOD].reshape(N_MOD * d)
    grad_conv = lax.dynamic_slice(tot[ROW_CONV:ROW_CONV + CONV_K], (0, chip * cs), (CONV_K, cs))
    dmod_all = got2[:, ROW_MOD:ROW_MOD + N_MOD, :].reshape(N_DEV, N_MOD * d)
    dmod16 = _pad_rows(lax.dynamic_slice(dmod_all, (0, chip * mod_shard), (N_DEV, mod_shard)), 16)

    out = {"ada_w": tuple(_ada_update(c16.T, dmod16, ada_w, m_ada_w, v_ada_w, "adamw_ada_w", exchange=traffic.share("ffn1_wu")))}
    totals = (traffic.totals("ffn1_wg") + traffic.totals("ffn1_wu") + traffic.totals("ffn1_wd")
              + traffic.totals("mix") + traffic.totals("ffn2"))

    names = ("ffn1_w_gate", "ffn1_w_up", "ffn1_w_down", "w_in", "w_out", "ffn2_w_gate", "ffn2_w_up", "ffn2_w_down")
    transposed = ("ffn1_w_gate", "ffn1_w_up", "w_in", "ffn2_w_gate", "ffn2_w_up")
    params = {"ffn1_w_gate": (ffn1_w_gate, m_ffn1_w_gate, v_ffn1_w_gate), "ffn1_w_up": (ffn1_w_up, m_ffn1_w_up, v_ffn1_w_up),
              "ffn1_w_down": (ffn1_w_down, m_ffn1_w_down, v_ffn1_w_down), "w_in": (w_in, m_w_in, v_w_in),
              "w_out": (w_out, m_w_out, v_w_out), "ffn2_w_gate": (ffn2_w_gate, m_ffn2_w_gate, v_ffn2_w_gate),
              "ffn2_w_up": (ffn2_w_up, m_ffn2_w_up, v_ffn2_w_up), "ffn2_w_down": (ffn2_w_down, m_ffn2_w_down, v_ffn2_w_down)}
    for name_, (mine, theirs) in zip(names, totals):
        w, m, v = params[name_]
        if name_ in transposed:
            w, m, v = w.T, m.T, v.T
        if name_ == "w_in":
            both = jnp.where(ci == 0, jnp.concatenate([mine, theirs]), jnp.concatenate([theirs, mine]))[:in_shard]
            res = (both,) + tuple(_adamw(w, both, m, v, "adamw_" + name_))
        else:
            res = _adamw_halves(w, mine, theirs, m, v, core, "adamw_" + name_)
        out[name_] = tuple(r.T for r in res) if name_ in transposed else tuple(res)

    def small_pack(ada_b_, gains_, forget_, conv_):
        p = _rows_at(ada_b_.reshape(N_MOD, d), PROW_ADA_B, SMALL_ROWS, d)
        p += sum(_rows_at(g[None, :], PROW_GAINS + i, SMALL_ROWS, d) for i, g in enumerate(gains_))
        p += _rows_at(forget_[None, :], PROW_FORGET, SMALL_ROWS, d) + _rows_at(conv_, PROW_CONV, SMALL_ROWS, d)
        return p

    g_gains = [tot[ROW_GAINS + i] for i in range(5)]
    g_forget = tot[ROW_FORGET, :n_heads]
    sw = small_pack(ada_b, (norm1_g, norm2_g, norm3_g, final_g, group_norm_g), forget_bias, conv_w)
    sm = small_pack(m_ada_b, (m_norm1_g, m_norm2_g, m_norm3_g, m_final_g, m_group_norm_g), m_forget_bias, m_conv_w)
    sv = small_pack(v_ada_b, (v_norm1_g, v_norm2_g, v_norm3_g, v_final_g, v_group_norm_g), v_forget_bias, v_conv_w)
    sg = small_pack(grad_ada_b, g_gains, g_forget, grad_conv)
    small = (sg,) + tuple(_adamw(sw, sg, sm, sv, "adamw_small"))

    def unpack(p):
        r = {"ada_b": p[PROW_ADA_B:PROW_ADA_B + N_MOD].reshape(N_MOD * d), "forget_bias": p[PROW_FORGET, :n_heads],
             "conv_w": p[PROW_CONV:PROW_CONV + CONV_K, :cs]}
        for i, nm in enumerate(("norm1_g", "norm2_g", "norm3_g", "final_g", "group_norm_g")):
            r[nm] = p[PROW_GAINS + i]
        return r

    small = [unpack(p) for p in small]
    order = ("ada_w", "ada_b", "norm1_g", "ffn1_w_gate", "ffn1_w_up", "ffn1_w_down", "norm2_g", "w_in", "forget_bias",
             "conv_w", "group_norm_g", "w_out", "norm3_g", "ffn2_w_gate", "ffn2_w_up", "ffn2_w_down", "final_g")
    result = [loss, dx[None]]
    for k in range(4):
        result += [out[nm][k] if nm in out else small[k][nm] for nm in order]
    return tuple(result)
```

```python
import functools
import math

import jax
import jax.numpy as jnp
from jax import lax
from jax.experimental import pallas as pl
from jax.experimental.pallas import tpu as pltpu

F32 = jnp.float32
BF16 = jnp.bfloat16

HEAD_DIM = 64
CONV_K = 3
N_MOD = 9
EPS = 1e-6
ADAM_LR = 0.001
ADAM_B1 = 0.9
ADAM_B2 = 0.999
ADAM_EPS = 1e-08
ADAM_WD = 0.01
ADAM_STEP = 10

LANES = 128
N_CHIPS = 4
N_DEV = 8
VMEM_LIMIT_BYTES = 56 * 1024 * 1024
MAX_CONTRACTION = 4096
NEG_BIG = -1e30
MESH = pl.DeviceIdType.MESH

_NT = (((1,), (1,)), ((), ()))
_NN = (((1,), (0,)), ((), ()))
_TN = (((0,), (0,)), ((), ()))


def _params(*sem):
    return pltpu.CompilerParams(dimension_semantics=sem, vmem_limit_bytes=VMEM_LIMIT_BYTES)


class _Exchange:
    def __init__(self, inputs, out_shapes, n_sems, start, finish, aliases=None):
        self.inputs, self.out_shapes, self.n_sems = list(inputs), list(out_shapes), n_sems
        self.start, self.finish, self.aliases = start, finish, dict(aliases or {})
        self.results = None

    def set_results(self, results):
        self.results = list(results)


class _SemaphoreWindow:
    def __init__(self, sems, base):
        self.sems, self.base = sems, base
        self.at = self

    def __getitem__(self, k):
        return self.sems.at[self.base + k]


class _JoinedExchange(_Exchange):
    def __init__(self, parts):
        self.parts = parts
        aliases, i0, o0 = {}, 0, 0
        for p in parts:
            aliases.update({i0 + a: o0 + b for a, b in p.aliases.items()})
            i0, o0 = i0 + len(p.inputs), o0 + len(p.out_shapes)

        def each(method, src, dst, send_sems, recv_sems):
            i0 = o0 = s0 = 0
            for p in parts:
                i1, o1 = i0 + len(p.inputs), o0 + len(p.out_shapes)
                getattr(p, method)(src[i0:i1], dst[o0:o1], _SemaphoreWindow(send_sems, s0), _SemaphoreWindow(recv_sems, s0))
                i0, o0, s0 = i1, o1, s0 + p.n_sems

        super().__init__([a for p in parts for a in p.inputs], [o for p in parts for o in p.out_shapes],
                         sum(p.n_sems for p in parts), functools.partial(each, "start"), functools.partial(each, "finish"),
                         aliases)

    def set_results(self, results):
        o0 = 0
        for p in self.parts:
            p.set_results(results[o0:o0 + len(p.out_shapes)])
            o0 += len(p.out_shapes)


def _join(*parts):
    return parts[0] if len(parts) == 1 else _JoinedExchange(list(parts))


def _pc(body, exchange=None, **kw):
    if exchange is None:
        return pl.pallas_call(body, **kw)
    grid = kw["grid"]
    single = not isinstance(kw["out_shape"], (tuple, list))
    out_shape = [kw["out_shape"]] if single else list(kw["out_shape"])
    out_specs = [kw["out_specs"]] if single else list(kw["out_specs"])
    in_specs = list(kw["in_specs"])
    scratch = list(kw.get("scratch_shapes", ()))
    n_in, n_out, n_scr = len(in_specs), len(out_shape), len(scratch)
    n_xi, n_xo = len(exchange.inputs), len(exchange.out_shapes)

    def wrapped(*refs):
        pos = [n_in, n_in + n_xi, n_in + n_xi + n_out, n_in + n_xi + n_out + n_xo]
        ins, x_in, outs, x_out = refs[:pos[0]], refs[pos[0]:pos[1]], refs[pos[1]:pos[2]], refs[pos[2]:pos[3]]
        scr = refs[pos[3]:pos[3] + n_scr]
        send_sems, recv_sems = refs[pos[3] + n_scr:]
        ids = [pl.program_id(a) for a in range(len(grid))]
        first = functools.reduce(jnp.logical_and, [i == 0 for i in ids])
        last = functools.reduce(jnp.logical_and, [i == g - 1 for i, g in zip(ids, grid)])

        @pl.when(first)
        def _():
            exchange.start(x_in, x_out, send_sems, recv_sems)

        body(*ins, *outs, *scr)

        @pl.when(last)
        def _():
            exchange.finish(x_in, x_out, send_sems, recv_sems)

    call = pl.pallas_call(
        wrapped, out_shape=tuple(out_shape) + tuple(exchange.out_shapes), grid=grid,
        in_specs=in_specs + [_ANY] * n_xi, out_specs=tuple(out_specs) + (_ANY,) * n_xo,
        scratch_shapes=scratch + [pltpu.SemaphoreType.DMA((exchange.n_sems,)), pltpu.SemaphoreType.DMA((exchange.n_sems,))],
        input_output_aliases={n_in + a: n_out + b for a, b in exchange.aliases.items()},
        compiler_params=_params(*(["arbitrary"] * len(grid))), name=kw["name"])

    def run(*args):
        res = call(*args, *exchange.inputs)
        exchange.set_results(res[n_out:])
        return res[0] if single else tuple(res[:n_out])

    return run


_ANY = pl.BlockSpec(memory_space=pl.ANY)


def _tile(n, pref, mult):
    best = None
    t = mult
    while t <= min(n, pref):
        if n % t == 0:
            best = t
        t += mult
    return n if best is None else best


def _sds(shape, dtype):
    return jax.ShapeDtypeStruct(shape, dtype)


def _vec_spec(d):
    return pl.BlockSpec((1, d), lambda *_: (0, 0))


def _norm_mod_fwd(x, g, shift, scale, name):
    s, d = x.shape
    tr = _tile(s, 512, 16)

    def body(x_ref, g_ref, sh_ref, sc_ref, h_ref):
        xv = x_ref[...]
        rstd = lax.rsqrt(jnp.mean(xv * xv, axis=-1, keepdims=True) + EPS)
        n = xv * rstd * g_ref[...]
        h_ref[...] = (n * (1.0 + sc_ref[...]) + sh_ref[...]).astype(BF16)

    row = pl.BlockSpec((tr, d), lambda i: (i, 0))
    return _pc(body, out_shape=_sds((s, d), BF16), grid=(s // tr,),
               in_specs=[row, _vec_spec(d), _vec_spec(d), _vec_spec(d)], out_specs=row,
               compiler_params=_params("parallel"), name=name)(x, g, shift, scale)


def _through_gate(dx, f_ref, gate_ref, df_ref, dgate_ref):
    df_ref[...] = (dx * gate_ref[...]).astype(BF16)
    dgate_ref[...] += jnp.sum(dx * f_ref[...].astype(F32), axis=0, keepdims=True)


def _norm_mod_bwd(dh, x, g, scale, dres, name, f=None, gate=None, exchange=None):
    s, d = x.shape
    tr = _tile(s, 256, 16)
    gated = f is not None

    def body(dh_ref, x_ref, g_ref, sc_ref, dres_ref, *rest):
        f_ref, gate_ref = rest[:2] if gated else (None, None)
        dx_ref, dsh_ref, dsc_ref, dg_ref = rest[2:6] if gated else rest[:4]
        df_ref, dgate_ref = rest[6:8] if gated else (None, None)

        @pl.when(pl.program_id(0) == 0)
        def _():
            for ref in (dsh_ref, dsc_ref, dg_ref) + ((dgate_ref,) if gated else ()):
                ref[...] = jnp.zeros_like(ref)

        xv = x_ref[...]
        dhv = dh_ref[...]
        gv = g_ref[...]
        rstd = lax.rsqrt(jnp.mean(xv * xv, axis=-1, keepdims=True) + EPS)
        xhat = xv * rstd
        dn = dhv * (1.0 + sc_ref[...])
        dsh_ref[...] += jnp.sum(dhv, axis=0, keepdims=True)
        dsc_ref[...] += jnp.sum(dhv * (xhat * gv), axis=0, keepdims=True)
        dg_ref[...] += jnp.sum(dn * xhat, axis=0, keepdims=True)
        dxh = dn * gv
        proj = jnp.mean(dxh * xhat, axis=-1, keepdims=True)
        dx = dres_ref[...] + rstd * (dxh - xhat * proj)
        dx_ref[...] = dx
        if gated:
            _through_gate(dx, f_ref, gate_ref, df_ref, dgate_ref)

    row = pl.BlockSpec((tr, d), lambda i: (i, 0))
    vec = _vec_spec(d)
    out_shape = [_sds((s, d), F32), _sds((1, d), F32), _sds((1, d), F32), _sds((1, d), F32)]
    out_specs, in_specs, args = [row, vec, vec, vec], [row, row, vec, vec, row], [dh, x, g, scale, dres]
    if gated:
        out_shape += [_sds((s, d), BF16), _sds((1, d), F32)]
        out_specs += [row, vec]
        in_specs += [row, vec]
        args += [f, gate]
    return _pc(body, exchange, out_shape=tuple(out_shape), grid=(s // tr,), in_specs=in_specs,
               out_specs=tuple(out_specs), compiler_params=_params("arbitrary"), name=name)(*args)


def _final_loss(x, g, target, f, gate, name):
    s, d = x.shape
    tr = _tile(s, 256, 16)
    nsteps = s // tr

    def body(x_ref, g_ref, t_ref, f_ref, gate_ref, dx_ref, loss_ref, dg_ref, df_ref, dgate_ref):
        i = pl.program_id(0)

        @pl.when(i == 0)
        def _():
            loss_ref[...] = jnp.zeros_like(loss_ref)
            dg_ref[...] = jnp.zeros_like(dg_ref)
            dgate_ref[...] = jnp.zeros_like(dgate_ref)

        xv = x_ref[...]
        gv = g_ref[...]
        rstd = lax.rsqrt(jnp.mean(xv * xv, axis=-1, keepdims=True) + EPS)
        xhat = xv * rstd
        err = xhat * gv - t_ref[...]
        dy = err * (1.0 / d)
        loss_ref[...] += jnp.sum(0.5 * err * dy, axis=0, keepdims=True)
        dg_ref[...] += jnp.sum(dy * xhat, axis=0, keepdims=True)
        dxh = dy * gv
        proj = jnp.mean(dxh * xhat, axis=-1, keepdims=True)
        dx = rstd * (dxh - xhat * proj)
        dx_ref[...] = dx
        _through_gate(dx, f_ref, gate_ref, df_ref, dgate_ref)

        @pl.when(i == nsteps - 1)
        def _():
            loss_ref[...] = jnp.broadcast_to(jnp.sum(loss_ref[...], axis=-1, keepdims=True), loss_ref.shape)

    row = pl.BlockSpec((tr, d), lambda i: (i, 0))
    vec = _vec_spec(d)
    return _pc(body, out_shape=(_sds((s, d), F32), _sds((1, d), F32), _sds((1, d), F32), _sds((s, d), BF16), _sds((1, d), F32)),
               grid=(nsteps,), in_specs=[row, vec, row, row, vec], out_specs=(row, vec, vec, row, vec),
               compiler_params=_params("arbitrary"), name=name)(x, g, target, f, gate)


def _mm(lhs, rhs, dims, out_dtype, name, res=None, gate=None, aux_dtype=None, exchange=None):
    lhs_list = list(lhs) if isinstance(lhs, (list, tuple)) else [lhs]
    rhs_list = list(rhs) if isinstance(rhs, (list, tuple)) else [rhs]
    n_terms = len(lhs_list)
    assert n_terms == len(rhs_list)
    m = lhs_list[0].shape[1 if dims == "tn" else 0]
    n = rhs_list[0].shape[0 if dims == "nt" else 1]
    tn = _tile(n, 1024, LANES)
    tm = _tile(m, 512, LANES if dims == "tn" else 16)
    dn = {"nn": _NN, "nt": _NT, "tn": _TN}[dims]
    in_specs, args = [], []
    for l, r in zip(lhs_list, rhs_list):
        k = l.shape[0 if dims == "tn" else 1]
        assert k == r.shape[1 if dims == "nt" else 0] and k <= MAX_CONTRACTION, (l.shape, r.shape, dims)
        in_specs.append(pl.BlockSpec((k, tm), lambda i, j: (0, i)) if dims == "tn" else pl.BlockSpec((tm, k), lambda i, j: (i, 0)))
        in_specs.append(pl.BlockSpec((tn, k), lambda i, j: (j, 0)) if dims == "nt" else pl.BlockSpec((k, tn), lambda i, j: (0, j)))
        args += [l, r]
    out_spec = pl.BlockSpec((tm, tn), lambda i, j: (i, j))
    has_res, has_gate, has_aux = res is not None, gate is not None, aux_dtype is not None

    def body(*refs):
        refs = list(refs)
        pos = 2 * n_terms
        res_ref = gate_ref = aux_ref = None
        if has_res:
            res_ref = refs[pos]; pos += 1
        if has_gate:
            gate_ref = refs[pos]; pos += 1
        out_ref = refs[pos]; pos += 1
        if has_aux:
            aux_ref = refs[pos]
        acc = lax.dot_general(refs[0][...], refs[1][...], dn, preferred_element_type=F32)
        for p in range(1, n_terms):
            acc += lax.dot_general(refs[2 * p][...], refs[2 * p + 1][...], dn, preferred_element_type=F32)
        if has_aux:
            aux_ref[...] = acc.astype(aux_dtype)
        if has_gate:
            acc = acc * gate_ref[...]
        if has_res:
            acc = res_ref[...] + acc
        out_ref[...] = acc.astype(out_dtype)

    if has_res:
        in_specs.append(out_spec); args.append(res)
    if has_gate:
        in_specs.append(pl.BlockSpec((1, tn), lambda i, j: (0, j))); args.append(gate)
    out_shape = [_sds((m, n), out_dtype)]
    out_specs = [out_spec]
    if has_aux:
        out_shape.append(_sds((m, n), aux_dtype)); out_specs.append(out_spec)
    outs = _pc(body, exchange, out_shape=tuple(out_shape), grid=(m // tm, n // tn), in_specs=in_specs,
               out_specs=tuple(out_specs), compiler_params=_params("parallel", "parallel"), name=name)(*args)
    return outs if has_aux else outs[0]


def _ffn_up(h, wg_t, wu_t, name, exchange=None):
    s, d = h.shape
    f = wg_t.shape[0]
    tm = _tile(s, 1024, 16)
    tn = _tile(f, 256, LANES)

    def body(h_ref, wg_ref, wu_ref, a_ref, u_ref, hid_ref):
        hv = h_ref[...]
        a = lax.dot_general(hv, wg_ref[...], _NT, preferred_element_type=F32)
        u = lax.dot_general(hv, wu_ref[...], _NT, preferred_element_type=F32)
        a_ref[...] = a.astype(BF16)
        u_ref[...] = u.astype(BF16)
        hid_ref[...] = (a * jax.nn.sigmoid(a) * u).astype(BF16)

    hs = pl.BlockSpec((tm, d), lambda i, j: (i, 0))
    ws = pl.BlockSpec((tn, d), lambda i, j: (j, 0))
    os_ = pl.BlockSpec((tm, tn), lambda i, j: (i, j))
    return _pc(body, exchange, out_shape=(_sds((s, f), BF16),) * 3, grid=(s // tm, f // tn),
               in_specs=[hs, ws, ws], out_specs=(os_, os_, os_),
               compiler_params=_params("parallel", "parallel"), name=name)(h, wg_t, wu_t)


def _ffn_dact(df, wd, a, u, name, exchange=None):
    s, d = df.shape
    f = wd.shape[0]
    tm = _tile(s, 1024, 16)
    tn = _tile(f, 256, LANES)

    def body(df_ref, wd_ref, a_ref, u_ref, da_ref, du_ref):
        dhid = lax.dot_general(df_ref[...], wd_ref[...], _NT, preferred_element_type=F32)
        av = a_ref[...].astype(F32)
        uv = u_ref[...].astype(F32)
        sig = jax.nn.sigmoid(av)
        da_ref[...] = (dhid * uv * (sig * (1.0 + av * (1.0 - sig)))).astype(BF16)
        du_ref[...] = (dhid * (av * sig)).astype(BF16)

    ds_ = pl.BlockSpec((tm, d), lambda i, j: (i, 0))
    ws = pl.BlockSpec((tn, d), lambda i, j: (j, 0))
    os_ = pl.BlockSpec((tm, tn), lambda i, j: (i, j))
    return _pc(body, exchange, out_shape=(_sds((s, f), BF16),) * 2, grid=(s // tm, f // tn),
               in_specs=[ds_, ws, os_, os_], out_specs=(os_, os_),
               compiler_params=_params("parallel", "parallel"), name=name)(df, wd, a, u)


def _split3(v):
    hi = v.astype(BF16)
    r1 = v - hi.astype(F32)
    mid = r1.astype(BF16)
    lo = (r1 - mid.astype(F32)).astype(BF16)
    return hi, mid, lo


def _dot3(v, mat):
    hi, mid, lo = _split3(v)
    out = lax.dot_general(hi, mat, _NN, preferred_element_type=F32)
    out += lax.dot_general(mid, mat, _NN, preferred_element_type=F32)
    out += lax.dot_general(lo, mat, _NN, preferred_element_type=F32)
    return out


def _forget_fwd(flog_t, bias, name):
    h, s = flog_t.shape
    blk = _tile(s, 512, LANES)
    tri = (jnp.arange(blk)[:, None] <= jnp.arange(blk)[None, :]).astype(BF16)

    def body(z_ref, b_ref, tri_ref, f_ref, carry):
        @pl.when(pl.program_id(0) == 0)
        def _():
            carry[...] = jnp.zeros_like(carry)

        z = z_ref[...] + b_ref[...]
        e = jnp.exp(-jnp.abs(z))
        w = 1.0 + e
        log1p_e = jnp.where(w == 1.0, e, jnp.log(w) * (e / (w - 1.0)))
        lf = jnp.minimum(z, 0.0) - log1p_e
        out = carry[...] + _dot3(lf, tri_ref[...])
        for j, piece in enumerate(_split3(out)):
            f_ref[j] = piece
        carry[...] = out[:, blk - 1:blk]

    zs = pl.BlockSpec((h, blk), lambda i: (0, i))
    return _pc(body, out_shape=_sds((3, h, s), BF16), grid=(s // blk,),
               in_specs=[zs, pl.BlockSpec((h, 1), lambda i: (0, 0)), pl.BlockSpec((blk, blk), lambda i: (0, 0))],
               out_specs=pl.BlockSpec((3, h, blk), lambda i: (0, 0, i)), scratch_shapes=[pltpu.VMEM((h, 1), F32)],
               compiler_params=_params("arbitrary"), name=name)(flog_t, bias, tri)


def _forget_bwd(df_t, flog_t, bias, name):
    h, s = flog_t.shape
    blk = _tile(s, 512, LANES)
    nb = s // blk
    tri = (jnp.arange(blk)[:, None] >= jnp.arange(blk)[None, :]).astype(BF16)

    def body(df_ref, z_ref, b_ref, tri_ref, dz_ref, db_ref, carry):
        @pl.when(pl.program_id(0) == 0)
        def _():
            carry[...] = jnp.zeros_like(carry)
            db_ref[...] = jnp.zeros_like(db_ref)

        rc = carry[...] + _dot3(df_ref[...], tri_ref[...])
        carry[...] = rc[:, 0:1]
        dz = rc * jax.nn.sigmoid(-(z_ref[...] + b_ref[...]))
        dz_ref[...] = dz
        db_ref[...] += jnp.sum(dz, axis=-1, keepdims=True)

    rev = pl.BlockSpec((h, blk), lambda i: (0, nb - 1 - i))
    col = pl.BlockSpec((h, 1), lambda i: (0, 0))
    return _pc(body, out_shape=(_sds((h, s), F32), _sds((h, 1), F32)), grid=(nb,),
               in_specs=[rev, rev, col, pl.BlockSpec((blk, blk), lambda i: (0, 0))],
               out_specs=(rev, col), scratch_shapes=[pltpu.VMEM((h, 1), F32)],
               compiler_params=_params("arbitrary"), name=name)(df_t, flog_t, bias, tri)


def _attn_tiles(s):
    return _tile(s, 1024, LANES)


def _attn_half(t):
    return t // 2 if t >= 4 * LANES else t


BIAS_ROWS = 16


def _attn_prep(qkv, f_pieces, name):
    s = qkv.shape[0]
    a_w = qkv.shape[1] // 3
    npair = a_w // LANES
    t = _attn_tiles(s)
    scale = 1.0 / math.sqrt(HEAD_DIM)

    six = f_pieces[:, :2 * npair].reshape(3, npair, 2, s).transpose(1, 3, 2, 0).reshape(npair, s, 6)
    feat = jnp.concatenate([six, jnp.ones((npair, s, 1), BF16), jnp.zeros((npair, s, BIAS_ROWS - 7), BF16)], axis=-1)
    place_q = [[0.0] * (2 * LANES) for _ in range(BIAS_ROWS)]
    place_k = [[0.0] * (2 * LANES) for _ in range(BIAS_ROWS)]
    for hh in range(2):
        b0 = hh * LANES + (HEAD_DIM if hh == 0 else 0)
        for j in range(3):
            place_q[3 * hh + j][b0 + j] = 1.0
            place_q[6][b0 + 3 + j] = 1.0
            place_k[6][b0 + j] = 1.0
            place_k[3 * hh + j][b0 + 3 + j] = -1.0
    place_q = jnp.array(place_q, BF16)
    place_k = jnp.array(place_k, BF16)

    def body(q_ref, k_ref, v_ref, f_ref, pq_ref, pk_ref, qa_ref, ka_ref, va_ref):
        lane = lax.broadcasted_iota(jnp.int32, (1, LANES), 1)
        q2 = (q_ref[...].astype(F32) * scale).astype(BF16)
        k2, v2 = k_ref[...], v_ref[...]
        qx = lax.dot_general(f_ref[0], pq_ref[...], _NN, preferred_element_type=F32).astype(BF16)
        kx = lax.dot_general(f_ref[0], pk_ref[...], _NN, preferred_element_type=F32).astype(BF16)
        for hh in range(2):
            real = (lane < HEAD_DIM) if hh == 0 else (lane >= HEAD_DIM)
            cols = slice(hh * LANES, (hh + 1) * LANES)
            qa_ref[:, cols] = jnp.where(real, q2, qx[:, cols])
            ka_ref[:, cols] = jnp.where(real, k2, kx[:, cols])
            va_ref[:, cols] = jnp.where(real, v2, jnp.zeros_like(v2))

    def col(off):
        return pl.BlockSpec((t, LANES), lambda p, i: (i, off + p))

    out = pl.BlockSpec((t, 2 * LANES), lambda p, i: (i, p))
    place = pl.BlockSpec((BIAS_ROWS, 2 * LANES), lambda p, i: (0, 0))
    return _pc(body, out_shape=(_sds((s, 2 * a_w), BF16),) * 3, grid=(npair, s // t),
               in_specs=[col(0), col(npair), col(2 * npair), pl.BlockSpec((1, t, BIAS_ROWS), lambda p, i: (p, i, 0)),
                         place, place],
               out_specs=(out, out, out), compiler_params=_params("parallel", "parallel"), name=name)(
                   qkv, qkv, qkv, feat, place_q, place_k)


def _attn_fwd(qa, ka, va, name, exchange=None):
    s = qa.shape[0]
    a_w = qa.shape[1] // 2
    npair = a_w // LANES
    t = _attn_tiles(s)
    nq = s // t
    half = _attn_half(t)

    def body(q_ref, k_ref, v_ref, o_ref, lse_ref, m_sc, l_sc, acc_sc):
        qi = pl.program_id(1)
        first = lax.broadcasted_iota(jnp.int32, (1, LANES), 1) < HEAD_DIM
        m_sc[...] = jnp.full_like(m_sc, NEG_BIG)
        l_sc[...] = jnp.zeros_like(l_sc)
        acc_sc[...] = jnp.zeros_like(acc_sc)

        def step(q0, k_start, size, diag):
            q_sl = slice(q0, q0 + size)
            k_rows = pl.ds(pl.multiple_of(k_start, size), size)
            m_old = m_sc[q_sl, :]
            keep = None
            if diag:
                keep = (lax.broadcasted_iota(jnp.int32, (size, size), 0) >= lax.broadcasted_iota(jnp.int32, (size, size), 1))
            m_new, rs, pv = [], [], []
            for hh in range(2):
                cols = slice(hh * LANES, (hh + 1) * LANES)
                sc = lax.dot_general(q_ref[q_sl, cols], k_ref[k_rows, cols], _NT, preferred_element_type=F32)
                if diag:
                    sc = jnp.where(keep, sc, NEG_BIG)
                mo = m_old[:, hh * HEAD_DIM:hh * HEAD_DIM + 1]
                mn = jnp.maximum(mo, jnp.max(sc, axis=1, keepdims=True))
                p = jnp.exp(sc - mn)
                m_new.append(mn)
                rs.append(jnp.sum(p, axis=1, keepdims=True))
                pv.append(lax.dot_general(p.astype(BF16), v_ref[k_rows, cols], _NN, preferred_element_type=F32))
            m2 = jnp.where(first, m_new[0], m_new[1])
            alpha = jnp.exp(m_old - m2)
            m_sc[q_sl, :] = m2
            l_sc[q_sl, :] = alpha * l_sc[q_sl, :] + jnp.where(first, rs[0], rs[1])
            acc_sc[q_sl, :] = alpha * acc_sc[q_sl, :] + pv[0] + pv[1]

        def below_diagonal(ki, carry):
            step(0, ki * t, t, False)
            return carry

        lax.fori_loop(0, qi, below_diagonal, 0)
        step(0, qi * t, half, True)
        if half < t:
            step(half, qi * t, half, False)
            step(half, qi * t + half, half, True)
        l2 = l_sc[...]
        o_ref[...] = acc_sc[...] / l2
        lse_ref[...] = m_sc[...] + jnp.log(l2)

    qs = pl.BlockSpec((t, 2 * LANES), lambda p, qi: (qi, p))
    ks = pl.BlockSpec((s, 2 * LANES), lambda p, qi: (0, p))
    os_ = pl.BlockSpec((t, LANES), lambda p, qi: (qi, p))
    return _pc(body, exchange, out_shape=(_sds((s, a_w), F32), _sds((s, a_w), F32)), grid=(npair, nq),
               in_specs=[qs, ks, ks], out_specs=(os_, os_),
               scratch_shapes=[pltpu.VMEM((t, LANES), F32)] * 3,
               compiler_params=_params("parallel", "arbitrary"), name=name)(qa, ka, va)


def _attn_bwd(qa, ka, va, do, o, lse, name, exchange=None):
    s = qa.shape[0]
    a_w = qa.shape[1] // 2
    npair = a_w // LANES
    t = _attn_tiles(s)
    nq = s // t
    half = _attn_half(t)
    scale = 1.0 / math.sqrt(HEAD_DIM)

    def body(q_ref, k_ref, v_ref, do_ref, o_ref, lse_ref, dq_ref, dk_ref, dv_ref, qx_ref, kx_ref, dk_sc, dv_sc, kx_sc):
        ki = pl.program_id(1)
        first = lax.broadcasted_iota(jnp.int32, (1, LANES), 1) < HEAD_DIM

        @pl.when(ki == 0)
        def _():
            dq_ref[...] = jnp.zeros_like(dq_ref)
            qx_ref[...] = jnp.zeros_like(qx_ref)

        def step(q_start, k0, size, diag, assign):
            rows = pl.ds(pl.multiple_of(q_start, size), size)
            k_sl = slice(k0, k0 + size)
            do2 = do_ref[rows, :]
            lse2 = lse_ref[rows, :]
            dd = do2.astype(F32) * o_ref[rows, :]
            keep = None
            if diag:
                keep = (lax.broadcasted_iota(jnp.int32, (size, size), 0) >= lax.broadcasted_iota(jnp.int32, (size, size), 1))
            dq_h, dk_h, dv_h = [], [], []
            for hh in range(2):
                sel = first if hh == 0 else jnp.logical_not(first)
                cols = slice(hh * LANES, (hh + 1) * LANES)
                qh, kh, vh = q_ref[rows, cols], k_ref[k_sl, cols], v_ref[k_sl, cols]
                delta = jnp.sum(jnp.where(sel, dd, 0.0), axis=1, keepdims=True)
                sc = lax.dot_general(qh, kh, _NT, preferred_element_type=F32)
                if diag:
                    sc = jnp.where(keep, sc, NEG_BIG)
                p = jnp.exp(sc - lse2[:, hh * HEAD_DIM:hh * HEAD_DIM + 1])
                dp = lax.dot_general(do2, vh, _NT, preferred_element_type=F32)
                ds_b = (p * (dp - delta)).astype(BF16)
                dv_h.append(lax.dot_general(p.astype(BF16), do2, _TN, preferred_element_type=F32))
                dk_h.append(lax.dot_general(ds_b, qh, _TN, preferred_element_type=F32))
                dq_h.append(lax.dot_general(ds_b, kh, _NN, preferred_element_type=F32))
            dq_ref[rows, :] += jnp.where(first, dq_h[0], dq_h[1]) * scale
            qx_ref[rows, :] += jnp.where(first, dq_h[1], dq_h[0])
            dk_new = jnp.where(first, dk_h[0], dk_h[1])
            kx_new = jnp.where(first, dk_h[1], dk_h[0])
            dv_new = jnp.where(first, dv_h[0], dv_h[1])
            if assign:
                dk_sc[k_sl, :] = dk_new
                kx_sc[k_sl, :] = kx_new
                dv_sc[k_sl, :] = dv_new
            else:
                dk_sc[k_sl, :] += dk_new
                kx_sc[k_sl, :] += kx_new
                dv_sc[k_sl, :] += dv_new

        def below_diagonal(qi, carry):
            step(qi * t, 0, t, False, False)
            return carry

        step(ki * t, 0, half, True, True)
        if half < t:
            step(ki * t + half, 0, half, False, False)
            step(ki * t + half, half, half, True, True)
        lax.fori_loop(ki + 1, nq, below_diagonal, 0)
        dk_ref[...] = dk_sc[...].astype(BF16)
        dv_ref[...] = dv_sc[...].astype(BF16)
        kx_ref[...] = kx_sc[...]

    ks2 = pl.BlockSpec((t, 2 * LANES), lambda p, ki: (ki, p))
    qs2 = pl.BlockSpec((s, 2 * LANES), lambda p, ki: (0, p))
    whole = pl.BlockSpec((s, LANES), lambda p, ki: (0, p))
    kout = pl.BlockSpec((t, LANES), lambda p, ki: (ki, p))
    return _pc(body, exchange,
               out_shape=(_sds((s, a_w), F32), _sds((s, a_w), BF16), _sds((s, a_w), BF16), _sds((s, a_w), F32),
                          _sds((s, a_w), F32)),
               grid=(npair, nq), in_specs=[qs2, ks2, ks2, whole, whole, whole],
               out_specs=(whole, kout, kout, whole, kout),
               scratch_shapes=[pltpu.VMEM((t, LANES), F32)] * 3,
               compiler_params=_params("parallel", "arbitrary"), name=name)(qa, ka, va, do, o, lse)

def _decay_grads(qx, kx, name):
    s, a_w = qx.shape
    n_heads = a_w // HEAD_DIM
    tr = _tile(s, 512, 8)
    pick_q = [[0.0] * LANES for _ in range(a_w)]
    pick_k = [[0.0] * LANES for _ in range(a_w)]
    for h in range(n_heads):
        b0 = (h // 2) * LANES + (HEAD_DIM if h % 2 == 0 else 0)
        pick_q[b0][h] = 1.0
        pick_k[b0 + 3][h] = 1.0
    pick_q = jnp.array(pick_q, BF16)
    pick_k = jnp.array(pick_k, BF16)

    def body(qx_ref, kx_ref, pq_ref, pk_ref, o_ref):
        o_ref[...] = _dot3(qx_ref[...], pq_ref[...]) - _dot3(kx_ref[...], pk_ref[...])

    row = pl.BlockSpec((tr, a_w), lambda i: (i, 0))
    pick = pl.BlockSpec((a_w, LANES), lambda i: (0, 0))
    return _pc(body, out_shape=_sds((s, LANES), F32), grid=(s // tr,), in_specs=[row, row, pick, pick],
               out_specs=pl.BlockSpec((tr, LANES), lambda i: (i, 0)),
               compiler_params=_params("parallel"), name=name)(qx, kx, pick_q, pick_k)


def _shift_down(z, k, rows):
    return jnp.where(rows >= k, pltpu.roll(z, k, 0), 0.0)


def _shift_up(z, k, rows, n):
    return jnp.where(rows < n - k, pltpu.roll(z, n - k, 0), 0.0)


def _conv_fwd(bcx, conv_w, name):
    s = bcx.shape[0]
    cw = bcx.shape[1] // 3
    nb = cw // LANES

    def body(b_ref, c_ref, x_ref, w_ref, cv_ref):
        rows = lax.broadcasted_iota(jnp.int32, (s, LANES), 0)
        z = c_ref[...] * x_ref[...]
        w = w_ref[...]
        y = w[2:3, :] * z + w[1:2, :] * _shift_down(z, 1, rows) + w[0:1, :] * _shift_down(z, 2, rows)
        cv_ref[...] = b_ref[...] * y

    def col(off):
        return pl.BlockSpec((s, LANES), lambda j: (0, j + off))

    return _pc(body, out_shape=_sds((s, cw), F32), grid=(nb,),
               in_specs=[col(0), col(nb), col(2 * nb), pl.BlockSpec((CONV_K, LANES), lambda j: (0, j))],
               out_specs=col(0), compiler_params=_params("parallel"), name=name)(bcx, bcx, bcx, conv_w)


def _conv_bwd(dcv, bcx, conv_w, name):
    s = bcx.shape[0]
    cw = bcx.shape[1] // 3
    nb = cw // LANES

    def body(dcv_ref, b_ref, c_ref, x_ref, w_ref, db_ref, dc_ref, dxc_ref, dw_ref):
        rows = lax.broadcasted_iota(jnp.int32, (s, LANES), 0)
        cv_, xv = c_ref[...], x_ref[...]
        z = cv_ * xv
        w = w_ref[...]
        z1 = _shift_down(z, 1, rows)
        z2 = _shift_down(z, 2, rows)
        y = w[2:3, :] * z + w[1:2, :] * z1 + w[0:1, :] * z2
        dcvv = dcv_ref[...]
        db_ref[...] = (dcvv * y).astype(BF16)
        dy = dcvv * b_ref[...]
        dw_ref[0:1, :] = jnp.sum(dy * z2, axis=0, keepdims=True)
        dw_ref[1:2, :] = jnp.sum(dy * z1, axis=0, keepdims=True)
        dw_ref[2:3, :] = jnp.sum(dy * z, axis=0, keepdims=True)
        dz = w[2:3, :] * dy + w[1:2, :] * _shift_up(dy, 1, rows, s) + w[0:1, :] * _shift_up(dy, 2, rows, s)
        dc_ref[...] = (dz * xv).astype(BF16)
        dxc_ref[...] = (dz * cv_).astype(BF16)

    def col(off):
        return pl.BlockSpec((s, LANES), lambda j: (0, j + off))

    wspec = pl.BlockSpec((CONV_K, LANES), lambda j: (0, j))
    db, dc, dxc, dw = _pc(body, out_shape=(_sds((s, cw), BF16),) * 3 + (_sds((CONV_K, cw), F32),), grid=(nb,),
                          in_specs=[col(0), col(0), col(nb), col(2 * nb), wspec],
                          out_specs=(col(0), col(0), col(0), wspec),
                          compiler_params=_params("parallel"), name=name)(dcv, bcx, bcx, bcx, conv_w)
    return db, dc, dxc, dw


def _group_matrix():
    idx = jnp.arange(LANES) // HEAD_DIM
    return (idx[:, None] == idx[None, :]).astype(BF16)


def _group_sum(v, gmat):
    return _dot3(v, gmat)


def _gnorm_fwd(att, cv, gg, name):
    s, a_w = att.shape
    cw = cv.shape[1]
    d = a_w + cw
    tr = _tile(s, 512, 16)
    gmat = _group_matrix()

    def body(att_ref, cv_ref, gg_ref, gm_ref, yn_ref):
        gm = gm_ref[...]
        for c0 in range(0, d, LANES):
            y = att_ref[:, c0:c0 + LANES] if c0 < a_w else cv_ref[:, c0 - a_w:c0 - a_w + LANES]
            ms = _group_sum(y * y, gm) * (1.0 / HEAD_DIM)
            yn_ref[:, c0:c0 + LANES] = (y * lax.rsqrt(ms + EPS) * gg_ref[:, c0:c0 + LANES]).astype(BF16)

    return _pc(body, out_shape=_sds((s, d), BF16), grid=(s // tr,),
               in_specs=[pl.BlockSpec((tr, a_w), lambda i: (i, 0)), pl.BlockSpec((tr, cw), lambda i: (i, 0)),
                         _vec_spec(d), pl.BlockSpec((LANES, LANES), lambda i: (0, 0))],
               out_specs=pl.BlockSpec((tr, d), lambda i: (i, 0)),
               compiler_params=_params("parallel"), name=name)(att, cv, gg, gmat)


def _gnorm_bwd(dyn, att, cv, gg, name):
    s, a_w = att.shape
    cw = cv.shape[1]
    d = a_w + cw
    tr = _tile(s, 256, 16)
    gmat = _group_matrix()

    def body(dyn_ref, att_ref, cv_ref, gg_ref, gm_ref, datt_ref, dcv_ref, dgg_ref):
        @pl.when(pl.program_id(0) == 0)
        def _():
            dgg_ref[...] = jnp.zeros_like(dgg_ref)

        gm = gm_ref[...]
        for c0 in range(0, d, LANES):
            y = att_ref[:, c0:c0 + LANES] if c0 < a_w else cv_ref[:, c0 - a_w:c0 - a_w + LANES]
            dv = dyn_ref[:, c0:c0 + LANES]
            r = lax.rsqrt(_group_sum(y * y, gm) * (1.0 / HEAD_DIM) + EPS)
            xhat = y * r
            dgg_ref[:, c0:c0 + LANES] += jnp.sum(dv * xhat, axis=0, keepdims=True)
            dxh = dv * gg_ref[:, c0:c0 + LANES]
            proj = _group_sum(dxh * xhat, gm) * (1.0 / HEAD_DIM)
            dy = r * (dxh - xhat * proj)
            if c0 < a_w:
                datt_ref[:, c0:c0 + LANES] = dy.astype(BF16)
            else:
                dcv_ref[:, c0 - a_w:c0 - a_w + LANES] = dy

    return _pc(body, out_shape=(_sds((s, a_w), BF16), _sds((s, cw), F32), _sds((1, d), F32)), grid=(s // tr,),
               in_specs=[pl.BlockSpec((tr, d), lambda i: (i, 0)), pl.BlockSpec((tr, a_w), lambda i: (i, 0)),
                         pl.BlockSpec((tr, cw), lambda i: (i, 0)), _vec_spec(d),
                         pl.BlockSpec((LANES, LANES), lambda i: (0, 0))],
               out_specs=(pl.BlockSpec((tr, a_w), lambda i: (i, 0)), pl.BlockSpec((tr, cw), lambda i: (i, 0)),
                          _vec_spec(d)),
               compiler_params=_params("arbitrary"), name=name)(dyn, att, cv, gg, gmat)


def _adamw_math(w, g, m, v):
    m_new = ADAM_B1 * m + (1.0 - ADAM_B1) * g
    v_new = ADAM_B2 * v + (1.0 - ADAM_B2) * (g * g)
    m_hat = m_new / (1.0 - ADAM_B1 ** ADAM_STEP)
    v_hat = v_new / (1.0 - ADAM_B2 ** ADAM_STEP)
    delta = -ADAM_LR * (m_hat / (jnp.sqrt(v_hat) + ADAM_EPS) + ADAM_WD * w)
    return delta, m_new, v_new


def _row_tile(r, c):
    return _tile(r, max(8, ((1 << 18) // c) // 8 * 8), 8)


def _adamw(w, g, m, v, name):
    r, c = w.shape
    tr = _row_tile(r, c)

    def body(w_ref, g_ref, m_ref, v_ref, d_ref, mo_ref, vo_ref):
        d, mn, vn = _adamw_math(w_ref[...], g_ref[...], m_ref[...], v_ref[...])
        d_ref[...] = d
        mo_ref[...] = mn
        vo_ref[...] = vn

    spec = pl.BlockSpec((tr, c), lambda i: (i, 0))
    return _pc(body, out_shape=(_sds((r, c), F32),) * 3, grid=(r // tr,), in_specs=[spec] * 4,
               out_specs=(spec,) * 3, compiler_params=_params("parallel"), name=name)(w, g, m, v)


def _adamw_halves(w, mine, theirs, m, v, core, name):
    r2, c = w.shape
    r = r2 // 2
    assert mine.shape == (r, c) and theirs.shape == (r, c)
    tr = _row_tile(r, c)
    nb = r // tr

    def body(core_ref, w_ref, a_ref, b_ref, m_ref, v_ref, g_ref, d_ref, mo_ref, vo_ref):
        g = jnp.where(pl.program_id(0) == core_ref[0], a_ref[...], b_ref[...])
        d, mn, vn = _adamw_math(w_ref[...], g, m_ref[...], v_ref[...])
        g_ref[...] = g
        d_ref[...] = d
        mo_ref[...] = mn
        vo_ref[...] = vn

    full = pl.BlockSpec((tr, c), lambda h, i, core_ref: (h * nb + i, 0))
    half = pl.BlockSpec((tr, c), lambda h, i, core_ref: (i, 0))
    grid_spec = pltpu.PrefetchScalarGridSpec(
        num_scalar_prefetch=1, grid=(2, nb), in_specs=[full, half, half, full, full], out_specs=(full,) * 4)
    return _pc(body, out_shape=(_sds((r2, c), F32),) * 4, grid_spec=grid_spec,
               compiler_params=_params("parallel", "parallel"), name=name)(core, w, mine, theirs, m, v)


def _ada_fwd(c16, ada_w, ada_b, name):
    d, n = ada_w.shape
    tn = _tile(n, 768, LANES)

    def body(c_ref, w_ref, b_ref, o_ref):
        cv = c_ref[...]
        sc = (cv * jax.nn.sigmoid(cv)).astype(BF16)
        o_ref[...] = lax.dot_general(sc, w_ref[...].astype(BF16), _NN, preferred_element_type=F32) + b_ref[...]

    return _pc(body, out_shape=_sds((16, n), F32), grid=(n // tn,),
               in_specs=[pl.BlockSpec((16, d), lambda j: (0, 0)), pl.BlockSpec((d, tn), lambda j: (0, j)),
                         pl.BlockSpec((1, tn), lambda j: (0, j))],
               out_specs=pl.BlockSpec((16, tn), lambda j: (0, j)),
               compiler_params=_params("parallel"), name=name)(c16, ada_w, ada_b)


def _ada_update(c16_t, dmod16, w, m, v, name, exchange=None):
    r, c = w.shape
    tr = _row_tile(r, c)

    def body(c_ref, dm_ref, w_ref, m_ref, v_ref, g_ref, d_ref, mo_ref, vo_ref):
        cv = c_ref[...]
        sc = (cv * jax.nn.sigmoid(cv)).astype(BF16)
        g = lax.dot_general(sc, dm_ref[...].astype(BF16), _NN, preferred_element_type=F32)
        d, mn, vn = _adamw_math(w_ref[...], g, m_ref[...], v_ref[...])
        g_ref[...] = g
        d_ref[...] = d
        mo_ref[...] = mn
        vo_ref[...] = vn

    spec = pl.BlockSpec((tr, c), lambda i: (i, 0))
    return _pc(body, exchange, out_shape=(_sds((r, c), F32),) * 4, grid=(r // tr,),
               in_specs=[pl.BlockSpec((tr, 16), lambda i: (i, 0)), pl.BlockSpec((16, c), lambda i: (0, 0)),
                         spec, spec, spec],
               out_specs=(spec,) * 4, compiler_params=_params("parallel"), name=name)(c16_t, dmod16, w, m, v)


def _add_half(dw, recv, core, name):
    _, _, r, w = dw.shape
    tr = _tile(r, 512, 16)

    def body(core_ref, a_ref, b_ref, o_ref):
        o_ref[...] = (a_ref[...].astype(F32) + b_ref[...].astype(F32)).astype(BF16)

    grid_spec = pltpu.PrefetchScalarGridSpec(
        num_scalar_prefetch=1, grid=(N_CHIPS, r // tr),
        in_specs=[pl.BlockSpec((None, None, tr, w), lambda s, i, core_ref: (s, core_ref[0], i, 0)),
                  pl.BlockSpec((None, tr, w), lambda s, i, core_ref: (s, i, 0))],
        out_specs=pl.BlockSpec((None, tr, w), lambda s, i, core_ref: (s, i, 0)))
    return _pc(body, out_shape=_sds((N_CHIPS, r, w), BF16), grid_spec=grid_spec,
               compiler_params=_params("parallel", "parallel"), name=name)(core, dw, recv)


def _sum_chips(own, recv, chip, name):
    _, r, w = own.shape
    tr = _tile(r, 512, 16)

    def body(chip_ref, own_ref, p_ref, o_ref):
        acc = own_ref[...].astype(F32)
        for q in range(N_CHIPS - 1):
            acc = acc + p_ref[q].astype(F32)
        o_ref[...] = acc

    grid_spec = pltpu.PrefetchScalarGridSpec(
        num_scalar_prefetch=1, grid=(r // tr,),
        in_specs=[pl.BlockSpec((None, tr, w), lambda i, chip_ref: (chip_ref[0], i, 0)),
                  pl.BlockSpec((N_CHIPS - 1, tr, w), lambda i, chip_ref: (0, i, 0))],
        out_specs=pl.BlockSpec((tr, w), lambda i, chip_ref: (i, 0)))
    return _pc(body, out_shape=_sds((r, w), F32), grid_spec=grid_spec,
               compiler_params=_params("parallel"), name=name)(chip, own, recv)


def _sum_devices(parts, name):
    nd, r, w = parts.shape

    def body(p_ref, o_ref):
        acc = p_ref[0]
        for q in range(1, nd):
            acc = acc + p_ref[q]
        o_ref[...] = acc

    return _pc(body, out_shape=_sds((r, w), F32), name=name)(parts)


def _place():
    x, y, c = lax.axis_index("x"), lax.axis_index("y"), lax.axis_index("c")
    chips = [(1 - x, y), (x, 1 - y), (1 - x, 1 - y)]
    return x, y, c, chips


def _small_gather_exchange(blk):
    r, w = blk.shape

    def copies(src, dst, send_sems, recv_sems):
        x, y, c, chips = _place()
        me, sibling = (x, y, c), (x, y, 1 - c)

        def rows(px, py, pc):
            return dst[0].at[pl.ds((4 * px + 2 * py + pc) * r, r), :]

        def copy(k, block, to, own=False):
            return _remote(src[0] if own else rows(*block), rows(*block), send_sems, recv_sems, k, to)

        mine = pltpu.make_async_copy(src[0], rows(*me), send_sems.at[7])
        first = [copy(0, me, sibling, own=True)] + [copy(1 + j, me, (*chip, c), own=True) for j, chip in enumerate(chips)]
        passed = [copy(4 + j, (*chip, c), sibling) for j, chip in enumerate(chips)]
        landed = [copy(1 + j, (*chip, c), me) for j, chip in enumerate(chips)]
        from_sibling = [copy(0, sibling, me)] + [copy(4 + j, (*chip, 1 - c), me) for j, chip in enumerate(chips)]
        return mine, first, passed, landed, from_sibling

    def start(src, dst, send_sems, recv_sems):
        mine, first, _, _, _ = copies(src, dst, send_sems, recv_sems)
        mine.start()
        for cp in first:
            cp.start()

    def finish(src, dst, send_sems, recv_sems):
        mine, first, passed, landed, from_sibling = copies(src, dst, send_sems, recv_sems)
        for arrival, onward in zip(landed, passed):
            arrival.wait_recv()
            onward.start()
        for cp in from_sibling:
            cp.wait_recv()
        for cp in first + passed:
            cp.wait_send()
        mine.wait()

    return _Exchange([blk], [_sds((N_DEV * r, w), blk.dtype)], 8, start, finish)


def _remote(src, dst, send_sems, recv_sems, k, to):
    return pltpu.make_async_remote_copy(src_ref=src, dst_ref=dst, send_sem=send_sems.at[k], recv_sem=recv_sems.at[k],
                                        device_id=to, device_id_type=MESH)


def _exchange_of(inputs, out_shapes, n_sems, copies, aliases=None):
    def start(src, dst, send_sems, recv_sems):
        for cp in copies(src, dst, send_sems, recv_sems)[0]:
            cp.start()

    def finish(src, dst, send_sems, recv_sems):
        sends, arrivals = copies(src, dst, send_sems, recv_sems)
        for cp in arrivals:
            cp.wait_recv()
        for cp in sends:
            cp.wait_send()

    return _Exchange(inputs, out_shapes, n_sems, start, finish, aliases)


def _run_exchange(ex, name):
    n_in, n_out = len(ex.inputs), len(ex.out_shapes)

    def body(*refs):
        src, dst = refs[:n_in], refs[n_in:n_in + n_out]
        send_sems, recv_sems = refs[n_in + n_out:]
        ex.start(src, dst, send_sems, recv_sems)
        ex.finish(src, dst, send_sems, recv_sems)

    ex.set_results(pl.pallas_call(
        body, out_shape=tuple(ex.out_shapes), in_specs=[_ANY] * n_in, out_specs=(_ANY,) * n_out,
        scratch_shapes=[pltpu.SemaphoreType.DMA((ex.n_sems,)), pltpu.SemaphoreType.DMA((ex.n_sems,))],
        input_output_aliases=ex.aliases, name=name)(*ex.inputs))


def _gather_ici_exchange(shards):
    n = len(shards)

    def copies(own, out, send_sems, recv_sems):
        x, y, c, chips = _place()
        my_chip = 2 * x + y
        sends, arrivals = [], []
        for i in range(n):
            for j, chip in enumerate(chips):
                to = (*chip, c)
                sends.append(_remote(own[i].at[c], out[i].at[my_chip, c], send_sems, recv_sems, 4 * i + j, to))
                arrivals.append(_remote(own[i].at[c], out[i].at[2 * chip[0] + chip[1], c], send_sems, recv_sems, 4 * i + j, to))
            whole = _remote(own[i], out[i].at[my_chip], send_sems, recv_sems, 4 * i + 3, (x, y, 1 - c))
            sends.append(whole)
            arrivals.append(whole)
        return sends, arrivals

    return _exchange_of(shards, [_sds((N_CHIPS,) + s.shape, s.dtype) for s in shards], 4 * n, copies)


def _gather_pass_exchange(gathered):
    n = len(gathered)

    def copies(src, dst, send_sems, recv_sems):
        x, y, c, chips = _place()
        sends, arrivals = [], []
        for i in range(n):
            for j, chip in enumerate(chips):
                idx = 2 * chip[0] + chip[1]
                sends.append(_remote(src[i].at[idx, c], dst[i].at[idx, c], send_sems, recv_sems, 3 * i + j, (x, y, 1 - c)))
                arrivals.append(_remote(src[i].at[idx, c], dst[i].at[idx, 1 - c], send_sems, recv_sems, 3 * i + j, (x, y, 1 - c)))
        return sends, arrivals

    return _exchange_of(gathered, [_sds(g.shape, g.dtype) for g in gathered], 3 * n, copies,
                        aliases={i: i for i in range(n)})


def _reduce_sibling_exchange(grads):
    n = len(grads)

    def copies(src, dst, send_sems, recv_sems):
        x, y, c, _ = _place()
        both = [_remote(src[i].at[s, 1 - c], dst[i].at[s], send_sems, recv_sems, N_CHIPS * i + s, (x, y, 1 - c))
                for i in range(n) for s in range(N_CHIPS)]
        return both, both

    return _exchange_of(grads, [_sds((N_CHIPS,) + g.shape[2:], g.dtype) for g in grads], N_CHIPS * n, copies)


def _reduce_chips_exchange(parts):
    n = len(parts)

    def copies(src, dst, send_sems, recv_sems):
        x, y, c, chips = _place()
        both = [_remote(src[i].at[2 * chip[0] + chip[1]], dst[i].at[j], send_sems, recv_sems, 3 * i + j, (*chip, c))
                for i in range(n) for j, chip in enumerate(chips)]
        return both, both

    return _exchange_of(parts, [_sds((N_CHIPS - 1,) + p.shape[1:], p.dtype) for p in parts], 3 * n, copies)


def _share_exchange(halves):
    n = len(halves)

    def copies(src, dst, send_sems, recv_sems):
        x, y, c, _ = _place()
        both = [_remote(src[i], dst[i], send_sems, recv_sems, i, (x, y, 1 - c)) for i in range(n)]
        return both, both

    return _exchange_of(halves, [_sds(h.shape, h.dtype) for h in halves], n, copies)


HEAD_ROWS = 16


class _WeightTraffic:
    def __init__(self, shards, core, chip):
        self.shards, self.core, self.chip = shards, core, chip
        self.gather, self.grads, self.reduce, self.chip_sums, self.half_sums, self.shared = {}, {}, {}, {}, {}, {}

    def gather_ici(self, grp):
        self.gather[grp] = _gather_ici_exchange(self.shards[grp])
        return self.gather[grp]

    def gather_pass(self, grp):
        self.gather[grp] = _gather_pass_exchange(self.gather[grp].results)
        return self.gather[grp]

    def weights(self, grp):
        return [g.reshape(-1, g.shape[-1]) for g in self.gather[grp].results]

    def reduce_sibling(self, grp, grads):
        self.grads[grp] = [g.reshape(N_CHIPS, 2, g.shape[0] // (2 * N_CHIPS), g.shape[1]) for g in grads]
        self.reduce[grp] = _reduce_sibling_exchange(self.grads[grp])
        return self.reduce[grp]

    def add_halves(self, grp):
        self.chip_sums[grp] = [_add_half(g, r, self.core, "add_half_%s%d" % (grp, i))
                               for i, (g, r) in enumerate(zip(self.grads[grp], self.reduce[grp].results))]

    def reduce_chips(self, grp):
        self.reduce[grp] = _reduce_chips_exchange(self.chip_sums[grp])
        return self.reduce[grp]

    def sum_chips(self, grp):
        self.half_sums[grp] = [_sum_chips(o, p, self.chip, "sum_chips_%s%d" % (grp, i))
                               for i, (o, p) in enumerate(zip(self.chip_sums[grp], self.reduce[grp].results))]

    def share(self, grp):
        self.shared[grp] = _share_exchange(self.half_sums[grp])
        return self.shared[grp]

    def totals(self, grp):
        return list(zip(self.half_sums[grp], self.shared[grp].results))


def _ffn_fwd(x, norm_g, shift, scale, gate, wg_t, wu_t, wd, tag, up_exchange=None, down_exchange=None):
    h = _norm_mod_fwd(x, norm_g, shift, scale, tag + "_norm_fwd")
    a, u, hid = _ffn_up(h, wg_t, wu_t, tag + "_up", exchange=up_exchange)
    wd = wd() if callable(wd) else wd
    x_out, f = _mm(hid, wd, "nn", F32, tag + "_down", res=x, gate=gate, aux_dtype=BF16,
                   exchange=down_exchange() if down_exchange else None)
    return x_out, (h, a, u, hid, f)


def _ffn_bwd(dx_out, df, x, saved, norm_g, scale, wg_t, wu_t, wd, tag, traffic, below=None, dact_exchange=None,
             dw_exchange=None, finish_reduction=False):
    h, a, u, hid, _ = saved
    f_below, gate_below = below if below else (None, None)
    da, du = _ffn_dact(df, wd, a, u, tag + "_dact", exchange=dact_exchange)
    dwd = _mm(hid, df, "tn", BF16, tag + "_dwd", exchange=dw_exchange() if dw_exchange else None)
    if not finish_reduction:
        dwg_t = _mm(da, h, "tn", BF16, tag + "_dwg")
        dwu_t = _mm(du, h, "tn", BF16, tag + "_dwu")
        dh = _mm([da, du], [wg_t, wu_t], "nn", F32, tag + "_dh", exchange=traffic.reduce_sibling(tag, [dwg_t, dwu_t, dwd]))
        traffic.add_halves(tag)
        dx, dshift, dscale, dnorm_g, *gated = _norm_mod_bwd(dh, x, norm_g, scale, dx_out, tag + "_norm_bwd",
                                                            f=f_below, gate=gate_below)
        return dx, (dshift, dscale, dnorm_g), gated
    kd, kg, ku = tag + "_wd", tag + "_wg", tag + "_wu"
    dwg_t = _mm(da, h, "tn", BF16, tag + "_dwg", exchange=traffic.reduce_sibling(kd, [dwd]))
    traffic.add_halves(kd)
    dwu_t = _mm(du, h, "tn", BF16, tag + "_dwu",
                exchange=_join(traffic.reduce_chips(kd), traffic.reduce_sibling(kg, [dwg_t])))
    traffic.add_halves(kg)
    dh = _mm([da, du], [wg_t, wu_t], "nn", F32, tag + "_dh",
             exchange=_join(traffic.reduce_chips(kg), traffic.reduce_sibling(ku, [dwu_t])))
    traffic.add_halves(ku)
    traffic.sum_chips(kd)
    traffic.sum_chips(kg)
    dx, dshift, dscale, dnorm_g, *gated = _norm_mod_bwd(
        dh, x, norm_g, scale, dx_out, tag + "_norm_bwd", f=f_below, gate=gate_below,
        exchange=_join(traffic.reduce_chips(ku), traffic.share(kd), traffic.share(kg)))
    traffic.sum_chips(ku)
    return dx, (dshift, dscale, dnorm_g), gated


def _layer_step(x, target, mod, gains, forget_bias, conv_w, traffic, att_w, in_shard, in_rows):
    sh1, sc1, g1, sh2, sc2, g2, sh3, sc3, g3 = mod
    norm1_g, norm2_g, norm3_g, final_g, group_g = gains
    s, d = x.shape
    n_heads = att_w // HEAD_DIM
    npair = n_heads // 2
    gate1, gate3 = 0.5 * g1, 0.5 * g3

    def split_w_in(w_in_pad):
        w_in_t = w_in_pad.reshape(N_CHIPS, in_rows, d)[:, :in_shard].reshape(N_CHIPS * in_shard, d)
        return (w_in_t[:3 * att_w], _pad_rows(w_in_t[3 * att_w:3 * att_w + n_heads], LANES), w_in_t[3 * att_w + n_heads:])

    wg1_t, wu1_t = traffic.weights("ffn1_gu")

    def wd1_ready():
        _run_exchange(traffic.gather_pass("ffn1_d"), "gather_ffn1_down_pass")
        return traffic.weights("ffn1_d")[0]

    x1, saved1 = _ffn_fwd(x, norm1_g, sh1, sc1, gate1, wg1_t, wu1_t, wd1_ready, "ffn1",
                          up_exchange=_join(traffic.gather_ici("ffn1_d"), traffic.gather_ici("mix")),
                          down_exchange=lambda: traffic.gather_pass("mix"))
    wd1 = traffic.weights("ffn1_d")[0]
    w_in_pad, w_out = traffic.weights("mix")
    wqkv_t, wf_t, wbcx_t = split_w_in(w_in_pad)

    h2 = _norm_mod_fwd(x1, norm2_g, sh2, sc2, "mix_norm_fwd")
    qkv = _mm(h2, wqkv_t, "nt", BF16, "mix_proj_qkv")
    bcx = _mm(h2, wbcx_t, "nt", F32, "mix_proj_bcx")
    flog = _mm(h2, wf_t, "nt", F32, "mix_proj_f")
    flog_t = jnp.pad(flog[:, :n_heads].T, ((0, HEAD_ROWS - n_heads), (0, 0)))
    bias_col = jnp.pad(forget_bias, (0, HEAD_ROWS - n_heads))[:, None]
    f_pieces = _forget_fwd(flog_t, bias_col, "forget_fwd")
    qa, ka, va = _attn_prep(qkv, f_pieces, "attn_prep")
    att, lse = _attn_fwd(qa, ka, va, "attn_fwd", exchange=traffic.gather_ici("ffn2"))
    cv = _conv_fwd(bcx, conv_w, "conv_fwd")
    yn = _gnorm_fwd(att, cv, group_g, "gnorm_fwd")
    x2, mix = _mm(yn, w_out, "nn", F32, "mix_out", res=x1, gate=g2, aux_dtype=BF16, exchange=traffic.gather_pass("ffn2"))
    wg2_t, wu2_t, wd2 = traffic.weights("ffn2")

    x3, saved3 = _ffn_fwd(x2, norm3_g, sh3, sc3, gate3, wg2_t, wu2_t, wd2, "ffn2")

    dx3, loss_row, dfinal_g, df2, dgate3 = _final_loss(x3, final_g, target, saved3[4], gate3, "final_loss")

    dx2, (dsh3, dsc3, dnorm3_g), (dmix, dg2) = _ffn_bwd(
        dx3, df2, x2, saved3, norm3_g, sc3, wg2_t, wu2_t, wd2, "ffn2", traffic, below=(mix, g2))
    dyn = _mm(dmix, w_out, "nt", F32, "mix_out_dyn")
    dw_out = _mm(yn, dmix, "tn", BF16, "mix_out_dw")
    datt, dcv, dgroup_g = _gnorm_bwd(dyn, att, cv, group_g, "gnorm_bwd")
    db, dc, dxc, dconv_w = _conv_bwd(dcv, bcx, conv_w, "conv_bwd")
    dbcx = jnp.concatenate([db, dc, dxc], axis=1)
    dq, dk, dv, qx, kx = _attn_bwd(qa, ka, va, datt, att, lse, "attn_bwd", exchange=traffic.reduce_chips("ffn2"))
    traffic.sum_chips("ffn2")
    dqkv = jnp.concatenate([dq.astype(BF16), dk, dv], axis=1)
    df_t = _decay_grads(qx, kx, "decay_grads")[:, :HEAD_ROWS].T
    dflog_t, dbias_col = _forget_bwd(df_t, flog_t, bias_col, "forget_bwd")
    dflog = jnp.pad(dflog_t[:n_heads].T, ((0, 0), (0, LANES - n_heads))).astype(BF16)
    dh2 = _mm([dqkv, dbcx, dflog], [wqkv_t, wbcx_t, wf_t], "nn", F32, "mix_dh", exchange=traffic.share("ffn2"))
    dwqkv_t = _mm(dqkv, h2, "tn", BF16, "mix_dw_qkv")
    dwbcx_t = _mm(dbcx, h2, "tn", BF16, "mix_dw_bcx")
    dwf_t = _mm(dflog, h2, "tn", BF16, "mix_dw_f")
    dw_in_t = jnp.concatenate([dwqkv_t, dwf_t[:n_heads], dwbcx_t], axis=0).reshape(N_CHIPS, in_shard, d)
    dw_in_t = jnp.pad(dw_in_t, ((0, 0), (0, in_rows - in_shard), (0, 0))).reshape(N_CHIPS * in_rows, d)
    dx1, dsh2, dsc2, dnorm2_g, df1, dgate1 = _norm_mod_bwd(
        dh2, x1, norm2_g, sc2, dx2, "mix_norm_bwd", f=saved1[4], gate=gate1,
        exchange=traffic.reduce_sibling("mix", [dw_in_t, dw_out]))
    traffic.add_halves("mix")

    def share_mix():
        traffic.sum_chips("mix")
        return traffic.share("mix")

    dx, (dsh1, dsc1, dnorm1_g), _ = _ffn_bwd(
        dx1, df1, x, saved1, norm1_g, sc1, wg1_t, wu1_t, wd1, "ffn1", traffic,
        dact_exchange=traffic.reduce_chips("mix"), dw_exchange=share_mix, finish_reduction=True)

    dmod = [dsh1, dsc1, 0.5 * dgate1, dsh2, dsc2, dg2, dsh3, dsc3, 0.5 * dgate3]
    dgains = [dnorm1_g, dnorm2_g, dnorm3_g, dfinal_g, dgroup_g]
    dbias = dbias_col[:n_heads, 0]
    return dx, loss_row, dmod, dgains, dbias, dconv_w


SMALL_ROWS = 24
ROW_GAINS, ROW_LOSS, ROW_FORGET, ROW_CONV, ROW_MOD = 0, 5, 6, 7, 10
PROW_ADA_B, PROW_GAINS, PROW_FORGET, PROW_CONV = 0, 9, 14, 15


def _round_up(n, m):
    return -(-n // m) * m


def _pad_rows(a, rows):
    return jnp.pad(a, ((0, rows - a.shape[0]), (0, 0)))


def _halves(a):
    return a.reshape(2, a.shape[0] // 2, a.shape[1])


def _rows_at(a, r0, total, width):
    return jnp.pad(a, ((r0, total - r0 - a.shape[0]), (0, width - a.shape[1])))


def kernel(x, c, ada_w, ada_b, norm1_g, ffn1_w_gate, ffn1_w_up, ffn1_w_down, norm2_g, w_in, forget_bias, conv_w, group_norm_g, w_out, norm3_g, ffn2_w_gate, ffn2_w_up, ffn2_w_down, final_g, loss_target, m_ada_w, m_ada_b, m_norm1_g, m_ffn1_w_gate, m_ffn1_w_up, m_ffn1_w_down, m_norm2_g, m_w_in, m_forget_bias, m_conv_w, m_group_norm_g, m_w_out, m_norm3_g, m_ffn2_w_gate, m_ffn2_w_up, m_ffn2_w_down, m_final_g, v_ada_w, v_ada_b, v_norm1_g, v_ffn1_w_gate, v_ffn1_w_up, v_ffn1_w_down, v_norm2_g, v_w_in, v_forget_bias, v_conv_w, v_group_norm_g, v_w_out, v_norm3_g, v_ffn2_w_gate, v_ffn2_w_up, v_ffn2_w_down, v_final_g):
    xi, yi, ci = lax.axis_index("x"), lax.axis_index("y"), lax.axis_index("c")
    chip = 2 * xi + yi
    dev = 4 * xi + 2 * yi + ci
    _, s, d = x.shape
    att_w = d // 2
    conv_width = d - att_w
    n_heads = att_w // HEAD_DIM
    in_shard = w_in.shape[1]
    in_rows = _round_up(in_shard, 32)
    cs = conv_w.shape[1]
    mod_shard = ada_w.shape[1]
    assert N_MOD * d == N_CHIPS * mod_shard and conv_width == N_CHIPS * cs and n_heads % 2 == 0

    def t_bf(w):
        return w.T.astype(BF16)

    shards = {"ffn1_gu": [_halves(t_bf(ffn1_w_gate)), _halves(t_bf(ffn1_w_up))], "ffn1_d": [_halves(ffn1_w_down.astype(BF16))],
              "mix": [_halves(_pad_rows(t_bf(w_in), in_rows)), _halves(w_out.astype(BF16))],
              "ffn2": [_halves(t_bf(ffn2_w_gate)), _halves(t_bf(ffn2_w_up)), _halves(ffn2_w_down.astype(BF16))]}
    core = ci.astype(jnp.int32).reshape(1)
    chip_arr = chip.astype(jnp.int32).reshape(1)
    traffic = _WeightTraffic(shards, core, chip_arr)

    cond = _small_gather_exchange(_rows_at(c, 0, 8, d) + _rows_at(conv_w, 1, 8, d))
    _run_exchange(_join(traffic.gather_ici("ffn1_gu"), cond), "gather_ffn1_ici")
    got0 = cond.results[0].reshape(N_DEV, 8, d)
    c16 = _pad_rows(got0[:, 0, :], 16)
    conv_full = got0[0::2, 1:1 + CONV_K, :cs].transpose(1, 0, 2).reshape(CONV_K, conv_width)

    ada_b_mine = lax.dynamic_slice(ada_b, (chip * mod_shard,), (mod_shard,))[None, :]
    mods = _small_gather_exchange(_ada_fwd(c16, ada_w, ada_b_mine, "ada_fwd"))
    _run_exchange(_join(traffic.gather_pass("ffn1_gu"), mods), "gather_ffn1_pass")
    got1 = mods.results[0].reshape(N_DEV, 16, mod_shard)
    mod_mine = lax.dynamic_index_in_dim(got1[0::2], dev, axis=1, keepdims=False).reshape(N_MOD, d)
    mod = [mod_mine[i:i + 1] for i in range(N_MOD)]

    gains = [g[None, :] for g in (norm1_g, norm2_g, norm3_g, final_g, group_norm_g)]
    dx, loss_row, dmod, dgains, dbias, dconv_w = _layer_step(
        x[0], loss_target[0], mod, gains, forget_bias, conv_full, traffic, att_w, in_shard, in_rows)

    pack = sum(_rows_at(g, ROW_GAINS + i, SMALL_ROWS, d) for i, g in enumerate(dgains))
    pack += _rows_at(loss_row, ROW_LOSS, SMALL_ROWS, d) + _rows_at(dbias[None, :], ROW_FORGET, SMALL_ROWS, d)
    pack += _rows_at(dconv_w, ROW_CONV, SMALL_ROWS, d)
    pack += sum(_rows_at(g, ROW_MOD + i, SMALL_ROWS, d) for i, g in enumerate(dmod))
    small = _small_gather_exchange(pack)
    _run_exchange(_join(traffic.share("ffn1_wu"), small), "gather_small_grads")
    got2 = small.results[0].reshape(N_DEV, SMALL_ROWS, d)
    tot = _sum_devices(got2, "sum_small_grads")
    loss = tot[ROW_LOSS, 0]
    grad_ada_b = tot[ROW_MOD:ROW_MOD + N_MOD].reshape(N_MOD * d)
    grad_conv = lax.dynamic_slice(tot[ROW_CONV:ROW_CONV + CONV_K], (0, chip * cs), (CONV_K, cs))
    dmod_all = got2[:, ROW_MOD:ROW_MOD + N_MOD, :].reshape(N_DEV, N_MOD * d)
    dmod16 = _pad_rows(lax.dynamic_slice(dmod_all, (0, chip * mod_shard), (N_DEV, mod_shard)), 16)

    out = {"ada_w": tuple(_ada_update(c16.T, dmod16, ada_w, m_ada_w, v_ada_w, "adamw_ada_w"))}
    totals = (traffic.totals("ffn1_wg") + traffic.totals("ffn1_wu") + traffic.totals("ffn1_wd")
              + traffic.totals("mix") + traffic.totals("ffn2"))

    names = ("ffn1_w_gate", "ffn1_w_up", "ffn1_w_down", "w_in", "w_out", "ffn2_w_gate", "ffn2_w_up", "ffn2_w_down")
    transposed = ("ffn1_w_gate", "ffn1_w_up", "w_in", "ffn2_w_gate", "ffn2_w_up")
    params = {"ffn1_w_gate": (ffn1_w_gate, m_ffn1_w_gate, v_ffn1_w_gate), "ffn1_w_up": (ffn1_w_up, m_ffn1_w_up, v_ffn1_w_up),
              "ffn1_w_down": (ffn1_w_down, m_ffn1_w_down, v_ffn1_w_down), "w_in": (w_in, m_w_in, v_w_in),
              "w_out": (w_out, m_w_out, v_w_out), "ffn2_w_gate": (ffn2_w_gate, m_ffn2_w_gate, v_ffn2_w_gate),
              "ffn2_w_up": (ffn2_w_up, m_ffn2_w_up, v_ffn2_w_up), "ffn2_w_down": (ffn2_w_down, m_ffn2_w_down, v_ffn2_w_down)}
    for name_, (mine, theirs) in zip(names, totals):
        w, m, v = params[name_]
        if name_ in transposed:
            w, m, v = w.T, m.T, v.T
        if name_ == "w_in":
            both = jnp.where(ci == 0, jnp.concatenate([mine, theirs]), jnp.concatenate([theirs, mine]))[:in_shard]
            res = (both,) + tuple(_adamw(w, both, m, v, "adamw_" + name_))
        else:
            res = _adamw_halves(w, mine, theirs, m, v, core, "adamw_" + name_)
        out[name_] = tuple(r.T for r in res) if name_ in transposed else tuple(res)

    def small_pack(ada_b_, gains_, forget_, conv_):
        p = _rows_at(ada_b_.reshape(N_MOD, d), PROW_ADA_B, SMALL_ROWS, d)
        p += sum(_rows_at(g[None, :], PROW_GAINS + i, SMALL_ROWS, d) for i, g in enumerate(gains_))
        p += _rows_at(forget_[None, :], PROW_FORGET, SMALL_ROWS, d) + _rows_at(conv_, PROW_CONV, SMALL_ROWS, d)
        return p

    g_gains = [tot[ROW_GAINS + i] for i in range(5)]
    g_forget = tot[ROW_FORGET, :n_heads]
    sw = small_pack(ada_b, (norm1_g, norm2_g, norm3_g, final_g, group_norm_g), forget_bias, conv_w)
    sm = small_pack(m_ada_b, (m_norm1_g, m_norm2_g, m_norm3_g, m_final_g, m_group_norm_g), m_forget_bias, m_conv_w)
    sv = small_pack(v_ada_b, (v_norm1_g, v_norm2_g, v_norm3_g, v_final_g, v_group_norm_g), v_forget_bias, v_conv_w)
    sg = small_pack(grad_ada_b, g_gains, g_forget, grad_conv)
    small = (sg,) + tuple(_adamw(sw, sg, sm, sv, "adamw_small"))

    def unpack(p):
        r = {"ada_b": p[PROW_ADA_B:PROW_ADA_B + N_MOD].reshape(N_MOD * d), "forget_bias": p[PROW_FORGET, :n_heads],
             "conv_w": p[PROW_CONV:PROW_CONV + CONV_K, :cs]}
        for i, nm in enumerate(("norm1_g", "norm2_g", "norm3_g", "final_g", "group_norm_g")):
            r[nm] = p[PROW_GAINS + i]
        return r

    small = [unpack(p) for p in small]
    order = ("ada_w", "ada_b", "norm1_g", "ffn1_w_gate", "ffn1_w_up", "ffn1_w_down", "norm2_g", "w_in", "forget_bias",
             "conv_w", "group_norm_g", "w_out", "norm3_g", "ffn2_w_gate", "ffn2_w_up", "ffn2_w_down", "final_g")
    result = [loss, dx[None]]
    for k in range(4):
        result += [out[nm][k] if nm in out else small[k][nm] for nm in order]
    return tuple(result)
```

```python
import functools
import math

import jax
import jax.numpy as jnp
from jax import lax
from jax.experimental import pallas as pl
from jax.experimental.pallas import tpu as pltpu

F32 = jnp.float32
BF16 = jnp.bfloat16

HEAD_DIM = 64
CONV_K = 3
N_MOD = 9
EPS = 1e-6
ADAM_LR = 0.001
ADAM_B1 = 0.9
ADAM_B2 = 0.999
ADAM_EPS = 1e-08
ADAM_WD = 0.01
ADAM_STEP = 10

LANES = 128
N_CHIPS = 4
N_DEV = 8
VMEM_LIMIT_BYTES = 56 * 1024 * 1024
MAX_CONTRACTION = 4096
NEG_BIG = -1e30
MESH = pl.DeviceIdType.MESH

_NT = (((1,), (1,)), ((), ()))
_NN = (((1,), (0,)), ((), ()))
_TN = (((0,), (0,)), ((), ()))


def _params(*sem):
    return pltpu.CompilerParams(dimension_semantics=sem, vmem_limit_bytes=VMEM_LIMIT_BYTES)


class _Exchange:
    def __init__(self, inputs, out_shapes, n_sems, start, finish, aliases=None):
        self.inputs, self.out_shapes, self.n_sems = list(inputs), list(out_shapes), n_sems
        self.start, self.finish, self.aliases = start, finish, dict(aliases or {})
        self.results = None

    def set_results(self, results):
        self.results = list(results)


class _SemaphoreWindow:
    def __init__(self, sems, base):
        self.sems, self.base = sems, base
        self.at = self

    def __getitem__(self, k):
        return self.sems.at[self.base + k]


class _JoinedExchange(_Exchange):
    def __init__(self, parts):
        self.parts = parts
        aliases, i0, o0 = {}, 0, 0
        for p in parts:
            aliases.update({i0 + a: o0 + b for a, b in p.aliases.items()})
            i0, o0 = i0 + len(p.inputs), o0 + len(p.out_shapes)

        def each(method, src, dst, send_sems, recv_sems):
            i0 = o0 = s0 = 0
            for p in parts:
                i1, o1 = i0 + len(p.inputs), o0 + len(p.out_shapes)
                getattr(p, method)(src[i0:i1], dst[o0:o1], _SemaphoreWindow(send_sems, s0), _SemaphoreWindow(recv_sems, s0))
                i0, o0, s0 = i1, o1, s0 + p.n_sems

        super().__init__([a for p in parts for a in p.inputs], [o for p in parts for o in p.out_shapes],
                         sum(p.n_sems for p in parts), functools.partial(each, "start"), functools.partial(each, "finish"),
                         aliases)

    def set_results(self, results):
        o0 = 0
        for p in self.parts:
            p.set_results(results[o0:o0 + len(p.out_shapes)])
            o0 += len(p.out_shapes)


def _join(*parts):
    return parts[0] if len(parts) == 1 else _JoinedExchange(list(parts))


def _pc(body, exchange=None, **kw):
    if exchange is None:
        return pl.pallas_call(body, **kw)
    grid = kw["grid"]
    single = not isinstance(kw["out_shape"], (tuple, list))
    out_shape = [kw["out_shape"]] if single else list(kw["out_shape"])
    out_specs = [kw["out_specs"]] if single else list(kw["out_specs"])
    in_specs = list(kw["in_specs"])
    scratch = list(kw.get("scratch_shapes", ()))
    n_in, n_out, n_scr = len(in_specs), len(out_shape), len(scratch)
    n_xi, n_xo = len(exchange.inputs), len(exchange.out_shapes)

    def wrapped(*refs):
        pos = [n_in, n_in + n_xi, n_in + n_xi + n_out, n_in + n_xi + n_out + n_xo]
        ins, x_in, outs, x_out = refs[:pos[0]], refs[pos[0]:pos[1]], refs[pos[1]:pos[2]], refs[pos[2]:pos[3]]
        scr = refs[pos[3]:pos[3] + n_scr]
        send_sems, recv_sems = refs[pos[3] + n_scr:]
        ids = [pl.program_id(a) for a in range(len(grid))]
        first = functools.reduce(jnp.logical_and, [i == 0 for i in ids])
        last = functools.reduce(jnp.logical_and, [i == g - 1 for i, g in zip(ids, grid)])

        @pl.when(first)
        def _():
            exchange.start(x_in, x_out, send_sems, recv_sems)

        body(*ins, *outs, *scr)

        @pl.when(last)
        def _():
            exchange.finish(x_in, x_out, send_sems, recv_sems)

    call = pl.pallas_call(
        wrapped, out_shape=tuple(out_shape) + tuple(exchange.out_shapes), grid=grid,
        in_specs=in_specs + [_ANY] * n_xi, out_specs=tuple(out_specs) + (_ANY,) * n_xo,
        scratch_shapes=scratch + [pltpu.SemaphoreType.DMA((exchange.n_sems,)), pltpu.SemaphoreType.DMA((exchange.n_sems,))],
        input_output_aliases={n_in + a: n_out + b for a, b in exchange.aliases.items()},
        compiler_params=_params(*(["arbitrary"] * len(grid))), name=kw["name"])

    def run(*args):
        res = call(*args, *exchange.inputs)
        exchange.set_results(res[n_out:])
        return res[0] if single else tuple(res[:n_out])

    return run


_ANY = pl.BlockSpec(memory_space=pl.ANY)


def _tile(n, pref, mult):
    best = None
    t = mult
    while t <= min(n, pref):
        if n % t == 0:
            best = t
        t += mult
    return n if best is None else best


def _sds(shape, dtype):
    return jax.ShapeDtypeStruct(shape, dtype)


def _vec_spec(d):
    return pl.BlockSpec((1, d), lambda *_: (0, 0))


def _norm_mod_fwd(x, g, shift, scale, name):
    s, d = x.shape
    tr = _tile(s, 512, 16)

    def body(x_ref, g_ref, sh_ref, sc_ref, h_ref):
        xv = x_ref[...]
        rstd = lax.rsqrt(jnp.mean(xv * xv, axis=-1, keepdims=True) + EPS)
        n = xv * rstd * g_ref[...]
        h_ref[...] = (n * (1.0 + sc_ref[...]) + sh_ref[...]).astype(BF16)

    row = pl.BlockSpec((tr, d), lambda i: (i, 0))
    return _pc(body, out_shape=_sds((s, d), BF16), grid=(s // tr,),
               in_specs=[row, _vec_spec(d), _vec_spec(d), _vec_spec(d)], out_specs=row,
               compiler_params=_params("parallel"), name=name)(x, g, shift, scale)


def _through_gate(dx, f_ref, gate_ref, df_ref, dgate_ref):
    df_ref[...] = (dx * gate_ref[...]).astype(BF16)
    dgate_ref[...] += jnp.sum(dx * f_ref[...].astype(F32), axis=0, keepdims=True)


def _norm_mod_bwd(dh, x, g, scale, dres, name, f=None, gate=None, exchange=None):
    s, d = x.shape
    gated = f is not None
    terms = list(zip(*dh)) if isinstance(dh, tuple) else None
    tr = _tile(s, 256, 16)
    n_lead = 2 * len(terms) if terms else 1

    def body(*refs):
        lead, (x_ref, g_ref, sc_ref, dres_ref), rest = refs[:n_lead], refs[n_lead:n_lead + 4], refs[n_lead + 4:]
        f_ref, gate_ref = rest[:2] if gated else (None, None)
        dx_ref, dsh_ref, dsc_ref, dg_ref = rest[2:6] if gated else rest[:4]
        df_ref, dgate_ref = rest[6:8] if gated else (None, None)

        @pl.when(pl.program_id(0) == 0)
        def _():
            for ref in (dsh_ref, dsc_ref, dg_ref) + ((dgate_ref,) if gated else ()):
                ref[...] = jnp.zeros_like(ref)

        if terms:
            dhv = lax.dot_general(lead[0][...], lead[1][...], _NN, preferred_element_type=F32)
            for p in range(1, len(terms)):
                dhv += lax.dot_general(lead[2 * p][...], lead[2 * p + 1][...], _NN, preferred_element_type=F32)
        else:
            dhv = lead[0][...]
        xv = x_ref[...]
        gv = g_ref[...]
        rstd = lax.rsqrt(jnp.mean(xv * xv, axis=-1, keepdims=True) + EPS)
        xhat = xv * rstd
        dn = dhv * (1.0 + sc_ref[...])
        dsh_ref[...] += jnp.sum(dhv, axis=0, keepdims=True)
        dsc_ref[...] += jnp.sum(dhv * (xhat * gv), axis=0, keepdims=True)
        dg_ref[...] += jnp.sum(dn * xhat, axis=0, keepdims=True)
        dxh = dn * gv
        proj = jnp.mean(dxh * xhat, axis=-1, keepdims=True)
        dx = dres_ref[...] + rstd * (dxh - xhat * proj)
        dx_ref[...] = dx
        if gated:
            _through_gate(dx, f_ref, gate_ref, df_ref, dgate_ref)

    row = pl.BlockSpec((tr, d), lambda i: (i, 0))
    vec = _vec_spec(d)
    if terms:
        in_specs, args = [], []
        for l, r in terms:
            assert l.shape[1] == r.shape[0] <= MAX_CONTRACTION and r.shape[1] == d
            in_specs += [pl.BlockSpec((tr, l.shape[1]), lambda i: (i, 0)), pl.BlockSpec(r.shape, lambda i: (0, 0))]
            args += [l, r]
    else:
        in_specs, args = [row], [dh]
    in_specs += [row, vec, vec, row]
    args += [x, g, scale, dres]
    out_shape = [_sds((s, d), F32), _sds((1, d), F32), _sds((1, d), F32), _sds((1, d), F32)]
    out_specs = [row, vec, vec, vec]
    if gated:
        out_shape += [_sds((s, d), BF16), _sds((1, d), F32)]
        out_specs += [row, vec]
        in_specs += [row, vec]
        args += [f, gate]
    return _pc(body, exchange, out_shape=tuple(out_shape), grid=(s // tr,), in_specs=in_specs,
               out_specs=tuple(out_specs), compiler_params=_params("arbitrary"), name=name)(*args)


def _final_loss(x, g, target, f, gate, name):
    s, d = x.shape
    tr = _tile(s, 256, 16)
    nsteps = s // tr

    def body(x_ref, g_ref, t_ref, f_ref, gate_ref, dx_ref, loss_ref, dg_ref, df_ref, dgate_ref):
        i = pl.program_id(0)

        @pl.when(i == 0)
        def _():
            loss_ref[...] = jnp.zeros_like(loss_ref)
            dg_ref[...] = jnp.zeros_like(dg_ref)
            dgate_ref[...] = jnp.zeros_like(dgate_ref)

        xv = x_ref[...]
        gv = g_ref[...]
        rstd = lax.rsqrt(jnp.mean(xv * xv, axis=-1, keepdims=True) + EPS)
        xhat = xv * rstd
        err = xhat * gv - t_ref[...]
        dy = err * (1.0 / d)
        loss_ref[...] += jnp.sum(0.5 * err * dy, axis=0, keepdims=True)
        dg_ref[...] += jnp.sum(dy * xhat, axis=0, keepdims=True)
        dxh = dy * gv
        proj = jnp.mean(dxh * xhat, axis=-1, keepdims=True)
        dx = rstd * (dxh - xhat * proj)
        dx_ref[...] = dx
        _through_gate(dx, f_ref, gate_ref, df_ref, dgate_ref)

        @pl.when(i == nsteps - 1)
        def _():
            loss_ref[...] = jnp.broadcast_to(jnp.sum(loss_ref[...], axis=-1, keepdims=True), loss_ref.shape)

    row = pl.BlockSpec((tr, d), lambda i: (i, 0))
    vec = _vec_spec(d)
    return _pc(body, out_shape=(_sds((s, d), F32), _sds((1, d), F32), _sds((1, d), F32), _sds((s, d), BF16), _sds((1, d), F32)),
               grid=(nsteps,), in_specs=[row, vec, row, row, vec], out_specs=(row, vec, vec, row, vec),
               compiler_params=_params("arbitrary"), name=name)(x, g, target, f, gate)


def _mm(lhs, rhs, dims, out_dtype, name, res=None, gate=None, aux_dtype=None, exchange=None):
    lhs_list = list(lhs) if isinstance(lhs, (list, tuple)) else [lhs]
    rhs_list = list(rhs) if isinstance(rhs, (list, tuple)) else [rhs]
    n_terms = len(lhs_list)
    assert n_terms == len(rhs_list)
    m = lhs_list[0].shape[1 if dims == "tn" else 0]
    n = rhs_list[0].shape[0 if dims == "nt" else 1]
    tn = _tile(n, 1024, LANES)
    tm = _tile(m, 512, LANES if dims == "tn" else 16)
    dn = {"nn": _NN, "nt": _NT, "tn": _TN}[dims]
    in_specs, args = [], []
    for l, r in zip(lhs_list, rhs_list):
        k = l.shape[0 if dims == "tn" else 1]
        assert k == r.shape[1 if dims == "nt" else 0] and k <= MAX_CONTRACTION, (l.shape, r.shape, dims)
        in_specs.append(pl.BlockSpec((k, tm), lambda i, j: (0, i)) if dims == "tn" else pl.BlockSpec((tm, k), lambda i, j: (i, 0)))
        in_specs.append(pl.BlockSpec((tn, k), lambda i, j: (j, 0)) if dims == "nt" else pl.BlockSpec((k, tn), lambda i, j: (0, j)))
        args += [l, r]
    out_spec = pl.BlockSpec((tm, tn), lambda i, j: (i, j))
    has_res, has_gate, has_aux = res is not None, gate is not None, aux_dtype is not None

    def body(*refs):
        refs = list(refs)
        pos = 2 * n_terms
        res_ref = gate_ref = aux_ref = None
        if has_res:
            res_ref = refs[pos]; pos += 1
        if has_gate:
            gate_ref = refs[pos]; pos += 1
        out_ref = refs[pos]; pos += 1
        if has_aux:
            aux_ref = refs[pos]
        acc = lax.dot_general(refs[0][...], refs[1][...], dn, preferred_element_type=F32)
        for p in range(1, n_terms):
            acc += lax.dot_general(refs[2 * p][...], refs[2 * p + 1][...], dn, preferred_element_type=F32)
        if has_aux:
            aux_ref[...] = acc.astype(aux_dtype)
        if has_gate:
            acc = acc * gate_ref[...]
        if has_res:
            acc = res_ref[...] + acc
        out_ref[...] = acc.astype(out_dtype)

    if has_res:
        in_specs.append(out_spec); args.append(res)
    if has_gate:
        in_specs.append(pl.BlockSpec((1, tn), lambda i, j: (0, j))); args.append(gate)
    out_shape = [_sds((m, n), out_dtype)]
    out_specs = [out_spec]
    if has_aux:
        out_shape.append(_sds((m, n), aux_dtype)); out_specs.append(out_spec)
    outs = _pc(body, exchange, out_shape=tuple(out_shape), grid=(m // tm, n // tn), in_specs=in_specs,
               out_specs=tuple(out_specs), compiler_params=_params("parallel", "parallel"), name=name)(*args)
    return outs if has_aux else outs[0]


def _ffn_up(h, wg_t, wu_t, name, exchange=None):
    s, d = h.shape
    f = wg_t.shape[0]
    tm = _tile(s, 1024, 16)
    tn = _tile(f, 256, LANES)

    def body(h_ref, wg_ref, wu_ref, a_ref, u_ref, hid_ref):
        hv = h_ref[...]
        a = lax.dot_general(hv, wg_ref[...], _NT, preferred_element_type=F32)
        u = lax.dot_general(hv, wu_ref[...], _NT, preferred_element_type=F32)
        a_ref[...] = a.astype(BF16)
        u_ref[...] = u.astype(BF16)
        hid_ref[...] = (a * jax.nn.sigmoid(a) * u).astype(BF16)

    hs = pl.BlockSpec((tm, d), lambda i, j: (i, 0))
    ws = pl.BlockSpec((tn, d), lambda i, j: (j, 0))
    os_ = pl.BlockSpec((tm, tn), lambda i, j: (i, j))
    return _pc(body, exchange, out_shape=(_sds((s, f), BF16),) * 3, grid=(s // tm, f // tn),
               in_specs=[hs, ws, ws], out_specs=(os_, os_, os_),
               compiler_params=_params("parallel", "parallel"), name=name)(h, wg_t, wu_t)


def _ffn_dact(df, wd, a, u, name, exchange=None):
    s, d = df.shape
    f = wd.shape[0]
    tm = _tile(s, 1024, 16)
    tn = _tile(f, 256, LANES)

    def body(df_ref, wd_ref, a_ref, u_ref, da_ref, du_ref):
        dhid = lax.dot_general(df_ref[...], wd_ref[...], _NT, preferred_element_type=F32)
        av = a_ref[...].astype(F32)
        uv = u_ref[...].astype(F32)
        sig = jax.nn.sigmoid(av)
        da_ref[...] = (dhid * uv * (sig * (1.0 + av * (1.0 - sig)))).astype(BF16)
        du_ref[...] = (dhid * (av * sig)).astype(BF16)

    ds_ = pl.BlockSpec((tm, d), lambda i, j: (i, 0))
    ws = pl.BlockSpec((tn, d), lambda i, j: (j, 0))
    os_ = pl.BlockSpec((tm, tn), lambda i, j: (i, j))
    return _pc(body, exchange, out_shape=(_sds((s, f), BF16),) * 2, grid=(s // tm, f // tn),
               in_specs=[ds_, ws, os_, os_], out_specs=(os_, os_),
               compiler_params=_params("parallel", "parallel"), name=name)(df, wd, a, u)


def _split3(v):
    hi = v.astype(BF16)
    r1 = v - hi.astype(F32)
    mid = r1.astype(BF16)
    lo = (r1 - mid.astype(F32)).astype(BF16)
    return hi, mid, lo


def _dot3(v, mat):
    hi, mid, lo = _split3(v)
    out = lax.dot_general(hi, mat, _NN, preferred_element_type=F32)
    out += lax.dot_general(mid, mat, _NN, preferred_element_type=F32)
    out += lax.dot_general(lo, mat, _NN, preferred_element_type=F32)
    return out


def _forget_fwd(flog_t, bias, name):
    h, s = flog_t.shape
    blk = _tile(s, 512, LANES)
    tri = (jnp.arange(blk)[:, None] <= jnp.arange(blk)[None, :]).astype(BF16)

    def body(z_ref, b_ref, tri_ref, f_ref, carry):
        @pl.when(pl.program_id(0) == 0)
        def _():
            carry[...] = jnp.zeros_like(carry)

        z = z_ref[...] + b_ref[...]
        e = jnp.exp(-jnp.abs(z))
        w = 1.0 + e
        log1p_e = jnp.where(w == 1.0, e, jnp.log(w) * (e / (w - 1.0)))
        lf = jnp.minimum(z, 0.0) - log1p_e
        out = carry[...] + _dot3(lf, tri_ref[...])
        for j, piece in enumerate(_split3(out)):
            f_ref[j] = piece
        carry[...] = out[:, blk - 1:blk]

    zs = pl.BlockSpec((h, blk), lambda i: (0, i))
    return _pc(body, out_shape=_sds((3, h, s), BF16), grid=(s // blk,),
               in_specs=[zs, pl.BlockSpec((h, 1), lambda i: (0, 0)), pl.BlockSpec((blk, blk), lambda i: (0, 0))],
               out_specs=pl.BlockSpec((3, h, blk), lambda i: (0, 0, i)), scratch_shapes=[pltpu.VMEM((h, 1), F32)],
               compiler_params=_params("arbitrary"), name=name)(flog_t, bias, tri)


def _forget_bwd(df_t, flog_t, bias, name):
    h, s = flog_t.shape
    blk = _tile(s, 512, LANES)
    nb = s // blk
    tri = (jnp.arange(blk)[:, None] >= jnp.arange(blk)[None, :]).astype(BF16)

    def body(df_ref, z_ref, b_ref, tri_ref, dz_ref, db_ref, carry):
        @pl.when(pl.program_id(0) == 0)
        def _():
            carry[...] = jnp.zeros_like(carry)
            db_ref[...] = jnp.zeros_like(db_ref)

        rc = carry[...] + _dot3(df_ref[...], tri_ref[...])
        carry[...] = rc[:, 0:1]
        dz = rc * jax.nn.sigmoid(-(z_ref[...] + b_ref[...]))
        dz_ref[...] = dz
        db_ref[...] += jnp.sum(dz, axis=-1, keepdims=True)

    rev = pl.BlockSpec((h, blk), lambda i: (0, nb - 1 - i))
    col = pl.BlockSpec((h, 1), lambda i: (0, 0))
    return _pc(body, out_shape=(_sds((h, s), F32), _sds((h, 1), F32)), grid=(nb,),
               in_specs=[rev, rev, col, pl.BlockSpec((blk, blk), lambda i: (0, 0))],
               out_specs=(rev, col), scratch_shapes=[pltpu.VMEM((h, 1), F32)],
               compiler_params=_params("arbitrary"), name=name)(df_t, flog_t, bias, tri)


def _attn_tiles(s):
    return _tile(s, 1024, LANES)


def _attn_half(t):
    return t // 2 if t >= 4 * LANES else t


BIAS_ROWS = 16


def _attn_prep(qkv, f_pieces, name):
    s = qkv.shape[0]
    a_w = qkv.shape[1] // 3
    npair = a_w // LANES
    t = _attn_tiles(s)
    scale = 1.0 / math.sqrt(HEAD_DIM)

    six = f_pieces[:, :2 * npair].reshape(3, npair, 2, s).transpose(1, 3, 2, 0).reshape(npair, s, 6)
    feat = jnp.concatenate([six, jnp.ones((npair, s, 1), BF16), jnp.zeros((npair, s, BIAS_ROWS - 7), BF16)], axis=-1)
    place_q = [[0.0] * (2 * LANES) for _ in range(BIAS_ROWS)]
    place_k = [[0.0] * (2 * LANES) for _ in range(BIAS_ROWS)]
    for hh in range(2):
        b0 = hh * LANES + (HEAD_DIM if hh == 0 else 0)
        for j in range(3):
            place_q[3 * hh + j][b0 + j] = 1.0
            place_q[6][b0 + 3 + j] = 1.0
            place_k[6][b0 + j] = 1.0
            place_k[3 * hh + j][b0 + 3 + j] = -1.0
    place_q = jnp.array(place_q, BF16)
    place_k = jnp.array(place_k, BF16)

    def body(q_ref, k_ref, v_ref, f_ref, pq_ref, pk_ref, qa_ref, ka_ref, va_ref):
        lane = lax.broadcasted_iota(jnp.int32, (1, LANES), 1)
        q2 = (q_ref[...].astype(F32) * scale).astype(BF16)
        k2, v2 = k_ref[...], v_ref[...]
        qx = lax.dot_general(f_ref[0], pq_ref[...], _NN, preferred_element_type=F32).astype(BF16)
        kx = lax.dot_general(f_ref[0], pk_ref[...], _NN, preferred_element_type=F32).astype(BF16)
        for hh in range(2):
            real = (lane < HEAD_DIM) if hh == 0 else (lane >= HEAD_DIM)
            cols = slice(hh * LANES, (hh + 1) * LANES)
            qa_ref[:, cols] = jnp.where(real, q2, qx[:, cols])
            ka_ref[:, cols] = jnp.where(real, k2, kx[:, cols])
            va_ref[:, cols] = jnp.where(real, v2, jnp.zeros_like(v2))

    def col(off):
        return pl.BlockSpec((t, LANES), lambda p, i: (i, off + p))

    out = pl.BlockSpec((t, 2 * LANES), lambda p, i: (i, p))
    place = pl.BlockSpec((BIAS_ROWS, 2 * LANES), lambda p, i: (0, 0))
    return _pc(body, out_shape=(_sds((s, 2 * a_w), BF16),) * 3, grid=(npair, s // t),
               in_specs=[col(0), col(npair), col(2 * npair), pl.BlockSpec((1, t, BIAS_ROWS), lambda p, i: (p, i, 0)),
                         place, place],
               out_specs=(out, out, out), compiler_params=_params("parallel", "parallel"), name=name)(
                   qkv, qkv, qkv, feat, place_q, place_k)


def _attn_fwd(qa, ka, va, name, exchange=None):
    s = qa.shape[0]
    a_w = qa.shape[1] // 2
    npair = a_w // LANES
    t = _attn_tiles(s)
    nq = s // t
    half = _attn_half(t)

    def body(q_ref, k_ref, v_ref, o_ref, lse_ref, m_sc, l_sc, acc_sc):
        qi = pl.program_id(1)
        first = lax.broadcasted_iota(jnp.int32, (1, LANES), 1) < HEAD_DIM
        m_sc[...] = jnp.full_like(m_sc, NEG_BIG)
        l_sc[...] = jnp.zeros_like(l_sc)
        acc_sc[...] = jnp.zeros_like(acc_sc)

        def step(q0, k_start, size, diag):
            q_sl = slice(q0, q0 + size)
            k_rows = pl.ds(pl.multiple_of(k_start, size), size)
            m_old = m_sc[q_sl, :]
            keep = None
            if diag:
                keep = (lax.broadcasted_iota(jnp.int32, (size, size), 0) >= lax.broadcasted_iota(jnp.int32, (size, size), 1))
            m_new, rs, pv = [], [], []
            for hh in range(2):
                cols = slice(hh * LANES, (hh + 1) * LANES)
                sc = lax.dot_general(q_ref[q_sl, cols], k_ref[k_rows, cols], _NT, preferred_element_type=F32)
                if diag:
                    sc = jnp.where(keep, sc, NEG_BIG)
                mo = m_old[:, hh * HEAD_DIM:hh * HEAD_DIM + 1]
                mn = jnp.maximum(mo, jnp.max(sc, axis=1, keepdims=True))
                p = jnp.exp(sc - mn)
                m_new.append(mn)
                rs.append(jnp.sum(p, axis=1, keepdims=True))
                pv.append(lax.dot_general(p.astype(BF16), v_ref[k_rows, cols], _NN, preferred_element_type=F32))
            m2 = jnp.where(first, m_new[0], m_new[1])
            alpha = jnp.exp(m_old - m2)
            m_sc[q_sl, :] = m2
            l_sc[q_sl, :] = alpha * l_sc[q_sl, :] + jnp.where(first, rs[0], rs[1])
            acc_sc[q_sl, :] = alpha * acc_sc[q_sl, :] + pv[0] + pv[1]

        def below_diagonal(ki, carry):
            step(0, ki * t, t, False)
            return carry

        lax.fori_loop(0, qi, below_diagonal, 0)
        step(0, qi * t, half, True)
        if half < t:
            step(half, qi * t, half, False)
            step(half, qi * t + half, half, True)
        l2 = l_sc[...]
        o_ref[...] = acc_sc[...] / l2
        lse_ref[...] = m_sc[...] + jnp.log(l2)

    qs = pl.BlockSpec((t, 2 * LANES), lambda p, qi: (qi, p))
    ks = pl.BlockSpec((s, 2 * LANES), lambda p, qi: (0, p))
    os_ = pl.BlockSpec((t, LANES), lambda p, qi: (qi, p))
    return _pc(body, exchange, out_shape=(_sds((s, a_w), F32), _sds((s, a_w), F32)), grid=(npair, nq),
               in_specs=[qs, ks, ks], out_specs=(os_, os_),
               scratch_shapes=[pltpu.VMEM((t, LANES), F32)] * 3,
               compiler_params=_params("parallel", "arbitrary"), name=name)(qa, ka, va)


def _attn_bwd(qa, ka, va, do, o, lse, name, exchange=None):
    s = qa.shape[0]
    a_w = qa.shape[1] // 2
    npair = a_w // LANES
    t = _attn_tiles(s)
    nq = s // t
    half = _attn_half(t)
    scale = 1.0 / math.sqrt(HEAD_DIM)

    def body(q_ref, k_ref, v_ref, do_ref, o_ref, lse_ref, dq_ref, dk_ref, dv_ref, qx_ref, kx_ref, dk_sc, dv_sc, kx_sc):
        ki = pl.program_id(1)
        first = lax.broadcasted_iota(jnp.int32, (1, LANES), 1) < HEAD_DIM

        @pl.when(ki == 0)
        def _():
            dq_ref[...] = jnp.zeros_like(dq_ref)
            qx_ref[...] = jnp.zeros_like(qx_ref)

        def step(q_start, k0, size, diag, assign):
            rows = pl.ds(pl.multiple_of(q_start, size), size)
            k_sl = slice(k0, k0 + size)
            do2 = do_ref[rows, :]
            lse2 = lse_ref[rows, :]
            dd = do2.astype(F32) * o_ref[rows, :]
            keep = None
            if diag:
                keep = (lax.broadcasted_iota(jnp.int32, (size, size), 0) >= lax.broadcasted_iota(jnp.int32, (size, size), 1))
            dq_h, dk_h, dv_h = [], [], []
            for hh in range(2):
                sel = first if hh == 0 else jnp.logical_not(first)
                cols = slice(hh * LANES, (hh + 1) * LANES)
                qh, kh, vh = q_ref[rows, cols], k_ref[k_sl, cols], v_ref[k_sl, cols]
                delta = jnp.sum(jnp.where(sel, dd, 0.0), axis=1, keepdims=True)
                sc = lax.dot_general(qh, kh, _NT, preferred_element_type=F32)
                if diag:
                    sc = jnp.where(keep, sc, NEG_BIG)
                p = jnp.exp(sc - lse2[:, hh * HEAD_DIM:hh * HEAD_DIM + 1])
                dp = lax.dot_general(do2, vh, _NT, preferred_element_type=F32)
                ds_b = (p * (dp - delta)).astype(BF16)
                dv_h.append(lax.dot_general(p.astype(BF16), do2, _TN, preferred_element_type=F32))
                dk_h.append(lax.dot_general(ds_b, qh, _TN, preferred_element_type=F32))
                dq_h.append(lax.dot_general(ds_b, kh, _NN, preferred_element_type=F32))
            dq_ref[rows, :] += jnp.where(first, dq_h[0], dq_h[1]) * scale
            qx_ref[rows, :] += jnp.where(first, dq_h[1], dq_h[0])
            dk_new = jnp.where(first, dk_h[0], dk_h[1])
            kx_new = jnp.where(first, dk_h[1], dk_h[0])
            dv_new = jnp.where(first, dv_h[0], dv_h[1])
            if assign:
                dk_sc[k_sl, :] = dk_new
                kx_sc[k_sl, :] = kx_new
                dv_sc[k_sl, :] = dv_new
            else:
                dk_sc[k_sl, :] += dk_new
                kx_sc[k_sl, :] += kx_new
                dv_sc[k_sl, :] += dv_new

        def below_diagonal(qi, carry):
            step(qi * t, 0, t, False, False)
            return carry

        step(ki * t, 0, half, True, True)
        if half < t:
            step(ki * t + half, 0, half, False, False)
            step(ki * t + half, half, half, True, True)
        lax.fori_loop(ki + 1, nq, below_diagonal, 0)
        dk_ref[...] = dk_sc[...].astype(BF16)
        dv_ref[...] = dv_sc[...].astype(BF16)
        kx_ref[...] = kx_sc[...]

    ks2 = pl.BlockSpec((t, 2 * LANES), lambda p, ki: (ki, p))
    qs2 = pl.BlockSpec((s, 2 * LANES), lambda p, ki: (0, p))
    whole = pl.BlockSpec((s, LANES), lambda p, ki: (0, p))
    kout = pl.BlockSpec((t, LANES), lambda p, ki: (ki, p))
    return _pc(body, exchange,
               out_shape=(_sds((s, a_w), F32), _sds((s, a_w), BF16), _sds((s, a_w), BF16), _sds((s, a_w), F32),
                          _sds((s, a_w), F32)),
               grid=(npair, nq), in_specs=[qs2, ks2, ks2, whole, whole, whole],
               out_specs=(whole, kout, kout, whole, kout),
               scratch_shapes=[pltpu.VMEM((t, LANES), F32)] * 3,
               compiler_params=_params("parallel", "arbitrary"), name=name)(qa, ka, va, do, o, lse)

def _decay_grads(qx, kx, name):
    s, a_w = qx.shape
    n_heads = a_w // HEAD_DIM
    tr = _tile(s, 512, 8)
    pick_q = [[0.0] * LANES for _ in range(a_w)]
    pick_k = [[0.0] * LANES for _ in range(a_w)]
    for h in range(n_heads):
        b0 = (h // 2) * LANES + (HEAD_DIM if h % 2 == 0 else 0)
        pick_q[b0][h] = 1.0
        pick_k[b0 + 3][h] = 1.0
    pick_q = jnp.array(pick_q, BF16)
    pick_k = jnp.array(pick_k, BF16)

    def body(qx_ref, kx_ref, pq_ref, pk_ref, o_ref):
        o_ref[...] = _dot3(qx_ref[...], pq_ref[...]) - _dot3(kx_ref[...], pk_ref[...])

    row = pl.BlockSpec((tr, a_w), lambda i: (i, 0))
    pick = pl.BlockSpec((a_w, LANES), lambda i: (0, 0))
    return _pc(body, out_shape=_sds((s, LANES), F32), grid=(s // tr,), in_specs=[row, row, pick, pick],
               out_specs=pl.BlockSpec((tr, LANES), lambda i: (i, 0)),
               compiler_params=_params("parallel"), name=name)(qx, kx, pick_q, pick_k)


def _shift_down(z, k, rows):
    return jnp.where(rows >= k, pltpu.roll(z, k, 0), 0.0)


def _shift_up(z, k, rows, n):
    return jnp.where(rows < n - k, pltpu.roll(z, n - k, 0), 0.0)


def _conv_fwd(bcx, conv_w, name):
    s = bcx.shape[0]
    cw = bcx.shape[1] // 3
    nb = cw // LANES

    def body(b_ref, c_ref, x_ref, w_ref, cv_ref):
        rows = lax.broadcasted_iota(jnp.int32, (s, LANES), 0)
        z = c_ref[...] * x_ref[...]
        w = w_ref[...]
        y = w[2:3, :] * z + w[1:2, :] * _shift_down(z, 1, rows) + w[0:1, :] * _shift_down(z, 2, rows)
        cv_ref[...] = b_ref[...] * y

    def col(off):
        return pl.BlockSpec((s, LANES), lambda j: (0, j + off))

    return _pc(body, out_shape=_sds((s, cw), F32), grid=(nb,),
               in_specs=[col(0), col(nb), col(2 * nb), pl.BlockSpec((CONV_K, LANES), lambda j: (0, j))],
               out_specs=col(0), compiler_params=_params("parallel"), name=name)(bcx, bcx, bcx, conv_w)


def _conv_bwd(dcv, bcx, conv_w, name):
    s = bcx.shape[0]
    cw = bcx.shape[1] // 3
    nb = cw // LANES

    def body(dcv_ref, b_ref, c_ref, x_ref, w_ref, db_ref, dc_ref, dxc_ref, dw_ref):
        rows = lax.broadcasted_iota(jnp.int32, (s, LANES), 0)
        cv_, xv = c_ref[...], x_ref[...]
        z = cv_ * xv
        w = w_ref[...]
        z1 = _shift_down(z, 1, rows)
        z2 = _shift_down(z, 2, rows)
        y = w[2:3, :] * z + w[1:2, :] * z1 + w[0:1, :] * z2
        dcvv = dcv_ref[...]
        db_ref[...] = (dcvv * y).astype(BF16)
        dy = dcvv * b_ref[...]
        dw_ref[0:1, :] = jnp.sum(dy * z2, axis=0, keepdims=True)
        dw_ref[1:2, :] = jnp.sum(dy * z1, axis=0, keepdims=True)
        dw_ref[2:3, :] = jnp.sum(dy * z, axis=0, keepdims=True)
        dz = w[2:3, :] * dy + w[1:2, :] * _shift_up(dy, 1, rows, s) + w[0:1, :] * _shift_up(dy, 2, rows, s)
        dc_ref[...] = (dz * xv).astype(BF16)
        dxc_ref[...] = (dz * cv_).astype(BF16)

    def col(off):
        return pl.BlockSpec((s, LANES), lambda j: (0, j + off))

    wspec = pl.BlockSpec((CONV_K, LANES), lambda j: (0, j))
    db, dc, dxc, dw = _pc(body, out_shape=(_sds((s, cw), BF16),) * 3 + (_sds((CONV_K, cw), F32),), grid=(nb,),
                          in_specs=[col(0), col(0), col(nb), col(2 * nb), wspec],
                          out_specs=(col(0), col(0), col(0), wspec),
                          compiler_params=_params("parallel"), name=name)(dcv, bcx, bcx, bcx, conv_w)
    return db, dc, dxc, dw


def _group_matrix():
    idx = jnp.arange(LANES) // HEAD_DIM
    return (idx[:, None] == idx[None, :]).astype(BF16)


def _group_sum(v, gmat):
    return _dot3(v, gmat)


def _gnorm_fwd(att, cv, gg, name):
    s, a_w = att.shape
    cw = cv.shape[1]
    d = a_w + cw
    tr = _tile(s, 512, 16)
    gmat = _group_matrix()

    def body(att_ref, cv_ref, gg_ref, gm_ref, yn_ref):
        gm = gm_ref[...]
        for c0 in range(0, d, LANES):
            y = att_ref[:, c0:c0 + LANES] if c0 < a_w else cv_ref[:, c0 - a_w:c0 - a_w + LANES]
            ms = _group_sum(y * y, gm) * (1.0 / HEAD_DIM)
            yn_ref[:, c0:c0 + LANES] = (y * lax.rsqrt(ms + EPS) * gg_ref[:, c0:c0 + LANES]).astype(BF16)

    return _pc(body, out_shape=_sds((s, d), BF16), grid=(s // tr,),
               in_specs=[pl.BlockSpec((tr, a_w), lambda i: (i, 0)), pl.BlockSpec((tr, cw), lambda i: (i, 0)),
                         _vec_spec(d), pl.BlockSpec((LANES, LANES), lambda i: (0, 0))],
               out_specs=pl.BlockSpec((tr, d), lambda i: (i, 0)),
               compiler_params=_params("parallel"), name=name)(att, cv, gg, gmat)


def _gnorm_bwd(dyn, att, cv, gg, name):
    s, a_w = att.shape
    cw = cv.shape[1]
    d = a_w + cw
    tr = _tile(s, 256, 16)
    gmat = _group_matrix()

    def body(dyn_ref, att_ref, cv_ref, gg_ref, gm_ref, datt_ref, dcv_ref, dgg_ref):
        @pl.when(pl.program_id(0) == 0)
        def _():
            dgg_ref[...] = jnp.zeros_like(dgg_ref)

        gm = gm_ref[...]
        for c0 in range(0, d, LANES):
            y = att_ref[:, c0:c0 + LANES] if c0 < a_w else cv_ref[:, c0 - a_w:c0 - a_w + LANES]
            dv = dyn_ref[:, c0:c0 + LANES]
            r = lax.rsqrt(_group_sum(y * y, gm) * (1.0 / HEAD_DIM) + EPS)
            xhat = y * r
            dgg_ref[:, c0:c0 + LANES] += jnp.sum(dv * xhat, axis=0, keepdims=True)
            dxh = dv * gg_ref[:, c0:c0 + LANES]
            proj = _group_sum(dxh * xhat, gm) * (1.0 / HEAD_DIM)
            dy = r * (dxh - xhat * proj)
            if c0 < a_w:
                datt_ref[:, c0:c0 + LANES] = dy.astype(BF16)
            else:
                dcv_ref[:, c0 - a_w:c0 - a_w + LANES] = dy

    return _pc(body, out_shape=(_sds((s, a_w), BF16), _sds((s, cw), F32), _sds((1, d), F32)), grid=(s // tr,),
               in_specs=[pl.BlockSpec((tr, d), lambda i: (i, 0)), pl.BlockSpec((tr, a_w), lambda i: (i, 0)),
                         pl.BlockSpec((tr, cw), lambda i: (i, 0)), _vec_spec(d),
                         pl.BlockSpec((LANES, LANES), lambda i: (0, 0))],
               out_specs=(pl.BlockSpec((tr, a_w), lambda i: (i, 0)), pl.BlockSpec((tr, cw), lambda i: (i, 0)),
                          _vec_spec(d)),
               compiler_params=_params("arbitrary"), name=name)(dyn, att, cv, gg, gmat)


def _adamw_math(w, g, m, v):
    m_new = ADAM_B1 * m + (1.0 - ADAM_B1) * g
    v_new = ADAM_B2 * v + (1.0 - ADAM_B2) * (g * g)
    m_hat = m_new / (1.0 - ADAM_B1 ** ADAM_STEP)
    v_hat = v_new / (1.0 - ADAM_B2 ** ADAM_STEP)
    delta = -ADAM_LR * (m_hat / (jnp.sqrt(v_hat) + ADAM_EPS) + ADAM_WD * w)
    return delta, m_new, v_new


def _row_tile(r, c):
    return _tile(r, max(8, ((1 << 18) // c) // 8 * 8), 8)


def _adamw(w, g, m, v, name):
    r, c = w.shape
    tr = _row_tile(r, c)

    def body(w_ref, g_ref, m_ref, v_ref, d_ref, mo_ref, vo_ref):
        d, mn, vn = _adamw_math(w_ref[...], g_ref[...], m_ref[...], v_ref[...])
        d_ref[...] = d
        mo_ref[...] = mn
        vo_ref[...] = vn

    spec = pl.BlockSpec((tr, c), lambda i: (i, 0))
    return _pc(body, out_shape=(_sds((r, c), F32),) * 3, grid=(r // tr,), in_specs=[spec] * 4,
               out_specs=(spec,) * 3, compiler_params=_params("parallel"), name=name)(w, g, m, v)


def _adamw_halves(w, mine, theirs, m, v, core, name):
    r2, c = w.shape
    r = r2 // 2
    assert mine.shape == (r, c) and theirs.shape == (r, c)
    tr = _row_tile(r, c)
    nb = r // tr

    def body(core_ref, w_ref, a_ref, b_ref, m_ref, v_ref, g_ref, d_ref, mo_ref, vo_ref):
        g = jnp.where(pl.program_id(0) == core_ref[0], a_ref[...], b_ref[...])
        d, mn, vn = _adamw_math(w_ref[...], g, m_ref[...], v_ref[...])
        g_ref[...] = g
        d_ref[...] = d
        mo_ref[...] = mn
        vo_ref[...] = vn

    full = pl.BlockSpec((tr, c), lambda h, i, core_ref: (h * nb + i, 0))
    half = pl.BlockSpec((tr, c), lambda h, i, core_ref: (i, 0))
    grid_spec = pltpu.PrefetchScalarGridSpec(
        num_scalar_prefetch=1, grid=(2, nb), in_specs=[full, half, half, full, full], out_specs=(full,) * 4)
    return _pc(body, out_shape=(_sds((r2, c), F32),) * 4, grid_spec=grid_spec,
               compiler_params=_params("parallel", "parallel"), name=name)(core, w, mine, theirs, m, v)


def _ada_fwd(c16, ada_w, ada_b, name):
    d, n = ada_w.shape
    tn = _tile(n, 768, LANES)

    def body(c_ref, w_ref, b_ref, o_ref):
        cv = c_ref[...]
        sc = (cv * jax.nn.sigmoid(cv)).astype(BF16)
        o_ref[...] = lax.dot_general(sc, w_ref[...].astype(BF16), _NN, preferred_element_type=F32) + b_ref[...]

    return _pc(body, out_shape=_sds((16, n), F32), grid=(n // tn,),
               in_specs=[pl.BlockSpec((16, d), lambda j: (0, 0)), pl.BlockSpec((d, tn), lambda j: (0, j)),
                         pl.BlockSpec((1, tn), lambda j: (0, j))],
               out_specs=pl.BlockSpec((16, tn), lambda j: (0, j)),
               compiler_params=_params("parallel"), name=name)(c16, ada_w, ada_b)


def _ada_update(c16_t, dmod16, w, m, v, name, exchange=None):
    r, c = w.shape
    tr = _row_tile(r, c)

    def body(c_ref, dm_ref, w_ref, m_ref, v_ref, g_ref, d_ref, mo_ref, vo_ref):
        cv = c_ref[...]
        sc = (cv * jax.nn.sigmoid(cv)).astype(BF16)
        g = lax.dot_general(sc, dm_ref[...].astype(BF16), _NN, preferred_element_type=F32)
        d, mn, vn = _adamw_math(w_ref[...], g, m_ref[...], v_ref[...])
        g_ref[...] = g
        d_ref[...] = d
        mo_ref[...] = mn
        vo_ref[...] = vn

    spec = pl.BlockSpec((tr, c), lambda i: (i, 0))
    return _pc(body, exchange, out_shape=(_sds((r, c), F32),) * 4, grid=(r // tr,),
               in_specs=[pl.BlockSpec((tr, 16), lambda i: (i, 0)), pl.BlockSpec((16, c), lambda i: (0, 0)),
                         spec, spec, spec],
               out_specs=(spec,) * 4, compiler_params=_params("parallel"), name=name)(c16_t, dmod16, w, m, v)


def _add_half(dw, recv, core, name):
    _, _, r, w = dw.shape
    tr = _tile(r, 512, 16)

    def body(core_ref, a_ref, b_ref, o_ref):
        o_ref[...] = (a_ref[...].astype(F32) + b_ref[...].astype(F32)).astype(BF16)

    grid_spec = pltpu.PrefetchScalarGridSpec(
        num_scalar_prefetch=1, grid=(N_CHIPS, r // tr),
        in_specs=[pl.BlockSpec((None, None, tr, w), lambda s, i, core_ref: (s, core_ref[0], i, 0)),
                  pl.BlockSpec((None, tr, w), lambda s, i, core_ref: (s, i, 0))],
        out_specs=pl.BlockSpec((None, tr, w), lambda s, i, core_ref: (s, i, 0)))
    return _pc(body, out_shape=_sds((N_CHIPS, r, w), BF16), grid_spec=grid_spec,
               compiler_params=_params("parallel", "parallel"), name=name)(core, dw, recv)


def _sum_chips(own, recv, chip, name):
    _, r, w = own.shape
    tr = _tile(r, 512, 16)

    def body(chip_ref, own_ref, p_ref, o_ref):
        acc = own_ref[...].astype(F32)
        for q in range(N_CHIPS - 1):
            acc = acc + p_ref[q].astype(F32)
        o_ref[...] = acc

    grid_spec = pltpu.PrefetchScalarGridSpec(
        num_scalar_prefetch=1, grid=(r // tr,),
        in_specs=[pl.BlockSpec((None, tr, w), lambda i, chip_ref: (chip_ref[0], i, 0)),
                  pl.BlockSpec((N_CHIPS - 1, tr, w), lambda i, chip_ref: (0, i, 0))],
        out_specs=pl.BlockSpec((tr, w), lambda i, chip_ref: (i, 0)))
    return _pc(body, out_shape=_sds((r, w), F32), grid_spec=grid_spec,
               compiler_params=_params("parallel"), name=name)(chip, own, recv)


def _sum_devices(parts, name):
    nd, r, w = parts.shape

    def body(p_ref, o_ref):
        acc = p_ref[0]
        for q in range(1, nd):
            acc = acc + p_ref[q]
        o_ref[...] = acc

    return _pc(body, out_shape=_sds((r, w), F32), name=name)(parts)


def _place():
    x, y, c = lax.axis_index("x"), lax.axis_index("y"), lax.axis_index("c")
    chips = [(1 - x, y), (x, 1 - y), (1 - x, 1 - y)]
    return x, y, c, chips


def _small_gather_exchange(blk):
    r, w = blk.shape

    def copies(src, dst, send_sems, recv_sems):
        x, y, c, chips = _place()
        me, sibling = (x, y, c), (x, y, 1 - c)

        def rows(px, py, pc):
            return dst[0].at[pl.ds((4 * px + 2 * py + pc) * r, r), :]

        def copy(k, block, to, own=False):
            return _remote(src[0] if own else rows(*block), rows(*block), send_sems, recv_sems, k, to)

        mine = pltpu.make_async_copy(src[0], rows(*me), send_sems.at[7])
        first = [copy(0, me, sibling, own=True)] + [copy(1 + j, me, (*chip, c), own=True) for j, chip in enumerate(chips)]
        passed = [copy(4 + j, (*chip, c), sibling) for j, chip in enumerate(chips)]
        landed = [copy(1 + j, (*chip, c), me) for j, chip in enumerate(chips)]
        from_sibling = [copy(0, sibling, me)] + [copy(4 + j, (*chip, 1 - c), me) for j, chip in enumerate(chips)]
        return mine, first, passed, landed, from_sibling

    def start(src, dst, send_sems, recv_sems):
        mine, first, _, _, _ = copies(src, dst, send_sems, recv_sems)
        mine.start()
        for cp in first:
            cp.start()

    def finish(src, dst, send_sems, recv_sems):
        mine, first, passed, landed, from_sibling = copies(src, dst, send_sems, recv_sems)
        for arrival, onward in zip(landed, passed):
            arrival.wait_recv()
            onward.start()
        for cp in from_sibling:
            cp.wait_recv()
        for cp in first + passed:
            cp.wait_send()
        mine.wait()

    return _Exchange([blk], [_sds((N_DEV * r, w), blk.dtype)], 8, start, finish)


def _remote(src, dst, send_sems, recv_sems, k, to):
    return pltpu.make_async_remote_copy(src_ref=src, dst_ref=dst, send_sem=send_sems.at[k], recv_sem=recv_sems.at[k],
                                        device_id=to, device_id_type=MESH)


def _exchange_of(inputs, out_shapes, n_sems, copies, aliases=None):
    def start(src, dst, send_sems, recv_sems):
        for cp in copies(src, dst, send_sems, recv_sems)[0]:
            cp.start()

    def finish(src, dst, send_sems, recv_sems):
        sends, arrivals = copies(src, dst, send_sems, recv_sems)
        for cp in arrivals:
            cp.wait_recv()
        for cp in sends:
            cp.wait_send()

    return _Exchange(inputs, out_shapes, n_sems, start, finish, aliases)


def _run_exchange(ex, name):
    n_in, n_out = len(ex.inputs), len(ex.out_shapes)

    def body(*refs):
        src, dst = refs[:n_in], refs[n_in:n_in + n_out]
        send_sems, recv_sems = refs[n_in + n_out:]
        ex.start(src, dst, send_sems, recv_sems)
        ex.finish(src, dst, send_sems, recv_sems)

    ex.set_results(pl.pallas_call(
        body, out_shape=tuple(ex.out_shapes), in_specs=[_ANY] * n_in, out_specs=(_ANY,) * n_out,
        scratch_shapes=[pltpu.SemaphoreType.DMA((ex.n_sems,)), pltpu.SemaphoreType.DMA((ex.n_sems,))],
        input_output_aliases=ex.aliases, name=name)(*ex.inputs))


def _gather_ici_exchange(shards):
    n = len(shards)

    def copies(own, out, send_sems, recv_sems):
        x, y, c, chips = _place()
        my_chip = 2 * x + y
        sends, arrivals = [], []
        for i in range(n):
            for j, chip in enumerate(chips):
                to = (*chip, c)
                sends.append(_remote(own[i].at[c], out[i].at[my_chip, c], send_sems, recv_sems, 4 * i + j, to))
                arrivals.append(_remote(own[i].at[c], out[i].at[2 * chip[0] + chip[1], c], send_sems, recv_sems, 4 * i + j, to))
            whole = _remote(own[i], out[i].at[my_chip], send_sems, recv_sems, 4 * i + 3, (x, y, 1 - c))
            sends.append(whole)
            arrivals.append(whole)
        return sends, arrivals

    return _exchange_of(shards, [_sds((N_CHIPS,) + s.shape, s.dtype) for s in shards], 4 * n, copies)


def _gather_pass_exchange(gathered):
    n = len(gathered)

    def copies(src, dst, send_sems, recv_sems):
        x, y, c, chips = _place()
        sends, arrivals = [], []
        for i in range(n):
            for j, chip in enumerate(chips):
                idx = 2 * chip[0] + chip[1]
                sends.append(_remote(src[i].at[idx, c], dst[i].at[idx, c], send_sems, recv_sems, 3 * i + j, (x, y, 1 - c)))
                arrivals.append(_remote(src[i].at[idx, c], dst[i].at[idx, 1 - c], send_sems, recv_sems, 3 * i + j, (x, y, 1 - c)))
        return sends, arrivals

    return _exchange_of(gathered, [_sds(g.shape, g.dtype) for g in gathered], 3 * n, copies,
                        aliases={i: i for i in range(n)})


def _reduce_sibling_exchange(grads):
    n = len(grads)

    def copies(src, dst, send_sems, recv_sems):
        x, y, c, _ = _place()
        both = [_remote(src[i].at[s, 1 - c], dst[i].at[s], send_sems, recv_sems, N_CHIPS * i + s, (x, y, 1 - c))
                for i in range(n) for s in range(N_CHIPS)]
        return both, both

    return _exchange_of(grads, [_sds((N_CHIPS,) + g.shape[2:], g.dtype) for g in grads], N_CHIPS * n, copies)


def _reduce_chips_exchange(parts):
    n = len(parts)

    def copies(src, dst, send_sems, recv_sems):
        x, y, c, chips = _place()
        both = [_remote(src[i].at[2 * chip[0] + chip[1]], dst[i].at[j], send_sems, recv_sems, 3 * i + j, (*chip, c))
                for i in range(n) for j, chip in enumerate(chips)]
        return both, both

    return _exchange_of(parts, [_sds((N_CHIPS - 1,) + p.shape[1:], p.dtype) for p in parts], 3 * n, copies)


def _share_exchange(halves):
    n = len(halves)

    def copies(src, dst, send_sems, recv_sems):
        x, y, c, _ = _place()
        both = [_remote(src[i], dst[i], send_sems, recv_sems, i, (x, y, 1 - c)) for i in range(n)]
        return both, both

    return _exchange_of(halves, [_sds(h.shape, h.dtype) for h in halves], n, copies)


HEAD_ROWS = 16


class _WeightTraffic:
    def __init__(self, shards, core, chip):
        self.shards, self.core, self.chip = shards, core, chip
        self.gather, self.grads, self.reduce, self.chip_sums, self.half_sums, self.shared = {}, {}, {}, {}, {}, {}

    def gather_ici(self, grp):
        self.gather[grp] = _gather_ici_exchange(self.shards[grp])
        return self.gather[grp]

    def gather_pass(self, grp):
        self.gather[grp] = _gather_pass_exchange(self.gather[grp].results)
        return self.gather[grp]

    def weights(self, grp):
        return [g.reshape(-1, g.shape[-1]) for g in self.gather[grp].results]

    def reduce_sibling(self, grp, grads):
        self.grads[grp] = [g.reshape(N_CHIPS, 2, g.shape[0] // (2 * N_CHIPS), g.shape[1]) for g in grads]
        self.reduce[grp] = _reduce_sibling_exchange(self.grads[grp])
        return self.reduce[grp]

    def add_halves(self, grp):
        self.chip_sums[grp] = [_add_half(g, r, self.core, "add_half_%s%d" % (grp, i))
                               for i, (g, r) in enumerate(zip(self.grads[grp], self.reduce[grp].results))]

    def reduce_chips(self, grp):
        self.reduce[grp] = _reduce_chips_exchange(self.chip_sums[grp])
        return self.reduce[grp]

    def sum_chips(self, grp):
        self.half_sums[grp] = [_sum_chips(o, p, self.chip, "sum_chips_%s%d" % (grp, i))
                               for i, (o, p) in enumerate(zip(self.chip_sums[grp], self.reduce[grp].results))]

    def share(self, grp):
        self.shared[grp] = _share_exchange(self.half_sums[grp])
        return self.shared[grp]

    def totals(self, grp):
        return list(zip(self.half_sums[grp], self.shared[grp].results))


def _ffn_fwd(x, norm_g, shift, scale, gate, wg_t, wu_t, wd, tag, up_exchange=None, down_exchange=None):
    h = _norm_mod_fwd(x, norm_g, shift, scale, tag + "_norm_fwd")
    a, u, hid = _ffn_up(h, wg_t, wu_t, tag + "_up", exchange=up_exchange)
    wd = wd() if callable(wd) else wd
    x_out, f = _mm(hid, wd, "nn", F32, tag + "_down", res=x, gate=gate, aux_dtype=BF16,
                   exchange=down_exchange() if down_exchange else None)
    return x_out, (h, a, u, hid, f)


def _ffn_bwd(dx_out, df, x, saved, norm_g, scale, wg_t, wu_t, wd, tag, traffic, below=None, dact_exchange=None,
             dw_exchange=None, finish_reduction=False):
    h, a, u, hid, _ = saved
    f_below, gate_below = below if below else (None, None)
    da, du = _ffn_dact(df, wd, a, u, tag + "_dact", exchange=dact_exchange)
    dwd = _mm(hid, df, "tn", BF16, tag + "_dwd", exchange=dw_exchange() if dw_exchange else None)
    if not finish_reduction:
        dwg_t = _mm(da, h, "tn", BF16, tag + "_dwg")
        dwu_t = _mm(du, h, "tn", BF16, tag + "_dwu")
        dx, dshift, dscale, dnorm_g, *gated = _norm_mod_bwd(
            ([da, du], [wg_t, wu_t]), x, norm_g, scale, dx_out, tag + "_dh_norm_bwd", f=f_below, gate=gate_below,
            exchange=traffic.reduce_sibling(tag, [dwg_t, dwu_t, dwd]))
        traffic.add_halves(tag)
        return dx, (dshift, dscale, dnorm_g), gated
    kd, kg, ku = tag + "_wd", tag + "_wg", tag + "_wu"
    dwg_t = _mm(da, h, "tn", BF16, tag + "_dwg", exchange=traffic.reduce_sibling(kd, [dwd]))
    traffic.add_halves(kd)
    dwu_t = _mm(du, h, "tn", BF16, tag + "_dwu",
                exchange=_join(traffic.reduce_chips(kd), traffic.reduce_sibling(kg, [dwg_t])))
    traffic.add_halves(kg)
    dh = _mm([da, du], [wg_t, wu_t], "nn", F32, tag + "_dh",
             exchange=_join(traffic.reduce_chips(kg), traffic.reduce_sibling(ku, [dwu_t])))
    traffic.add_halves(ku)
    traffic.sum_chips(kd)
    traffic.sum_chips(kg)
    dx, dshift, dscale, dnorm_g, *gated = _norm_mod_bwd(
        dh, x, norm_g, scale, dx_out, tag + "_norm_bwd", f=f_below, gate=gate_below,
        exchange=_join(traffic.reduce_chips(ku), traffic.share(kd), traffic.share(kg)))
    traffic.sum_chips(ku)
    return dx, (dshift, dscale, dnorm_g), gated


def _layer_step(x, target, mod, gains, forget_bias, conv_w, traffic, att_w, in_shard, in_rows):
    sh1, sc1, g1, sh2, sc2, g2, sh3, sc3, g3 = mod
    norm1_g, norm2_g, norm3_g, final_g, group_g = gains
    s, d = x.shape
    n_heads = att_w // HEAD_DIM
    npair = n_heads // 2
    gate1, gate3 = 0.5 * g1, 0.5 * g3

    def split_w_in(w_in_pad):
        w_in_t = w_in_pad.reshape(N_CHIPS, in_rows, d)[:, :in_shard].reshape(N_CHIPS * in_shard, d)
        return (w_in_t[:3 * att_w], _pad_rows(w_in_t[3 * att_w:3 * att_w + n_heads], LANES), w_in_t[3 * att_w + n_heads:])

    wg1_t, wu1_t = traffic.weights("ffn1_gu")

    def wd1_ready():
        _run_exchange(traffic.gather_pass("ffn1_d"), "gather_ffn1_down_pass")
        return traffic.weights("ffn1_d")[0]

    x1, saved1 = _ffn_fwd(x, norm1_g, sh1, sc1, gate1, wg1_t, wu1_t, wd1_ready, "ffn1",
                          up_exchange=_join(traffic.gather_ici("ffn1_d"), traffic.gather_ici("mix")),
                          down_exchange=lambda: traffic.gather_pass("mix"))
    wd1 = traffic.weights("ffn1_d")[0]
    w_in_pad, w_out = traffic.weights("mix")
    wqkv_t, wf_t, wbcx_t = split_w_in(w_in_pad)

    h2 = _norm_mod_fwd(x1, norm2_g, sh2, sc2, "mix_norm_fwd")
    qkv = _mm(h2, wqkv_t, "nt", BF16, "mix_proj_qkv")
    bcx = _mm(h2, wbcx_t, "nt", F32, "mix_proj_bcx")
    flog = _mm(h2, wf_t, "nt", F32, "mix_proj_f")
    flog_t = jnp.pad(flog[:, :n_heads].T, ((0, HEAD_ROWS - n_heads), (0, 0)))
    bias_col = jnp.pad(forget_bias, (0, HEAD_ROWS - n_heads))[:, None]
    f_pieces = _forget_fwd(flog_t, bias_col, "forget_fwd")
    qa, ka, va = _attn_prep(qkv, f_pieces, "attn_prep")
    att, lse = _attn_fwd(qa, ka, va, "attn_fwd", exchange=traffic.gather_ici("ffn2"))
    cv = _conv_fwd(bcx, conv_w, "conv_fwd")
    yn = _gnorm_fwd(att, cv, group_g, "gnorm_fwd")
    x2, mix = _mm(yn, w_out, "nn", F32, "mix_out", res=x1, gate=g2, aux_dtype=BF16, exchange=traffic.gather_pass("ffn2"))
    wg2_t, wu2_t, wd2 = traffic.weights("ffn2")

    x3, saved3 = _ffn_fwd(x2, norm3_g, sh3, sc3, gate3, wg2_t, wu2_t, wd2, "ffn2")

    dx3, loss_row, dfinal_g, df2, dgate3 = _final_loss(x3, final_g, target, saved3[4], gate3, "final_loss")

    dx2, (dsh3, dsc3, dnorm3_g), (dmix, dg2) = _ffn_bwd(
        dx3, df2, x2, saved3, norm3_g, sc3, wg2_t, wu2_t, wd2, "ffn2", traffic, below=(mix, g2))
    dyn = _mm(dmix, w_out, "nt", F32, "mix_out_dyn")
    dw_out = _mm(yn, dmix, "tn", BF16, "mix_out_dw")
    datt, dcv, dgroup_g = _gnorm_bwd(dyn, att, cv, group_g, "gnorm_bwd")
    db, dc, dxc, dconv_w = _conv_bwd(dcv, bcx, conv_w, "conv_bwd")
    dbcx = jnp.concatenate([db, dc, dxc], axis=1)
    dq, dk, dv, qx, kx = _attn_bwd(qa, ka, va, datt, att, lse, "attn_bwd", exchange=traffic.reduce_chips("ffn2"))
    traffic.sum_chips("ffn2")
    dqkv = jnp.concatenate([dq.astype(BF16), dk, dv], axis=1)
    df_t = _decay_grads(qx, kx, "decay_grads")[:, :HEAD_ROWS].T
    dflog_t, dbias_col = _forget_bwd(df_t, flog_t, bias_col, "forget_bwd")
    dflog = jnp.pad(dflog_t[:n_heads].T, ((0, 0), (0, LANES - n_heads))).astype(BF16)
    dwqkv_t = _mm(dqkv, h2, "tn", BF16, "mix_dw_qkv", exchange=traffic.share("ffn2"))
    dwbcx_t = _mm(dbcx, h2, "tn", BF16, "mix_dw_bcx")
    dwf_t = _mm(dflog, h2, "tn", BF16, "mix_dw_f")
    dw_in_t = jnp.concatenate([dwqkv_t, dwf_t[:n_heads], dwbcx_t], axis=0).reshape(N_CHIPS, in_shard, d)
    dw_in_t = jnp.pad(dw_in_t, ((0, 0), (0, in_rows - in_shard), (0, 0))).reshape(N_CHIPS * in_rows, d)
    dx1, dsh2, dsc2, dnorm2_g, df1, dgate1 = _norm_mod_bwd(
        ([dqkv, dbcx, dflog], [wqkv_t, wbcx_t, wf_t]), x1, norm2_g, sc2, dx2, "mix_dh_norm_bwd", f=saved1[4], gate=gate1,
        exchange=traffic.reduce_sibling("mix", [dw_in_t, dw_out]))
    traffic.add_halves("mix")

    def share_mix():
        traffic.sum_chips("mix")
        return traffic.share("mix")

    dx, (dsh1, dsc1, dnorm1_g), _ = _ffn_bwd(
        dx1, df1, x, saved1, norm1_g, sc1, wg1_t, wu1_t, wd1, "ffn1", traffic,
        dact_exchange=traffic.reduce_chips("mix"), dw_exchange=share_mix, finish_reduction=True)

    dmod = [dsh1, dsc1, 0.5 * dgate1, dsh2, dsc2, dg2, dsh3, dsc3, 0.5 * dgate3]
    dgains = [dnorm1_g, dnorm2_g, dnorm3_g, dfinal_g, dgroup_g]
    dbias = dbias_col[:n_heads, 0]
    return dx, loss_row, dmod, dgains, dbias, dconv_w


SMALL_ROWS = 24
ROW_GAINS, ROW_LOSS, ROW_FORGET, ROW_CONV, ROW_MOD = 0, 5, 6, 7, 10
PROW_ADA_B, PROW_GAINS, PROW_FORGET, PROW_CONV = 0, 9, 14, 15


def _round_up(n, m):
    return -(-n // m) * m


def _pad_rows(a, rows):
    return jnp.pad(a, ((0, rows - a.shape[0]), (0, 0)))


def _halves(a):
    return a.reshape(2, a.shape[0] // 2, a.shape[1])


def _rows_at(a, r0, total, width):
    return jnp.pad(a, ((r0, total - r0 - a.shape[0]), (0, width - a.shape[1])))


def kernel(x, c, ada_w, ada_b, norm1_g, ffn1_w_gate, ffn1_w_up, ffn1_w_down, norm2_g, w_in, forget_bias, conv_w, group_norm_g, w_out, norm3_g, ffn2_w_gate, ffn2_w_up, ffn2_w_down, final_g, loss_target, m_ada_w, m_ada_b, m_norm1_g, m_ffn1_w_gate, m_ffn1_w_up, m_ffn1_w_down, m_norm2_g, m_w_in, m_forget_bias, m_conv_w, m_group_norm_g, m_w_out, m_norm3_g, m_ffn2_w_gate, m_ffn2_w_up, m_ffn2_w_down, m_final_g, v_ada_w, v_ada_b, v_norm1_g, v_ffn1_w_gate, v_ffn1_w_up, v_ffn1_w_down, v_norm2_g, v_w_in, v_forget_bias, v_conv_w, v_group_norm_g, v_w_out, v_norm3_g, v_ffn2_w_gate, v_ffn2_w_up, v_ffn2_w_down, v_final_g):
    xi, yi, ci = lax.axis_index("x"), lax.axis_index("y"), lax.axis_index("c")
    chip = 2 * xi + yi
    dev = 4 * xi + 2 * yi + ci
    _, s, d = x.shape
    att_w = d // 2
    conv_width = d - att_w
    n_heads = att_w // HEAD_DIM
    in_shard = w_in.shape[1]
    in_rows = _round_up(in_shard, 32)
    cs = conv_w.shape[1]
    mod_shard = ada_w.shape[1]
    assert N_MOD * d == N_CHIPS * mod_shard and conv_width == N_CHIPS * cs and n_heads % 2 == 0

    def t_bf(w):
        return w.T.astype(BF16)

    shards = {"ffn1_gu": [_halves(t_bf(ffn1_w_gate)), _halves(t_bf(ffn1_w_up))], "ffn1_d": [_halves(ffn1_w_down.astype(BF16))],
              "mix": [_halves(_pad_rows(t_bf(w_in), in_rows)), _halves(w_out.astype(BF16))],
              "ffn2": [_halves(t_bf(ffn2_w_gate)), _halves(t_bf(ffn2_w_up)), _halves(ffn2_w_down.astype(BF16))]}
    core = ci.astype(jnp.int32).reshape(1)
    chip_arr = chip.astype(jnp.int32).reshape(1)
    traffic = _WeightTraffic(shards, core, chip_arr)

    cond = _small_gather_exchange(_rows_at(c, 0, 8, d) + _rows_at(conv_w, 1, 8, d))
    _run_exchange(_join(traffic.gather_ici("ffn1_gu"), cond), "gather_ffn1_ici")
    got0 = cond.results[0].reshape(N_DEV, 8, d)
    c16 = _pad_rows(got0[:, 0, :], 16)
    conv_full = got0[0::2, 1:1 + CONV_K, :cs].transpose(1, 0, 2).reshape(CONV_K, conv_width)

    ada_b_mine = lax.dynamic_slice(ada_b, (chip * mod_shard,), (mod_shard,))[None, :]
    mods = _small_gather_exchange(_ada_fwd(c16, ada_w, ada_b_mine, "ada_fwd"))
    _run_exchange(_join(traffic.gather_pass("ffn1_gu"), mods), "gather_ffn1_pass")
    got1 = mods.results[0].reshape(N_DEV, 16, mod_shard)
    mod_mine = lax.dynamic_index_in_dim(got1[0::2], dev, axis=1, keepdims=False).reshape(N_MOD, d)
    mod = [mod_mine[i:i + 1] for i in range(N_MOD)]

    gains = [g[None, :] for g in (norm1_g, norm2_g, norm3_g, final_g, group_norm_g)]
    dx, loss_row, dmod, dgains, dbias, dconv_w = _layer_step(
        x[0], loss_target[0], mod, gains, forget_bias, conv_full, traffic, att_w, in_shard, in_rows)

    pack = sum(_rows_at(g, ROW_GAINS + i, SMALL_ROWS, d) for i, g in enumerate(dgains))
    pack += _rows_at(loss_row, ROW_LOSS, SMALL_ROWS, d) + _rows_at(dbias[None, :], ROW_FORGET, SMALL_ROWS, d)
    pack += _rows_at(dconv_w, ROW_CONV, SMALL_ROWS, d)
    pack += sum(_rows_at(g, ROW_MOD + i, SMALL_ROWS, d) for i, g in enumerate(dmod))
    small = _small_gather_exchange(pack)
    _run_exchange(_join(traffic.share("ffn1_wu"), small), "gather_small_grads")
    got2 = small.results[0].reshape(N_DEV, SMALL_ROWS, d)
    tot = _sum_devices(got2, "sum_small_grads")
    loss = tot[ROW_LOSS, 0]
    grad_ada_b = tot[ROW_MOD:ROW_MOD + N_MOD].reshape(N_MOD * d)
    grad_conv = lax.dynamic_slice(tot[ROW_CONV:ROW_CONV + CONV_K], (0, chip * cs), (CONV_K, cs))
    dmod_all = got2[:, ROW_MOD:ROW_MOD + N_MOD, :].reshape(N_DEV, N_MOD * d)
    dmod16 = _pad_rows(lax.dynamic_slice(dmod_all, (0, chip * mod_shard), (N_DEV, mod_shard)), 16)

    out = {"ada_w": tuple(_ada_update(c16.T, dmod16, ada_w, m_ada_w, v_ada_w, "adamw_ada_w"))}
    totals = (traffic.totals("ffn1_wg") + traffic.totals("ffn1_wu") + traffic.totals("ffn1_wd")
              + traffic.totals("mix") + traffic.totals("ffn2"))

    names = ("ffn1_w_gate", "ffn1_w_up", "ffn1_w_down", "w_in", "w_out", "ffn2_w_gate", "ffn2_w_up", "ffn2_w_down")
    transposed = ("ffn1_w_gate", "ffn1_w_up", "w_in", "ffn2_w_gate", "ffn2_w_up")
    params = {"ffn1_w_gate": (ffn1_w_gate, m_ffn1_w_gate, v_ffn1_w_gate), "ffn1_w_up": (ffn1_w_up, m_ffn1_w_up, v_ffn1_w_up),
              "ffn1_w_down": (ffn1_w_down, m_ffn1_w_down, v_ffn1_w_down), "w_in": (w_in, m_w_in, v_w_in),
              "w_out": (w_out, m_w_out, v_w_out), "ffn2_w_gate": (ffn2_w_gate, m_ffn2_w_gate, v_ffn2_w_gate),
              "ffn2_w_up": (ffn2_w_up, m_ffn2_w_up, v_ffn2_w_up), "ffn2_w_down": (ffn2_w_down, m_ffn2_w_down, v_ffn2_w_down)}
    for name_, (mine, theirs) in zip(names, totals):
        w, m, v = params[name_]
        if name_ in transposed:
            w, m, v = w.T, m.T, v.T
        if name_ == "w_in":
            both = jnp.where(ci == 0, jnp.concatenate([mine, theirs]), jnp.concatenate([theirs, mine]))[:in_shard]
            res = (both,) + tuple(_adamw(w, both, m, v, "adamw_" + name_))
        else:
            res = _adamw_halves(w, mine, theirs, m, v, core, "adamw_" + name_)
        out[name_] = tuple(r.T for r in res) if name_ in transposed else tuple(res)

    def small_pack(ada_b_, gains_, forget_, conv_):
        p = _rows_at(ada_b_.reshape(N_MOD, d), PROW_ADA_B, SMALL_ROWS, d)
        p += sum(_rows_at(g[None, :], PROW_GAINS + i, SMALL_ROWS, d) for i, g in enumerate(gains_))
        p += _rows_at(forget_[None, :], PROW_FORGET, SMALL_ROWS, d) + _rows_at(conv_, PROW_CONV, SMALL_ROWS, d)
        return p

    g_gains = [tot[ROW_GAINS + i] for i in range(5)]
    g_forget = tot[ROW_FORGET, :n_heads]
    sw = small_pack(ada_b, (norm1_g, norm2_g, norm3_g, final_g, group_norm_g), forget_bias, conv_w)
    sm = small_pack(m_ada_b, (m_norm1_g, m_norm2_g, m_norm3_g, m_final_g, m_group_norm_g), m_forget_bias, m_conv_w)
    sv = small_pack(v_ada_b, (v_norm1_g, v_norm2_g, v_norm3_g, v_final_g, v_group_norm_g), v_forget_bias, v_conv_w)
    sg = small_pack(grad_ada_b, g_gains, g_forget, grad_conv)
    small = (sg,) + tuple(_adamw(sw, sg, sm, sv, "adamw_small"))

    def unpack(p):
        r = {"ada_b": p[PROW_ADA_B:PROW_ADA_B + N_MOD].reshape(N_MOD * d), "forget_bias": p[PROW_FORGET, :n_heads],
             "conv_w": p[PROW_CONV:PROW_CONV + CONV_K, :cs]}
        for i, nm in enumerate(("norm1_g", "norm2_g", "norm3_g", "final_g", "group_norm_g")):
            r[nm] = p[PROW_GAINS + i]
        return r

    small = [unpack(p) for p in small]
    order = ("ada_w", "ada_b", "norm1_g", "ffn1_w_gate", "ffn1_w_up", "ffn1_w_down", "norm2_g", "w_in", "forget_bias",
             "conv_w", "group_norm_g", "w_out", "norm3_g", "ffn2_w_gate", "ffn2_w_up", "ffn2_w_down", "final_g")
    result = [loss, dx[None]]
    for k in range(4):
        result += [out[nm][k] if nm in out else small[k][nm] for nm in order]
    return tuple(result)
```

```python
import functools
import math

import jax
import jax.numpy as jnp
from jax import lax
from jax.experimental import pallas as pl
from jax.experimental.pallas import tpu as pltpu

F32 = jnp.float32
BF16 = jnp.bfloat16

HEAD_DIM = 64
CONV_K = 3
N_MOD = 9
EPS = 1e-6
ADAM_LR = 0.001
ADAM_B1 = 0.9
ADAM_B2 = 0.999
ADAM_EPS = 1e-08
ADAM_WD = 0.01
ADAM_STEP = 10

LANES = 128
N_CHIPS = 4
N_DEV = 8
VMEM_LIMIT_BYTES = 56 * 1024 * 1024
MAX_CONTRACTION = 4096
NEG_BIG = -1e30
MESH = pl.DeviceIdType.MESH

_NT = (((1,), (1,)), ((), ()))
_NN = (((1,), (0,)), ((), ()))
_TN = (((0,), (0,)), ((), ()))


def _params(*sem):
    return pltpu.CompilerParams(dimension_semantics=sem, vmem_limit_bytes=VMEM_LIMIT_BYTES)


class _Exchange:
    def __init__(self, inputs, out_shapes, n_sems, start, finish, aliases=None):
        self.inputs, self.out_shapes, self.n_sems = list(inputs), list(out_shapes), n_sems
        self.start, self.finish, self.aliases = start, finish, dict(aliases or {})
        self.results = None

    def set_results(self, results):
        self.results = list(results)


class _SemaphoreWindow:
    def __init__(self, sems, base):
        self.sems, self.base = sems, base
        self.at = self

    def __getitem__(self, k):
        return self.sems.at[self.base + k]


class _JoinedExchange(_Exchange):
    def __init__(self, parts):
        self.parts = parts
        aliases, i0, o0 = {}, 0, 0
        for p in parts:
            aliases.update({i0 + a: o0 + b for a, b in p.aliases.items()})
            i0, o0 = i0 + len(p.inputs), o0 + len(p.out_shapes)

        def each(method, src, dst, send_sems, recv_sems):
            i0 = o0 = s0 = 0
            for p in parts:
                i1, o1 = i0 + len(p.inputs), o0 + len(p.out_shapes)
                getattr(p, method)(src[i0:i1], dst[o0:o1], _SemaphoreWindow(send_sems, s0), _SemaphoreWindow(recv_sems, s0))
                i0, o0, s0 = i1, o1, s0 + p.n_sems

        super().__init__([a for p in parts for a in p.inputs], [o for p in parts for o in p.out_shapes],
                         sum(p.n_sems for p in parts), functools.partial(each, "start"), functools.partial(each, "finish"),
                         aliases)

    def set_results(self, results):
        o0 = 0
        for p in self.parts:
            p.set_results(results[o0:o0 + len(p.out_shapes)])
            o0 += len(p.out_shapes)


def _join(*parts):
    return parts[0] if len(parts) == 1 else _JoinedExchange(list(parts))


def _pc(body, exchange=None, **kw):
    if exchange is None:
        return pl.pallas_call(body, **kw)
    grid = kw["grid"]
    single = not isinstance(kw["out_shape"], (tuple, list))
    out_shape = [kw["out_shape"]] if single else list(kw["out_shape"])
    out_specs = [kw["out_specs"]] if single else list(kw["out_specs"])
    in_specs = list(kw["in_specs"])
    scratch = list(kw.get("scratch_shapes", ()))
    n_in, n_out, n_scr = len(in_specs), len(out_shape), len(scratch)
    n_xi, n_xo = len(exchange.inputs), len(exchange.out_shapes)

    def wrapped(*refs):
        pos = [n_in, n_in + n_xi, n_in + n_xi + n_out, n_in + n_xi + n_out + n_xo]
        ins, x_in, outs, x_out = refs[:pos[0]], refs[pos[0]:pos[1]], refs[pos[1]:pos[2]], refs[pos[2]:pos[3]]
        scr = refs[pos[3]:pos[3] + n_scr]
        send_sems, recv_sems = refs[pos[3] + n_scr:]
        ids = [pl.program_id(a) for a in range(len(grid))]
        first = functools.reduce(jnp.logical_and, [i == 0 for i in ids])
        last = functools.reduce(jnp.logical_and, [i == g - 1 for i, g in zip(ids, grid)])

        @pl.when(first)
        def _():
            exchange.start(x_in, x_out, send_sems, recv_sems)

        body(*ins, *outs, *scr)

        @pl.when(last)
        def _():
            exchange.finish(x_in, x_out, send_sems, recv_sems)

    call = pl.pallas_call(
        wrapped, out_shape=tuple(out_shape) + tuple(exchange.out_shapes), grid=grid,
        in_specs=in_specs + [_ANY] * n_xi, out_specs=tuple(out_specs) + (_ANY,) * n_xo,
        scratch_shapes=scratch + [pltpu.SemaphoreType.DMA((exchange.n_sems,)), pltpu.SemaphoreType.DMA((exchange.n_sems,))],
        input_output_aliases={n_in + a: n_out + b for a, b in exchange.aliases.items()},
        compiler_params=_params(*(["arbitrary"] * len(grid))), name=kw["name"])

    def run(*args):
        res = call(*args, *exchange.inputs)
        exchange.set_results(res[n_out:])
        return res[0] if single else tuple(res[:n_out])

    return run


_ANY = pl.BlockSpec(memory_space=pl.ANY)


def _tile(n, pref, mult):
    best = None
    t = mult
    while t <= min(n, pref):
        if n % t == 0:
            best = t
        t += mult
    return n if best is None else best


def _sds(shape, dtype):
    return jax.ShapeDtypeStruct(shape, dtype)


def _vec_spec(d):
    return pl.BlockSpec((1, d), lambda *_: (0, 0))


def _norm_mod_fwd(x, g, shift, scale, name):
    s, d = x.shape
    tr = _tile(s, 512, 16)

    def body(x_ref, g_ref, sh_ref, sc_ref, h_ref):
        xv = x_ref[...]
        rstd = lax.rsqrt(jnp.mean(xv * xv, axis=-1, keepdims=True) + EPS)
        n = xv * rstd * g_ref[...]
        h_ref[...] = (n * (1.0 + sc_ref[...]) + sh_ref[...]).astype(BF16)

    row = pl.BlockSpec((tr, d), lambda i: (i, 0))
    return _pc(body, out_shape=_sds((s, d), BF16), grid=(s // tr,),
               in_specs=[row, _vec_spec(d), _vec_spec(d), _vec_spec(d)], out_specs=row,
               compiler_params=_params("parallel"), name=name)(x, g, shift, scale)


def _through_gate(dx, f_ref, gate_ref, df_ref, dgate_ref):
    df_ref[...] = (dx * gate_ref[...]).astype(BF16)
    dgate_ref[...] += jnp.sum(dx * f_ref[...].astype(F32), axis=0, keepdims=True)


def _norm_mod_bwd(dh, x, g, scale, dres, name, f=None, gate=None, rows=None, exchange=None):
    d = x.shape[1]
    first_row, s = rows if rows else (0, x.shape[0])
    gated = f is not None
    terms = list(zip(*dh)) if isinstance(dh, tuple) else None
    tr = _tile(s, 256, 16)
    b0 = first_row // tr
    assert first_row % tr == 0
    n_lead = 2 * len(terms) if terms else 1

    def body(*refs):
        lead, (x_ref, g_ref, sc_ref, dres_ref), rest = refs[:n_lead], refs[n_lead:n_lead + 4], refs[n_lead + 4:]
        f_ref, gate_ref = rest[:2] if gated else (None, None)
        dx_ref, dsh_ref, dsc_ref, dg_ref = rest[2:6] if gated else rest[:4]
        df_ref, dgate_ref = rest[6:8] if gated else (None, None)

        @pl.when(pl.program_id(0) == 0)
        def _():
            for ref in (dsh_ref, dsc_ref, dg_ref) + ((dgate_ref,) if gated else ()):
                ref[...] = jnp.zeros_like(ref)

        if terms:
            dhv = lax.dot_general(lead[0][...], lead[1][...], _NN, preferred_element_type=F32)
            for p in range(1, len(terms)):
                dhv += lax.dot_general(lead[2 * p][...], lead[2 * p + 1][...], _NN, preferred_element_type=F32)
        else:
            dhv = lead[0][...]
        xv = x_ref[...]
        gv = g_ref[...]
        rstd = lax.rsqrt(jnp.mean(xv * xv, axis=-1, keepdims=True) + EPS)
        xhat = xv * rstd
        dn = dhv * (1.0 + sc_ref[...])
        dsh_ref[...] += jnp.sum(dhv, axis=0, keepdims=True)
        dsc_ref[...] += jnp.sum(dhv * (xhat * gv), axis=0, keepdims=True)
        dg_ref[...] += jnp.sum(dn * xhat, axis=0, keepdims=True)
        dxh = dn * gv
        proj = jnp.mean(dxh * xhat, axis=-1, keepdims=True)
        dx = dres_ref[...] + rstd * (dxh - xhat * proj)
        dx_ref[...] = dx
        if gated:
            _through_gate(dx, f_ref, gate_ref, df_ref, dgate_ref)

    row = pl.BlockSpec((tr, d), lambda i: (b0 + i, 0))
    out_row = pl.BlockSpec((tr, d), lambda i: (i, 0))
    vec = _vec_spec(d)
    if terms:
        in_specs, args = [], []
        for l, r in terms:
            assert l.shape[1] == r.shape[0] <= MAX_CONTRACTION and r.shape[1] == d
            in_specs += [pl.BlockSpec((tr, l.shape[1]), lambda i: (b0 + i, 0)), pl.BlockSpec(r.shape, lambda i: (0, 0))]
            args += [l, r]
    else:
        in_specs, args = [row], [dh]
    in_specs += [row, vec, vec, row]
    args += [x, g, scale, dres]
    out_shape = [_sds((s, d), F32), _sds((1, d), F32), _sds((1, d), F32), _sds((1, d), F32)]
    out_specs = [out_row, vec, vec, vec]
    if gated:
        out_shape += [_sds((s, d), BF16), _sds((1, d), F32)]
        out_specs += [out_row, vec]
        in_specs += [row, vec]
        args += [f, gate]
    return _pc(body, exchange, out_shape=tuple(out_shape), grid=(s // tr,), in_specs=in_specs,
               out_specs=tuple(out_specs), compiler_params=_params("arbitrary"), name=name)(*args)


def _final_loss(x, g, target, f, gate, name):
    s, d = x.shape
    tr = _tile(s, 256, 16)
    nsteps = s // tr

    def body(x_ref, g_ref, t_ref, f_ref, gate_ref, dx_ref, loss_ref, dg_ref, df_ref, dgate_ref):
        i = pl.program_id(0)

        @pl.when(i == 0)
        def _():
            loss_ref[...] = jnp.zeros_like(loss_ref)
            dg_ref[...] = jnp.zeros_like(dg_ref)
            dgate_ref[...] = jnp.zeros_like(dgate_ref)

        xv = x_ref[...]
        gv = g_ref[...]
        rstd = lax.rsqrt(jnp.mean(xv * xv, axis=-1, keepdims=True) + EPS)
        xhat = xv * rstd
        err = xhat * gv - t_ref[...]
        dy = err * (1.0 / d)
        loss_ref[...] += jnp.sum(0.5 * err * dy, axis=0, keepdims=True)
        dg_ref[...] += jnp.sum(dy * xhat, axis=0, keepdims=True)
        dxh = dy * gv
        proj = jnp.mean(dxh * xhat, axis=-1, keepdims=True)
        dx = rstd * (dxh - xhat * proj)
        dx_ref[...] = dx
        _through_gate(dx, f_ref, gate_ref, df_ref, dgate_ref)

        @pl.when(i == nsteps - 1)
        def _():
            loss_ref[...] = jnp.broadcast_to(jnp.sum(loss_ref[...], axis=-1, keepdims=True), loss_ref.shape)

    row = pl.BlockSpec((tr, d), lambda i: (i, 0))
    vec = _vec_spec(d)
    return _pc(body, out_shape=(_sds((s, d), F32), _sds((1, d), F32), _sds((1, d), F32), _sds((s, d), BF16), _sds((1, d), F32)),
               grid=(nsteps,), in_specs=[row, vec, row, row, vec], out_specs=(row, vec, vec, row, vec),
               compiler_params=_params("arbitrary"), name=name)(x, g, target, f, gate)


def _mm(lhs, rhs, dims, out_dtype, name, res=None, gate=None, aux_dtype=None, exchange=None):
    lhs_list = list(lhs) if isinstance(lhs, (list, tuple)) else [lhs]
    rhs_list = list(rhs) if isinstance(rhs, (list, tuple)) else [rhs]
    n_terms = len(lhs_list)
    assert n_terms == len(rhs_list)
    m = lhs_list[0].shape[1 if dims == "tn" else 0]
    n = rhs_list[0].shape[0 if dims == "nt" else 1]
    tn = _tile(n, 1024, LANES)
    tm = _tile(m, 512, LANES if dims == "tn" else 16)
    dn = {"nn": _NN, "nt": _NT, "tn": _TN}[dims]
    in_specs, args = [], []
    for l, r in zip(lhs_list, rhs_list):
        k = l.shape[0 if dims == "tn" else 1]
        assert k == r.shape[1 if dims == "nt" else 0] and k <= MAX_CONTRACTION, (l.shape, r.shape, dims)
        in_specs.append(pl.BlockSpec((k, tm), lambda i, j: (0, i)) if dims == "tn" else pl.BlockSpec((tm, k), lambda i, j: (i, 0)))
        in_specs.append(pl.BlockSpec((tn, k), lambda i, j: (j, 0)) if dims == "nt" else pl.BlockSpec((k, tn), lambda i, j: (0, j)))
        args += [l, r]
    out_spec = pl.BlockSpec((tm, tn), lambda i, j: (i, j))
    has_res, has_gate, has_aux = res is not None, gate is not None, aux_dtype is not None

    def body(*refs):
        refs = list(refs)
        pos = 2 * n_terms
        res_ref = gate_ref = aux_ref = None
        if has_res:
            res_ref = refs[pos]; pos += 1
        if has_gate:
            gate_ref = refs[pos]; pos += 1
        out_ref = refs[pos]; pos += 1
        if has_aux:
            aux_ref = refs[pos]
        acc = lax.dot_general(refs[0][...], refs[1][...], dn, preferred_element_type=F32)
        for p in range(1, n_terms):
            acc += lax.dot_general(refs[2 * p][...], refs[2 * p + 1][...], dn, preferred_element_type=F32)
        if has_aux:
            aux_ref[...] = acc.astype(aux_dtype)
        if has_gate:
            acc = acc * gate_ref[...]
        if has_res:
            acc = res_ref[...] + acc
        out_ref[...] = acc.astype(out_dtype)

    if has_res:
        in_specs.append(out_spec); args.append(res)
    if has_gate:
        in_specs.append(pl.BlockSpec((1, tn), lambda i, j: (0, j))); args.append(gate)
    out_shape = [_sds((m, n), out_dtype)]
    out_specs = [out_spec]
    if has_aux:
        out_shape.append(_sds((m, n), aux_dtype)); out_specs.append(out_spec)
    outs = _pc(body, exchange, out_shape=tuple(out_shape), grid=(m // tm, n // tn), in_specs=in_specs,
               out_specs=tuple(out_specs), compiler_params=_params("parallel", "parallel"), name=name)(*args)
    return outs if has_aux else outs[0]


def _ffn_up(h, wg_t, wu_t, name, exchange=None):
    s, d = h.shape
    f = wg_t.shape[0]
    tm = _tile(s, 1024, 16)
    tn = _tile(f, 256, LANES)

    def body(h_ref, wg_ref, wu_ref, a_ref, u_ref, hid_ref):
        hv = h_ref[...]
        a = lax.dot_general(hv, wg_ref[...], _NT, preferred_element_type=F32)
        u = lax.dot_general(hv, wu_ref[...], _NT, preferred_element_type=F32)
        a_ref[...] = a.astype(BF16)
        u_ref[...] = u.astype(BF16)
        hid_ref[...] = (a * jax.nn.sigmoid(a) * u).astype(BF16)

    hs = pl.BlockSpec((tm, d), lambda i, j: (i, 0))
    ws = pl.BlockSpec((tn, d), lambda i, j: (j, 0))
    os_ = pl.BlockSpec((tm, tn), lambda i, j: (i, j))
    return _pc(body, exchange, out_shape=(_sds((s, f), BF16),) * 3, grid=(s // tm, f // tn),
               in_specs=[hs, ws, ws], out_specs=(os_, os_, os_),
               compiler_params=_params("parallel", "parallel"), name=name)(h, wg_t, wu_t)


def _ffn_dact(df, wd, a, u, name, exchange=None):
    s, d = df.shape
    f = wd.shape[0]
    tm = _tile(s, 1024, 16)
    tn = _tile(f, 256, LANES)

    def body(df_ref, wd_ref, a_ref, u_ref, da_ref, du_ref):
        dhid = lax.dot_general(df_ref[...], wd_ref[...], _NT, preferred_element_type=F32)
        av = a_ref[...].astype(F32)
        uv = u_ref[...].astype(F32)
        sig = jax.nn.sigmoid(av)
        da_ref[...] = (dhid * uv * (sig * (1.0 + av * (1.0 - sig)))).astype(BF16)
        du_ref[...] = (dhid * (av * sig)).astype(BF16)

    ds_ = pl.BlockSpec((tm, d), lambda i, j: (i, 0))
    ws = pl.BlockSpec((tn, d), lambda i, j: (j, 0))
    os_ = pl.BlockSpec((tm, tn), lambda i, j: (i, j))
    return _pc(body, exchange, out_shape=(_sds((s, f), BF16),) * 2, grid=(s // tm, f // tn),
               in_specs=[ds_, ws, os_, os_], out_specs=(os_, os_),
               compiler_params=_params("parallel", "parallel"), name=name)(df, wd, a, u)


def _split3(v):
    hi = v.astype(BF16)
    r1 = v - hi.astype(F32)
    mid = r1.astype(BF16)
    lo = (r1 - mid.astype(F32)).astype(BF16)
    return hi, mid, lo


def _dot3(v, mat):
    hi, mid, lo = _split3(v)
    out = lax.dot_general(hi, mat, _NN, preferred_element_type=F32)
    out += lax.dot_general(mid, mat, _NN, preferred_element_type=F32)
    out += lax.dot_general(lo, mat, _NN, preferred_element_type=F32)
    return out


def _forget_fwd(flog_t, bias, name):
    h, s = flog_t.shape
    blk = _tile(s, 512, LANES)
    tri = (jnp.arange(blk)[:, None] <= jnp.arange(blk)[None, :]).astype(BF16)

    def body(z_ref, b_ref, tri_ref, f_ref, carry):
        @pl.when(pl.program_id(0) == 0)
        def _():
            carry[...] = jnp.zeros_like(carry)

        z = z_ref[...] + b_ref[...]
        e = jnp.exp(-jnp.abs(z))
        w = 1.0 + e
        log1p_e = jnp.where(w == 1.0, e, jnp.log(w) * (e / (w - 1.0)))
        lf = jnp.minimum(z, 0.0) - log1p_e
        out = carry[...] + _dot3(lf, tri_ref[...])
        for j, piece in enumerate(_split3(out)):
            f_ref[j] = piece
        carry[...] = out[:, blk - 1:blk]

    zs = pl.BlockSpec((h, blk), lambda i: (0, i))
    return _pc(body, out_shape=_sds((3, h, s), BF16), grid=(s // blk,),
               in_specs=[zs, pl.BlockSpec((h, 1), lambda i: (0, 0)), pl.BlockSpec((blk, blk), lambda i: (0, 0))],
               out_specs=pl.BlockSpec((3, h, blk), lambda i: (0, 0, i)), scratch_shapes=[pltpu.VMEM((h, 1), F32)],
               compiler_params=_params("arbitrary"), name=name)(flog_t, bias, tri)


def _forget_bwd(df_t, flog_t, bias, name):
    h, s = flog_t.shape
    blk = _tile(s, 512, LANES)
    nb = s // blk
    tri = (jnp.arange(blk)[:, None] >= jnp.arange(blk)[None, :]).astype(BF16)

    def body(df_ref, z_ref, b_ref, tri_ref, dz_ref, db_ref, carry):
        @pl.when(pl.program_id(0) == 0)
        def _():
            carry[...] = jnp.zeros_like(carry)
            db_ref[...] = jnp.zeros_like(db_ref)

        rc = carry[...] + _dot3(df_ref[...], tri_ref[...])
        carry[...] = rc[:, 0:1]
        dz = rc * jax.nn.sigmoid(-(z_ref[...] + b_ref[...]))
        dz_ref[...] = dz
        db_ref[...] += jnp.sum(dz, axis=-1, keepdims=True)

    rev = pl.BlockSpec((h, blk), lambda i: (0, nb - 1 - i))
    col = pl.BlockSpec((h, 1), lambda i: (0, 0))
    return _pc(body, out_shape=(_sds((h, s), F32), _sds((h, 1), F32)), grid=(nb,),
               in_specs=[rev, rev, col, pl.BlockSpec((blk, blk), lambda i: (0, 0))],
               out_specs=(rev, col), scratch_shapes=[pltpu.VMEM((h, 1), F32)],
               compiler_params=_params("arbitrary"), name=name)(df_t, flog_t, bias, tri)


def _attn_tiles(s):
    return _tile(s, 1024, LANES)


def _attn_half(t):
    return t // 2 if t >= 4 * LANES else t


BIAS_ROWS = 16


def _attn_prep(qkv, f_pieces, name):
    s = qkv.shape[0]
    a_w = qkv.shape[1] // 3
    npair = a_w // LANES
    t = _attn_tiles(s)
    scale = 1.0 / math.sqrt(HEAD_DIM)

    six = f_pieces[:, :2 * npair].reshape(3, npair, 2, s).transpose(1, 3, 2, 0).reshape(npair, s, 6)
    feat = jnp.concatenate([six, jnp.ones((npair, s, 1), BF16), jnp.zeros((npair, s, BIAS_ROWS - 7), BF16)], axis=-1)
    place_q = [[0.0] * (2 * LANES) for _ in range(BIAS_ROWS)]
    place_k = [[0.0] * (2 * LANES) for _ in range(BIAS_ROWS)]
    for hh in range(2):
        b0 = hh * LANES + (HEAD_DIM if hh == 0 else 0)
        for j in range(3):
            place_q[3 * hh + j][b0 + j] = 1.0
            place_q[6][b0 + 3 + j] = 1.0
            place_k[6][b0 + j] = 1.0
            place_k[3 * hh + j][b0 + 3 + j] = -1.0
    place_q = jnp.array(place_q, BF16)
    place_k = jnp.array(place_k, BF16)

    def body(q_ref, k_ref, v_ref, f_ref, pq_ref, pk_ref, qa_ref, ka_ref, va_ref):
        lane = lax.broadcasted_iota(jnp.int32, (1, LANES), 1)
        q2 = (q_ref[...].astype(F32) * scale).astype(BF16)
        k2, v2 = k_ref[...], v_ref[...]
        qx = lax.dot_general(f_ref[0], pq_ref[...], _NN, preferred_element_type=F32).astype(BF16)
        kx = lax.dot_general(f_ref[0], pk_ref[...], _NN, preferred_element_type=F32).astype(BF16)
        for hh in range(2):
            real = (lane < HEAD_DIM) if hh == 0 else (lane >= HEAD_DIM)
            cols = slice(hh * LANES, (hh + 1) * LANES)
            qa_ref[:, cols] = jnp.where(real, q2, qx[:, cols])
            ka_ref[:, cols] = jnp.where(real, k2, kx[:, cols])
            va_ref[:, cols] = jnp.where(real, v2, jnp.zeros_like(v2))

    def col(off):
        return pl.BlockSpec((t, LANES), lambda p, i: (i, off + p))

    out = pl.BlockSpec((t, 2 * LANES), lambda p, i: (i, p))
    place = pl.BlockSpec((BIAS_ROWS, 2 * LANES), lambda p, i: (0, 0))
    return _pc(body, out_shape=(_sds((s, 2 * a_w), BF16),) * 3, grid=(npair, s // t),
               in_specs=[col(0), col(npair), col(2 * npair), pl.BlockSpec((1, t, BIAS_ROWS), lambda p, i: (p, i, 0)),
                         place, place],
               out_specs=(out, out, out), compiler_params=_params("parallel", "parallel"), name=name)(
                   qkv, qkv, qkv, feat, place_q, place_k)


def _attn_fwd(qa, ka, va, name, exchange=None):
    s = qa.shape[0]
    a_w = qa.shape[1] // 2
    npair = a_w // LANES
    t = _attn_tiles(s)
    nq = s // t
    half = _attn_half(t)

    def body(q_ref, k_ref, v_ref, o_ref, lse_ref, m_sc, l_sc, acc_sc):
        qi = pl.program_id(1)
        first = lax.broadcasted_iota(jnp.int32, (1, LANES), 1) < HEAD_DIM
        m_sc[...] = jnp.full_like(m_sc, NEG_BIG)
        l_sc[...] = jnp.zeros_like(l_sc)
        acc_sc[...] = jnp.zeros_like(acc_sc)

        def step(q0, k_start, size, diag):
            q_sl = slice(q0, q0 + size)
            k_rows = pl.ds(pl.multiple_of(k_start, size), size)
            m_old = m_sc[q_sl, :]
            keep = None
            if diag:
                keep = (lax.broadcasted_iota(jnp.int32, (size, size), 0) >= lax.broadcasted_iota(jnp.int32, (size, size), 1))
            m_new, rs, pv = [], [], []
            for hh in range(2):
                cols = slice(hh * LANES, (hh + 1) * LANES)
                sc = lax.dot_general(q_ref[q_sl, cols], k_ref[k_rows, cols], _NT, preferred_element_type=F32)
                if diag:
                    sc = jnp.where(keep, sc, NEG_BIG)
                mo = m_old[:, hh * HEAD_DIM:hh * HEAD_DIM + 1]
                mn = jnp.maximum(mo, jnp.max(sc, axis=1, keepdims=True))
                p = jnp.exp(sc - mn)
                m_new.append(mn)
                rs.append(jnp.sum(p, axis=1, keepdims=True))
                pv.append(lax.dot_general(p.astype(BF16), v_ref[k_rows, cols], _NN, preferred_element_type=F32))
            m2 = jnp.where(first, m_new[0], m_new[1])
            alpha = jnp.exp(m_old - m2)
            m_sc[q_sl, :] = m2
            l_sc[q_sl, :] = alpha * l_sc[q_sl, :] + jnp.where(first, rs[0], rs[1])
            acc_sc[q_sl, :] = alpha * acc_sc[q_sl, :] + pv[0] + pv[1]

        def below_diagonal(ki, carry):
            step(0, ki * t, t, False)
            return carry

        lax.fori_loop(0, qi, below_diagonal, 0)
        step(0, qi * t, half, True)
        if half < t:
            step(half, qi * t, half, False)
            step(half, qi * t + half, half, True)
        l2 = l_sc[...]
        o_ref[...] = acc_sc[...] / l2
        lse_ref[...] = m_sc[...] + jnp.log(l2)

    qs = pl.BlockSpec((t, 2 * LANES), lambda p, qi: (qi, p))
    ks = pl.BlockSpec((s, 2 * LANES), lambda p, qi: (0, p))
    os_ = pl.BlockSpec((t, LANES), lambda p, qi: (qi, p))
    return _pc(body, exchange, out_shape=(_sds((s, a_w), F32), _sds((s, a_w), F32)), grid=(npair, nq),
               in_specs=[qs, ks, ks], out_specs=(os_, os_),
               scratch_shapes=[pltpu.VMEM((t, LANES), F32)] * 3,
               compiler_params=_params("parallel", "arbitrary"), name=name)(qa, ka, va)


def _attn_bwd(qa, ka, va, do, o, lse, name, exchange=None):
    s = qa.shape[0]
    a_w = qa.shape[1] // 2
    npair = a_w // LANES
    t = _attn_tiles(s)
    nq = s // t
    half = _attn_half(t)
    scale = 1.0 / math.sqrt(HEAD_DIM)

    def body(q_ref, k_ref, v_ref, do_ref, o_ref, lse_ref, dq_ref, dk_ref, dv_ref, qx_ref, kx_ref, dk_sc, dv_sc, kx_sc):
        ki = pl.program_id(1)
        first = lax.broadcasted_iota(jnp.int32, (1, LANES), 1) < HEAD_DIM

        @pl.when(ki == 0)
        def _():
            dq_ref[...] = jnp.zeros_like(dq_ref)
            qx_ref[...] = jnp.zeros_like(qx_ref)

        def step(q_start, k0, size, diag, assign):
            rows = pl.ds(pl.multiple_of(q_start, size), size)
            k_sl = slice(k0, k0 + size)
            do2 = do_ref[rows, :]
            lse2 = lse_ref[rows, :]
            dd = do2.astype(F32) * o_ref[rows, :]
            keep = None
            if diag:
                keep = (lax.broadcasted_iota(jnp.int32, (size, size), 0) >= lax.broadcasted_iota(jnp.int32, (size, size), 1))
            dq_h, dk_h, dv_h = [], [], []
            for hh in range(2):
                sel = first if hh == 0 else jnp.logical_not(first)
                cols = slice(hh * LANES, (hh + 1) * LANES)
                qh, kh, vh = q_ref[rows, cols], k_ref[k_sl, cols], v_ref[k_sl, cols]
                delta = jnp.sum(jnp.where(sel, dd, 0.0), axis=1, keepdims=True)
                sc = lax.dot_general(qh, kh, _NT, preferred_element_type=F32)
                if diag:
                    sc = jnp.where(keep, sc, NEG_BIG)
                p = jnp.exp(sc - lse2[:, hh * HEAD_DIM:hh * HEAD_DIM + 1])
                dp = lax.dot_general(do2, vh, _NT, preferred_element_type=F32)
                ds_b = (p * (dp - delta)).astype(BF16)
                dv_h.append(lax.dot_general(p.astype(BF16), do2, _TN, preferred_element_type=F32))
                dk_h.append(lax.dot_general(ds_b, qh, _TN, preferred_element_type=F32))
                dq_h.append(lax.dot_general(ds_b, kh, _NN, preferred_element_type=F32))
            dq_ref[rows, :] += jnp.where(first, dq_h[0], dq_h[1]) * scale
            qx_ref[rows, :] += jnp.where(first, dq_h[1], dq_h[0])
            dk_new = jnp.where(first, dk_h[0], dk_h[1])
            kx_new = jnp.where(first, dk_h[1], dk_h[0])
            dv_new = jnp.where(first, dv_h[0], dv_h[1])
            if assign:
                dk_sc[k_sl, :] = dk_new
                kx_sc[k_sl, :] = kx_new
                dv_sc[k_sl, :] = dv_new
            else:
                dk_sc[k_sl, :] += dk_new
                kx_sc[k_sl, :] += kx_new
                dv_sc[k_sl, :] += dv_new

        def below_diagonal(qi, carry):
            step(qi * t, 0, t, False, False)
            return carry

        step(ki * t, 0, half, True, True)
        if half < t:
            step(ki * t + half, 0, half, False, False)
            step(ki * t + half, half, half, True, True)
        lax.fori_loop(ki + 1, nq, below_diagonal, 0)
        dk_ref[...] = dk_sc[...].astype(BF16)
        dv_ref[...] = dv_sc[...].astype(BF16)
        kx_ref[...] = kx_sc[...]

    ks2 = pl.BlockSpec((t, 2 * LANES), lambda p, ki: (ki, p))
    qs2 = pl.BlockSpec((s, 2 * LANES), lambda p, ki: (0, p))
    whole = pl.BlockSpec((s, LANES), lambda p, ki: (0, p))
    kout = pl.BlockSpec((t, LANES), lambda p, ki: (ki, p))
    return _pc(body, exchange,
               out_shape=(_sds((s, a_w), F32), _sds((s, a_w), BF16), _sds((s, a_w), BF16), _sds((s, a_w), F32),
                          _sds((s, a_w), F32)),
               grid=(npair, nq), in_specs=[qs2, ks2, ks2, whole, whole, whole],
               out_specs=(whole, kout, kout, whole, kout),
               scratch_shapes=[pltpu.VMEM((t, LANES), F32)] * 3,
               compiler_params=_params("parallel", "arbitrary"), name=name)(qa, ka, va, do, o, lse)

def _decay_grads(qx, kx, name):
    s, a_w = qx.shape
    n_heads = a_w // HEAD_DIM
    tr = _tile(s, 512, 8)
    pick_q = [[0.0] * LANES for _ in range(a_w)]
    pick_k = [[0.0] * LANES for _ in range(a_w)]
    for h in range(n_heads):
        b0 = (h // 2) * LANES + (HEAD_DIM if h % 2 == 0 else 0)
        pick_q[b0][h] = 1.0
        pick_k[b0 + 3][h] = 1.0
    pick_q = jnp.array(pick_q, BF16)
    pick_k = jnp.array(pick_k, BF16)

    def body(qx_ref, kx_ref, pq_ref, pk_ref, o_ref):
        o_ref[...] = _dot3(qx_ref[...], pq_ref[...]) - _dot3(kx_ref[...], pk_ref[...])

    row = pl.BlockSpec((tr, a_w), lambda i: (i, 0))
    pick = pl.BlockSpec((a_w, LANES), lambda i: (0, 0))
    return _pc(body, out_shape=_sds((s, LANES), F32), grid=(s // tr,), in_specs=[row, row, pick, pick],
               out_specs=pl.BlockSpec((tr, LANES), lambda i: (i, 0)),
               compiler_params=_params("parallel"), name=name)(qx, kx, pick_q, pick_k)


def _shift_down(z, k, rows):
    return jnp.where(rows >= k, pltpu.roll(z, k, 0), 0.0)


def _shift_up(z, k, rows, n):
    return jnp.where(rows < n - k, pltpu.roll(z, n - k, 0), 0.0)


def _conv_fwd(bcx, conv_w, name):
    s = bcx.shape[0]
    cw = bcx.shape[1] // 3
    nb = cw // LANES

    def body(b_ref, c_ref, x_ref, w_ref, cv_ref):
        rows = lax.broadcasted_iota(jnp.int32, (s, LANES), 0)
        z = c_ref[...] * x_ref[...]
        w = w_ref[...]
        y = w[2:3, :] * z + w[1:2, :] * _shift_down(z, 1, rows) + w[0:1, :] * _shift_down(z, 2, rows)
        cv_ref[...] = b_ref[...] * y

    def col(off):
        return pl.BlockSpec((s, LANES), lambda j: (0, j + off))

    return _pc(body, out_shape=_sds((s, cw), F32), grid=(nb,),
               in_specs=[col(0), col(nb), col(2 * nb), pl.BlockSpec((CONV_K, LANES), lambda j: (0, j))],
               out_specs=col(0), compiler_params=_params("parallel"), name=name)(bcx, bcx, bcx, conv_w)


def _conv_bwd(dcv, bcx, conv_w, name):
    s = bcx.shape[0]
    cw = bcx.shape[1] // 3
    nb = cw // LANES

    def body(dcv_ref, b_ref, c_ref, x_ref, w_ref, db_ref, dc_ref, dxc_ref, dw_ref):
        rows = lax.broadcasted_iota(jnp.int32, (s, LANES), 0)
        cv_, xv = c_ref[...], x_ref[...]
        z = cv_ * xv
        w = w_ref[...]
        z1 = _shift_down(z, 1, rows)
        z2 = _shift_down(z, 2, rows)
        y = w[2:3, :] * z + w[1:2, :] * z1 + w[0:1, :] * z2
        dcvv = dcv_ref[...]
        db_ref[...] = (dcvv * y).astype(BF16)
        dy = dcvv * b_ref[...]
        dw_ref[0:1, :] = jnp.sum(dy * z2, axis=0, keepdims=True)
        dw_ref[1:2, :] = jnp.sum(dy * z1, axis=0, keepdims=True)
        dw_ref[2:3, :] = jnp.sum(dy * z, axis=0, keepdims=True)
        dz = w[2:3, :] * dy + w[1:2, :] * _shift_up(dy, 1, rows, s) + w[0:1, :] * _shift_up(dy, 2, rows, s)
        dc_ref[...] = (dz * xv).astype(BF16)
        dxc_ref[...] = (dz * cv_).astype(BF16)

    def col(off):
        return pl.BlockSpec((s, LANES), lambda j: (0, j + off))

    wspec = pl.BlockSpec((CONV_K, LANES), lambda j: (0, j))
    db, dc, dxc, dw = _pc(body, out_shape=(_sds((s, cw), BF16),) * 3 + (_sds((CONV_K, cw), F32),), grid=(nb,),
                          in_specs=[col(0), col(0), col(nb), col(2 * nb), wspec],
                          out_specs=(col(0), col(0), col(0), wspec),
                          compiler_params=_params("parallel"), name=name)(dcv, bcx, bcx, bcx, conv_w)
    return db, dc, dxc, dw


def _group_matrix():
    idx = jnp.arange(LANES) // HEAD_DIM
    return (idx[:, None] == idx[None, :]).astype(BF16)


def _group_sum(v, gmat):
    return _dot3(v, gmat)


def _gnorm_fwd(att, cv, gg, name):
    s, a_w = att.shape
    cw = cv.shape[1]
    d = a_w + cw
    tr = _tile(s, 512, 16)
    gmat = _group_matrix()

    def body(att_ref, cv_ref, gg_ref, gm_ref, yn_ref):
        gm = gm_ref[...]
        for c0 in range(0, d, LANES):
            y = att_ref[:, c0:c0 + LANES] if c0 < a_w else cv_ref[:, c0 - a_w:c0 - a_w + LANES]
            ms = _group_sum(y * y, gm) * (1.0 / HEAD_DIM)
            yn_ref[:, c0:c0 + LANES] = (y * lax.rsqrt(ms + EPS) * gg_ref[:, c0:c0 + LANES]).astype(BF16)

    return _pc(body, out_shape=_sds((s, d), BF16), grid=(s // tr,),
               in_specs=[pl.BlockSpec((tr, a_w), lambda i: (i, 0)), pl.BlockSpec((tr, cw), lambda i: (i, 0)),
                         _vec_spec(d), pl.BlockSpec((LANES, LANES), lambda i: (0, 0))],
               out_specs=pl.BlockSpec((tr, d), lambda i: (i, 0)),
               compiler_params=_params("parallel"), name=name)(att, cv, gg, gmat)


def _gnorm_bwd(dyn, att, cv, gg, name):
    s, a_w = att.shape
    cw = cv.shape[1]
    d = a_w + cw
    tr = _tile(s, 256, 16)
    gmat = _group_matrix()

    def body(dyn_ref, att_ref, cv_ref, gg_ref, gm_ref, datt_ref, dcv_ref, dgg_ref):
        @pl.when(pl.program_id(0) == 0)
        def _():
            dgg_ref[...] = jnp.zeros_like(dgg_ref)

        gm = gm_ref[...]
        for c0 in range(0, d, LANES):
            y = att_ref[:, c0:c0 + LANES] if c0 < a_w else cv_ref[:, c0 - a_w:c0 - a_w + LANES]
            dv = dyn_ref[:, c0:c0 + LANES]
            r = lax.rsqrt(_group_sum(y * y, gm) * (1.0 / HEAD_DIM) + EPS)
            xhat = y * r
            dgg_ref[:, c0:c0 + LANES] += jnp.sum(dv * xhat, axis=0, keepdims=True)
            dxh = dv * gg_ref[:, c0:c0 + LANES]
            proj = _group_sum(dxh * xhat, gm) * (1.0 / HEAD_DIM)
            dy = r * (dxh - xhat * proj)
            if c0 < a_w:
                datt_ref[:, c0:c0 + LANES] = dy.astype(BF16)
            else:
                dcv_ref[:, c0 - a_w:c0 - a_w + LANES] = dy

    return _pc(body, out_shape=(_sds((s, a_w), BF16), _sds((s, cw), F32), _sds((1, d), F32)), grid=(s // tr,),
               in_specs=[pl.BlockSpec((tr, d), lambda i: (i, 0)), pl.BlockSpec((tr, a_w), lambda i: (i, 0)),
                         pl.BlockSpec((tr, cw), lambda i: (i, 0)), _vec_spec(d),
                         pl.BlockSpec((LANES, LANES), lambda i: (0, 0))],
               out_specs=(pl.BlockSpec((tr, a_w), lambda i: (i, 0)), pl.BlockSpec((tr, cw), lambda i: (i, 0)),
                          _vec_spec(d)),
               compiler_params=_params("arbitrary"), name=name)(dyn, att, cv, gg, gmat)


def _adamw_math(w, g, m, v):
    m_new = ADAM_B1 * m + (1.0 - ADAM_B1) * g
    v_new = ADAM_B2 * v + (1.0 - ADAM_B2) * (g * g)
    m_hat = m_new / (1.0 - ADAM_B1 ** ADAM_STEP)
    v_hat = v_new / (1.0 - ADAM_B2 ** ADAM_STEP)
    delta = -ADAM_LR * (m_hat / (jnp.sqrt(v_hat) + ADAM_EPS) + ADAM_WD * w)
    return delta, m_new, v_new


def _row_tile(r, c):
    return _tile(r, max(8, ((1 << 18) // c) // 8 * 8), 8)


def _adamw(w, g, m, v, name):
    r, c = w.shape
    tr = _row_tile(r, c)

    def body(w_ref, g_ref, m_ref, v_ref, d_ref, mo_ref, vo_ref):
        d, mn, vn = _adamw_math(w_ref[...], g_ref[...], m_ref[...], v_ref[...])
        d_ref[...] = d
        mo_ref[...] = mn
        vo_ref[...] = vn

    spec = pl.BlockSpec((tr, c), lambda i: (i, 0))
    return _pc(body, out_shape=(_sds((r, c), F32),) * 3, grid=(r // tr,), in_specs=[spec] * 4,
               out_specs=(spec,) * 3, compiler_params=_params("parallel"), name=name)(w, g, m, v)


def _adamw_halves(w, mine, theirs, m, v, core, name):
    r2, c = w.shape
    r = r2 // 2
    assert mine.shape == (r, c) and theirs.shape == (r, c)
    tr = _row_tile(r, c)
    nb = r // tr

    def body(core_ref, w_ref, a_ref, b_ref, m_ref, v_ref, g_ref, d_ref, mo_ref, vo_ref):
        g = jnp.where(pl.program_id(0) == core_ref[0], a_ref[...], b_ref[...])
        d, mn, vn = _adamw_math(w_ref[...], g, m_ref[...], v_ref[...])
        g_ref[...] = g
        d_ref[...] = d
        mo_ref[...] = mn
        vo_ref[...] = vn

    full = pl.BlockSpec((tr, c), lambda h, i, core_ref: (h * nb + i, 0))
    half = pl.BlockSpec((tr, c), lambda h, i, core_ref: (i, 0))
    grid_spec = pltpu.PrefetchScalarGridSpec(
        num_scalar_prefetch=1, grid=(2, nb), in_specs=[full, half, half, full, full], out_specs=(full,) * 4)
    return _pc(body, out_shape=(_sds((r2, c), F32),) * 4, grid_spec=grid_spec,
               compiler_params=_params("parallel", "parallel"), name=name)(core, w, mine, theirs, m, v)


def _ada_fwd(c16, ada_w, ada_b, name):
    d, n = ada_w.shape
    tn = _tile(n, 768, LANES)

    def body(c_ref, w_ref, b_ref, o_ref):
        cv = c_ref[...]
        sc = (cv * jax.nn.sigmoid(cv)).astype(BF16)
        o_ref[...] = lax.dot_general(sc, w_ref[...].astype(BF16), _NN, preferred_element_type=F32) + b_ref[...]

    return _pc(body, out_shape=_sds((16, n), F32), grid=(n // tn,),
               in_specs=[pl.BlockSpec((16, d), lambda j: (0, 0)), pl.BlockSpec((d, tn), lambda j: (0, j)),
                         pl.BlockSpec((1, tn), lambda j: (0, j))],
               out_specs=pl.BlockSpec((16, tn), lambda j: (0, j)),
               compiler_params=_params("parallel"), name=name)(c16, ada_w, ada_b)


def _ada_update(c16_t, dmod16, w, m, v, name, exchange=None):
    r, c = w.shape
    tr = _row_tile(r, c)

    def body(c_ref, dm_ref, w_ref, m_ref, v_ref, g_ref, d_ref, mo_ref, vo_ref):
        cv = c_ref[...]
        sc = (cv * jax.nn.sigmoid(cv)).astype(BF16)
        g = lax.dot_general(sc, dm_ref[...].astype(BF16), _NN, preferred_element_type=F32)
        d, mn, vn = _adamw_math(w_ref[...], g, m_ref[...], v_ref[...])
        g_ref[...] = g
        d_ref[...] = d
        mo_ref[...] = mn
        vo_ref[...] = vn

    spec = pl.BlockSpec((tr, c), lambda i: (i, 0))
    return _pc(body, exchange, out_shape=(_sds((r, c), F32),) * 4, grid=(r // tr,),
               in_specs=[pl.BlockSpec((tr, 16), lambda i: (i, 0)), pl.BlockSpec((16, c), lambda i: (0, 0)),
                         spec, spec, spec],
               out_specs=(spec,) * 4, compiler_params=_params("parallel"), name=name)(c16_t, dmod16, w, m, v)


def _add_half(dw, recv, core, name):
    _, _, r, w = dw.shape
    tr = _tile(r, 512, 16)

    def body(core_ref, a_ref, b_ref, o_ref):
        o_ref[...] = (a_ref[...].astype(F32) + b_ref[...].astype(F32)).astype(BF16)

    grid_spec = pltpu.PrefetchScalarGridSpec(
        num_scalar_prefetch=1, grid=(N_CHIPS, r // tr),
        in_specs=[pl.BlockSpec((None, None, tr, w), lambda s, i, core_ref: (s, core_ref[0], i, 0)),
                  pl.BlockSpec((None, tr, w), lambda s, i, core_ref: (s, i, 0))],
        out_specs=pl.BlockSpec((None, tr, w), lambda s, i, core_ref: (s, i, 0)))
    return _pc(body, out_shape=_sds((N_CHIPS, r, w), BF16), grid_spec=grid_spec,
               compiler_params=_params("parallel", "parallel"), name=name)(core, dw, recv)


def _sum_chips(own, recv, chip, name):
    _, r, w = own.shape
    tr = _tile(r, 512, 16)

    def body(chip_ref, own_ref, p_ref, o_ref):
        acc = own_ref[...].astype(F32)
        for q in range(N_CHIPS - 1):
            acc = acc + p_ref[q].astype(F32)
        o_ref[...] = acc

    grid_spec = pltpu.PrefetchScalarGridSpec(
        num_scalar_prefetch=1, grid=(r // tr,),
        in_specs=[pl.BlockSpec((None, tr, w), lambda i, chip_ref: (chip_ref[0], i, 0)),
                  pl.BlockSpec((N_CHIPS - 1, tr, w), lambda i, chip_ref: (0, i, 0))],
        out_specs=pl.BlockSpec((tr, w), lambda i, chip_ref: (i, 0)))
    return _pc(body, out_shape=_sds((r, w), F32), grid_spec=grid_spec,
               compiler_params=_params("parallel"), name=name)(chip, own, recv)


def _sum_devices(parts, name):
    nd, r, w = parts.shape

    def body(p_ref, o_ref):
        acc = p_ref[0]
        for q in range(1, nd):
            acc = acc + p_ref[q]
        o_ref[...] = acc

    return _pc(body, out_shape=_sds((r, w), F32), name=name)(parts)


def _place():
    x, y, c = lax.axis_index("x"), lax.axis_index("y"), lax.axis_index("c")
    chips = [(1 - x, y), (x, 1 - y), (1 - x, 1 - y)]
    return x, y, c, chips


def _small_gather_exchange(blk):
    r, w = blk.shape

    def copies(src, dst, send_sems, recv_sems):
        x, y, c, chips = _place()
        me, sibling = (x, y, c), (x, y, 1 - c)

        def rows(px, py, pc):
            return dst[0].at[pl.ds((4 * px + 2 * py + pc) * r, r), :]

        def copy(k, block, to, own=False):
            return _remote(src[0] if own else rows(*block), rows(*block), send_sems, recv_sems, k, to)

        mine = pltpu.make_async_copy(src[0], rows(*me), send_sems.at[7])
        first = [copy(0, me, sibling, own=True)] + [copy(1 + j, me, (*chip, c), own=True) for j, chip in enumerate(chips)]
        passed = [copy(4 + j, (*chip, c), sibling) for j, chip in enumerate(chips)]
        landed = [copy(1 + j, (*chip, c), me) for j, chip in enumerate(chips)]
        from_sibling = [copy(0, sibling, me)] + [copy(4 + j, (*chip, 1 - c), me) for j, chip in enumerate(chips)]
        return mine, first, passed, landed, from_sibling

    def start(src, dst, send_sems, recv_sems):
        mine, first, _, _, _ = copies(src, dst, send_sems, recv_sems)
        mine.start()
        for cp in first:
            cp.start()

    def finish(src, dst, send_sems, recv_sems):
        mine, first, passed, landed, from_sibling = copies(src, dst, send_sems, recv_sems)
        for arrival, onward in zip(landed, passed):
            arrival.wait_recv()
            onward.start()
        for cp in from_sibling:
            cp.wait_recv()
        for cp in first + passed:
            cp.wait_send()
        mine.wait()

    return _Exchange([blk], [_sds((N_DEV * r, w), blk.dtype)], 8, start, finish)


def _remote(src, dst, send_sems, recv_sems, k, to):
    return pltpu.make_async_remote_copy(src_ref=src, dst_ref=dst, send_sem=send_sems.at[k], recv_sem=recv_sems.at[k],
                                        device_id=to, device_id_type=MESH)


def _exchange_of(inputs, out_shapes, n_sems, copies, aliases=None):
    def start(src, dst, send_sems, recv_sems):
        for cp in copies(src, dst, send_sems, recv_sems)[0]:
            cp.start()

    def finish(src, dst, send_sems, recv_sems):
        sends, arrivals = copies(src, dst, send_sems, recv_sems)
        for cp in arrivals:
            cp.wait_recv()
        for cp in sends:
            cp.wait_send()

    return _Exchange(inputs, out_shapes, n_sems, start, finish, aliases)


def _run_exchange(ex, name):
    n_in, n_out = len(ex.inputs), len(ex.out_shapes)

    def body(*refs):
        src, dst = refs[:n_in], refs[n_in:n_in + n_out]
        send_sems, recv_sems = refs[n_in + n_out:]
        ex.start(src, dst, send_sems, recv_sems)
        ex.finish(src, dst, send_sems, recv_sems)

    ex.set_results(pl.pallas_call(
        body, out_shape=tuple(ex.out_shapes), in_specs=[_ANY] * n_in, out_specs=(_ANY,) * n_out,
        scratch_shapes=[pltpu.SemaphoreType.DMA((ex.n_sems,)), pltpu.SemaphoreType.DMA((ex.n_sems,))],
        input_output_aliases=ex.aliases, name=name)(*ex.inputs))


def _gather_ici_exchange(shards):
    n = len(shards)

    def copies(own, out, send_sems, recv_sems):
        x, y, c, chips = _place()
        my_chip = 2 * x + y
        sends, arrivals = [], []
        for i in range(n):
            for j, chip in enumerate(chips):
                to = (*chip, c)
                sends.append(_remote(own[i].at[c], out[i].at[my_chip, c], send_sems, recv_sems, 4 * i + j, to))
                arrivals.append(_remote(own[i].at[c], out[i].at[2 * chip[0] + chip[1], c], send_sems, recv_sems, 4 * i + j, to))
            whole = _remote(own[i], out[i].at[my_chip], send_sems, recv_sems, 4 * i + 3, (x, y, 1 - c))
            sends.append(whole)
            arrivals.append(whole)
        return sends, arrivals

    return _exchange_of(shards, [_sds((N_CHIPS,) + s.shape, s.dtype) for s in shards], 4 * n, copies)


def _gather_pass_exchange(gathered):
    n = len(gathered)

    def copies(src, dst, send_sems, recv_sems):
        x, y, c, chips = _place()
        sends, arrivals = [], []
        for i in range(n):
            for j, chip in enumerate(chips):
                idx = 2 * chip[0] + chip[1]
                sends.append(_remote(src[i].at[idx, c], dst[i].at[idx, c], send_sems, recv_sems, 3 * i + j, (x, y, 1 - c)))
                arrivals.append(_remote(src[i].at[idx, c], dst[i].at[idx, 1 - c], send_sems, recv_sems, 3 * i + j, (x, y, 1 - c)))
        return sends, arrivals

    return _exchange_of(gathered, [_sds(g.shape, g.dtype) for g in gathered], 3 * n, copies,
                        aliases={i: i for i in range(n)})


def _reduce_sibling_exchange(grads):
    n = len(grads)

    def copies(src, dst, send_sems, recv_sems):
        x, y, c, _ = _place()
        both = [_remote(src[i].at[s, 1 - c], dst[i].at[s], send_sems, recv_sems, N_CHIPS * i + s, (x, y, 1 - c))
                for i in range(n) for s in range(N_CHIPS)]
        return both, both

    return _exchange_of(grads, [_sds((N_CHIPS,) + g.shape[2:], g.dtype) for g in grads], N_CHIPS * n, copies)


def _reduce_chips_exchange(parts):
    n = len(parts)

    def copies(src, dst, send_sems, recv_sems):
        x, y, c, chips = _place()
        both = [_remote(src[i].at[2 * chip[0] + chip[1]], dst[i].at[j], send_sems, recv_sems, 3 * i + j, (*chip, c))
                for i in range(n) for j, chip in enumerate(chips)]
        return both, both

    return _exchange_of(parts, [_sds((N_CHIPS - 1,) + p.shape[1:], p.dtype) for p in parts], 3 * n, copies)


def _share_exchange(halves):
    n = len(halves)

    def copies(src, dst, send_sems, recv_sems):
        x, y, c, _ = _place()
        both = [_remote(src[i], dst[i], send_sems, recv_sems, i, (x, y, 1 - c)) for i in range(n)]
        return both, both

    return _exchange_of(halves, [_sds(h.shape, h.dtype) for h in halves], n, copies)


HEAD_ROWS = 16


class _WeightTraffic:
    def __init__(self, shards, core, chip):
        self.shards, self.core, self.chip = shards, core, chip
        self.gather, self.grads, self.reduce, self.chip_sums, self.half_sums, self.shared = {}, {}, {}, {}, {}, {}

    def gather_ici(self, grp):
        self.gather[grp] = _gather_ici_exchange(self.shards[grp])
        return self.gather[grp]

    def gather_pass(self, grp):
        self.gather[grp] = _gather_pass_exchange(self.gather[grp].results)
        return self.gather[grp]

    def weights(self, grp):
        return [g.reshape(-1, g.shape[-1]) for g in self.gather[grp].results]

    def reduce_sibling(self, grp, grads):
        self.grads[grp] = [g.reshape(N_CHIPS, 2, g.shape[0] // (2 * N_CHIPS), g.shape[1]) for g in grads]
        self.reduce[grp] = _reduce_sibling_exchange(self.grads[grp])
        return self.reduce[grp]

    def add_halves(self, grp):
        self.chip_sums[grp] = [_add_half(g, r, self.core, "add_half_%s%d" % (grp, i))
                               for i, (g, r) in enumerate(zip(self.grads[grp], self.reduce[grp].results))]

    def reduce_chips(self, grp):
        self.reduce[grp] = _reduce_chips_exchange(self.chip_sums[grp])
        return self.reduce[grp]

    def sum_chips(self, grp):
        self.half_sums[grp] = [_sum_chips(o, p, self.chip, "sum_chips_%s%d" % (grp, i))
                               for i, (o, p) in enumerate(zip(self.chip_sums[grp], self.reduce[grp].results))]

    def share(self, grp):
        self.shared[grp] = _share_exchange(self.half_sums[grp])
        return self.shared[grp]

    def totals(self, grp):
        return list(zip(self.half_sums[grp], self.shared[grp].results))


def _ffn_fwd(x, norm_g, shift, scale, gate, wg_t, wu_t, wd, tag, up_exchange=None, down_exchange=None):
    h = _norm_mod_fwd(x, norm_g, shift, scale, tag + "_norm_fwd")
    a, u, hid = _ffn_up(h, wg_t, wu_t, tag + "_up", exchange=up_exchange)
    wd = wd() if callable(wd) else wd
    x_out, f = _mm(hid, wd, "nn", F32, tag + "_down", res=x, gate=gate, aux_dtype=BF16,
                   exchange=down_exchange() if down_exchange else None)
    return x_out, (h, a, u, hid, f)


def _ffn_bwd(dx_out, df, x, saved, norm_g, scale, wg_t, wu_t, wd, tag, traffic, below=None, dact_exchange=None,
             dw_exchange=None, finish_reduction=False):
    h, a, u, hid, _ = saved
    f_below, gate_below = below if below else (None, None)
    da, du = _ffn_dact(df, wd, a, u, tag + "_dact", exchange=dact_exchange)
    dwd = _mm(hid, df, "tn", BF16, tag + "_dwd", exchange=dw_exchange() if dw_exchange else None)
    if not finish_reduction:
        dwg_t = _mm(da, h, "tn", BF16, tag + "_dwg")
        dwu_t = _mm(du, h, "tn", BF16, tag + "_dwu")
        dx, dshift, dscale, dnorm_g, *gated = _norm_mod_bwd(
            ([da, du], [wg_t, wu_t]), x, norm_g, scale, dx_out, tag + "_dh_norm_bwd", f=f_below, gate=gate_below,
            exchange=traffic.reduce_sibling(tag, [dwg_t, dwu_t, dwd]))
        traffic.add_halves(tag)
        return dx, (dshift, dscale, dnorm_g), gated
    kd, kg, ku = tag + "_wd", tag + "_wg", tag + "_wu"
    dwg_t = _mm(da, h, "tn", BF16, tag + "_dwg", exchange=traffic.reduce_sibling(kd, [dwd]))
    traffic.add_halves(kd)
    dwu_t = _mm(du, h, "tn", BF16, tag + "_dwu",
                exchange=_join(traffic.reduce_chips(kd), traffic.reduce_sibling(kg, [dwg_t])))
    traffic.add_halves(kg)
    half = x.shape[0] // 2
    top = _norm_mod_bwd(([da, du], [wg_t, wu_t]), x, norm_g, scale, dx_out, tag + "_dh_norm_bwd_top", f=f_below,
                        gate=gate_below, rows=(0, half),
                        exchange=_join(traffic.reduce_chips(kg), traffic.reduce_sibling(ku, [dwu_t])))
    traffic.add_halves(ku)
    traffic.sum_chips(kd)
    traffic.sum_chips(kg)
    bottom = _norm_mod_bwd(([da, du], [wg_t, wu_t]), x, norm_g, scale, dx_out, tag + "_dh_norm_bwd_bottom", f=f_below,
                           gate=gate_below, rows=(half, half),
                           exchange=_join(traffic.reduce_chips(ku), traffic.share(kd), traffic.share(kg)))
    traffic.sum_chips(ku)
    dx, dshift, dscale, dnorm_g, *gated = [jnp.concatenate([a, b]) if a.shape[0] == half else a + b
                                           for a, b in zip(top, bottom)]
    return dx, (dshift, dscale, dnorm_g), gated


def _layer_step(x, target, mod, gains, forget_bias, conv_w, traffic, att_w, in_shard, in_rows):
    sh1, sc1, g1, sh2, sc2, g2, sh3, sc3, g3 = mod
    norm1_g, norm2_g, norm3_g, final_g, group_g = gains
    s, d = x.shape
    n_heads = att_w // HEAD_DIM
    npair = n_heads // 2
    gate1, gate3 = 0.5 * g1, 0.5 * g3

    def split_w_in(w_in_pad):
        w_in_t = w_in_pad.reshape(N_CHIPS, in_rows, d)[:, :in_shard].reshape(N_CHIPS * in_shard, d)
        return (w_in_t[:3 * att_w], _pad_rows(w_in_t[3 * att_w:3 * att_w + n_heads], LANES), w_in_t[3 * att_w + n_heads:])

    wg1_t, wu1_t = traffic.weights("ffn1_gu")

    def wd1_ready():
        _run_exchange(traffic.gather_pass("ffn1_d"), "gather_ffn1_down_pass")
        return traffic.weights("ffn1_d")[0]

    x1, saved1 = _ffn_fwd(x, norm1_g, sh1, sc1, gate1, wg1_t, wu1_t, wd1_ready, "ffn1",
                          up_exchange=_join(traffic.gather_ici("ffn1_d"), traffic.gather_ici("mix")),
                          down_exchange=lambda: traffic.gather_pass("mix"))
    wd1 = traffic.weights("ffn1_d")[0]
    w_in_pad, w_out = traffic.weights("mix")
    wqkv_t, wf_t, wbcx_t = split_w_in(w_in_pad)

    h2 = _norm_mod_fwd(x1, norm2_g, sh2, sc2, "mix_norm_fwd")
    qkv = _mm(h2, wqkv_t, "nt", BF16, "mix_proj_qkv")
    bcx = _mm(h2, wbcx_t, "nt", F32, "mix_proj_bcx")
    flog = _mm(h2, wf_t, "nt", F32, "mix_proj_f")
    flog_t = jnp.pad(flog[:, :n_heads].T, ((0, HEAD_ROWS - n_heads), (0, 0)))
    bias_col = jnp.pad(forget_bias, (0, HEAD_ROWS - n_heads))[:, None]
    f_pieces = _forget_fwd(flog_t, bias_col, "forget_fwd")
    qa, ka, va = _attn_prep(qkv, f_pieces, "attn_prep")
    att, lse = _attn_fwd(qa, ka, va, "attn_fwd", exchange=traffic.gather_ici("ffn2"))
    cv = _conv_fwd(bcx, conv_w, "conv_fwd")
    yn = _gnorm_fwd(att, cv, group_g, "gnorm_fwd")
    x2, mix = _mm(yn, w_out, "nn", F32, "mix_out", res=x1, gate=g2, aux_dtype=BF16, exchange=traffic.gather_pass("ffn2"))
    wg2_t, wu2_t, wd2 = traffic.weights("ffn2")

    x3, saved3 = _ffn_fwd(x2, norm3_g, sh3, sc3, gate3, wg2_t, wu2_t, wd2, "ffn2")

    dx3, loss_row, dfinal_g, df2, dgate3 = _final_loss(x3, final_g, target, saved3[4], gate3, "final_loss")

    dx2, (dsh3, dsc3, dnorm3_g), (dmix, dg2) = _ffn_bwd(
        dx3, df2, x2, saved3, norm3_g, sc3, wg2_t, wu2_t, wd2, "ffn2", traffic, below=(mix, g2))
    dyn = _mm(dmix, w_out, "nt", F32, "mix_out_dyn")
    dw_out = _mm(yn, dmix, "tn", BF16, "mix_out_dw")
    datt, dcv, dgroup_g = _gnorm_bwd(dyn, att, cv, group_g, "gnorm_bwd")
    db, dc, dxc, dconv_w = _conv_bwd(dcv, bcx, conv_w, "conv_bwd")
    dbcx = jnp.concatenate([db, dc, dxc], axis=1)
    dq, dk, dv, qx, kx = _attn_bwd(qa, ka, va, datt, att, lse, "attn_bwd", exchange=traffic.reduce_chips("ffn2"))
    traffic.sum_chips("ffn2")
    dqkv = jnp.concatenate([dq.astype(BF16), dk, dv], axis=1)
    df_t = _decay_grads(qx, kx, "decay_grads")[:, :HEAD_ROWS].T
    dflog_t, dbias_col = _forget_bwd(df_t, flog_t, bias_col, "forget_bwd")
    dflog = jnp.pad(dflog_t[:n_heads].T, ((0, 0), (0, LANES - n_heads))).astype(BF16)
    dwqkv_t = _mm(dqkv, h2, "tn", BF16, "mix_dw_qkv", exchange=traffic.share("ffn2"))
    dwbcx_t = _mm(dbcx, h2, "tn", BF16, "mix_dw_bcx")
    dwf_t = _mm(dflog, h2, "tn", BF16, "mix_dw_f")
    dw_in_t = jnp.concatenate([dwqkv_t, dwf_t[:n_heads], dwbcx_t], axis=0).reshape(N_CHIPS, in_shard, d)
    dw_in_t = jnp.pad(dw_in_t, ((0, 0), (0, in_rows - in_shard), (0, 0))).reshape(N_CHIPS * in_rows, d)
    dx1, dsh2, dsc2, dnorm2_g, df1, dgate1 = _norm_mod_bwd(
        ([dqkv, dbcx, dflog], [wqkv_t, wbcx_t, wf_t]), x1, norm2_g, sc2, dx2, "mix_dh_norm_bwd", f=saved1[4], gate=gate1,
        exchange=traffic.reduce_sibling("mix", [dw_in_t, dw_out]))
    traffic.add_halves("mix")

    def share_mix():
        traffic.sum_chips("mix")
        return traffic.share("mix")

    dx, (dsh1, dsc1, dnorm1_g), _ = _ffn_bwd(
        dx1, df1, x, saved1, norm1_g, sc1, wg1_t, wu1_t, wd1, "ffn1", traffic,
        dact_exchange=traffic.reduce_chips("mix"), dw_exchange=share_mix, finish_reduction=True)

    dmod = [dsh1, dsc1, 0.5 * dgate1, dsh2, dsc2, dg2, dsh3, dsc3, 0.5 * dgate3]
    dgains = [dnorm1_g, dnorm2_g, dnorm3_g, dfinal_g, dgroup_g]
    dbias = dbias_col[:n_heads, 0]
    return dx, loss_row, dmod, dgains, dbias, dconv_w


SMALL_ROWS = 24
ROW_GAINS, ROW_LOSS, ROW_FORGET, ROW_CONV, ROW_MOD = 0, 5, 6, 7, 10
PROW_ADA_B, PROW_GAINS, PROW_FORGET, PROW_CONV = 0, 9, 14, 15


def _round_up(n, m):
    return -(-n // m) * m


def _pad_rows(a, rows):
    return jnp.pad(a, ((0, rows - a.shape[0]), (0, 0)))


def _halves(a):
    return a.reshape(2, a.shape[0] // 2, a.shape[1])


def _rows_at(a, r0, total, width):
    return jnp.pad(a, ((r0, total - r0 - a.shape[0]), (0, width - a.shape[1])))


def kernel(x, c, ada_w, ada_b, norm1_g, ffn1_w_gate, ffn1_w_up, ffn1_w_down, norm2_g, w_in, forget_bias, conv_w, group_norm_g, w_out, norm3_g, ffn2_w_gate, ffn2_w_up, ffn2_w_down, final_g, loss_target, m_ada_w, m_ada_b, m_norm1_g, m_ffn1_w_gate, m_ffn1_w_up, m_ffn1_w_down, m_norm2_g, m_w_in, m_forget_bias, m_conv_w, m_group_norm_g, m_w_out, m_norm3_g, m_ffn2_w_gate, m_ffn2_w_up, m_ffn2_w_down, m_final_g, v_ada_w, v_ada_b, v_norm1_g, v_ffn1_w_gate, v_ffn1_w_up, v_ffn1_w_down, v_norm2_g, v_w_in, v_forget_bias, v_conv_w, v_group_norm_g, v_w_out, v_norm3_g, v_ffn2_w_gate, v_ffn2_w_up, v_ffn2_w_down, v_final_g):
    xi, yi, ci = lax.axis_index("x"), lax.axis_index("y"), lax.axis_index("c")
    chip = 2 * xi + yi
    dev = 4 * xi + 2 * yi + ci
    _, s, d = x.shape
    att_w = d // 2
    conv_width = d - att_w
    n_heads = att_w // HEAD_DIM
    in_shard = w_in.shape[1]
    in_rows = _round_up(in_shard, 32)
    cs = conv_w.shape[1]
    mod_shard = ada_w.shape[1]
    assert N_MOD * d == N_CHIPS * mod_shard and conv_width == N_CHIPS * cs and n_heads % 2 == 0

    def t_bf(w):
        return w.T.astype(BF16)

    shards = {"ffn1_gu": [_halves(t_bf(ffn1_w_gate)), _halves(t_bf(ffn1_w_up))], "ffn1_d": [_halves(ffn1_w_down.astype(BF16))],
              "mix": [_halves(_pad_rows(t_bf(w_in), in_rows)), _halves(w_out.astype(BF16))],
              "ffn2": [_halves(t_bf(ffn2_w_gate)), _halves(t_bf(ffn2_w_up)), _halves(ffn2_w_down.astype(BF16))]}
    core = ci.astype(jnp.int32).reshape(1)
    chip_arr = chip.astype(jnp.int32).reshape(1)
    traffic = _WeightTraffic(shards, core, chip_arr)

    cond = _small_gather_exchange(_rows_at(c, 0, 8, d) + _rows_at(conv_w, 1, 8, d))
    _run_exchange(_join(traffic.gather_ici("ffn1_gu"), cond), "gather_ffn1_ici")
    got0 = cond.results[0].reshape(N_DEV, 8, d)
    c16 = _pad_rows(got0[:, 0, :], 16)
    conv_full = got0[0::2, 1:1 + CONV_K, :cs].transpose(1, 0, 2).reshape(CONV_K, conv_width)

    ada_b_mine = lax.dynamic_slice(ada_b, (chip * mod_shard,), (mod_shard,))[None, :]
    mods = _small_gather_exchange(_ada_fwd(c16, ada_w, ada_b_mine, "ada_fwd"))
    _run_exchange(_join(traffic.gather_pass("ffn1_gu"), mods), "gather_ffn1_pass")
    got1 = mods.results[0].reshape(N_DEV, 16, mod_shard)
    mod_mine = lax.dynamic_index_in_dim(got1[0::2], dev, axis=1, keepdims=False).reshape(N_MOD, d)
    mod = [mod_mine[i:i + 1] for i in range(N_MOD)]

    gains = [g[None, :] for g in (norm1_g, norm2_g, norm3_g, final_g, group_norm_g)]
    dx, loss_row, dmod, dgains, dbias, dconv_w = _layer_step(
        x[0], loss_target[0], mod, gains, forget_bias, conv_full, traffic, att_w, in_shard, in_rows)

    pack = sum(_rows_at(g, ROW_GAINS + i, SMALL_ROWS, d) for i, g in enumerate(dgains))
    pack += _rows_at(loss_row, ROW_LOSS, SMALL_ROWS, d) + _rows_at(dbias[None, :], ROW_FORGET, SMALL_ROWS, d)
    pack += _rows_at(dconv_w, ROW_CONV, SMALL_ROWS, d)
    pack += sum(_rows_at(g, ROW_MOD + i, SMALL_ROWS, d) for i, g in enumerate(dmod))
    small = _small_gather_exchange(pack)
    _run_exchange(_join(traffic.share("ffn1_wu"), small), "gather_small_grads")
    got2 = small.results[0].reshape(N_DEV, SMALL_ROWS, d)
    tot = _sum_devices(got2, "sum_small_grads")
    loss = tot[ROW_LOSS, 0]
    grad_ada_b = tot[ROW_MOD:ROW_MOD + N_MOD].reshape(N_MOD * d)
    grad_conv = lax.dynamic_slice(tot[ROW_CONV:ROW_CONV + CONV_K], (0, chip * cs), (CONV_K, cs))
    dmod_all = got2[:, ROW_MOD:ROW_MOD + N_MOD, :].reshape(N_DEV, N_MOD * d)
    dmod16 = _pad_rows(lax.dynamic_slice(dmod_all, (0, chip * mod_shard), (N_DEV, mod_shard)), 16)

    out = {"ada_w": tuple(_ada_update(c16.T, dmod16, ada_w, m_ada_w, v_ada_w, "adamw_ada_w"))}
    totals = (traffic.totals("ffn1_wg") + traffic.totals("ffn1_wu") + traffic.totals("ffn1_wd")
              + traffic.totals("mix") + traffic.totals("ffn2"))

    names = ("ffn1_w_gate", "ffn1_w_up", "ffn1_w_down", "w_in", "w_out", "ffn2_w_gate", "ffn2_w_up", "ffn2_w_down")
    transposed = ("ffn1_w_gate", "ffn1_w_up", "w_in", "ffn2_w_gate", "ffn2_w_up")
    params = {"ffn1_w_gate": (ffn1_w_gate, m_ffn1_w_gate, v_ffn1_w_gate), "ffn1_w_up": (ffn1_w_up, m_ffn1_w_up, v_ffn1_w_up),
              "ffn1_w_down": (ffn1_w_down, m_ffn1_w_down, v_ffn1_w_down), "w_in": (w_in, m_w_in, v_w_in),
              "w_out": (w_out, m_w_out, v_w_out), "ffn2_w_gate": (ffn2_w_gate, m_ffn2_w_gate, v_ffn2_w_gate),
              "ffn2_w_up": (ffn2_w_up, m_ffn2_w_up, v_ffn2_w_up), "ffn2_w_down": (ffn2_w_down, m_ffn2_w_down, v_ffn2_w_down)}
    for name_, (mine, theirs) in zip(names, totals):
        w, m, v = params[name_]
        if name_ in transposed:
            w, m, v = w.T, m.T, v.T
        if name_ == "w_in":
            both = jnp.where(ci == 0, jnp.concatenate([mine, theirs]), jnp.concatenate([theirs, mine]))[:in_shard]
            res = (both,) + tuple(_adamw(w, both, m, v, "adamw_" + name_))
        else:
            res = _adamw_halves(w, mine, theirs, m, v, core, "adamw_" + name_)
        out[name_] = tuple(r.T for r in res) if name_ in transposed else tuple(res)

    def small_pack(ada_b_, gains_, forget_, conv_):
        p = _rows_at(ada_b_.reshape(N_MOD, d), PROW_ADA_B, SMALL_ROWS, d)
        p += sum(_rows_at(g[None, :], PROW_GAINS + i, SMALL_ROWS, d) for i, g in enumerate(gains_))
        p += _rows_at(forget_[None, :], PROW_FORGET, SMALL_ROWS, d) + _rows_at(conv_, PROW_CONV, SMALL_ROWS, d)
        return p

    g_gains = [tot[ROW_GAINS + i] for i in range(5)]
    g_forget = tot[ROW_FORGET, :n_heads]
    sw = small_pack(ada_b, (norm1_g, norm2_g, norm3_g, final_g, group_norm_g), forget_bias, conv_w)
    sm = small_pack(m_ada_b, (m_norm1_g, m_norm2_g, m_norm3_g, m_final_g, m_group_norm_g), m_forget_bias, m_conv_w)
    sv = small_pack(v_ada_b, (v_norm1_g, v_norm2_g, v_norm3_g, v_final_g, v_group_norm_g), v_forget_bias, v_conv_w)
    sg = small_pack(grad_ada_b, g_gains, g_forget, grad_conv)
    small = (sg,) + tuple(_adamw(sw, sg, sm, sv, "adamw_small"))

    def unpack(p):
        r = {"ada_b": p[PROW_ADA_B:PROW_ADA_B + N_MOD].reshape(N_MOD * d), "forget_bias": p[PROW_FORGET, :n_heads],
             "conv_w": p[PROW_CONV:PROW_CONV + CONV_K, :cs]}
        for i, nm in enumerate(("norm1_g", "norm2_g", "norm3_g", "final_g", "group_norm_g")):
            r[nm] = p[PROW_GAINS + i]
        return r

    small = [unpack(p) for p in small]
    order = ("ada_w", "ada_b", "norm1_g", "ffn1_w_gate", "ffn1_w_up", "ffn1_w_down", "norm2_g", "w_in", "forget_bias",
             "conv_w", "group_norm_g", "w_out", "norm3_g", "ffn2_w_gate", "ffn2_w_up", "ffn2_w_down", "final_g")
    result = [loss, dx[None]]
    for k in range(4):
        result += [out[nm][k] if nm in out else small[k][nm] for nm in order]
    return tuple(result)
```

```python
import functools
import math

import jax
import jax.numpy as jnp
from jax import lax
from jax.experimental import pallas as pl
from jax.experimental.pallas import tpu as pltpu

F32 = jnp.float32
BF16 = jnp.bfloat16

HEAD_DIM = 64
CONV_K = 3
N_MOD = 9
EPS = 1e-6
ADAM_LR = 0.001
ADAM_B1 = 0.9
ADAM_B2 = 0.999
ADAM_EPS = 1e-08
ADAM_WD = 0.01
ADAM_STEP = 10

LANES = 128
N_CHIPS = 4
N_DEV = 8
VMEM_LIMIT_BYTES = 56 * 1024 * 1024
MAX_CONTRACTION = 4096
NEG_BIG = -1e30
MESH = pl.DeviceIdType.MESH

_NT = (((1,), (1,)), ((), ()))
_NN = (((1,), (0,)), ((), ()))
_TN = (((0,), (0,)), ((), ()))


def _params(*sem):
    return pltpu.CompilerParams(dimension_semantics=sem, vmem_limit_bytes=VMEM_LIMIT_BYTES)


class _Exchange:
    def __init__(self, inputs, out_shapes, n_sems, start, finish, aliases=None):
        self.inputs, self.out_shapes, self.n_sems = list(inputs), list(out_shapes), n_sems
        self.start, self.finish, self.aliases = start, finish, dict(aliases or {})
        self.results = None

    def set_results(self, results):
        self.results = list(results)


class _SemaphoreWindow:
    def __init__(self, sems, base):
        self.sems, self.base = sems, base
        self.at = self

    def __getitem__(self, k):
        return self.sems.at[self.base + k]


class _JoinedExchange(_Exchange):
    def __init__(self, parts):
        self.parts = parts
        aliases, i0, o0 = {}, 0, 0
        for p in parts:
            aliases.update({i0 + a: o0 + b for a, b in p.aliases.items()})
            i0, o0 = i0 + len(p.inputs), o0 + len(p.out_shapes)

        def each(method, src, dst, send_sems, recv_sems):
            i0 = o0 = s0 = 0
            for p in parts:
                i1, o1 = i0 + len(p.inputs), o0 + len(p.out_shapes)
                getattr(p, method)(src[i0:i1], dst[o0:o1], _SemaphoreWindow(send_sems, s0), _SemaphoreWindow(recv_sems, s0))
                i0, o0, s0 = i1, o1, s0 + p.n_sems

        super().__init__([a for p in parts for a in p.inputs], [o for p in parts for o in p.out_shapes],
                         sum(p.n_sems for p in parts), functools.partial(each, "start"), functools.partial(each, "finish"),
                         aliases)

    def set_results(self, results):
        o0 = 0
        for p in self.parts:
            p.set_results(results[o0:o0 + len(p.out_shapes)])
            o0 += len(p.out_shapes)


def _join(*parts):
    return parts[0] if len(parts) == 1 else _JoinedExchange(list(parts))


def _pc(body, exchange=None, **kw):
    if exchange is None:
        return pl.pallas_call(body, **kw)
    grid = kw["grid"]
    single = not isinstance(kw["out_shape"], (tuple, list))
    out_shape = [kw["out_shape"]] if single else list(kw["out_shape"])
    out_specs = [kw["out_specs"]] if single else list(kw["out_specs"])
    in_specs = list(kw["in_specs"])
    scratch = list(kw.get("scratch_shapes", ()))
    n_in, n_out, n_scr = len(in_specs), len(out_shape), len(scratch)
    n_xi, n_xo = len(exchange.inputs), len(exchange.out_shapes)

    def wrapped(*refs):
        pos = [n_in, n_in + n_xi, n_in + n_xi + n_out, n_in + n_xi + n_out + n_xo]
        ins, x_in, outs, x_out = refs[:pos[0]], refs[pos[0]:pos[1]], refs[pos[1]:pos[2]], refs[pos[2]:pos[3]]
        scr = refs[pos[3]:pos[3] + n_scr]
        send_sems, recv_sems = refs[pos[3] + n_scr:]
        ids = [pl.program_id(a) for a in range(len(grid))]
        first = functools.reduce(jnp.logical_and, [i == 0 for i in ids])
        last = functools.reduce(jnp.logical_and, [i == g - 1 for i, g in zip(ids, grid)])

        @pl.when(first)
        def _():
            exchange.start(x_in, x_out, send_sems, recv_sems)

        body(*ins, *outs, *scr)

        @pl.when(last)
        def _():
            exchange.finish(x_in, x_out, send_sems, recv_sems)

    call = pl.pallas_call(
        wrapped, out_shape=tuple(out_shape) + tuple(exchange.out_shapes), grid=grid,
        in_specs=in_specs + [_ANY] * n_xi, out_specs=tuple(out_specs) + (_ANY,) * n_xo,
        scratch_shapes=scratch + [pltpu.SemaphoreType.DMA((exchange.n_sems,)), pltpu.SemaphoreType.DMA((exchange.n_sems,))],
        input_output_aliases={n_in + a: n_out + b for a, b in exchange.aliases.items()},
        compiler_params=_params(*(["arbitrary"] * len(grid))), name=kw["name"])

    def run(*args):
        res = call(*args, *exchange.inputs)
        exchange.set_results(res[n_out:])
        return res[0] if single else tuple(res[:n_out])

    return run


_ANY = pl.BlockSpec(memory_space=pl.ANY)


def _tile(n, pref, mult):
    best = None
    t = mult
    while t <= min(n, pref):
        if n % t == 0:
            best = t
        t += mult
    return n if best is None else best


def _sds(shape, dtype):
    return jax.ShapeDtypeStruct(shape, dtype)


def _vec_spec(d):
    return pl.BlockSpec((1, d), lambda *_: (0, 0))


def _norm_mod_fwd(x, g, shift, scale, name):
    s, d = x.shape
    tr = _tile(s, 512, 16)

    def body(x_ref, g_ref, sh_ref, sc_ref, h_ref):
        xv = x_ref[...]
        rstd = lax.rsqrt(jnp.mean(xv * xv, axis=-1, keepdims=True) + EPS)
        n = xv * rstd * g_ref[...]
        h_ref[...] = (n * (1.0 + sc_ref[...]) + sh_ref[...]).astype(BF16)

    row = pl.BlockSpec((tr, d), lambda i: (i, 0))
    return _pc(body, out_shape=_sds((s, d), BF16), grid=(s // tr,),
               in_specs=[row, _vec_spec(d), _vec_spec(d), _vec_spec(d)], out_specs=row,
               compiler_params=_params("parallel"), name=name)(x, g, shift, scale)


def _through_gate(dx, f_ref, gate_ref, df_ref, dgate_ref):
    df_ref[...] = (dx * gate_ref[...]).astype(BF16)
    dgate_ref[...] += jnp.sum(dx * f_ref[...].astype(F32), axis=0, keepdims=True)


def _norm_mod_bwd(dh, x, g, scale, dres, name, f=None, gate=None, rows=None, exchange=None):
    d = x.shape[1]
    first_row, s = rows if rows else (0, x.shape[0])
    gated = f is not None
    terms = list(zip(*dh)) if isinstance(dh, tuple) else None
    tr = _tile(s, 256, 16)
    b0 = first_row // tr
    assert first_row % tr == 0
    n_lead = 2 * len(terms) if terms else 1

    def body(*refs):
        lead, (x_ref, g_ref, sc_ref, dres_ref), rest = refs[:n_lead], refs[n_lead:n_lead + 4], refs[n_lead + 4:]
        f_ref, gate_ref = rest[:2] if gated else (None, None)
        dx_ref, dsh_ref, dsc_ref, dg_ref = rest[2:6] if gated else rest[:4]
        df_ref, dgate_ref = rest[6:8] if gated else (None, None)

        @pl.when(pl.program_id(0) == 0)
        def _():
            for ref in (dsh_ref, dsc_ref, dg_ref) + ((dgate_ref,) if gated else ()):
                ref[...] = jnp.zeros_like(ref)

        if terms:
            dhv = lax.dot_general(lead[0][...], lead[1][...], _NN, preferred_element_type=F32)
            for p in range(1, len(terms)):
                dhv += lax.dot_general(lead[2 * p][...], lead[2 * p + 1][...], _NN, preferred_element_type=F32)
        else:
            dhv = lead[0][...]
        xv = x_ref[...]
        gv = g_ref[...]
        rstd = lax.rsqrt(jnp.mean(xv * xv, axis=-1, keepdims=True) + EPS)
        xhat = xv * rstd
        dn = dhv * (1.0 + sc_ref[...])
        dsh_ref[...] += jnp.sum(dhv, axis=0, keepdims=True)
        dsc_ref[...] += jnp.sum(dhv * (xhat * gv), axis=0, keepdims=True)
        dg_ref[...] += jnp.sum(dn * xhat, axis=0, keepdims=True)
        dxh = dn * gv
        proj = jnp.mean(dxh * xhat, axis=-1, keepdims=True)
        dx = dres_ref[...] + rstd * (dxh - xhat * proj)
        dx_ref[...] = dx
        if gated:
            _through_gate(dx, f_ref, gate_ref, df_ref, dgate_ref)

    row = pl.BlockSpec((tr, d), lambda i: (b0 + i, 0))
    out_row = pl.BlockSpec((tr, d), lambda i: (i, 0))
    vec = _vec_spec(d)
    if terms:
        in_specs, args = [], []
        for l, r in terms:
            assert l.shape[1] == r.shape[0] <= MAX_CONTRACTION and r.shape[1] == d
            in_specs += [pl.BlockSpec((tr, l.shape[1]), lambda i: (b0 + i, 0)), pl.BlockSpec(r.shape, lambda i: (0, 0))]
            args += [l, r]
    else:
        in_specs, args = [row], [dh]
    in_specs += [row, vec, vec, row]
    args += [x, g, scale, dres]
    out_shape = [_sds((s, d), F32), _sds((1, d), F32), _sds((1, d), F32), _sds((1, d), F32)]
    out_specs = [out_row, vec, vec, vec]
    if gated:
        out_shape += [_sds((s, d), BF16), _sds((1, d), F32)]
        out_specs += [out_row, vec]
        in_specs += [row, vec]
        args += [f, gate]
    return _pc(body, exchange, out_shape=tuple(out_shape), grid=(s // tr,), in_specs=in_specs,
               out_specs=tuple(out_specs), compiler_params=_params("arbitrary"), name=name)(*args)


def _final_loss(x, g, target, f, gate, name):
    s, d = x.shape
    tr = _tile(s, 256, 16)
    nsteps = s // tr

    def body(x_ref, g_ref, t_ref, f_ref, gate_ref, dx_ref, loss_ref, dg_ref, df_ref, dgate_ref):
        i = pl.program_id(0)

        @pl.when(i == 0)
        def _():
            loss_ref[...] = jnp.zeros_like(loss_ref)
            dg_ref[...] = jnp.zeros_like(dg_ref)
            dgate_ref[...] = jnp.zeros_like(dgate_ref)

        xv = x_ref[...]
        gv = g_ref[...]
        rstd = lax.rsqrt(jnp.mean(xv * xv, axis=-1, keepdims=True) + EPS)
        xhat = xv * rstd
        err = xhat * gv - t_ref[...]
        dy = err * (1.0 / d)
        loss_ref[...] += jnp.sum(0.5 * err * dy, axis=0, keepdims=True)
        dg_ref[...] += jnp.sum(dy * xhat, axis=0, keepdims=True)
        dxh = dy * gv
        proj = jnp.mean(dxh * xhat, axis=-1, keepdims=True)
        dx = rstd * (dxh - xhat * proj)
        dx_ref[...] = dx
        _through_gate(dx, f_ref, gate_ref, df_ref, dgate_ref)

        @pl.when(i == nsteps - 1)
        def _():
            loss_ref[...] = jnp.broadcast_to(jnp.sum(loss_ref[...], axis=-1, keepdims=True), loss_ref.shape)

    row = pl.BlockSpec((tr, d), lambda i: (i, 0))
    vec = _vec_spec(d)
    return _pc(body, out_shape=(_sds((s, d), F32), _sds((1, d), F32), _sds((1, d), F32), _sds((s, d), BF16), _sds((1, d), F32)),
               grid=(nsteps,), in_specs=[row, vec, row, row, vec], out_specs=(row, vec, vec, row, vec),
               compiler_params=_params("arbitrary"), name=name)(x, g, target, f, gate)


def _mm(lhs, rhs, dims, out_dtype, name, res=None, gate=None, aux_dtype=None, exchange=None):
    lhs_list = list(lhs) if isinstance(lhs, (list, tuple)) else [lhs]
    rhs_list = list(rhs) if isinstance(rhs, (list, tuple)) else [rhs]
    n_terms = len(lhs_list)
    assert n_terms == len(rhs_list)
    m = lhs_list[0].shape[1 if dims == "tn" else 0]
    n = rhs_list[0].shape[0 if dims == "nt" else 1]
    tn = _tile(n, 1024, LANES)
    tm = _tile(m, 512, LANES if dims == "tn" else 16)
    dn = {"nn": _NN, "nt": _NT, "tn": _TN}[dims]
    in_specs, args = [], []
    for l, r in zip(lhs_list, rhs_list):
        k = l.shape[0 if dims == "tn" else 1]
        assert k == r.shape[1 if dims == "nt" else 0] and k <= MAX_CONTRACTION, (l.shape, r.shape, dims)
        in_specs.append(pl.BlockSpec((k, tm), lambda i, j: (0, i)) if dims == "tn" else pl.BlockSpec((tm, k), lambda i, j: (i, 0)))
        in_specs.append(pl.BlockSpec((tn, k), lambda i, j: (j, 0)) if dims == "nt" else pl.BlockSpec((k, tn), lambda i, j: (0, j)))
        args += [l, r]
    out_spec = pl.BlockSpec((tm, tn), lambda i, j: (i, j))
    has_res, has_gate, has_aux = res is not None, gate is not None, aux_dtype is not None

    def body(*refs):
        refs = list(refs)
        pos = 2 * n_terms
        res_ref = gate_ref = aux_ref = None
        if has_res:
            res_ref = refs[pos]; pos += 1
        if has_gate:
            gate_ref = refs[pos]; pos += 1
        out_ref = refs[pos]; pos += 1
        if has_aux:
            aux_ref = refs[pos]
        acc = lax.dot_general(refs[0][...], refs[1][...], dn, preferred_element_type=F32)
        for p in range(1, n_terms):
            acc += lax.dot_general(refs[2 * p][...], refs[2 * p + 1][...], dn, preferred_element_type=F32)
        if has_aux:
            aux_ref[...] = acc.astype(aux_dtype)
        if has_gate:
            acc = acc * gate_ref[...]
        if has_res:
            acc = res_ref[...] + acc
        out_ref[...] = acc.astype(out_dtype)

    if has_res:
        in_specs.append(out_spec); args.append(res)
    if has_gate:
        in_specs.append(pl.BlockSpec((1, tn), lambda i, j: (0, j))); args.append(gate)
    out_shape = [_sds((m, n), out_dtype)]
    out_specs = [out_spec]
    if has_aux:
        out_shape.append(_sds((m, n), aux_dtype)); out_specs.append(out_spec)
    outs = _pc(body, exchange, out_shape=tuple(out_shape), grid=(m // tm, n // tn), in_specs=in_specs,
               out_specs=tuple(out_specs), compiler_params=_params("parallel", "parallel"), name=name)(*args)
    return outs if has_aux else outs[0]


def _ffn_up(h, wg_t, wu_t, name, exchange=None):
    s, d = h.shape
    f = wg_t.shape[0]
    tm = _tile(s, 1024, 16)
    tn = _tile(f, 256, LANES)

    def body(h_ref, wg_ref, wu_ref, a_ref, u_ref, hid_ref):
        hv = h_ref[...]
        a = lax.dot_general(hv, wg_ref[...], _NT, preferred_element_type=F32)
        u = lax.dot_general(hv, wu_ref[...], _NT, preferred_element_type=F32)
        a_ref[...] = a.astype(BF16)
        u_ref[...] = u.astype(BF16)
        hid_ref[...] = (a * jax.nn.sigmoid(a) * u).astype(BF16)

    hs = pl.BlockSpec((tm, d), lambda i, j: (i, 0))
    ws = pl.BlockSpec((tn, d), lambda i, j: (j, 0))
    os_ = pl.BlockSpec((tm, tn), lambda i, j: (i, j))
    return _pc(body, exchange, out_shape=(_sds((s, f), BF16),) * 3, grid=(s // tm, f // tn),
               in_specs=[hs, ws, ws], out_specs=(os_, os_, os_),
               compiler_params=_params("parallel", "parallel"), name=name)(h, wg_t, wu_t)


def _ffn_dact(df, wd, a, u, name, exchange=None):
    s, d = df.shape
    f = wd.shape[0]
    tm = _tile(s, 1024, 16)
    tn = _tile(f, 256, LANES)

    def body(df_ref, wd_ref, a_ref, u_ref, da_ref, du_ref):
        dhid = lax.dot_general(df_ref[...], wd_ref[...], _NT, preferred_element_type=F32)
        av = a_ref[...].astype(F32)
        uv = u_ref[...].astype(F32)
        sig = jax.nn.sigmoid(av)
        da_ref[...] = (dhid * uv * (sig * (1.0 + av * (1.0 - sig)))).astype(BF16)
        du_ref[...] = (dhid * (av * sig)).astype(BF16)

    ds_ = pl.BlockSpec((tm, d), lambda i, j: (i, 0))
    ws = pl.BlockSpec((tn, d), lambda i, j: (j, 0))
    os_ = pl.BlockSpec((tm, tn), lambda i, j: (i, j))
    return _pc(body, exchange, out_shape=(_sds((s, f), BF16),) * 2, grid=(s // tm, f // tn),
               in_specs=[ds_, ws, os_, os_], out_specs=(os_, os_),
               compiler_params=_params("parallel", "parallel"), name=name)(df, wd, a, u)


def _split3(v):
    hi = v.astype(BF16)
    r1 = v - hi.astype(F32)
    mid = r1.astype(BF16)
    lo = (r1 - mid.astype(F32)).astype(BF16)
    return hi, mid, lo


def _dot3(v, mat):
    hi, mid, lo = _split3(v)
    out = lax.dot_general(hi, mat, _NN, preferred_element_type=F32)
    out += lax.dot_general(mid, mat, _NN, preferred_element_type=F32)
    out += lax.dot_general(lo, mat, _NN, preferred_element_type=F32)
    return out


def _forget_fwd(flog_t, bias, name):
    h, s = flog_t.shape
    blk = _tile(s, 512, LANES)
    tri = (jnp.arange(blk)[:, None] <= jnp.arange(blk)[None, :]).astype(BF16)

    def body(z_ref, b_ref, tri_ref, f_ref, carry):
        @pl.when(pl.program_id(0) == 0)
        def _():
            carry[...] = jnp.zeros_like(carry)

        z = z_ref[...] + b_ref[...]
        e = jnp.exp(-jnp.abs(z))
        w = 1.0 + e
        log1p_e = jnp.where(w == 1.0, e, jnp.log(w) * (e / (w - 1.0)))
        lf = jnp.minimum(z, 0.0) - log1p_e
        out = carry[...] + _dot3(lf, tri_ref[...])
        for j, piece in enumerate(_split3(out)):
            f_ref[j] = piece
        carry[...] = out[:, blk - 1:blk]

    zs = pl.BlockSpec((h, blk), lambda i: (0, i))
    return _pc(body, out_shape=_sds((3, h, s), BF16), grid=(s // blk,),
               in_specs=[zs, pl.BlockSpec((h, 1), lambda i: (0, 0)), pl.BlockSpec((blk, blk), lambda i: (0, 0))],
               out_specs=pl.BlockSpec((3, h, blk), lambda i: (0, 0, i)), scratch_shapes=[pltpu.VMEM((h, 1), F32)],
               compiler_params=_params("arbitrary"), name=name)(flog_t, bias, tri)


def _forget_bwd(df_t, flog_t, bias, name):
    h, s = flog_t.shape
    blk = _tile(s, 512, LANES)
    nb = s // blk
    tri = (jnp.arange(blk)[:, None] >= jnp.arange(blk)[None, :]).astype(BF16)

    def body(df_ref, z_ref, b_ref, tri_ref, dz_ref, db_ref, carry):
        @pl.when(pl.program_id(0) == 0)
        def _():
            carry[...] = jnp.zeros_like(carry)
            db_ref[...] = jnp.zeros_like(db_ref)

        rc = carry[...] + _dot3(df_ref[...], tri_ref[...])
        carry[...] = rc[:, 0:1]
        dz = rc * jax.nn.sigmoid(-(z_ref[...] + b_ref[...]))
        dz_ref[...] = dz
        db_ref[...] += jnp.sum(dz, axis=-1, keepdims=True)

    rev = pl.BlockSpec((h, blk), lambda i: (0, nb - 1 - i))
    col = pl.BlockSpec((h, 1), lambda i: (0, 0))
    return _pc(body, out_shape=(_sds((h, s), F32), _sds((h, 1), F32)), grid=(nb,),
               in_specs=[rev, rev, col, pl.BlockSpec((blk, blk), lambda i: (0, 0))],
               out_specs=(rev, col), scratch_shapes=[pltpu.VMEM((h, 1), F32)],
               compiler_params=_params("arbitrary"), name=name)(df_t, flog_t, bias, tri)


def _attn_tiles(s):
    return _tile(s, 1024, LANES)


def _attn_half(t):
    return t // 2 if t >= 4 * LANES else t


BIAS_ROWS = 16


def _attn_prep(qkv, f_pieces, name):
    s = qkv.shape[0]
    a_w = qkv.shape[1] // 3
    npair = a_w // LANES
    t = _attn_tiles(s)
    scale = 1.0 / math.sqrt(HEAD_DIM)

    six = f_pieces[:, :2 * npair].reshape(3, npair, 2, s).transpose(1, 3, 2, 0).reshape(npair, s, 6)
    feat = jnp.concatenate([six, jnp.ones((npair, s, 1), BF16), jnp.zeros((npair, s, BIAS_ROWS - 7), BF16)], axis=-1)
    place_q = [[0.0] * (2 * LANES) for _ in range(BIAS_ROWS)]
    place_k = [[0.0] * (2 * LANES) for _ in range(BIAS_ROWS)]
    for hh in range(2):
        b0 = hh * LANES + (HEAD_DIM if hh == 0 else 0)
        for j in range(3):
            place_q[3 * hh + j][b0 + j] = 1.0
            place_q[6][b0 + 3 + j] = 1.0
            place_k[6][b0 + j] = 1.0
            place_k[3 * hh + j][b0 + 3 + j] = -1.0
    place_q = jnp.array(place_q, BF16)
    place_k = jnp.array(place_k, BF16)

    def body(q_ref, k_ref, v_ref, f_ref, pq_ref, pk_ref, qa_ref, ka_ref, va_ref):
        lane = lax.broadcasted_iota(jnp.int32, (1, LANES), 1)
        q2 = (q_ref[...].astype(F32) * scale).astype(BF16)
        k2, v2 = k_ref[...], v_ref[...]
        qx = lax.dot_general(f_ref[0], pq_ref[...], _NN, preferred_element_type=F32).astype(BF16)
        kx = lax.dot_general(f_ref[0], pk_ref[...], _NN, preferred_element_type=F32).astype(BF16)
        for hh in range(2):
            real = (lane < HEAD_DIM) if hh == 0 else (lane >= HEAD_DIM)
            cols = slice(hh * LANES, (hh + 1) * LANES)
            qa_ref[:, cols] = jnp.where(real, q2, qx[:, cols])
            ka_ref[:, cols] = jnp.where(real, k2, kx[:, cols])
            va_ref[:, cols] = jnp.where(real, v2, jnp.zeros_like(v2))

    def col(off):
        return pl.BlockSpec((t, LANES), lambda p, i: (i, off + p))

    out = pl.BlockSpec((t, 2 * LANES), lambda p, i: (i, p))
    place = pl.BlockSpec((BIAS_ROWS, 2 * LANES), lambda p, i: (0, 0))
    return _pc(body, out_shape=(_sds((s, 2 * a_w), BF16),) * 3, grid=(npair, s // t),
               in_specs=[col(0), col(npair), col(2 * npair), pl.BlockSpec((1, t, BIAS_ROWS), lambda p, i: (p, i, 0)),
                         place, place],
               out_specs=(out, out, out), compiler_params=_params("parallel", "parallel"), name=name)(
                   qkv, qkv, qkv, feat, place_q, place_k)


def _attn_fwd(qa, ka, va, name, exchange=None):
    s = qa.shape[0]
    a_w = qa.shape[1] // 2
    npair = a_w // LANES
    t = _attn_tiles(s)
    nq = s // t
    half = _attn_half(t)

    def body(q_ref, k_ref, v_ref, o_ref, lse_ref, m_sc, l_sc, acc_sc):
        qi = pl.program_id(1)
        first = lax.broadcasted_iota(jnp.int32, (1, LANES), 1) < HEAD_DIM
        m_sc[...] = jnp.full_like(m_sc, NEG_BIG)
        l_sc[...] = jnp.zeros_like(l_sc)
        acc_sc[...] = jnp.zeros_like(acc_sc)

        def step(q0, k_start, size, diag):
            q_sl = slice(q0, q0 + size)
            k_rows = pl.ds(pl.multiple_of(k_start, size), size)
            m_old = m_sc[q_sl, :]
            keep = None
            if diag:
                keep = (lax.broadcasted_iota(jnp.int32, (size, size), 0) >= lax.broadcasted_iota(jnp.int32, (size, size), 1))
            m_new, rs, pv = [], [], []
            for hh in range(2):
                cols = slice(hh * LANES, (hh + 1) * LANES)
                sc = lax.dot_general(q_ref[q_sl, cols], k_ref[k_rows, cols], _NT, preferred_element_type=F32)
                if diag:
                    sc = jnp.where(keep, sc, NEG_BIG)
                mo = m_old[:, hh * HEAD_DIM:hh * HEAD_DIM + 1]
                mn = jnp.maximum(mo, jnp.max(sc, axis=1, keepdims=True))
                p = jnp.exp(sc - mn)
                m_new.append(mn)
                rs.append(jnp.sum(p, axis=1, keepdims=True))
                pv.append(lax.dot_general(p.astype(BF16), v_ref[k_rows, cols], _NN, preferred_element_type=F32))
            m2 = jnp.where(first, m_new[0], m_new[1])
            alpha = jnp.exp(m_old - m2)
            m_sc[q_sl, :] = m2
            l_sc[q_sl, :] = alpha * l_sc[q_sl, :] + jnp.where(first, rs[0], rs[1])
            acc_sc[q_sl, :] = alpha * acc_sc[q_sl, :] + pv[0] + pv[1]

        def below_diagonal(ki, carry):
            step(0, ki * t, t, False)
            return carry

        lax.fori_loop(0, qi, below_diagonal, 0)
        step(0, qi * t, half, True)
        if half < t:
            step(half, qi * t, half, False)
            step(half, qi * t + half, half, True)
        l2 = l_sc[...]
        o_ref[...] = acc_sc[...] / l2
        lse_ref[...] = m_sc[...] + jnp.log(l2)

    qs = pl.BlockSpec((t, 2 * LANES), lambda p, qi: (qi, p))
    ks = pl.BlockSpec((s, 2 * LANES), lambda p, qi: (0, p))
    os_ = pl.BlockSpec((t, LANES), lambda p, qi: (qi, p))
    return _pc(body, exchange, out_shape=(_sds((s, a_w), F32), _sds((s, a_w), F32)), grid=(npair, nq),
               in_specs=[qs, ks, ks], out_specs=(os_, os_),
               scratch_shapes=[pltpu.VMEM((t, LANES), F32)] * 3,
               compiler_params=_params("parallel", "arbitrary"), name=name)(qa, ka, va)


def _attn_bwd(qa, ka, va, do, o, lse, name, exchange=None):
    s = qa.shape[0]
    a_w = qa.shape[1] // 2
    npair = a_w // LANES
    t = _attn_tiles(s)
    nq = s // t
    half = _attn_half(t)
    scale = 1.0 / math.sqrt(HEAD_DIM)

    def body(q_ref, k_ref, v_ref, do_ref, o_ref, lse_ref, dq_ref, dk_ref, dv_ref, qx_ref, kx_ref, dk_sc, dv_sc, kx_sc):
        ki = pl.program_id(1)
        first = lax.broadcasted_iota(jnp.int32, (1, LANES), 1) < HEAD_DIM

        @pl.when(ki == 0)
        def _():
            dq_ref[...] = jnp.zeros_like(dq_ref)
            qx_ref[...] = jnp.zeros_like(qx_ref)

        def step(q_start, k0, size, diag, assign):
            rows = pl.ds(pl.multiple_of(q_start, size), size)
            k_sl = slice(k0, k0 + size)
            do2 = do_ref[rows, :]
            lse2 = lse_ref[rows, :]
            dd = do2.astype(F32) * o_ref[rows, :]
            keep = None
            if diag:
                keep = (lax.broadcasted_iota(jnp.int32, (size, size), 0) >= lax.broadcasted_iota(jnp.int32, (size, size), 1))
            dq_h, dk_h, dv_h = [], [], []
            for hh in range(2):
                sel = first if hh == 0 else jnp.logical_not(first)
                cols = slice(hh * LANES, (hh + 1) * LANES)
                qh, kh, vh = q_ref[rows, cols], k_ref[k_sl, cols], v_ref[k_sl, cols]
                delta = jnp.sum(jnp.where(sel, dd, 0.0), axis=1, keepdims=True)
                sc = lax.dot_general(qh, kh, _NT, preferred_element_type=F32)
                if diag:
                    sc = jnp.where(keep, sc, NEG_BIG)
                p = jnp.exp(sc - lse2[:, hh * HEAD_DIM:hh * HEAD_DIM + 1])
                dp = lax.dot_general(do2, vh, _NT, preferred_element_type=F32)
                ds_b = (p * (dp - delta)).astype(BF16)
                dv_h.append(lax.dot_general(p.astype(BF16), do2, _TN, preferred_element_type=F32))
                dk_h.append(lax.dot_general(ds_b, qh, _TN, preferred_element_type=F32))
                dq_h.append(lax.dot_general(ds_b, kh, _NN, preferred_element_type=F32))
            dq_ref[rows, :] += jnp.where(first, dq_h[0], dq_h[1]) * scale
            qx_ref[rows, :] += jnp.where(first, dq_h[1], dq_h[0])
            dk_new = jnp.where(first, dk_h[0], dk_h[1])
            kx_new = jnp.where(first, dk_h[1], dk_h[0])
            dv_new = jnp.where(first, dv_h[0], dv_h[1])
            if assign:
                dk_sc[k_sl, :] = dk_new
                kx_sc[k_sl, :] = kx_new
                dv_sc[k_sl, :] = dv_new
            else:
                dk_sc[k_sl, :] += dk_new
                kx_sc[k_sl, :] += kx_new
                dv_sc[k_sl, :] += dv_new

        def below_diagonal(qi, carry):
            step(qi * t, 0, t, False, False)
            return carry

        step(ki * t, 0, half, True, True)
        if half < t:
            step(ki * t + half, 0, half, False, False)
            step(ki * t + half, half, half, True, True)
        lax.fori_loop(ki + 1, nq, below_diagonal, 0)
        dk_ref[...] = dk_sc[...].astype(BF16)
        dv_ref[...] = dv_sc[...].astype(BF16)
        kx_ref[...] = kx_sc[...]

    ks2 = pl.BlockSpec((t, 2 * LANES), lambda p, ki: (ki, p))
    qs2 = pl.BlockSpec((s, 2 * LANES), lambda p, ki: (0, p))
    whole = pl.BlockSpec((s, LANES), lambda p, ki: (0, p))
    kout = pl.BlockSpec((t, LANES), lambda p, ki: (ki, p))
    return _pc(body, exchange,
               out_shape=(_sds((s, a_w), F32), _sds((s, a_w), BF16), _sds((s, a_w), BF16), _sds((s, a_w), F32),
                          _sds((s, a_w), F32)),
               grid=(npair, nq), in_specs=[qs2, ks2, ks2, whole, whole, whole],
               out_specs=(whole, kout, kout, whole, kout),
               scratch_shapes=[pltpu.VMEM((t, LANES), F32)] * 3,
               compiler_params=_params("parallel", "arbitrary"), name=name)(qa, ka, va, do, o, lse)

def _decay_grads(qx, kx, name):
    s, a_w = qx.shape
    n_heads = a_w // HEAD_DIM
    tr = _tile(s, 512, 8)
    pick_q = [[0.0] * LANES for _ in range(a_w)]
    pick_k = [[0.0] * LANES for _ in range(a_w)]
    for h in range(n_heads):
        b0 = (h // 2) * LANES + (HEAD_DIM if h % 2 == 0 else 0)
        pick_q[b0][h] = 1.0
        pick_k[b0 + 3][h] = 1.0
    pick_q = jnp.array(pick_q, BF16)
    pick_k = jnp.array(pick_k, BF16)

    def body(qx_ref, kx_ref, pq_ref, pk_ref, o_ref):
        o_ref[...] = _dot3(qx_ref[...], pq_ref[...]) - _dot3(kx_ref[...], pk_ref[...])

    row = pl.BlockSpec((tr, a_w), lambda i: (i, 0))
    pick = pl.BlockSpec((a_w, LANES), lambda i: (0, 0))
    return _pc(body, out_shape=_sds((s, LANES), F32), grid=(s // tr,), in_specs=[row, row, pick, pick],
               out_specs=pl.BlockSpec((tr, LANES), lambda i: (i, 0)),
               compiler_params=_params("parallel"), name=name)(qx, kx, pick_q, pick_k)


def _shift_down(z, k, rows):
    return jnp.where(rows >= k, pltpu.roll(z, k, 0), 0.0)


def _shift_up(z, k, rows, n):
    return jnp.where(rows < n - k, pltpu.roll(z, n - k, 0), 0.0)


def _conv_fwd(bcx, conv_w, name):
    s = bcx.shape[0]
    cw = bcx.shape[1] // 3
    nb = cw // LANES

    def body(b_ref, c_ref, x_ref, w_ref, cv_ref):
        rows = lax.broadcasted_iota(jnp.int32, (s, LANES), 0)
        z = c_ref[...] * x_ref[...]
        w = w_ref[...]
        y = w[2:3, :] * z + w[1:2, :] * _shift_down(z, 1, rows) + w[0:1, :] * _shift_down(z, 2, rows)
        cv_ref[...] = b_ref[...] * y

    def col(off):
        return pl.BlockSpec((s, LANES), lambda j: (0, j + off))

    return _pc(body, out_shape=_sds((s, cw), F32), grid=(nb,),
               in_specs=[col(0), col(nb), col(2 * nb), pl.BlockSpec((CONV_K, LANES), lambda j: (0, j))],
               out_specs=col(0), compiler_params=_params("parallel"), name=name)(bcx, bcx, bcx, conv_w)


def _conv_bwd(dcv, bcx, conv_w, name):
    s = bcx.shape[0]
    cw = bcx.shape[1] // 3
    nb = cw // LANES

    def body(dcv_ref, b_ref, c_ref, x_ref, w_ref, db_ref, dc_ref, dxc_ref, dw_ref):
        rows = lax.broadcasted_iota(jnp.int32, (s, LANES), 0)
        cv_, xv = c_ref[...], x_ref[...]
        z = cv_ * xv
        w = w_ref[...]
        z1 = _shift_down(z, 1, rows)
        z2 = _shift_down(z, 2, rows)
        y = w[2:3, :] * z + w[1:2, :] * z1 + w[0:1, :] * z2
        dcvv = dcv_ref[...]
        db_ref[...] = (dcvv * y).astype(BF16)
        dy = dcvv * b_ref[...]
        dw_ref[0:1, :] = jnp.sum(dy * z2, axis=0, keepdims=True)
        dw_ref[1:2, :] = jnp.sum(dy * z1, axis=0, keepdims=True)
        dw_ref[2:3, :] = jnp.sum(dy * z, axis=0, keepdims=True)
        dz = w[2:3, :] * dy + w[1:2, :] * _shift_up(dy, 1, rows, s) + w[0:1, :] * _shift_up(dy, 2, rows, s)
        dc_ref[...] = (dz * xv).astype(BF16)
        dxc_ref[...] = (dz * cv_).astype(BF16)

    def col(off):
        return pl.BlockSpec((s, LANES), lambda j: (0, j + off))

    wspec = pl.BlockSpec((CONV_K, LANES), lambda j: (0, j))
    db, dc, dxc, dw = _pc(body, out_shape=(_sds((s, cw), BF16),) * 3 + (_sds((CONV_K, cw), F32),), grid=(nb,),
                          in_specs=[col(0), col(0), col(nb), col(2 * nb), wspec],
                          out_specs=(col(0), col(0), col(0), wspec),
                          compiler_params=_params("parallel"), name=name)(dcv, bcx, bcx, bcx, conv_w)
    return db, dc, dxc, dw


def _group_matrix():
    idx = jnp.arange(LANES) // HEAD_DIM
    return (idx[:, None] == idx[None, :]).astype(BF16)


def _group_sum(v, gmat):
    return _dot3(v, gmat)


def _gnorm_fwd(att, cv, gg, name):
    s, a_w = att.shape
    cw = cv.shape[1]
    d = a_w + cw
    tr = _tile(s, 512, 16)
    gmat = _group_matrix()

    def body(att_ref, cv_ref, gg_ref, gm_ref, yn_ref):
        gm = gm_ref[...]
        for c0 in range(0, d, LANES):
            y = att_ref[:, c0:c0 + LANES] if c0 < a_w else cv_ref[:, c0 - a_w:c0 - a_w + LANES]
            ms = _group_sum(y * y, gm) * (1.0 / HEAD_DIM)
            yn_ref[:, c0:c0 + LANES] = (y * lax.rsqrt(ms + EPS) * gg_ref[:, c0:c0 + LANES]).astype(BF16)

    return _pc(body, out_shape=_sds((s, d), BF16), grid=(s // tr,),
               in_specs=[pl.BlockSpec((tr, a_w), lambda i: (i, 0)), pl.BlockSpec((tr, cw), lambda i: (i, 0)),
                         _vec_spec(d), pl.BlockSpec((LANES, LANES), lambda i: (0, 0))],
               out_specs=pl.BlockSpec((tr, d), lambda i: (i, 0)),
               compiler_params=_params("parallel"), name=name)(att, cv, gg, gmat)


def _gnorm_bwd(dyn, att, cv, gg, name):
    s, a_w = att.shape
    cw = cv.shape[1]
    d = a_w + cw
    tr = _tile(s, 256, 16)
    gmat = _group_matrix()

    def body(dyn_ref, att_ref, cv_ref, gg_ref, gm_ref, datt_ref, dcv_ref, dgg_ref):
        @pl.when(pl.program_id(0) == 0)
        def _():
            dgg_ref[...] = jnp.zeros_like(dgg_ref)

        gm = gm_ref[...]
        for c0 in range(0, d, LANES):
            y = att_ref[:, c0:c0 + LANES] if c0 < a_w else cv_ref[:, c0 - a_w:c0 - a_w + LANES]
            dv = dyn_ref[:, c0:c0 + LANES]
            r = lax.rsqrt(_group_sum(y * y, gm) * (1.0 / HEAD_DIM) + EPS)
            xhat = y * r
            dgg_ref[:, c0:c0 + LANES] += jnp.sum(dv * xhat, axis=0, keepdims=True)
            dxh = dv * gg_ref[:, c0:c0 + LANES]
            proj = _group_sum(dxh * xhat, gm) * (1.0 / HEAD_DIM)
            dy = r * (dxh - xhat * proj)
            if c0 < a_w:
                datt_ref[:, c0:c0 + LANES] = dy.astype(BF16)
            else:
                dcv_ref[:, c0 - a_w:c0 - a_w + LANES] = dy

    return _pc(body, out_shape=(_sds((s, a_w), BF16), _sds((s, cw), F32), _sds((1, d), F32)), grid=(s // tr,),
               in_specs=[pl.BlockSpec((tr, d), lambda i: (i, 0)), pl.BlockSpec((tr, a_w), lambda i: (i, 0)),
                         pl.BlockSpec((tr, cw), lambda i: (i, 0)), _vec_spec(d),
                         pl.BlockSpec((LANES, LANES), lambda i: (0, 0))],
               out_specs=(pl.BlockSpec((tr, a_w), lambda i: (i, 0)), pl.BlockSpec((tr, cw), lambda i: (i, 0)),
                          _vec_spec(d)),
               compiler_params=_params("arbitrary"), name=name)(dyn, att, cv, gg, gmat)


def _adamw_math(w, g, m, v):
    m_new = ADAM_B1 * m + (1.0 - ADAM_B1) * g
    v_new = ADAM_B2 * v + (1.0 - ADAM_B2) * (g * g)
    m_hat = m_new / (1.0 - ADAM_B1 ** ADAM_STEP)
    v_hat = v_new / (1.0 - ADAM_B2 ** ADAM_STEP)
    delta = -ADAM_LR * (m_hat / (jnp.sqrt(v_hat) + ADAM_EPS) + ADAM_WD * w)
    return delta, m_new, v_new


def _row_tile(r, c):
    return _tile(r, max(8, ((1 << 19) // c) // 8 * 8), 8)


def _adamw(w, g, m, v, name):
    r, c = w.shape
    tr = _row_tile(r, c)

    def body(w_ref, g_ref, m_ref, v_ref, d_ref, mo_ref, vo_ref):
        d, mn, vn = _adamw_math(w_ref[...], g_ref[...], m_ref[...], v_ref[...])
        d_ref[...] = d
        mo_ref[...] = mn
        vo_ref[...] = vn

    spec = pl.BlockSpec((tr, c), lambda i: (i, 0))
    return _pc(body, out_shape=(_sds((r, c), F32),) * 3, grid=(r // tr,), in_specs=[spec] * 4,
               out_specs=(spec,) * 3, compiler_params=_params("parallel"), name=name)(w, g, m, v)


def _adamw_halves(w, mine, theirs, m, v, core, name):
    r2, c = w.shape
    r = r2 // 2
    assert mine.shape == (r, c) and theirs.shape == (r, c)
    tr = _row_tile(r, c)
    nb = r // tr

    def body(core_ref, w_ref, a_ref, b_ref, m_ref, v_ref, g_ref, d_ref, mo_ref, vo_ref):
        g = jnp.where(pl.program_id(0) == core_ref[0], a_ref[...], b_ref[...])
        d, mn, vn = _adamw_math(w_ref[...], g, m_ref[...], v_ref[...])
        g_ref[...] = g
        d_ref[...] = d
        mo_ref[...] = mn
        vo_ref[...] = vn

    full = pl.BlockSpec((tr, c), lambda h, i, core_ref: (h * nb + i, 0))
    half = pl.BlockSpec((tr, c), lambda h, i, core_ref: (i, 0))
    grid_spec = pltpu.PrefetchScalarGridSpec(
        num_scalar_prefetch=1, grid=(2, nb), in_specs=[full, half, half, full, full], out_specs=(full,) * 4)
    return _pc(body, out_shape=(_sds((r2, c), F32),) * 4, grid_spec=grid_spec,
               compiler_params=_params("parallel", "parallel"), name=name)(core, w, mine, theirs, m, v)


def _ada_fwd(c16, ada_w, ada_b, name):
    d, n = ada_w.shape
    tn = _tile(n, 768, LANES)

    def body(c_ref, w_ref, b_ref, o_ref):
        cv = c_ref[...]
        sc = (cv * jax.nn.sigmoid(cv)).astype(BF16)
        o_ref[...] = lax.dot_general(sc, w_ref[...].astype(BF16), _NN, preferred_element_type=F32) + b_ref[...]

    return _pc(body, out_shape=_sds((16, n), F32), grid=(n // tn,),
               in_specs=[pl.BlockSpec((16, d), lambda j: (0, 0)), pl.BlockSpec((d, tn), lambda j: (0, j)),
                         pl.BlockSpec((1, tn), lambda j: (0, j))],
               out_specs=pl.BlockSpec((16, tn), lambda j: (0, j)),
               compiler_params=_params("parallel"), name=name)(c16, ada_w, ada_b)


def _ada_update(c16_t, dmod16, w, m, v, name, exchange=None):
    r, c = w.shape
    tr = _row_tile(r, c)

    def body(c_ref, dm_ref, w_ref, m_ref, v_ref, g_ref, d_ref, mo_ref, vo_ref):
        cv = c_ref[...]
        sc = (cv * jax.nn.sigmoid(cv)).astype(BF16)
        g = lax.dot_general(sc, dm_ref[...].astype(BF16), _NN, preferred_element_type=F32)
        d, mn, vn = _adamw_math(w_ref[...], g, m_ref[...], v_ref[...])
        g_ref[...] = g
        d_ref[...] = d
        mo_ref[...] = mn
        vo_ref[...] = vn

    spec = pl.BlockSpec((tr, c), lambda i: (i, 0))
    return _pc(body, exchange, out_shape=(_sds((r, c), F32),) * 4, grid=(r // tr,),
               in_specs=[pl.BlockSpec((tr, 16), lambda i: (i, 0)), pl.BlockSpec((16, c), lambda i: (0, 0)),
                         spec, spec, spec],
               out_specs=(spec,) * 4, compiler_params=_params("parallel"), name=name)(c16_t, dmod16, w, m, v)


def _add_half(dw, recv, core, name):
    _, _, r, w = dw.shape
    tr = _tile(r, 512, 16)

    def body(core_ref, a_ref, b_ref, o_ref):
        o_ref[...] = (a_ref[...].astype(F32) + b_ref[...].astype(F32)).astype(BF16)

    grid_spec = pltpu.PrefetchScalarGridSpec(
        num_scalar_prefetch=1, grid=(N_CHIPS, r // tr),
        in_specs=[pl.BlockSpec((None, None, tr, w), lambda s, i, core_ref: (s, core_ref[0], i, 0)),
                  pl.BlockSpec((None, tr, w), lambda s, i, core_ref: (s, i, 0))],
        out_specs=pl.BlockSpec((None, tr, w), lambda s, i, core_ref: (s, i, 0)))
    return _pc(body, out_shape=_sds((N_CHIPS, r, w), BF16), grid_spec=grid_spec,
               compiler_params=_params("parallel", "parallel"), name=name)(core, dw, recv)


def _sum_chips(own, recv, chip, name):
    _, r, w = own.shape
    tr = _tile(r, 512, 16)

    def body(chip_ref, own_ref, p_ref, o_ref):
        acc = own_ref[...].astype(F32)
        for q in range(N_CHIPS - 1):
            acc = acc + p_ref[q].astype(F32)
        o_ref[...] = acc

    grid_spec = pltpu.PrefetchScalarGridSpec(
        num_scalar_prefetch=1, grid=(r // tr,),
        in_specs=[pl.BlockSpec((None, tr, w), lambda i, chip_ref: (chip_ref[0], i, 0)),
                  pl.BlockSpec((N_CHIPS - 1, tr, w), lambda i, chip_ref: (0, i, 0))],
        out_specs=pl.BlockSpec((tr, w), lambda i, chip_ref: (i, 0)))
    return _pc(body, out_shape=_sds((r, w), F32), grid_spec=grid_spec,
               compiler_params=_params("parallel"), name=name)(chip, own, recv)


def _sum_devices(parts, name):
    nd, r, w = parts.shape

    def body(p_ref, o_ref):
        acc = p_ref[0]
        for q in range(1, nd):
            acc = acc + p_ref[q]
        o_ref[...] = acc

    return _pc(body, out_shape=_sds((r, w), F32), name=name)(parts)


def _place():
    x, y, c = lax.axis_index("x"), lax.axis_index("y"), lax.axis_index("c")
    chips = [(1 - x, y), (x, 1 - y), (1 - x, 1 - y)]
    return x, y, c, chips


def _small_gather_exchange(blk):
    r, w = blk.shape

    def copies(src, dst, send_sems, recv_sems):
        x, y, c, chips = _place()
        me, sibling = (x, y, c), (x, y, 1 - c)

        def rows(px, py, pc):
            return dst[0].at[pl.ds((4 * px + 2 * py + pc) * r, r), :]

        def copy(k, block, to, own=False):
            return _remote(src[0] if own else rows(*block), rows(*block), send_sems, recv_sems, k, to)

        mine = pltpu.make_async_copy(src[0], rows(*me), send_sems.at[7])
        first = [copy(0, me, sibling, own=True)] + [copy(1 + j, me, (*chip, c), own=True) for j, chip in enumerate(chips)]
        passed = [copy(4 + j, (*chip, c), sibling) for j, chip in enumerate(chips)]
        landed = [copy(1 + j, (*chip, c), me) for j, chip in enumerate(chips)]
        from_sibling = [copy(0, sibling, me)] + [copy(4 + j, (*chip, 1 - c), me) for j, chip in enumerate(chips)]
        return mine, first, passed, landed, from_sibling

    def start(src, dst, send_sems, recv_sems):
        mine, first, _, _, _ = copies(src, dst, send_sems, recv_sems)
        mine.start()
        for cp in first:
            cp.start()

    def finish(src, dst, send_sems, recv_sems):
        mine, first, passed, landed, from_sibling = copies(src, dst, send_sems, recv_sems)
        for arrival, onward in zip(landed, passed):
            arrival.wait_recv()
            onward.start()
        for cp in from_sibling:
            cp.wait_recv()
        for cp in first + passed:
            cp.wait_send()
        mine.wait()

    return _Exchange([blk], [_sds((N_DEV * r, w), blk.dtype)], 8, start, finish)


def _remote(src, dst, send_sems, recv_sems, k, to):
    return pltpu.make_async_remote_copy(src_ref=src, dst_ref=dst, send_sem=send_sems.at[k], recv_sem=recv_sems.at[k],
                                        device_id=to, device_id_type=MESH)


def _exchange_of(inputs, out_shapes, n_sems, copies, aliases=None):
    def start(src, dst, send_sems, recv_sems):
        for cp in copies(src, dst, send_sems, recv_sems)[0]:
            cp.start()

    def finish(src, dst, send_sems, recv_sems):
        sends, arrivals = copies(src, dst, send_sems, recv_sems)
        for cp in arrivals:
            cp.wait_recv()
        for cp in sends:
            cp.wait_send()

    return _Exchange(inputs, out_shapes, n_sems, start, finish, aliases)


def _run_exchange(ex, name):
    n_in, n_out = len(ex.inputs), len(ex.out_shapes)

    def body(*refs):
        src, dst = refs[:n_in], refs[n_in:n_in + n_out]
        send_sems, recv_sems = refs[n_in + n_out:]
        ex.start(src, dst, send_sems, recv_sems)
        ex.finish(src, dst, send_sems, recv_sems)

    ex.set_results(pl.pallas_call(
        body, out_shape=tuple(ex.out_shapes), in_specs=[_ANY] * n_in, out_specs=(_ANY,) * n_out,
        scratch_shapes=[pltpu.SemaphoreType.DMA((ex.n_sems,)), pltpu.SemaphoreType.DMA((ex.n_sems,))],
        input_output_aliases=ex.aliases, name=name)(*ex.inputs))


def _gather_ici_exchange(shards):
    n = len(shards)

    def copies(own, out, send_sems, recv_sems):
        x, y, c, chips = _place()
        my_chip = 2 * x + y
        sends, arrivals = [], []
        for i in range(n):
            for j, chip in enumerate(chips):
                to = (*chip, c)
                sends.append(_remote(own[i].at[c], out[i].at[my_chip, c], send_sems, recv_sems, 4 * i + j, to))
                arrivals.append(_remote(own[i].at[c], out[i].at[2 * chip[0] + chip[1], c], send_sems, recv_sems, 4 * i + j, to))
            whole = _remote(own[i], out[i].at[my_chip], send_sems, recv_sems, 4 * i + 3, (x, y, 1 - c))
            sends.append(whole)
            arrivals.append(whole)
        return sends, arrivals

    return _exchange_of(shards, [_sds((N_CHIPS,) + s.shape, s.dtype) for s in shards], 4 * n, copies)


def _gather_pass_exchange(gathered):
    n = len(gathered)

    def copies(src, dst, send_sems, recv_sems):
        x, y, c, chips = _place()
        sends, arrivals = [], []
        for i in range(n):
            for j, chip in enumerate(chips):
                idx = 2 * chip[0] + chip[1]
                sends.append(_remote(src[i].at[idx, c], dst[i].at[idx, c], send_sems, recv_sems, 3 * i + j, (x, y, 1 - c)))
                arrivals.append(_remote(src[i].at[idx, c], dst[i].at[idx, 1 - c], send_sems, recv_sems, 3 * i + j, (x, y, 1 - c)))
        return sends, arrivals

    return _exchange_of(gathered, [_sds(g.shape, g.dtype) for g in gathered], 3 * n, copies,
                        aliases={i: i for i in range(n)})


def _reduce_sibling_exchange(grads):
    n = len(grads)

    def copies(src, dst, send_sems, recv_sems):
        x, y, c, _ = _place()
        both = [_remote(src[i].at[s, 1 - c], dst[i].at[s], send_sems, recv_sems, N_CHIPS * i + s, (x, y, 1 - c))
                for i in range(n) for s in range(N_CHIPS)]
        return both, both

    return _exchange_of(grads, [_sds((N_CHIPS,) + g.shape[2:], g.dtype) for g in grads], N_CHIPS * n, copies)


def _reduce_chips_exchange(parts):
    n = len(parts)

    def copies(src, dst, send_sems, recv_sems):
        x, y, c, chips = _place()
        both = [_remote(src[i].at[2 * chip[0] + chip[1]], dst[i].at[j], send_sems, recv_sems, 3 * i + j, (*chip, c))
                for i in range(n) for j, chip in enumerate(chips)]
        return both, both

    return _exchange_of(parts, [_sds((N_CHIPS - 1,) + p.shape[1:], p.dtype) for p in parts], 3 * n, copies)


def _share_exchange(halves):
    n = len(halves)

    def copies(src, dst, send_sems, recv_sems):
        x, y, c, _ = _place()
        both = [_remote(src[i], dst[i], send_sems, recv_sems, i, (x, y, 1 - c)) for i in range(n)]
        return both, both

    return _exchange_of(halves, [_sds(h.shape, h.dtype) for h in halves], n, copies)


HEAD_ROWS = 16


class _WeightTraffic:
    def __init__(self, shards, core, chip):
        self.shards, self.core, self.chip = shards, core, chip
        self.gather, self.grads, self.reduce, self.chip_sums, self.half_sums, self.shared = {}, {}, {}, {}, {}, {}

    def gather_ici(self, grp):
        self.gather[grp] = _gather_ici_exchange(self.shards[grp])
        return self.gather[grp]

    def gather_pass(self, grp):
        self.gather[grp] = _gather_pass_exchange(self.gather[grp].results)
        return self.gather[grp]

    def weights(self, grp):
        return [g.reshape(-1, g.shape[-1]) for g in self.gather[grp].results]

    def reduce_sibling(self, grp, grads):
        self.grads[grp] = [g.reshape(N_CHIPS, 2, g.shape[0] // (2 * N_CHIPS), g.shape[1]) for g in grads]
        self.reduce[grp] = _reduce_sibling_exchange(self.grads[grp])
        return self.reduce[grp]

    def add_halves(self, grp):
        self.chip_sums[grp] = [_add_half(g, r, self.core, "add_half_%s%d" % (grp, i))
                               for i, (g, r) in enumerate(zip(self.grads[grp], self.reduce[grp].results))]

    def reduce_chips(self, grp):
        self.reduce[grp] = _reduce_chips_exchange(self.chip_sums[grp])
        return self.reduce[grp]

    def sum_chips(self, grp):
        self.half_sums[grp] = [_sum_chips(o, p, self.chip, "sum_chips_%s%d" % (grp, i))
                               for i, (o, p) in enumerate(zip(self.chip_sums[grp], self.reduce[grp].results))]

    def share(self, grp):
        self.shared[grp] = _share_exchange(self.half_sums[grp])
        return self.shared[grp]

    def totals(self, grp):
        return list(zip(self.half_sums[grp], self.shared[grp].results))


def _ffn_fwd(x, norm_g, shift, scale, gate, wg_t, wu_t, wd, tag, up_exchange=None, down_exchange=None):
    h = _norm_mod_fwd(x, norm_g, shift, scale, tag + "_norm_fwd")
    a, u, hid = _ffn_up(h, wg_t, wu_t, tag + "_up", exchange=up_exchange)
    wd = wd() if callable(wd) else wd
    x_out, f = _mm(hid, wd, "nn", F32, tag + "_down", res=x, gate=gate, aux_dtype=BF16,
                   exchange=down_exchange() if down_exchange else None)
    return x_out, (h, a, u, hid, f)


def _ffn_bwd(dx_out, df, x, saved, norm_g, scale, wg_t, wu_t, wd, tag, traffic, below=None, dact_exchange=None,
             dw_exchange=None, finish_reduction=False):
    h, a, u, hid, _ = saved
    f_below, gate_below = below if below else (None, None)
    da, du = _ffn_dact(df, wd, a, u, tag + "_dact", exchange=dact_exchange)
    dwd = _mm(hid, df, "tn", BF16, tag + "_dwd", exchange=dw_exchange() if dw_exchange else None)
    if not finish_reduction:
        dwg_t = _mm(da, h, "tn", BF16, tag + "_dwg")
        dwu_t = _mm(du, h, "tn", BF16, tag + "_dwu")
        dx, dshift, dscale, dnorm_g, *gated = _norm_mod_bwd(
            ([da, du], [wg_t, wu_t]), x, norm_g, scale, dx_out, tag + "_dh_norm_bwd", f=f_below, gate=gate_below,
            exchange=traffic.reduce_sibling(tag, [dwg_t, dwu_t, dwd]))
        traffic.add_halves(tag)
        return dx, (dshift, dscale, dnorm_g), gated
    kd, kg, ku = tag + "_wd", tag + "_wg", tag + "_wu"
    dwg_t = _mm(da, h, "tn", BF16, tag + "_dwg", exchange=traffic.reduce_sibling(kd, [dwd]))
    traffic.add_halves(kd)
    dwu_t = _mm(du, h, "tn", BF16, tag + "_dwu",
                exchange=_join(traffic.reduce_chips(kd), traffic.reduce_sibling(kg, [dwg_t])))
    traffic.add_halves(kg)
    half = x.shape[0] // 2
    top = _norm_mod_bwd(([da, du], [wg_t, wu_t]), x, norm_g, scale, dx_out, tag + "_dh_norm_bwd_top", f=f_below,
                        gate=gate_below, rows=(0, half),
                        exchange=_join(traffic.reduce_chips(kg), traffic.reduce_sibling(ku, [dwu_t])))
    traffic.add_halves(ku)
    traffic.sum_chips(kd)
    traffic.sum_chips(kg)
    bottom = _norm_mod_bwd(([da, du], [wg_t, wu_t]), x, norm_g, scale, dx_out, tag + "_dh_norm_bwd_bottom", f=f_below,
                           gate=gate_below, rows=(half, half),
                           exchange=_join(traffic.reduce_chips(ku), traffic.share(kd), traffic.share(kg)))
    traffic.sum_chips(ku)
    dx, dshift, dscale, dnorm_g, *gated = [jnp.concatenate([a, b]) if a.shape[0] == half else a + b
                                           for a, b in zip(top, bottom)]
    return dx, (dshift, dscale, dnorm_g), gated


def _layer_step(x, target, mod, gains, forget_bias, conv_w, traffic, att_w, in_shard, in_rows):
    sh1, sc1, g1, sh2, sc2, g2, sh3, sc3, g3 = mod
    norm1_g, norm2_g, norm3_g, final_g, group_g = gains
    s, d = x.shape
    n_heads = att_w // HEAD_DIM
    npair = n_heads // 2
    gate1, gate3 = 0.5 * g1, 0.5 * g3

    def split_w_in(w_in_pad):
        w_in_t = w_in_pad.reshape(N_CHIPS, in_rows, d)[:, :in_shard].reshape(N_CHIPS * in_shard, d)
        return (w_in_t[:3 * att_w], _pad_rows(w_in_t[3 * att_w:3 * att_w + n_heads], LANES), w_in_t[3 * att_w + n_heads:])

    wg1_t, wu1_t = traffic.weights("ffn1_gu")

    def wd1_ready():
        _run_exchange(traffic.gather_pass("ffn1_d"), "gather_ffn1_down_pass")
        return traffic.weights("ffn1_d")[0]

    x1, saved1 = _ffn_fwd(x, norm1_g, sh1, sc1, gate1, wg1_t, wu1_t, wd1_ready, "ffn1",
                          up_exchange=_join(traffic.gather_ici("ffn1_d"), traffic.gather_ici("mix_in")),
                          down_exchange=lambda: _join(traffic.gather_pass("mix_in"), traffic.gather_ici("mix_out")))
    wd1 = traffic.weights("ffn1_d")[0]
    wqkv_t, wf_t, wbcx_t = split_w_in(traffic.weights("mix_in")[0])

    h2 = _norm_mod_fwd(x1, norm2_g, sh2, sc2, "mix_norm_fwd")
    qkv = _mm(h2, wqkv_t, "nt", BF16, "mix_proj_qkv", exchange=traffic.gather_pass("mix_out"))
    w_out = traffic.weights("mix_out")[0]
    bcx = _mm(h2, wbcx_t, "nt", F32, "mix_proj_bcx")
    flog = _mm(h2, wf_t, "nt", F32, "mix_proj_f")
    flog_t = jnp.pad(flog[:, :n_heads].T, ((0, HEAD_ROWS - n_heads), (0, 0)))
    bias_col = jnp.pad(forget_bias, (0, HEAD_ROWS - n_heads))[:, None]
    f_pieces = _forget_fwd(flog_t, bias_col, "forget_fwd")
    qa, ka, va = _attn_prep(qkv, f_pieces, "attn_prep")
    att, lse = _attn_fwd(qa, ka, va, "attn_fwd", exchange=traffic.gather_ici("ffn2"))
    cv = _conv_fwd(bcx, conv_w, "conv_fwd")
    yn = _gnorm_fwd(att, cv, group_g, "gnorm_fwd")
    x2, mix = _mm(yn, w_out, "nn", F32, "mix_out", res=x1, gate=g2, aux_dtype=BF16, exchange=traffic.gather_pass("ffn2"))
    wg2_t, wu2_t, wd2 = traffic.weights("ffn2")

    x3, saved3 = _ffn_fwd(x2, norm3_g, sh3, sc3, gate3, wg2_t, wu2_t, wd2, "ffn2")

    dx3, loss_row, dfinal_g, df2, dgate3 = _final_loss(x3, final_g, target, saved3[4], gate3, "final_loss")

    dx2, (dsh3, dsc3, dnorm3_g), (dmix, dg2) = _ffn_bwd(
        dx3, df2, x2, saved3, norm3_g, sc3, wg2_t, wu2_t, wd2, "ffn2", traffic, below=(mix, g2))
    dyn = _mm(dmix, w_out, "nt", F32, "mix_out_dyn")
    dw_out = _mm(yn, dmix, "tn", BF16, "mix_out_dw")
    datt, dcv, dgroup_g = _gnorm_bwd(dyn, att, cv, group_g, "gnorm_bwd")
    db, dc, dxc, dconv_w = _conv_bwd(dcv, bcx, conv_w, "conv_bwd")
    dbcx = jnp.concatenate([db, dc, dxc], axis=1)
    dq, dk, dv, qx, kx = _attn_bwd(qa, ka, va, datt, att, lse, "attn_bwd", exchange=traffic.reduce_chips("ffn2"))
    traffic.sum_chips("ffn2")
    dqkv = jnp.concatenate([dq.astype(BF16), dk, dv], axis=1)
    df_t = _decay_grads(qx, kx, "decay_grads")[:, :HEAD_ROWS].T
    dflog_t, dbias_col = _forget_bwd(df_t, flog_t, bias_col, "forget_bwd")
    dflog = jnp.pad(dflog_t[:n_heads].T, ((0, 0), (0, LANES - n_heads))).astype(BF16)
    dwqkv_t = _mm(dqkv, h2, "tn", BF16, "mix_dw_qkv", exchange=traffic.share("ffn2"))
    dwbcx_t = _mm(dbcx, h2, "tn", BF16, "mix_dw_bcx")
    dwf_t = _mm(dflog, h2, "tn", BF16, "mix_dw_f")
    dw_in_t = jnp.concatenate([dwqkv_t, dwf_t[:n_heads], dwbcx_t], axis=0).reshape(N_CHIPS, in_shard, d)
    dw_in_t = jnp.pad(dw_in_t, ((0, 0), (0, in_rows - in_shard), (0, 0))).reshape(N_CHIPS * in_rows, d)
    dx1, dsh2, dsc2, dnorm2_g, df1, dgate1 = _norm_mod_bwd(
        ([dqkv, dbcx, dflog], [wqkv_t, wbcx_t, wf_t]), x1, norm2_g, sc2, dx2, "mix_dh_norm_bwd", f=saved1[4], gate=gate1,
        exchange=traffic.reduce_sibling("mix", [dw_in_t, dw_out]))
    traffic.add_halves("mix")

    def share_mix():
        traffic.sum_chips("mix")
        return traffic.share("mix")

    dx, (dsh1, dsc1, dnorm1_g), _ = _ffn_bwd(
        dx1, df1, x, saved1, norm1_g, sc1, wg1_t, wu1_t, wd1, "ffn1", traffic,
        dact_exchange=traffic.reduce_chips("mix"), dw_exchange=share_mix, finish_reduction=True)

    dmod = [dsh1, dsc1, 0.5 * dgate1, dsh2, dsc2, dg2, dsh3, dsc3, 0.5 * dgate3]
    dgains = [dnorm1_g, dnorm2_g, dnorm3_g, dfinal_g, dgroup_g]
    dbias = dbias_col[:n_heads, 0]
    return dx, loss_row, dmod, dgains, dbias, dconv_w


SMALL_ROWS = 24
ROW_GAINS, ROW_LOSS, ROW_FORGET, ROW_CONV, ROW_MOD = 0, 5, 6, 7, 10
PROW_ADA_B, PROW_GAINS, PROW_FORGET, PROW_CONV = 0, 9, 14, 15


def _round_up(n, m):
    return -(-n // m) * m


def _pad_rows(a, rows):
    return jnp.pad(a, ((0, rows - a.shape[0]), (0, 0)))


def _halves(a):
    return a.reshape(2, a.shape[0] // 2, a.shape[1])


def _rows_at(a, r0, total, width):
    return jnp.pad(a, ((r0, total - r0 - a.shape[0]), (0, width - a.shape[1])))


def kernel(x, c, ada_w, ada_b, norm1_g, ffn1_w_gate, ffn1_w_up, ffn1_w_down, norm2_g, w_in, forget_bias, conv_w, group_norm_g, w_out, norm3_g, ffn2_w_gate, ffn2_w_up, ffn2_w_down, final_g, loss_target, m_ada_w, m_ada_b, m_norm1_g, m_ffn1_w_gate, m_ffn1_w_up, m_ffn1_w_down, m_norm2_g, m_w_in, m_forget_bias, m_conv_w, m_group_norm_g, m_w_out, m_norm3_g, m_ffn2_w_gate, m_ffn2_w_up, m_ffn2_w_down, m_final_g, v_ada_w, v_ada_b, v_norm1_g, v_ffn1_w_gate, v_ffn1_w_up, v_ffn1_w_down, v_norm2_g, v_w_in, v_forget_bias, v_conv_w, v_group_norm_g, v_w_out, v_norm3_g, v_ffn2_w_gate, v_ffn2_w_up, v_ffn2_w_down, v_final_g):
    xi, yi, ci = lax.axis_index("x"), lax.axis_index("y"), lax.axis_index("c")
    chip = 2 * xi + yi
    dev = 4 * xi + 2 * yi + ci
    _, s, d = x.shape
    att_w = d // 2
    conv_width = d - att_w
    n_heads = att_w // HEAD_DIM
    in_shard = w_in.shape[1]
    in_rows = _round_up(in_shard, 32)
    cs = conv_w.shape[1]
    mod_shard = ada_w.shape[1]
    assert N_MOD * d == N_CHIPS * mod_shard and conv_width == N_CHIPS * cs and n_heads % 2 == 0

    def t_bf(w):
        return w.T.astype(BF16)

    shards = {"ffn1_gu": [_halves(t_bf(ffn1_w_gate)), _halves(t_bf(ffn1_w_up))], "ffn1_d": [_halves(ffn1_w_down.astype(BF16))],
              "mix_in": [_halves(_pad_rows(t_bf(w_in), in_rows))], "mix_out": [_halves(w_out.astype(BF16))],
              "ffn2": [_halves(t_bf(ffn2_w_gate)), _halves(t_bf(ffn2_w_up)), _halves(ffn2_w_down.astype(BF16))]}
    core = ci.astype(jnp.int32).reshape(1)
    chip_arr = chip.astype(jnp.int32).reshape(1)
    traffic = _WeightTraffic(shards, core, chip_arr)

    cond = _small_gather_exchange(_rows_at(c, 0, 8, d) + _rows_at(conv_w, 1, 8, d))
    _run_exchange(_join(traffic.gather_ici("ffn1_gu"), cond), "gather_ffn1_ici")
    got0 = cond.results[0].reshape(N_DEV, 8, d)
    c16 = _pad_rows(got0[:, 0, :], 16)
    conv_full = got0[0::2, 1:1 + CONV_K, :cs].transpose(1, 0, 2).reshape(CONV_K, conv_width)

    ada_b_mine = lax.dynamic_slice(ada_b, (chip * mod_shard,), (mod_shard,))[None, :]
    mods = _small_gather_exchange(_ada_fwd(c16, ada_w, ada_b_mine, "ada_fwd"))
    _run_exchange(_join(traffic.gather_pass("ffn1_gu"), mods), "gather_ffn1_pass")
    got1 = mods.results[0].reshape(N_DEV, 16, mod_shard)
    mod_mine = lax.dynamic_index_in_dim(got1[0::2], dev, axis=1, keepdims=False).reshape(N_MOD, d)
    mod = [mod_mine[i:i + 1] for i in range(N_MOD)]

    gains = [g[None, :] for g in (norm1_g, norm2_g, norm3_g, final_g, group_norm_g)]
    dx, loss_row, dmod, dgains, dbias, dconv_w = _layer_step(
        x[0], loss_target[0], mod, gains, forget_bias, conv_full, traffic, att_w, in_shard, in_rows)

    pack = sum(_rows_at(g, ROW_GAINS + i, SMALL_ROWS, d) for i, g in enumerate(dgains))
    pack += _rows_at(loss_row, ROW_LOSS, SMALL_ROWS, d) + _rows_at(dbias[None, :], ROW_FORGET, SMALL_ROWS, d)
    pack += _rows_at(dconv_w, ROW_CONV, SMALL_ROWS, d)
    pack += sum(_rows_at(g, ROW_MOD + i, SMALL_ROWS, d) for i, g in enumerate(dmod))
    small = _small_gather_exchange(pack)
    _run_exchange(_join(traffic.share("ffn1_wu"), small), "gather_small_grads")
    got2 = small.results[0].reshape(N_DEV, SMALL_ROWS, d)
    tot = _sum_devices(got2, "sum_small_grads")
    loss = tot[ROW_LOSS, 0]
    grad_ada_b = tot[ROW_MOD:ROW_MOD + N_MOD].reshape(N_MOD * d)
    grad_conv = lax.dynamic_slice(tot[ROW_CONV:ROW_CONV + CONV_K], (0, chip * cs), (CONV_K, cs))
    dmod_all = got2[:, ROW_MOD:ROW_MOD + N_MOD, :].reshape(N_DEV, N_MOD * d)
    dmod16 = _pad_rows(lax.dynamic_slice(dmod_all, (0, chip * mod_shard), (N_DEV, mod_shard)), 16)

    out = {"ada_w": tuple(_ada_update(c16.T, dmod16, ada_w, m_ada_w, v_ada_w, "adamw_ada_w"))}
    totals = (traffic.totals("ffn1_wg") + traffic.totals("ffn1_wu") + traffic.totals("ffn1_wd")
              + traffic.totals("mix") + traffic.totals("ffn2"))

    names = ("ffn1_w_gate", "ffn1_w_up", "ffn1_w_down", "w_in", "w_out", "ffn2_w_gate", "ffn2_w_up", "ffn2_w_down")
    transposed = ("ffn1_w_gate", "ffn1_w_up", "w_in", "ffn2_w_gate", "ffn2_w_up")
    params = {"ffn1_w_gate": (ffn1_w_gate, m_ffn1_w_gate, v_ffn1_w_gate), "ffn1_w_up": (ffn1_w_up, m_ffn1_w_up, v_ffn1_w_up),
              "ffn1_w_down": (ffn1_w_down, m_ffn1_w_down, v_ffn1_w_down), "w_in": (w_in, m_w_in, v_w_in),
              "w_out": (w_out, m_w_out, v_w_out), "ffn2_w_gate": (ffn2_w_gate, m_ffn2_w_gate, v_ffn2_w_gate),
              "ffn2_w_up": (ffn2_w_up, m_ffn2_w_up, v_ffn2_w_up), "ffn2_w_down": (ffn2_w_down, m_ffn2_w_down, v_ffn2_w_down)}
    for name_, (mine, theirs) in zip(names, totals):
        w, m, v = params[name_]
        if name_ in transposed:
            w, m, v = w.T, m.T, v.T
        if name_ == "w_in":
            both = jnp.where(ci == 0, jnp.concatenate([mine, theirs]), jnp.concatenate([theirs, mine]))[:in_shard]
            res = (both,) + tuple(_adamw(w, both, m, v, "adamw_" + name_))
        else:
            res = _adamw_halves(w, mine, theirs, m, v, core, "adamw_" + name_)
        out[name_] = tuple(r.T for r in res) if name_ in transposed else tuple(res)

    def small_pack(ada_b_, gains_, forget_, conv_):
        p = _rows_at(ada_b_.reshape(N_MOD, d), PROW_ADA_B, SMALL_ROWS, d)
        p += sum(_rows_at(g[None, :], PROW_GAINS + i, SMALL_ROWS, d) for i, g in enumerate(gains_))
        p += _rows_at(forget_[None, :], PROW_FORGET, SMALL_ROWS, d) + _rows_at(conv_, PROW_CONV, SMALL_ROWS, d)
        return p

    g_gains = [tot[ROW_GAINS + i] for i in range(5)]
    g_forget = tot[ROW_FORGET, :n_heads]
    sw = small_pack(ada_b, (norm1_g, norm2_g, norm3_g, final_g, group_norm_g), forget_bias, conv_w)
    sm = small_pack(m_ada_b, (m_norm1_g, m_norm2_g, m_norm3_g, m_final_g, m_group_norm_g), m_forget_bias, m_conv_w)
    sv = small_pack(v_ada_b, (v_norm1_g, v_norm2_g, v_norm3_g, v_final_g, v_group_norm_g), v_forget_bias, v_conv_w)
    sg = small_pack(grad_ada_b, g_gains, g_forget, grad_conv)
    small = (sg,) + tuple(_adamw(sw, sg, sm, sv, "adamw_small"))

    def unpack(p):
        r = {"ada_b": p[PROW_ADA_B:PROW_ADA_B + N_MOD].reshape(N_MOD * d), "forget_bias": p[PROW_FORGET, :n_heads],
             "conv_w": p[PROW_CONV:PROW_CONV + CONV_K, :cs]}
        for i, nm in enumerate(("norm1_g", "norm2_g", "norm3_g", "final_g", "group_norm_g")):
            r[nm] = p[PROW_GAINS + i]
        return r

    small = [unpack(p) for p in small]
    order = ("ada_w", "ada_b", "norm1_g", "ffn1_w_gate", "ffn1_w_up", "ffn1_w_down", "norm2_g", "w_in", "forget_bias",
             "conv_w", "group_norm_g", "w_out", "norm3_g", "ffn2_w_gate", "ffn2_w_up", "ffn2_w_down", "final_g")
    result = [loss, dx[None]]
    for k in range(4):
        result += [out[nm][k] if nm in out else small[k][nm] for nm in order]
    return tuple(result)
```

```python
import functools
import math

import jax
import jax.numpy as jnp
from jax import lax
from jax.experimental import pallas as pl
from jax.experimental.pallas import tpu as pltpu

F32 = jnp.float32
BF16 = jnp.bfloat16

HEAD_DIM = 64
CONV_K = 3
N_MOD = 9
EPS = 1e-6
ADAM_LR = 0.001
ADAM_B1 = 0.9
ADAM_B2 = 0.999
ADAM_EPS = 1e-08
ADAM_WD = 0.01
ADAM_STEP = 10

LANES = 128
N_CHIPS = 4
N_DEV = 8
VMEM_LIMIT_BYTES = 56 * 1024 * 1024
MAX_CONTRACTION = 4096
NEG_BIG = -1e30
MESH = pl.DeviceIdType.MESH

_NT = (((1,), (1,)), ((), ()))
_NN = (((1,), (0,)), ((), ()))
_TN = (((0,), (0,)), ((), ()))


def _params(*sem):
    return pltpu.CompilerParams(dimension_semantics=sem, vmem_limit_bytes=VMEM_LIMIT_BYTES)


class _Exchange:
    def __init__(self, inputs, out_shapes, n_sems, start, finish, aliases=None):
        self.inputs, self.out_shapes, self.n_sems = list(inputs), list(out_shapes), n_sems
        self.start, self.finish, self.aliases = start, finish, dict(aliases or {})
        self.results = None

    def set_results(self, results):
        self.results = list(results)


class _SemaphoreWindow:
    def __init__(self, sems, base):
        self.sems, self.base = sems, base
        self.at = self

    def __getitem__(self, k):
        return self.sems.at[self.base + k]


class _JoinedExchange(_Exchange):
    def __init__(self, parts):
        self.parts = parts
        aliases, i0, o0 = {}, 0, 0
        for p in parts:
            aliases.update({i0 + a: o0 + b for a, b in p.aliases.items()})
            i0, o0 = i0 + len(p.inputs), o0 + len(p.out_shapes)

        def each(method, src, dst, send_sems, recv_sems):
            i0 = o0 = s0 = 0
            for p in parts:
                i1, o1 = i0 + len(p.inputs), o0 + len(p.out_shapes)
                getattr(p, method)(src[i0:i1], dst[o0:o1], _SemaphoreWindow(send_sems, s0), _SemaphoreWindow(recv_sems, s0))
                i0, o0, s0 = i1, o1, s0 + p.n_sems

        super().__init__([a for p in parts for a in p.inputs], [o for p in parts for o in p.out_shapes],
                         sum(p.n_sems for p in parts), functools.partial(each, "start"), functools.partial(each, "finish"),
                         aliases)

    def set_results(self, results):
        o0 = 0
        for p in self.parts:
            p.set_results(results[o0:o0 + len(p.out_shapes)])
            o0 += len(p.out_shapes)


def _join(*parts):
    return parts[0] if len(parts) == 1 else _JoinedExchange(list(parts))


def _pc(body, exchange=None, **kw):
    if exchange is None:
        return pl.pallas_call(body, **kw)
    grid = kw["grid"]
    single = not isinstance(kw["out_shape"], (tuple, list))
    out_shape = [kw["out_shape"]] if single else list(kw["out_shape"])
    out_specs = [kw["out_specs"]] if single else list(kw["out_specs"])
    in_specs = list(kw["in_specs"])
    scratch = list(kw.get("scratch_shapes", ()))
    n_in, n_out, n_scr = len(in_specs), len(out_shape), len(scratch)
    n_xi, n_xo = len(exchange.inputs), len(exchange.out_shapes)

    def wrapped(*refs):
        pos = [n_in, n_in + n_xi, n_in + n_xi + n_out, n_in + n_xi + n_out + n_xo]
        ins, x_in, outs, x_out = refs[:pos[0]], refs[pos[0]:pos[1]], refs[pos[1]:pos[2]], refs[pos[2]:pos[3]]
        scr = refs[pos[3]:pos[3] + n_scr]
        send_sems, recv_sems = refs[pos[3] + n_scr:]
        ids = [pl.program_id(a) for a in range(len(grid))]
        first = functools.reduce(jnp.logical_and, [i == 0 for i in ids])
        last = functools.reduce(jnp.logical_and, [i == g - 1 for i, g in zip(ids, grid)])

        @pl.when(first)
        def _():
            exchange.start(x_in, x_out, send_sems, recv_sems)

        body(*ins, *outs, *scr)

        @pl.when(last)
        def _():
            exchange.finish(x_in, x_out, send_sems, recv_sems)

    call = pl.pallas_call(
        wrapped, out_shape=tuple(out_shape) + tuple(exchange.out_shapes), grid=grid,
        in_specs=in_specs + [_ANY] * n_xi, out_specs=tuple(out_specs) + (_ANY,) * n_xo,
        scratch_shapes=scratch + [pltpu.SemaphoreType.DMA((exchange.n_sems,)), pltpu.SemaphoreType.DMA((exchange.n_sems,))],
        input_output_aliases={n_in + a: n_out + b for a, b in exchange.aliases.items()},
        compiler_params=_params(*(["arbitrary"] * len(grid))), name=kw["name"])

    def run(*args):
        res = call(*args, *exchange.inputs)
        exchange.set_results(res[n_out:])
        return res[0] if single else tuple(res[:n_out])

    return run


_ANY = pl.BlockSpec(memory_space=pl.ANY)


def _tile(n, pref, mult):
    best = None
    t = mult
    while t <= min(n, pref):
        if n % t == 0:
            best = t
        t += mult
    return n if best is None else best


def _sds(shape, dtype):
    return jax.ShapeDtypeStruct(shape, dtype)


def _vec_spec(d):
    return pl.BlockSpec((1, d), lambda *_: (0, 0))


def _norm_mod_fwd(x, g, shift, scale, name):
    s, d = x.shape
    tr = _tile(s, 512, 16)

    def body(x_ref, g_ref, sh_ref, sc_ref, h_ref):
        xv = x_ref[...]
        rstd = lax.rsqrt(jnp.mean(xv * xv, axis=-1, keepdims=True) + EPS)
        n = xv * rstd * g_ref[...]
        h_ref[...] = (n * (1.0 + sc_ref[...]) + sh_ref[...]).astype(BF16)

    row = pl.BlockSpec((tr, d), lambda i: (i, 0))
    return _pc(body, out_shape=_sds((s, d), BF16), grid=(s // tr,),
               in_specs=[row, _vec_spec(d), _vec_spec(d), _vec_spec(d)], out_specs=row,
               compiler_params=_params("parallel"), name=name)(x, g, shift, scale)


def _through_gate(dx, f_ref, gate_ref, df_ref, dgate_ref):
    df_ref[...] = (dx * gate_ref[...]).astype(BF16)
    dgate_ref[...] += jnp.sum(dx * f_ref[...].astype(F32), axis=0, keepdims=True)


def _norm_mod_bwd(dh, x, g, scale, dres, name, f=None, gate=None, rows=None, exchange=None):
    d = x.shape[1]
    first_row, s = rows if rows else (0, x.shape[0])
    gated = f is not None
    terms = list(zip(*dh)) if isinstance(dh, tuple) else None
    tr = _tile(s, 256, 16)
    b0 = first_row // tr
    assert first_row % tr == 0
    n_lead = 2 * len(terms) if terms else 1

    def body(*refs):
        lead, (x_ref, g_ref, sc_ref, dres_ref), rest = refs[:n_lead], refs[n_lead:n_lead + 4], refs[n_lead + 4:]
        f_ref, gate_ref = rest[:2] if gated else (None, None)
        dx_ref, dsh_ref, dsc_ref, dg_ref = rest[2:6] if gated else rest[:4]
        df_ref, dgate_ref = rest[6:8] if gated else (None, None)

        @pl.when(pl.program_id(0) == 0)
        def _():
            for ref in (dsh_ref, dsc_ref, dg_ref) + ((dgate_ref,) if gated else ()):
                ref[...] = jnp.zeros_like(ref)

        if terms:
            dhv = lax.dot_general(lead[0][...], lead[1][...], _NN, preferred_element_type=F32)
            for p in range(1, len(terms)):
                dhv += lax.dot_general(lead[2 * p][...], lead[2 * p + 1][...], _NN, preferred_element_type=F32)
        else:
            dhv = lead[0][...]
        xv = x_ref[...]
        gv = g_ref[...]
        rstd = lax.rsqrt(jnp.mean(xv * xv, axis=-1, keepdims=True) + EPS)
        xhat = xv * rstd
        dn = dhv * (1.0 + sc_ref[...])
        dsh_ref[...] += jnp.sum(dhv, axis=0, keepdims=True)
        dsc_ref[...] += jnp.sum(dhv * (xhat * gv), axis=0, keepdims=True)
        dg_ref[...] += jnp.sum(dn * xhat, axis=0, keepdims=True)
        dxh = dn * gv
        proj = jnp.mean(dxh * xhat, axis=-1, keepdims=True)
        dx = dres_ref[...] + rstd * (dxh - xhat * proj)
        dx_ref[...] = dx
        if gated:
            _through_gate(dx, f_ref, gate_ref, df_ref, dgate_ref)

    row = pl.BlockSpec((tr, d), lambda i: (b0 + i, 0))
    out_row = pl.BlockSpec((tr, d), lambda i: (i, 0))
    vec = _vec_spec(d)
    if terms:
        in_specs, args = [], []
        for l, r in terms:
            assert l.shape[1] == r.shape[0] <= MAX_CONTRACTION and r.shape[1] == d
            in_specs += [pl.BlockSpec((tr, l.shape[1]), lambda i: (b0 + i, 0)), pl.BlockSpec(r.shape, lambda i: (0, 0))]
            args += [l, r]
    else:
        in_specs, args = [row], [dh]
    in_specs += [row, vec, vec, row]
    args += [x, g, scale, dres]
    out_shape = [_sds((s, d), F32), _sds((1, d), F32), _sds((1, d), F32), _sds((1, d), F32)]
    out_specs = [out_row, vec, vec, vec]
    if gated:
        out_shape += [_sds((s, d), BF16), _sds((1, d), F32)]
        out_specs += [out_row, vec]
        in_specs += [row, vec]
        args += [f, gate]
    return _pc(body, exchange, out_shape=tuple(out_shape), grid=(s // tr,), in_specs=in_specs,
               out_specs=tuple(out_specs), compiler_params=_params("arbitrary"), name=name)(*args)


def _down_final_loss(hid, wd, res, gate, g, target, name):
    s, d = res.shape
    k = hid.shape[1]
    assert k <= MAX_CONTRACTION
    tr = _tile(s, 256, 16)
    nsteps = s // tr

    def body(hid_ref, wd_ref, res_ref, gate_ref, g_ref, t_ref, dx_ref, loss_ref, dg_ref, df_ref, dgate_ref):
        i = pl.program_id(0)

        @pl.when(i == 0)
        def _():
            loss_ref[...] = jnp.zeros_like(loss_ref)
            dg_ref[...] = jnp.zeros_like(dg_ref)
            dgate_ref[...] = jnp.zeros_like(dgate_ref)

        f = lax.dot_general(hid_ref[...], wd_ref[...], _NN, preferred_element_type=F32)
        gatev = gate_ref[...]
        xv = res_ref[...] + gatev * f
        gv = g_ref[...]
        rstd = lax.rsqrt(jnp.mean(xv * xv, axis=-1, keepdims=True) + EPS)
        xhat = xv * rstd
        err = xhat * gv - t_ref[...]
        dy = err * (1.0 / d)
        loss_ref[...] += jnp.sum(0.5 * err * dy, axis=0, keepdims=True)
        dg_ref[...] += jnp.sum(dy * xhat, axis=0, keepdims=True)
        dxh = dy * gv
        proj = jnp.mean(dxh * xhat, axis=-1, keepdims=True)
        dx = rstd * (dxh - xhat * proj)
        dx_ref[...] = dx
        df_ref[...] = (dx * gatev).astype(BF16)
        dgate_ref[...] += jnp.sum(dx * f, axis=0, keepdims=True)

        @pl.when(i == nsteps - 1)
        def _():
            loss_ref[...] = jnp.broadcast_to(jnp.sum(loss_ref[...], axis=-1, keepdims=True), loss_ref.shape)

    row = pl.BlockSpec((tr, d), lambda i: (i, 0))
    vec = _vec_spec(d)
    return _pc(body, out_shape=(_sds((s, d), F32), _sds((1, d), F32), _sds((1, d), F32), _sds((s, d), BF16), _sds((1, d), F32)),
               grid=(nsteps,),
               in_specs=[pl.BlockSpec((tr, k), lambda i: (i, 0)), pl.BlockSpec((k, d), lambda i: (0, 0)), row, vec, vec, row],
               out_specs=(row, vec, vec, row, vec),
               compiler_params=_params("arbitrary"), name=name)(hid, wd, res, gate, g, target)


def _mm(lhs, rhs, dims, out_dtype, name, res=None, gate=None, aux_dtype=None, exchange=None):
    lhs_list = list(lhs) if isinstance(lhs, (list, tuple)) else [lhs]
    rhs_list = list(rhs) if isinstance(rhs, (list, tuple)) else [rhs]
    n_terms = len(lhs_list)
    assert n_terms == len(rhs_list)
    m = lhs_list[0].shape[1 if dims == "tn" else 0]
    n = rhs_list[0].shape[0 if dims == "nt" else 1]
    tn = _tile(n, 1024, LANES)
    tm = _tile(m, 512, LANES if dims == "tn" else 16)
    dn = {"nn": _NN, "nt": _NT, "tn": _TN}[dims]
    in_specs, args = [], []
    for l, r in zip(lhs_list, rhs_list):
        k = l.shape[0 if dims == "tn" else 1]
        assert k == r.shape[1 if dims == "nt" else 0] and k <= MAX_CONTRACTION, (l.shape, r.shape, dims)
        in_specs.append(pl.BlockSpec((k, tm), lambda i, j: (0, i)) if dims == "tn" else pl.BlockSpec((tm, k), lambda i, j: (i, 0)))
        in_specs.append(pl.BlockSpec((tn, k), lambda i, j: (j, 0)) if dims == "nt" else pl.BlockSpec((k, tn), lambda i, j: (0, j)))
        args += [l, r]
    out_spec = pl.BlockSpec((tm, tn), lambda i, j: (i, j))
    has_res, has_gate, has_aux = res is not None, gate is not None, aux_dtype is not None

    def body(*refs):
        refs = list(refs)
        pos = 2 * n_terms
        res_ref = gate_ref = aux_ref = None
        if has_res:
            res_ref = refs[pos]; pos += 1
        if has_gate:
            gate_ref = refs[pos]; pos += 1
        out_ref = refs[pos]; pos += 1
        if has_aux:
            aux_ref = refs[pos]
        acc = lax.dot_general(refs[0][...], refs[1][...], dn, preferred_element_type=F32)
        for p in range(1, n_terms):
            acc += lax.dot_general(refs[2 * p][...], refs[2 * p + 1][...], dn, preferred_element_type=F32)
        if has_aux:
            aux_ref[...] = acc.astype(aux_dtype)
        if has_gate:
            acc = acc * gate_ref[...]
        if has_res:
            acc = res_ref[...] + acc
        out_ref[...] = acc.astype(out_dtype)

    if has_res:
        in_specs.append(out_spec); args.append(res)
    if has_gate:
        in_specs.append(pl.BlockSpec((1, tn), lambda i, j: (0, j))); args.append(gate)
    out_shape = [_sds((m, n), out_dtype)]
    out_specs = [out_spec]
    if has_aux:
        out_shape.append(_sds((m, n), aux_dtype)); out_specs.append(out_spec)
    outs = _pc(body, exchange, out_shape=tuple(out_shape), grid=(m // tm, n // tn), in_specs=in_specs,
               out_specs=tuple(out_specs), compiler_params=_params("parallel", "parallel"), name=name)(*args)
    return outs if has_aux else outs[0]


def _ffn_up(h, wg_t, wu_t, name, exchange=None):
    s, d = h.shape
    f = wg_t.shape[0]
    tm = _tile(s, 1024, 16)
    tn = _tile(f, 256, LANES)

    def body(h_ref, wg_ref, wu_ref, a_ref, u_ref, hid_ref):
        hv = h_ref[...]
        a = lax.dot_general(hv, wg_ref[...], _NT, preferred_element_type=F32)
        u = lax.dot_general(hv, wu_ref[...], _NT, preferred_element_type=F32)
        a_ref[...] = a.astype(BF16)
        u_ref[...] = u.astype(BF16)
        hid_ref[...] = (a * jax.nn.sigmoid(a) * u).astype(BF16)

    hs = pl.BlockSpec((tm, d), lambda i, j: (i, 0))
    ws = pl.BlockSpec((tn, d), lambda i, j: (j, 0))
    os_ = pl.BlockSpec((tm, tn), lambda i, j: (i, j))
    return _pc(body, exchange, out_shape=(_sds((s, f), BF16),) * 3, grid=(s // tm, f // tn),
               in_specs=[hs, ws, ws], out_specs=(os_, os_, os_),
               compiler_params=_params("parallel", "parallel"), name=name)(h, wg_t, wu_t)


def _ffn_dact(df, wd, a, u, name, exchange=None):
    s, d = df.shape
    f = wd.shape[0]
    tm = _tile(s, 1024, 16)
    tn = _tile(f, 256, LANES)

    def body(df_ref, wd_ref, a_ref, u_ref, da_ref, du_ref):
        dhid = lax.dot_general(df_ref[...], wd_ref[...], _NT, preferred_element_type=F32)
        av = a_ref[...].astype(F32)
        uv = u_ref[...].astype(F32)
        sig = jax.nn.sigmoid(av)
        da_ref[...] = (dhid * uv * (sig * (1.0 + av * (1.0 - sig)))).astype(BF16)
        du_ref[...] = (dhid * (av * sig)).astype(BF16)

    ds_ = pl.BlockSpec((tm, d), lambda i, j: (i, 0))
    ws = pl.BlockSpec((tn, d), lambda i, j: (j, 0))
    os_ = pl.BlockSpec((tm, tn), lambda i, j: (i, j))
    return _pc(body, exchange, out_shape=(_sds((s, f), BF16),) * 2, grid=(s // tm, f // tn),
               in_specs=[ds_, ws, os_, os_], out_specs=(os_, os_),
               compiler_params=_params("parallel", "parallel"), name=name)(df, wd, a, u)


def _split3(v):
    hi = v.astype(BF16)
    r1 = v - hi.astype(F32)
    mid = r1.astype(BF16)
    lo = (r1 - mid.astype(F32)).astype(BF16)
    return hi, mid, lo


def _dot3(v, mat):
    hi, mid, lo = _split3(v)
    out = lax.dot_general(hi, mat, _NN, preferred_element_type=F32)
    out += lax.dot_general(mid, mat, _NN, preferred_element_type=F32)
    out += lax.dot_general(lo, mat, _NN, preferred_element_type=F32)
    return out


def _forget_fwd(flog_t, bias, name):
    h, s = flog_t.shape
    blk = _tile(s, 512, LANES)
    tri = (jnp.arange(blk)[:, None] <= jnp.arange(blk)[None, :]).astype(BF16)

    def body(z_ref, b_ref, tri_ref, f_ref, carry):
        @pl.when(pl.program_id(0) == 0)
        def _():
            carry[...] = jnp.zeros_like(carry)

        z = z_ref[...] + b_ref[...]
        e = jnp.exp(-jnp.abs(z))
        w = 1.0 + e
        log1p_e = jnp.where(w == 1.0, e, jnp.log(w) * (e / (w - 1.0)))
        lf = jnp.minimum(z, 0.0) - log1p_e
        out = carry[...] + _dot3(lf, tri_ref[...])
        for j, piece in enumerate(_split3(out)):
            f_ref[j] = piece
        carry[...] = out[:, blk - 1:blk]

    zs = pl.BlockSpec((h, blk), lambda i: (0, i))
    return _pc(body, out_shape=_sds((3, h, s), BF16), grid=(s // blk,),
               in_specs=[zs, pl.BlockSpec((h, 1), lambda i: (0, 0)), pl.BlockSpec((blk, blk), lambda i: (0, 0))],
               out_specs=pl.BlockSpec((3, h, blk), lambda i: (0, 0, i)), scratch_shapes=[pltpu.VMEM((h, 1), F32)],
               compiler_params=_params("arbitrary"), name=name)(flog_t, bias, tri)


def _forget_bwd(df_t, flog_t, bias, name):
    h, s = flog_t.shape
    blk = _tile(s, 512, LANES)
    nb = s // blk
    tri = (jnp.arange(blk)[:, None] >= jnp.arange(blk)[None, :]).astype(BF16)

    def body(df_ref, z_ref, b_ref, tri_ref, dz_ref, db_ref, carry):
        @pl.when(pl.program_id(0) == 0)
        def _():
            carry[...] = jnp.zeros_like(carry)
            db_ref[...] = jnp.zeros_like(db_ref)

        rc = carry[...] + _dot3(df_ref[...], tri_ref[...])
        carry[...] = rc[:, 0:1]
        dz = rc * jax.nn.sigmoid(-(z_ref[...] + b_ref[...]))
        dz_ref[...] = dz
        db_ref[...] += jnp.sum(dz, axis=-1, keepdims=True)

    rev = pl.BlockSpec((h, blk), lambda i: (0, nb - 1 - i))
    col = pl.BlockSpec((h, 1), lambda i: (0, 0))
    return _pc(body, out_shape=(_sds((h, s), F32), _sds((h, 1), F32)), grid=(nb,),
               in_specs=[rev, rev, col, pl.BlockSpec((blk, blk), lambda i: (0, 0))],
               out_specs=(rev, col), scratch_shapes=[pltpu.VMEM((h, 1), F32)],
               compiler_params=_params("arbitrary"), name=name)(df_t, flog_t, bias, tri)


def _attn_tiles(s):
    return _tile(s, 1024, LANES)


def _attn_half(t):
    return t // 2 if t >= 4 * LANES else t


BIAS_ROWS = 16


def _attn_prep(qkv, f_pieces, name):
    s = qkv.shape[0]
    a_w = qkv.shape[1] // 3
    npair = a_w // LANES
    t = _attn_tiles(s)
    scale = 1.0 / math.sqrt(HEAD_DIM)

    six = f_pieces[:, :2 * npair].reshape(3, npair, 2, s).transpose(1, 3, 2, 0).reshape(npair, s, 6)
    feat = jnp.concatenate([six, jnp.ones((npair, s, 1), BF16), jnp.zeros((npair, s, BIAS_ROWS - 7), BF16)], axis=-1)
    place_q = [[0.0] * (2 * LANES) for _ in range(BIAS_ROWS)]
    place_k = [[0.0] * (2 * LANES) for _ in range(BIAS_ROWS)]
    for hh in range(2):
        b0 = hh * LANES + (HEAD_DIM if hh == 0 else 0)
        for j in range(3):
            place_q[3 * hh + j][b0 + j] = 1.0
            place_q[6][b0 + 3 + j] = 1.0
            place_k[6][b0 + j] = 1.0
            place_k[3 * hh + j][b0 + 3 + j] = -1.0
    place_q = jnp.array(place_q, BF16)
    place_k = jnp.array(place_k, BF16)

    def body(q_ref, k_ref, v_ref, f_ref, pq_ref, pk_ref, qa_ref, ka_ref, va_ref):
        lane = lax.broadcasted_iota(jnp.int32, (1, LANES), 1)
        q2 = (q_ref[...].astype(F32) * scale).astype(BF16)
        k2, v2 = k_ref[...], v_ref[...]
        qx = lax.dot_general(f_ref[0], pq_ref[...], _NN, preferred_element_type=F32).astype(BF16)
        kx = lax.dot_general(f_ref[0], pk_ref[...], _NN, preferred_element_type=F32).astype(BF16)
        for hh in range(2):
            real = (lane < HEAD_DIM) if hh == 0 else (lane >= HEAD_DIM)
            cols = slice(hh * LANES, (hh + 1) * LANES)
            qa_ref[:, cols] = jnp.where(real, q2, qx[:, cols])
            ka_ref[:, cols] = jnp.where(real, k2, kx[:, cols])
            va_ref[:, cols] = jnp.where(real, v2, jnp.zeros_like(v2))

    def col(off):
        return pl.BlockSpec((t, LANES), lambda p, i: (i, off + p))

    out = pl.BlockSpec((t, 2 * LANES), lambda p, i: (i, p))
    place = pl.BlockSpec((BIAS_ROWS, 2 * LANES), lambda p, i: (0, 0))
    return _pc(body, out_shape=(_sds((s, 2 * a_w), BF16),) * 3, grid=(npair, s // t),
               in_specs=[col(0), col(npair), col(2 * npair), pl.BlockSpec((1, t, BIAS_ROWS), lambda p, i: (p, i, 0)),
                         place, place],
               out_specs=(out, out, out), compiler_params=_params("parallel", "parallel"), name=name)(
                   qkv, qkv, qkv, feat, place_q, place_k)


def _attn_fwd(qa, ka, va, name, exchange=None):
    s = qa.shape[0]
    a_w = qa.shape[1] // 2
    npair = a_w // LANES
    t = _attn_tiles(s)
    nq = s // t
    half = _attn_half(t)

    def body(q_ref, k_ref, v_ref, o_ref, lse_ref, m_sc, l_sc, acc_sc):
        qi = pl.program_id(1)
        first = lax.broadcasted_iota(jnp.int32, (1, LANES), 1) < HEAD_DIM
        m_sc[...] = jnp.full_like(m_sc, NEG_BIG)
        l_sc[...] = jnp.zeros_like(l_sc)
        acc_sc[...] = jnp.zeros_like(acc_sc)

        def step(q0, k_start, size, diag):
            q_sl = slice(q0, q0 + size)
            k_rows = pl.ds(pl.multiple_of(k_start, size), size)
            m_old = m_sc[q_sl, :]
            keep = None
            if diag:
                keep = (lax.broadcasted_iota(jnp.int32, (size, size), 0) >= lax.broadcasted_iota(jnp.int32, (size, size), 1))
            m_new, rs, pv = [], [], []
            for hh in range(2):
                cols = slice(hh * LANES, (hh + 1) * LANES)
                sc = lax.dot_general(q_ref[q_sl, cols], k_ref[k_rows, cols], _NT, preferred_element_type=F32)
                if diag:
                    sc = jnp.where(keep, sc, NEG_BIG)
                mo = m_old[:, hh * HEAD_DIM:hh * HEAD_DIM + 1]
                mn = jnp.maximum(mo, jnp.max(sc, axis=1, keepdims=True))
                p = jnp.exp(sc - mn)
                m_new.append(mn)
                rs.append(jnp.sum(p, axis=1, keepdims=True))
                pv.append(lax.dot_general(p.astype(BF16), v_ref[k_rows, cols], _NN, preferred_element_type=F32))
            m2 = jnp.where(first, m_new[0], m_new[1])
            alpha = jnp.exp(m_old - m2)
            m_sc[q_sl, :] = m2
            l_sc[q_sl, :] = alpha * l_sc[q_sl, :] + jnp.where(first, rs[0], rs[1])
            acc_sc[q_sl, :] = alpha * acc_sc[q_sl, :] + pv[0] + pv[1]

        def below_diagonal(ki, carry):
            step(0, ki * t, t, False)
            return carry

        lax.fori_loop(0, qi, below_diagonal, 0)
        step(0, qi * t, half, True)
        if half < t:
            step(half, qi * t, half, False)
            step(half, qi * t + half, half, True)
        l2 = l_sc[...]
        o_ref[...] = acc_sc[...] / l2
        lse_ref[...] = m_sc[...] + jnp.log(l2)

    qs = pl.BlockSpec((t, 2 * LANES), lambda p, qi: (qi, p))
    ks = pl.BlockSpec((s, 2 * LANES), lambda p, qi: (0, p))
    os_ = pl.BlockSpec((t, LANES), lambda p, qi: (qi, p))
    return _pc(body, exchange, out_shape=(_sds((s, a_w), F32), _sds((s, a_w), F32)), grid=(npair, nq),
               in_specs=[qs, ks, ks], out_specs=(os_, os_),
               scratch_shapes=[pltpu.VMEM((t, LANES), F32)] * 3,
               compiler_params=_params("parallel", "arbitrary"), name=name)(qa, ka, va)


def _attn_bwd(qa, ka, va, do, o, lse, name, exchange=None):
    s = qa.shape[0]
    a_w = qa.shape[1] // 2
    npair = a_w // LANES
    t = _attn_tiles(s)
    nq = s // t
    half = _attn_half(t)
    scale = 1.0 / math.sqrt(HEAD_DIM)

    def body(q_ref, k_ref, v_ref, do_ref, o_ref, lse_ref, dq_ref, dk_ref, dv_ref, qx_ref, kx_ref, dk_sc, dv_sc, kx_sc):
        ki = pl.program_id(1)
        first = lax.broadcasted_iota(jnp.int32, (1, LANES), 1) < HEAD_DIM

        @pl.when(ki == 0)
        def _():
            dq_ref[...] = jnp.zeros_like(dq_ref)
            qx_ref[...] = jnp.zeros_like(qx_ref)

        def step(q_start, k0, size, diag, assign):
            rows = pl.ds(pl.multiple_of(q_start, size), size)
            k_sl = slice(k0, k0 + size)
            do2 = do_ref[rows, :]
            lse2 = lse_ref[rows, :]
            dd = do2.astype(F32) * o_ref[rows, :]
            keep = None
            if diag:
                keep = (lax.broadcasted_iota(jnp.int32, (size, size), 0) >= lax.broadcasted_iota(jnp.int32, (size, size), 1))
            dq_h, dk_h, dv_h = [], [], []
            for hh in range(2):
                sel = first if hh == 0 else jnp.logical_not(first)
                cols = slice(hh * LANES, (hh + 1) * LANES)
                qh, kh, vh = q_ref[rows, cols], k_ref[k_sl, cols], v_ref[k_sl, cols]
                delta = jnp.sum(jnp.where(sel, dd, 0.0), axis=1, keepdims=True)
                sc = lax.dot_general(qh, kh, _NT, preferred_element_type=F32)
                if diag:
                    sc = jnp.where(keep, sc, NEG_BIG)
                p = jnp.exp(sc - lse2[:, hh * HEAD_DIM:hh * HEAD_DIM + 1])
                dp = lax.dot_general(do2, vh, _NT, preferred_element_type=F32)
                ds_b = (p * (dp - delta)).astype(BF16)
                dv_h.append(lax.dot_general(p.astype(BF16), do2, _TN, preferred_element_type=F32))
                dk_h.append(lax.dot_general(ds_b, qh, _TN, preferred_element_type=F32))
                dq_h.append(lax.dot_general(ds_b, kh, _NN, preferred_element_type=F32))
            dq_ref[rows, :] += jnp.where(first, dq_h[0], dq_h[1]) * scale
            qx_ref[rows, :] += jnp.where(first, dq_h[1], dq_h[0])
            dk_new = jnp.where(first, dk_h[0], dk_h[1])
            kx_new = jnp.where(first, dk_h[1], dk_h[0])
            dv_new = jnp.where(first, dv_h[0], dv_h[1])
            if assign:
                dk_sc[k_sl, :] = dk_new
                kx_sc[k_sl, :] = kx_new
                dv_sc[k_sl, :] = dv_new
            else:
                dk_sc[k_sl, :] += dk_new
                kx_sc[k_sl, :] += kx_new
                dv_sc[k_sl, :] += dv_new

        def below_diagonal(qi, carry):
            step(qi * t, 0, t, False, False)
            return carry

        step(ki * t, 0, half, True, True)
        if half < t:
            step(ki * t + half, 0, half, False, False)
            step(ki * t + half, half, half, True, True)
        lax.fori_loop(ki + 1, nq, below_diagonal, 0)
        dk_ref[...] = dk_sc[...].astype(BF16)
        dv_ref[...] = dv_sc[...].astype(BF16)
        kx_ref[...] = kx_sc[...]

    ks2 = pl.BlockSpec((t, 2 * LANES), lambda p, ki: (ki, p))
    qs2 = pl.BlockSpec((s, 2 * LANES), lambda p, ki: (0, p))
    whole = pl.BlockSpec((s, LANES), lambda p, ki: (0, p))
    kout = pl.BlockSpec((t, LANES), lambda p, ki: (ki, p))
    return _pc(body, exchange,
               out_shape=(_sds((s, a_w), F32), _sds((s, a_w), BF16), _sds((s, a_w), BF16), _sds((s, a_w), F32),
                          _sds((s, a_w), F32)),
               grid=(npair, nq), in_specs=[qs2, ks2, ks2, whole, whole, whole],
               out_specs=(whole, kout, kout, whole, kout),
               scratch_shapes=[pltpu.VMEM((t, LANES), F32)] * 3,
               compiler_params=_params("parallel", "arbitrary"), name=name)(qa, ka, va, do, o, lse)

def _decay_grads(qx, kx, name):
    s, a_w = qx.shape
    n_heads = a_w // HEAD_DIM
    tr = _tile(s, 512, 8)
    pick_q = [[0.0] * LANES for _ in range(a_w)]
    pick_k = [[0.0] * LANES for _ in range(a_w)]
    for h in range(n_heads):
        b0 = (h // 2) * LANES + (HEAD_DIM if h % 2 == 0 else 0)
        pick_q[b0][h] = 1.0
        pick_k[b0 + 3][h] = 1.0
    pick_q = jnp.array(pick_q, BF16)
    pick_k = jnp.array(pick_k, BF16)

    def body(qx_ref, kx_ref, pq_ref, pk_ref, o_ref):
        o_ref[...] = _dot3(qx_ref[...], pq_ref[...]) - _dot3(kx_ref[...], pk_ref[...])

    row = pl.BlockSpec((tr, a_w), lambda i: (i, 0))
    pick = pl.BlockSpec((a_w, LANES), lambda i: (0, 0))
    return _pc(body, out_shape=_sds((s, LANES), F32), grid=(s // tr,), in_specs=[row, row, pick, pick],
               out_specs=pl.BlockSpec((tr, LANES), lambda i: (i, 0)),
               compiler_params=_params("parallel"), name=name)(qx, kx, pick_q, pick_k)


def _shift_down(z, k, rows):
    return jnp.where(rows >= k, pltpu.roll(z, k, 0), 0.0)


def _shift_up(z, k, rows, n):
    return jnp.where(rows < n - k, pltpu.roll(z, n - k, 0), 0.0)


def _conv_fwd(bcx, conv_w, name):
    s = bcx.shape[0]
    cw = bcx.shape[1] // 3
    nb = cw // LANES

    def body(b_ref, c_ref, x_ref, w_ref, cv_ref):
        rows = lax.broadcasted_iota(jnp.int32, (s, LANES), 0)
        z = c_ref[...] * x_ref[...]
        w = w_ref[...]
        y = w[2:3, :] * z + w[1:2, :] * _shift_down(z, 1, rows) + w[0:1, :] * _shift_down(z, 2, rows)
        cv_ref[...] = b_ref[...] * y

    def col(off):
        return pl.BlockSpec((s, LANES), lambda j: (0, j + off))

    return _pc(body, out_shape=_sds((s, cw), F32), grid=(nb,),
               in_specs=[col(0), col(nb), col(2 * nb), pl.BlockSpec((CONV_K, LANES), lambda j: (0, j))],
               out_specs=col(0), compiler_params=_params("parallel"), name=name)(bcx, bcx, bcx, conv_w)


def _conv_bwd(dcv, bcx, conv_w, name):
    s = bcx.shape[0]
    cw = bcx.shape[1] // 3
    nb = cw // LANES

    def body(dcv_ref, b_ref, c_ref, x_ref, w_ref, db_ref, dc_ref, dxc_ref, dw_ref):
        rows = lax.broadcasted_iota(jnp.int32, (s, LANES), 0)
        cv_, xv = c_ref[...], x_ref[...]
        z = cv_ * xv
        w = w_ref[...]
        z1 = _shift_down(z, 1, rows)
        z2 = _shift_down(z, 2, rows)
        y = w[2:3, :] * z + w[1:2, :] * z1 + w[0:1, :] * z2
        dcvv = dcv_ref[...]
        db_ref[...] = (dcvv * y).astype(BF16)
        dy = dcvv * b_ref[...]
        dw_ref[0:1, :] = jnp.sum(dy * z2, axis=0, keepdims=True)
        dw_ref[1:2, :] = jnp.sum(dy * z1, axis=0, keepdims=True)
        dw_ref[2:3, :] = jnp.sum(dy * z, axis=0, keepdims=True)
        dz = w[2:3, :] * dy + w[1:2, :] * _shift_up(dy, 1, rows, s) + w[0:1, :] * _shift_up(dy, 2, rows, s)
        dc_ref[...] = (dz * xv).astype(BF16)
        dxc_ref[...] = (dz * cv_).astype(BF16)

    def col(off):
        return pl.BlockSpec((s, LANES), lambda j: (0, j + off))

    wspec = pl.BlockSpec((CONV_K, LANES), lambda j: (0, j))
    db, dc, dxc, dw = _pc(body, out_shape=(_sds((s, cw), BF16),) * 3 + (_sds((CONV_K, cw), F32),), grid=(nb,),
                          in_specs=[col(0), col(0), col(nb), col(2 * nb), wspec],
                          out_specs=(col(0), col(0), col(0), wspec),
                          compiler_params=_params("parallel"), name=name)(dcv, bcx, bcx, bcx, conv_w)
    return db, dc, dxc, dw


def _group_matrix():
    idx = jnp.arange(LANES) // HEAD_DIM
    return (idx[:, None] == idx[None, :]).astype(BF16)


def _group_sum(v, gmat):
    return _dot3(v, gmat)


def _gnorm_fwd(att, cv, gg, name):
    s, a_w = att.shape
    cw = cv.shape[1]
    d = a_w + cw
    tr = _tile(s, 512, 16)
    gmat = _group_matrix()

    def body(att_ref, cv_ref, gg_ref, gm_ref, yn_ref):
        gm = gm_ref[...]
        for c0 in range(0, d, LANES):
            y = att_ref[:, c0:c0 + LANES] if c0 < a_w else cv_ref[:, c0 - a_w:c0 - a_w + LANES]
            ms = _group_sum(y * y, gm) * (1.0 / HEAD_DIM)
            yn_ref[:, c0:c0 + LANES] = (y * lax.rsqrt(ms + EPS) * gg_ref[:, c0:c0 + LANES]).astype(BF16)

    return _pc(body, out_shape=_sds((s, d), BF16), grid=(s // tr,),
               in_specs=[pl.BlockSpec((tr, a_w), lambda i: (i, 0)), pl.BlockSpec((tr, cw), lambda i: (i, 0)),
                         _vec_spec(d), pl.BlockSpec((LANES, LANES), lambda i: (0, 0))],
               out_specs=pl.BlockSpec((tr, d), lambda i: (i, 0)),
               compiler_params=_params("parallel"), name=name)(att, cv, gg, gmat)


def _gnorm_bwd(dyn, att, cv, gg, name):
    s, a_w = att.shape
    cw = cv.shape[1]
    d = a_w + cw
    tr = _tile(s, 256, 16)
    gmat = _group_matrix()

    def body(dyn_ref, att_ref, cv_ref, gg_ref, gm_ref, datt_ref, dcv_ref, dgg_ref):
        @pl.when(pl.program_id(0) == 0)
        def _():
            dgg_ref[...] = jnp.zeros_like(dgg_ref)

        gm = gm_ref[...]
        for c0 in range(0, d, LANES):
            y = att_ref[:, c0:c0 + LANES] if c0 < a_w else cv_ref[:, c0 - a_w:c0 - a_w + LANES]
            dv = dyn_ref[:, c0:c0 + LANES]
            r = lax.rsqrt(_group_sum(y * y, gm) * (1.0 / HEAD_DIM) + EPS)
            xhat = y * r
            dgg_ref[:, c0:c0 + LANES] += jnp.sum(dv * xhat, axis=0, keepdims=True)
            dxh = dv * gg_ref[:, c0:c0 + LANES]
            proj = _group_sum(dxh * xhat, gm) * (1.0 / HEAD_DIM)
            dy = r * (dxh - xhat * proj)
            if c0 < a_w:
                datt_ref[:, c0:c0 + LANES] = dy.astype(BF16)
            else:
                dcv_ref[:, c0 - a_w:c0 - a_w + LANES] = dy

    return _pc(body, out_shape=(_sds((s, a_w), BF16), _sds((s, cw), F32), _sds((1, d), F32)), grid=(s // tr,),
               in_specs=[pl.BlockSpec((tr, d), lambda i: (i, 0)), pl.BlockSpec((tr, a_w), lambda i: (i, 0)),
                         pl.BlockSpec((tr, cw), lambda i: (i, 0)), _vec_spec(d),
                         pl.BlockSpec((LANES, LANES), lambda i: (0, 0))],
               out_specs=(pl.BlockSpec((tr, a_w), lambda i: (i, 0)), pl.BlockSpec((tr, cw), lambda i: (i, 0)),
                          _vec_spec(d)),
               compiler_params=_params("arbitrary"), name=name)(dyn, att, cv, gg, gmat)


def _adamw_math(w, g, m, v):
    m_new = ADAM_B1 * m + (1.0 - ADAM_B1) * g
    v_new = ADAM_B2 * v + (1.0 - ADAM_B2) * (g * g)
    m_hat = m_new / (1.0 - ADAM_B1 ** ADAM_STEP)
    v_hat = v_new / (1.0 - ADAM_B2 ** ADAM_STEP)
    delta = -ADAM_LR * (m_hat / (jnp.sqrt(v_hat) + ADAM_EPS) + ADAM_WD * w)
    return delta, m_new, v_new


def _row_tile(r, c):
    return _tile(r, max(8, ((1 << 19) // c) // 8 * 8), 8)


def _adamw(w, g, m, v, name):
    r, c = w.shape
    tr = _row_tile(r, c)

    def body(w_ref, g_ref, m_ref, v_ref, d_ref, mo_ref, vo_ref):
        d, mn, vn = _adamw_math(w_ref[...], g_ref[...], m_ref[...], v_ref[...])
        d_ref[...] = d
        mo_ref[...] = mn
        vo_ref[...] = vn

    spec = pl.BlockSpec((tr, c), lambda i: (i, 0))
    return _pc(body, out_shape=(_sds((r, c), F32),) * 3, grid=(r // tr,), in_specs=[spec] * 4,
               out_specs=(spec,) * 3, compiler_params=_params("parallel"), name=name)(w, g, m, v)


def _adamw_halves(w, mine, theirs, m, v, core, name):
    r2, c = w.shape
    r = r2 // 2
    assert mine.shape == (r, c) and theirs.shape == (r, c)
    tr = _row_tile(r, c)
    nb = r // tr

    def body(core_ref, w_ref, a_ref, b_ref, m_ref, v_ref, g_ref, d_ref, mo_ref, vo_ref):
        g = jnp.where(pl.program_id(0) == core_ref[0], a_ref[...], b_ref[...])
        d, mn, vn = _adamw_math(w_ref[...], g, m_ref[...], v_ref[...])
        g_ref[...] = g
        d_ref[...] = d
        mo_ref[...] = mn
        vo_ref[...] = vn

    full = pl.BlockSpec((tr, c), lambda h, i, core_ref: (h * nb + i, 0))
    half = pl.BlockSpec((tr, c), lambda h, i, core_ref: (i, 0))
    grid_spec = pltpu.PrefetchScalarGridSpec(
        num_scalar_prefetch=1, grid=(2, nb), in_specs=[full, half, half, full, full], out_specs=(full,) * 4)
    return _pc(body, out_shape=(_sds((r2, c), F32),) * 4, grid_spec=grid_spec,
               compiler_params=_params("parallel", "parallel"), name=name)(core, w, mine, theirs, m, v)


def _ada_fwd(c16, ada_w, ada_b, name):
    d, n = ada_w.shape
    tn = _tile(n, 768, LANES)

    def body(c_ref, w_ref, b_ref, o_ref):
        cv = c_ref[...]
        sc = (cv * jax.nn.sigmoid(cv)).astype(BF16)
        o_ref[...] = lax.dot_general(sc, w_ref[...].astype(BF16), _NN, preferred_element_type=F32) + b_ref[...]

    return _pc(body, out_shape=_sds((16, n), F32), grid=(n // tn,),
               in_specs=[pl.BlockSpec((16, d), lambda j: (0, 0)), pl.BlockSpec((d, tn), lambda j: (0, j)),
                         pl.BlockSpec((1, tn), lambda j: (0, j))],
               out_specs=pl.BlockSpec((16, tn), lambda j: (0, j)),
               compiler_params=_params("parallel"), name=name)(c16, ada_w, ada_b)


def _ada_update(c16_t, dmod16, w, m, v, name, exchange=None):
    r, c = w.shape
    tr = _row_tile(r, c)

    def body(c_ref, dm_ref, w_ref, m_ref, v_ref, g_ref, d_ref, mo_ref, vo_ref):
        cv = c_ref[...]
        sc = (cv * jax.nn.sigmoid(cv)).astype(BF16)
        g = lax.dot_general(sc, dm_ref[...].astype(BF16), _NN, preferred_element_type=F32)
        d, mn, vn = _adamw_math(w_ref[...], g, m_ref[...], v_ref[...])
        g_ref[...] = g
        d_ref[...] = d
        mo_ref[...] = mn
        vo_ref[...] = vn

    spec = pl.BlockSpec((tr, c), lambda i: (i, 0))
    return _pc(body, exchange, out_shape=(_sds((r, c), F32),) * 4, grid=(r // tr,),
               in_specs=[pl.BlockSpec((tr, 16), lambda i: (i, 0)), pl.BlockSpec((16, c), lambda i: (0, 0)),
                         spec, spec, spec],
               out_specs=(spec,) * 4, compiler_params=_params("parallel"), name=name)(c16_t, dmod16, w, m, v)


def _add_half(dw, recv, core, name):
    _, _, r, w = dw.shape
    tr = _tile(r, 512, 16)

    def body(core_ref, a_ref, b_ref, o_ref):
        o_ref[...] = (a_ref[...].astype(F32) + b_ref[...].astype(F32)).astype(BF16)

    grid_spec = pltpu.PrefetchScalarGridSpec(
        num_scalar_prefetch=1, grid=(N_CHIPS, r // tr),
        in_specs=[pl.BlockSpec((None, None, tr, w), lambda s, i, core_ref: (s, core_ref[0], i, 0)),
                  pl.BlockSpec((None, tr, w), lambda s, i, core_ref: (s, i, 0))],
        out_specs=pl.BlockSpec((None, tr, w), lambda s, i, core_ref: (s, i, 0)))
    return _pc(body, out_shape=_sds((N_CHIPS, r, w), BF16), grid_spec=grid_spec,
               compiler_params=_params("parallel", "parallel"), name=name)(core, dw, recv)


def _sum_chips(own, recv, chip, name):
    _, r, w = own.shape
    tr = _tile(r, 512, 16)

    def body(chip_ref, own_ref, p_ref, o_ref):
        acc = own_ref[...].astype(F32)
        for q in range(N_CHIPS - 1):
            acc = acc + p_ref[q].astype(F32)
        o_ref[...] = acc

    grid_spec = pltpu.PrefetchScalarGridSpec(
        num_scalar_prefetch=1, grid=(r // tr,),
        in_specs=[pl.BlockSpec((None, tr, w), lambda i, chip_ref: (chip_ref[0], i, 0)),
                  pl.BlockSpec((N_CHIPS - 1, tr, w), lambda i, chip_ref: (0, i, 0))],
        out_specs=pl.BlockSpec((tr, w), lambda i, chip_ref: (i, 0)))
    return _pc(body, out_shape=_sds((r, w), F32), grid_spec=grid_spec,
               compiler_params=_params("parallel"), name=name)(chip, own, recv)


def _sum_devices(parts, name):
    nd, r, w = parts.shape

    def body(p_ref, o_ref):
        acc = p_ref[0]
        for q in range(1, nd):
            acc = acc + p_ref[q]
        o_ref[...] = acc

    return _pc(body, out_shape=_sds((r, w), F32), name=name)(parts)


def _place():
    x, y, c = lax.axis_index("x"), lax.axis_index("y"), lax.axis_index("c")
    chips = [(1 - x, y), (x, 1 - y), (1 - x, 1 - y)]
    return x, y, c, chips


def _small_gather_exchange(blk):
    r, w = blk.shape

    def copies(src, dst, send_sems, recv_sems):
        x, y, c, chips = _place()
        me, sibling = (x, y, c), (x, y, 1 - c)

        def rows(px, py, pc):
            return dst[0].at[pl.ds((4 * px + 2 * py + pc) * r, r), :]

        def copy(k, block, to, own=False):
            return _remote(src[0] if own else rows(*block), rows(*block), send_sems, recv_sems, k, to)

        mine = pltpu.make_async_copy(src[0], rows(*me), send_sems.at[7])
        first = [copy(0, me, sibling, own=True)] + [copy(1 + j, me, (*chip, c), own=True) for j, chip in enumerate(chips)]
        passed = [copy(4 + j, (*chip, c), sibling) for j, chip in enumerate(chips)]
        landed = [copy(1 + j, (*chip, c), me) for j, chip in enumerate(chips)]
        from_sibling = [copy(0, sibling, me)] + [copy(4 + j, (*chip, 1 - c), me) for j, chip in enumerate(chips)]
        return mine, first, passed, landed, from_sibling

    def start(src, dst, send_sems, recv_sems):
        mine, first, _, _, _ = copies(src, dst, send_sems, recv_sems)
        mine.start()
        for cp in first:
            cp.start()

    def finish(src, dst, send_sems, recv_sems):
        mine, first, passed, landed, from_sibling = copies(src, dst, send_sems, recv_sems)
        for arrival, onward in zip(landed, passed):
            arrival.wait_recv()
            onward.start()
        for cp in from_sibling:
            cp.wait_recv()
        for cp in first + passed:
            cp.wait_send()
        mine.wait()

    return _Exchange([blk], [_sds((N_DEV * r, w), blk.dtype)], 8, start, finish)


def _remote(src, dst, send_sems, recv_sems, k, to):
    return pltpu.make_async_remote_copy(src_ref=src, dst_ref=dst, send_sem=send_sems.at[k], recv_sem=recv_sems.at[k],
                                        device_id=to, device_id_type=MESH)


def _exchange_of(inputs, out_shapes, n_sems, copies, aliases=None):
    def start(src, dst, send_sems, recv_sems):
        for cp in copies(src, dst, send_sems, recv_sems)[0]:
            cp.start()

    def finish(src, dst, send_sems, recv_sems):
        sends, arrivals = copies(src, dst, send_sems, recv_sems)
        for cp in arrivals:
            cp.wait_recv()
        for cp in sends:
            cp.wait_send()

    return _Exchange(inputs, out_shapes, n_sems, start, finish, aliases)


def _run_exchange(ex, name):
    n_in, n_out = len(ex.inputs), len(ex.out_shapes)

    def body(*refs):
        src, dst = refs[:n_in], refs[n_in:n_in + n_out]
        send_sems, recv_sems = refs[n_in + n_out:]
        ex.start(src, dst, send_sems, recv_sems)
        ex.finish(src, dst, send_sems, recv_sems)

    ex.set_results(pl.pallas_call(
        body, out_shape=tuple(ex.out_shapes), in_specs=[_ANY] * n_in, out_specs=(_ANY,) * n_out,
        scratch_shapes=[pltpu.SemaphoreType.DMA((ex.n_sems,)), pltpu.SemaphoreType.DMA((ex.n_sems,))],
        input_output_aliases=ex.aliases, name=name)(*ex.inputs))


def _gather_ici_exchange(shards):
    n = len(shards)

    def copies(own, out, send_sems, recv_sems):
        x, y, c, chips = _place()
        my_chip = 2 * x + y
        sends, arrivals = [], []
        for i in range(n):
            for j, chip in enumerate(chips):
                to = (*chip, c)
                sends.append(_remote(own[i].at[c], out[i].at[my_chip, c], send_sems, recv_sems, 4 * i + j, to))
                arrivals.append(_remote(own[i].at[c], out[i].at[2 * chip[0] + chip[1], c], send_sems, recv_sems, 4 * i + j, to))
            whole = _remote(own[i], out[i].at[my_chip], send_sems, recv_sems, 4 * i + 3, (x, y, 1 - c))
            sends.append(whole)
            arrivals.append(whole)
        return sends, arrivals

    return _exchange_of(shards, [_sds((N_CHIPS,) + s.shape, s.dtype) for s in shards], 4 * n, copies)


def _gather_pass_exchange(gathered):
    n = len(gathered)

    def copies(src, dst, send_sems, recv_sems):
        x, y, c, chips = _place()
        sends, arrivals = [], []
        for i in range(n):
            for j, chip in enumerate(chips):
                idx = 2 * chip[0] + chip[1]
                sends.append(_remote(src[i].at[idx, c], dst[i].at[idx, c], send_sems, recv_sems, 3 * i + j, (x, y, 1 - c)))
                arrivals.append(_remote(src[i].at[idx, c], dst[i].at[idx, 1 - c], send_sems, recv_sems, 3 * i + j, (x, y, 1 - c)))
        return sends, arrivals

    return _exchange_of(gathered, [_sds(g.shape, g.dtype) for g in gathered], 3 * n, copies,
                        aliases={i: i for i in range(n)})


def _reduce_sibling_exchange(grads):
    n = len(grads)

    def copies(src, dst, send_sems, recv_sems):
        x, y, c, _ = _place()
        both = [_remote(src[i].at[s, 1 - c], dst[i].at[s], send_sems, recv_sems, N_CHIPS * i + s, (x, y, 1 - c))
                for i in range(n) for s in range(N_CHIPS)]
        return both, both

    return _exchange_of(grads, [_sds((N_CHIPS,) + g.shape[2:], g.dtype) for g in grads], N_CHIPS * n, copies)


def _reduce_chips_exchange(parts):
    n = len(parts)

    def copies(src, dst, send_sems, recv_sems):
        x, y, c, chips = _place()
        both = [_remote(src[i].at[2 * chip[0] + chip[1]], dst[i].at[j], send_sems, recv_sems, 3 * i + j, (*chip, c))
                for i in range(n) for j, chip in enumerate(chips)]
        return both, both

    return _exchange_of(parts, [_sds((N_CHIPS - 1,) + p.shape[1:], p.dtype) for p in parts], 3 * n, copies)


def _share_exchange(halves):
    n = len(halves)

    def copies(src, dst, send_sems, recv_sems):
        x, y, c, _ = _place()
        both = [_remote(src[i], dst[i], send_sems, recv_sems, i, (x, y, 1 - c)) for i in range(n)]
        return both, both

    return _exchange_of(halves, [_sds(h.shape, h.dtype) for h in halves], n, copies)


HEAD_ROWS = 16


class _WeightTraffic:
    def __init__(self, shards, core, chip):
        self.shards, self.core, self.chip = shards, core, chip
        self.gather, self.grads, self.reduce, self.chip_sums, self.half_sums, self.shared = {}, {}, {}, {}, {}, {}

    def gather_ici(self, grp):
        self.gather[grp] = _gather_ici_exchange(self.shards[grp])
        return self.gather[grp]

    def gather_pass(self, grp):
        self.gather[grp] = _gather_pass_exchange(self.gather[grp].results)
        return self.gather[grp]

    def weights(self, grp):
        return [g.reshape(-1, g.shape[-1]) for g in self.gather[grp].results]

    def reduce_sibling(self, grp, grads):
        self.grads[grp] = [g.reshape(N_CHIPS, 2, g.shape[0] // (2 * N_CHIPS), g.shape[1]) for g in grads]
        self.reduce[grp] = _reduce_sibling_exchange(self.grads[grp])
        return self.reduce[grp]

    def add_halves(self, grp):
        self.chip_sums[grp] = [_add_half(g, r, self.core, "add_half_%s%d" % (grp, i))
                               for i, (g, r) in enumerate(zip(self.grads[grp], self.reduce[grp].results))]

    def reduce_chips(self, grp):
        self.reduce[grp] = _reduce_chips_exchange(self.chip_sums[grp])
        return self.reduce[grp]

    def sum_chips(self, grp):
        self.half_sums[grp] = [_sum_chips(o, p, self.chip, "sum_chips_%s%d" % (grp, i))
                               for i, (o, p) in enumerate(zip(self.chip_sums[grp], self.reduce[grp].results))]

    def share(self, grp):
        self.shared[grp] = _share_exchange(self.half_sums[grp])
        return self.shared[grp]

    def totals(self, grp):
        return list(zip(self.half_sums[grp], self.shared[grp].results))


def _ffn_fwd(x, norm_g, shift, scale, gate, wg_t, wu_t, wd, tag, up_exchange=None, down_exchange=None):
    h = _norm_mod_fwd(x, norm_g, shift, scale, tag + "_norm_fwd")
    a, u, hid = _ffn_up(h, wg_t, wu_t, tag + "_up", exchange=up_exchange)
    wd = wd() if callable(wd) else wd
    x_out, f = _mm(hid, wd, "nn", F32, tag + "_down", res=x, gate=gate, aux_dtype=BF16,
                   exchange=down_exchange() if down_exchange else None)
    return x_out, (h, a, u, hid, f)


def _ffn_bwd(dx_out, df, x, saved, norm_g, scale, wg_t, wu_t, wd, tag, traffic, below=None, dact_exchange=None,
             dw_exchange=None, finish_reduction=False):
    h, a, u, hid, _ = saved
    f_below, gate_below = below if below else (None, None)
    da, du = _ffn_dact(df, wd, a, u, tag + "_dact", exchange=dact_exchange)
    dwd = _mm(hid, df, "tn", BF16, tag + "_dwd", exchange=dw_exchange() if dw_exchange else None)
    if not finish_reduction:
        dwg_t = _mm(da, h, "tn", BF16, tag + "_dwg")
        dwu_t = _mm(du, h, "tn", BF16, tag + "_dwu")
        dx, dshift, dscale, dnorm_g, *gated = _norm_mod_bwd(
            ([da, du], [wg_t, wu_t]), x, norm_g, scale, dx_out, tag + "_dh_norm_bwd", f=f_below, gate=gate_below,
            exchange=traffic.reduce_sibling(tag, [dwg_t, dwu_t, dwd]))
        traffic.add_halves(tag)
        return dx, (dshift, dscale, dnorm_g), gated
    kd, kg, ku = tag + "_wd", tag + "_wg", tag + "_wu"
    dwg_t = _mm(da, h, "tn", BF16, tag + "_dwg", exchange=traffic.reduce_sibling(kd, [dwd]))
    traffic.add_halves(kd)
    dwu_t = _mm(du, h, "tn", BF16, tag + "_dwu",
                exchange=_join(traffic.reduce_chips(kd), traffic.reduce_sibling(kg, [dwg_t])))
    traffic.add_halves(kg)
    half = x.shape[0] // 2
    top = _norm_mod_bwd(([da, du], [wg_t, wu_t]), x, norm_g, scale, dx_out, tag + "_dh_norm_bwd_top", f=f_below,
                        gate=gate_below, rows=(0, half),
                        exchange=_join(traffic.reduce_chips(kg), traffic.reduce_sibling(ku, [dwu_t])))
    traffic.add_halves(ku)
    traffic.sum_chips(kd)
    traffic.sum_chips(kg)
    bottom = _norm_mod_bwd(([da, du], [wg_t, wu_t]), x, norm_g, scale, dx_out, tag + "_dh_norm_bwd_bottom", f=f_below,
                           gate=gate_below, rows=(half, half),
                           exchange=_join(traffic.reduce_chips(ku), traffic.share(kd), traffic.share(kg)))
    traffic.sum_chips(ku)
    dx, dshift, dscale, dnorm_g, *gated = [jnp.concatenate([a, b]) if a.shape[0] == half else a + b
                                           for a, b in zip(top, bottom)]
    return dx, (dshift, dscale, dnorm_g), gated


def _layer_step(x, target, mod, gains, forget_bias, conv_w, traffic, att_w, in_shard, in_rows):
    sh1, sc1, g1, sh2, sc2, g2, sh3, sc3, g3 = mod
    norm1_g, norm2_g, norm3_g, final_g, group_g = gains
    s, d = x.shape
    n_heads = att_w // HEAD_DIM
    npair = n_heads // 2
    gate1, gate3 = 0.5 * g1, 0.5 * g3

    def split_w_in(w_in_pad):
        w_in_t = w_in_pad.reshape(N_CHIPS, in_rows, d)[:, :in_shard].reshape(N_CHIPS * in_shard, d)
        return (w_in_t[:3 * att_w], _pad_rows(w_in_t[3 * att_w:3 * att_w + n_heads], LANES), w_in_t[3 * att_w + n_heads:])

    wg1_t, wu1_t = traffic.weights("ffn1_gu")

    def wd1_ready():
        _run_exchange(traffic.gather_pass("ffn1_d"), "gather_ffn1_down_pass")
        return traffic.weights("ffn1_d")[0]

    x1, saved1 = _ffn_fwd(x, norm1_g, sh1, sc1, gate1, wg1_t, wu1_t, wd1_ready, "ffn1",
                          up_exchange=_join(traffic.gather_ici("ffn1_d"), traffic.gather_ici("mix_in")),
                          down_exchange=lambda: _join(traffic.gather_pass("mix_in"), traffic.gather_ici("mix_out")))
    wd1 = traffic.weights("ffn1_d")[0]
    wqkv_t, wf_t, wbcx_t = split_w_in(traffic.weights("mix_in")[0])

    h2 = _norm_mod_fwd(x1, norm2_g, sh2, sc2, "mix_norm_fwd")
    qkv = _mm(h2, wqkv_t, "nt", BF16, "mix_proj_qkv", exchange=traffic.gather_pass("mix_out"))
    w_out = traffic.weights("mix_out")[0]
    bcx = _mm(h2, wbcx_t, "nt", F32, "mix_proj_bcx")
    flog = _mm(h2, wf_t, "nt", F32, "mix_proj_f")
    flog_t = jnp.pad(flog[:, :n_heads].T, ((0, HEAD_ROWS - n_heads), (0, 0)))
    bias_col = jnp.pad(forget_bias, (0, HEAD_ROWS - n_heads))[:, None]
    f_pieces = _forget_fwd(flog_t, bias_col, "forget_fwd")
    qa, ka, va = _attn_prep(qkv, f_pieces, "attn_prep")
    att, lse = _attn_fwd(qa, ka, va, "attn_fwd", exchange=traffic.gather_ici("ffn2"))
    cv = _conv_fwd(bcx, conv_w, "conv_fwd")
    yn = _gnorm_fwd(att, cv, group_g, "gnorm_fwd")
    x2, mix = _mm(yn, w_out, "nn", F32, "mix_out", res=x1, gate=g2, aux_dtype=BF16, exchange=traffic.gather_pass("ffn2"))
    wg2_t, wu2_t, wd2 = traffic.weights("ffn2")

    h3 = _norm_mod_fwd(x2, norm3_g, sh3, sc3, "ffn2_norm_fwd")
    a3, u3, hid3 = _ffn_up(h3, wg2_t, wu2_t, "ffn2_up")
    saved3 = (h3, a3, u3, hid3, None)
    dx3, loss_row, dfinal_g, df2, dgate3 = _down_final_loss(hid3, wd2, x2, gate3, final_g, target, "ffn2_down_loss")

    dx2, (dsh3, dsc3, dnorm3_g), (dmix, dg2) = _ffn_bwd(
        dx3, df2, x2, saved3, norm3_g, sc3, wg2_t, wu2_t, wd2, "ffn2", traffic, below=(mix, g2))
    dyn = _mm(dmix, w_out, "nt", F32, "mix_out_dyn")
    dw_out = _mm(yn, dmix, "tn", BF16, "mix_out_dw")
    datt, dcv, dgroup_g = _gnorm_bwd(dyn, att, cv, group_g, "gnorm_bwd")
    db, dc, dxc, dconv_w = _conv_bwd(dcv, bcx, conv_w, "conv_bwd")
    dbcx = jnp.concatenate([db, dc, dxc], axis=1)
    dq, dk, dv, qx, kx = _attn_bwd(qa, ka, va, datt, att, lse, "attn_bwd", exchange=traffic.reduce_chips("ffn2"))
    traffic.sum_chips("ffn2")
    dqkv = jnp.concatenate([dq.astype(BF16), dk, dv], axis=1)
    df_t = _decay_grads(qx, kx, "decay_grads")[:, :HEAD_ROWS].T
    dflog_t, dbias_col = _forget_bwd(df_t, flog_t, bias_col, "forget_bwd")
    dflog = jnp.pad(dflog_t[:n_heads].T, ((0, 0), (0, LANES - n_heads))).astype(BF16)
    dwqkv_t = _mm(dqkv, h2, "tn", BF16, "mix_dw_qkv", exchange=traffic.share("ffn2"))
    dwbcx_t = _mm(dbcx, h2, "tn", BF16, "mix_dw_bcx")
    dwf_t = _mm(dflog, h2, "tn", BF16, "mix_dw_f")
    dw_in_t = jnp.concatenate([dwqkv_t, dwf_t[:n_heads], dwbcx_t], axis=0).reshape(N_CHIPS, in_shard, d)
    dw_in_t = jnp.pad(dw_in_t, ((0, 0), (0, in_rows - in_shard), (0, 0))).reshape(N_CHIPS * in_rows, d)
    dx1, dsh2, dsc2, dnorm2_g, df1, dgate1 = _norm_mod_bwd(
        ([dqkv, dbcx, dflog], [wqkv_t, wbcx_t, wf_t]), x1, norm2_g, sc2, dx2, "mix_dh_norm_bwd", f=saved1[4], gate=gate1,
        exchange=traffic.reduce_sibling("mix", [dw_in_t, dw_out]))
    traffic.add_halves("mix")

    def share_mix():
        traffic.sum_chips("mix")
        return traffic.share("mix")

    dx, (dsh1, dsc1, dnorm1_g), _ = _ffn_bwd(
        dx1, df1, x, saved1, norm1_g, sc1, wg1_t, wu1_t, wd1, "ffn1", traffic,
        dact_exchange=traffic.reduce_chips("mix"), dw_exchange=share_mix, finish_reduction=True)

    dmod = [dsh1, dsc1, 0.5 * dgate1, dsh2, dsc2, dg2, dsh3, dsc3, 0.5 * dgate3]
    dgains = [dnorm1_g, dnorm2_g, dnorm3_g, dfinal_g, dgroup_g]
    dbias = dbias_col[:n_heads, 0]
    return dx, loss_row, dmod, dgains, dbias, dconv_w


SMALL_ROWS = 24
ROW_GAINS, ROW_LOSS, ROW_FORGET, ROW_CONV, ROW_MOD = 0, 5, 6, 7, 10
PROW_ADA_B, PROW_GAINS, PROW_FORGET, PROW_CONV = 0, 9, 14, 15


def _round_up(n, m):
    return -(-n // m) * m


def _pad_rows(a, rows):
    return jnp.pad(a, ((0, rows - a.shape[0]), (0, 0)))


def _halves(a):
    return a.reshape(2, a.shape[0] // 2, a.shape[1])


def _rows_at(a, r0, total, width):
    return jnp.pad(a, ((r0, total - r0 - a.shape[0]), (0, width - a.shape[1])))


def kernel(x, c, ada_w, ada_b, norm1_g, ffn1_w_gate, ffn1_w_up, ffn1_w_down, norm2_g, w_in, forget_bias, conv_w, group_norm_g, w_out, norm3_g, ffn2_w_gate, ffn2_w_up, ffn2_w_down, final_g, loss_target, m_ada_w, m_ada_b, m_norm1_g, m_ffn1_w_gate, m_ffn1_w_up, m_ffn1_w_down, m_norm2_g, m_w_in, m_forget_bias, m_conv_w, m_group_norm_g, m_w_out, m_norm3_g, m_ffn2_w_gate, m_ffn2_w_up, m_ffn2_w_down, m_final_g, v_ada_w, v_ada_b, v_norm1_g, v_ffn1_w_gate, v_ffn1_w_up, v_ffn1_w_down, v_norm2_g, v_w_in, v_forget_bias, v_conv_w, v_group_norm_g, v_w_out, v_norm3_g, v_ffn2_w_gate, v_ffn2_w_up, v_ffn2_w_down, v_final_g):
    xi, yi, ci = lax.axis_index("x"), lax.axis_index("y"), lax.axis_index("c")
    chip = 2 * xi + yi
    dev = 4 * xi + 2 * yi + ci
    _, s, d = x.shape
    att_w = d // 2
    conv_width = d - att_w
    n_heads = att_w // HEAD_DIM
    in_shard = w_in.shape[1]
    in_rows = _round_up(in_shard, 32)
    cs = conv_w.shape[1]
    mod_shard = ada_w.shape[1]
    assert N_MOD * d == N_CHIPS * mod_shard and conv_width == N_CHIPS * cs and n_heads % 2 == 0

    def t_bf(w):
        return w.T.astype(BF16)

    shards = {"ffn1_gu": [_halves(t_bf(ffn1_w_gate)), _halves(t_bf(ffn1_w_up))], "ffn1_d": [_halves(ffn1_w_down.astype(BF16))],
              "mix_in": [_halves(_pad_rows(t_bf(w_in), in_rows))], "mix_out": [_halves(w_out.astype(BF16))],
              "ffn2": [_halves(t_bf(ffn2_w_gate)), _halves(t_bf(ffn2_w_up)), _halves(ffn2_w_down.astype(BF16))]}
    core = ci.astype(jnp.int32).reshape(1)
    chip_arr = chip.astype(jnp.int32).reshape(1)
    traffic = _WeightTraffic(shards, core, chip_arr)

    cond = _small_gather_exchange(_rows_at(c, 0, 8, d) + _rows_at(conv_w, 1, 8, d))
    _run_exchange(_join(traffic.gather_ici("ffn1_gu"), cond), "gather_ffn1_ici")
    got0 = cond.results[0].reshape(N_DEV, 8, d)
    c16 = _pad_rows(got0[:, 0, :], 16)
    conv_full = got0[0::2, 1:1 + CONV_K, :cs].transpose(1, 0, 2).reshape(CONV_K, conv_width)

    ada_b_mine = lax.dynamic_slice(ada_b, (chip * mod_shard,), (mod_shard,))[None, :]
    mods = _small_gather_exchange(_ada_fwd(c16, ada_w, ada_b_mine, "ada_fwd"))
    _run_exchange(_join(traffic.gather_pass("ffn1_gu"), mods), "gather_ffn1_pass")
    got1 = mods.results[0].reshape(N_DEV, 16, mod_shard)
    mod_mine = lax.dynamic_index_in_dim(got1[0::2], dev, axis=1, keepdims=False).reshape(N_MOD, d)
    mod = [mod_mine[i:i + 1] for i in range(N_MOD)]

    gains = [g[None, :] for g in (norm1_g, norm2_g, norm3_g, final_g, group_norm_g)]
    dx, loss_row, dmod, dgains, dbias, dconv_w = _layer_step(
        x[0], loss_target[0], mod, gains, forget_bias, conv_full, traffic, att_w, in_shard, in_rows)

    pack = sum(_rows_at(g, ROW_GAINS + i, SMALL_ROWS, d) for i, g in enumerate(dgains))
    pack += _rows_at(loss_row, ROW_LOSS, SMALL_ROWS, d) + _rows_at(dbias[None, :], ROW_FORGET, SMALL_ROWS, d)
    pack += _rows_at(dconv_w, ROW_CONV, SMALL_ROWS, d)
    pack += sum(_rows_at(g, ROW_MOD + i, SMALL_ROWS, d) for i, g in enumerate(dmod))
    small = _small_gather_exchange(pack)
    _run_exchange(_join(traffic.share("ffn1_wu"), small), "gather_small_grads")
    got2 = small.results[0].reshape(N_DEV, SMALL_ROWS, d)
    tot = _sum_devices(got2, "sum_small_grads")
    loss = tot[ROW_LOSS, 0]
    grad_ada_b = tot[ROW_MOD:ROW_MOD + N_MOD].reshape(N_MOD * d)
    grad_conv = lax.dynamic_slice(tot[ROW_CONV:ROW_CONV + CONV_K], (0, chip * cs), (CONV_K, cs))
    dmod_all = got2[:, ROW_MOD:ROW_MOD + N_MOD, :].reshape(N_DEV, N_MOD * d)
    dmod16 = _pad_rows(lax.dynamic_slice(dmod_all, (0, chip * mod_shard), (N_DEV, mod_shard)), 16)

    out = {"ada_w": tuple(_ada_update(c16.T, dmod16, ada_w, m_ada_w, v_ada_w, "adamw_ada_w"))}
    totals = (traffic.totals("ffn1_wg") + traffic.totals("ffn1_wu") + traffic.totals("ffn1_wd")
              + traffic.totals("mix") + traffic.totals("ffn2"))

    names = ("ffn1_w_gate", "ffn1_w_up", "ffn1_w_down", "w_in", "w_out", "ffn2_w_gate", "ffn2_w_up", "ffn2_w_down")
    transposed = ("ffn1_w_gate", "ffn1_w_up", "w_in", "ffn2_w_gate", "ffn2_w_up")
    params = {"ffn1_w_gate": (ffn1_w_gate, m_ffn1_w_gate, v_ffn1_w_gate), "ffn1_w_up": (ffn1_w_up, m_ffn1_w_up, v_ffn1_w_up),
              "ffn1_w_down": (ffn1_w_down, m_ffn1_w_down, v_ffn1_w_down), "w_in": (w_in, m_w_in, v_w_in),
              "w_out": (w_out, m_w_out, v_w_out), "ffn2_w_gate": (ffn2_w_gate, m_ffn2_w_gate, v_ffn2_w_gate),
              "ffn2_w_up": (ffn2_w_up, m_ffn2_w_up, v_ffn2_w_up), "ffn2_w_down": (ffn2_w_down, m_ffn2_w_down, v_ffn2_w_down)}
    for name_, (mine, theirs) in zip(names, totals):
        w, m, v = params[name_]
        if name_ in transposed:
            w, m, v = w.T, m.T, v.T
        if name_ == "w_in":
            both = jnp.where(ci == 0, jnp.concatenate([mine, theirs]), jnp.concatenate([theirs, mine]))[:in_shard]
            res = (both,) + tuple(_adamw(w, both, m, v, "adamw_" + name_))
        else:
            res = _adamw_halves(w, mine, theirs, m, v, core, "adamw_" + name_)
        out[name_] = tuple(r.T for r in res) if name_ in transposed else tuple(res)

    def small_pack(ada_b_, gains_, forget_, conv_):
        p = _rows_at(ada_b_.reshape(N_MOD, d), PROW_ADA_B, SMALL_ROWS, d)
        p += sum(_rows_at(g[None, :], PROW_GAINS + i, SMALL_ROWS, d) for i, g in enumerate(gains_))
        p += _rows_at(forget_[None, :], PROW_FORGET, SMALL_ROWS, d) + _rows_at(conv_, PROW_CONV, SMALL_ROWS, d)
        return p

    g_gains = [tot[ROW_GAINS + i] for i in range(5)]
    g_forget = tot[ROW_FORGET, :n_heads]
    sw = small_pack(ada_b, (norm1_g, norm2_g, norm3_g, final_g, group_norm_g), forget_bias, conv_w)
    sm = small_pack(m_ada_b, (m_norm1_g, m_norm2_g, m_norm3_g, m_final_g, m_group_norm_g), m_forget_bias, m_conv_w)
    sv = small_pack(v_ada_b, (v_norm1_g, v_norm2_g, v_norm3_g, v_final_g, v_group_norm_g), v_forget_bias, v_conv_w)
    sg = small_pack(grad_ada_b, g_gains, g_forget, grad_conv)
    small = (sg,) + tuple(_adamw(sw, sg, sm, sv, "adamw_small"))

    def unpack(p):
        r = {"ada_b": p[PROW_ADA_B:PROW_ADA_B + N_MOD].reshape(N_MOD * d), "forget_bias": p[PROW_FORGET, :n_heads],
             "conv_w": p[PROW_CONV:PROW_CONV + CONV_K, :cs]}
        for i, nm in enumerate(("norm1_g", "norm2_g", "norm3_g", "final_g", "group_norm_g")):
            r[nm] = p[PROW_GAINS + i]
        return r

    small = [unpack(p) for p in small]
    order = ("ada_w", "ada_b", "norm1_g", "ffn1_w_gate", "ffn1_w_up", "ffn1_w_down", "norm2_g", "w_in", "forget_bias",
             "conv_w", "group_norm_g", "w_out", "norm3_g", "ffn2_w_gate", "ffn2_w_up", "ffn2_w_down", "final_g")
    result = [loss, dx[None]]
    for k in range(4):
        result += [out[nm][k] if nm in out else small[k][nm] for nm in order]
    return tuple(result)
```

```python
import functools
import math

import jax
import jax.numpy as jnp
from jax import lax
from jax.experimental import pallas as pl
from jax.experimental.pallas import tpu as pltpu

F32 = jnp.float32
BF16 = jnp.bfloat16

HEAD_DIM = 64
CONV_K = 3
N_MOD = 9
EPS = 1e-6
ADAM_LR = 0.001
ADAM_B1 = 0.9
ADAM_B2 = 0.999
ADAM_EPS = 1e-08
ADAM_WD = 0.01
ADAM_STEP = 10

LANES = 128
N_CHIPS = 4
N_DEV = 8
VMEM_LIMIT_BYTES = 56 * 1024 * 1024
MAX_CONTRACTION = 4096
NEG_BIG = -1e30
MESH = pl.DeviceIdType.MESH

_NT = (((1,), (1,)), ((), ()))
_NN = (((1,), (0,)), ((), ()))
_TN = (((0,), (0,)), ((), ()))


def _params(*sem):
    return pltpu.CompilerParams(dimension_semantics=sem, vmem_limit_bytes=VMEM_LIMIT_BYTES)


class _Exchange:
    def __init__(self, inputs, out_shapes, n_sems, start, finish, aliases=None):
        self.inputs, self.out_shapes, self.n_sems = list(inputs), list(out_shapes), n_sems
        self.start, self.finish, self.aliases = start, finish, dict(aliases or {})
        self.results = None

    def set_results(self, results):
        self.results = list(results)


class _SemaphoreWindow:
    def __init__(self, sems, base):
        self.sems, self.base = sems, base
        self.at = self

    def __getitem__(self, k):
        return self.sems.at[self.base + k]


class _JoinedExchange(_Exchange):
    def __init__(self, parts):
        self.parts = parts
        aliases, i0, o0 = {}, 0, 0
        for p in parts:
            aliases.update({i0 + a: o0 + b for a, b in p.aliases.items()})
            i0, o0 = i0 + len(p.inputs), o0 + len(p.out_shapes)

        def each(method, src, dst, send_sems, recv_sems):
            i0 = o0 = s0 = 0
            for p in parts:
                i1, o1 = i0 + len(p.inputs), o0 + len(p.out_shapes)
                getattr(p, method)(src[i0:i1], dst[o0:o1], _SemaphoreWindow(send_sems, s0), _SemaphoreWindow(recv_sems, s0))
                i0, o0, s0 = i1, o1, s0 + p.n_sems

        super().__init__([a for p in parts for a in p.inputs], [o for p in parts for o in p.out_shapes],
                         sum(p.n_sems for p in parts), functools.partial(each, "start"), functools.partial(each, "finish"),
                         aliases)

    def set_results(self, results):
        o0 = 0
        for p in self.parts:
            p.set_results(results[o0:o0 + len(p.out_shapes)])
            o0 += len(p.out_shapes)


def _join(*parts):
    return parts[0] if len(parts) == 1 else _JoinedExchange(list(parts))


def _pc(body, exchange=None, **kw):
    if exchange is None:
        return pl.pallas_call(body, **kw)
    grid = kw["grid"]
    single = not isinstance(kw["out_shape"], (tuple, list))
    out_shape = [kw["out_shape"]] if single else list(kw["out_shape"])
    out_specs = [kw["out_specs"]] if single else list(kw["out_specs"])
    in_specs = list(kw["in_specs"])
    scratch = list(kw.get("scratch_shapes", ()))
    n_in, n_out, n_scr = len(in_specs), len(out_shape), len(scratch)
    n_xi, n_xo = len(exchange.inputs), len(exchange.out_shapes)

    def wrapped(*refs):
        pos = [n_in, n_in + n_xi, n_in + n_xi + n_out, n_in + n_xi + n_out + n_xo]
        ins, x_in, outs, x_out = refs[:pos[0]], refs[pos[0]:pos[1]], refs[pos[1]:pos[2]], refs[pos[2]:pos[3]]
        scr = refs[pos[3]:pos[3] + n_scr]
        send_sems, recv_sems = refs[pos[3] + n_scr:]
        ids = [pl.program_id(a) for a in range(len(grid))]
        first = functools.reduce(jnp.logical_and, [i == 0 for i in ids])
        last = functools.reduce(jnp.logical_and, [i == g - 1 for i, g in zip(ids, grid)])

        @pl.when(first)
        def _():
            exchange.start(x_in, x_out, send_sems, recv_sems)

        body(*ins, *outs, *scr)

        @pl.when(last)
        def _():
            exchange.finish(x_in, x_out, send_sems, recv_sems)

    call = pl.pallas_call(
        wrapped, out_shape=tuple(out_shape) + tuple(exchange.out_shapes), grid=grid,
        in_specs=in_specs + [_ANY] * n_xi, out_specs=tuple(out_specs) + (_ANY,) * n_xo,
        scratch_shapes=scratch + [pltpu.SemaphoreType.DMA((exchange.n_sems,)), pltpu.SemaphoreType.DMA((exchange.n_sems,))],
        input_output_aliases={n_in + a: n_out + b for a, b in exchange.aliases.items()},
        compiler_params=_params(*(["arbitrary"] * len(grid))), name=kw["name"])

    def run(*args):
        res = call(*args, *exchange.inputs)
        exchange.set_results(res[n_out:])
        return res[0] if single else tuple(res[:n_out])

    return run


_ANY = pl.BlockSpec(memory_space=pl.ANY)


def _tile(n, pref, mult):
    best = None
    t = mult
    while t <= min(n, pref):
        if n % t == 0:
            best = t
        t += mult
    return n if best is None else best


def _sds(shape, dtype):
    return jax.ShapeDtypeStruct(shape, dtype)


def _vec_spec(d):
    return pl.BlockSpec((1, d), lambda *_: (0, 0))


def _norm_mod_fwd(x, g, shift, scale, name):
    s, d = x.shape
    tr = _tile(s, 512, 16)

    def body(x_ref, g_ref, sh_ref, sc_ref, h_ref):
        xv = x_ref[...]
        rstd = lax.rsqrt(jnp.mean(xv * xv, axis=-1, keepdims=True) + EPS)
        n = xv * rstd * g_ref[...]
        h_ref[...] = (n * (1.0 + sc_ref[...]) + sh_ref[...]).astype(BF16)

    row = pl.BlockSpec((tr, d), lambda i: (i, 0))
    return _pc(body, out_shape=_sds((s, d), BF16), grid=(s // tr,),
               in_specs=[row, _vec_spec(d), _vec_spec(d), _vec_spec(d)], out_specs=row,
               compiler_params=_params("parallel"), name=name)(x, g, shift, scale)


def _through_gate(dx, f_ref, gate_ref, df_ref, dgate_ref):
    df_ref[...] = (dx * gate_ref[...]).astype(BF16)
    dgate_ref[...] += jnp.sum(dx * f_ref[...].astype(F32), axis=0, keepdims=True)


def _norm_mod_bwd(dh, x, g, scale, dres, name, f=None, gate=None, rows=None, exchange=None):
    d = x.shape[1]
    first_row, s = rows if rows else (0, x.shape[0])
    gated = f is not None
    terms = list(zip(*dh)) if isinstance(dh, tuple) else None
    tr = _tile(s, 256, 16)
    b0 = first_row // tr
    assert first_row % tr == 0
    n_lead = 2 * len(terms) if terms else 1

    def body(*refs):
        lead, (x_ref, g_ref, sc_ref, dres_ref), rest = refs[:n_lead], refs[n_lead:n_lead + 4], refs[n_lead + 4:]
        f_ref, gate_ref = rest[:2] if gated else (None, None)
        dx_ref, dsh_ref, dsc_ref, dg_ref = rest[2:6] if gated else rest[:4]
        df_ref, dgate_ref = rest[6:8] if gated else (None, None)

        @pl.when(pl.program_id(0) == 0)
        def _():
            for ref in (dsh_ref, dsc_ref, dg_ref) + ((dgate_ref,) if gated else ()):
                ref[...] = jnp.zeros_like(ref)

        if terms:
            dhv = lax.dot_general(lead[0][...], lead[1][...], _NN, preferred_element_type=F32)
            for p in range(1, len(terms)):
                dhv += lax.dot_general(lead[2 * p][...], lead[2 * p + 1][...], _NN, preferred_element_type=F32)
        else:
            dhv = lead[0][...]
        xv = x_ref[...]
        gv = g_ref[...]
        rstd = lax.rsqrt(jnp.mean(xv * xv, axis=-1, keepdims=True) + EPS)
        xhat = xv * rstd
        dn = dhv * (1.0 + sc_ref[...])
        dsh_ref[...] += jnp.sum(dhv, axis=0, keepdims=True)
        dsc_ref[...] += jnp.sum(dhv * (xhat * gv), axis=0, keepdims=True)
        dg_ref[...] += jnp.sum(dn * xhat, axis=0, keepdims=True)
        dxh = dn * gv
        proj = jnp.mean(dxh * xhat, axis=-1, keepdims=True)
        dx = dres_ref[...] + rstd * (dxh - xhat * proj)
        dx_ref[...] = dx
        if gated:
            _through_gate(dx, f_ref, gate_ref, df_ref, dgate_ref)

    row = pl.BlockSpec((tr, d), lambda i: (b0 + i, 0))
    out_row = pl.BlockSpec((tr, d), lambda i: (i, 0))
    vec = _vec_spec(d)
    if terms:
        in_specs, args = [], []
        for l, r in terms:
            assert l.shape[1] == r.shape[0] <= MAX_CONTRACTION and r.shape[1] == d
            in_specs += [pl.BlockSpec((tr, l.shape[1]), lambda i: (b0 + i, 0)), pl.BlockSpec(r.shape, lambda i: (0, 0))]
            args += [l, r]
    else:
        in_specs, args = [row], [dh]
    in_specs += [row, vec, vec, row]
    args += [x, g, scale, dres]
    out_shape = [_sds((s, d), F32), _sds((1, d), F32), _sds((1, d), F32), _sds((1, d), F32)]
    out_specs = [out_row, vec, vec, vec]
    if gated:
        out_shape += [_sds((s, d), BF16), _sds((1, d), F32)]
        out_specs += [out_row, vec]
        in_specs += [row, vec]
        args += [f, gate]
    return _pc(body, exchange, out_shape=tuple(out_shape), grid=(s // tr,), in_specs=in_specs,
               out_specs=tuple(out_specs), compiler_params=_params("arbitrary"), name=name)(*args)


def _down_final_loss(hid, wd, res, gate, g, target, name):
    s, d = res.shape
    k = hid.shape[1]
    assert k <= MAX_CONTRACTION
    tr = _tile(s, 256, 16)
    nsteps = s // tr

    def body(hid_ref, wd_ref, res_ref, gate_ref, g_ref, t_ref, dx_ref, loss_ref, dg_ref, df_ref, dgate_ref):
        i = pl.program_id(0)

        @pl.when(i == 0)
        def _():
            loss_ref[...] = jnp.zeros_like(loss_ref)
            dg_ref[...] = jnp.zeros_like(dg_ref)
            dgate_ref[...] = jnp.zeros_like(dgate_ref)

        f = lax.dot_general(hid_ref[...], wd_ref[...], _NN, preferred_element_type=F32)
        gatev = gate_ref[...]
        xv = res_ref[...] + gatev * f
        gv = g_ref[...]
        rstd = lax.rsqrt(jnp.mean(xv * xv, axis=-1, keepdims=True) + EPS)
        xhat = xv * rstd
        err = xhat * gv - t_ref[...]
        dy = err * (1.0 / d)
        loss_ref[...] += jnp.sum(0.5 * err * dy, axis=0, keepdims=True)
        dg_ref[...] += jnp.sum(dy * xhat, axis=0, keepdims=True)
        dxh = dy * gv
        proj = jnp.mean(dxh * xhat, axis=-1, keepdims=True)
        dx = rstd * (dxh - xhat * proj)
        dx_ref[...] = dx
        df_ref[...] = (dx * gatev).astype(BF16)
        dgate_ref[...] += jnp.sum(dx * f, axis=0, keepdims=True)

        @pl.when(i == nsteps - 1)
        def _():
            loss_ref[...] = jnp.broadcast_to(jnp.sum(loss_ref[...], axis=-1, keepdims=True), loss_ref.shape)

    row = pl.BlockSpec((tr, d), lambda i: (i, 0))
    vec = _vec_spec(d)
    return _pc(body, out_shape=(_sds((s, d), F32), _sds((1, d), F32), _sds((1, d), F32), _sds((s, d), BF16), _sds((1, d), F32)),
               grid=(nsteps,),
               in_specs=[pl.BlockSpec((tr, k), lambda i: (i, 0)), pl.BlockSpec((k, d), lambda i: (0, 0)), row, vec, vec, row],
               out_specs=(row, vec, vec, row, vec),
               compiler_params=_params("arbitrary"), name=name)(hid, wd, res, gate, g, target)


def _mm(lhs, rhs, dims, out_dtype, name, res=None, gate=None, aux_dtype=None, norm=None, exchange=None):
    lhs_list = list(lhs) if isinstance(lhs, (list, tuple)) else [lhs]
    rhs_list = list(rhs) if isinstance(rhs, (list, tuple)) else [rhs]
    n_terms = len(lhs_list)
    assert n_terms == len(rhs_list)
    m = lhs_list[0].shape[1 if dims == "tn" else 0]
    n = rhs_list[0].shape[0 if dims == "nt" else 1]
    tn = _tile(n, 1024, LANES)
    tm = _tile(m, 512, LANES if dims == "tn" else 16)
    dn = {"nn": _NN, "nt": _NT, "tn": _TN}[dims]
    in_specs, args = [], []
    for l, r in zip(lhs_list, rhs_list):
        k = l.shape[0 if dims == "tn" else 1]
        assert k == r.shape[1 if dims == "nt" else 0] and k <= MAX_CONTRACTION, (l.shape, r.shape, dims)
        in_specs.append(pl.BlockSpec((k, tm), lambda i, j: (0, i)) if dims == "tn" else pl.BlockSpec((tm, k), lambda i, j: (i, 0)))
        in_specs.append(pl.BlockSpec((tn, k), lambda i, j: (j, 0)) if dims == "nt" else pl.BlockSpec((k, tn), lambda i, j: (0, j)))
        args += [l, r]
    out_spec = pl.BlockSpec((tm, tn), lambda i, j: (i, j))
    has_res, has_gate, has_aux, has_norm = res is not None, gate is not None, aux_dtype is not None, norm is not None
    assert not has_norm or tn == n

    def body(*refs):
        refs = list(refs)
        pos = 2 * n_terms
        res_ref = gate_ref = aux_ref = None
        if has_res:
            res_ref = refs[pos]; pos += 1
        if has_gate:
            gate_ref = refs[pos]; pos += 1
        if has_norm:
            ng_ref, nsh_ref, nsc_ref = refs[pos:pos + 3]; pos += 3
        out_ref = refs[pos]; pos += 1
        if has_aux:
            aux_ref = refs[pos]; pos += 1
        acc = lax.dot_general(refs[0][...], refs[1][...], dn, preferred_element_type=F32)
        for p in range(1, n_terms):
            acc += lax.dot_general(refs[2 * p][...], refs[2 * p + 1][...], dn, preferred_element_type=F32)
        if has_aux:
            aux_ref[...] = acc.astype(aux_dtype)
        if has_gate:
            acc = acc * gate_ref[...]
        if has_res:
            acc = res_ref[...] + acc
        out_ref[...] = acc.astype(out_dtype)
        if has_norm:
            rstd = lax.rsqrt(jnp.mean(acc * acc, axis=-1, keepdims=True) + EPS)
            refs[pos][...] = (acc * rstd * ng_ref[...] * (1.0 + nsc_ref[...]) + nsh_ref[...]).astype(BF16)

    if has_res:
        in_specs.append(out_spec); args.append(res)
    if has_gate:
        in_specs.append(pl.BlockSpec((1, tn), lambda i, j: (0, j))); args.append(gate)
    if has_norm:
        in_specs += [pl.BlockSpec((1, tn), lambda i, j: (0, j))] * 3
        args += list(norm)
    out_shape = [_sds((m, n), out_dtype)]
    out_specs = [out_spec]
    if has_aux:
        out_shape.append(_sds((m, n), aux_dtype)); out_specs.append(out_spec)
    if has_norm:
        out_shape.append(_sds((m, n), BF16)); out_specs.append(out_spec)
    outs = _pc(body, exchange, out_shape=tuple(out_shape), grid=(m // tm, n // tn), in_specs=in_specs,
               out_specs=tuple(out_specs), compiler_params=_params("parallel", "parallel"), name=name)(*args)
    return outs if len(out_shape) > 1 else outs[0]


def _ffn_up(h, wg_t, wu_t, name, exchange=None):
    s, d = h.shape
    f = wg_t.shape[0]
    tm = _tile(s, 1024, 16)
    tn = _tile(f, 256, LANES)

    def body(h_ref, wg_ref, wu_ref, a_ref, u_ref, hid_ref):
        hv = h_ref[...]
        a = lax.dot_general(hv, wg_ref[...], _NT, preferred_element_type=F32)
        u = lax.dot_general(hv, wu_ref[...], _NT, preferred_element_type=F32)
        a_ref[...] = a.astype(BF16)
        u_ref[...] = u.astype(BF16)
        hid_ref[...] = (a * jax.nn.sigmoid(a) * u).astype(BF16)

    hs = pl.BlockSpec((tm, d), lambda i, j: (i, 0))
    ws = pl.BlockSpec((tn, d), lambda i, j: (j, 0))
    os_ = pl.BlockSpec((tm, tn), lambda i, j: (i, j))
    return _pc(body, exchange, out_shape=(_sds((s, f), BF16),) * 3, grid=(s // tm, f // tn),
               in_specs=[hs, ws, ws], out_specs=(os_, os_, os_),
               compiler_params=_params("parallel", "parallel"), name=name)(h, wg_t, wu_t)


def _ffn_dact(df, wd, a, u, name, exchange=None):
    s, d = df.shape
    f = wd.shape[0]
    tm = _tile(s, 1024, 16)
    tn = _tile(f, 256, LANES)

    def body(df_ref, wd_ref, a_ref, u_ref, da_ref, du_ref):
        dhid = lax.dot_general(df_ref[...], wd_ref[...], _NT, preferred_element_type=F32)
        av = a_ref[...].astype(F32)
        uv = u_ref[...].astype(F32)
        sig = jax.nn.sigmoid(av)
        da_ref[...] = (dhid * uv * (sig * (1.0 + av * (1.0 - sig)))).astype(BF16)
        du_ref[...] = (dhid * (av * sig)).astype(BF16)

    ds_ = pl.BlockSpec((tm, d), lambda i, j: (i, 0))
    ws = pl.BlockSpec((tn, d), lambda i, j: (j, 0))
    os_ = pl.BlockSpec((tm, tn), lambda i, j: (i, j))
    return _pc(body, exchange, out_shape=(_sds((s, f), BF16),) * 2, grid=(s // tm, f // tn),
               in_specs=[ds_, ws, os_, os_], out_specs=(os_, os_),
               compiler_params=_params("parallel", "parallel"), name=name)(df, wd, a, u)


def _split3(v):
    hi = v.astype(BF16)
    r1 = v - hi.astype(F32)
    mid = r1.astype(BF16)
    lo = (r1 - mid.astype(F32)).astype(BF16)
    return hi, mid, lo


def _dot3(v, mat):
    hi, mid, lo = _split3(v)
    out = lax.dot_general(hi, mat, _NN, preferred_element_type=F32)
    out += lax.dot_general(mid, mat, _NN, preferred_element_type=F32)
    out += lax.dot_general(lo, mat, _NN, preferred_element_type=F32)
    return out


def _forget_fwd(flog_t, bias, name):
    h, s = flog_t.shape
    blk = _tile(s, 512, LANES)
    tri = (jnp.arange(blk)[:, None] <= jnp.arange(blk)[None, :]).astype(BF16)

    def body(z_ref, b_ref, tri_ref, f_ref, carry):
        @pl.when(pl.program_id(0) == 0)
        def _():
            carry[...] = jnp.zeros_like(carry)

        z = z_ref[...] + b_ref[...]
        e = jnp.exp(-jnp.abs(z))
        w = 1.0 + e
        log1p_e = jnp.where(w == 1.0, e, jnp.log(w) * (e / (w - 1.0)))
        lf = jnp.minimum(z, 0.0) - log1p_e
        out = carry[...] + _dot3(lf, tri_ref[...])
        for j, piece in enumerate(_split3(out)):
            f_ref[j] = piece
        carry[...] = out[:, blk - 1:blk]

    zs = pl.BlockSpec((h, blk), lambda i: (0, i))
    return _pc(body, out_shape=_sds((3, h, s), BF16), grid=(s // blk,),
               in_specs=[zs, pl.BlockSpec((h, 1), lambda i: (0, 0)), pl.BlockSpec((blk, blk), lambda i: (0, 0))],
               out_specs=pl.BlockSpec((3, h, blk), lambda i: (0, 0, i)), scratch_shapes=[pltpu.VMEM((h, 1), F32)],
               compiler_params=_params("arbitrary"), name=name)(flog_t, bias, tri)


def _forget_bwd(df_t, flog_t, bias, name):
    h, s = flog_t.shape
    blk = _tile(s, 512, LANES)
    nb = s // blk
    tri = (jnp.arange(blk)[:, None] >= jnp.arange(blk)[None, :]).astype(BF16)

    def body(df_ref, z_ref, b_ref, tri_ref, dz_ref, db_ref, carry):
        @pl.when(pl.program_id(0) == 0)
        def _():
            carry[...] = jnp.zeros_like(carry)
            db_ref[...] = jnp.zeros_like(db_ref)

        rc = carry[...] + _dot3(df_ref[...], tri_ref[...])
        carry[...] = rc[:, 0:1]
        dz = rc * jax.nn.sigmoid(-(z_ref[...] + b_ref[...]))
        dz_ref[...] = dz
        db_ref[...] += jnp.sum(dz, axis=-1, keepdims=True)

    rev = pl.BlockSpec((h, blk), lambda i: (0, nb - 1 - i))
    col = pl.BlockSpec((h, 1), lambda i: (0, 0))
    return _pc(body, out_shape=(_sds((h, s), F32), _sds((h, 1), F32)), grid=(nb,),
               in_specs=[rev, rev, col, pl.BlockSpec((blk, blk), lambda i: (0, 0))],
               out_specs=(rev, col), scratch_shapes=[pltpu.VMEM((h, 1), F32)],
               compiler_params=_params("arbitrary"), name=name)(df_t, flog_t, bias, tri)


def _attn_tiles(s):
    return _tile(s, 1024, LANES)


def _attn_half(t):
    return t // 2 if t >= 4 * LANES else t


BIAS_ROWS = 16


def _attn_prep(qkv, f_pieces, name):
    s = qkv.shape[0]
    a_w = qkv.shape[1] // 3
    npair = a_w // LANES
    t = _attn_tiles(s)
    scale = 1.0 / math.sqrt(HEAD_DIM)

    six = f_pieces[:, :2 * npair].reshape(3, npair, 2, s).transpose(1, 3, 2, 0).reshape(npair, s, 6)
    feat = jnp.concatenate([six, jnp.ones((npair, s, 1), BF16), jnp.zeros((npair, s, BIAS_ROWS - 7), BF16)], axis=-1)
    place_q = [[0.0] * (2 * LANES) for _ in range(BIAS_ROWS)]
    place_k = [[0.0] * (2 * LANES) for _ in range(BIAS_ROWS)]
    for hh in range(2):
        b0 = hh * LANES + (HEAD_DIM if hh == 0 else 0)
        for j in range(3):
            place_q[3 * hh + j][b0 + j] = 1.0
            place_q[6][b0 + 3 + j] = 1.0
            place_k[6][b0 + j] = 1.0
            place_k[3 * hh + j][b0 + 3 + j] = -1.0
    place_q = jnp.array(place_q, BF16)
    place_k = jnp.array(place_k, BF16)

    def body(q_ref, k_ref, v_ref, f_ref, pq_ref, pk_ref, qa_ref, ka_ref, va_ref):
        lane = lax.broadcasted_iota(jnp.int32, (1, LANES), 1)
        q2 = (q_ref[...].astype(F32) * scale).astype(BF16)
        k2, v2 = k_ref[...], v_ref[...]
        qx = lax.dot_general(f_ref[0], pq_ref[...], _NN, preferred_element_type=F32).astype(BF16)
        kx = lax.dot_general(f_ref[0], pk_ref[...], _NN, preferred_element_type=F32).astype(BF16)
        for hh in range(2):
            real = (lane < HEAD_DIM) if hh == 0 else (lane >= HEAD_DIM)
            cols = slice(hh * LANES, (hh + 1) * LANES)
            qa_ref[:, cols] = jnp.where(real, q2, qx[:, cols])
            ka_ref[:, cols] = jnp.where(real, k2, kx[:, cols])
            va_ref[:, cols] = jnp.where(real, v2, jnp.zeros_like(v2))

    def col(off):
        return pl.BlockSpec((t, LANES), lambda p, i: (i, off + p))

    out = pl.BlockSpec((t, 2 * LANES), lambda p, i: (i, p))
    place = pl.BlockSpec((BIAS_ROWS, 2 * LANES), lambda p, i: (0, 0))
    return _pc(body, out_shape=(_sds((s, 2 * a_w), BF16),) * 3, grid=(npair, s // t),
               in_specs=[col(0), col(npair), col(2 * npair), pl.BlockSpec((1, t, BIAS_ROWS), lambda p, i: (p, i, 0)),
                         place, place],
               out_specs=(out, out, out), compiler_params=_params("parallel", "parallel"), name=name)(
                   qkv, qkv, qkv, feat, place_q, place_k)


def _attn_fwd(qa, ka, va, name, exchange=None):
    s = qa.shape[0]
    a_w = qa.shape[1] // 2
    npair = a_w // LANES
    t = _attn_tiles(s)
    nq = s // t
    half = _attn_half(t)

    def body(q_ref, k_ref, v_ref, o_ref, lse_ref, m_sc, l_sc, acc_sc):
        qi = pl.program_id(1)
        first = lax.broadcasted_iota(jnp.int32, (1, LANES), 1) < HEAD_DIM
        m_sc[...] = jnp.full_like(m_sc, NEG_BIG)
        l_sc[...] = jnp.zeros_like(l_sc)
        acc_sc[...] = jnp.zeros_like(acc_sc)

        def step(q0, k_start, size, diag):
            q_sl = slice(q0, q0 + size)
            k_rows = pl.ds(pl.multiple_of(k_start, size), size)
            m_old = m_sc[q_sl, :]
            keep = None
            if diag:
                keep = (lax.broadcasted_iota(jnp.int32, (size, size), 0) >= lax.broadcasted_iota(jnp.int32, (size, size), 1))
            m_new, rs, pv = [], [], []
            for hh in range(2):
                cols = slice(hh * LANES, (hh + 1) * LANES)
                sc = lax.dot_general(q_ref[q_sl, cols], k_ref[k_rows, cols], _NT, preferred_element_type=F32)
                if diag:
                    sc = jnp.where(keep, sc, NEG_BIG)
                mo = m_old[:, hh * HEAD_DIM:hh * HEAD_DIM + 1]
                mn = jnp.maximum(mo, jnp.max(sc, axis=1, keepdims=True))
                p = jnp.exp(sc - mn)
                m_new.append(mn)
                rs.append(jnp.sum(p, axis=1, keepdims=True))
                pv.append(lax.dot_general(p.astype(BF16), v_ref[k_rows, cols], _NN, preferred_element_type=F32))
            m2 = jnp.where(first, m_new[0], m_new[1])
            alpha = jnp.exp(m_old - m2)
            m_sc[q_sl, :] = m2
            l_sc[q_sl, :] = alpha * l_sc[q_sl, :] + jnp.where(first, rs[0], rs[1])
            acc_sc[q_sl, :] = alpha * acc_sc[q_sl, :] + pv[0] + pv[1]

        def below_diagonal(ki, carry):
            step(0, ki * t, t, False)
            return carry

        lax.fori_loop(0, qi, below_diagonal, 0)
        step(0, qi * t, half, True)
        if half < t:
            step(half, qi * t, half, False)
            step(half, qi * t + half, half, True)
        l2 = l_sc[...]
        o_ref[...] = acc_sc[...] / l2
        lse_ref[...] = m_sc[...] + jnp.log(l2)

    qs = pl.BlockSpec((t, 2 * LANES), lambda p, qi: (qi, p))
    ks = pl.BlockSpec((s, 2 * LANES), lambda p, qi: (0, p))
    os_ = pl.BlockSpec((t, LANES), lambda p, qi: (qi, p))
    return _pc(body, exchange, out_shape=(_sds((s, a_w), F32), _sds((s, a_w), F32)), grid=(npair, nq),
               in_specs=[qs, ks, ks], out_specs=(os_, os_),
               scratch_shapes=[pltpu.VMEM((t, LANES), F32)] * 3,
               compiler_params=_params("parallel", "arbitrary"), name=name)(qa, ka, va)


def _attn_bwd(qa, ka, va, do, o, lse, name, exchange=None):
    s = qa.shape[0]
    a_w = qa.shape[1] // 2
    npair = a_w // LANES
    t = _attn_tiles(s)
    nq = s // t
    half = _attn_half(t)
    scale = 1.0 / math.sqrt(HEAD_DIM)

    def body(q_ref, k_ref, v_ref, do_ref, o_ref, lse_ref, dq_ref, dk_ref, dv_ref, qx_ref, kx_ref, dk_sc, dv_sc, kx_sc):
        ki = pl.program_id(1)
        first = lax.broadcasted_iota(jnp.int32, (1, LANES), 1) < HEAD_DIM

        @pl.when(ki == 0)
        def _():
            dq_ref[...] = jnp.zeros_like(dq_ref)
            qx_ref[...] = jnp.zeros_like(qx_ref)

        def step(q_start, k0, size, diag, assign):
            rows = pl.ds(pl.multiple_of(q_start, size), size)
            k_sl = slice(k0, k0 + size)
            do2 = do_ref[rows, :]
            lse2 = lse_ref[rows, :]
            dd = do2.astype(F32) * o_ref[rows, :]
            keep = None
            if diag:
                keep = (lax.broadcasted_iota(jnp.int32, (size, size), 0) >= lax.broadcasted_iota(jnp.int32, (size, size), 1))
            dq_h, dk_h, dv_h = [], [], []
            for hh in range(2):
                sel = first if hh == 0 else jnp.logical_not(first)
                cols = slice(hh * LANES, (hh + 1) * LANES)
                qh, kh, vh = q_ref[rows, cols], k_ref[k_sl, cols], v_ref[k_sl, cols]
                delta = jnp.sum(jnp.where(sel, dd, 0.0), axis=1, keepdims=True)
                sc = lax.dot_general(qh, kh, _NT, preferred_element_type=F32)
                if diag:
                    sc = jnp.where(keep, sc, NEG_BIG)
                p = jnp.exp(sc - lse2[:, hh * HEAD_DIM:hh * HEAD_DIM + 1])
                dp = lax.dot_general(do2, vh, _NT, preferred_element_type=F32)
                ds_b = (p * (dp - delta)).astype(BF16)
                dv_h.append(lax.dot_general(p.astype(BF16), do2, _TN, preferred_element_type=F32))
                dk_h.append(lax.dot_general(ds_b, qh, _TN, preferred_element_type=F32))
                dq_h.append(lax.dot_general(ds_b, kh, _NN, preferred_element_type=F32))
            dq_ref[rows, :] += jnp.where(first, dq_h[0], dq_h[1]) * scale
            qx_ref[rows, :] += jnp.where(first, dq_h[1], dq_h[0])
            dk_new = jnp.where(first, dk_h[0], dk_h[1])
            kx_new = jnp.where(first, dk_h[1], dk_h[0])
            dv_new = jnp.where(first, dv_h[0], dv_h[1])
            if assign:
                dk_sc[k_sl, :] = dk_new
                kx_sc[k_sl, :] = kx_new
                dv_sc[k_sl, :] = dv_new
            else:
                dk_sc[k_sl, :] += dk_new
                kx_sc[k_sl, :] += kx_new
                dv_sc[k_sl, :] += dv_new

        def below_diagonal(qi, carry):
            step(qi * t, 0, t, False, False)
            return carry

        step(ki * t, 0, half, True, True)
        if half < t:
            step(ki * t + half, 0, half, False, False)
            step(ki * t + half, half, half, True, True)
        lax.fori_loop(ki + 1, nq, below_diagonal, 0)
        dk_ref[...] = dk_sc[...].astype(BF16)
        dv_ref[...] = dv_sc[...].astype(BF16)
        kx_ref[...] = kx_sc[...]

    ks2 = pl.BlockSpec((t, 2 * LANES), lambda p, ki: (ki, p))
    qs2 = pl.BlockSpec((s, 2 * LANES), lambda p, ki: (0, p))
    whole = pl.BlockSpec((s, LANES), lambda p, ki: (0, p))
    kout = pl.BlockSpec((t, LANES), lambda p, ki: (ki, p))
    return _pc(body, exchange,
               out_shape=(_sds((s, a_w), F32), _sds((s, a_w), BF16), _sds((s, a_w), BF16), _sds((s, a_w), F32),
                          _sds((s, a_w), F32)),
               grid=(npair, nq), in_specs=[qs2, ks2, ks2, whole, whole, whole],
               out_specs=(whole, kout, kout, whole, kout),
               scratch_shapes=[pltpu.VMEM((t, LANES), F32)] * 3,
               compiler_params=_params("parallel", "arbitrary"), name=name)(qa, ka, va, do, o, lse)

def _decay_grads(qx, kx, name):
    s, a_w = qx.shape
    n_heads = a_w // HEAD_DIM
    tr = _tile(s, 512, 8)
    pick_q = [[0.0] * LANES for _ in range(a_w)]
    pick_k = [[0.0] * LANES for _ in range(a_w)]
    for h in range(n_heads):
        b0 = (h // 2) * LANES + (HEAD_DIM if h % 2 == 0 else 0)
        pick_q[b0][h] = 1.0
        pick_k[b0 + 3][h] = 1.0
    pick_q = jnp.array(pick_q, BF16)
    pick_k = jnp.array(pick_k, BF16)

    def body(qx_ref, kx_ref, pq_ref, pk_ref, o_ref):
        o_ref[...] = _dot3(qx_ref[...], pq_ref[...]) - _dot3(kx_ref[...], pk_ref[...])

    row = pl.BlockSpec((tr, a_w), lambda i: (i, 0))
    pick = pl.BlockSpec((a_w, LANES), lambda i: (0, 0))
    return _pc(body, out_shape=_sds((s, LANES), F32), grid=(s // tr,), in_specs=[row, row, pick, pick],
               out_specs=pl.BlockSpec((tr, LANES), lambda i: (i, 0)),
               compiler_params=_params("parallel"), name=name)(qx, kx, pick_q, pick_k)


def _shift_down(z, k, rows):
    return jnp.where(rows >= k, pltpu.roll(z, k, 0), 0.0)


def _shift_up(z, k, rows, n):
    return jnp.where(rows < n - k, pltpu.roll(z, n - k, 0), 0.0)


def _conv_fwd(bcx, conv_w, name):
    s = bcx.shape[0]
    cw = bcx.shape[1] // 3
    nb = cw // LANES

    def body(b_ref, c_ref, x_ref, w_ref, cv_ref):
        rows = lax.broadcasted_iota(jnp.int32, (s, LANES), 0)
        z = c_ref[...] * x_ref[...]
        w = w_ref[...]
        y = w[2:3, :] * z + w[1:2, :] * _shift_down(z, 1, rows) + w[0:1, :] * _shift_down(z, 2, rows)
        cv_ref[...] = b_ref[...] * y

    def col(off):
        return pl.BlockSpec((s, LANES), lambda j: (0, j + off))

    return _pc(body, out_shape=_sds((s, cw), F32), grid=(nb,),
               in_specs=[col(0), col(nb), col(2 * nb), pl.BlockSpec((CONV_K, LANES), lambda j: (0, j))],
               out_specs=col(0), compiler_params=_params("parallel"), name=name)(bcx, bcx, bcx, conv_w)


def _conv_bwd(dcv, bcx, conv_w, name):
    s = bcx.shape[0]
    cw = bcx.shape[1] // 3
    nb = cw // LANES

    def body(dcv_ref, b_ref, c_ref, x_ref, w_ref, db_ref, dc_ref, dxc_ref, dw_ref):
        rows = lax.broadcasted_iota(jnp.int32, (s, LANES), 0)
        cv_, xv = c_ref[...], x_ref[...]
        z = cv_ * xv
        w = w_ref[...]
        z1 = _shift_down(z, 1, rows)
        z2 = _shift_down(z, 2, rows)
        y = w[2:3, :] * z + w[1:2, :] * z1 + w[0:1, :] * z2
        dcvv = dcv_ref[...]
        db_ref[...] = (dcvv * y).astype(BF16)
        dy = dcvv * b_ref[...]
        dw_ref[0:1, :] = jnp.sum(dy * z2, axis=0, keepdims=True)
        dw_ref[1:2, :] = jnp.sum(dy * z1, axis=0, keepdims=True)
        dw_ref[2:3, :] = jnp.sum(dy * z, axis=0, keepdims=True)
        dz = w[2:3, :] * dy + w[1:2, :] * _shift_up(dy, 1, rows, s) + w[0:1, :] * _shift_up(dy, 2, rows, s)
        dc_ref[...] = (dz * xv).astype(BF16)
        dxc_ref[...] = (dz * cv_).astype(BF16)

    def col(off):
        return pl.BlockSpec((s, LANES), lambda j: (0, j + off))

    wspec = pl.BlockSpec((CONV_K, LANES), lambda j: (0, j))
    db, dc, dxc, dw = _pc(body, out_shape=(_sds((s, cw), BF16),) * 3 + (_sds((CONV_K, cw), F32),), grid=(nb,),
                          in_specs=[col(0), col(0), col(nb), col(2 * nb), wspec],
                          out_specs=(col(0), col(0), col(0), wspec),
                          compiler_params=_params("parallel"), name=name)(dcv, bcx, bcx, bcx, conv_w)
    return db, dc, dxc, dw


def _group_matrix():
    idx = jnp.arange(LANES) // HEAD_DIM
    return (idx[:, None] == idx[None, :]).astype(BF16)


def _group_sum(v, gmat):
    return _dot3(v, gmat)


def _gnorm_fwd(att, cv, gg, name):
    s, a_w = att.shape
    cw = cv.shape[1]
    d = a_w + cw
    tr = _tile(s, 512, 16)
    gmat = _group_matrix()

    def body(att_ref, cv_ref, gg_ref, gm_ref, yn_ref):
        gm = gm_ref[...]
        for c0 in range(0, d, LANES):
            y = att_ref[:, c0:c0 + LANES] if c0 < a_w else cv_ref[:, c0 - a_w:c0 - a_w + LANES]
            ms = _group_sum(y * y, gm) * (1.0 / HEAD_DIM)
            yn_ref[:, c0:c0 + LANES] = (y * lax.rsqrt(ms + EPS) * gg_ref[:, c0:c0 + LANES]).astype(BF16)

    return _pc(body, out_shape=_sds((s, d), BF16), grid=(s // tr,),
               in_specs=[pl.BlockSpec((tr, a_w), lambda i: (i, 0)), pl.BlockSpec((tr, cw), lambda i: (i, 0)),
                         _vec_spec(d), pl.BlockSpec((LANES, LANES), lambda i: (0, 0))],
               out_specs=pl.BlockSpec((tr, d), lambda i: (i, 0)),
               compiler_params=_params("parallel"), name=name)(att, cv, gg, gmat)


def _gnorm_bwd(dyn, att, cv, gg, name):
    s, a_w = att.shape
    cw = cv.shape[1]
    d = a_w + cw
    tr = _tile(s, 256, 16)
    gmat = _group_matrix()

    def body(dyn_ref, att_ref, cv_ref, gg_ref, gm_ref, datt_ref, dcv_ref, dgg_ref):
        @pl.when(pl.program_id(0) == 0)
        def _():
            dgg_ref[...] = jnp.zeros_like(dgg_ref)

        gm = gm_ref[...]
        for c0 in range(0, d, LANES):
            y = att_ref[:, c0:c0 + LANES] if c0 < a_w else cv_ref[:, c0 - a_w:c0 - a_w + LANES]
            dv = dyn_ref[:, c0:c0 + LANES]
            r = lax.rsqrt(_group_sum(y * y, gm) * (1.0 / HEAD_DIM) + EPS)
            xhat = y * r
            dgg_ref[:, c0:c0 + LANES] += jnp.sum(dv * xhat, axis=0, keepdims=True)
            dxh = dv * gg_ref[:, c0:c0 + LANES]
            proj = _group_sum(dxh * xhat, gm) * (1.0 / HEAD_DIM)
            dy = r * (dxh - xhat * proj)
            if c0 < a_w:
                datt_ref[:, c0:c0 + LANES] = dy.astype(BF16)
            else:
                dcv_ref[:, c0 - a_w:c0 - a_w + LANES] = dy

    return _pc(body, out_shape=(_sds((s, a_w), BF16), _sds((s, cw), F32), _sds((1, d), F32)), grid=(s // tr,),
               in_specs=[pl.BlockSpec((tr, d), lambda i: (i, 0)), pl.BlockSpec((tr, a_w), lambda i: (i, 0)),
                         pl.BlockSpec((tr, cw), lambda i: (i, 0)), _vec_spec(d),
                         pl.BlockSpec((LANES, LANES), lambda i: (0, 0))],
               out_specs=(pl.BlockSpec((tr, a_w), lambda i: (i, 0)), pl.BlockSpec((tr, cw), lambda i: (i, 0)),
                          _vec_spec(d)),
               compiler_params=_params("arbitrary"), name=name)(dyn, att, cv, gg, gmat)


def _adamw_math(w, g, m, v):
    m_new = ADAM_B1 * m + (1.0 - ADAM_B1) * g
    v_new = ADAM_B2 * v + (1.0 - ADAM_B2) * (g * g)
    m_hat = m_new / (1.0 - ADAM_B1 ** ADAM_STEP)
    v_hat = v_new / (1.0 - ADAM_B2 ** ADAM_STEP)
    delta = -ADAM_LR * (m_hat / (jnp.sqrt(v_hat) + ADAM_EPS) + ADAM_WD * w)
    return delta, m_new, v_new


def _row_tile(r, c):
    return _tile(r, max(8, ((1 << 19) // c) // 8 * 8), 8)


def _adamw(w, g, m, v, name):
    r, c = w.shape
    tr = _row_tile(r, c)

    def body(w_ref, g_ref, m_ref, v_ref, d_ref, mo_ref, vo_ref):
        d, mn, vn = _adamw_math(w_ref[...], g_ref[...], m_ref[...], v_ref[...])
        d_ref[...] = d
        mo_ref[...] = mn
        vo_ref[...] = vn

    spec = pl.BlockSpec((tr, c), lambda i: (i, 0))
    return _pc(body, out_shape=(_sds((r, c), F32),) * 3, grid=(r // tr,), in_specs=[spec] * 4,
               out_specs=(spec,) * 3, compiler_params=_params("parallel"), name=name)(w, g, m, v)


def _adamw_halves(w, mine, theirs, m, v, core, name):
    r2, c = w.shape
    r = r2 // 2
    assert mine.shape == (r, c) and theirs.shape == (r, c)
    tr = _row_tile(r, c)
    nb = r // tr

    def body(core_ref, w_ref, a_ref, b_ref, m_ref, v_ref, g_ref, d_ref, mo_ref, vo_ref):
        g = jnp.where(pl.program_id(0) == core_ref[0], a_ref[...], b_ref[...])
        d, mn, vn = _adamw_math(w_ref[...], g, m_ref[...], v_ref[...])
        g_ref[...] = g
        d_ref[...] = d
        mo_ref[...] = mn
        vo_ref[...] = vn

    full = pl.BlockSpec((tr, c), lambda h, i, core_ref: (h * nb + i, 0))
    half = pl.BlockSpec((tr, c), lambda h, i, core_ref: (i, 0))
    grid_spec = pltpu.PrefetchScalarGridSpec(
        num_scalar_prefetch=1, grid=(2, nb), in_specs=[full, half, half, full, full], out_specs=(full,) * 4)
    return _pc(body, out_shape=(_sds((r2, c), F32),) * 4, grid_spec=grid_spec,
               compiler_params=_params("parallel", "parallel"), name=name)(core, w, mine, theirs, m, v)


def _ada_fwd(c16, ada_w, ada_b, name):
    d, n = ada_w.shape
    tn = _tile(n, 768, LANES)

    def body(c_ref, w_ref, b_ref, o_ref):
        cv = c_ref[...]
        sc = (cv * jax.nn.sigmoid(cv)).astype(BF16)
        o_ref[...] = lax.dot_general(sc, w_ref[...].astype(BF16), _NN, preferred_element_type=F32) + b_ref[...]

    return _pc(body, out_shape=_sds((16, n), F32), grid=(n // tn,),
               in_specs=[pl.BlockSpec((16, d), lambda j: (0, 0)), pl.BlockSpec((d, tn), lambda j: (0, j)),
                         pl.BlockSpec((1, tn), lambda j: (0, j))],
               out_specs=pl.BlockSpec((16, tn), lambda j: (0, j)),
               compiler_params=_params("parallel"), name=name)(c16, ada_w, ada_b)


def _ada_update(c16_t, dmod16, w, m, v, name, exchange=None):
    r, c = w.shape
    tr = _row_tile(r, c)

    def body(c_ref, dm_ref, w_ref, m_ref, v_ref, g_ref, d_ref, mo_ref, vo_ref):
        cv = c_ref[...]
        sc = (cv * jax.nn.sigmoid(cv)).astype(BF16)
        g = lax.dot_general(sc, dm_ref[...].astype(BF16), _NN, preferred_element_type=F32)
        d, mn, vn = _adamw_math(w_ref[...], g, m_ref[...], v_ref[...])
        g_ref[...] = g
        d_ref[...] = d
        mo_ref[...] = mn
        vo_ref[...] = vn

    spec = pl.BlockSpec((tr, c), lambda i: (i, 0))
    return _pc(body, exchange, out_shape=(_sds((r, c), F32),) * 4, grid=(r // tr,),
               in_specs=[pl.BlockSpec((tr, 16), lambda i: (i, 0)), pl.BlockSpec((16, c), lambda i: (0, 0)),
                         spec, spec, spec],
               out_specs=(spec,) * 4, compiler_params=_params("parallel"), name=name)(c16_t, dmod16, w, m, v)


def _add_half(dw, recv, core, name):
    _, _, r, w = dw.shape
    tr = _tile(r, 512, 16)

    def body(core_ref, a_ref, b_ref, o_ref):
        o_ref[...] = (a_ref[...].astype(F32) + b_ref[...].astype(F32)).astype(BF16)

    grid_spec = pltpu.PrefetchScalarGridSpec(
        num_scalar_prefetch=1, grid=(N_CHIPS, r // tr),
        in_specs=[pl.BlockSpec((None, None, tr, w), lambda s, i, core_ref: (s, core_ref[0], i, 0)),
                  pl.BlockSpec((None, tr, w), lambda s, i, core_ref: (s, i, 0))],
        out_specs=pl.BlockSpec((None, tr, w), lambda s, i, core_ref: (s, i, 0)))
    return _pc(body, out_shape=_sds((N_CHIPS, r, w), BF16), grid_spec=grid_spec,
               compiler_params=_params("parallel", "parallel"), name=name)(core, dw, recv)


def _sum_chips(own, recv, chip, name):
    _, r, w = own.shape
    tr = _tile(r, 512, 16)

    def body(chip_ref, own_ref, p_ref, o_ref):
        acc = own_ref[...].astype(F32)
        for q in range(N_CHIPS - 1):
            acc = acc + p_ref[q].astype(F32)
        o_ref[...] = acc

    grid_spec = pltpu.PrefetchScalarGridSpec(
        num_scalar_prefetch=1, grid=(r // tr,),
        in_specs=[pl.BlockSpec((None, tr, w), lambda i, chip_ref: (chip_ref[0], i, 0)),
                  pl.BlockSpec((N_CHIPS - 1, tr, w), lambda i, chip_ref: (0, i, 0))],
        out_specs=pl.BlockSpec((tr, w), lambda i, chip_ref: (i, 0)))
    return _pc(body, out_shape=_sds((r, w), F32), grid_spec=grid_spec,
               compiler_params=_params("parallel"), name=name)(chip, own, recv)


def _sum_devices(parts, name):
    nd, r, w = parts.shape

    def body(p_ref, o_ref):
        acc = p_ref[0]
        for q in range(1, nd):
            acc = acc + p_ref[q]
        o_ref[...] = acc

    return _pc(body, out_shape=_sds((r, w), F32), name=name)(parts)


def _place():
    x, y, c = lax.axis_index("x"), lax.axis_index("y"), lax.axis_index("c")
    chips = [(1 - x, y), (x, 1 - y), (1 - x, 1 - y)]
    return x, y, c, chips


def _small_gather_exchange(blk):
    r, w = blk.shape

    def copies(src, dst, send_sems, recv_sems):
        x, y, c, chips = _place()
        me, sibling = (x, y, c), (x, y, 1 - c)

        def rows(px, py, pc):
            return dst[0].at[pl.ds((4 * px + 2 * py + pc) * r, r), :]

        def copy(k, block, to, own=False):
            return _remote(src[0] if own else rows(*block), rows(*block), send_sems, recv_sems, k, to)

        mine = pltpu.make_async_copy(src[0], rows(*me), send_sems.at[7])
        first = [copy(0, me, sibling, own=True)] + [copy(1 + j, me, (*chip, c), own=True) for j, chip in enumerate(chips)]
        passed = [copy(4 + j, (*chip, c), sibling) for j, chip in enumerate(chips)]
        landed = [copy(1 + j, (*chip, c), me) for j, chip in enumerate(chips)]
        from_sibling = [copy(0, sibling, me)] + [copy(4 + j, (*chip, 1 - c), me) for j, chip in enumerate(chips)]
        return mine, first, passed, landed, from_sibling

    def start(src, dst, send_sems, recv_sems):
        mine, first, _, _, _ = copies(src, dst, send_sems, recv_sems)
        mine.start()
        for cp in first:
            cp.start()

    def finish(src, dst, send_sems, recv_sems):
        mine, first, passed, landed, from_sibling = copies(src, dst, send_sems, recv_sems)
        for arrival, onward in zip(landed, passed):
            arrival.wait_recv()
            onward.start()
        for cp in from_sibling:
            cp.wait_recv()
        for cp in first + passed:
            cp.wait_send()
        mine.wait()

    return _Exchange([blk], [_sds((N_DEV * r, w), blk.dtype)], 8, start, finish)


def _remote(src, dst, send_sems, recv_sems, k, to):
    return pltpu.make_async_remote_copy(src_ref=src, dst_ref=dst, send_sem=send_sems.at[k], recv_sem=recv_sems.at[k],
                                        device_id=to, device_id_type=MESH)


def _exchange_of(inputs, out_shapes, n_sems, copies, aliases=None):
    def start(src, dst, send_sems, recv_sems):
        for cp in copies(src, dst, send_sems, recv_sems)[0]:
            cp.start()

    def finish(src, dst, send_sems, recv_sems):
        sends, arrivals = copies(src, dst, send_sems, recv_sems)
        for cp in arrivals:
            cp.wait_recv()
        for cp in sends:
            cp.wait_send()

    return _Exchange(inputs, out_shapes, n_sems, start, finish, aliases)


def _run_exchange(ex, name):
    n_in, n_out = len(ex.inputs), len(ex.out_shapes)

    def body(*refs):
        src, dst = refs[:n_in], refs[n_in:n_in + n_out]
        send_sems, recv_sems = refs[n_in + n_out:]
        ex.start(src, dst, send_sems, recv_sems)
        ex.finish(src, dst, send_sems, recv_sems)

    ex.set_results(pl.pallas_call(
        body, out_shape=tuple(ex.out_shapes), in_specs=[_ANY] * n_in, out_specs=(_ANY,) * n_out,
        scratch_shapes=[pltpu.SemaphoreType.DMA((ex.n_sems,)), pltpu.SemaphoreType.DMA((ex.n_sems,))],
        input_output_aliases=ex.aliases, name=name)(*ex.inputs))


def _gather_ici_exchange(shards):
    n = len(shards)

    def copies(own, out, send_sems, recv_sems):
        x, y, c, chips = _place()
        my_chip = 2 * x + y
        sends, arrivals = [], []
        for i in range(n):
            for j, chip in enumerate(chips):
                to = (*chip, c)
                sends.append(_remote(own[i].at[c], out[i].at[my_chip, c], send_sems, recv_sems, 4 * i + j, to))
                arrivals.append(_remote(own[i].at[c], out[i].at[2 * chip[0] + chip[1], c], send_sems, recv_sems, 4 * i + j, to))
            whole = _remote(own[i], out[i].at[my_chip], send_sems, recv_sems, 4 * i + 3, (x, y, 1 - c))
            sends.append(whole)
            arrivals.append(whole)
        return sends, arrivals

    return _exchange_of(shards, [_sds((N_CHIPS,) + s.shape, s.dtype) for s in shards], 4 * n, copies)


def _gather_pass_exchange(gathered):
    n = len(gathered)

    def copies(src, dst, send_sems, recv_sems):
        x, y, c, chips = _place()
        sends, arrivals = [], []
        for i in range(n):
            for j, chip in enumerate(chips):
                idx = 2 * chip[0] + chip[1]
                sends.append(_remote(src[i].at[idx, c], dst[i].at[idx, c], send_sems, recv_sems, 3 * i + j, (x, y, 1 - c)))
                arrivals.append(_remote(src[i].at[idx, c], dst[i].at[idx, 1 - c], send_sems, recv_sems, 3 * i + j, (x, y, 1 - c)))
        return sends, arrivals

    return _exchange_of(gathered, [_sds(g.shape, g.dtype) for g in gathered], 3 * n, copies,
                        aliases={i: i for i in range(n)})


def _reduce_sibling_exchange(grads):
    n = len(grads)

    def copies(src, dst, send_sems, recv_sems):
        x, y, c, _ = _place()
        both = [_remote(src[i].at[s, 1 - c], dst[i].at[s], send_sems, recv_sems, N_CHIPS * i + s, (x, y, 1 - c))
                for i in range(n) for s in range(N_CHIPS)]
        return both, both

    return _exchange_of(grads, [_sds((N_CHIPS,) + g.shape[2:], g.dtype) for g in grads], N_CHIPS * n, copies)


def _reduce_chips_exchange(parts):
    n = len(parts)

    def copies(src, dst, send_sems, recv_sems):
        x, y, c, chips = _place()
        both = [_remote(src[i].at[2 * chip[0] + chip[1]], dst[i].at[j], send_sems, recv_sems, 3 * i + j, (*chip, c))
                for i in range(n) for j, chip in enumerate(chips)]
        return both, both

    return _exchange_of(parts, [_sds((N_CHIPS - 1,) + p.shape[1:], p.dtype) for p in parts], 3 * n, copies)


def _share_exchange(halves):
    n = len(halves)

    def copies(src, dst, send_sems, recv_sems):
        x, y, c, _ = _place()
        both = [_remote(src[i], dst[i], send_sems, recv_sems, i, (x, y, 1 - c)) for i in range(n)]
        return both, both

    return _exchange_of(halves, [_sds(h.shape, h.dtype) for h in halves], n, copies)


HEAD_ROWS = 16


class _WeightTraffic:
    def __init__(self, shards, core, chip):
        self.shards, self.core, self.chip = shards, core, chip
        self.gather, self.grads, self.reduce, self.chip_sums, self.half_sums, self.shared = {}, {}, {}, {}, {}, {}

    def gather_ici(self, grp):
        self.gather[grp] = _gather_ici_exchange(self.shards[grp])
        return self.gather[grp]

    def gather_pass(self, grp):
        self.gather[grp] = _gather_pass_exchange(self.gather[grp].results)
        return self.gather[grp]

    def weights(self, grp):
        return [g.reshape(-1, g.shape[-1]) for g in self.gather[grp].results]

    def reduce_sibling(self, grp, grads):
        self.grads[grp] = [g.reshape(N_CHIPS, 2, g.shape[0] // (2 * N_CHIPS), g.shape[1]) for g in grads]
        self.reduce[grp] = _reduce_sibling_exchange(self.grads[grp])
        return self.reduce[grp]

    def add_halves(self, grp):
        self.chip_sums[grp] = [_add_half(g, r, self.core, "add_half_%s%d" % (grp, i))
                               for i, (g, r) in enumerate(zip(self.grads[grp], self.reduce[grp].results))]

    def reduce_chips(self, grp):
        self.reduce[grp] = _reduce_chips_exchange(self.chip_sums[grp])
        return self.reduce[grp]

    def sum_chips(self, grp):
        self.half_sums[grp] = [_sum_chips(o, p, self.chip, "sum_chips_%s%d" % (grp, i))
                               for i, (o, p) in enumerate(zip(self.chip_sums[grp], self.reduce[grp].results))]

    def share(self, grp):
        self.shared[grp] = _share_exchange(self.half_sums[grp])
        return self.shared[grp]

    def totals(self, grp):
        return list(zip(self.half_sums[grp], self.shared[grp].results))


def _ffn_fwd(x, norm_g, shift, scale, gate, wg_t, wu_t, wd, tag, next_norm, up_exchange=None, down_exchange=None):
    h = _norm_mod_fwd(x, norm_g, shift, scale, tag + "_norm_fwd")
    a, u, hid = _ffn_up(h, wg_t, wu_t, tag + "_up", exchange=up_exchange)
    wd = wd() if callable(wd) else wd
    x_out, f, h_next = _mm(hid, wd, "nn", F32, tag + "_down", res=x, gate=gate, aux_dtype=BF16, norm=next_norm,
                           exchange=down_exchange() if down_exchange else None)
    return x_out, (h, a, u, hid, f), h_next


def _ffn_bwd(dx_out, df, x, saved, norm_g, scale, wg_t, wu_t, wd, tag, traffic, below=None, dact_exchange=None,
             dw_exchange=None, finish_reduction=False):
    h, a, u, hid, _ = saved
    f_below, gate_below = below if below else (None, None)
    da, du = _ffn_dact(df, wd, a, u, tag + "_dact", exchange=dact_exchange)
    dwd = _mm(hid, df, "tn", BF16, tag + "_dwd", exchange=dw_exchange() if dw_exchange else None)
    if not finish_reduction:
        dwg_t = _mm(da, h, "tn", BF16, tag + "_dwg")
        dwu_t = _mm(du, h, "tn", BF16, tag + "_dwu")
        dx, dshift, dscale, dnorm_g, *gated = _norm_mod_bwd(
            ([da, du], [wg_t, wu_t]), x, norm_g, scale, dx_out, tag + "_dh_norm_bwd", f=f_below, gate=gate_below,
            exchange=traffic.reduce_sibling(tag, [dwg_t, dwu_t, dwd]))
        traffic.add_halves(tag)
        return dx, (dshift, dscale, dnorm_g), gated
    kd, kg, ku = tag + "_wd", tag + "_wg", tag + "_wu"
    dwg_t = _mm(da, h, "tn", BF16, tag + "_dwg", exchange=traffic.reduce_sibling(kd, [dwd]))
    traffic.add_halves(kd)
    dwu_t = _mm(du, h, "tn", BF16, tag + "_dwu",
                exchange=_join(traffic.reduce_chips(kd), traffic.reduce_sibling(kg, [dwg_t])))
    traffic.add_halves(kg)
    half = x.shape[0] // 2
    top = _norm_mod_bwd(([da, du], [wg_t, wu_t]), x, norm_g, scale, dx_out, tag + "_dh_norm_bwd_top", f=f_below,
                        gate=gate_below, rows=(0, half),
                        exchange=_join(traffic.reduce_chips(kg), traffic.reduce_sibling(ku, [dwu_t])))
    traffic.add_halves(ku)
    traffic.sum_chips(kd)
    traffic.sum_chips(kg)
    bottom = _norm_mod_bwd(([da, du], [wg_t, wu_t]), x, norm_g, scale, dx_out, tag + "_dh_norm_bwd_bottom", f=f_below,
                           gate=gate_below, rows=(half, half),
                           exchange=_join(traffic.reduce_chips(ku), traffic.share(kd), traffic.share(kg)))
    traffic.sum_chips(ku)
    dx, dshift, dscale, dnorm_g, *gated = [jnp.concatenate([a, b]) if a.shape[0] == half else a + b
                                           for a, b in zip(top, bottom)]
    return dx, (dshift, dscale, dnorm_g), gated


def _layer_step(x, target, mod, gains, forget_bias, conv_w, traffic, att_w, in_shard, in_rows):
    sh1, sc1, g1, sh2, sc2, g2, sh3, sc3, g3 = mod
    norm1_g, norm2_g, norm3_g, final_g, group_g = gains
    s, d = x.shape
    n_heads = att_w // HEAD_DIM
    npair = n_heads // 2
    gate1, gate3 = 0.5 * g1, 0.5 * g3

    def split_w_in(w_in_pad):
        w_in_t = w_in_pad.reshape(N_CHIPS, in_rows, d)[:, :in_shard].reshape(N_CHIPS * in_shard, d)
        return (w_in_t[:3 * att_w], _pad_rows(w_in_t[3 * att_w:3 * att_w + n_heads], LANES), w_in_t[3 * att_w + n_heads:])

    wg1_t, wu1_t = traffic.weights("ffn1_gu")

    def wd1_ready():
        _run_exchange(traffic.gather_pass("ffn1_d"), "gather_ffn1_down_pass")
        return traffic.weights("ffn1_d")[0]

    x1, saved1, h2 = _ffn_fwd(x, norm1_g, sh1, sc1, gate1, wg1_t, wu1_t, wd1_ready, "ffn1", (norm2_g, sh2, sc2),
                              up_exchange=_join(traffic.gather_ici("ffn1_d"), traffic.gather_ici("mix_in")),
                              down_exchange=lambda: _join(traffic.gather_pass("mix_in"), traffic.gather_ici("mix_out")))
    wd1 = traffic.weights("ffn1_d")[0]
    wqkv_t, wf_t, wbcx_t = split_w_in(traffic.weights("mix_in")[0])

    qkv = _mm(h2, wqkv_t, "nt", BF16, "mix_proj_qkv", exchange=traffic.gather_pass("mix_out"))
    w_out = traffic.weights("mix_out")[0]
    bcx = _mm(h2, wbcx_t, "nt", F32, "mix_proj_bcx")
    flog = _mm(h2, wf_t, "nt", F32, "mix_proj_f")
    flog_t = jnp.pad(flog[:, :n_heads].T, ((0, HEAD_ROWS - n_heads), (0, 0)))
    bias_col = jnp.pad(forget_bias, (0, HEAD_ROWS - n_heads))[:, None]
    f_pieces = _forget_fwd(flog_t, bias_col, "forget_fwd")
    qa, ka, va = _attn_prep(qkv, f_pieces, "attn_prep")
    att, lse = _attn_fwd(qa, ka, va, "attn_fwd", exchange=traffic.gather_ici("ffn2"))
    cv = _conv_fwd(bcx, conv_w, "conv_fwd")
    yn = _gnorm_fwd(att, cv, group_g, "gnorm_fwd")
    x2, mix, h3 = _mm(yn, w_out, "nn", F32, "mix_out", res=x1, gate=g2, aux_dtype=BF16, norm=(norm3_g, sh3, sc3),
                      exchange=traffic.gather_pass("ffn2"))
    wg2_t, wu2_t, wd2 = traffic.weights("ffn2")

    a3, u3, hid3 = _ffn_up(h3, wg2_t, wu2_t, "ffn2_up")
    saved3 = (h3, a3, u3, hid3, None)
    dx3, loss_row, dfinal_g, df2, dgate3 = _down_final_loss(hid3, wd2, x2, gate3, final_g, target, "ffn2_down_loss")

    dx2, (dsh3, dsc3, dnorm3_g), (dmix, dg2) = _ffn_bwd(
        dx3, df2, x2, saved3, norm3_g, sc3, wg2_t, wu2_t, wd2, "ffn2", traffic, below=(mix, g2))
    dyn = _mm(dmix, w_out, "nt", F32, "mix_out_dyn")
    dw_out = _mm(yn, dmix, "tn", BF16, "mix_out_dw")
    datt, dcv, dgroup_g = _gnorm_bwd(dyn, att, cv, group_g, "gnorm_bwd")
    db, dc, dxc, dconv_w = _conv_bwd(dcv, bcx, conv_w, "conv_bwd")
    dbcx = jnp.concatenate([db, dc, dxc], axis=1)
    dq, dk, dv, qx, kx = _attn_bwd(qa, ka, va, datt, att, lse, "attn_bwd", exchange=traffic.reduce_chips("ffn2"))
    traffic.sum_chips("ffn2")
    dqkv = jnp.concatenate([dq.astype(BF16), dk, dv], axis=1)
    df_t = _decay_grads(qx, kx, "decay_grads")[:, :HEAD_ROWS].T
    dflog_t, dbias_col = _forget_bwd(df_t, flog_t, bias_col, "forget_bwd")
    dflog = jnp.pad(dflog_t[:n_heads].T, ((0, 0), (0, LANES - n_heads))).astype(BF16)
    dwqkv_t = _mm(dqkv, h2, "tn", BF16, "mix_dw_qkv", exchange=traffic.share("ffn2"))
    dwbcx_t = _mm(dbcx, h2, "tn", BF16, "mix_dw_bcx")
    dwf_t = _mm(dflog, h2, "tn", BF16, "mix_dw_f")
    dw_in_t = jnp.concatenate([dwqkv_t, dwf_t[:n_heads], dwbcx_t], axis=0).reshape(N_CHIPS, in_shard, d)
    dw_in_t = jnp.pad(dw_in_t, ((0, 0), (0, in_rows - in_shard), (0, 0))).reshape(N_CHIPS * in_rows, d)
    dx1, dsh2, dsc2, dnorm2_g, df1, dgate1 = _norm_mod_bwd(
        ([dqkv, dbcx, dflog], [wqkv_t, wbcx_t, wf_t]), x1, norm2_g, sc2, dx2, "mix_dh_norm_bwd", f=saved1[4], gate=gate1,
        exchange=traffic.reduce_sibling("mix", [dw_in_t, dw_out]))
    traffic.add_halves("mix")

    def share_mix():
        traffic.sum_chips("mix")
        return traffic.share("mix")

    dx, (dsh1, dsc1, dnorm1_g), _ = _ffn_bwd(
        dx1, df1, x, saved1, norm1_g, sc1, wg1_t, wu1_t, wd1, "ffn1", traffic,
        dact_exchange=traffic.reduce_chips("mix"), dw_exchange=share_mix, finish_reduction=True)

    dmod = [dsh1, dsc1, 0.5 * dgate1, dsh2, dsc2, dg2, dsh3, dsc3, 0.5 * dgate3]
    dgains = [dnorm1_g, dnorm2_g, dnorm3_g, dfinal_g, dgroup_g]
    dbias = dbias_col[:n_heads, 0]
    return dx, loss_row, dmod, dgains, dbias, dconv_w


SMALL_ROWS = 24
ROW_GAINS, ROW_LOSS, ROW_FORGET, ROW_CONV, ROW_MOD = 0, 5, 6, 7, 10
PROW_ADA_B, PROW_GAINS, PROW_FORGET, PROW_CONV = 0, 9, 14, 15


def _round_up(n, m):
    return -(-n // m) * m


def _pad_rows(a, rows):
    return jnp.pad(a, ((0, rows - a.shape[0]), (0, 0)))


def _halves(a):
    return a.reshape(2, a.shape[0] // 2, a.shape[1])


def _rows_at(a, r0, total, width):
    return jnp.pad(a, ((r0, total - r0 - a.shape[0]), (0, width - a.shape[1])))


def kernel(x, c, ada_w, ada_b, norm1_g, ffn1_w_gate, ffn1_w_up, ffn1_w_down, norm2_g, w_in, forget_bias, conv_w, group_norm_g, w_out, norm3_g, ffn2_w_gate, ffn2_w_up, ffn2_w_down, final_g, loss_target, m_ada_w, m_ada_b, m_norm1_g, m_ffn1_w_gate, m_ffn1_w_up, m_ffn1_w_down, m_norm2_g, m_w_in, m_forget_bias, m_conv_w, m_group_norm_g, m_w_out, m_norm3_g, m_ffn2_w_gate, m_ffn2_w_up, m_ffn2_w_down, m_final_g, v_ada_w, v_ada_b, v_norm1_g, v_ffn1_w_gate, v_ffn1_w_up, v_ffn1_w_down, v_norm2_g, v_w_in, v_forget_bias, v_conv_w, v_group_norm_g, v_w_out, v_norm3_g, v_ffn2_w_gate, v_ffn2_w_up, v_ffn2_w_down, v_final_g):
    xi, yi, ci = lax.axis_index("x"), lax.axis_index("y"), lax.axis_index("c")
    chip = 2 * xi + yi
    dev = 4 * xi + 2 * yi + ci
    _, s, d = x.shape
    att_w = d // 2
    conv_width = d - att_w
    n_heads = att_w // HEAD_DIM
    in_shard = w_in.shape[1]
    in_rows = _round_up(in_shard, 32)
    cs = conv_w.shape[1]
    mod_shard = ada_w.shape[1]
    assert N_MOD * d == N_CHIPS * mod_shard and conv_width == N_CHIPS * cs and n_heads % 2 == 0

    def t_bf(w):
        return w.T.astype(BF16)

    shards = {"ffn1_gu": [_halves(t_bf(ffn1_w_gate)), _halves(t_bf(ffn1_w_up))], "ffn1_d": [_halves(ffn1_w_down.astype(BF16))],
              "mix_in": [_halves(_pad_rows(t_bf(w_in), in_rows))], "mix_out": [_halves(w_out.astype(BF16))],
              "ffn2": [_halves(t_bf(ffn2_w_gate)), _halves(t_bf(ffn2_w_up)), _halves(ffn2_w_down.astype(BF16))]}
    core = ci.astype(jnp.int32).reshape(1)
    chip_arr = chip.astype(jnp.int32).reshape(1)
    traffic = _WeightTraffic(shards, core, chip_arr)

    cond = _small_gather_exchange(_rows_at(c, 0, 8, d) + _rows_at(conv_w, 1, 8, d))
    _run_exchange(_join(traffic.gather_ici("ffn1_gu"), cond), "gather_ffn1_ici")
    got0 = cond.results[0].reshape(N_DEV, 8, d)
    c16 = _pad_rows(got0[:, 0, :], 16)
    conv_full = got0[0::2, 1:1 + CONV_K, :cs].transpose(1, 0, 2).reshape(CONV_K, conv_width)

    ada_b_mine = lax.dynamic_slice(ada_b, (chip * mod_shard,), (mod_shard,))[None, :]
    mods = _small_gather_exchange(_ada_fwd(c16, ada_w, ada_b_mine, "ada_fwd"))
    _run_exchange(_join(traffic.gather_pass("ffn1_gu"), mods), "gather_ffn1_pass")
    got1 = mods.results[0].reshape(N_DEV, 16, mod_shard)
    mod_mine = lax.dynamic_index_in_dim(got1[0::2], dev, axis=1, keepdims=False).reshape(N_MOD, d)
    mod = [mod_mine[i:i + 1] for i in range(N_MOD)]

    gains = [g[None, :] for g in (norm1_g, norm2_g, norm3_g, final_g, group_norm_g)]
    dx, loss_row, dmod, dgains, dbias, dconv_w = _layer_step(
        x[0], loss_target[0], mod, gains, forget_bias, conv_full, traffic, att_w, in_shard, in_rows)

    pack = sum(_rows_at(g, ROW_GAINS + i, SMALL_ROWS, d) for i, g in enumerate(dgains))
    pack += _rows_at(loss_row, ROW_LOSS, SMALL_ROWS, d) + _rows_at(dbias[None, :], ROW_FORGET, SMALL_ROWS, d)
    pack += _rows_at(dconv_w, ROW_CONV, SMALL_ROWS, d)
    pack += sum(_rows_at(g, ROW_MOD + i, SMALL_ROWS, d) for i, g in enumerate(dmod))
    small = _small_gather_exchange(pack)
    _run_exchange(_join(traffic.share("ffn1_wu"), small), "gather_small_grads")
    got2 = small.results[0].reshape(N_DEV, SMALL_ROWS, d)
    tot = _sum_devices(got2, "sum_small_grads")
    loss = tot[ROW_LOSS, 0]
    grad_ada_b = tot[ROW_MOD:ROW_MOD + N_MOD].reshape(N_MOD * d)
    grad_conv = lax.dynamic_slice(tot[ROW_CONV:ROW_CONV + CONV_K], (0, chip * cs), (CONV_K, cs))
    dmod_all = got2[:, ROW_MOD:ROW_MOD + N_MOD, :].reshape(N_DEV, N_MOD * d)
    dmod16 = _pad_rows(lax.dynamic_slice(dmod_all, (0, chip * mod_shard), (N_DEV, mod_shard)), 16)

    out = {"ada_w": tuple(_ada_update(c16.T, dmod16, ada_w, m_ada_w, v_ada_w, "adamw_ada_w"))}
    totals = (traffic.totals("ffn1_wg") + traffic.totals("ffn1_wu") + traffic.totals("ffn1_wd")
              + traffic.totals("mix") + traffic.totals("ffn2"))

    names = ("ffn1_w_gate", "ffn1_w_up", "ffn1_w_down", "w_in", "w_out", "ffn2_w_gate", "ffn2_w_up", "ffn2_w_down")
    transposed = ("ffn1_w_gate", "ffn1_w_up", "w_in", "ffn2_w_gate", "ffn2_w_up")
    params = {"ffn1_w_gate": (ffn1_w_gate, m_ffn1_w_gate, v_ffn1_w_gate), "ffn1_w_up": (ffn1_w_up, m_ffn1_w_up, v_ffn1_w_up),
              "ffn1_w_down": (ffn1_w_down, m_ffn1_w_down, v_ffn1_w_down), "w_in": (w_in, m_w_in, v_w_in),
              "w_out": (w_out, m_w_out, v_w_out), "ffn2_w_gate": (ffn2_w_gate, m_ffn2_w_gate, v_ffn2_w_gate),
              "ffn2_w_up": (ffn2_w_up, m_ffn2_w_up, v_ffn2_w_up), "ffn2_w_down": (ffn2_w_down, m_ffn2_w_down, v_ffn2_w_down)}
    for name_, (mine, theirs) in zip(names, totals):
        w, m, v = params[name_]
        if name_ in transposed:
            w, m, v = w.T, m.T, v.T
        if name_ == "w_in":
            both = jnp.where(ci == 0, jnp.concatenate([mine, theirs]), jnp.concatenate([theirs, mine]))[:in_shard]
            res = (both,) + tuple(_adamw(w, both, m, v, "adamw_" + name_))
        else:
            res = _adamw_halves(w, mine, theirs, m, v, core, "adamw_" + name_)
        out[name_] = tuple(r.T for r in res) if name_ in transposed else tuple(res)

    def small_pack(ada_b_, gains_, forget_, conv_):
        p = _rows_at(ada_b_.reshape(N_MOD, d), PROW_ADA_B, SMALL_ROWS, d)
        p += sum(_rows_at(g[None, :], PROW_GAINS + i, SMALL_ROWS, d) for i, g in enumerate(gains_))
        p += _rows_at(forget_[None, :], PROW_FORGET, SMALL_ROWS, d) + _rows_at(conv_, PROW_CONV, SMALL_ROWS, d)
        return p

    g_gains = [tot[ROW_GAINS + i] for i in range(5)]
    g_forget = tot[ROW_FORGET, :n_heads]
    sw = small_pack(ada_b, (norm1_g, norm2_g, norm3_g, final_g, group_norm_g), forget_bias, conv_w)
    sm = small_pack(m_ada_b, (m_norm1_g, m_norm2_g, m_norm3_g, m_final_g, m_group_norm_g), m_forget_bias, m_conv_w)
    sv = small_pack(v_ada_b, (v_norm1_g, v_norm2_g, v_norm3_g, v_final_g, v_group_norm_g), v_forget_bias, v_conv_w)
    sg = small_pack(grad_ada_b, g_gains, g_forget, grad_conv)
    small = (sg,) + tuple(_adamw(sw, sg, sm, sv, "adamw_small"))

    def unpack(p):
        r = {"ada_b": p[PROW_ADA_B:PROW_ADA_B + N_MOD].reshape(N_MOD * d), "forget_bias": p[PROW_FORGET, :n_heads],
             "conv_w": p[PROW_CONV:PROW_CONV + CONV_K, :cs]}
        for i, nm in enumerate(("norm1_g", "norm2_g", "norm3_g", "final_g", "group_norm_g")):
            r[nm] = p[PROW_GAINS + i]
        return r

    small = [unpack(p) for p in small]
    order = ("ada_w", "ada_b", "norm1_g", "ffn1_w_gate", "ffn1_w_up", "ffn1_w_down", "norm2_g", "w_in", "forget_bias",
             "conv_w", "group_norm_g", "w_out", "norm3_g", "ffn2_w_gate", "ffn2_w_up", "ffn2_w_down", "final_g")
    result = [loss, dx[None]]
    for k in range(4):
        result += [out[nm][k] if nm in out else small[k][nm] for nm in order]
    return tuple(result)
```

```python
import functools
import math

import jax
import jax.numpy as jnp
from jax import lax
from jax.experimental import pallas as pl
from jax.experimental.pallas import tpu as pltpu

F32 = jnp.float32
BF16 = jnp.bfloat16

HEAD_DIM = 64
CONV_K = 3
N_MOD = 9
EPS = 1e-6
ADAM_LR = 0.001
ADAM_B1 = 0.9
ADAM_B2 = 0.999
ADAM_EPS = 1e-08
ADAM_WD = 0.01
ADAM_STEP = 10

LANES = 128
N_CHIPS = 4
N_DEV = 8
VMEM_LIMIT_BYTES = 56 * 1024 * 1024
MAX_CONTRACTION = 4096
NEG_BIG = -1e30
MESH = pl.DeviceIdType.MESH

_NT = (((1,), (1,)), ((), ()))
_NN = (((1,), (0,)), ((), ()))
_TN = (((0,), (0,)), ((), ()))


def _params(*sem):
    return pltpu.CompilerParams(dimension_semantics=sem, vmem_limit_bytes=VMEM_LIMIT_BYTES)


class _Exchange:
    def __init__(self, inputs, out_shapes, n_sems, start, finish, aliases=None):
        self.inputs, self.out_shapes, self.n_sems = list(inputs), list(out_shapes), n_sems
        self.start, self.finish, self.aliases = start, finish, dict(aliases or {})
        self.results = None

    def set_results(self, results):
        self.results = list(results)


class _SemaphoreWindow:
    def __init__(self, sems, base):
        self.sems, self.base = sems, base
        self.at = self

    def __getitem__(self, k):
        return self.sems.at[self.base + k]


class _JoinedExchange(_Exchange):
    def __init__(self, parts):
        self.parts = parts
        aliases, i0, o0 = {}, 0, 0
        for p in parts:
            aliases.update({i0 + a: o0 + b for a, b in p.aliases.items()})
            i0, o0 = i0 + len(p.inputs), o0 + len(p.out_shapes)

        def each(method, src, dst, send_sems, recv_sems):
            i0 = o0 = s0 = 0
            for p in parts:
                i1, o1 = i0 + len(p.inputs), o0 + len(p.out_shapes)
                getattr(p, method)(src[i0:i1], dst[o0:o1], _SemaphoreWindow(send_sems, s0), _SemaphoreWindow(recv_sems, s0))
                i0, o0, s0 = i1, o1, s0 + p.n_sems

        super().__init__([a for p in parts for a in p.inputs], [o for p in parts for o in p.out_shapes],
                         sum(p.n_sems for p in parts), functools.partial(each, "start"), functools.partial(each, "finish"),
                         aliases)

    def set_results(self, results):
        o0 = 0
        for p in self.parts:
            p.set_results(results[o0:o0 + len(p.out_shapes)])
            o0 += len(p.out_shapes)


def _join(*parts):
    return parts[0] if len(parts) == 1 else _JoinedExchange(list(parts))


def _pc(body, exchange=None, **kw):
    if exchange is None:
        return pl.pallas_call(body, **kw)
    grid = kw["grid"]
    single = not isinstance(kw["out_shape"], (tuple, list))
    out_shape = [kw["out_shape"]] if single else list(kw["out_shape"])
    out_specs = [kw["out_specs"]] if single else list(kw["out_specs"])
    in_specs = list(kw["in_specs"])
    scratch = list(kw.get("scratch_shapes", ()))
    n_in, n_out, n_scr = len(in_specs), len(out_shape), len(scratch)
    n_xi, n_xo = len(exchange.inputs), len(exchange.out_shapes)

    def wrapped(*refs):
        pos = [n_in, n_in + n_xi, n_in + n_xi + n_out, n_in + n_xi + n_out + n_xo]
        ins, x_in, outs, x_out = refs[:pos[0]], refs[pos[0]:pos[1]], refs[pos[1]:pos[2]], refs[pos[2]:pos[3]]
        scr = refs[pos[3]:pos[3] + n_scr]
        send_sems, recv_sems = refs[pos[3] + n_scr:]
        ids = [pl.program_id(a) for a in range(len(grid))]
        first = functools.reduce(jnp.logical_and, [i == 0 for i in ids])
        last = functools.reduce(jnp.logical_and, [i == g - 1 for i, g in zip(ids, grid)])

        @pl.when(first)
        def _():
            exchange.start(x_in, x_out, send_sems, recv_sems)

        body(*ins, *outs, *scr)

        @pl.when(last)
        def _():
            exchange.finish(x_in, x_out, send_sems, recv_sems)

    call = pl.pallas_call(
        wrapped, out_shape=tuple(out_shape) + tuple(exchange.out_shapes), grid=grid,
        in_specs=in_specs + [_ANY] * n_xi, out_specs=tuple(out_specs) + (_ANY,) * n_xo,
        scratch_shapes=scratch + [pltpu.SemaphoreType.DMA((exchange.n_sems,)), pltpu.SemaphoreType.DMA((exchange.n_sems,))],
        input_output_aliases={n_in + a: n_out + b for a, b in exchange.aliases.items()},
        compiler_params=_params(*(["arbitrary"] * len(grid))), name=kw["name"])

    def run(*args):
        res = call(*args, *exchange.inputs)
        exchange.set_results(res[n_out:])
        return res[0] if single else tuple(res[:n_out])

    return run


_ANY = pl.BlockSpec(memory_space=pl.ANY)


def _tile(n, pref, mult):
    best = None
    t = mult
    while t <= min(n, pref):
        if n % t == 0:
            best = t
        t += mult
    return n if best is None else best


def _sds(shape, dtype):
    return jax.ShapeDtypeStruct(shape, dtype)


def _vec_spec(d):
    return pl.BlockSpec((1, d), lambda *_: (0, 0))


def _norm_mod_fwd(x, g, shift, scale, name):
    s, d = x.shape
    tr = _tile(s, 512, 16)

    def body(x_ref, g_ref, sh_ref, sc_ref, h_ref):
        xv = x_ref[...]
        rstd = lax.rsqrt(jnp.mean(xv * xv, axis=-1, keepdims=True) + EPS)
        n = xv * rstd * g_ref[...]
        h_ref[...] = (n * (1.0 + sc_ref[...]) + sh_ref[...]).astype(BF16)

    row = pl.BlockSpec((tr, d), lambda i: (i, 0))
    return _pc(body, out_shape=_sds((s, d), BF16), grid=(s // tr,),
               in_specs=[row, _vec_spec(d), _vec_spec(d), _vec_spec(d)], out_specs=row,
               compiler_params=_params("parallel"), name=name)(x, g, shift, scale)


def _through_gate(dx, f_ref, gate_ref, df_ref, dgate_ref):
    df_ref[...] = (dx * gate_ref[...]).astype(BF16)
    dgate_ref[...] += jnp.sum(dx * f_ref[...].astype(F32), axis=0, keepdims=True)


def _norm_mod_bwd(dh, x, g, scale, dres, name, f=None, gate=None, rows=None, exchange=None):
    d = x.shape[1]
    first_row, s = rows if rows else (0, x.shape[0])
    gated = f is not None
    terms = list(zip(*dh)) if isinstance(dh, tuple) else None
    tr = _tile(s, 256, 16)
    b0 = first_row // tr
    assert first_row % tr == 0
    n_lead = 2 * len(terms) if terms else 1

    def body(*refs):
        lead, (x_ref, g_ref, sc_ref, dres_ref), rest = refs[:n_lead], refs[n_lead:n_lead + 4], refs[n_lead + 4:]
        f_ref, gate_ref = rest[:2] if gated else (None, None)
        dx_ref, dsh_ref, dsc_ref, dg_ref = rest[2:6] if gated else rest[:4]
        df_ref, dgate_ref = rest[6:8] if gated else (None, None)

        @pl.when(pl.program_id(0) == 0)
        def _():
            for ref in (dsh_ref, dsc_ref, dg_ref) + ((dgate_ref,) if gated else ()):
                ref[...] = jnp.zeros_like(ref)

        if terms:
            dhv = lax.dot_general(lead[0][...], lead[1][...], _NN, preferred_element_type=F32)
            for p in range(1, len(terms)):
                dhv += lax.dot_general(lead[2 * p][...], lead[2 * p + 1][...], _NN, preferred_element_type=F32)
        else:
            dhv = lead[0][...]
        xv = x_ref[...]
        gv = g_ref[...]
        rstd = lax.rsqrt(jnp.mean(xv * xv, axis=-1, keepdims=True) + EPS)
        xhat = xv * rstd
        dn = dhv * (1.0 + sc_ref[...])
        dsh_ref[...] += jnp.sum(dhv, axis=0, keepdims=True)
        dsc_ref[...] += jnp.sum(dhv * (xhat * gv), axis=0, keepdims=True)
        dg_ref[...] += jnp.sum(dn * xhat, axis=0, keepdims=True)
        dxh = dn * gv
        proj = jnp.mean(dxh * xhat, axis=-1, keepdims=True)
        dx = dres_ref[...] + rstd * (dxh - xhat * proj)
        dx_ref[...] = dx
        if gated:
            _through_gate(dx, f_ref, gate_ref, df_ref, dgate_ref)

    row = pl.BlockSpec((tr, d), lambda i: (b0 + i, 0))
    out_row = pl.BlockSpec((tr, d), lambda i: (i, 0))
    vec = _vec_spec(d)
    if terms:
        in_specs, args = [], []
        for l, r in terms:
            assert l.shape[1] == r.shape[0] <= MAX_CONTRACTION and r.shape[1] == d
            in_specs += [pl.BlockSpec((tr, l.shape[1]), lambda i: (b0 + i, 0)), pl.BlockSpec(r.shape, lambda i: (0, 0))]
            args += [l, r]
    else:
        in_specs, args = [row], [dh]
    in_specs += [row, vec, vec, row]
    args += [x, g, scale, dres]
    out_shape = [_sds((s, d), F32), _sds((1, d), F32), _sds((1, d), F32), _sds((1, d), F32)]
    out_specs = [out_row, vec, vec, vec]
    if gated:
        out_shape += [_sds((s, d), BF16), _sds((1, d), F32)]
        out_specs += [out_row, vec]
        in_specs += [row, vec]
        args += [f, gate]
    return _pc(body, exchange, out_shape=tuple(out_shape), grid=(s // tr,), in_specs=in_specs,
               out_specs=tuple(out_specs), compiler_params=_params("arbitrary"), name=name)(*args)


def _down_final_loss(hid, wd, res, gate, g, target, name):
    s, d = res.shape
    k = hid.shape[1]
    assert k <= MAX_CONTRACTION
    tr = _tile(s, 256, 16)
    nsteps = s // tr

    def body(hid_ref, wd_ref, res_ref, gate_ref, g_ref, t_ref, dx_ref, loss_ref, dg_ref, df_ref, dgate_ref):
        i = pl.program_id(0)

        @pl.when(i == 0)
        def _():
            loss_ref[...] = jnp.zeros_like(loss_ref)
            dg_ref[...] = jnp.zeros_like(dg_ref)
            dgate_ref[...] = jnp.zeros_like(dgate_ref)

        f = lax.dot_general(hid_ref[...], wd_ref[...], _NN, preferred_element_type=F32)
        gatev = gate_ref[...]
        xv = res_ref[...] + gatev * f
        gv = g_ref[...]
        rstd = lax.rsqrt(jnp.mean(xv * xv, axis=-1, keepdims=True) + EPS)
        xhat = xv * rstd
        err = xhat * gv - t_ref[...]
        dy = err * (1.0 / d)
        loss_ref[...] += jnp.sum(0.5 * err * dy, axis=0, keepdims=True)
        dg_ref[...] += jnp.sum(dy * xhat, axis=0, keepdims=True)
        dxh = dy * gv
        proj = jnp.mean(dxh * xhat, axis=-1, keepdims=True)
        dx = rstd * (dxh - xhat * proj)
        dx_ref[...] = dx
        df_ref[...] = (dx * gatev).astype(BF16)
        dgate_ref[...] += jnp.sum(dx * f, axis=0, keepdims=True)

        @pl.when(i == nsteps - 1)
        def _():
            loss_ref[...] = jnp.broadcast_to(jnp.sum(loss_ref[...], axis=-1, keepdims=True), loss_ref.shape)

    row = pl.BlockSpec((tr, d), lambda i: (i, 0))
    vec = _vec_spec(d)
    return _pc(body, out_shape=(_sds((s, d), F32), _sds((1, d), F32), _sds((1, d), F32), _sds((s, d), BF16), _sds((1, d), F32)),
               grid=(nsteps,),
               in_specs=[pl.BlockSpec((tr, k), lambda i: (i, 0)), pl.BlockSpec((k, d), lambda i: (0, 0)), row, vec, vec, row],
               out_specs=(row, vec, vec, row, vec),
               compiler_params=_params("arbitrary"), name=name)(hid, wd, res, gate, g, target)


def _mm(lhs, rhs, dims, out_dtype, name, res=None, gate=None, aux_dtype=None, norm=None, exchange=None):
    lhs_list = list(lhs) if isinstance(lhs, (list, tuple)) else [lhs]
    rhs_list = list(rhs) if isinstance(rhs, (list, tuple)) else [rhs]
    n_terms = len(lhs_list)
    assert n_terms == len(rhs_list)
    m = lhs_list[0].shape[1 if dims == "tn" else 0]
    n = rhs_list[0].shape[0 if dims == "nt" else 1]
    tn = _tile(n, 1024, LANES)
    tm = _tile(m, 512, LANES if dims == "tn" else 16)
    dn = {"nn": _NN, "nt": _NT, "tn": _TN}[dims]
    in_specs, args = [], []
    for l, r in zip(lhs_list, rhs_list):
        k = l.shape[0 if dims == "tn" else 1]
        assert k == r.shape[1 if dims == "nt" else 0] and k <= MAX_CONTRACTION, (l.shape, r.shape, dims)
        in_specs.append(pl.BlockSpec((k, tm), lambda i, j: (0, i)) if dims == "tn" else pl.BlockSpec((tm, k), lambda i, j: (i, 0)))
        in_specs.append(pl.BlockSpec((tn, k), lambda i, j: (j, 0)) if dims == "nt" else pl.BlockSpec((k, tn), lambda i, j: (0, j)))
        args += [l, r]
    out_spec = pl.BlockSpec((tm, tn), lambda i, j: (i, j))
    has_res, has_gate, has_aux, has_norm = res is not None, gate is not None, aux_dtype is not None, norm is not None
    assert not has_norm or tn == n

    def body(*refs):
        refs = list(refs)
        pos = 2 * n_terms
        res_ref = gate_ref = aux_ref = None
        if has_res:
            res_ref = refs[pos]; pos += 1
        if has_gate:
            gate_ref = refs[pos]; pos += 1
        if has_norm:
            ng_ref, nsh_ref, nsc_ref = refs[pos:pos + 3]; pos += 3
        out_ref = refs[pos]; pos += 1
        if has_aux:
            aux_ref = refs[pos]; pos += 1
        acc = lax.dot_general(refs[0][...], refs[1][...], dn, preferred_element_type=F32)
        for p in range(1, n_terms):
            acc += lax.dot_general(refs[2 * p][...], refs[2 * p + 1][...], dn, preferred_element_type=F32)
        if has_aux:
            aux_ref[...] = acc.astype(aux_dtype)
        if has_gate:
            acc = acc * gate_ref[...]
        if has_res:
            acc = res_ref[...] + acc
        out_ref[...] = acc.astype(out_dtype)
        if has_norm:
            rstd = lax.rsqrt(jnp.mean(acc * acc, axis=-1, keepdims=True) + EPS)
            refs[pos][...] = (acc * rstd * ng_ref[...] * (1.0 + nsc_ref[...]) + nsh_ref[...]).astype(BF16)

    if has_res:
        in_specs.append(out_spec); args.append(res)
    if has_gate:
        in_specs.append(pl.BlockSpec((1, tn), lambda i, j: (0, j))); args.append(gate)
    if has_norm:
        in_specs += [pl.BlockSpec((1, tn), lambda i, j: (0, j))] * 3
        args += list(norm)
    out_shape = [_sds((m, n), out_dtype)]
    out_specs = [out_spec]
    if has_aux:
        out_shape.append(_sds((m, n), aux_dtype)); out_specs.append(out_spec)
    if has_norm:
        out_shape.append(_sds((m, n), BF16)); out_specs.append(out_spec)
    outs = _pc(body, exchange, out_shape=tuple(out_shape), grid=(m // tm, n // tn), in_specs=in_specs,
               out_specs=tuple(out_specs), compiler_params=_params("parallel", "parallel"), name=name)(*args)
    return outs if len(out_shape) > 1 else outs[0]


def _project(h, weights_t, out_dtypes, name, exchange=None):
    s, d = h.shape
    n = len(weights_t)
    tm = _tile(s, 512, 16)

    def body(*refs):
        hv = refs[0][...]
        for i in range(n):
            acc = lax.dot_general(hv, refs[1 + i][...], _NT, preferred_element_type=F32)
            refs[1 + n + i][...] = acc.astype(out_dtypes[i])

    return _pc(body, exchange, out_shape=tuple(_sds((s, w.shape[0]), dt) for w, dt in zip(weights_t, out_dtypes)),
               grid=(s // tm,),
               in_specs=[pl.BlockSpec((tm, d), lambda i: (i, 0))] + [pl.BlockSpec(w.shape, lambda i: (0, 0)) for w in weights_t],
               out_specs=tuple(pl.BlockSpec((tm, w.shape[0]), lambda i: (i, 0)) for w in weights_t),
               compiler_params=_params("parallel"), name=name)(h, *weights_t)


def _ffn_up(h, wg_t, wu_t, name, exchange=None):
    s, d = h.shape
    f = wg_t.shape[0]
    tm = _tile(s, 1024, 16)
    tn = _tile(f, 256, LANES)

    def body(h_ref, wg_ref, wu_ref, a_ref, u_ref, hid_ref):
        hv = h_ref[...]
        a = lax.dot_general(hv, wg_ref[...], _NT, preferred_element_type=F32)
        u = lax.dot_general(hv, wu_ref[...], _NT, preferred_element_type=F32)
        a_ref[...] = a.astype(BF16)
        u_ref[...] = u.astype(BF16)
        hid_ref[...] = (a * jax.nn.sigmoid(a) * u).astype(BF16)

    hs = pl.BlockSpec((tm, d), lambda i, j: (i, 0))
    ws = pl.BlockSpec((tn, d), lambda i, j: (j, 0))
    os_ = pl.BlockSpec((tm, tn), lambda i, j: (i, j))
    return _pc(body, exchange, out_shape=(_sds((s, f), BF16),) * 3, grid=(s // tm, f // tn),
               in_specs=[hs, ws, ws], out_specs=(os_, os_, os_),
               compiler_params=_params("parallel", "parallel"), name=name)(h, wg_t, wu_t)


def _ffn_dact(df, wd, a, u, name, exchange=None):
    s, d = df.shape
    f = wd.shape[0]
    tm = _tile(s, 1024, 16)
    tn = _tile(f, 256, LANES)

    def body(df_ref, wd_ref, a_ref, u_ref, da_ref, du_ref):
        dhid = lax.dot_general(df_ref[...], wd_ref[...], _NT, preferred_element_type=F32)
        av = a_ref[...].astype(F32)
        uv = u_ref[...].astype(F32)
        sig = jax.nn.sigmoid(av)
        da_ref[...] = (dhid * uv * (sig * (1.0 + av * (1.0 - sig)))).astype(BF16)
        du_ref[...] = (dhid * (av * sig)).astype(BF16)

    ds_ = pl.BlockSpec((tm, d), lambda i, j: (i, 0))
    ws = pl.BlockSpec((tn, d), lambda i, j: (j, 0))
    os_ = pl.BlockSpec((tm, tn), lambda i, j: (i, j))
    return _pc(body, exchange, out_shape=(_sds((s, f), BF16),) * 2, grid=(s // tm, f // tn),
               in_specs=[ds_, ws, os_, os_], out_specs=(os_, os_),
               compiler_params=_params("parallel", "parallel"), name=name)(df, wd, a, u)


def _split3(v):
    hi = v.astype(BF16)
    r1 = v - hi.astype(F32)
    mid = r1.astype(BF16)
    lo = (r1 - mid.astype(F32)).astype(BF16)
    return hi, mid, lo


def _dot3(v, mat):
    hi, mid, lo = _split3(v)
    out = lax.dot_general(hi, mat, _NN, preferred_element_type=F32)
    out += lax.dot_general(mid, mat, _NN, preferred_element_type=F32)
    out += lax.dot_general(lo, mat, _NN, preferred_element_type=F32)
    return out


def _forget_fwd(flog_t, bias, name):
    h, s = flog_t.shape
    blk = _tile(s, 512, LANES)
    tri = (jnp.arange(blk)[:, None] <= jnp.arange(blk)[None, :]).astype(BF16)

    def body(z_ref, b_ref, tri_ref, f_ref, carry):
        @pl.when(pl.program_id(0) == 0)
        def _():
            carry[...] = jnp.zeros_like(carry)

        z = z_ref[...] + b_ref[...]
        e = jnp.exp(-jnp.abs(z))
        w = 1.0 + e
        log1p_e = jnp.where(w == 1.0, e, jnp.log(w) * (e / (w - 1.0)))
        lf = jnp.minimum(z, 0.0) - log1p_e
        out = carry[...] + _dot3(lf, tri_ref[...])
        for j, piece in enumerate(_split3(out)):
            f_ref[j] = piece
        carry[...] = out[:, blk - 1:blk]

    zs = pl.BlockSpec((h, blk), lambda i: (0, i))
    return _pc(body, out_shape=_sds((3, h, s), BF16), grid=(s // blk,),
               in_specs=[zs, pl.BlockSpec((h, 1), lambda i: (0, 0)), pl.BlockSpec((blk, blk), lambda i: (0, 0))],
               out_specs=pl.BlockSpec((3, h, blk), lambda i: (0, 0, i)), scratch_shapes=[pltpu.VMEM((h, 1), F32)],
               compiler_params=_params("arbitrary"), name=name)(flog_t, bias, tri)


def _forget_bwd(df_t, flog_t, bias, name):
    h, s = flog_t.shape
    blk = _tile(s, 512, LANES)
    nb = s // blk
    tri = (jnp.arange(blk)[:, None] >= jnp.arange(blk)[None, :]).astype(BF16)

    def body(df_ref, z_ref, b_ref, tri_ref, dz_ref, db_ref, carry):
        @pl.when(pl.program_id(0) == 0)
        def _():
            carry[...] = jnp.zeros_like(carry)
            db_ref[...] = jnp.zeros_like(db_ref)

        rc = carry[...] + _dot3(df_ref[...], tri_ref[...])
        carry[...] = rc[:, 0:1]
        dz = rc * jax.nn.sigmoid(-(z_ref[...] + b_ref[...]))
        dz_ref[...] = dz
        db_ref[...] += jnp.sum(dz, axis=-1, keepdims=True)

    rev = pl.BlockSpec((h, blk), lambda i: (0, nb - 1 - i))
    col = pl.BlockSpec((h, 1), lambda i: (0, 0))
    return _pc(body, out_shape=(_sds((h, s), F32), _sds((h, 1), F32)), grid=(nb,),
               in_specs=[rev, rev, col, pl.BlockSpec((blk, blk), lambda i: (0, 0))],
               out_specs=(rev, col), scratch_shapes=[pltpu.VMEM((h, 1), F32)],
               compiler_params=_params("arbitrary"), name=name)(df_t, flog_t, bias, tri)


def _attn_tiles(s):
    return _tile(s, 1024, LANES)


def _attn_half(t):
    return t // 2 if t >= 4 * LANES else t


BIAS_ROWS = 16


def _attn_prep(qkv, f_pieces, name):
    s = qkv.shape[0]
    a_w = qkv.shape[1] // 3
    npair = a_w // LANES
    t = _attn_tiles(s)
    scale = 1.0 / math.sqrt(HEAD_DIM)

    six = f_pieces[:, :2 * npair].reshape(3, npair, 2, s).transpose(1, 3, 2, 0).reshape(npair, s, 6)
    feat = jnp.concatenate([six, jnp.ones((npair, s, 1), BF16), jnp.zeros((npair, s, BIAS_ROWS - 7), BF16)], axis=-1)
    place_q = [[0.0] * (2 * LANES) for _ in range(BIAS_ROWS)]
    place_k = [[0.0] * (2 * LANES) for _ in range(BIAS_ROWS)]
    for hh in range(2):
        b0 = hh * LANES + (HEAD_DIM if hh == 0 else 0)
        for j in range(3):
            place_q[3 * hh + j][b0 + j] = 1.0
            place_q[6][b0 + 3 + j] = 1.0
            place_k[6][b0 + j] = 1.0
            place_k[3 * hh + j][b0 + 3 + j] = -1.0
    place_q = jnp.array(place_q, BF16)
    place_k = jnp.array(place_k, BF16)

    def body(q_ref, k_ref, v_ref, f_ref, pq_ref, pk_ref, qa_ref, ka_ref, va_ref):
        lane = lax.broadcasted_iota(jnp.int32, (1, LANES), 1)
        q2 = (q_ref[...].astype(F32) * scale).astype(BF16)
        k2, v2 = k_ref[...], v_ref[...]
        qx = lax.dot_general(f_ref[0], pq_ref[...], _NN, preferred_element_type=F32).astype(BF16)
        kx = lax.dot_general(f_ref[0], pk_ref[...], _NN, preferred_element_type=F32).astype(BF16)
        for hh in range(2):
            real = (lane < HEAD_DIM) if hh == 0 else (lane >= HEAD_DIM)
            cols = slice(hh * LANES, (hh + 1) * LANES)
            qa_ref[:, cols] = jnp.where(real, q2, qx[:, cols])
            ka_ref[:, cols] = jnp.where(real, k2, kx[:, cols])
            va_ref[:, cols] = jnp.where(real, v2, jnp.zeros_like(v2))

    def col(off):
        return pl.BlockSpec((t, LANES), lambda p, i: (i, off + p))

    out = pl.BlockSpec((t, 2 * LANES), lambda p, i: (i, p))
    place = pl.BlockSpec((BIAS_ROWS, 2 * LANES), lambda p, i: (0, 0))
    return _pc(body, out_shape=(_sds((s, 2 * a_w), BF16),) * 3, grid=(npair, s // t),
               in_specs=[col(0), col(npair), col(2 * npair), pl.BlockSpec((1, t, BIAS_ROWS), lambda p, i: (p, i, 0)),
                         place, place],
               out_specs=(out, out, out), compiler_params=_params("parallel", "parallel"), name=name)(
                   qkv, qkv, qkv, feat, place_q, place_k)


def _attn_fwd(qa, ka, va, name, exchange=None):
    s = qa.shape[0]
    a_w = qa.shape[1] // 2
    npair = a_w // LANES
    t = _attn_tiles(s)
    nq = s // t
    half = _attn_half(t)

    def body(q_ref, k_ref, v_ref, o_ref, lse_ref, m_sc, l_sc, acc_sc):
        qi = pl.program_id(1)
        first = lax.broadcasted_iota(jnp.int32, (1, LANES), 1) < HEAD_DIM
        m_sc[...] = jnp.full_like(m_sc, NEG_BIG)
        l_sc[...] = jnp.zeros_like(l_sc)
        acc_sc[...] = jnp.zeros_like(acc_sc)

        def step(q0, k_start, size, diag):
            q_sl = slice(q0, q0 + size)
            k_rows = pl.ds(pl.multiple_of(k_start, size), size)
            m_old = m_sc[q_sl, :]
            keep = None
            if diag:
                keep = (lax.broadcasted_iota(jnp.int32, (size, size), 0) >= lax.broadcasted_iota(jnp.int32, (size, size), 1))
            m_new, rs, pv = [], [], []
            for hh in range(2):
                cols = slice(hh * LANES, (hh + 1) * LANES)
                sc = lax.dot_general(q_ref[q_sl, cols], k_ref[k_rows, cols], _NT, preferred_element_type=F32)
                if diag:
                    sc = jnp.where(keep, sc, NEG_BIG)
                mo = m_old[:, hh * HEAD_DIM:hh * HEAD_DIM + 1]
                mn = jnp.maximum(mo, jnp.max(sc, axis=1, keepdims=True))
                p = jnp.exp(sc - mn)
                m_new.append(mn)
                rs.append(jnp.sum(p, axis=1, keepdims=True))
                pv.append(lax.dot_general(p.astype(BF16), v_ref[k_rows, cols], _NN, preferred_element_type=F32))
            m2 = jnp.where(first, m_new[0], m_new[1])
            alpha = jnp.exp(m_old - m2)
            m_sc[q_sl, :] = m2
            l_sc[q_sl, :] = alpha * l_sc[q_sl, :] + jnp.where(first, rs[0], rs[1])
            acc_sc[q_sl, :] = alpha * acc_sc[q_sl, :] + pv[0] + pv[1]

        def below_diagonal(ki, carry):
            step(0, ki * t, t, False)
            return carry

        lax.fori_loop(0, qi, below_diagonal, 0)
        step(0, qi * t, half, True)
        if half < t:
            step(half, qi * t, half, False)
            step(half, qi * t + half, half, True)
        l2 = l_sc[...]
        o_ref[...] = acc_sc[...] / l2
        lse_ref[...] = m_sc[...] + jnp.log(l2)

    qs = pl.BlockSpec((t, 2 * LANES), lambda p, qi: (qi, p))
    ks = pl.BlockSpec((s, 2 * LANES), lambda p, qi: (0, p))
    os_ = pl.BlockSpec((t, LANES), lambda p, qi: (qi, p))
    return _pc(body, exchange, out_shape=(_sds((s, a_w), F32), _sds((s, a_w), F32)), grid=(npair, nq),
               in_specs=[qs, ks, ks], out_specs=(os_, os_),
               scratch_shapes=[pltpu.VMEM((t, LANES), F32)] * 3,
               compiler_params=_params("parallel", "arbitrary"), name=name)(qa, ka, va)


def _attn_bwd(qa, ka, va, do, o, lse, name, exchange=None):
    s = qa.shape[0]
    a_w = qa.shape[1] // 2
    npair = a_w // LANES
    t = _attn_tiles(s)
    nq = s // t
    half = _attn_half(t)
    scale = 1.0 / math.sqrt(HEAD_DIM)

    def body(q_ref, k_ref, v_ref, do_ref, o_ref, lse_ref, dq_ref, dk_ref, dv_ref, qx_ref, kx_ref, dk_sc, dv_sc, kx_sc):
        ki = pl.program_id(1)
        first = lax.broadcasted_iota(jnp.int32, (1, LANES), 1) < HEAD_DIM

        @pl.when(ki == 0)
        def _():
            dq_ref[...] = jnp.zeros_like(dq_ref)
            qx_ref[...] = jnp.zeros_like(qx_ref)

        def step(q_start, k0, size, diag, assign):
            rows = pl.ds(pl.multiple_of(q_start, size), size)
            k_sl = slice(k0, k0 + size)
            do2 = do_ref[rows, :]
            lse2 = lse_ref[rows, :]
            dd = do2.astype(F32) * o_ref[rows, :]
            keep = None
            if diag:
                keep = (lax.broadcasted_iota(jnp.int32, (size, size), 0) >= lax.broadcasted_iota(jnp.int32, (size, size), 1))
            dq_h, dk_h, dv_h = [], [], []
            for hh in range(2):
                sel = first if hh == 0 else jnp.logical_not(first)
                cols = slice(hh * LANES, (hh + 1) * LANES)
                qh, kh, vh = q_ref[rows, cols], k_ref[k_sl, cols], v_ref[k_sl, cols]
                delta = jnp.sum(jnp.where(sel, dd, 0.0), axis=1, keepdims=True)
                sc = lax.dot_general(qh, kh, _NT, preferred_element_type=F32)
                if diag:
                    sc = jnp.where(keep, sc, NEG_BIG)
                p = jnp.exp(sc - lse2[:, hh * HEAD_DIM:hh * HEAD_DIM + 1])
                dp = lax.dot_general(do2, vh, _NT, preferred_element_type=F32)
                ds_b = (p * (dp - delta)).astype(BF16)
                dv_h.append(lax.dot_general(p.astype(BF16), do2, _TN, preferred_element_type=F32))
                dk_h.append(lax.dot_general(ds_b, qh, _TN, preferred_element_type=F32))
                dq_h.append(lax.dot_general(ds_b, kh, _NN, preferred_element_type=F32))
            dq_ref[rows, :] += jnp.where(first, dq_h[0], dq_h[1]) * scale
            qx_ref[rows, :] += jnp.where(first, dq_h[1], dq_h[0])
            dk_new = jnp.where(first, dk_h[0], dk_h[1])
            kx_new = jnp.where(first, dk_h[1], dk_h[0])
            dv_new = jnp.where(first, dv_h[0], dv_h[1])
            if assign:
                dk_sc[k_sl, :] = dk_new
                kx_sc[k_sl, :] = kx_new
                dv_sc[k_sl, :] = dv_new
            else:
                dk_sc[k_sl, :] += dk_new
                kx_sc[k_sl, :] += kx_new
                dv_sc[k_sl, :] += dv_new

        def below_diagonal(qi, carry):
            step(qi * t, 0, t, False, False)
            return carry

        step(ki * t, 0, half, True, True)
        if half < t:
            step(ki * t + half, 0, half, False, False)
            step(ki * t + half, half, half, True, True)
        lax.fori_loop(ki + 1, nq, below_diagonal, 0)
        dk_ref[...] = dk_sc[...].astype(BF16)
        dv_ref[...] = dv_sc[...].astype(BF16)
        kx_ref[...] = kx_sc[...]

    ks2 = pl.BlockSpec((t, 2 * LANES), lambda p, ki: (ki, p))
    qs2 = pl.BlockSpec((s, 2 * LANES), lambda p, ki: (0, p))
    whole = pl.BlockSpec((s, LANES), lambda p, ki: (0, p))
    kout = pl.BlockSpec((t, LANES), lambda p, ki: (ki, p))
    return _pc(body, exchange,
               out_shape=(_sds((s, a_w), F32), _sds((s, a_w), BF16), _sds((s, a_w), BF16), _sds((s, a_w), F32),
                          _sds((s, a_w), F32)),
               grid=(npair, nq), in_specs=[qs2, ks2, ks2, whole, whole, whole],
               out_specs=(whole, kout, kout, whole, kout),
               scratch_shapes=[pltpu.VMEM((t, LANES), F32)] * 3,
               compiler_params=_params("parallel", "arbitrary"), name=name)(qa, ka, va, do, o, lse)

def _decay_grads(qx, kx, name):
    s, a_w = qx.shape
    n_heads = a_w // HEAD_DIM
    tr = _tile(s, 512, 8)
    pick_q = [[0.0] * LANES for _ in range(a_w)]
    pick_k = [[0.0] * LANES for _ in range(a_w)]
    for h in range(n_heads):
        b0 = (h // 2) * LANES + (HEAD_DIM if h % 2 == 0 else 0)
        pick_q[b0][h] = 1.0
        pick_k[b0 + 3][h] = 1.0
    pick_q = jnp.array(pick_q, BF16)
    pick_k = jnp.array(pick_k, BF16)

    def body(qx_ref, kx_ref, pq_ref, pk_ref, o_ref):
        o_ref[...] = _dot3(qx_ref[...], pq_ref[...]) - _dot3(kx_ref[...], pk_ref[...])

    row = pl.BlockSpec((tr, a_w), lambda i: (i, 0))
    pick = pl.BlockSpec((a_w, LANES), lambda i: (0, 0))
    return _pc(body, out_shape=_sds((s, LANES), F32), grid=(s // tr,), in_specs=[row, row, pick, pick],
               out_specs=pl.BlockSpec((tr, LANES), lambda i: (i, 0)),
               compiler_params=_params("parallel"), name=name)(qx, kx, pick_q, pick_k)


def _shift_down(z, k, rows):
    return jnp.where(rows >= k, pltpu.roll(z, k, 0), 0.0)


def _shift_up(z, k, rows, n):
    return jnp.where(rows < n - k, pltpu.roll(z, n - k, 0), 0.0)


def _conv_fwd(bcx, conv_w, name):
    s = bcx.shape[0]
    cw = bcx.shape[1] // 3
    nb = cw // LANES

    def body(b_ref, c_ref, x_ref, w_ref, cv_ref):
        rows = lax.broadcasted_iota(jnp.int32, (s, LANES), 0)
        z = c_ref[...] * x_ref[...]
        w = w_ref[...]
        y = w[2:3, :] * z + w[1:2, :] * _shift_down(z, 1, rows) + w[0:1, :] * _shift_down(z, 2, rows)
        cv_ref[...] = b_ref[...] * y

    def col(off):
        return pl.BlockSpec((s, LANES), lambda j: (0, j + off))

    return _pc(body, out_shape=_sds((s, cw), F32), grid=(nb,),
               in_specs=[col(0), col(nb), col(2 * nb), pl.BlockSpec((CONV_K, LANES), lambda j: (0, j))],
               out_specs=col(0), compiler_params=_params("parallel"), name=name)(bcx, bcx, bcx, conv_w)


def _conv_bwd(dcv, bcx, conv_w, name):
    s = bcx.shape[0]
    cw = bcx.shape[1] // 3
    nb = cw // LANES

    def body(dcv_ref, b_ref, c_ref, x_ref, w_ref, db_ref, dc_ref, dxc_ref, dw_ref):
        rows = lax.broadcasted_iota(jnp.int32, (s, LANES), 0)
        cv_, xv = c_ref[...], x_ref[...]
        z = cv_ * xv
        w = w_ref[...]
        z1 = _shift_down(z, 1, rows)
        z2 = _shift_down(z, 2, rows)
        y = w[2:3, :] * z + w[1:2, :] * z1 + w[0:1, :] * z2
        dcvv = dcv_ref[...]
        db_ref[...] = (dcvv * y).astype(BF16)
        dy = dcvv * b_ref[...]
        dw_ref[0:1, :] = jnp.sum(dy * z2, axis=0, keepdims=True)
        dw_ref[1:2, :] = jnp.sum(dy * z1, axis=0, keepdims=True)
        dw_ref[2:3, :] = jnp.sum(dy * z, axis=0, keepdims=True)
        dz = w[2:3, :] * dy + w[1:2, :] * _shift_up(dy, 1, rows, s) + w[0:1, :] * _shift_up(dy, 2, rows, s)
        dc_ref[...] = (dz * xv).astype(BF16)
        dxc_ref[...] = (dz * cv_).astype(BF16)

    def col(off):
        return pl.BlockSpec((s, LANES), lambda j: (0, j + off))

    wspec = pl.BlockSpec((CONV_K, LANES), lambda j: (0, j))
    db, dc, dxc, dw = _pc(body, out_shape=(_sds((s, cw), BF16),) * 3 + (_sds((CONV_K, cw), F32),), grid=(nb,),
                          in_specs=[col(0), col(0), col(nb), col(2 * nb), wspec],
                          out_specs=(col(0), col(0), col(0), wspec),
                          compiler_params=_params("parallel"), name=name)(dcv, bcx, bcx, bcx, conv_w)
    return db, dc, dxc, dw


def _group_matrix():
    idx = jnp.arange(LANES) // HEAD_DIM
    return (idx[:, None] == idx[None, :]).astype(BF16)


def _group_sum(v, gmat):
    return _dot3(v, gmat)


def _gnorm_fwd(att, cv, gg, name):
    s, a_w = att.shape
    cw = cv.shape[1]
    d = a_w + cw
    tr = _tile(s, 512, 16)
    gmat = _group_matrix()

    def body(att_ref, cv_ref, gg_ref, gm_ref, yn_ref):
        gm = gm_ref[...]
        for c0 in range(0, d, LANES):
            y = att_ref[:, c0:c0 + LANES] if c0 < a_w else cv_ref[:, c0 - a_w:c0 - a_w + LANES]
            ms = _group_sum(y * y, gm) * (1.0 / HEAD_DIM)
            yn_ref[:, c0:c0 + LANES] = (y * lax.rsqrt(ms + EPS) * gg_ref[:, c0:c0 + LANES]).astype(BF16)

    return _pc(body, out_shape=_sds((s, d), BF16), grid=(s // tr,),
               in_specs=[pl.BlockSpec((tr, a_w), lambda i: (i, 0)), pl.BlockSpec((tr, cw), lambda i: (i, 0)),
                         _vec_spec(d), pl.BlockSpec((LANES, LANES), lambda i: (0, 0))],
               out_specs=pl.BlockSpec((tr, d), lambda i: (i, 0)),
               compiler_params=_params("parallel"), name=name)(att, cv, gg, gmat)


def _gnorm_bwd(dyn, att, cv, gg, name):
    s, a_w = att.shape
    cw = cv.shape[1]
    d = a_w + cw
    tr = _tile(s, 256, 16)
    gmat = _group_matrix()

    def body(dyn_ref, att_ref, cv_ref, gg_ref, gm_ref, datt_ref, dcv_ref, dgg_ref):
        @pl.when(pl.program_id(0) == 0)
        def _():
            dgg_ref[...] = jnp.zeros_like(dgg_ref)

        gm = gm_ref[...]
        for c0 in range(0, d, LANES):
            y = att_ref[:, c0:c0 + LANES] if c0 < a_w else cv_ref[:, c0 - a_w:c0 - a_w + LANES]
            dv = dyn_ref[:, c0:c0 + LANES]
            r = lax.rsqrt(_group_sum(y * y, gm) * (1.0 / HEAD_DIM) + EPS)
            xhat = y * r
            dgg_ref[:, c0:c0 + LANES] += jnp.sum(dv * xhat, axis=0, keepdims=True)
            dxh = dv * gg_ref[:, c0:c0 + LANES]
            proj = _group_sum(dxh * xhat, gm) * (1.0 / HEAD_DIM)
            dy = r * (dxh - xhat * proj)
            if c0 < a_w:
                datt_ref[:, c0:c0 + LANES] = dy.astype(BF16)
            else:
                dcv_ref[:, c0 - a_w:c0 - a_w + LANES] = dy

    return _pc(body, out_shape=(_sds((s, a_w), BF16), _sds((s, cw), F32), _sds((1, d), F32)), grid=(s // tr,),
               in_specs=[pl.BlockSpec((tr, d), lambda i: (i, 0)), pl.BlockSpec((tr, a_w), lambda i: (i, 0)),
                         pl.BlockSpec((tr, cw), lambda i: (i, 0)), _vec_spec(d),
                         pl.BlockSpec((LANES, LANES), lambda i: (0, 0))],
               out_specs=(pl.BlockSpec((tr, a_w), lambda i: (i, 0)), pl.BlockSpec((tr, cw), lambda i: (i, 0)),
                          _vec_spec(d)),
               compiler_params=_params("arbitrary"), name=name)(dyn, att, cv, gg, gmat)


def _adamw_math(w, g, m, v):
    m_new = ADAM_B1 * m + (1.0 - ADAM_B1) * g
    v_new = ADAM_B2 * v + (1.0 - ADAM_B2) * (g * g)
    m_hat = m_new / (1.0 - ADAM_B1 ** ADAM_STEP)
    v_hat = v_new / (1.0 - ADAM_B2 ** ADAM_STEP)
    delta = -ADAM_LR * (m_hat / (jnp.sqrt(v_hat) + ADAM_EPS) + ADAM_WD * w)
    return delta, m_new, v_new


def _row_tile(r, c):
    return _tile(r, max(8, ((1 << 19) // c) // 8 * 8), 8)


def _adamw(w, g, m, v, name):
    r, c = w.shape
    tr = _row_tile(r, c)

    def body(w_ref, g_ref, m_ref, v_ref, d_ref, mo_ref, vo_ref):
        d, mn, vn = _adamw_math(w_ref[...], g_ref[...], m_ref[...], v_ref[...])
        d_ref[...] = d
        mo_ref[...] = mn
        vo_ref[...] = vn

    spec = pl.BlockSpec((tr, c), lambda i: (i, 0))
    return _pc(body, out_shape=(_sds((r, c), F32),) * 3, grid=(r // tr,), in_specs=[spec] * 4,
               out_specs=(spec,) * 3, compiler_params=_params("parallel"), name=name)(w, g, m, v)


def _adamw_halves(w, mine, theirs, m, v, core, name):
    r2, c = w.shape
    r = r2 // 2
    assert mine.shape == (r, c) and theirs.shape == (r, c)
    tr = _row_tile(r, c)
    nb = r // tr

    def body(core_ref, w_ref, a_ref, b_ref, m_ref, v_ref, g_ref, d_ref, mo_ref, vo_ref):
        g = jnp.where(pl.program_id(0) == core_ref[0], a_ref[...], b_ref[...])
        d, mn, vn = _adamw_math(w_ref[...], g, m_ref[...], v_ref[...])
        g_ref[...] = g
        d_ref[...] = d
        mo_ref[...] = mn
        vo_ref[...] = vn

    full = pl.BlockSpec((tr, c), lambda h, i, core_ref: (h * nb + i, 0))
    half = pl.BlockSpec((tr, c), lambda h, i, core_ref: (i, 0))
    grid_spec = pltpu.PrefetchScalarGridSpec(
        num_scalar_prefetch=1, grid=(2, nb), in_specs=[full, half, half, full, full], out_specs=(full,) * 4)
    return _pc(body, out_shape=(_sds((r2, c), F32),) * 4, grid_spec=grid_spec,
               compiler_params=_params("parallel", "parallel"), name=name)(core, w, mine, theirs, m, v)


def _ada_fwd(c16, ada_w, ada_b, name):
    d, n = ada_w.shape
    tn = _tile(n, 768, LANES)

    def body(c_ref, w_ref, b_ref, o_ref):
        cv = c_ref[...]
        sc = (cv * jax.nn.sigmoid(cv)).astype(BF16)
        o_ref[...] = lax.dot_general(sc, w_ref[...].astype(BF16), _NN, preferred_element_type=F32) + b_ref[...]

    return _pc(body, out_shape=_sds((16, n), F32), grid=(n // tn,),
               in_specs=[pl.BlockSpec((16, d), lambda j: (0, 0)), pl.BlockSpec((d, tn), lambda j: (0, j)),
                         pl.BlockSpec((1, tn), lambda j: (0, j))],
               out_specs=pl.BlockSpec((16, tn), lambda j: (0, j)),
               compiler_params=_params("parallel"), name=name)(c16, ada_w, ada_b)


def _ada_update(c16_t, dmod16, w, m, v, name, exchange=None):
    r, c = w.shape
    tr = _row_tile(r, c)

    def body(c_ref, dm_ref, w_ref, m_ref, v_ref, g_ref, d_ref, mo_ref, vo_ref):
        cv = c_ref[...]
        sc = (cv * jax.nn.sigmoid(cv)).astype(BF16)
        g = lax.dot_general(sc, dm_ref[...].astype(BF16), _NN, preferred_element_type=F32)
        d, mn, vn = _adamw_math(w_ref[...], g, m_ref[...], v_ref[...])
        g_ref[...] = g
        d_ref[...] = d
        mo_ref[...] = mn
        vo_ref[...] = vn

    spec = pl.BlockSpec((tr, c), lambda i: (i, 0))
    return _pc(body, exchange, out_shape=(_sds((r, c), F32),) * 4, grid=(r // tr,),
               in_specs=[pl.BlockSpec((tr, 16), lambda i: (i, 0)), pl.BlockSpec((16, c), lambda i: (0, 0)),
                         spec, spec, spec],
               out_specs=(spec,) * 4, compiler_params=_params("parallel"), name=name)(c16_t, dmod16, w, m, v)


def _add_half(dw, recv, core, name):
    _, _, r, w = dw.shape
    tr = _tile(r, 512, 16)

    def body(core_ref, a_ref, b_ref, o_ref):
        o_ref[...] = (a_ref[...].astype(F32) + b_ref[...].astype(F32)).astype(BF16)

    grid_spec = pltpu.PrefetchScalarGridSpec(
        num_scalar_prefetch=1, grid=(N_CHIPS, r // tr),
        in_specs=[pl.BlockSpec((None, None, tr, w), lambda s, i, core_ref: (s, core_ref[0], i, 0)),
                  pl.BlockSpec((None, tr, w), lambda s, i, core_ref: (s, i, 0))],
        out_specs=pl.BlockSpec((None, tr, w), lambda s, i, core_ref: (s, i, 0)))
    return _pc(body, out_shape=_sds((N_CHIPS, r, w), BF16), grid_spec=grid_spec,
               compiler_params=_params("parallel", "parallel"), name=name)(core, dw, recv)


def _sum_chips(own, recv, chip, name):
    _, r, w = own.shape
    tr = _tile(r, 512, 16)

    def body(chip_ref, own_ref, p_ref, o_ref):
        acc = own_ref[...].astype(F32)
        for q in range(N_CHIPS - 1):
            acc = acc + p_ref[q].astype(F32)
        o_ref[...] = acc

    grid_spec = pltpu.PrefetchScalarGridSpec(
        num_scalar_prefetch=1, grid=(r // tr,),
        in_specs=[pl.BlockSpec((None, tr, w), lambda i, chip_ref: (chip_ref[0], i, 0)),
                  pl.BlockSpec((N_CHIPS - 1, tr, w), lambda i, chip_ref: (0, i, 0))],
        out_specs=pl.BlockSpec((tr, w), lambda i, chip_ref: (i, 0)))
    return _pc(body, out_shape=_sds((r, w), F32), grid_spec=grid_spec,
               compiler_params=_params("parallel"), name=name)(chip, own, recv)


def _sum_devices(parts, name):
    nd, r, w = parts.shape

    def body(p_ref, o_ref):
        acc = p_ref[0]
        for q in range(1, nd):
            acc = acc + p_ref[q]
        o_ref[...] = acc

    return _pc(body, out_shape=_sds((r, w), F32), name=name)(parts)


def _place():
    x, y, c = lax.axis_index("x"), lax.axis_index("y"), lax.axis_index("c")
    chips = [(1 - x, y), (x, 1 - y), (1 - x, 1 - y)]
    return x, y, c, chips


def _small_gather_exchange(blk):
    r, w = blk.shape

    def copies(src, dst, send_sems, recv_sems):
        x, y, c, chips = _place()
        me, sibling = (x, y, c), (x, y, 1 - c)

        def rows(px, py, pc):
            return dst[0].at[pl.ds((4 * px + 2 * py + pc) * r, r), :]

        def copy(k, block, to, own=False):
            return _remote(src[0] if own else rows(*block), rows(*block), send_sems, recv_sems, k, to)

        mine = pltpu.make_async_copy(src[0], rows(*me), send_sems.at[7])
        first = [copy(0, me, sibling, own=True)] + [copy(1 + j, me, (*chip, c), own=True) for j, chip in enumerate(chips)]
        passed = [copy(4 + j, (*chip, c), sibling) for j, chip in enumerate(chips)]
        landed = [copy(1 + j, (*chip, c), me) for j, chip in enumerate(chips)]
        from_sibling = [copy(0, sibling, me)] + [copy(4 + j, (*chip, 1 - c), me) for j, chip in enumerate(chips)]
        return mine, first, passed, landed, from_sibling

    def start(src, dst, send_sems, recv_sems):
        mine, first, _, _, _ = copies(src, dst, send_sems, recv_sems)
        mine.start()
        for cp in first:
            cp.start()

    def finish(src, dst, send_sems, recv_sems):
        mine, first, passed, landed, from_sibling = copies(src, dst, send_sems, recv_sems)
        for arrival, onward in zip(landed, passed):
            arrival.wait_recv()
            onward.start()
        for cp in from_sibling:
            cp.wait_recv()
        for cp in first + passed:
            cp.wait_send()
        mine.wait()

    return _Exchange([blk], [_sds((N_DEV * r, w), blk.dtype)], 8, start, finish)


def _remote(src, dst, send_sems, recv_sems, k, to):
    return pltpu.make_async_remote_copy(src_ref=src, dst_ref=dst, send_sem=send_sems.at[k], recv_sem=recv_sems.at[k],
                                        device_id=to, device_id_type=MESH)


def _exchange_of(inputs, out_shapes, n_sems, copies, aliases=None):
    def start(src, dst, send_sems, recv_sems):
        for cp in copies(src, dst, send_sems, recv_sems)[0]:
            cp.start()

    def finish(src, dst, send_sems, recv_sems):
        sends, arrivals = copies(src, dst, send_sems, recv_sems)
        for cp in arrivals:
            cp.wait_recv()
        for cp in sends:
            cp.wait_send()

    return _Exchange(inputs, out_shapes, n_sems, start, finish, aliases)


def _run_exchange(ex, name):
    n_in, n_out = len(ex.inputs), len(ex.out_shapes)

    def body(*refs):
        src, dst = refs[:n_in], refs[n_in:n_in + n_out]
        send_sems, recv_sems = refs[n_in + n_out:]
        ex.start(src, dst, send_sems, recv_sems)
        ex.finish(src, dst, send_sems, recv_sems)

    ex.set_results(pl.pallas_call(
        body, out_shape=tuple(ex.out_shapes), in_specs=[_ANY] * n_in, out_specs=(_ANY,) * n_out,
        scratch_shapes=[pltpu.SemaphoreType.DMA((ex.n_sems,)), pltpu.SemaphoreType.DMA((ex.n_sems,))],
        input_output_aliases=ex.aliases, name=name)(*ex.inputs))


def _gather_ici_exchange(shards):
    n = len(shards)

    def copies(own, out, send_sems, recv_sems):
        x, y, c, chips = _place()
        my_chip = 2 * x + y
        sends, arrivals = [], []
        for i in range(n):
            for j, chip in enumerate(chips):
                to = (*chip, c)
                sends.append(_remote(own[i].at[c], out[i].at[my_chip, c], send_sems, recv_sems, 4 * i + j, to))
                arrivals.append(_remote(own[i].at[c], out[i].at[2 * chip[0] + chip[1], c], send_sems, recv_sems, 4 * i + j, to))
            whole = _remote(own[i], out[i].at[my_chip], send_sems, recv_sems, 4 * i + 3, (x, y, 1 - c))
            sends.append(whole)
            arrivals.append(whole)
        return sends, arrivals

    return _exchange_of(shards, [_sds((N_CHIPS,) + s.shape, s.dtype) for s in shards], 4 * n, copies)


def _gather_pass_exchange(gathered):
    n = len(gathered)

    def copies(src, dst, send_sems, recv_sems):
        x, y, c, chips = _place()
        sends, arrivals = [], []
        for i in range(n):
            for j, chip in enumerate(chips):
                idx = 2 * chip[0] + chip[1]
                sends.append(_remote(src[i].at[idx, c], dst[i].at[idx, c], send_sems, recv_sems, 3 * i + j, (x, y, 1 - c)))
                arrivals.append(_remote(src[i].at[idx, c], dst[i].at[idx, 1 - c], send_sems, recv_sems, 3 * i + j, (x, y, 1 - c)))
        return sends, arrivals

    return _exchange_of(gathered, [_sds(g.shape, g.dtype) for g in gathered], 3 * n, copies,
                        aliases={i: i for i in range(n)})


def _reduce_sibling_exchange(grads):
    n = len(grads)

    def copies(src, dst, send_sems, recv_sems):
        x, y, c, _ = _place()
        both = [_remote(src[i].at[s, 1 - c], dst[i].at[s], send_sems, recv_sems, N_CHIPS * i + s, (x, y, 1 - c))
                for i in range(n) for s in range(N_CHIPS)]
        return both, both

    return _exchange_of(grads, [_sds((N_CHIPS,) + g.shape[2:], g.dtype) for g in grads], N_CHIPS * n, copies)


def _reduce_chips_exchange(parts):
    n = len(parts)

    def copies(src, dst, send_sems, recv_sems):
        x, y, c, chips = _place()
        both = [_remote(src[i].at[2 * chip[0] + chip[1]], dst[i].at[j], send_sems, recv_sems, 3 * i + j, (*chip, c))
                for i in range(n) for j, chip in enumerate(chips)]
        return both, both

    return _exchange_of(parts, [_sds((N_CHIPS - 1,) + p.shape[1:], p.dtype) for p in parts], 3 * n, copies)


def _share_exchange(halves):
    n = len(halves)

    def copies(src, dst, send_sems, recv_sems):
        x, y, c, _ = _place()
        both = [_remote(src[i], dst[i], send_sems, recv_sems, i, (x, y, 1 - c)) for i in range(n)]
        return both, both

    return _exchange_of(halves, [_sds(h.shape, h.dtype) for h in halves], n, copies)


HEAD_ROWS = 16


class _WeightTraffic:
    def __init__(self, shards, core, chip):
        self.shards, self.core, self.chip = shards, core, chip
        self.gather, self.grads, self.reduce, self.chip_sums, self.half_sums, self.shared = {}, {}, {}, {}, {}, {}

    def gather_ici(self, grp):
        self.gather[grp] = _gather_ici_exchange(self.shards[grp])
        return self.gather[grp]

    def gather_pass(self, grp):
        self.gather[grp] = _gather_pass_exchange(self.gather[grp].results)
        return self.gather[grp]

    def weights(self, grp):
        return [g.reshape(-1, g.shape[-1]) for g in self.gather[grp].results]

    def reduce_sibling(self, grp, grads):
        self.grads[grp] = [g.reshape(N_CHIPS, 2, g.shape[0] // (2 * N_CHIPS), g.shape[1]) for g in grads]
        self.reduce[grp] = _reduce_sibling_exchange(self.grads[grp])
        return self.reduce[grp]

    def add_halves(self, grp):
        self.chip_sums[grp] = [_add_half(g, r, self.core, "add_half_%s%d" % (grp, i))
                               for i, (g, r) in enumerate(zip(self.grads[grp], self.reduce[grp].results))]

    def reduce_chips(self, grp):
        self.reduce[grp] = _reduce_chips_exchange(self.chip_sums[grp])
        return self.reduce[grp]

    def sum_chips(self, grp):
        self.half_sums[grp] = [_sum_chips(o, p, self.chip, "sum_chips_%s%d" % (grp, i))
                               for i, (o, p) in enumerate(zip(self.chip_sums[grp], self.reduce[grp].results))]

    def share(self, grp):
        self.shared[grp] = _share_exchange(self.half_sums[grp])
        return self.shared[grp]

    def totals(self, grp):
        return list(zip(self.half_sums[grp], self.shared[grp].results))


def _ffn_fwd(x, norm_g, shift, scale, gate, wg_t, wu_t, wd, tag, next_norm, up_exchange=None, down_exchange=None):
    h = _norm_mod_fwd(x, norm_g, shift, scale, tag + "_norm_fwd")
    a, u, hid = _ffn_up(h, wg_t, wu_t, tag + "_up", exchange=up_exchange)
    wd = wd() if callable(wd) else wd
    x_out, f, h_next = _mm(hid, wd, "nn", F32, tag + "_down", res=x, gate=gate, aux_dtype=BF16, norm=next_norm,
                           exchange=down_exchange() if down_exchange else None)
    return x_out, (h, a, u, hid, f), h_next


def _ffn_bwd(dx_out, df, x, saved, norm_g, scale, wg_t, wu_t, wd, tag, traffic, below=None, dact_exchange=None,
             dw_exchange=None, finish_reduction=False):
    h, a, u, hid, _ = saved
    f_below, gate_below = below if below else (None, None)
    da, du = _ffn_dact(df, wd, a, u, tag + "_dact", exchange=dact_exchange)
    dwd = _mm(hid, df, "tn", BF16, tag + "_dwd", exchange=dw_exchange() if dw_exchange else None)
    if not finish_reduction:
        dwg_t = _mm(da, h, "tn", BF16, tag + "_dwg")
        dwu_t = _mm(du, h, "tn", BF16, tag + "_dwu")
        dx, dshift, dscale, dnorm_g, *gated = _norm_mod_bwd(
            ([da, du], [wg_t, wu_t]), x, norm_g, scale, dx_out, tag + "_dh_norm_bwd", f=f_below, gate=gate_below,
            exchange=traffic.reduce_sibling(tag, [dwg_t, dwu_t, dwd]))
        traffic.add_halves(tag)
        return dx, (dshift, dscale, dnorm_g), gated
    kd, kg, ku = tag + "_wd", tag + "_wg", tag + "_wu"
    dwg_t = _mm(da, h, "tn", BF16, tag + "_dwg", exchange=traffic.reduce_sibling(kd, [dwd]))
    traffic.add_halves(kd)
    dwu_t = _mm(du, h, "tn", BF16, tag + "_dwu",
                exchange=_join(traffic.reduce_chips(kd), traffic.reduce_sibling(kg, [dwg_t])))
    traffic.add_halves(kg)
    half = x.shape[0] // 2
    top = _norm_mod_bwd(([da, du], [wg_t, wu_t]), x, norm_g, scale, dx_out, tag + "_dh_norm_bwd_top", f=f_below,
                        gate=gate_below, rows=(0, half),
                        exchange=_join(traffic.reduce_chips(kg), traffic.reduce_sibling(ku, [dwu_t])))
    traffic.add_halves(ku)
    traffic.sum_chips(kd)
    traffic.sum_chips(kg)
    bottom = _norm_mod_bwd(([da, du], [wg_t, wu_t]), x, norm_g, scale, dx_out, tag + "_dh_norm_bwd_bottom", f=f_below,
                           gate=gate_below, rows=(half, half),
                           exchange=_join(traffic.reduce_chips(ku), traffic.share(kd), traffic.share(kg)))
    traffic.sum_chips(ku)
    dx, dshift, dscale, dnorm_g, *gated = [jnp.concatenate([a, b]) if a.shape[0] == half else a + b
                                           for a, b in zip(top, bottom)]
    return dx, (dshift, dscale, dnorm_g), gated


def _layer_step(x, target, mod, gains, forget_bias, conv_w, traffic, att_w, in_shard, in_rows):
    sh1, sc1, g1, sh2, sc2, g2, sh3, sc3, g3 = mod
    norm1_g, norm2_g, norm3_g, final_g, group_g = gains
    s, d = x.shape
    n_heads = att_w // HEAD_DIM
    npair = n_heads // 2
    gate1, gate3 = 0.5 * g1, 0.5 * g3

    def split_w_in(w_in_pad):
        w_in_t = w_in_pad.reshape(N_CHIPS, in_rows, d)[:, :in_shard].reshape(N_CHIPS * in_shard, d)
        return (w_in_t[:3 * att_w], _pad_rows(w_in_t[3 * att_w:3 * att_w + n_heads], LANES), w_in_t[3 * att_w + n_heads:])

    wg1_t, wu1_t = traffic.weights("ffn1_gu")

    def wd1_ready():
        _run_exchange(traffic.gather_pass("ffn1_d"), "gather_ffn1_down_pass")
        return traffic.weights("ffn1_d")[0]

    x1, saved1, h2 = _ffn_fwd(x, norm1_g, sh1, sc1, gate1, wg1_t, wu1_t, wd1_ready, "ffn1", (norm2_g, sh2, sc2),
                              up_exchange=_join(traffic.gather_ici("ffn1_d"), traffic.gather_ici("mix_in")),
                              down_exchange=lambda: _join(traffic.gather_pass("mix_in"), traffic.gather_ici("mix_out")))
    wd1 = traffic.weights("ffn1_d")[0]
    wqkv_t, wf_t, wbcx_t = split_w_in(traffic.weights("mix_in")[0])

    qkv, bcx, flog = _project(h2, [wqkv_t, wbcx_t, wf_t], [BF16, F32, F32], "mix_proj",
                              exchange=traffic.gather_pass("mix_out"))
    w_out = traffic.weights("mix_out")[0]
    flog_t = jnp.pad(flog[:, :n_heads].T, ((0, HEAD_ROWS - n_heads), (0, 0)))
    bias_col = jnp.pad(forget_bias, (0, HEAD_ROWS - n_heads))[:, None]
    f_pieces = _forget_fwd(flog_t, bias_col, "forget_fwd")
    qa, ka, va = _attn_prep(qkv, f_pieces, "attn_prep")
    att, lse = _attn_fwd(qa, ka, va, "attn_fwd", exchange=traffic.gather_ici("ffn2"))
    cv = _conv_fwd(bcx, conv_w, "conv_fwd")
    yn = _gnorm_fwd(att, cv, group_g, "gnorm_fwd")
    x2, mix, h3 = _mm(yn, w_out, "nn", F32, "mix_out", res=x1, gate=g2, aux_dtype=BF16, norm=(norm3_g, sh3, sc3),
                      exchange=traffic.gather_pass("ffn2"))
    wg2_t, wu2_t, wd2 = traffic.weights("ffn2")

    a3, u3, hid3 = _ffn_up(h3, wg2_t, wu2_t, "ffn2_up")
    saved3 = (h3, a3, u3, hid3, None)
    dx3, loss_row, dfinal_g, df2, dgate3 = _down_final_loss(hid3, wd2, x2, gate3, final_g, target, "ffn2_down_loss")

    dx2, (dsh3, dsc3, dnorm3_g), (dmix, dg2) = _ffn_bwd(
        dx3, df2, x2, saved3, norm3_g, sc3, wg2_t, wu2_t, wd2, "ffn2", traffic, below=(mix, g2))
    dyn = _mm(dmix, w_out, "nt", F32, "mix_out_dyn")
    dw_out = _mm(yn, dmix, "tn", BF16, "mix_out_dw")
    datt, dcv, dgroup_g = _gnorm_bwd(dyn, att, cv, group_g, "gnorm_bwd")
    db, dc, dxc, dconv_w = _conv_bwd(dcv, bcx, conv_w, "conv_bwd")
    dbcx = jnp.concatenate([db, dc, dxc], axis=1)
    dq, dk, dv, qx, kx = _attn_bwd(qa, ka, va, datt, att, lse, "attn_bwd", exchange=traffic.reduce_chips("ffn2"))
    traffic.sum_chips("ffn2")
    dqkv = jnp.concatenate([dq.astype(BF16), dk, dv], axis=1)
    df_t = _decay_grads(qx, kx, "decay_grads")[:, :HEAD_ROWS].T
    dflog_t, dbias_col = _forget_bwd(df_t, flog_t, bias_col, "forget_bwd")
    dflog = jnp.pad(dflog_t[:n_heads].T, ((0, 0), (0, LANES - n_heads))).astype(BF16)
    dwqkv_t = _mm(dqkv, h2, "tn", BF16, "mix_dw_qkv", exchange=traffic.share("ffn2"))
    dwbcx_t = _mm(dbcx, h2, "tn", BF16, "mix_dw_bcx")
    dwf_t = _mm(dflog, h2, "tn", BF16, "mix_dw_f")
    dw_in_t = jnp.concatenate([dwqkv_t, dwf_t[:n_heads], dwbcx_t], axis=0).reshape(N_CHIPS, in_shard, d)
    dw_in_t = jnp.pad(dw_in_t, ((0, 0), (0, in_rows - in_shard), (0, 0))).reshape(N_CHIPS * in_rows, d)
    dx1, dsh2, dsc2, dnorm2_g, df1, dgate1 = _norm_mod_bwd(
        ([dqkv, dbcx, dflog], [wqkv_t, wbcx_t, wf_t]), x1, norm2_g, sc2, dx2, "mix_dh_norm_bwd", f=saved1[4], gate=gate1,
        exchange=traffic.reduce_sibling("mix", [dw_in_t, dw_out]))
    traffic.add_halves("mix")

    def share_mix():
        traffic.sum_chips("mix")
        return traffic.share("mix")

    dx, (dsh1, dsc1, dnorm1_g), _ = _ffn_bwd(
        dx1, df1, x, saved1, norm1_g, sc1, wg1_t, wu1_t, wd1, "ffn1", traffic,
        dact_exchange=traffic.reduce_chips("mix"), dw_exchange=share_mix, finish_reduction=True)

    dmod = [dsh1, dsc1, 0.5 * dgate1, dsh2, dsc2, dg2, dsh3, dsc3, 0.5 * dgate3]
    dgains = [dnorm1_g, dnorm2_g, dnorm3_g, dfinal_g, dgroup_g]
    dbias = dbias_col[:n_heads, 0]
    return dx, loss_row, dmod, dgains, dbias, dconv_w


SMALL_ROWS = 24
ROW_GAINS, ROW_LOSS, ROW_FORGET, ROW_CONV, ROW_MOD = 0, 5, 6, 7, 10
PROW_ADA_B, PROW_GAINS, PROW_FORGET, PROW_CONV = 0, 9, 14, 15


def _round_up(n, m):
    return -(-n // m) * m


def _pad_rows(a, rows):
    return jnp.pad(a, ((0, rows - a.shape[0]), (0, 0)))


def _halves(a):
    return a.reshape(2, a.shape[0] // 2, a.shape[1])


def _rows_at(a, r0, total, width):
    return jnp.pad(a, ((r0, total - r0 - a.shape[0]), (0, width - a.shape[1])))


def kernel(x, c, ada_w, ada_b, norm1_g, ffn1_w_gate, ffn1_w_up, ffn1_w_down, norm2_g, w_in, forget_bias, conv_w, group_norm_g, w_out, norm3_g, ffn2_w_gate, ffn2_w_up, ffn2_w_down, final_g, loss_target, m_ada_w, m_ada_b, m_norm1_g, m_ffn1_w_gate, m_ffn1_w_up, m_ffn1_w_down, m_norm2_g, m_w_in, m_forget_bias, m_conv_w, m_group_norm_g, m_w_out, m_norm3_g, m_ffn2_w_gate, m_ffn2_w_up, m_ffn2_w_down, m_final_g, v_ada_w, v_ada_b, v_norm1_g, v_ffn1_w_gate, v_ffn1_w_up, v_ffn1_w_down, v_norm2_g, v_w_in, v_forget_bias, v_conv_w, v_group_norm_g, v_w_out, v_norm3_g, v_ffn2_w_gate, v_ffn2_w_up, v_ffn2_w_down, v_final_g):
    xi, yi, ci = lax.axis_index("x"), lax.axis_index("y"), lax.axis_index("c")
    chip = 2 * xi + yi
    dev = 4 * xi + 2 * yi + ci
    _, s, d = x.shape
    att_w = d // 2
    conv_width = d - att_w
    n_heads = att_w // HEAD_DIM
    in_shard = w_in.shape[1]
    in_rows = _round_up(in_shard, 32)
    cs = conv_w.shape[1]
    mod_shard = ada_w.shape[1]
    assert N_MOD * d == N_CHIPS * mod_shard and conv_width == N_CHIPS * cs and n_heads % 2 == 0

    def t_bf(w):
        return w.T.astype(BF16)

    shards = {"ffn1_gu": [_halves(t_bf(ffn1_w_gate)), _halves(t_bf(ffn1_w_up))], "ffn1_d": [_halves(ffn1_w_down.astype(BF16))],
              "mix_in": [_halves(_pad_rows(t_bf(w_in), in_rows))], "mix_out": [_halves(w_out.astype(BF16))],
              "ffn2": [_halves(t_bf(ffn2_w_gate)), _halves(t_bf(ffn2_w_up)), _halves(ffn2_w_down.astype(BF16))]}
    core = ci.astype(jnp.int32).reshape(1)
    chip_arr = chip.astype(jnp.int32).reshape(1)
    traffic = _WeightTraffic(shards, core, chip_arr)

    cond = _small_gather_exchange(_rows_at(c, 0, 8, d) + _rows_at(conv_w, 1, 8, d))
    _run_exchange(_join(traffic.gather_ici("ffn1_gu"), cond), "gather_ffn1_ici")
    got0 = cond.results[0].reshape(N_DEV, 8, d)
    c16 = _pad_rows(got0[:, 0, :], 16)
    conv_full = got0[0::2, 1:1 + CONV_K, :cs].transpose(1, 0, 2).reshape(CONV_K, conv_width)

    ada_b_mine = lax.dynamic_slice(ada_b, (chip * mod_shard,), (mod_shard,))[None, :]
    mods = _small_gather_exchange(_ada_fwd(c16, ada_w, ada_b_mine, "ada_fwd"))
    _run_exchange(_join(traffic.gather_pass("ffn1_gu"), mods), "gather_ffn1_pass")
    got1 = mods.results[0].reshape(N_DEV, 16, mod_shard)
    mod_mine = lax.dynamic_index_in_dim(got1[0::2], dev, axis=1, keepdims=False).reshape(N_MOD, d)
    mod = [mod_mine[i:i + 1] for i in range(N_MOD)]

    gains = [g[None, :] for g in (norm1_g, norm2_g, norm3_g, final_g, group_norm_g)]
    dx, loss_row, dmod, dgains, dbias, dconv_w = _layer_step(
        x[0], loss_target[0], mod, gains, forget_bias, conv_full, traffic, att_w, in_shard, in_rows)

    pack = sum(_rows_at(g, ROW_GAINS + i, SMALL_ROWS, d) for i, g in enumerate(dgains))
    pack += _rows_at(loss_row, ROW_LOSS, SMALL_ROWS, d) + _rows_at(dbias[None, :], ROW_FORGET, SMALL_ROWS, d)
    pack += _rows_at(dconv_w, ROW_CONV, SMALL_ROWS, d)
    pack += sum(_rows_at(g, ROW_MOD + i, SMALL_ROWS, d) for i, g in enumerate(dmod))
    small = _small_gather_exchange(pack)
    _run_exchange(_join(traffic.share("ffn1_wu"), small), "gather_small_grads")
    got2 = small.results[0].reshape(N_DEV, SMALL_ROWS, d)
    tot = _sum_devices(got2, "sum_small_grads")
    loss = tot[ROW_LOSS, 0]
    grad_ada_b = tot[ROW_MOD:ROW_MOD + N_MOD].reshape(N_MOD * d)
    grad_conv = lax.dynamic_slice(tot[ROW_CONV:ROW_CONV + CONV_K], (0, chip * cs), (CONV_K, cs))
    dmod_all = got2[:, ROW_MOD:ROW_MOD + N_MOD, :].reshape(N_DEV, N_MOD * d)
    dmod16 = _pad_rows(lax.dynamic_slice(dmod_all, (0, chip * mod_shard), (N_DEV, mod_shard)), 16)

    out = {"ada_w": tuple(_ada_update(c16.T, dmod16, ada_w, m_ada_w, v_ada_w, "adamw_ada_w"))}
    totals = (traffic.totals("ffn1_wg") + traffic.totals("ffn1_wu") + traffic.totals("ffn1_wd")
              + traffic.totals("mix") + traffic.totals("ffn2"))

    names = ("ffn1_w_gate", "ffn1_w_up", "ffn1_w_down", "w_in", "w_out", "ffn2_w_gate", "ffn2_w_up", "ffn2_w_down")
    transposed = ("ffn1_w_gate", "ffn1_w_up", "w_in", "ffn2_w_gate", "ffn2_w_up")
    params = {"ffn1_w_gate": (ffn1_w_gate, m_ffn1_w_gate, v_ffn1_w_gate), "ffn1_w_up": (ffn1_w_up, m_ffn1_w_up, v_ffn1_w_up),
              "ffn1_w_down": (ffn1_w_down, m_ffn1_w_down, v_ffn1_w_down), "w_in": (w_in, m_w_in, v_w_in),
              "w_out": (w_out, m_w_out, v_w_out), "ffn2_w_gate": (ffn2_w_gate, m_ffn2_w_gate, v_ffn2_w_gate),
              "ffn2_w_up": (ffn2_w_up, m_ffn2_w_up, v_ffn2_w_up), "ffn2_w_down": (ffn2_w_down, m_ffn2_w_down, v_ffn2_w_down)}
    for name_, (mine, theirs) in zip(names, totals):
        w, m, v = params[name_]
        if name_ in transposed:
            w, m, v = w.T, m.T, v.T
        if name_ == "w_in":
            both = jnp.where(ci == 0, jnp.concatenate([mine, theirs]), jnp.concatenate([theirs, mine]))[:in_shard]
            res = (both,) + tuple(_adamw(w, both, m, v, "adamw_" + name_))
        else:
            res = _adamw_halves(w, mine, theirs, m, v, core, "adamw_" + name_)
        out[name_] = tuple(r.T for r in res) if name_ in transposed else tuple(res)

    def small_pack(ada_b_, gains_, forget_, conv_):
        p = _rows_at(ada_b_.reshape(N_MOD, d), PROW_ADA_B, SMALL_ROWS, d)
        p += sum(_rows_at(g[None, :], PROW_GAINS + i, SMALL_ROWS, d) for i, g in enumerate(gains_))
        p += _rows_at(forget_[None, :], PROW_FORGET, SMALL_ROWS, d) + _rows_at(conv_, PROW_CONV, SMALL_ROWS, d)
        return p

    g_gains = [tot[ROW_GAINS + i] for i in range(5)]
    g_forget = tot[ROW_FORGET, :n_heads]
    sw = small_pack(ada_b, (norm1_g, norm2_g, norm3_g, final_g, group_norm_g), forget_bias, conv_w)
    sm = small_pack(m_ada_b, (m_norm1_g, m_norm2_g, m_norm3_g, m_final_g, m_group_norm_g), m_forget_bias, m_conv_w)
    sv = small_pack(v_ada_b, (v_norm1_g, v_norm2_g, v_norm3_g, v_final_g, v_group_norm_g), v_forget_bias, v_conv_w)
    sg = small_pack(grad_ada_b, g_gains, g_forget, grad_conv)
    small = (sg,) + tuple(_adamw(sw, sg, sm, sv, "adamw_small"))

    def unpack(p):
        r = {"ada_b": p[PROW_ADA_B:PROW_ADA_B + N_MOD].reshape(N_MOD * d), "forget_bias": p[PROW_FORGET, :n_heads],
             "conv_w": p[PROW_CONV:PROW_CONV + CONV_K, :cs]}
        for i, nm in enumerate(("norm1_g", "norm2_g", "norm3_g", "final_g", "group_norm_g")):
            r[nm] = p[PROW_GAINS + i]
        return r

    small = [unpack(p) for p in small]
    order = ("ada_w", "ada_b", "norm1_g", "ffn1_w_gate", "ffn1_w_up", "ffn1_w_down", "norm2_g", "w_in", "forget_bias",
             "conv_w", "group_norm_g", "w_out", "norm3_g", "ffn2_w_gate", "ffn2_w_up", "ffn2_w_down", "final_g")
    result = [loss, dx[None]]
    for k in range(4):
        result += [out[nm][k] if nm in out else small[k][nm] for nm in order]
    return tuple(result)
```

```python
import functools
import math

import jax
import jax.numpy as jnp
from jax import lax
from jax.experimental import pallas as pl
from jax.experimental.pallas import tpu as pltpu

F32 = jnp.float32
BF16 = jnp.bfloat16

HEAD_DIM = 64
CONV_K = 3
N_MOD = 9
EPS = 1e-6
ADAM_LR = 0.001
ADAM_B1 = 0.9
ADAM_B2 = 0.999
ADAM_EPS = 1e-08
ADAM_WD = 0.01
ADAM_STEP = 10

LANES = 128
N_CHIPS = 4
N_DEV = 8
VMEM_LIMIT_BYTES = 56 * 1024 * 1024
MAX_CONTRACTION = 4096
NEG_BIG = -1e30
MESH = pl.DeviceIdType.MESH

_NT = (((1,), (1,)), ((), ()))
_NN = (((1,), (0,)), ((), ()))
_TN = (((0,), (0,)), ((), ()))


def _params(*sem):
    return pltpu.CompilerParams(dimension_semantics=sem, vmem_limit_bytes=VMEM_LIMIT_BYTES)


class _Exchange:
    def __init__(self, inputs, out_shapes, n_sems, start, finish, aliases=None):
        self.inputs, self.out_shapes, self.n_sems = list(inputs), list(out_shapes), n_sems
        self.start, self.finish, self.aliases = start, finish, dict(aliases or {})
        self.results = None

    def set_results(self, results):
        self.results = list(results)


class _SemaphoreWindow:
    def __init__(self, sems, base):
        self.sems, self.base = sems, base
        self.at = self

    def __getitem__(self, k):
        return self.sems.at[self.base + k]


class _JoinedExchange(_Exchange):
    def __init__(self, parts):
        self.parts = parts
        aliases, i0, o0 = {}, 0, 0
        for p in parts:
            aliases.update({i0 + a: o0 + b for a, b in p.aliases.items()})
            i0, o0 = i0 + len(p.inputs), o0 + len(p.out_shapes)

        def each(method, src, dst, send_sems, recv_sems):
            i0 = o0 = s0 = 0
            for p in parts:
                i1, o1 = i0 + len(p.inputs), o0 + len(p.out_shapes)
                getattr(p, method)(src[i0:i1], dst[o0:o1], _SemaphoreWindow(send_sems, s0), _SemaphoreWindow(recv_sems, s0))
                i0, o0, s0 = i1, o1, s0 + p.n_sems

        super().__init__([a for p in parts for a in p.inputs], [o for p in parts for o in p.out_shapes],
                         sum(p.n_sems for p in parts), functools.partial(each, "start"), functools.partial(each, "finish"),
                         aliases)

    def set_results(self, results):
        o0 = 0
        for p in self.parts:
            p.set_results(results[o0:o0 + len(p.out_shapes)])
            o0 += len(p.out_shapes)


def _join(*parts):
    return parts[0] if len(parts) == 1 else _JoinedExchange(list(parts))


def _pc(body, exchange=None, **kw):
    if exchange is None:
        return pl.pallas_call(body, **kw)
    grid = kw["grid"]
    single = not isinstance(kw["out_shape"], (tuple, list))
    out_shape = [kw["out_shape"]] if single else list(kw["out_shape"])
    out_specs = [kw["out_specs"]] if single else list(kw["out_specs"])
    in_specs = list(kw["in_specs"])
    scratch = list(kw.get("scratch_shapes", ()))
    n_in, n_out, n_scr = len(in_specs), len(out_shape), len(scratch)
    n_xi, n_xo = len(exchange.inputs), len(exchange.out_shapes)

    def wrapped(*refs):
        pos = [n_in, n_in + n_xi, n_in + n_xi + n_out, n_in + n_xi + n_out + n_xo]
        ins, x_in, outs, x_out = refs[:pos[0]], refs[pos[0]:pos[1]], refs[pos[1]:pos[2]], refs[pos[2]:pos[3]]
        scr = refs[pos[3]:pos[3] + n_scr]
        send_sems, recv_sems = refs[pos[3] + n_scr:]
        ids = [pl.program_id(a) for a in range(len(grid))]
        first = functools.reduce(jnp.logical_and, [i == 0 for i in ids])
        last = functools.reduce(jnp.logical_and, [i == g - 1 for i, g in zip(ids, grid)])

        @pl.when(first)
        def _():
            exchange.start(x_in, x_out, send_sems, recv_sems)

        body(*ins, *outs, *scr)

        @pl.when(last)
        def _():
            exchange.finish(x_in, x_out, send_sems, recv_sems)

    call = pl.pallas_call(
        wrapped, out_shape=tuple(out_shape) + tuple(exchange.out_shapes), grid=grid,
        in_specs=in_specs + [_ANY] * n_xi, out_specs=tuple(out_specs) + (_ANY,) * n_xo,
        scratch_shapes=scratch + [pltpu.SemaphoreType.DMA((exchange.n_sems,)), pltpu.SemaphoreType.DMA((exchange.n_sems,))],
        input_output_aliases={n_in + a: n_out + b for a, b in exchange.aliases.items()},
        compiler_params=_params(*(["arbitrary"] * len(grid))), name=kw["name"])

    def run(*args):
        res = call(*args, *exchange.inputs)
        exchange.set_results(res[n_out:])
        return res[0] if single else tuple(res[:n_out])

    return run


_ANY = pl.BlockSpec(memory_space=pl.ANY)


def _tile(n, pref, mult):
    best = None
    t = mult
    while t <= min(n, pref):
        if n % t == 0:
            best = t
        t += mult
    return n if best is None else best


def _sds(shape, dtype):
    return jax.ShapeDtypeStruct(shape, dtype)


def _vec_spec(d):
    return pl.BlockSpec((1, d), lambda *_: (0, 0))


def _norm_mod_fwd(x, g, shift, scale, name):
    s, d = x.shape
    tr = _tile(s, 512, 16)

    def body(x_ref, g_ref, sh_ref, sc_ref, h_ref):
        xv = x_ref[...]
        rstd = lax.rsqrt(jnp.mean(xv * xv, axis=-1, keepdims=True) + EPS)
        n = xv * rstd * g_ref[...]
        h_ref[...] = (n * (1.0 + sc_ref[...]) + sh_ref[...]).astype(BF16)

    row = pl.BlockSpec((tr, d), lambda i: (i, 0))
    return _pc(body, out_shape=_sds((s, d), BF16), grid=(s // tr,),
               in_specs=[row, _vec_spec(d), _vec_spec(d), _vec_spec(d)], out_specs=row,
               compiler_params=_params("parallel"), name=name)(x, g, shift, scale)


def _through_gate(dx, f_ref, gate_ref, df_ref, dgate_ref):
    df_ref[...] = (dx * gate_ref[...]).astype(BF16)
    dgate_ref[...] += jnp.sum(dx * f_ref[...].astype(F32), axis=0, keepdims=True)


def _norm_mod_bwd(dh, x, g, scale, dres, name, f=None, gate=None, rows=None, exchange=None):
    d = x.shape[1]
    first_row, s = rows if rows else (0, x.shape[0])
    gated = f is not None
    terms = list(zip(*dh)) if isinstance(dh, tuple) else None
    tr = _tile(s, 256, 16)
    b0 = first_row // tr
    assert first_row % tr == 0
    n_lead = 2 * len(terms) if terms else 1

    def body(*refs):
        lead, (x_ref, g_ref, sc_ref, dres_ref), rest = refs[:n_lead], refs[n_lead:n_lead + 4], refs[n_lead + 4:]
        f_ref, gate_ref = rest[:2] if gated else (None, None)
        dx_ref, dsh_ref, dsc_ref, dg_ref = rest[2:6] if gated else rest[:4]
        df_ref, dgate_ref = rest[6:8] if gated else (None, None)

        @pl.when(pl.program_id(0) == 0)
        def _():
            for ref in (dsh_ref, dsc_ref, dg_ref) + ((dgate_ref,) if gated else ()):
                ref[...] = jnp.zeros_like(ref)

        if terms:
            dhv = lax.dot_general(lead[0][...], lead[1][...], _NN, preferred_element_type=F32)
            for p in range(1, len(terms)):
                dhv += lax.dot_general(lead[2 * p][...], lead[2 * p + 1][...], _NN, preferred_element_type=F32)
        else:
            dhv = lead[0][...]
        xv = x_ref[...]
        gv = g_ref[...]
        rstd = lax.rsqrt(jnp.mean(xv * xv, axis=-1, keepdims=True) + EPS)
        xhat = xv * rstd
        dn = dhv * (1.0 + sc_ref[...])
        dsh_ref[...] += jnp.sum(dhv, axis=0, keepdims=True)
        dsc_ref[...] += jnp.sum(dhv * (xhat * gv), axis=0, keepdims=True)
        dg_ref[...] += jnp.sum(dn * xhat, axis=0, keepdims=True)
        dxh = dn * gv
        proj = jnp.mean(dxh * xhat, axis=-1, keepdims=True)
        dx = dres_ref[...] + rstd * (dxh - xhat * proj)
        dx_ref[...] = dx
        if gated:
            _through_gate(dx, f_ref, gate_ref, df_ref, dgate_ref)

    row = pl.BlockSpec((tr, d), lambda i: (b0 + i, 0))
    out_row = pl.BlockSpec((tr, d), lambda i: (i, 0))
    vec = _vec_spec(d)
    if terms:
        in_specs, args = [], []
        for l, r in terms:
            assert l.shape[1] == r.shape[0] <= MAX_CONTRACTION and r.shape[1] == d
            in_specs += [pl.BlockSpec((tr, l.shape[1]), lambda i: (b0 + i, 0)), pl.BlockSpec(r.shape, lambda i: (0, 0))]
            args += [l, r]
    else:
        in_specs, args = [row], [dh]
    in_specs += [row, vec, vec, row]
    args += [x, g, scale, dres]
    out_shape = [_sds((s, d), F32), _sds((1, d), F32), _sds((1, d), F32), _sds((1, d), F32)]
    out_specs = [out_row, vec, vec, vec]
    if gated:
        out_shape += [_sds((s, d), BF16), _sds((1, d), F32)]
        out_specs += [out_row, vec]
        in_specs += [row, vec]
        args += [f, gate]
    return _pc(body, exchange, out_shape=tuple(out_shape), grid=(s // tr,), in_specs=in_specs,
               out_specs=tuple(out_specs), compiler_params=_params("arbitrary"), name=name)(*args)


def _down_final_loss(hid, wd, res, gate, g, target, name):
    s, d = res.shape
    k = hid.shape[1]
    assert k <= MAX_CONTRACTION
    tr = _tile(s, 256, 16)
    nsteps = s // tr

    def body(hid_ref, wd_ref, res_ref, gate_ref, g_ref, t_ref, dx_ref, loss_ref, dg_ref, df_ref, dgate_ref):
        i = pl.program_id(0)

        @pl.when(i == 0)
        def _():
            loss_ref[...] = jnp.zeros_like(loss_ref)
            dg_ref[...] = jnp.zeros_like(dg_ref)
            dgate_ref[...] = jnp.zeros_like(dgate_ref)

        f = lax.dot_general(hid_ref[...], wd_ref[...], _NN, preferred_element_type=F32)
        gatev = gate_ref[...]
        xv = res_ref[...] + gatev * f
        gv = g_ref[...]
        rstd = lax.rsqrt(jnp.mean(xv * xv, axis=-1, keepdims=True) + EPS)
        xhat = xv * rstd
        err = xhat * gv - t_ref[...]
        dy = err * (1.0 / d)
        loss_ref[...] += jnp.sum(0.5 * err * dy, axis=0, keepdims=True)
        dg_ref[...] += jnp.sum(dy * xhat, axis=0, keepdims=True)
        dxh = dy * gv
        proj = jnp.mean(dxh * xhat, axis=-1, keepdims=True)
        dx = rstd * (dxh - xhat * proj)
        dx_ref[...] = dx
        df_ref[...] = (dx * gatev).astype(BF16)
        dgate_ref[...] += jnp.sum(dx * f, axis=0, keepdims=True)

        @pl.when(i == nsteps - 1)
        def _():
            loss_ref[...] = jnp.broadcast_to(jnp.sum(loss_ref[...], axis=-1, keepdims=True), loss_ref.shape)

    row = pl.BlockSpec((tr, d), lambda i: (i, 0))
    vec = _vec_spec(d)
    return _pc(body, out_shape=(_sds((s, d), F32), _sds((1, d), F32), _sds((1, d), F32), _sds((s, d), BF16), _sds((1, d), F32)),
               grid=(nsteps,),
               in_specs=[pl.BlockSpec((tr, k), lambda i: (i, 0)), pl.BlockSpec((k, d), lambda i: (0, 0)), row, vec, vec, row],
               out_specs=(row, vec, vec, row, vec),
               compiler_params=_params("arbitrary"), name=name)(hid, wd, res, gate, g, target)


def _mm(lhs, rhs, dims, out_dtype, name, res=None, gate=None, aux_dtype=None, norm=None, exchange=None):
    lhs_list = list(lhs) if isinstance(lhs, (list, tuple)) else [lhs]
    rhs_list = list(rhs) if isinstance(rhs, (list, tuple)) else [rhs]
    n_terms = len(lhs_list)
    assert n_terms == len(rhs_list)
    m = lhs_list[0].shape[1 if dims == "tn" else 0]
    n = rhs_list[0].shape[0 if dims == "nt" else 1]
    tn = _tile(n, 1024, LANES)
    tm = _tile(m, 512, LANES if dims == "tn" else 16)
    dn = {"nn": _NN, "nt": _NT, "tn": _TN}[dims]
    in_specs, args = [], []
    for l, r in zip(lhs_list, rhs_list):
        k = l.shape[0 if dims == "tn" else 1]
        assert k == r.shape[1 if dims == "nt" else 0] and k <= MAX_CONTRACTION, (l.shape, r.shape, dims)
        in_specs.append(pl.BlockSpec((k, tm), lambda i, j: (0, i)) if dims == "tn" else pl.BlockSpec((tm, k), lambda i, j: (i, 0)))
        in_specs.append(pl.BlockSpec((tn, k), lambda i, j: (j, 0)) if dims == "nt" else pl.BlockSpec((k, tn), lambda i, j: (0, j)))
        args += [l, r]
    out_spec = pl.BlockSpec((tm, tn), lambda i, j: (i, j))
    has_res, has_gate, has_aux, has_norm = res is not None, gate is not None, aux_dtype is not None, norm is not None
    assert not has_norm or tn == n

    def body(*refs):
        refs = list(refs)
        pos = 2 * n_terms
        res_ref = gate_ref = aux_ref = None
        if has_res:
            res_ref = refs[pos]; pos += 1
        if has_gate:
            gate_ref = refs[pos]; pos += 1
        if has_norm:
            ng_ref, nsh_ref, nsc_ref = refs[pos:pos + 3]; pos += 3
        out_ref = refs[pos]; pos += 1
        if has_aux:
            aux_ref = refs[pos]; pos += 1
        acc = lax.dot_general(refs[0][...], refs[1][...], dn, preferred_element_type=F32)
        for p in range(1, n_terms):
            acc += lax.dot_general(refs[2 * p][...], refs[2 * p + 1][...], dn, preferred_element_type=F32)
        if has_aux:
            aux_ref[...] = acc.astype(aux_dtype)
        if has_gate:
            acc = acc * gate_ref[...]
        if has_res:
            acc = res_ref[...] + acc
        out_ref[...] = acc.astype(out_dtype)
        if has_norm:
            rstd = lax.rsqrt(jnp.mean(acc * acc, axis=-1, keepdims=True) + EPS)
            refs[pos][...] = (acc * rstd * ng_ref[...] * (1.0 + nsc_ref[...]) + nsh_ref[...]).astype(BF16)

    if has_res:
        in_specs.append(out_spec); args.append(res)
    if has_gate:
        in_specs.append(pl.BlockSpec((1, tn), lambda i, j: (0, j))); args.append(gate)
    if has_norm:
        in_specs += [pl.BlockSpec((1, tn), lambda i, j: (0, j))] * 3
        args += list(norm)
    out_shape = [_sds((m, n), out_dtype)]
    out_specs = [out_spec]
    if has_aux:
        out_shape.append(_sds((m, n), aux_dtype)); out_specs.append(out_spec)
    if has_norm:
        out_shape.append(_sds((m, n), BF16)); out_specs.append(out_spec)
    outs = _pc(body, exchange, out_shape=tuple(out_shape), grid=(m // tm, n // tn), in_specs=in_specs,
               out_specs=tuple(out_specs), compiler_params=_params("parallel", "parallel"), name=name)(*args)
    return outs if len(out_shape) > 1 else outs[0]


def _project(h, weights_t, out_dtypes, name, exchange=None):
    s, d = h.shape
    n = len(weights_t)
    tm = _tile(s, 512, 16)

    def body(*refs):
        hv = refs[0][...]
        for i in range(n):
            acc = lax.dot_general(hv, refs[1 + i][...], _NT, preferred_element_type=F32)
            refs[1 + n + i][...] = acc.astype(out_dtypes[i])

    return _pc(body, exchange, out_shape=tuple(_sds((s, w.shape[0]), dt) for w, dt in zip(weights_t, out_dtypes)),
               grid=(s // tm,),
               in_specs=[pl.BlockSpec((tm, d), lambda i: (i, 0))] + [pl.BlockSpec(w.shape, lambda i: (0, 0)) for w in weights_t],
               out_specs=tuple(pl.BlockSpec((tm, w.shape[0]), lambda i: (i, 0)) for w in weights_t),
               compiler_params=_params("parallel"), name=name)(h, *weights_t)


def _weight_grads(cotangents, h, name, exchange=None):
    s, d = h.shape
    m = cotangents[0].shape[1]
    n = len(cotangents)
    assert s <= MAX_CONTRACTION and all(c.shape == (s, m) for c in cotangents)
    tm = _tile(m, 256, LANES)

    def body(*refs):
        hv = refs[n][...]
        for i in range(n):
            refs[n + 1 + i][...] = lax.dot_general(refs[i][...], hv, _TN, preferred_element_type=F32).astype(BF16)

    return _pc(body, exchange, out_shape=(_sds((m, d), BF16),) * n, grid=(m // tm,),
               in_specs=[pl.BlockSpec((s, tm), lambda i: (0, i))] * n + [pl.BlockSpec((s, d), lambda i: (0, 0))],
               out_specs=(pl.BlockSpec((tm, d), lambda i: (i, 0)),) * n,
               compiler_params=_params("parallel"), name=name)(*cotangents, h)


def _ffn_up(h, wg_t, wu_t, name, exchange=None):
    s, d = h.shape
    f = wg_t.shape[0]
    tm = _tile(s, 1024, 16)
    tn = _tile(f, 256, LANES)

    def body(h_ref, wg_ref, wu_ref, a_ref, u_ref, hid_ref):
        hv = h_ref[...]
        a = lax.dot_general(hv, wg_ref[...], _NT, preferred_element_type=F32)
        u = lax.dot_general(hv, wu_ref[...], _NT, preferred_element_type=F32)
        a_ref[...] = a.astype(BF16)
        u_ref[...] = u.astype(BF16)
        hid_ref[...] = (a * jax.nn.sigmoid(a) * u).astype(BF16)

    hs = pl.BlockSpec((tm, d), lambda i, j: (i, 0))
    ws = pl.BlockSpec((tn, d), lambda i, j: (j, 0))
    os_ = pl.BlockSpec((tm, tn), lambda i, j: (i, j))
    return _pc(body, exchange, out_shape=(_sds((s, f), BF16),) * 3, grid=(s // tm, f // tn),
               in_specs=[hs, ws, ws], out_specs=(os_, os_, os_),
               compiler_params=_params("parallel", "parallel"), name=name)(h, wg_t, wu_t)


def _ffn_dact(df, wd, a, u, name, exchange=None):
    s, d = df.shape
    f = wd.shape[0]
    tm = _tile(s, 1024, 16)
    tn = _tile(f, 256, LANES)

    def body(df_ref, wd_ref, a_ref, u_ref, da_ref, du_ref):
        dhid = lax.dot_general(df_ref[...], wd_ref[...], _NT, preferred_element_type=F32)
        av = a_ref[...].astype(F32)
        uv = u_ref[...].astype(F32)
        sig = jax.nn.sigmoid(av)
        da_ref[...] = (dhid * uv * (sig * (1.0 + av * (1.0 - sig)))).astype(BF16)
        du_ref[...] = (dhid * (av * sig)).astype(BF16)

    ds_ = pl.BlockSpec((tm, d), lambda i, j: (i, 0))
    ws = pl.BlockSpec((tn, d), lambda i, j: (j, 0))
    os_ = pl.BlockSpec((tm, tn), lambda i, j: (i, j))
    return _pc(body, exchange, out_shape=(_sds((s, f), BF16),) * 2, grid=(s // tm, f // tn),
               in_specs=[ds_, ws, os_, os_], out_specs=(os_, os_),
               compiler_params=_params("parallel", "parallel"), name=name)(df, wd, a, u)


def _split3(v):
    hi = v.astype(BF16)
    r1 = v - hi.astype(F32)
    mid = r1.astype(BF16)
    lo = (r1 - mid.astype(F32)).astype(BF16)
    return hi, mid, lo


def _dot3(v, mat):
    hi, mid, lo = _split3(v)
    out = lax.dot_general(hi, mat, _NN, preferred_element_type=F32)
    out += lax.dot_general(mid, mat, _NN, preferred_element_type=F32)
    out += lax.dot_general(lo, mat, _NN, preferred_element_type=F32)
    return out


def _forget_fwd(flog_t, bias, name):
    h, s = flog_t.shape
    blk = _tile(s, 512, LANES)
    tri = (jnp.arange(blk)[:, None] <= jnp.arange(blk)[None, :]).astype(BF16)

    def body(z_ref, b_ref, tri_ref, f_ref, carry):
        @pl.when(pl.program_id(0) == 0)
        def _():
            carry[...] = jnp.zeros_like(carry)

        z = z_ref[...] + b_ref[...]
        e = jnp.exp(-jnp.abs(z))
        w = 1.0 + e
        log1p_e = jnp.where(w == 1.0, e, jnp.log(w) * (e / (w - 1.0)))
        lf = jnp.minimum(z, 0.0) - log1p_e
        out = carry[...] + _dot3(lf, tri_ref[...])
        for j, piece in enumerate(_split3(out)):
            f_ref[j] = piece
        carry[...] = out[:, blk - 1:blk]

    zs = pl.BlockSpec((h, blk), lambda i: (0, i))
    return _pc(body, out_shape=_sds((3, h, s), BF16), grid=(s // blk,),
               in_specs=[zs, pl.BlockSpec((h, 1), lambda i: (0, 0)), pl.BlockSpec((blk, blk), lambda i: (0, 0))],
               out_specs=pl.BlockSpec((3, h, blk), lambda i: (0, 0, i)), scratch_shapes=[pltpu.VMEM((h, 1), F32)],
               compiler_params=_params("arbitrary"), name=name)(flog_t, bias, tri)


def _forget_bwd(df_t, flog_t, bias, name):
    h, s = flog_t.shape
    blk = _tile(s, 512, LANES)
    nb = s // blk
    tri = (jnp.arange(blk)[:, None] >= jnp.arange(blk)[None, :]).astype(BF16)

    def body(df_ref, z_ref, b_ref, tri_ref, dz_ref, db_ref, carry):
        @pl.when(pl.program_id(0) == 0)
        def _():
            carry[...] = jnp.zeros_like(carry)
            db_ref[...] = jnp.zeros_like(db_ref)

        rc = carry[...] + _dot3(df_ref[...], tri_ref[...])
        carry[...] = rc[:, 0:1]
        dz = rc * jax.nn.sigmoid(-(z_ref[...] + b_ref[...]))
        dz_ref[...] = dz
        db_ref[...] += jnp.sum(dz, axis=-1, keepdims=True)

    rev = pl.BlockSpec((h, blk), lambda i: (0, nb - 1 - i))
    col = pl.BlockSpec((h, 1), lambda i: (0, 0))
    return _pc(body, out_shape=(_sds((h, s), F32), _sds((h, 1), F32)), grid=(nb,),
               in_specs=[rev, rev, col, pl.BlockSpec((blk, blk), lambda i: (0, 0))],
               out_specs=(rev, col), scratch_shapes=[pltpu.VMEM((h, 1), F32)],
               compiler_params=_params("arbitrary"), name=name)(df_t, flog_t, bias, tri)


def _attn_tiles(s):
    return _tile(s, 1024, LANES)


def _attn_half(t):
    return t // 2 if t >= 4 * LANES else t


BIAS_ROWS = 16


def _attn_prep(qkv, f_pieces, name):
    s = qkv.shape[0]
    a_w = qkv.shape[1] // 3
    npair = a_w // LANES
    t = _attn_tiles(s)
    scale = 1.0 / math.sqrt(HEAD_DIM)

    six = f_pieces[:, :2 * npair].reshape(3, npair, 2, s).transpose(1, 3, 2, 0).reshape(npair, s, 6)
    feat = jnp.concatenate([six, jnp.ones((npair, s, 1), BF16), jnp.zeros((npair, s, BIAS_ROWS - 7), BF16)], axis=-1)
    place_q = [[0.0] * (2 * LANES) for _ in range(BIAS_ROWS)]
    place_k = [[0.0] * (2 * LANES) for _ in range(BIAS_ROWS)]
    for hh in range(2):
        b0 = hh * LANES + (HEAD_DIM if hh == 0 else 0)
        for j in range(3):
            place_q[3 * hh + j][b0 + j] = 1.0
            place_q[6][b0 + 3 + j] = 1.0
            place_k[6][b0 + j] = 1.0
            place_k[3 * hh + j][b0 + 3 + j] = -1.0
    place_q = jnp.array(place_q, BF16)
    place_k = jnp.array(place_k, BF16)

    def body(q_ref, k_ref, v_ref, f_ref, pq_ref, pk_ref, qa_ref, ka_ref, va_ref):
        lane = lax.broadcasted_iota(jnp.int32, (1, LANES), 1)
        q2 = (q_ref[...].astype(F32) * scale).astype(BF16)
        k2, v2 = k_ref[...], v_ref[...]
        qx = lax.dot_general(f_ref[0], pq_ref[...], _NN, preferred_element_type=F32).astype(BF16)
        kx = lax.dot_general(f_ref[0], pk_ref[...], _NN, preferred_element_type=F32).astype(BF16)
        for hh in range(2):
            real = (lane < HEAD_DIM) if hh == 0 else (lane >= HEAD_DIM)
            cols = slice(hh * LANES, (hh + 1) * LANES)
            qa_ref[:, cols] = jnp.where(real, q2, qx[:, cols])
            ka_ref[:, cols] = jnp.where(real, k2, kx[:, cols])
            va_ref[:, cols] = jnp.where(real, v2, jnp.zeros_like(v2))

    def col(off):
        return pl.BlockSpec((t, LANES), lambda p, i: (i, off + p))

    out = pl.BlockSpec((t, 2 * LANES), lambda p, i: (i, p))
    place = pl.BlockSpec((BIAS_ROWS, 2 * LANES), lambda p, i: (0, 0))
    return _pc(body, out_shape=(_sds((s, 2 * a_w), BF16),) * 3, grid=(npair, s // t),
               in_specs=[col(0), col(npair), col(2 * npair), pl.BlockSpec((1, t, BIAS_ROWS), lambda p, i: (p, i, 0)),
                         place, place],
               out_specs=(out, out, out), compiler_params=_params("parallel", "parallel"), name=name)(
                   qkv, qkv, qkv, feat, place_q, place_k)


def _attn_fwd(qa, ka, va, name, exchange=None):
    s = qa.shape[0]
    a_w = qa.shape[1] // 2
    npair = a_w // LANES
    t = _attn_tiles(s)
    nq = s // t
    half = _attn_half(t)

    def body(q_ref, k_ref, v_ref, o_ref, lse_ref, m_sc, l_sc, acc_sc):
        qi = pl.program_id(1)
        first = lax.broadcasted_iota(jnp.int32, (1, LANES), 1) < HEAD_DIM
        m_sc[...] = jnp.full_like(m_sc, NEG_BIG)
        l_sc[...] = jnp.zeros_like(l_sc)
        acc_sc[...] = jnp.zeros_like(acc_sc)

        def step(q0, k_start, size, diag):
            q_sl = slice(q0, q0 + size)
            k_rows = pl.ds(pl.multiple_of(k_start, size), size)
            m_old = m_sc[q_sl, :]
            keep = None
            if diag:
                keep = (lax.broadcasted_iota(jnp.int32, (size, size), 0) >= lax.broadcasted_iota(jnp.int32, (size, size), 1))
            m_new, rs, pv = [], [], []
            for hh in range(2):
                cols = slice(hh * LANES, (hh + 1) * LANES)
                sc = lax.dot_general(q_ref[q_sl, cols], k_ref[k_rows, cols], _NT, preferred_element_type=F32)
                if diag:
                    sc = jnp.where(keep, sc, NEG_BIG)
                mo = m_old[:, hh * HEAD_DIM:hh * HEAD_DIM + 1]
                mn = jnp.maximum(mo, jnp.max(sc, axis=1, keepdims=True))
                p = jnp.exp(sc - mn)
                m_new.append(mn)
                rs.append(jnp.sum(p, axis=1, keepdims=True))
                pv.append(lax.dot_general(p.astype(BF16), v_ref[k_rows, cols], _NN, preferred_element_type=F32))
            m2 = jnp.where(first, m_new[0], m_new[1])
            alpha = jnp.exp(m_old - m2)
            m_sc[q_sl, :] = m2
            l_sc[q_sl, :] = alpha * l_sc[q_sl, :] + jnp.where(first, rs[0], rs[1])
            acc_sc[q_sl, :] = alpha * acc_sc[q_sl, :] + pv[0] + pv[1]

        def below_diagonal(ki, carry):
            step(0, ki * t, t, False)
            return carry

        lax.fori_loop(0, qi, below_diagonal, 0)
        step(0, qi * t, half, True)
        if half < t:
            step(half, qi * t, half, False)
            step(half, qi * t + half, half, True)
        l2 = l_sc[...]
        o_ref[...] = acc_sc[...] / l2
        lse_ref[...] = m_sc[...] + jnp.log(l2)

    qs = pl.BlockSpec((t, 2 * LANES), lambda p, qi: (qi, p))
    ks = pl.BlockSpec((s, 2 * LANES), lambda p, qi: (0, p))
    os_ = pl.BlockSpec((t, LANES), lambda p, qi: (qi, p))
    return _pc(body, exchange, out_shape=(_sds((s, a_w), F32), _sds((s, a_w), F32)), grid=(npair, nq),
               in_specs=[qs, ks, ks], out_specs=(os_, os_),
               scratch_shapes=[pltpu.VMEM((t, LANES), F32)] * 3,
               compiler_params=_params("parallel", "arbitrary"), name=name)(qa, ka, va)


def _attn_bwd(qa, ka, va, do, o, lse, name, exchange=None):
    s = qa.shape[0]
    a_w = qa.shape[1] // 2
    npair = a_w // LANES
    t = _attn_tiles(s)
    nq = s // t
    half = _attn_half(t)
    scale = 1.0 / math.sqrt(HEAD_DIM)

    def body(q_ref, k_ref, v_ref, do_ref, o_ref, lse_ref, dq_ref, dk_ref, dv_ref, qx_ref, kx_ref, dk_sc, dv_sc, kx_sc):
        ki = pl.program_id(1)
        first = lax.broadcasted_iota(jnp.int32, (1, LANES), 1) < HEAD_DIM

        @pl.when(ki == 0)
        def _():
            dq_ref[...] = jnp.zeros_like(dq_ref)
            qx_ref[...] = jnp.zeros_like(qx_ref)

        def step(q_start, k0, size, diag, assign):
            rows = pl.ds(pl.multiple_of(q_start, size), size)
            k_sl = slice(k0, k0 + size)
            do2 = do_ref[rows, :]
            lse2 = lse_ref[rows, :]
            dd = do2.astype(F32) * o_ref[rows, :]
            keep = None
            if diag:
                keep = (lax.broadcasted_iota(jnp.int32, (size, size), 0) >= lax.broadcasted_iota(jnp.int32, (size, size), 1))
            dq_h, dk_h, dv_h = [], [], []
            for hh in range(2):
                sel = first if hh == 0 else jnp.logical_not(first)
                cols = slice(hh * LANES, (hh + 1) * LANES)
                qh, kh, vh = q_ref[rows, cols], k_ref[k_sl, cols], v_ref[k_sl, cols]
                delta = jnp.sum(jnp.where(sel, dd, 0.0), axis=1, keepdims=True)
                sc = lax.dot_general(qh, kh, _NT, preferred_element_type=F32)
                if diag:
                    sc = jnp.where(keep, sc, NEG_BIG)
                p = jnp.exp(sc - lse2[:, hh * HEAD_DIM:hh * HEAD_DIM + 1])
                dp = lax.dot_general(do2, vh, _NT, preferred_element_type=F32)
                ds_b = (p * (dp - delta)).astype(BF16)
                dv_h.append(lax.dot_general(p.astype(BF16), do2, _TN, preferred_element_type=F32))
                dk_h.append(lax.dot_general(ds_b, qh, _TN, preferred_element_type=F32))
                dq_h.append(lax.dot_general(ds_b, kh, _NN, preferred_element_type=F32))
            dq_ref[rows, :] += jnp.where(first, dq_h[0], dq_h[1]) * scale
            qx_ref[rows, :] += jnp.where(first, dq_h[1], dq_h[0])
            dk_new = jnp.where(first, dk_h[0], dk_h[1])
            kx_new = jnp.where(first, dk_h[1], dk_h[0])
            dv_new = jnp.where(first, dv_h[0], dv_h[1])
            if assign:
                dk_sc[k_sl, :] = dk_new
                kx_sc[k_sl, :] = kx_new
                dv_sc[k_sl, :] = dv_new
            else:
                dk_sc[k_sl, :] += dk_new
                kx_sc[k_sl, :] += kx_new
                dv_sc[k_sl, :] += dv_new

        def below_diagonal(qi, carry):
            step(qi * t, 0, t, False, False)
            return carry

        step(ki * t, 0, half, True, True)
        if half < t:
            step(ki * t + half, 0, half, False, False)
            step(ki * t + half, half, half, True, True)
        lax.fori_loop(ki + 1, nq, below_diagonal, 0)
        dk_ref[...] = dk_sc[...].astype(BF16)
        dv_ref[...] = dv_sc[...].astype(BF16)
        kx_ref[...] = kx_sc[...]

    ks2 = pl.BlockSpec((t, 2 * LANES), lambda p, ki: (ki, p))
    qs2 = pl.BlockSpec((s, 2 * LANES), lambda p, ki: (0, p))
    whole = pl.BlockSpec((s, LANES), lambda p, ki: (0, p))
    kout = pl.BlockSpec((t, LANES), lambda p, ki: (ki, p))
    return _pc(body, exchange,
               out_shape=(_sds((s, a_w), F32), _sds((s, a_w), BF16), _sds((s, a_w), BF16), _sds((s, a_w), F32),
                          _sds((s, a_w), F32)),
               grid=(npair, nq), in_specs=[qs2, ks2, ks2, whole, whole, whole],
               out_specs=(whole, kout, kout, whole, kout),
               scratch_shapes=[pltpu.VMEM((t, LANES), F32)] * 3,
               compiler_params=_params("parallel", "arbitrary"), name=name)(qa, ka, va, do, o, lse)

def _decay_grads(qx, kx, name):
    s, a_w = qx.shape
    n_heads = a_w // HEAD_DIM
    tr = _tile(s, 512, 8)
    pick_q = [[0.0] * LANES for _ in range(a_w)]
    pick_k = [[0.0] * LANES for _ in range(a_w)]
    for h in range(n_heads):
        b0 = (h // 2) * LANES + (HEAD_DIM if h % 2 == 0 else 0)
        pick_q[b0][h] = 1.0
        pick_k[b0 + 3][h] = 1.0
    pick_q = jnp.array(pick_q, BF16)
    pick_k = jnp.array(pick_k, BF16)

    def body(qx_ref, kx_ref, pq_ref, pk_ref, o_ref):
        o_ref[...] = _dot3(qx_ref[...], pq_ref[...]) - _dot3(kx_ref[...], pk_ref[...])

    row = pl.BlockSpec((tr, a_w), lambda i: (i, 0))
    pick = pl.BlockSpec((a_w, LANES), lambda i: (0, 0))
    return _pc(body, out_shape=_sds((s, LANES), F32), grid=(s // tr,), in_specs=[row, row, pick, pick],
               out_specs=pl.BlockSpec((tr, LANES), lambda i: (i, 0)),
               compiler_params=_params("parallel"), name=name)(qx, kx, pick_q, pick_k)


def _shift_down(z, k, rows):
    return jnp.where(rows >= k, pltpu.roll(z, k, 0), 0.0)


def _shift_up(z, k, rows, n):
    return jnp.where(rows < n - k, pltpu.roll(z, n - k, 0), 0.0)


def _conv_fwd(bcx, conv_w, name):
    s = bcx.shape[0]
    cw = bcx.shape[1] // 3
    nb = cw // LANES

    def body(b_ref, c_ref, x_ref, w_ref, cv_ref):
        rows = lax.broadcasted_iota(jnp.int32, (s, LANES), 0)
        z = c_ref[...] * x_ref[...]
        w = w_ref[...]
        y = w[2:3, :] * z + w[1:2, :] * _shift_down(z, 1, rows) + w[0:1, :] * _shift_down(z, 2, rows)
        cv_ref[...] = b_ref[...] * y

    def col(off):
        return pl.BlockSpec((s, LANES), lambda j: (0, j + off))

    return _pc(body, out_shape=_sds((s, cw), F32), grid=(nb,),
               in_specs=[col(0), col(nb), col(2 * nb), pl.BlockSpec((CONV_K, LANES), lambda j: (0, j))],
               out_specs=col(0), compiler_params=_params("parallel"), name=name)(bcx, bcx, bcx, conv_w)


def _conv_bwd(dcv, bcx, conv_w, name):
    s = bcx.shape[0]
    cw = bcx.shape[1] // 3
    nb = cw // LANES

    def body(dcv_ref, b_ref, c_ref, x_ref, w_ref, db_ref, dc_ref, dxc_ref, dw_ref):
        rows = lax.broadcasted_iota(jnp.int32, (s, LANES), 0)
        cv_, xv = c_ref[...], x_ref[...]
        z = cv_ * xv
        w = w_ref[...]
        z1 = _shift_down(z, 1, rows)
        z2 = _shift_down(z, 2, rows)
        y = w[2:3, :] * z + w[1:2, :] * z1 + w[0:1, :] * z2
        dcvv = dcv_ref[...]
        db_ref[...] = (dcvv * y).astype(BF16)
        dy = dcvv * b_ref[...]
        dw_ref[0:1, :] = jnp.sum(dy * z2, axis=0, keepdims=True)
        dw_ref[1:2, :] = jnp.sum(dy * z1, axis=0, keepdims=True)
        dw_ref[2:3, :] = jnp.sum(dy * z, axis=0, keepdims=True)
        dz = w[2:3, :] * dy + w[1:2, :] * _shift_up(dy, 1, rows, s) + w[0:1, :] * _shift_up(dy, 2, rows, s)
        dc_ref[...] = (dz * xv).astype(BF16)
        dxc_ref[...] = (dz * cv_).astype(BF16)

    def col(off):
        return pl.BlockSpec((s, LANES), lambda j: (0, j + off))

    wspec = pl.BlockSpec((CONV_K, LANES), lambda j: (0, j))
    db, dc, dxc, dw = _pc(body, out_shape=(_sds((s, cw), BF16),) * 3 + (_sds((CONV_K, cw), F32),), grid=(nb,),
                          in_specs=[col(0), col(0), col(nb), col(2 * nb), wspec],
                          out_specs=(col(0), col(0), col(0), wspec),
                          compiler_params=_params("parallel"), name=name)(dcv, bcx, bcx, bcx, conv_w)
    return db, dc, dxc, dw


def _group_matrix():
    idx = jnp.arange(LANES) // HEAD_DIM
    return (idx[:, None] == idx[None, :]).astype(BF16)


def _group_sum(v, gmat):
    return _dot3(v, gmat)


def _gnorm_fwd(att, cv, gg, name):
    s, a_w = att.shape
    cw = cv.shape[1]
    d = a_w + cw
    tr = _tile(s, 512, 16)
    gmat = _group_matrix()

    def body(att_ref, cv_ref, gg_ref, gm_ref, yn_ref):
        gm = gm_ref[...]
        for c0 in range(0, d, LANES):
            y = att_ref[:, c0:c0 + LANES] if c0 < a_w else cv_ref[:, c0 - a_w:c0 - a_w + LANES]
            ms = _group_sum(y * y, gm) * (1.0 / HEAD_DIM)
            yn_ref[:, c0:c0 + LANES] = (y * lax.rsqrt(ms + EPS) * gg_ref[:, c0:c0 + LANES]).astype(BF16)

    return _pc(body, out_shape=_sds((s, d), BF16), grid=(s // tr,),
               in_specs=[pl.BlockSpec((tr, a_w), lambda i: (i, 0)), pl.BlockSpec((tr, cw), lambda i: (i, 0)),
                         _vec_spec(d), pl.BlockSpec((LANES, LANES), lambda i: (0, 0))],
               out_specs=pl.BlockSpec((tr, d), lambda i: (i, 0)),
               compiler_params=_params("parallel"), name=name)(att, cv, gg, gmat)


def _gnorm_bwd(dyn, att, cv, gg, name):
    s, a_w = att.shape
    cw = cv.shape[1]
    d = a_w + cw
    tr = _tile(s, 256, 16)
    gmat = _group_matrix()

    def body(dyn_ref, att_ref, cv_ref, gg_ref, gm_ref, datt_ref, dcv_ref, dgg_ref):
        @pl.when(pl.program_id(0) == 0)
        def _():
            dgg_ref[...] = jnp.zeros_like(dgg_ref)

        gm = gm_ref[...]
        for c0 in range(0, d, LANES):
            y = att_ref[:, c0:c0 + LANES] if c0 < a_w else cv_ref[:, c0 - a_w:c0 - a_w + LANES]
            dv = dyn_ref[:, c0:c0 + LANES]
            r = lax.rsqrt(_group_sum(y * y, gm) * (1.0 / HEAD_DIM) + EPS)
            xhat = y * r
            dgg_ref[:, c0:c0 + LANES] += jnp.sum(dv * xhat, axis=0, keepdims=True)
            dxh = dv * gg_ref[:, c0:c0 + LANES]
            proj = _group_sum(dxh * xhat, gm) * (1.0 / HEAD_DIM)
            dy = r * (dxh - xhat * proj)
            if c0 < a_w:
                datt_ref[:, c0:c0 + LANES] = dy.astype(BF16)
            else:
                dcv_ref[:, c0 - a_w:c0 - a_w + LANES] = dy

    return _pc(body, out_shape=(_sds((s, a_w), BF16), _sds((s, cw), F32), _sds((1, d), F32)), grid=(s // tr,),
               in_specs=[pl.BlockSpec((tr, d), lambda i: (i, 0)), pl.BlockSpec((tr, a_w), lambda i: (i, 0)),
                         pl.BlockSpec((tr, cw), lambda i: (i, 0)), _vec_spec(d),
                         pl.BlockSpec((LANES, LANES), lambda i: (0, 0))],
               out_specs=(pl.BlockSpec((tr, a_w), lambda i: (i, 0)), pl.BlockSpec((tr, cw), lambda i: (i, 0)),
                          _vec_spec(d)),
               compiler_params=_params("arbitrary"), name=name)(dyn, att, cv, gg, gmat)


def _adamw_math(w, g, m, v):
    m_new = ADAM_B1 * m + (1.0 - ADAM_B1) * g
    v_new = ADAM_B2 * v + (1.0 - ADAM_B2) * (g * g)
    m_hat = m_new / (1.0 - ADAM_B1 ** ADAM_STEP)
    v_hat = v_new / (1.0 - ADAM_B2 ** ADAM_STEP)
    delta = -ADAM_LR * (m_hat / (jnp.sqrt(v_hat) + ADAM_EPS) + ADAM_WD * w)
    return delta, m_new, v_new


def _row_tile(r, c):
    return _tile(r, max(8, ((1 << 19) // c) // 8 * 8), 8)


def _adamw(w, g, m, v, name):
    r, c = w.shape
    tr = _row_tile(r, c)

    def body(w_ref, g_ref, m_ref, v_ref, d_ref, mo_ref, vo_ref):
        d, mn, vn = _adamw_math(w_ref[...], g_ref[...], m_ref[...], v_ref[...])
        d_ref[...] = d
        mo_ref[...] = mn
        vo_ref[...] = vn

    spec = pl.BlockSpec((tr, c), lambda i: (i, 0))
    return _pc(body, out_shape=(_sds((r, c), F32),) * 3, grid=(r // tr,), in_specs=[spec] * 4,
               out_specs=(spec,) * 3, compiler_params=_params("parallel"), name=name)(w, g, m, v)


def _adamw_halves(w, mine, theirs, m, v, core, name):
    r2, c = w.shape
    r = r2 // 2
    assert mine.shape == (r, c) and theirs.shape == (r, c)
    tr = _row_tile(r, c)
    nb = r // tr

    def body(core_ref, w_ref, a_ref, b_ref, m_ref, v_ref, g_ref, d_ref, mo_ref, vo_ref):
        g = jnp.where(pl.program_id(0) == core_ref[0], a_ref[...], b_ref[...])
        d, mn, vn = _adamw_math(w_ref[...], g, m_ref[...], v_ref[...])
        g_ref[...] = g
        d_ref[...] = d
        mo_ref[...] = mn
        vo_ref[...] = vn

    full = pl.BlockSpec((tr, c), lambda h, i, core_ref: (h * nb + i, 0))
    half = pl.BlockSpec((tr, c), lambda h, i, core_ref: (i, 0))
    grid_spec = pltpu.PrefetchScalarGridSpec(
        num_scalar_prefetch=1, grid=(2, nb), in_specs=[full, half, half, full, full], out_specs=(full,) * 4)
    return _pc(body, out_shape=(_sds((r2, c), F32),) * 4, grid_spec=grid_spec,
               compiler_params=_params("parallel", "parallel"), name=name)(core, w, mine, theirs, m, v)


def _ada_fwd(c16, ada_w, ada_b, name):
    d, n = ada_w.shape
    tn = _tile(n, 768, LANES)

    def body(c_ref, w_ref, b_ref, o_ref):
        cv = c_ref[...]
        sc = (cv * jax.nn.sigmoid(cv)).astype(BF16)
        o_ref[...] = lax.dot_general(sc, w_ref[...].astype(BF16), _NN, preferred_element_type=F32) + b_ref[...]

    return _pc(body, out_shape=_sds((16, n), F32), grid=(n // tn,),
               in_specs=[pl.BlockSpec((16, d), lambda j: (0, 0)), pl.BlockSpec((d, tn), lambda j: (0, j)),
                         pl.BlockSpec((1, tn), lambda j: (0, j))],
               out_specs=pl.BlockSpec((16, tn), lambda j: (0, j)),
               compiler_params=_params("parallel"), name=name)(c16, ada_w, ada_b)


def _ada_update(c16_t, dmod16, w, m, v, name, exchange=None):
    r, c = w.shape
    tr = _row_tile(r, c)

    def body(c_ref, dm_ref, w_ref, m_ref, v_ref, g_ref, d_ref, mo_ref, vo_ref):
        cv = c_ref[...]
        sc = (cv * jax.nn.sigmoid(cv)).astype(BF16)
        g = lax.dot_general(sc, dm_ref[...].astype(BF16), _NN, preferred_element_type=F32)
        d, mn, vn = _adamw_math(w_ref[...], g, m_ref[...], v_ref[...])
        g_ref[...] = g
        d_ref[...] = d
        mo_ref[...] = mn
        vo_ref[...] = vn

    spec = pl.BlockSpec((tr, c), lambda i: (i, 0))
    return _pc(body, exchange, out_shape=(_sds((r, c), F32),) * 4, grid=(r // tr,),
               in_specs=[pl.BlockSpec((tr, 16), lambda i: (i, 0)), pl.BlockSpec((16, c), lambda i: (0, 0)),
                         spec, spec, spec],
               out_specs=(spec,) * 4, compiler_params=_params("parallel"), name=name)(c16_t, dmod16, w, m, v)


def _add_half(dw, recv, core, name):
    _, _, r, w = dw.shape
    tr = _tile(r, 512, 16)

    def body(core_ref, a_ref, b_ref, o_ref):
        o_ref[...] = (a_ref[...].astype(F32) + b_ref[...].astype(F32)).astype(BF16)

    grid_spec = pltpu.PrefetchScalarGridSpec(
        num_scalar_prefetch=1, grid=(N_CHIPS, r // tr),
        in_specs=[pl.BlockSpec((None, None, tr, w), lambda s, i, core_ref: (s, core_ref[0], i, 0)),
                  pl.BlockSpec((None, tr, w), lambda s, i, core_ref: (s, i, 0))],
        out_specs=pl.BlockSpec((None, tr, w), lambda s, i, core_ref: (s, i, 0)))
    return _pc(body, out_shape=_sds((N_CHIPS, r, w), BF16), grid_spec=grid_spec,
               compiler_params=_params("parallel", "parallel"), name=name)(core, dw, recv)


def _sum_chips(own, recv, chip, name):
    _, r, w = own.shape
    tr = _tile(r, 512, 16)

    def body(chip_ref, own_ref, p_ref, o_ref):
        acc = own_ref[...].astype(F32)
        for q in range(N_CHIPS - 1):
            acc = acc + p_ref[q].astype(F32)
        o_ref[...] = acc

    grid_spec = pltpu.PrefetchScalarGridSpec(
        num_scalar_prefetch=1, grid=(r // tr,),
        in_specs=[pl.BlockSpec((None, tr, w), lambda i, chip_ref: (chip_ref[0], i, 0)),
                  pl.BlockSpec((N_CHIPS - 1, tr, w), lambda i, chip_ref: (0, i, 0))],
        out_specs=pl.BlockSpec((tr, w), lambda i, chip_ref: (i, 0)))
    return _pc(body, out_shape=_sds((r, w), F32), grid_spec=grid_spec,
               compiler_params=_params("parallel"), name=name)(chip, own, recv)


def _sum_devices(parts, name):
    nd, r, w = parts.shape

    def body(p_ref, o_ref):
        acc = p_ref[0]
        for q in range(1, nd):
            acc = acc + p_ref[q]
        o_ref[...] = acc

    return _pc(body, out_shape=_sds((r, w), F32), name=name)(parts)


def _place():
    x, y, c = lax.axis_index("x"), lax.axis_index("y"), lax.axis_index("c")
    chips = [(1 - x, y), (x, 1 - y), (1 - x, 1 - y)]
    return x, y, c, chips


def _small_gather_exchange(blk):
    r, w = blk.shape

    def copies(src, dst, send_sems, recv_sems):
        x, y, c, chips = _place()
        me, sibling = (x, y, c), (x, y, 1 - c)

        def rows(px, py, pc):
            return dst[0].at[pl.ds((4 * px + 2 * py + pc) * r, r), :]

        def copy(k, block, to, own=False):
            return _remote(src[0] if own else rows(*block), rows(*block), send_sems, recv_sems, k, to)

        mine = pltpu.make_async_copy(src[0], rows(*me), send_sems.at[7])
        first = [copy(0, me, sibling, own=True)] + [copy(1 + j, me, (*chip, c), own=True) for j, chip in enumerate(chips)]
        passed = [copy(4 + j, (*chip, c), sibling) for j, chip in enumerate(chips)]
        landed = [copy(1 + j, (*chip, c), me) for j, chip in enumerate(chips)]
        from_sibling = [copy(0, sibling, me)] + [copy(4 + j, (*chip, 1 - c), me) for j, chip in enumerate(chips)]
        return mine, first, passed, landed, from_sibling

    def start(src, dst, send_sems, recv_sems):
        mine, first, _, _, _ = copies(src, dst, send_sems, recv_sems)
        mine.start()
        for cp in first:
            cp.start()

    def finish(src, dst, send_sems, recv_sems):
        mine, first, passed, landed, from_sibling = copies(src, dst, send_sems, recv_sems)
        for arrival, onward in zip(landed, passed):
            arrival.wait_recv()
            onward.start()
        for cp in from_sibling:
            cp.wait_recv()
        for cp in first + passed:
            cp.wait_send()
        mine.wait()

    return _Exchange([blk], [_sds((N_DEV * r, w), blk.dtype)], 8, start, finish)


def _remote(src, dst, send_sems, recv_sems, k, to):
    return pltpu.make_async_remote_copy(src_ref=src, dst_ref=dst, send_sem=send_sems.at[k], recv_sem=recv_sems.at[k],
                                        device_id=to, device_id_type=MESH)


def _exchange_of(inputs, out_shapes, n_sems, copies, aliases=None):
    def start(src, dst, send_sems, recv_sems):
        for cp in copies(src, dst, send_sems, recv_sems)[0]:
            cp.start()

    def finish(src, dst, send_sems, recv_sems):
        sends, arrivals = copies(src, dst, send_sems, recv_sems)
        for cp in arrivals:
            cp.wait_recv()
        for cp in sends:
            cp.wait_send()

    return _Exchange(inputs, out_shapes, n_sems, start, finish, aliases)


def _run_exchange(ex, name):
    n_in, n_out = len(ex.inputs), len(ex.out_shapes)

    def body(*refs):
        src, dst = refs[:n_in], refs[n_in:n_in + n_out]
        send_sems, recv_sems = refs[n_in + n_out:]
        ex.start(src, dst, send_sems, recv_sems)
        ex.finish(src, dst, send_sems, recv_sems)

    ex.set_results(pl.pallas_call(
        body, out_shape=tuple(ex.out_shapes), in_specs=[_ANY] * n_in, out_specs=(_ANY,) * n_out,
        scratch_shapes=[pltpu.SemaphoreType.DMA((ex.n_sems,)), pltpu.SemaphoreType.DMA((ex.n_sems,))],
        input_output_aliases=ex.aliases, name=name)(*ex.inputs))


def _gather_ici_exchange(shards):
    n = len(shards)

    def copies(own, out, send_sems, recv_sems):
        x, y, c, chips = _place()
        my_chip = 2 * x + y
        sends, arrivals = [], []
        for i in range(n):
            for j, chip in enumerate(chips):
                to = (*chip, c)
                sends.append(_remote(own[i].at[c], out[i].at[my_chip, c], send_sems, recv_sems, 4 * i + j, to))
                arrivals.append(_remote(own[i].at[c], out[i].at[2 * chip[0] + chip[1], c], send_sems, recv_sems, 4 * i + j, to))
            whole = _remote(own[i], out[i].at[my_chip], send_sems, recv_sems, 4 * i + 3, (x, y, 1 - c))
            sends.append(whole)
            arrivals.append(whole)
        return sends, arrivals

    return _exchange_of(shards, [_sds((N_CHIPS,) + s.shape, s.dtype) for s in shards], 4 * n, copies)


def _gather_pass_exchange(gathered):
    n = len(gathered)

    def copies(src, dst, send_sems, recv_sems):
        x, y, c, chips = _place()
        sends, arrivals = [], []
        for i in range(n):
            for j, chip in enumerate(chips):
                idx = 2 * chip[0] + chip[1]
                sends.append(_remote(src[i].at[idx, c], dst[i].at[idx, c], send_sems, recv_sems, 3 * i + j, (x, y, 1 - c)))
                arrivals.append(_remote(src[i].at[idx, c], dst[i].at[idx, 1 - c], send_sems, recv_sems, 3 * i + j, (x, y, 1 - c)))
        return sends, arrivals

    return _exchange_of(gathered, [_sds(g.shape, g.dtype) for g in gathered], 3 * n, copies,
                        aliases={i: i for i in range(n)})


def _reduce_sibling_exchange(grads):
    n = len(grads)

    def copies(src, dst, send_sems, recv_sems):
        x, y, c, _ = _place()
        both = [_remote(src[i].at[s, 1 - c], dst[i].at[s], send_sems, recv_sems, N_CHIPS * i + s, (x, y, 1 - c))
                for i in range(n) for s in range(N_CHIPS)]
        return both, both

    return _exchange_of(grads, [_sds((N_CHIPS,) + g.shape[2:], g.dtype) for g in grads], N_CHIPS * n, copies)


def _reduce_chips_exchange(parts):
    n = len(parts)

    def copies(src, dst, send_sems, recv_sems):
        x, y, c, chips = _place()
        both = [_remote(src[i].at[2 * chip[0] + chip[1]], dst[i].at[j], send_sems, recv_sems, 3 * i + j, (*chip, c))
                for i in range(n) for j, chip in enumerate(chips)]
        return both, both

    return _exchange_of(parts, [_sds((N_CHIPS - 1,) + p.shape[1:], p.dtype) for p in parts], 3 * n, copies)


def _share_exchange(halves):
    n = len(halves)

    def copies(src, dst, send_sems, recv_sems):
        x, y, c, _ = _place()
        both = [_remote(src[i], dst[i], send_sems, recv_sems, i, (x, y, 1 - c)) for i in range(n)]
        return both, both

    return _exchange_of(halves, [_sds(h.shape, h.dtype) for h in halves], n, copies)


HEAD_ROWS = 16


class _WeightTraffic:
    def __init__(self, shards, core, chip):
        self.shards, self.core, self.chip = shards, core, chip
        self.gather, self.grads, self.reduce, self.chip_sums, self.half_sums, self.shared = {}, {}, {}, {}, {}, {}

    def gather_ici(self, grp):
        self.gather[grp] = _gather_ici_exchange(self.shards[grp])
        return self.gather[grp]

    def gather_pass(self, grp):
        self.gather[grp] = _gather_pass_exchange(self.gather[grp].results)
        return self.gather[grp]

    def weights(self, grp):
        return [g.reshape(-1, g.shape[-1]) for g in self.gather[grp].results]

    def reduce_sibling(self, grp, grads):
        self.grads[grp] = [g.reshape(N_CHIPS, 2, g.shape[0] // (2 * N_CHIPS), g.shape[1]) for g in grads]
        self.reduce[grp] = _reduce_sibling_exchange(self.grads[grp])
        return self.reduce[grp]

    def add_halves(self, grp):
        self.chip_sums[grp] = [_add_half(g, r, self.core, "add_half_%s%d" % (grp, i))
                               for i, (g, r) in enumerate(zip(self.grads[grp], self.reduce[grp].results))]

    def reduce_chips(self, grp):
        self.reduce[grp] = _reduce_chips_exchange(self.chip_sums[grp])
        return self.reduce[grp]

    def sum_chips(self, grp):
        self.half_sums[grp] = [_sum_chips(o, p, self.chip, "sum_chips_%s%d" % (grp, i))
                               for i, (o, p) in enumerate(zip(self.chip_sums[grp], self.reduce[grp].results))]

    def share(self, grp):
        self.shared[grp] = _share_exchange(self.half_sums[grp])
        return self.shared[grp]

    def totals(self, grp):
        return list(zip(self.half_sums[grp], self.shared[grp].results))


def _ffn_fwd(x, norm_g, shift, scale, gate, wg_t, wu_t, wd, tag, next_norm, up_exchange=None, down_exchange=None):
    h = _norm_mod_fwd(x, norm_g, shift, scale, tag + "_norm_fwd")
    a, u, hid = _ffn_up(h, wg_t, wu_t, tag + "_up", exchange=up_exchange)
    wd = wd() if callable(wd) else wd
    x_out, f, h_next = _mm(hid, wd, "nn", F32, tag + "_down", res=x, gate=gate, aux_dtype=BF16, norm=next_norm,
                           exchange=down_exchange() if down_exchange else None)
    return x_out, (h, a, u, hid, f), h_next


def _ffn_bwd(dx_out, df, x, saved, norm_g, scale, wg_t, wu_t, wd, tag, traffic, below=None, dact_exchange=None,
             dw_exchange=None, finish_reduction=False):
    h, a, u, hid, _ = saved
    f_below, gate_below = below if below else (None, None)
    da, du = _ffn_dact(df, wd, a, u, tag + "_dact", exchange=dact_exchange)
    dwd = _mm(hid, df, "tn", BF16, tag + "_dwd", exchange=dw_exchange() if dw_exchange else None)
    if not finish_reduction:
        dwg_t, dwu_t = _weight_grads([da, du], h, tag + "_dwg_dwu")
        dx, dshift, dscale, dnorm_g, *gated = _norm_mod_bwd(
            ([da, du], [wg_t, wu_t]), x, norm_g, scale, dx_out, tag + "_dh_norm_bwd", f=f_below, gate=gate_below,
            exchange=traffic.reduce_sibling(tag, [dwg_t, dwu_t, dwd]))
        traffic.add_halves(tag)
        return dx, (dshift, dscale, dnorm_g), gated
    kd, kg, ku = tag + "_wd", tag + "_wg", tag + "_wu"
    dwg_t = _mm(da, h, "tn", BF16, tag + "_dwg", exchange=traffic.reduce_sibling(kd, [dwd]))
    traffic.add_halves(kd)
    dwu_t = _mm(du, h, "tn", BF16, tag + "_dwu",
                exchange=_join(traffic.reduce_chips(kd), traffic.reduce_sibling(kg, [dwg_t])))
    traffic.add_halves(kg)
    half = x.shape[0] // 2
    top = _norm_mod_bwd(([da, du], [wg_t, wu_t]), x, norm_g, scale, dx_out, tag + "_dh_norm_bwd_top", f=f_below,
                        gate=gate_below, rows=(0, half),
                        exchange=_join(traffic.reduce_chips(kg), traffic.reduce_sibling(ku, [dwu_t])))
    traffic.add_halves(ku)
    traffic.sum_chips(kd)
    traffic.sum_chips(kg)
    bottom = _norm_mod_bwd(([da, du], [wg_t, wu_t]), x, norm_g, scale, dx_out, tag + "_dh_norm_bwd_bottom", f=f_below,
                           gate=gate_below, rows=(half, half),
                           exchange=_join(traffic.reduce_chips(ku), traffic.share(kd), traffic.share(kg)))
    traffic.sum_chips(ku)
    dx, dshift, dscale, dnorm_g, *gated = [jnp.concatenate([a, b]) if a.shape[0] == half else a + b
                                           for a, b in zip(top, bottom)]
    return dx, (dshift, dscale, dnorm_g), gated


def _layer_step(x, target, mod, gains, forget_bias, conv_w, traffic, att_w, in_shard, in_rows):
    sh1, sc1, g1, sh2, sc2, g2, sh3, sc3, g3 = mod
    norm1_g, norm2_g, norm3_g, final_g, group_g = gains
    s, d = x.shape
    n_heads = att_w // HEAD_DIM
    npair = n_heads // 2
    gate1, gate3 = 0.5 * g1, 0.5 * g3

    def split_w_in(w_in_pad):
        w_in_t = w_in_pad.reshape(N_CHIPS, in_rows, d)[:, :in_shard].reshape(N_CHIPS * in_shard, d)
        return (w_in_t[:3 * att_w], _pad_rows(w_in_t[3 * att_w:3 * att_w + n_heads], LANES), w_in_t[3 * att_w + n_heads:])

    wg1_t, wu1_t = traffic.weights("ffn1_gu")

    def wd1_ready():
        _run_exchange(traffic.gather_pass("ffn1_d"), "gather_ffn1_down_pass")
        return traffic.weights("ffn1_d")[0]

    x1, saved1, h2 = _ffn_fwd(x, norm1_g, sh1, sc1, gate1, wg1_t, wu1_t, wd1_ready, "ffn1", (norm2_g, sh2, sc2),
                              up_exchange=_join(traffic.gather_ici("ffn1_d"), traffic.gather_ici("mix_in")),
                              down_exchange=lambda: _join(traffic.gather_pass("mix_in"), traffic.gather_ici("mix_out")))
    wd1 = traffic.weights("ffn1_d")[0]
    wqkv_t, wf_t, wbcx_t = split_w_in(traffic.weights("mix_in")[0])

    qkv, bcx, flog = _project(h2, [wqkv_t, wbcx_t, wf_t], [BF16, F32, F32], "mix_proj",
                              exchange=traffic.gather_pass("mix_out"))
    w_out = traffic.weights("mix_out")[0]
    flog_t = jnp.pad(flog[:, :n_heads].T, ((0, HEAD_ROWS - n_heads), (0, 0)))
    bias_col = jnp.pad(forget_bias, (0, HEAD_ROWS - n_heads))[:, None]
    f_pieces = _forget_fwd(flog_t, bias_col, "forget_fwd")
    qa, ka, va = _attn_prep(qkv, f_pieces, "attn_prep")
    att, lse = _attn_fwd(qa, ka, va, "attn_fwd", exchange=traffic.gather_ici("ffn2"))
    cv = _conv_fwd(bcx, conv_w, "conv_fwd")
    yn = _gnorm_fwd(att, cv, group_g, "gnorm_fwd")
    x2, mix, h3 = _mm(yn, w_out, "nn", F32, "mix_out", res=x1, gate=g2, aux_dtype=BF16, norm=(norm3_g, sh3, sc3),
                      exchange=traffic.gather_pass("ffn2"))
    wg2_t, wu2_t, wd2 = traffic.weights("ffn2")

    a3, u3, hid3 = _ffn_up(h3, wg2_t, wu2_t, "ffn2_up")
    saved3 = (h3, a3, u3, hid3, None)
    dx3, loss_row, dfinal_g, df2, dgate3 = _down_final_loss(hid3, wd2, x2, gate3, final_g, target, "ffn2_down_loss")

    dx2, (dsh3, dsc3, dnorm3_g), (dmix, dg2) = _ffn_bwd(
        dx3, df2, x2, saved3, norm3_g, sc3, wg2_t, wu2_t, wd2, "ffn2", traffic, below=(mix, g2))
    dyn = _mm(dmix, w_out, "nt", F32, "mix_out_dyn")
    dw_out = _mm(yn, dmix, "tn", BF16, "mix_out_dw")
    datt, dcv, dgroup_g = _gnorm_bwd(dyn, att, cv, group_g, "gnorm_bwd")
    db, dc, dxc, dconv_w = _conv_bwd(dcv, bcx, conv_w, "conv_bwd")
    dbcx = jnp.concatenate([db, dc, dxc], axis=1)
    dq, dk, dv, qx, kx = _attn_bwd(qa, ka, va, datt, att, lse, "attn_bwd", exchange=traffic.reduce_chips("ffn2"))
    traffic.sum_chips("ffn2")
    dqkv = jnp.concatenate([dq.astype(BF16), dk, dv], axis=1)
    df_t = _decay_grads(qx, kx, "decay_grads")[:, :HEAD_ROWS].T
    dflog_t, dbias_col = _forget_bwd(df_t, flog_t, bias_col, "forget_bwd")
    dflog = jnp.pad(dflog_t[:n_heads].T, ((0, 0), (0, LANES - n_heads))).astype(BF16)
    dwqkv_t, dwbcx_t = _weight_grads([dqkv, dbcx], h2, "mix_dw", exchange=traffic.share("ffn2"))
    dwf_t = _mm(dflog, h2, "tn", BF16, "mix_dw_f")
    dw_in_t = jnp.concatenate([dwqkv_t, dwf_t[:n_heads], dwbcx_t], axis=0).reshape(N_CHIPS, in_shard, d)
    dw_in_t = jnp.pad(dw_in_t, ((0, 0), (0, in_rows - in_shard), (0, 0))).reshape(N_CHIPS * in_rows, d)
    dx1, dsh2, dsc2, dnorm2_g, df1, dgate1 = _norm_mod_bwd(
        ([dqkv, dbcx, dflog], [wqkv_t, wbcx_t, wf_t]), x1, norm2_g, sc2, dx2, "mix_dh_norm_bwd", f=saved1[4], gate=gate1,
        exchange=traffic.reduce_sibling("mix", [dw_in_t, dw_out]))
    traffic.add_halves("mix")

    def share_mix():
        traffic.sum_chips("mix")
        return traffic.share("mix")

    dx, (dsh1, dsc1, dnorm1_g), _ = _ffn_bwd(
        dx1, df1, x, saved1, norm1_g, sc1, wg1_t, wu1_t, wd1, "ffn1", traffic,
        dact_exchange=traffic.reduce_chips("mix"), dw_exchange=share_mix, finish_reduction=True)

    dmod = [dsh1, dsc1, 0.5 * dgate1, dsh2, dsc2, dg2, dsh3, dsc3, 0.5 * dgate3]
    dgains = [dnorm1_g, dnorm2_g, dnorm3_g, dfinal_g, dgroup_g]
    dbias = dbias_col[:n_heads, 0]
    return dx, loss_row, dmod, dgains, dbias, dconv_w


SMALL_ROWS = 24
ROW_GAINS, ROW_LOSS, ROW_FORGET, ROW_CONV, ROW_MOD = 0, 5, 6, 7, 10
PROW_ADA_B, PROW_GAINS, PROW_FORGET, PROW_CONV = 0, 9, 14, 15


def _round_up(n, m):
    return -(-n // m) * m


def _pad_rows(a, rows):
    return jnp.pad(a, ((0, rows - a.shape[0]), (0, 0)))


def _halves(a):
    return a.reshape(2, a.shape[0] // 2, a.shape[1])


def _rows_at(a, r0, total, width):
    return jnp.pad(a, ((r0, total - r0 - a.shape[0]), (0, width - a.shape[1])))


def kernel(x, c, ada_w, ada_b, norm1_g, ffn1_w_gate, ffn1_w_up, ffn1_w_down, norm2_g, w_in, forget_bias, conv_w, group_norm_g, w_out, norm3_g, ffn2_w_gate, ffn2_w_up, ffn2_w_down, final_g, loss_target, m_ada_w, m_ada_b, m_norm1_g, m_ffn1_w_gate, m_ffn1_w_up, m_ffn1_w_down, m_norm2_g, m_w_in, m_forget_bias, m_conv_w, m_group_norm_g, m_w_out, m_norm3_g, m_ffn2_w_gate, m_ffn2_w_up, m_ffn2_w_down, m_final_g, v_ada_w, v_ada_b, v_norm1_g, v_ffn1_w_gate, v_ffn1_w_up, v_ffn1_w_down, v_norm2_g, v_w_in, v_forget_bias, v_conv_w, v_group_norm_g, v_w_out, v_norm3_g, v_ffn2_w_gate, v_ffn2_w_up, v_ffn2_w_down, v_final_g):
    xi, yi, ci = lax.axis_index("x"), lax.axis_index("y"), lax.axis_index("c")
    chip = 2 * xi + yi
    dev = 4 * xi + 2 * yi + ci
    _, s, d = x.shape
    att_w = d // 2
    conv_width = d - att_w
    n_heads = att_w // HEAD_DIM
    in_shard = w_in.shape[1]
    in_rows = _round_up(in_shard, 32)
    cs = conv_w.shape[1]
    mod_shard = ada_w.shape[1]
    assert N_MOD * d == N_CHIPS * mod_shard and conv_width == N_CHIPS * cs and n_heads % 2 == 0

    def t_bf(w):
        return w.T.astype(BF16)

    shards = {"ffn1_gu": [_halves(t_bf(ffn1_w_gate)), _halves(t_bf(ffn1_w_up))], "ffn1_d": [_halves(ffn1_w_down.astype(BF16))],
              "mix_in": [_halves(_pad_rows(t_bf(w_in), in_rows))], "mix_out": [_halves(w_out.astype(BF16))],
              "ffn2": [_halves(t_bf(ffn2_w_gate)), _halves(t_bf(ffn2_w_up)), _halves(ffn2_w_down.astype(BF16))]}
    core = ci.astype(jnp.int32).reshape(1)
    chip_arr = chip.astype(jnp.int32).reshape(1)
    traffic = _WeightTraffic(shards, core, chip_arr)

    cond = _small_gather_exchange(_rows_at(c, 0, 8, d) + _rows_at(conv_w, 1, 8, d))
    _run_exchange(_join(traffic.gather_ici("ffn1_gu"), cond), "gather_ffn1_ici")
    got0 = cond.results[0].reshape(N_DEV, 8, d)
    c16 = _pad_rows(got0[:, 0, :], 16)
    conv_full = got0[0::2, 1:1 + CONV_K, :cs].transpose(1, 0, 2).reshape(CONV_K, conv_width)

    ada_b_mine = lax.dynamic_slice(ada_b, (chip * mod_shard,), (mod_shard,))[None, :]
    mods = _small_gather_exchange(_ada_fwd(c16, ada_w, ada_b_mine, "ada_fwd"))
    _run_exchange(_join(traffic.gather_pass("ffn1_gu"), mods), "gather_ffn1_pass")
    got1 = mods.results[0].reshape(N_DEV, 16, mod_shard)
    mod_mine = lax.dynamic_index_in_dim(got1[0::2], dev, axis=1, keepdims=False).reshape(N_MOD, d)
    mod = [mod_mine[i:i + 1] for i in range(N_MOD)]

    gains = [g[None, :] for g in (norm1_g, norm2_g, norm3_g, final_g, group_norm_g)]
    dx, loss_row, dmod, dgains, dbias, dconv_w = _layer_step(
        x[0], loss_target[0], mod, gains, forget_bias, conv_full, traffic, att_w, in_shard, in_rows)

    pack = sum(_rows_at(g, ROW_GAINS + i, SMALL_ROWS, d) for i, g in enumerate(dgains))
    pack += _rows_at(loss_row, ROW_LOSS, SMALL_ROWS, d) + _rows_at(dbias[None, :], ROW_FORGET, SMALL_ROWS, d)
    pack += _rows_at(dconv_w, ROW_CONV, SMALL_ROWS, d)
    pack += sum(_rows_at(g, ROW_MOD + i, SMALL_ROWS, d) for i, g in enumerate(dmod))
    small = _small_gather_exchange(pack)
    _run_exchange(_join(traffic.share("ffn1_wu"), small), "gather_small_grads")
    got2 = small.results[0].reshape(N_DEV, SMALL_ROWS, d)
    tot = _sum_devices(got2, "sum_small_grads")
    loss = tot[ROW_LOSS, 0]
    grad_ada_b = tot[ROW_MOD:ROW_MOD + N_MOD].reshape(N_MOD * d)
    grad_conv = lax.dynamic_slice(tot[ROW_CONV:ROW_CONV + CONV_K], (0, chip * cs), (CONV_K, cs))
    dmod_all = got2[:, ROW_MOD:ROW_MOD + N_MOD, :].reshape(N_DEV, N_MOD * d)
    dmod16 = _pad_rows(lax.dynamic_slice(dmod_all, (0, chip * mod_shard), (N_DEV, mod_shard)), 16)

    out = {"ada_w": tuple(_ada_update(c16.T, dmod16, ada_w, m_ada_w, v_ada_w, "adamw_ada_w"))}
    totals = (traffic.totals("ffn1_wg") + traffic.totals("ffn1_wu") + traffic.totals("ffn1_wd")
              + traffic.totals("mix") + traffic.totals("ffn2"))

    names = ("ffn1_w_gate", "ffn1_w_up", "ffn1_w_down", "w_in", "w_out", "ffn2_w_gate", "ffn2_w_up", "ffn2_w_down")
    transposed = ("ffn1_w_gate", "ffn1_w_up", "w_in", "ffn2_w_gate", "ffn2_w_up")
    params = {"ffn1_w_gate": (ffn1_w_gate, m_ffn1_w_gate, v_ffn1_w_gate), "ffn1_w_up": (ffn1_w_up, m_ffn1_w_up, v_ffn1_w_up),
              "ffn1_w_down": (ffn1_w_down, m_ffn1_w_down, v_ffn1_w_down), "w_in": (w_in, m_w_in, v_w_in),
              "w_out": (w_out, m_w_out, v_w_out), "ffn2_w_gate": (ffn2_w_gate, m_ffn2_w_gate, v_ffn2_w_gate),
              "ffn2_w_up": (ffn2_w_up, m_ffn2_w_up, v_ffn2_w_up), "ffn2_w_down": (ffn2_w_down, m_ffn2_w_down, v_ffn2_w_down)}
    for name_, (mine, theirs) in zip(names, totals):
        w, m, v = params[name_]
        if name_ in transposed:
            w, m, v = w.T, m.T, v.T
        if name_ == "w_in":
            both = jnp.where(ci == 0, jnp.concatenate([mine, theirs]), jnp.concatenate([theirs, mine]))[:in_shard]
            res = (both,) + tuple(_adamw(w, both, m, v, "adamw_" + name_))
        else:
            res = _adamw_halves(w, mine, theirs, m, v, core, "adamw_" + name_)
        out[name_] = tuple(r.T for r in res) if name_ in transposed else tuple(res)

    def small_pack(ada_b_, gains_, forget_, conv_):
        p = _rows_at(ada_b_.reshape(N_MOD, d), PROW_ADA_B, SMALL_ROWS, d)
        p += sum(_rows_at(g[None, :], PROW_GAINS + i, SMALL_ROWS, d) for i, g in enumerate(gains_))
        p += _rows_at(forget_[None, :], PROW_FORGET, SMALL_ROWS, d) + _rows_at(conv_, PROW_CONV, SMALL_ROWS, d)
        return p

    g_gains = [tot[ROW_GAINS + i] for i in range(5)]
    g_forget = tot[ROW_FORGET, :n_heads]
    sw = small_pack(ada_b, (norm1_g, norm2_g, norm3_g, final_g, group_norm_g), forget_bias, conv_w)
    sm = small_pack(m_ada_b, (m_norm1_g, m_norm2_g, m_norm3_g, m_final_g, m_group_norm_g), m_forget_bias, m_conv_w)
    sv = small_pack(v_ada_b, (v_norm1_g, v_norm2_g, v_norm3_g, v_final_g, v_group_norm_g), v_forget_bias, v_conv_w)
    sg = small_pack(grad_ada_b, g_gains, g_forget, grad_conv)
    small = (sg,) + tuple(_adamw(sw, sg, sm, sv, "adamw_small"))

    def unpack(p):
        r = {"ada_b": p[PROW_ADA_B:PROW_ADA_B + N_MOD].reshape(N_MOD * d), "forget_bias": p[PROW_FORGET, :n_heads],
             "conv_w": p[PROW_CONV:PROW_CONV + CONV_K, :cs]}
        for i, nm in enumerate(("norm1_g", "norm2_g", "norm3_g", "final_g", "group_norm_g")):
            r[nm] = p[PROW_GAINS + i]
        return r

    small = [unpack(p) for p in small]
    order = ("ada_w", "ada_b", "norm1_g", "ffn1_w_gate", "ffn1_w_up", "ffn1_w_down", "norm2_g", "w_in", "forget_bias",
             "conv_w", "group_norm_g", "w_out", "norm3_g", "ffn2_w_gate", "ffn2_w_up", "ffn2_w_down", "final_g")
    result = [loss, dx[None]]
    for k in range(4):
        result += [out[nm][k] if nm in out else small[k][nm] for nm in order]
    return tuple(result)
```

```python
import functools
import math

import jax
import jax.numpy as jnp
from jax import lax
from jax.experimental import pallas as pl
from jax.experimental.pallas import tpu as pltpu

F32 = jnp.float32
BF16 = jnp.bfloat16

HEAD_DIM = 64
CONV_K = 3
N_MOD = 9
EPS = 1e-6
ADAM_LR = 0.001
ADAM_B1 = 0.9
ADAM_B2 = 0.999
ADAM_EPS = 1e-08
ADAM_WD = 0.01
ADAM_STEP = 10

LANES = 128
N_CHIPS = 4
N_DEV = 8
VMEM_LIMIT_BYTES = 56 * 1024 * 1024
MAX_CONTRACTION = 4096
NEG_BIG = -1e30
MESH = pl.DeviceIdType.MESH

_NT = (((1,), (1,)), ((), ()))
_NN = (((1,), (0,)), ((), ()))
_TN = (((0,), (0,)), ((), ()))


def _params(*sem):
    return pltpu.CompilerParams(dimension_semantics=sem, vmem_limit_bytes=VMEM_LIMIT_BYTES)


class _Exchange:
    def __init__(self, inputs, out_shapes, n_sems, start, finish, aliases=None):
        self.inputs, self.out_shapes, self.n_sems = list(inputs), list(out_shapes), n_sems
        self.start, self.finish, self.aliases = start, finish, dict(aliases or {})
        self.results = None

    def set_results(self, results):
        self.results = list(results)


class _SemaphoreWindow:
    def __init__(self, sems, base):
        self.sems, self.base = sems, base
        self.at = self

    def __getitem__(self, k):
        return self.sems.at[self.base + k]


class _JoinedExchange(_Exchange):
    def __init__(self, parts):
        self.parts = parts
        aliases, i0, o0 = {}, 0, 0
        for p in parts:
            aliases.update({i0 + a: o0 + b for a, b in p.aliases.items()})
            i0, o0 = i0 + len(p.inputs), o0 + len(p.out_shapes)

        def each(method, src, dst, send_sems, recv_sems):
            i0 = o0 = s0 = 0
            for p in parts:
                i1, o1 = i0 + len(p.inputs), o0 + len(p.out_shapes)
                getattr(p, method)(src[i0:i1], dst[o0:o1], _SemaphoreWindow(send_sems, s0), _SemaphoreWindow(recv_sems, s0))
                i0, o0, s0 = i1, o1, s0 + p.n_sems

        super().__init__([a for p in parts for a in p.inputs], [o for p in parts for o in p.out_shapes],
                         sum(p.n_sems for p in parts), functools.partial(each, "start"), functools.partial(each, "finish"),
                         aliases)

    def set_results(self, results):
        o0 = 0
        for p in self.parts:
            p.set_results(results[o0:o0 + len(p.out_shapes)])
            o0 += len(p.out_shapes)


def _join(*parts):
    return parts[0] if len(parts) == 1 else _JoinedExchange(list(parts))


def _pc(body, exchange=None, **kw):
    if exchange is None:
        return pl.pallas_call(body, **kw)
    grid = kw["grid"]
    single = not isinstance(kw["out_shape"], (tuple, list))
    out_shape = [kw["out_shape"]] if single else list(kw["out_shape"])
    out_specs = [kw["out_specs"]] if single else list(kw["out_specs"])
    in_specs = list(kw["in_specs"])
    scratch = list(kw.get("scratch_shapes", ()))
    n_in, n_out, n_scr = len(in_specs), len(out_shape), len(scratch)
    n_xi, n_xo = len(exchange.inputs), len(exchange.out_shapes)

    def wrapped(*refs):
        pos = [n_in, n_in + n_xi, n_in + n_xi + n_out, n_in + n_xi + n_out + n_xo]
        ins, x_in, outs, x_out = refs[:pos[0]], refs[pos[0]:pos[1]], refs[pos[1]:pos[2]], refs[pos[2]:pos[3]]
        scr = refs[pos[3]:pos[3] + n_scr]
        send_sems, recv_sems = refs[pos[3] + n_scr:]
        ids = [pl.program_id(a) for a in range(len(grid))]
        first = functools.reduce(jnp.logical_and, [i == 0 for i in ids])
        last = functools.reduce(jnp.logical_and, [i == g - 1 for i, g in zip(ids, grid)])

        @pl.when(first)
        def _():
            exchange.start(x_in, x_out, send_sems, recv_sems)

        body(*ins, *outs, *scr)

        @pl.when(last)
        def _():
            exchange.finish(x_in, x_out, send_sems, recv_sems)

    call = pl.pallas_call(
        wrapped, out_shape=tuple(out_shape) + tuple(exchange.out_shapes), grid=grid,
        in_specs=in_specs + [_ANY] * n_xi, out_specs=tuple(out_specs) + (_ANY,) * n_xo,
        scratch_shapes=scratch + [pltpu.SemaphoreType.DMA((exchange.n_sems,)), pltpu.SemaphoreType.DMA((exchange.n_sems,))],
        input_output_aliases={n_in + a: n_out + b for a, b in exchange.aliases.items()},
        compiler_params=_params(*(["arbitrary"] * len(grid))), name=kw["name"])

    def run(*args):
        res = call(*args, *exchange.inputs)
        exchange.set_results(res[n_out:])
        return res[0] if single else tuple(res[:n_out])

    return run


_ANY = pl.BlockSpec(memory_space=pl.ANY)


def _tile(n, pref, mult):
    best = None
    t = mult
    while t <= min(n, pref):
        if n % t == 0:
            best = t
        t += mult
    return n if best is None else best


def _sds(shape, dtype):
    return jax.ShapeDtypeStruct(shape, dtype)


def _vec_spec(d):
    return pl.BlockSpec((1, d), lambda *_: (0, 0))


def _norm_mod_fwd(x, g, shift, scale, name):
    s, d = x.shape
    tr = _tile(s, 512, 16)

    def body(x_ref, g_ref, sh_ref, sc_ref, h_ref):
        xv = x_ref[...]
        rstd = lax.rsqrt(jnp.mean(xv * xv, axis=-1, keepdims=True) + EPS)
        n = xv * rstd * g_ref[...]
        h_ref[...] = (n * (1.0 + sc_ref[...]) + sh_ref[...]).astype(BF16)

    row = pl.BlockSpec((tr, d), lambda i: (i, 0))
    return _pc(body, out_shape=_sds((s, d), BF16), grid=(s // tr,),
               in_specs=[row, _vec_spec(d), _vec_spec(d), _vec_spec(d)], out_specs=row,
               compiler_params=_params("parallel"), name=name)(x, g, shift, scale)


def _through_gate(dx, f_ref, gate_ref, df_ref, dgate_ref):
    df_ref[...] = (dx * gate_ref[...]).astype(BF16)
    dgate_ref[...] += jnp.sum(dx * f_ref[...].astype(F32), axis=0, keepdims=True)


def _norm_mod_bwd(dh, x, g, scale, dres, name, f=None, gate=None, rows=None, exchange=None):
    d = x.shape[1]
    first_row, s = rows if rows else (0, x.shape[0])
    gated = f is not None
    terms = list(zip(*dh)) if isinstance(dh, tuple) else None
    tr = _tile(s, 256, 16)
    b0 = first_row // tr
    assert first_row % tr == 0
    n_lead = 2 * len(terms) if terms else 1

    def body(*refs):
        lead, (x_ref, g_ref, sc_ref, dres_ref), rest = refs[:n_lead], refs[n_lead:n_lead + 4], refs[n_lead + 4:]
        f_ref, gate_ref = rest[:2] if gated else (None, None)
        dx_ref, dsh_ref, dsc_ref, dg_ref = rest[2:6] if gated else rest[:4]
        df_ref, dgate_ref = rest[6:8] if gated else (None, None)

        @pl.when(pl.program_id(0) == 0)
        def _():
            for ref in (dsh_ref, dsc_ref, dg_ref) + ((dgate_ref,) if gated else ()):
                ref[...] = jnp.zeros_like(ref)

        if terms:
            dhv = lax.dot_general(lead[0][...], lead[1][...], _NN, preferred_element_type=F32)
            for p in range(1, len(terms)):
                dhv += lax.dot_general(lead[2 * p][...], lead[2 * p + 1][...], _NN, preferred_element_type=F32)
        else:
            dhv = lead[0][...]
        xv = x_ref[...]
        gv = g_ref[...]
        rstd = lax.rsqrt(jnp.mean(xv * xv, axis=-1, keepdims=True) + EPS)
        xhat = xv * rstd
        dn = dhv * (1.0 + sc_ref[...])
        dsh_ref[...] += jnp.sum(dhv, axis=0, keepdims=True)
        dsc_ref[...] += jnp.sum(dhv * (xhat * gv), axis=0, keepdims=True)
        dg_ref[...] += jnp.sum(dn * xhat, axis=0, keepdims=True)
        dxh = dn * gv
        proj = jnp.mean(dxh * xhat, axis=-1, keepdims=True)
        dx = dres_ref[...] + rstd * (dxh - xhat * proj)
        dx_ref[...] = dx
        if gated:
            _through_gate(dx, f_ref, gate_ref, df_ref, dgate_ref)

    row = pl.BlockSpec((tr, d), lambda i: (b0 + i, 0))
    out_row = pl.BlockSpec((tr, d), lambda i: (i, 0))
    vec = _vec_spec(d)
    if terms:
        in_specs, args = [], []
        for l, r in terms:
            assert l.shape[1] == r.shape[0] <= MAX_CONTRACTION and r.shape[1] == d
            in_specs += [pl.BlockSpec((tr, l.shape[1]), lambda i: (b0 + i, 0)), pl.BlockSpec(r.shape, lambda i: (0, 0))]
            args += [l, r]
    else:
        in_specs, args = [row], [dh]
    in_specs += [row, vec, vec, row]
    args += [x, g, scale, dres]
    out_shape = [_sds((s, d), F32), _sds((1, d), F32), _sds((1, d), F32), _sds((1, d), F32)]
    out_specs = [out_row, vec, vec, vec]
    if gated:
        out_shape += [_sds((s, d), BF16), _sds((1, d), F32)]
        out_specs += [out_row, vec]
        in_specs += [row, vec]
        args += [f, gate]
    return _pc(body, exchange, out_shape=tuple(out_shape), grid=(s // tr,), in_specs=in_specs,
               out_specs=tuple(out_specs), compiler_params=_params("arbitrary"), name=name)(*args)


def _down_final_loss(hid, wd, res, gate, g, target, name):
    s, d = res.shape
    k = hid.shape[1]
    assert k <= MAX_CONTRACTION
    tr = _tile(s, 256, 16)
    nsteps = s // tr

    def body(hid_ref, wd_ref, res_ref, gate_ref, g_ref, t_ref, dx_ref, loss_ref, dg_ref, df_ref, dgate_ref):
        i = pl.program_id(0)

        @pl.when(i == 0)
        def _():
            loss_ref[...] = jnp.zeros_like(loss_ref)
            dg_ref[...] = jnp.zeros_like(dg_ref)
            dgate_ref[...] = jnp.zeros_like(dgate_ref)

        f = lax.dot_general(hid_ref[...], wd_ref[...], _NN, preferred_element_type=F32)
        gatev = gate_ref[...]
        xv = res_ref[...] + gatev * f
        gv = g_ref[...]
        rstd = lax.rsqrt(jnp.mean(xv * xv, axis=-1, keepdims=True) + EPS)
        xhat = xv * rstd
        err = xhat * gv - t_ref[...]
        dy = err * (1.0 / d)
        loss_ref[...] += jnp.sum(0.5 * err * dy, axis=0, keepdims=True)
        dg_ref[...] += jnp.sum(dy * xhat, axis=0, keepdims=True)
        dxh = dy * gv
        proj = jnp.mean(dxh * xhat, axis=-1, keepdims=True)
        dx = rstd * (dxh - xhat * proj)
        dx_ref[...] = dx
        df_ref[...] = (dx * gatev).astype(BF16)
        dgate_ref[...] += jnp.sum(dx * f, axis=0, keepdims=True)

        @pl.when(i == nsteps - 1)
        def _():
            loss_ref[...] = jnp.broadcast_to(jnp.sum(loss_ref[...], axis=-1, keepdims=True), loss_ref.shape)

    row = pl.BlockSpec((tr, d), lambda i: (i, 0))
    vec = _vec_spec(d)
    return _pc(body, out_shape=(_sds((s, d), F32), _sds((1, d), F32), _sds((1, d), F32), _sds((s, d), BF16), _sds((1, d), F32)),
               grid=(nsteps,),
               in_specs=[pl.BlockSpec((tr, k), lambda i: (i, 0)), pl.BlockSpec((k, d), lambda i: (0, 0)), row, vec, vec, row],
               out_specs=(row, vec, vec, row, vec),
               compiler_params=_params("arbitrary"), name=name)(hid, wd, res, gate, g, target)


def _mm(lhs, rhs, dims, out_dtype, name, res=None, gate=None, aux_dtype=None, norm=None, exchange=None):
    lhs_list = list(lhs) if isinstance(lhs, (list, tuple)) else [lhs]
    rhs_list = list(rhs) if isinstance(rhs, (list, tuple)) else [rhs]
    n_terms = len(lhs_list)
    assert n_terms == len(rhs_list)
    m = lhs_list[0].shape[1 if dims == "tn" else 0]
    n = rhs_list[0].shape[0 if dims == "nt" else 1]
    tn = _tile(n, 1024, LANES)
    tm = _tile(m, 512, LANES if dims == "tn" else 16)
    dn = {"nn": _NN, "nt": _NT, "tn": _TN}[dims]
    in_specs, args = [], []
    for l, r in zip(lhs_list, rhs_list):
        k = l.shape[0 if dims == "tn" else 1]
        assert k == r.shape[1 if dims == "nt" else 0] and k <= MAX_CONTRACTION, (l.shape, r.shape, dims)
        in_specs.append(pl.BlockSpec((k, tm), lambda i, j: (0, i)) if dims == "tn" else pl.BlockSpec((tm, k), lambda i, j: (i, 0)))
        in_specs.append(pl.BlockSpec((tn, k), lambda i, j: (j, 0)) if dims == "nt" else pl.BlockSpec((k, tn), lambda i, j: (0, j)))
        args += [l, r]
    out_spec = pl.BlockSpec((tm, tn), lambda i, j: (i, j))
    has_res, has_gate, has_aux, has_norm = res is not None, gate is not None, aux_dtype is not None, norm is not None
    assert not has_norm or tn == n

    def body(*refs):
        refs = list(refs)
        pos = 2 * n_terms
        res_ref = gate_ref = aux_ref = None
        if has_res:
            res_ref = refs[pos]; pos += 1
        if has_gate:
            gate_ref = refs[pos]; pos += 1
        if has_norm:
            ng_ref, nsh_ref, nsc_ref = refs[pos:pos + 3]; pos += 3
        out_ref = refs[pos]; pos += 1
        if has_aux:
            aux_ref = refs[pos]; pos += 1
        acc = lax.dot_general(refs[0][...], refs[1][...], dn, preferred_element_type=F32)
        for p in range(1, n_terms):
            acc += lax.dot_general(refs[2 * p][...], refs[2 * p + 1][...], dn, preferred_element_type=F32)
        if has_aux:
            aux_ref[...] = acc.astype(aux_dtype)
        if has_gate:
            acc = acc * gate_ref[...]
        if has_res:
            acc = res_ref[...] + acc
        out_ref[...] = acc.astype(out_dtype)
        if has_norm:
            rstd = lax.rsqrt(jnp.mean(acc * acc, axis=-1, keepdims=True) + EPS)
            refs[pos][...] = (acc * rstd * ng_ref[...] * (1.0 + nsc_ref[...]) + nsh_ref[...]).astype(BF16)

    if has_res:
        in_specs.append(out_spec); args.append(res)
    if has_gate:
        in_specs.append(pl.BlockSpec((1, tn), lambda i, j: (0, j))); args.append(gate)
    if has_norm:
        in_specs += [pl.BlockSpec((1, tn), lambda i, j: (0, j))] * 3
        args += list(norm)
    out_shape = [_sds((m, n), out_dtype)]
    out_specs = [out_spec]
    if has_aux:
        out_shape.append(_sds((m, n), aux_dtype)); out_specs.append(out_spec)
    if has_norm:
        out_shape.append(_sds((m, n), BF16)); out_specs.append(out_spec)
    outs = _pc(body, exchange, out_shape=tuple(out_shape), grid=(m // tm, n // tn), in_specs=in_specs,
               out_specs=tuple(out_specs), compiler_params=_params("parallel", "parallel"), name=name)(*args)
    return outs if len(out_shape) > 1 else outs[0]


def _project(h, weights_t, out_dtypes, name, exchange=None):
    s, d = h.shape
    n = len(weights_t)
    tm = _tile(s, 512, 16)

    def body(*refs):
        hv = refs[0][...]
        for i in range(n):
            acc = lax.dot_general(hv, refs[1 + i][...], _NT, preferred_element_type=F32)
            refs[1 + n + i][...] = acc.astype(out_dtypes[i])

    return _pc(body, exchange, out_shape=tuple(_sds((s, w.shape[0]), dt) for w, dt in zip(weights_t, out_dtypes)),
               grid=(s // tm,),
               in_specs=[pl.BlockSpec((tm, d), lambda i: (i, 0))] + [pl.BlockSpec(w.shape, lambda i: (0, 0)) for w in weights_t],
               out_specs=tuple(pl.BlockSpec((tm, w.shape[0]), lambda i: (i, 0)) for w in weights_t),
               compiler_params=_params("parallel"), name=name)(h, *weights_t)


def _weight_grads(cotangents, h, name, exchange=None):
    s, d = h.shape
    m = cotangents[0].shape[1]
    n = len(cotangents)
    assert s <= MAX_CONTRACTION and all(c.shape == (s, m) for c in cotangents)
    tm = _tile(m, 256, LANES)

    def body(*refs):
        hv = refs[n][...]
        for i in range(n):
            refs[n + 1 + i][...] = lax.dot_general(refs[i][...], hv, _TN, preferred_element_type=F32).astype(BF16)

    return _pc(body, exchange, out_shape=(_sds((m, d), BF16),) * n, grid=(m // tm,),
               in_specs=[pl.BlockSpec((s, tm), lambda i: (0, i))] * n + [pl.BlockSpec((s, d), lambda i: (0, 0))],
               out_specs=(pl.BlockSpec((tm, d), lambda i: (i, 0)),) * n,
               compiler_params=_params("parallel"), name=name)(*cotangents, h)


def _ffn_up(h, wg_t, wu_t, name, exchange=None):
    s, d = h.shape
    f = wg_t.shape[0]
    tm = _tile(s, 512, 16)
    tn = _tile(f, 1408, LANES)

    def body(h_ref, wg_ref, wu_ref, a_ref, u_ref, hid_ref):
        hv = h_ref[...]
        a = lax.dot_general(hv, wg_ref[...], _NT, preferred_element_type=F32)
        u = lax.dot_general(hv, wu_ref[...], _NT, preferred_element_type=F32)
        a_ref[...] = a.astype(BF16)
        u_ref[...] = u.astype(BF16)
        hid_ref[...] = (a * jax.nn.sigmoid(a) * u).astype(BF16)

    hs = pl.BlockSpec((tm, d), lambda i, j: (i, 0))
    ws = pl.BlockSpec((tn, d), lambda i, j: (j, 0))
    os_ = pl.BlockSpec((tm, tn), lambda i, j: (i, j))
    return _pc(body, exchange, out_shape=(_sds((s, f), BF16),) * 3, grid=(s // tm, f // tn),
               in_specs=[hs, ws, ws], out_specs=(os_, os_, os_),
               compiler_params=_params("parallel", "parallel"), name=name)(h, wg_t, wu_t)


def _ffn_dact(df, wd, a, u, name, exchange=None):
    s, d = df.shape
    f = wd.shape[0]
    tm = _tile(s, 512, 16)
    tn = _tile(f, 1408, LANES)

    def body(df_ref, wd_ref, a_ref, u_ref, da_ref, du_ref):
        dhid = lax.dot_general(df_ref[...], wd_ref[...], _NT, preferred_element_type=F32)
        av = a_ref[...].astype(F32)
        uv = u_ref[...].astype(F32)
        sig = jax.nn.sigmoid(av)
        da_ref[...] = (dhid * uv * (sig * (1.0 + av * (1.0 - sig)))).astype(BF16)
        du_ref[...] = (dhid * (av * sig)).astype(BF16)

    ds_ = pl.BlockSpec((tm, d), lambda i, j: (i, 0))
    ws = pl.BlockSpec((tn, d), lambda i, j: (j, 0))
    os_ = pl.BlockSpec((tm, tn), lambda i, j: (i, j))
    return _pc(body, exchange, out_shape=(_sds((s, f), BF16),) * 2, grid=(s // tm, f // tn),
               in_specs=[ds_, ws, os_, os_], out_specs=(os_, os_),
               compiler_params=_params("parallel", "parallel"), name=name)(df, wd, a, u)


def _split3(v):
    hi = v.astype(BF16)
    r1 = v - hi.astype(F32)
    mid = r1.astype(BF16)
    lo = (r1 - mid.astype(F32)).astype(BF16)
    return hi, mid, lo


def _dot3(v, mat):
    hi, mid, lo = _split3(v)
    out = lax.dot_general(hi, mat, _NN, preferred_element_type=F32)
    out += lax.dot_general(mid, mat, _NN, preferred_element_type=F32)
    out += lax.dot_general(lo, mat, _NN, preferred_element_type=F32)
    return out


def _forget_fwd(flog_t, bias, name):
    h, s = flog_t.shape
    blk = _tile(s, 512, LANES)
    tri = (jnp.arange(blk)[:, None] <= jnp.arange(blk)[None, :]).astype(BF16)

    def body(z_ref, b_ref, tri_ref, f_ref, carry):
        @pl.when(pl.program_id(0) == 0)
        def _():
            carry[...] = jnp.zeros_like(carry)

        z = z_ref[...] + b_ref[...]
        e = jnp.exp(-jnp.abs(z))
        w = 1.0 + e
        log1p_e = jnp.where(w == 1.0, e, jnp.log(w) * (e / (w - 1.0)))
        lf = jnp.minimum(z, 0.0) - log1p_e
        out = carry[...] + _dot3(lf, tri_ref[...])
        for j, piece in enumerate(_split3(out)):
            f_ref[j] = piece
        carry[...] = out[:, blk - 1:blk]

    zs = pl.BlockSpec((h, blk), lambda i: (0, i))
    return _pc(body, out_shape=_sds((3, h, s), BF16), grid=(s // blk,),
               in_specs=[zs, pl.BlockSpec((h, 1), lambda i: (0, 0)), pl.BlockSpec((blk, blk), lambda i: (0, 0))],
               out_specs=pl.BlockSpec((3, h, blk), lambda i: (0, 0, i)), scratch_shapes=[pltpu.VMEM((h, 1), F32)],
               compiler_params=_params("arbitrary"), name=name)(flog_t, bias, tri)


def _forget_bwd(df_t, flog_t, bias, name):
    h, s = flog_t.shape
    blk = _tile(s, 512, LANES)
    nb = s // blk
    tri = (jnp.arange(blk)[:, None] >= jnp.arange(blk)[None, :]).astype(BF16)

    def body(df_ref, z_ref, b_ref, tri_ref, dz_ref, db_ref, carry):
        @pl.when(pl.program_id(0) == 0)
        def _():
            carry[...] = jnp.zeros_like(carry)
            db_ref[...] = jnp.zeros_like(db_ref)

        rc = carry[...] + _dot3(df_ref[...], tri_ref[...])
        carry[...] = rc[:, 0:1]
        dz = rc * jax.nn.sigmoid(-(z_ref[...] + b_ref[...]))
        dz_ref[...] = dz
        db_ref[...] += jnp.sum(dz, axis=-1, keepdims=True)

    rev = pl.BlockSpec((h, blk), lambda i: (0, nb - 1 - i))
    col = pl.BlockSpec((h, 1), lambda i: (0, 0))
    return _pc(body, out_shape=(_sds((h, s), F32), _sds((h, 1), F32)), grid=(nb,),
               in_specs=[rev, rev, col, pl.BlockSpec((blk, blk), lambda i: (0, 0))],
               out_specs=(rev, col), scratch_shapes=[pltpu.VMEM((h, 1), F32)],
               compiler_params=_params("arbitrary"), name=name)(df_t, flog_t, bias, tri)


def _attn_tiles(s):
    return _tile(s, 1024, LANES)


def _attn_half(t):
    return t // 2 if t >= 4 * LANES else t


BIAS_ROWS = 16


def _attn_prep(qkv, f_pieces, name):
    s = qkv.shape[0]
    a_w = qkv.shape[1] // 3
    npair = a_w // LANES
    t = _attn_tiles(s)
    scale = 1.0 / math.sqrt(HEAD_DIM)

    six = f_pieces[:, :2 * npair].reshape(3, npair, 2, s).transpose(1, 3, 2, 0).reshape(npair, s, 6)
    feat = jnp.concatenate([six, jnp.ones((npair, s, 1), BF16), jnp.zeros((npair, s, BIAS_ROWS - 7), BF16)], axis=-1)
    place_q = [[0.0] * (2 * LANES) for _ in range(BIAS_ROWS)]
    place_k = [[0.0] * (2 * LANES) for _ in range(BIAS_ROWS)]
    for hh in range(2):
        b0 = hh * LANES + (HEAD_DIM if hh == 0 else 0)
        for j in range(3):
            place_q[3 * hh + j][b0 + j] = 1.0
            place_q[6][b0 + 3 + j] = 1.0
            place_k[6][b0 + j] = 1.0
            place_k[3 * hh + j][b0 + 3 + j] = -1.0
    place_q = jnp.array(place_q, BF16)
    place_k = jnp.array(place_k, BF16)

    def body(q_ref, k_ref, v_ref, f_ref, pq_ref, pk_ref, qa_ref, ka_ref, va_ref):
        lane = lax.broadcasted_iota(jnp.int32, (1, LANES), 1)
        q2 = (q_ref[...].astype(F32) * scale).astype(BF16)
        k2, v2 = k_ref[...], v_ref[...]
        qx = lax.dot_general(f_ref[0], pq_ref[...], _NN, preferred_element_type=F32).astype(BF16)
        kx = lax.dot_general(f_ref[0], pk_ref[...], _NN, preferred_element_type=F32).astype(BF16)
        for hh in range(2):
            real = (lane < HEAD_DIM) if hh == 0 else (lane >= HEAD_DIM)
            cols = slice(hh * LANES, (hh + 1) * LANES)
            qa_ref[:, cols] = jnp.where(real, q2, qx[:, cols])
            ka_ref[:, cols] = jnp.where(real, k2, kx[:, cols])
            va_ref[:, cols] = jnp.where(real, v2, jnp.zeros_like(v2))

    def col(off):
        return pl.BlockSpec((t, LANES), lambda p, i: (i, off + p))

    out = pl.BlockSpec((t, 2 * LANES), lambda p, i: (i, p))
    place = pl.BlockSpec((BIAS_ROWS, 2 * LANES), lambda p, i: (0, 0))
    return _pc(body, out_shape=(_sds((s, 2 * a_w), BF16),) * 3, grid=(npair, s // t),
               in_specs=[col(0), col(npair), col(2 * npair), pl.BlockSpec((1, t, BIAS_ROWS), lambda p, i: (p, i, 0)),
                         place, place],
               out_specs=(out, out, out), compiler_params=_params("parallel", "parallel"), name=name)(
                   qkv, qkv, qkv, feat, place_q, place_k)


def _attn_fwd(qa, ka, va, name, exchange=None):
    s = qa.shape[0]
    a_w = qa.shape[1] // 2
    npair = a_w // LANES
    t = _attn_tiles(s)
    nq = s // t
    half = _attn_half(t)

    def body(q_ref, k_ref, v_ref, o_ref, lse_ref, m_sc, l_sc, acc_sc):
        qi = pl.program_id(1)
        first = lax.broadcasted_iota(jnp.int32, (1, LANES), 1) < HEAD_DIM
        m_sc[...] = jnp.full_like(m_sc, NEG_BIG)
        l_sc[...] = jnp.zeros_like(l_sc)
        acc_sc[...] = jnp.zeros_like(acc_sc)

        def step(q0, k_start, size, diag):
            q_sl = slice(q0, q0 + size)
            k_rows = pl.ds(pl.multiple_of(k_start, size), size)
            m_old = m_sc[q_sl, :]
            keep = None
            if diag:
                keep = (lax.broadcasted_iota(jnp.int32, (size, size), 0) >= lax.broadcasted_iota(jnp.int32, (size, size), 1))
            m_new, rs, pv = [], [], []
            for hh in range(2):
                cols = slice(hh * LANES, (hh + 1) * LANES)
                sc = lax.dot_general(q_ref[q_sl, cols], k_ref[k_rows, cols], _NT, preferred_element_type=F32)
                if diag:
                    sc = jnp.where(keep, sc, NEG_BIG)
                mo = m_old[:, hh * HEAD_DIM:hh * HEAD_DIM + 1]
                mn = jnp.maximum(mo, jnp.max(sc, axis=1, keepdims=True))
                p = jnp.exp(sc - mn)
                m_new.append(mn)
                rs.append(jnp.sum(p, axis=1, keepdims=True))
                pv.append(lax.dot_general(p.astype(BF16), v_ref[k_rows, cols], _NN, preferred_element_type=F32))
            m2 = jnp.where(first, m_new[0], m_new[1])
            alpha = jnp.exp(m_old - m2)
            m_sc[q_sl, :] = m2
            l_sc[q_sl, :] = alpha * l_sc[q_sl, :] + jnp.where(first, rs[0], rs[1])
            acc_sc[q_sl, :] = alpha * acc_sc[q_sl, :] + pv[0] + pv[1]

        def below_diagonal(ki, carry):
            step(0, ki * t, t, False)
            return carry

        lax.fori_loop(0, qi, below_diagonal, 0)
        step(0, qi * t, half, True)
        if half < t:
            step(half, qi * t, half, False)
            step(half, qi * t + half, half, True)
        l2 = l_sc[...]
        o_ref[...] = acc_sc[...] / l2
        lse_ref[...] = m_sc[...] + jnp.log(l2)

    qs = pl.BlockSpec((t, 2 * LANES), lambda p, qi: (qi, p))
    ks = pl.BlockSpec((s, 2 * LANES), lambda p, qi: (0, p))
    os_ = pl.BlockSpec((t, LANES), lambda p, qi: (qi, p))
    return _pc(body, exchange, out_shape=(_sds((s, a_w), F32), _sds((s, a_w), F32)), grid=(npair, nq),
               in_specs=[qs, ks, ks], out_specs=(os_, os_),
               scratch_shapes=[pltpu.VMEM((t, LANES), F32)] * 3,
               compiler_params=_params("parallel", "arbitrary"), name=name)(qa, ka, va)


def _attn_bwd(qa, ka, va, do, o, lse, name, exchange=None):
    s = qa.shape[0]
    a_w = qa.shape[1] // 2
    npair = a_w // LANES
    t = _attn_tiles(s)
    nq = s // t
    half = _attn_half(t)
    scale = 1.0 / math.sqrt(HEAD_DIM)

    def body(q_ref, k_ref, v_ref, do_ref, o_ref, lse_ref, dq_ref, dk_ref, dv_ref, qx_ref, kx_ref, dk_sc, dv_sc, kx_sc):
        ki = pl.program_id(1)
        first = lax.broadcasted_iota(jnp.int32, (1, LANES), 1) < HEAD_DIM

        @pl.when(ki == 0)
        def _():
            dq_ref[...] = jnp.zeros_like(dq_ref)
            qx_ref[...] = jnp.zeros_like(qx_ref)

        def step(q_start, k0, size, diag, assign):
            rows = pl.ds(pl.multiple_of(q_start, size), size)
            k_sl = slice(k0, k0 + size)
            do2 = do_ref[rows, :]
            lse2 = lse_ref[rows, :]
            dd = do2.astype(F32) * o_ref[rows, :]
            keep = None
            if diag:
                keep = (lax.broadcasted_iota(jnp.int32, (size, size), 0) >= lax.broadcasted_iota(jnp.int32, (size, size), 1))
            dq_h, dk_h, dv_h = [], [], []
            for hh in range(2):
                sel = first if hh == 0 else jnp.logical_not(first)
                cols = slice(hh * LANES, (hh + 1) * LANES)
                qh, kh, vh = q_ref[rows, cols], k_ref[k_sl, cols], v_ref[k_sl, cols]
                delta = jnp.sum(jnp.where(sel, dd, 0.0), axis=1, keepdims=True)
                sc = lax.dot_general(qh, kh, _NT, preferred_element_type=F32)
                if diag:
                    sc = jnp.where(keep, sc, NEG_BIG)
                p = jnp.exp(sc - lse2[:, hh * HEAD_DIM:hh * HEAD_DIM + 1])
                dp = lax.dot_general(do2, vh, _NT, preferred_element_type=F32)
                ds_b = (p * (dp - delta)).astype(BF16)
                dv_h.append(lax.dot_general(p.astype(BF16), do2, _TN, preferred_element_type=F32))
                dk_h.append(lax.dot_general(ds_b, qh, _TN, preferred_element_type=F32))
                dq_h.append(lax.dot_general(ds_b, kh, _NN, preferred_element_type=F32))
            dq_ref[rows, :] += jnp.where(first, dq_h[0], dq_h[1]) * scale
            qx_ref[rows, :] += jnp.where(first, dq_h[1], dq_h[0])
            dk_new = jnp.where(first, dk_h[0], dk_h[1])
            kx_new = jnp.where(first, dk_h[1], dk_h[0])
            dv_new = jnp.where(first, dv_h[0], dv_h[1])
            if assign:
                dk_sc[k_sl, :] = dk_new
                kx_sc[k_sl, :] = kx_new
                dv_sc[k_sl, :] = dv_new
            else:
                dk_sc[k_sl, :] += dk_new
                kx_sc[k_sl, :] += kx_new
                dv_sc[k_sl, :] += dv_new

        def below_diagonal(qi, carry):
            step(qi * t, 0, t, False, False)
            return carry

        step(ki * t, 0, half, True, True)
        if half < t:
            step(ki * t + half, 0, half, False, False)
            step(ki * t + half, half, half, True, True)
        lax.fori_loop(ki + 1, nq, below_diagonal, 0)
        dk_ref[...] = dk_sc[...].astype(BF16)
        dv_ref[...] = dv_sc[...].astype(BF16)
        kx_ref[...] = kx_sc[...]

    ks2 = pl.BlockSpec((t, 2 * LANES), lambda p, ki: (ki, p))
    qs2 = pl.BlockSpec((s, 2 * LANES), lambda p, ki: (0, p))
    whole = pl.BlockSpec((s, LANES), lambda p, ki: (0, p))
    kout = pl.BlockSpec((t, LANES), lambda p, ki: (ki, p))
    return _pc(body, exchange,
               out_shape=(_sds((s, a_w), F32), _sds((s, a_w), BF16), _sds((s, a_w), BF16), _sds((s, a_w), F32),
                          _sds((s, a_w), F32)),
               grid=(npair, nq), in_specs=[qs2, ks2, ks2, whole, whole, whole],
               out_specs=(whole, kout, kout, whole, kout),
               scratch_shapes=[pltpu.VMEM((t, LANES), F32)] * 3,
               compiler_params=_params("parallel", "arbitrary"), name=name)(qa, ka, va, do, o, lse)

def _decay_grads(qx, kx, name):
    s, a_w = qx.shape
    n_heads = a_w // HEAD_DIM
    tr = _tile(s, 512, 8)
    pick_q = [[0.0] * LANES for _ in range(a_w)]
    pick_k = [[0.0] * LANES for _ in range(a_w)]
    for h in range(n_heads):
        b0 = (h // 2) * LANES + (HEAD_DIM if h % 2 == 0 else 0)
        pick_q[b0][h] = 1.0
        pick_k[b0 + 3][h] = 1.0
    pick_q = jnp.array(pick_q, BF16)
    pick_k = jnp.array(pick_k, BF16)

    def body(qx_ref, kx_ref, pq_ref, pk_ref, o_ref):
        o_ref[...] = _dot3(qx_ref[...], pq_ref[...]) - _dot3(kx_ref[...], pk_ref[...])

    row = pl.BlockSpec((tr, a_w), lambda i: (i, 0))
    pick = pl.BlockSpec((a_w, LANES), lambda i: (0, 0))
    return _pc(body, out_shape=_sds((s, LANES), F32), grid=(s // tr,), in_specs=[row, row, pick, pick],
               out_specs=pl.BlockSpec((tr, LANES), lambda i: (i, 0)),
               compiler_params=_params("parallel"), name=name)(qx, kx, pick_q, pick_k)


def _shift_down(z, k, rows):
    return jnp.where(rows >= k, pltpu.roll(z, k, 0), 0.0)


def _shift_up(z, k, rows, n):
    return jnp.where(rows < n - k, pltpu.roll(z, n - k, 0), 0.0)


def _conv_fwd(bcx, conv_w, name):
    s = bcx.shape[0]
    cw = bcx.shape[1] // 3
    nb = cw // LANES

    def body(b_ref, c_ref, x_ref, w_ref, cv_ref):
        rows = lax.broadcasted_iota(jnp.int32, (s, LANES), 0)
        z = c_ref[...] * x_ref[...]
        w = w_ref[...]
        y = w[2:3, :] * z + w[1:2, :] * _shift_down(z, 1, rows) + w[0:1, :] * _shift_down(z, 2, rows)
        cv_ref[...] = b_ref[...] * y

    def col(off):
        return pl.BlockSpec((s, LANES), lambda j: (0, j + off))

    return _pc(body, out_shape=_sds((s, cw), F32), grid=(nb,),
               in_specs=[col(0), col(nb), col(2 * nb), pl.BlockSpec((CONV_K, LANES), lambda j: (0, j))],
               out_specs=col(0), compiler_params=_params("parallel"), name=name)(bcx, bcx, bcx, conv_w)


def _conv_bwd(dcv, bcx, conv_w, name):
    s = bcx.shape[0]
    cw = bcx.shape[1] // 3
    nb = cw // LANES

    def body(dcv_ref, b_ref, c_ref, x_ref, w_ref, db_ref, dc_ref, dxc_ref, dw_ref):
        rows = lax.broadcasted_iota(jnp.int32, (s, LANES), 0)
        cv_, xv = c_ref[...], x_ref[...]
        z = cv_ * xv
        w = w_ref[...]
        z1 = _shift_down(z, 1, rows)
        z2 = _shift_down(z, 2, rows)
        y = w[2:3, :] * z + w[1:2, :] * z1 + w[0:1, :] * z2
        dcvv = dcv_ref[...]
        db_ref[...] = (dcvv * y).astype(BF16)
        dy = dcvv * b_ref[...]
        dw_ref[0:1, :] = jnp.sum(dy * z2, axis=0, keepdims=True)
        dw_ref[1:2, :] = jnp.sum(dy * z1, axis=0, keepdims=True)
        dw_ref[2:3, :] = jnp.sum(dy * z, axis=0, keepdims=True)
        dz = w[2:3, :] * dy + w[1:2, :] * _shift_up(dy, 1, rows, s) + w[0:1, :] * _shift_up(dy, 2, rows, s)
        dc_ref[...] = (dz * xv).astype(BF16)
        dxc_ref[...] = (dz * cv_).astype(BF16)

    def col(off):
        return pl.BlockSpec((s, LANES), lambda j: (0, j + off))

    wspec = pl.BlockSpec((CONV_K, LANES), lambda j: (0, j))
    db, dc, dxc, dw = _pc(body, out_shape=(_sds((s, cw), BF16),) * 3 + (_sds((CONV_K, cw), F32),), grid=(nb,),
                          in_specs=[col(0), col(0), col(nb), col(2 * nb), wspec],
                          out_specs=(col(0), col(0), col(0), wspec),
                          compiler_params=_params("parallel"), name=name)(dcv, bcx, bcx, bcx, conv_w)
    return db, dc, dxc, dw


def _group_matrix():
    idx = jnp.arange(LANES) // HEAD_DIM
    return (idx[:, None] == idx[None, :]).astype(BF16)


def _group_sum(v, gmat):
    return _dot3(v, gmat)


def _gnorm_fwd(att, cv, gg, name):
    s, a_w = att.shape
    cw = cv.shape[1]
    d = a_w + cw
    tr = _tile(s, 512, 16)
    gmat = _group_matrix()

    def body(att_ref, cv_ref, gg_ref, gm_ref, yn_ref):
        gm = gm_ref[...]
        for c0 in range(0, d, LANES):
            y = att_ref[:, c0:c0 + LANES] if c0 < a_w else cv_ref[:, c0 - a_w:c0 - a_w + LANES]
            ms = _group_sum(y * y, gm) * (1.0 / HEAD_DIM)
            yn_ref[:, c0:c0 + LANES] = (y * lax.rsqrt(ms + EPS) * gg_ref[:, c0:c0 + LANES]).astype(BF16)

    return _pc(body, out_shape=_sds((s, d), BF16), grid=(s // tr,),
               in_specs=[pl.BlockSpec((tr, a_w), lambda i: (i, 0)), pl.BlockSpec((tr, cw), lambda i: (i, 0)),
                         _vec_spec(d), pl.BlockSpec((LANES, LANES), lambda i: (0, 0))],
               out_specs=pl.BlockSpec((tr, d), lambda i: (i, 0)),
               compiler_params=_params("parallel"), name=name)(att, cv, gg, gmat)


def _gnorm_bwd(dyn, att, cv, gg, name):
    s, a_w = att.shape
    cw = cv.shape[1]
    d = a_w + cw
    tr = _tile(s, 256, 16)
    gmat = _group_matrix()

    def body(dyn_ref, att_ref, cv_ref, gg_ref, gm_ref, datt_ref, dcv_ref, dgg_ref):
        @pl.when(pl.program_id(0) == 0)
        def _():
            dgg_ref[...] = jnp.zeros_like(dgg_ref)

        gm = gm_ref[...]
        for c0 in range(0, d, LANES):
            y = att_ref[:, c0:c0 + LANES] if c0 < a_w else cv_ref[:, c0 - a_w:c0 - a_w + LANES]
            dv = dyn_ref[:, c0:c0 + LANES]
            r = lax.rsqrt(_group_sum(y * y, gm) * (1.0 / HEAD_DIM) + EPS)
            xhat = y * r
            dgg_ref[:, c0:c0 + LANES] += jnp.sum(dv * xhat, axis=0, keepdims=True)
            dxh = dv * gg_ref[:, c0:c0 + LANES]
            proj = _group_sum(dxh * xhat, gm) * (1.0 / HEAD_DIM)
            dy = r * (dxh - xhat * proj)
            if c0 < a_w:
                datt_ref[:, c0:c0 + LANES] = dy.astype(BF16)
            else:
                dcv_ref[:, c0 - a_w:c0 - a_w + LANES] = dy

    return _pc(body, out_shape=(_sds((s, a_w), BF16), _sds((s, cw), F32), _sds((1, d), F32)), grid=(s // tr,),
               in_specs=[pl.BlockSpec((tr, d), lambda i: (i, 0)), pl.BlockSpec((tr, a_w), lambda i: (i, 0)),
                         pl.BlockSpec((tr, cw), lambda i: (i, 0)), _vec_spec(d),
                         pl.BlockSpec((LANES, LANES), lambda i: (0, 0))],
               out_specs=(pl.BlockSpec((tr, a_w), lambda i: (i, 0)), pl.BlockSpec((tr, cw), lambda i: (i, 0)),
                          _vec_spec(d)),
               compiler_params=_params("arbitrary"), name=name)(dyn, att, cv, gg, gmat)


def _adamw_math(w, g, m, v):
    m_new = ADAM_B1 * m + (1.0 - ADAM_B1) * g
    v_new = ADAM_B2 * v + (1.0 - ADAM_B2) * (g * g)
    m_hat = m_new / (1.0 - ADAM_B1 ** ADAM_STEP)
    v_hat = v_new / (1.0 - ADAM_B2 ** ADAM_STEP)
    delta = -ADAM_LR * (m_hat / (jnp.sqrt(v_hat) + ADAM_EPS) + ADAM_WD * w)
    return delta, m_new, v_new


def _row_tile(r, c):
    return _tile(r, max(8, ((1 << 19) // c) // 8 * 8), 8)


def _adamw(w, g, m, v, name):
    r, c = w.shape
    tr = _row_tile(r, c)

    def body(w_ref, g_ref, m_ref, v_ref, d_ref, mo_ref, vo_ref):
        d, mn, vn = _adamw_math(w_ref[...], g_ref[...], m_ref[...], v_ref[...])
        d_ref[...] = d
        mo_ref[...] = mn
        vo_ref[...] = vn

    spec = pl.BlockSpec((tr, c), lambda i: (i, 0))
    return _pc(body, out_shape=(_sds((r, c), F32),) * 3, grid=(r // tr,), in_specs=[spec] * 4,
               out_specs=(spec,) * 3, compiler_params=_params("parallel"), name=name)(w, g, m, v)


def _adamw_halves(w, mine, theirs, m, v, core, name):
    r2, c = w.shape
    r = r2 // 2
    assert mine.shape == (r, c) and theirs.shape == (r, c)
    tr = _row_tile(r, c)
    nb = r // tr

    def body(core_ref, w_ref, a_ref, b_ref, m_ref, v_ref, g_ref, d_ref, mo_ref, vo_ref):
        g = jnp.where(pl.program_id(0) == core_ref[0], a_ref[...], b_ref[...])
        d, mn, vn = _adamw_math(w_ref[...], g, m_ref[...], v_ref[...])
        g_ref[...] = g
        d_ref[...] = d
        mo_ref[...] = mn
        vo_ref[...] = vn

    full = pl.BlockSpec((tr, c), lambda h, i, core_ref: (h * nb + i, 0))
    half = pl.BlockSpec((tr, c), lambda h, i, core_ref: (i, 0))
    grid_spec = pltpu.PrefetchScalarGridSpec(
        num_scalar_prefetch=1, grid=(2, nb), in_specs=[full, half, half, full, full], out_specs=(full,) * 4)
    return _pc(body, out_shape=(_sds((r2, c), F32),) * 4, grid_spec=grid_spec,
               compiler_params=_params("parallel", "parallel"), name=name)(core, w, mine, theirs, m, v)


def _ada_fwd(c16, ada_w, ada_b, name):
    d, n = ada_w.shape
    tn = _tile(n, 768, LANES)

    def body(c_ref, w_ref, b_ref, o_ref):
        cv = c_ref[...]
        sc = (cv * jax.nn.sigmoid(cv)).astype(BF16)
        o_ref[...] = lax.dot_general(sc, w_ref[...].astype(BF16), _NN, preferred_element_type=F32) + b_ref[...]

    return _pc(body, out_shape=_sds((16, n), F32), grid=(n // tn,),
               in_specs=[pl.BlockSpec((16, d), lambda j: (0, 0)), pl.BlockSpec((d, tn), lambda j: (0, j)),
                         pl.BlockSpec((1, tn), lambda j: (0, j))],
               out_specs=pl.BlockSpec((16, tn), lambda j: (0, j)),
               compiler_params=_params("parallel"), name=name)(c16, ada_w, ada_b)


def _ada_update(c16_t, dmod16, w, m, v, name, exchange=None):
    r, c = w.shape
    tr = _row_tile(r, c)

    def body(c_ref, dm_ref, w_ref, m_ref, v_ref, g_ref, d_ref, mo_ref, vo_ref):
        cv = c_ref[...]
        sc = (cv * jax.nn.sigmoid(cv)).astype(BF16)
        g = lax.dot_general(sc, dm_ref[...].astype(BF16), _NN, preferred_element_type=F32)
        d, mn, vn = _adamw_math(w_ref[...], g, m_ref[...], v_ref[...])
        g_ref[...] = g
        d_ref[...] = d
        mo_ref[...] = mn
        vo_ref[...] = vn

    spec = pl.BlockSpec((tr, c), lambda i: (i, 0))
    return _pc(body, exchange, out_shape=(_sds((r, c), F32),) * 4, grid=(r // tr,),
               in_specs=[pl.BlockSpec((tr, 16), lambda i: (i, 0)), pl.BlockSpec((16, c), lambda i: (0, 0)),
                         spec, spec, spec],
               out_specs=(spec,) * 4, compiler_params=_params("parallel"), name=name)(c16_t, dmod16, w, m, v)


def _add_half(dw, recv, core, name):
    _, _, r, w = dw.shape
    tr = _tile(r, 512, 16)

    def body(core_ref, a_ref, b_ref, o_ref):
        o_ref[...] = (a_ref[...].astype(F32) + b_ref[...].astype(F32)).astype(BF16)

    grid_spec = pltpu.PrefetchScalarGridSpec(
        num_scalar_prefetch=1, grid=(N_CHIPS, r // tr),
        in_specs=[pl.BlockSpec((None, None, tr, w), lambda s, i, core_ref: (s, core_ref[0], i, 0)),
                  pl.BlockSpec((None, tr, w), lambda s, i, core_ref: (s, i, 0))],
        out_specs=pl.BlockSpec((None, tr, w), lambda s, i, core_ref: (s, i, 0)))
    return _pc(body, out_shape=_sds((N_CHIPS, r, w), BF16), grid_spec=grid_spec,
               compiler_params=_params("parallel", "parallel"), name=name)(core, dw, recv)


def _sum_chips(own, recv, chip, name):
    _, r, w = own.shape
    tr = _tile(r, 512, 16)

    def body(chip_ref, own_ref, p_ref, o_ref):
        acc = own_ref[...].astype(F32)
        for q in range(N_CHIPS - 1):
            acc = acc + p_ref[q].astype(F32)
        o_ref[...] = acc

    grid_spec = pltpu.PrefetchScalarGridSpec(
        num_scalar_prefetch=1, grid=(r // tr,),
        in_specs=[pl.BlockSpec((None, tr, w), lambda i, chip_ref: (chip_ref[0], i, 0)),
                  pl.BlockSpec((N_CHIPS - 1, tr, w), lambda i, chip_ref: (0, i, 0))],
        out_specs=pl.BlockSpec((tr, w), lambda i, chip_ref: (i, 0)))
    return _pc(body, out_shape=_sds((r, w), F32), grid_spec=grid_spec,
               compiler_params=_params("parallel"), name=name)(chip, own, recv)


def _sum_devices(parts, name):
    nd, r, w = parts.shape

    def body(p_ref, o_ref):
        acc = p_ref[0]
        for q in range(1, nd):
            acc = acc + p_ref[q]
        o_ref[...] = acc

    return _pc(body, out_shape=_sds((r, w), F32), name=name)(parts)


def _place():
    x, y, c = lax.axis_index("x"), lax.axis_index("y"), lax.axis_index("c")
    chips = [(1 - x, y), (x, 1 - y), (1 - x, 1 - y)]
    return x, y, c, chips


def _small_gather_exchange(blk):
    r, w = blk.shape

    def copies(src, dst, send_sems, recv_sems):
        x, y, c, chips = _place()
        me, sibling = (x, y, c), (x, y, 1 - c)

        def rows(px, py, pc):
            return dst[0].at[pl.ds((4 * px + 2 * py + pc) * r, r), :]

        def copy(k, block, to, own=False):
            return _remote(src[0] if own else rows(*block), rows(*block), send_sems, recv_sems, k, to)

        mine = pltpu.make_async_copy(src[0], rows(*me), send_sems.at[7])
        first = [copy(0, me, sibling, own=True)] + [copy(1 + j, me, (*chip, c), own=True) for j, chip in enumerate(chips)]
        passed = [copy(4 + j, (*chip, c), sibling) for j, chip in enumerate(chips)]
        landed = [copy(1 + j, (*chip, c), me) for j, chip in enumerate(chips)]
        from_sibling = [copy(0, sibling, me)] + [copy(4 + j, (*chip, 1 - c), me) for j, chip in enumerate(chips)]
        return mine, first, passed, landed, from_sibling

    def start(src, dst, send_sems, recv_sems):
        mine, first, _, _, _ = copies(src, dst, send_sems, recv_sems)
        mine.start()
        for cp in first:
            cp.start()

    def finish(src, dst, send_sems, recv_sems):
        mine, first, passed, landed, from_sibling = copies(src, dst, send_sems, recv_sems)
        for arrival, onward in zip(landed, passed):
            arrival.wait_recv()
            onward.start()
        for cp in from_sibling:
            cp.wait_recv()
        for cp in first + passed:
            cp.wait_send()
        mine.wait()

    return _Exchange([blk], [_sds((N_DEV * r, w), blk.dtype)], 8, start, finish)


def _remote(src, dst, send_sems, recv_sems, k, to):
    return pltpu.make_async_remote_copy(src_ref=src, dst_ref=dst, send_sem=send_sems.at[k], recv_sem=recv_sems.at[k],
                                        device_id=to, device_id_type=MESH)


def _exchange_of(inputs, out_shapes, n_sems, copies, aliases=None):
    def start(src, dst, send_sems, recv_sems):
        for cp in copies(src, dst, send_sems, recv_sems)[0]:
            cp.start()

    def finish(src, dst, send_sems, recv_sems):
        sends, arrivals = copies(src, dst, send_sems, recv_sems)
        for cp in arrivals:
            cp.wait_recv()
        for cp in sends:
            cp.wait_send()

    return _Exchange(inputs, out_shapes, n_sems, start, finish, aliases)


def _run_exchange(ex, name):
    n_in, n_out = len(ex.inputs), len(ex.out_shapes)

    def body(*refs):
        src, dst = refs[:n_in], refs[n_in:n_in + n_out]
        send_sems, recv_sems = refs[n_in + n_out:]
        ex.start(src, dst, send_sems, recv_sems)
        ex.finish(src, dst, send_sems, recv_sems)

    ex.set_results(pl.pallas_call(
        body, out_shape=tuple(ex.out_shapes), in_specs=[_ANY] * n_in, out_specs=(_ANY,) * n_out,
        scratch_shapes=[pltpu.SemaphoreType.DMA((ex.n_sems,)), pltpu.SemaphoreType.DMA((ex.n_sems,))],
        input_output_aliases=ex.aliases, name=name)(*ex.inputs))


def _gather_ici_exchange(shards):
    n = len(shards)

    def copies(own, out, send_sems, recv_sems):
        x, y, c, chips = _place()
        my_chip = 2 * x + y
        sends, arrivals = [], []
        for i in range(n):
            for j, chip in enumerate(chips):
                to = (*chip, c)
                sends.append(_remote(own[i].at[c], out[i].at[my_chip, c], send_sems, recv_sems, 4 * i + j, to))
                arrivals.append(_remote(own[i].at[c], out[i].at[2 * chip[0] + chip[1], c], send_sems, recv_sems, 4 * i + j, to))
            whole = _remote(own[i], out[i].at[my_chip], send_sems, recv_sems, 4 * i + 3, (x, y, 1 - c))
            sends.append(whole)
            arrivals.append(whole)
        return sends, arrivals

    return _exchange_of(shards, [_sds((N_CHIPS,) + s.shape, s.dtype) for s in shards], 4 * n, copies)


def _gather_pass_exchange(gathered):
    n = len(gathered)

    def copies(src, dst, send_sems, recv_sems):
        x, y, c, chips = _place()
        sends, arrivals = [], []
        for i in range(n):
            for j, chip in enumerate(chips):
                idx = 2 * chip[0] + chip[1]
                sends.append(_remote(src[i].at[idx, c], dst[i].at[idx, c], send_sems, recv_sems, 3 * i + j, (x, y, 1 - c)))
                arrivals.append(_remote(src[i].at[idx, c], dst[i].at[idx, 1 - c], send_sems, recv_sems, 3 * i + j, (x, y, 1 - c)))
        return sends, arrivals

    return _exchange_of(gathered, [_sds(g.shape, g.dtype) for g in gathered], 3 * n, copies,
                        aliases={i: i for i in range(n)})


def _reduce_sibling_exchange(grads):
    n = len(grads)

    def copies(src, dst, send_sems, recv_sems):
        x, y, c, _ = _place()
        both = [_remote(src[i].at[s, 1 - c], dst[i].at[s], send_sems, recv_sems, N_CHIPS * i + s, (x, y, 1 - c))
                for i in range(n) for s in range(N_CHIPS)]
        return both, both

    return _exchange_of(grads, [_sds((N_CHIPS,) + g.shape[2:], g.dtype) for g in grads], N_CHIPS * n, copies)


def _reduce_chips_exchange(parts):
    n = len(parts)

    def copies(src, dst, send_sems, recv_sems):
        x, y, c, chips = _place()
        both = [_remote(src[i].at[2 * chip[0] + chip[1]], dst[i].at[j], send_sems, recv_sems, 3 * i + j, (*chip, c))
                for i in range(n) for j, chip in enumerate(chips)]
        return both, both

    return _exchange_of(parts, [_sds((N_CHIPS - 1,) + p.shape[1:], p.dtype) for p in parts], 3 * n, copies)


def _share_exchange(halves):
    n = len(halves)

    def copies(src, dst, send_sems, recv_sems):
        x, y, c, _ = _place()
        both = [_remote(src[i], dst[i], send_sems, recv_sems, i, (x, y, 1 - c)) for i in range(n)]
        return both, both

    return _exchange_of(halves, [_sds(h.shape, h.dtype) for h in halves], n, copies)


HEAD_ROWS = 16


class _WeightTraffic:
    def __init__(self, shards, core, chip):
        self.shards, self.core, self.chip = shards, core, chip
        self.gather, self.grads, self.reduce, self.chip_sums, self.half_sums, self.shared = {}, {}, {}, {}, {}, {}

    def gather_ici(self, grp):
        self.gather[grp] = _gather_ici_exchange(self.shards[grp])
        return self.gather[grp]

    def gather_pass(self, grp):
        self.gather[grp] = _gather_pass_exchange(self.gather[grp].results)
        return self.gather[grp]

    def weights(self, grp):
        return [g.reshape(-1, g.shape[-1]) for g in self.gather[grp].results]

    def reduce_sibling(self, grp, grads):
        self.grads[grp] = [g.reshape(N_CHIPS, 2, g.shape[0] // (2 * N_CHIPS), g.shape[1]) for g in grads]
        self.reduce[grp] = _reduce_sibling_exchange(self.grads[grp])
        return self.reduce[grp]

    def add_halves(self, grp):
        self.chip_sums[grp] = [_add_half(g, r, self.core, "add_half_%s%d" % (grp, i))
                               for i, (g, r) in enumerate(zip(self.grads[grp], self.reduce[grp].results))]

    def reduce_chips(self, grp):
        self.reduce[grp] = _reduce_chips_exchange(self.chip_sums[grp])
        return self.reduce[grp]

    def sum_chips(self, grp):
        self.half_sums[grp] = [_sum_chips(o, p, self.chip, "sum_chips_%s%d" % (grp, i))
                               for i, (o, p) in enumerate(zip(self.chip_sums[grp], self.reduce[grp].results))]

    def share(self, grp):
        self.shared[grp] = _share_exchange(self.half_sums[grp])
        return self.shared[grp]

    def totals(self, grp):
        return list(zip(self.half_sums[grp], self.shared[grp].results))


def _ffn_fwd(x, norm_g, shift, scale, gate, wg_t, wu_t, wd, tag, next_norm, up_exchange=None, down_exchange=None):
    h = _norm_mod_fwd(x, norm_g, shift, scale, tag + "_norm_fwd")
    a, u, hid = _ffn_up(h, wg_t, wu_t, tag + "_up", exchange=up_exchange)
    wd = wd() if callable(wd) else wd
    x_out, f, h_next = _mm(hid, wd, "nn", F32, tag + "_down", res=x, gate=gate, aux_dtype=BF16, norm=next_norm,
                           exchange=down_exchange() if down_exchange else None)
    return x_out, (h, a, u, hid, f), h_next


def _ffn_bwd(dx_out, df, x, saved, norm_g, scale, wg_t, wu_t, wd, tag, traffic, below=None, dact_exchange=None,
             dw_exchange=None, finish_reduction=False):
    h, a, u, hid, _ = saved
    f_below, gate_below = below if below else (None, None)
    da, du = _ffn_dact(df, wd, a, u, tag + "_dact", exchange=dact_exchange)
    dwd = _mm(hid, df, "tn", BF16, tag + "_dwd", exchange=dw_exchange() if dw_exchange else None)
    if not finish_reduction:
        dwg_t, dwu_t = _weight_grads([da, du], h, tag + "_dwg_dwu")
        dx, dshift, dscale, dnorm_g, *gated = _norm_mod_bwd(
            ([da, du], [wg_t, wu_t]), x, norm_g, scale, dx_out, tag + "_dh_norm_bwd", f=f_below, gate=gate_below,
            exchange=traffic.reduce_sibling(tag, [dwg_t, dwu_t, dwd]))
        traffic.add_halves(tag)
        return dx, (dshift, dscale, dnorm_g), gated
    kd, kg, ku = tag + "_wd", tag + "_wg", tag + "_wu"
    dwg_t = _mm(da, h, "tn", BF16, tag + "_dwg", exchange=traffic.reduce_sibling(kd, [dwd]))
    traffic.add_halves(kd)
    dwu_t = _mm(du, h, "tn", BF16, tag + "_dwu",
                exchange=_join(traffic.reduce_chips(kd), traffic.reduce_sibling(kg, [dwg_t])))
    traffic.add_halves(kg)
    half = x.shape[0] // 2
    top = _norm_mod_bwd(([da, du], [wg_t, wu_t]), x, norm_g, scale, dx_out, tag + "_dh_norm_bwd_top", f=f_below,
                        gate=gate_below, rows=(0, half),
                        exchange=_join(traffic.reduce_chips(kg), traffic.reduce_sibling(ku, [dwu_t])))
    traffic.add_halves(ku)
    traffic.sum_chips(kd)
    traffic.sum_chips(kg)
    bottom = _norm_mod_bwd(([da, du], [wg_t, wu_t]), x, norm_g, scale, dx_out, tag + "_dh_norm_bwd_bottom", f=f_below,
                           gate=gate_below, rows=(half, half),
                           exchange=_join(traffic.reduce_chips(ku), traffic.share(kd), traffic.share(kg)))
    traffic.sum_chips(ku)
    dx, dshift, dscale, dnorm_g, *gated = [jnp.concatenate([a, b]) if a.shape[0] == half else a + b
                                           for a, b in zip(top, bottom)]
    return dx, (dshift, dscale, dnorm_g), gated


def _layer_step(x, target, mod, gains, forget_bias, conv_w, traffic, att_w, in_shard, in_rows):
    sh1, sc1, g1, sh2, sc2, g2, sh3, sc3, g3 = mod
    norm1_g, norm2_g, norm3_g, final_g, group_g = gains
    s, d = x.shape
    n_heads = att_w // HEAD_DIM
    npair = n_heads // 2
    gate1, gate3 = 0.5 * g1, 0.5 * g3

    def split_w_in(w_in_pad):
        w_in_t = w_in_pad.reshape(N_CHIPS, in_rows, d)[:, :in_shard].reshape(N_CHIPS * in_shard, d)
        return (w_in_t[:3 * att_w], _pad_rows(w_in_t[3 * att_w:3 * att_w + n_heads], LANES), w_in_t[3 * att_w + n_heads:])

    wg1_t, wu1_t = traffic.weights("ffn1_gu")

    def wd1_ready():
        _run_exchange(traffic.gather_pass("ffn1_d"), "gather_ffn1_down_pass")
        return traffic.weights("ffn1_d")[0]

    x1, saved1, h2 = _ffn_fwd(x, norm1_g, sh1, sc1, gate1, wg1_t, wu1_t, wd1_ready, "ffn1", (norm2_g, sh2, sc2),
                              up_exchange=_join(traffic.gather_ici("ffn1_d"), traffic.gather_ici("mix_in")),
                              down_exchange=lambda: _join(traffic.gather_pass("mix_in"), traffic.gather_ici("mix_out")))
    wd1 = traffic.weights("ffn1_d")[0]
    wqkv_t, wf_t, wbcx_t = split_w_in(traffic.weights("mix_in")[0])

    qkv, bcx, flog = _project(h2, [wqkv_t, wbcx_t, wf_t], [BF16, F32, F32], "mix_proj",
                              exchange=traffic.gather_pass("mix_out"))
    w_out = traffic.weights("mix_out")[0]
    flog_t = jnp.pad(flog[:, :n_heads].T, ((0, HEAD_ROWS - n_heads), (0, 0)))
    bias_col = jnp.pad(forget_bias, (0, HEAD_ROWS - n_heads))[:, None]
    f_pieces = _forget_fwd(flog_t, bias_col, "forget_fwd")
    qa, ka, va = _attn_prep(qkv, f_pieces, "attn_prep")
    att, lse = _attn_fwd(qa, ka, va, "attn_fwd", exchange=traffic.gather_ici("ffn2"))
    cv = _conv_fwd(bcx, conv_w, "conv_fwd")
    yn = _gnorm_fwd(att, cv, group_g, "gnorm_fwd")
    x2, mix, h3 = _mm(yn, w_out, "nn", F32, "mix_out", res=x1, gate=g2, aux_dtype=BF16, norm=(norm3_g, sh3, sc3),
                      exchange=traffic.gather_pass("ffn2"))
    wg2_t, wu2_t, wd2 = traffic.weights("ffn2")

    a3, u3, hid3 = _ffn_up(h3, wg2_t, wu2_t, "ffn2_up")
    saved3 = (h3, a3, u3, hid3, None)
    dx3, loss_row, dfinal_g, df2, dgate3 = _down_final_loss(hid3, wd2, x2, gate3, final_g, target, "ffn2_down_loss")

    dx2, (dsh3, dsc3, dnorm3_g), (dmix, dg2) = _ffn_bwd(
        dx3, df2, x2, saved3, norm3_g, sc3, wg2_t, wu2_t, wd2, "ffn2", traffic, below=(mix, g2))
    dyn = _mm(dmix, w_out, "nt", F32, "mix_out_dyn")
    dw_out = _mm(yn, dmix, "tn", BF16, "mix_out_dw")
    datt, dcv, dgroup_g = _gnorm_bwd(dyn, att, cv, group_g, "gnorm_bwd")
    db, dc, dxc, dconv_w = _conv_bwd(dcv, bcx, conv_w, "conv_bwd")
    dbcx = jnp.concatenate([db, dc, dxc], axis=1)
    dq, dk, dv, qx, kx = _attn_bwd(qa, ka, va, datt, att, lse, "attn_bwd", exchange=traffic.reduce_chips("ffn2"))
    traffic.sum_chips("ffn2")
    dqkv = jnp.concatenate([dq.astype(BF16), dk, dv], axis=1)
    df_t = _decay_grads(qx, kx, "decay_grads")[:, :HEAD_ROWS].T
    dflog_t, dbias_col = _forget_bwd(df_t, flog_t, bias_col, "forget_bwd")
    dflog = jnp.pad(dflog_t[:n_heads].T, ((0, 0), (0, LANES - n_heads))).astype(BF16)
    dwqkv_t, dwbcx_t = _weight_grads([dqkv, dbcx], h2, "mix_dw", exchange=traffic.share("ffn2"))
    dwf_t = _mm(dflog, h2, "tn", BF16, "mix_dw_f")
    dw_in_t = jnp.concatenate([dwqkv_t, dwf_t[:n_heads], dwbcx_t], axis=0).reshape(N_CHIPS, in_shard, d)
    dw_in_t = jnp.pad(dw_in_t, ((0, 0), (0, in_rows - in_shard), (0, 0))).reshape(N_CHIPS * in_rows, d)
    dx1, dsh2, dsc2, dnorm2_g, df1, dgate1 = _norm_mod_bwd(
        ([dqkv, dbcx, dflog], [wqkv_t, wbcx_t, wf_t]), x1, norm2_g, sc2, dx2, "mix_dh_norm_bwd", f=saved1[4], gate=gate1,
        exchange=traffic.reduce_sibling("mix", [dw_in_t, dw_out]))
    traffic.add_halves("mix")

    def share_mix():
        traffic.sum_chips("mix")
        return traffic.share("mix")

    dx, (dsh1, dsc1, dnorm1_g), _ = _ffn_bwd(
        dx1, df1, x, saved1, norm1_g, sc1, wg1_t, wu1_t, wd1, "ffn1", traffic,
        dact_exchange=traffic.reduce_chips("mix"), dw_exchange=share_mix, finish_reduction=True)

    dmod = [dsh1, dsc1, 0.5 * dgate1, dsh2, dsc2, dg2, dsh3, dsc3, 0.5 * dgate3]
    dgains = [dnorm1_g, dnorm2_g, dnorm3_g, dfinal_g, dgroup_g]
    dbias = dbias_col[:n_heads, 0]
    return dx, loss_row, dmod, dgains, dbias, dconv_w


SMALL_ROWS = 24
ROW_GAINS, ROW_LOSS, ROW_FORGET, ROW_CONV, ROW_MOD = 0, 5, 6, 7, 10
PROW_ADA_B, PROW_GAINS, PROW_FORGET, PROW_CONV = 0, 9, 14, 15


def _round_up(n, m):
    return -(-n // m) * m


def _pad_rows(a, rows):
    return jnp.pad(a, ((0, rows - a.shape[0]), (0, 0)))


def _halves(a):
    return a.reshape(2, a.shape[0] // 2, a.shape[1])


def _rows_at(a, r0, total, width):
    return jnp.pad(a, ((r0, total - r0 - a.shape[0]), (0, width - a.shape[1])))


def kernel(x, c, ada_w, ada_b, norm1_g, ffn1_w_gate, ffn1_w_up, ffn1_w_down, norm2_g, w_in, forget_bias, conv_w, group_norm_g, w_out, norm3_g, ffn2_w_gate, ffn2_w_up, ffn2_w_down, final_g, loss_target, m_ada_w, m_ada_b, m_norm1_g, m_ffn1_w_gate, m_ffn1_w_up, m_ffn1_w_down, m_norm2_g, m_w_in, m_forget_bias, m_conv_w, m_group_norm_g, m_w_out, m_norm3_g, m_ffn2_w_gate, m_ffn2_w_up, m_ffn2_w_down, m_final_g, v_ada_w, v_ada_b, v_norm1_g, v_ffn1_w_gate, v_ffn1_w_up, v_ffn1_w_down, v_norm2_g, v_w_in, v_forget_bias, v_conv_w, v_group_norm_g, v_w_out, v_norm3_g, v_ffn2_w_gate, v_ffn2_w_up, v_ffn2_w_down, v_final_g):
    xi, yi, ci = lax.axis_index("x"), lax.axis_index("y"), lax.axis_index("c")
    chip = 2 * xi + yi
    dev = 4 * xi + 2 * yi + ci
    _, s, d = x.shape
    att_w = d // 2
    conv_width = d - att_w
    n_heads = att_w // HEAD_DIM
    in_shard = w_in.shape[1]
    in_rows = _round_up(in_shard, 32)
    cs = conv_w.shape[1]
    mod_shard = ada_w.shape[1]
    assert N_MOD * d == N_CHIPS * mod_shard and conv_width == N_CHIPS * cs and n_heads % 2 == 0

    def t_bf(w):
        return w.T.astype(BF16)

    shards = {"ffn1_gu": [_halves(t_bf(ffn1_w_gate)), _halves(t_bf(ffn1_w_up))], "ffn1_d": [_halves(ffn1_w_down.astype(BF16))],
              "mix_in": [_halves(_pad_rows(t_bf(w_in), in_rows))], "mix_out": [_halves(w_out.astype(BF16))],
              "ffn2": [_halves(t_bf(ffn2_w_gate)), _halves(t_bf(ffn2_w_up)), _halves(ffn2_w_down.astype(BF16))]}
    core = ci.astype(jnp.int32).reshape(1)
    chip_arr = chip.astype(jnp.int32).reshape(1)
    traffic = _WeightTraffic(shards, core, chip_arr)

    cond = _small_gather_exchange(_rows_at(c, 0, 8, d) + _rows_at(conv_w, 1, 8, d))
    _run_exchange(_join(traffic.gather_ici("ffn1_gu"), cond), "gather_ffn1_ici")
    got0 = cond.results[0].reshape(N_DEV, 8, d)
    c16 = _pad_rows(got0[:, 0, :], 16)
    conv_full = got0[0::2, 1:1 + CONV_K, :cs].transpose(1, 0, 2).reshape(CONV_K, conv_width)

    ada_b_mine = lax.dynamic_slice(ada_b, (chip * mod_shard,), (mod_shard,))[None, :]
    mods = _small_gather_exchange(_ada_fwd(c16, ada_w, ada_b_mine, "ada_fwd"))
    _run_exchange(_join(traffic.gather_pass("ffn1_gu"), mods), "gather_ffn1_pass")
    got1 = mods.results[0].reshape(N_DEV, 16, mod_shard)
    mod_mine = lax.dynamic_index_in_dim(got1[0::2], dev, axis=1, keepdims=False).reshape(N_MOD, d)
    mod = [mod_mine[i:i + 1] for i in range(N_MOD)]

    gains = [g[None, :] for g in (norm1_g, norm2_g, norm3_g, final_g, group_norm_g)]
    dx, loss_row, dmod, dgains, dbias, dconv_w = _layer_step(
        x[0], loss_target[0], mod, gains, forget_bias, conv_full, traffic, att_w, in_shard, in_rows)

    pack = sum(_rows_at(g, ROW_GAINS + i, SMALL_ROWS, d) for i, g in enumerate(dgains))
    pack += _rows_at(loss_row, ROW_LOSS, SMALL_ROWS, d) + _rows_at(dbias[None, :], ROW_FORGET, SMALL_ROWS, d)
    pack += _rows_at(dconv_w, ROW_CONV, SMALL_ROWS, d)
    pack += sum(_rows_at(g, ROW_MOD + i, SMALL_ROWS, d) for i, g in enumerate(dmod))
    small = _small_gather_exchange(pack)
    _run_exchange(_join(traffic.share("ffn1_wu"), small), "gather_small_grads")
    got2 = small.results[0].reshape(N_DEV, SMALL_ROWS, d)
    tot = _sum_devices(got2, "sum_small_grads")
    loss = tot[ROW_LOSS, 0]
    grad_ada_b = tot[ROW_MOD:ROW_MOD + N_MOD].reshape(N_MOD * d)
    grad_conv = lax.dynamic_slice(tot[ROW_CONV:ROW_CONV + CONV_K], (0, chip * cs), (CONV_K, cs))
    dmod_all = got2[:, ROW_MOD:ROW_MOD + N_MOD, :].reshape(N_DEV, N_MOD * d)
    dmod16 = _pad_rows(lax.dynamic_slice(dmod_all, (0, chip * mod_shard), (N_DEV, mod_shard)), 16)

    out = {"ada_w": tuple(_ada_update(c16.T, dmod16, ada_w, m_ada_w, v_ada_w, "adamw_ada_w"))}
    totals = (traffic.totals("ffn1_wg") + traffic.totals("ffn1_wu") + traffic.totals("ffn1_wd")
              + traffic.totals("mix") + traffic.totals("ffn2"))

    names = ("ffn1_w_gate", "ffn1_w_up", "ffn1_w_down", "w_in", "w_out", "ffn2_w_gate", "ffn2_w_up", "ffn2_w_down")
    transposed = ("ffn1_w_gate", "ffn1_w_up", "w_in", "ffn2_w_gate", "ffn2_w_up")
    params = {"ffn1_w_gate": (ffn1_w_gate, m_ffn1_w_gate, v_ffn1_w_gate), "ffn1_w_up": (ffn1_w_up, m_ffn1_w_up, v_ffn1_w_up),
              "ffn1_w_down": (ffn1_w_down, m_ffn1_w_down, v_ffn1_w_down), "w_in": (w_in, m_w_in, v_w_in),
              "w_out": (w_out, m_w_out, v_w_out), "ffn2_w_gate": (ffn2_w_gate, m_ffn2_w_gate, v_ffn2_w_gate),
              "ffn2_w_up": (ffn2_w_up, m_ffn2_w_up, v_ffn2_w_up), "ffn2_w_down": (ffn2_w_down, m_ffn2_w_down, v_ffn2_w_down)}
    for name_, (mine, theirs) in zip(names, totals):
        w, m, v = params[name_]
        if name_ in transposed:
            w, m, v = w.T, m.T, v.T
        if name_ == "w_in":
            both = jnp.where(ci == 0, jnp.concatenate([mine, theirs]), jnp.concatenate([theirs, mine]))[:in_shard]
            res = (both,) + tuple(_adamw(w, both, m, v, "adamw_" + name_))
        else:
            res = _adamw_halves(w, mine, theirs, m, v, core, "adamw_" + name_)
        out[name_] = tuple(r.T for r in res) if name_ in transposed else tuple(res)

    def small_pack(ada_b_, gains_, forget_, conv_):
        p = _rows_at(ada_b_.reshape(N_MOD, d), PROW_ADA_B, SMALL_ROWS, d)
        p += sum(_rows_at(g[None, :], PROW_GAINS + i, SMALL_ROWS, d) for i, g in enumerate(gains_))
        p += _rows_at(forget_[None, :], PROW_FORGET, SMALL_ROWS, d) + _rows_at(conv_, PROW_CONV, SMALL_ROWS, d)
        return p

    g_gains = [tot[ROW_GAINS + i] for i in range(5)]
    g_forget = tot[ROW_FORGET, :n_heads]
    sw = small_pack(ada_b, (norm1_g, norm2_g, norm3_g, final_g, group_norm_g), forget_bias, conv_w)
    sm = small_pack(m_ada_b, (m_norm1_g, m_norm2_g, m_norm3_g, m_final_g, m_group_norm_g), m_forget_bias, m_conv_w)
    sv = small_pack(v_ada_b, (v_norm1_g, v_norm2_g, v_norm3_g, v_final_g, v_group_norm_g), v_forget_bias, v_conv_w)
    sg = small_pack(grad_ada_b, g_gains, g_forget, grad_conv)
    small = (sg,) + tuple(_adamw(sw, sg, sm, sv, "adamw_small"))

    def unpack(p):
        r = {"ada_b": p[PROW_ADA_B:PROW_ADA_B + N_MOD].reshape(N_MOD * d), "forget_bias": p[PROW_FORGET, :n_heads],
             "conv_w": p[PROW_CONV:PROW_CONV + CONV_K, :cs]}
        for i, nm in enumerate(("norm1_g", "norm2_g", "norm3_g", "final_g", "group_norm_g")):
            r[nm] = p[PROW_GAINS + i]
        return r

    small = [unpack(p) for p in small]
    order = ("ada_w", "ada_b", "norm1_g", "ffn1_w_gate", "ffn1_w_up", "ffn1_w_down", "norm2_g", "w_in", "forget_bias",
             "conv_w", "group_norm_g", "w_out", "norm3_g", "ffn2_w_gate", "ffn2_w_up", "ffn2_w_down", "final_g")
    result = [loss, dx[None]]
    for k in range(4):
        result += [out[nm][k] if nm in out else small[k][nm] for nm in order]
    return tuple(result)
```

```python
import functools
import math

import jax
import jax.numpy as jnp
from jax import lax
from jax.experimental import pallas as pl
from jax.experimental.pallas import tpu as pltpu

F32 = jnp.float32
BF16 = jnp.bfloat16

HEAD_DIM = 64
CONV_K = 3
N_MOD = 9
EPS = 1e-6
ADAM_LR = 0.001
ADAM_B1 = 0.9
ADAM_B2 = 0.999
ADAM_EPS = 1e-08
ADAM_WD = 0.01
ADAM_STEP = 10

LANES = 128
N_CHIPS = 4
N_DEV = 8
VMEM_LIMIT_BYTES = 56 * 1024 * 1024
MAX_CONTRACTION = 4096
NEG_BIG = -1e30
MESH = pl.DeviceIdType.MESH

_NT = (((1,), (1,)), ((), ()))
_NN = (((1,), (0,)), ((), ()))
_TN = (((0,), (0,)), ((), ()))


def _params(*sem):
    return pltpu.CompilerParams(dimension_semantics=sem, vmem_limit_bytes=VMEM_LIMIT_BYTES)


class _Exchange:
    def __init__(self, inputs, out_shapes, n_sems, start, finish, aliases=None):
        self.inputs, self.out_shapes, self.n_sems = list(inputs), list(out_shapes), n_sems
        self.start, self.finish, self.aliases = start, finish, dict(aliases or {})
        self.results = None

    def set_results(self, results):
        self.results = list(results)


class _SemaphoreWindow:
    def __init__(self, sems, base):
        self.sems, self.base = sems, base
        self.at = self

    def __getitem__(self, k):
        return self.sems.at[self.base + k]


class _JoinedExchange(_Exchange):
    def __init__(self, parts):
        self.parts = parts
        aliases, i0, o0 = {}, 0, 0
        for p in parts:
            aliases.update({i0 + a: o0 + b for a, b in p.aliases.items()})
            i0, o0 = i0 + len(p.inputs), o0 + len(p.out_shapes)

        def each(method, src, dst, send_sems, recv_sems):
            i0 = o0 = s0 = 0
            for p in parts:
                i1, o1 = i0 + len(p.inputs), o0 + len(p.out_shapes)
                getattr(p, method)(src[i0:i1], dst[o0:o1], _SemaphoreWindow(send_sems, s0), _SemaphoreWindow(recv_sems, s0))
                i0, o0, s0 = i1, o1, s0 + p.n_sems

        super().__init__([a for p in parts for a in p.inputs], [o for p in parts for o in p.out_shapes],
                         sum(p.n_sems for p in parts), functools.partial(each, "start"), functools.partial(each, "finish"),
                         aliases)

    def set_results(self, results):
        o0 = 0
        for p in self.parts:
            p.set_results(results[o0:o0 + len(p.out_shapes)])
            o0 += len(p.out_shapes)


def _join(*parts):
    return parts[0] if len(parts) == 1 else _JoinedExchange(list(parts))


def _pc(body, exchange=None, **kw):
    if exchange is None:
        return pl.pallas_call(body, **kw)
    grid = kw["grid"]
    single = not isinstance(kw["out_shape"], (tuple, list))
    out_shape = [kw["out_shape"]] if single else list(kw["out_shape"])
    out_specs = [kw["out_specs"]] if single else list(kw["out_specs"])
    in_specs = list(kw["in_specs"])
    scratch = list(kw.get("scratch_shapes", ()))
    n_in, n_out, n_scr = len(in_specs), len(out_shape), len(scratch)
    n_xi, n_xo = len(exchange.inputs), len(exchange.out_shapes)

    def wrapped(*refs):
        pos = [n_in, n_in + n_xi, n_in + n_xi + n_out, n_in + n_xi + n_out + n_xo]
        ins, x_in, outs, x_out = refs[:pos[0]], refs[pos[0]:pos[1]], refs[pos[1]:pos[2]], refs[pos[2]:pos[3]]
        scr = refs[pos[3]:pos[3] + n_scr]
        send_sems, recv_sems = refs[pos[3] + n_scr:]
        ids = [pl.program_id(a) for a in range(len(grid))]
        first = functools.reduce(jnp.logical_and, [i == 0 for i in ids])
        last = functools.reduce(jnp.logical_and, [i == g - 1 for i, g in zip(ids, grid)])

        @pl.when(first)
        def _():
            exchange.start(x_in, x_out, send_sems, recv_sems)

        body(*ins, *outs, *scr)

        @pl.when(last)
        def _():
            exchange.finish(x_in, x_out, send_sems, recv_sems)

    call = pl.pallas_call(
        wrapped, out_shape=tuple(out_shape) + tuple(exchange.out_shapes), grid=grid,
        in_specs=in_specs + [_ANY] * n_xi, out_specs=tuple(out_specs) + (_ANY,) * n_xo,
        scratch_shapes=scratch + [pltpu.SemaphoreType.DMA((exchange.n_sems,)), pltpu.SemaphoreType.DMA((exchange.n_sems,))],
        input_output_aliases={n_in + a: n_out + b for a, b in exchange.aliases.items()},
        compiler_params=_params(*(["arbitrary"] * len(grid))), name=kw["name"])

    def run(*args):
        res = call(*args, *exchange.inputs)
        exchange.set_results(res[n_out:])
        return res[0] if single else tuple(res[:n_out])

    return run


_ANY = pl.BlockSpec(memory_space=pl.ANY)


def _tile(n, pref, mult):
    best = None
    t = mult
    while t <= min(n, pref):
        if n % t == 0:
            best = t
        t += mult
    return n if best is None else best


def _sds(shape, dtype):
    return jax.ShapeDtypeStruct(shape, dtype)


def _vec_spec(d):
    return pl.BlockSpec((1, d), lambda *_: (0, 0))


def _norm_mod_fwd(x, g, shift, scale, name):
    s, d = x.shape
    tr = _tile(s, 512, 16)

    def body(x_ref, g_ref, sh_ref, sc_ref, h_ref):
        xv = x_ref[...]
        rstd = lax.rsqrt(jnp.mean(xv * xv, axis=-1, keepdims=True) + EPS)
        n = xv * rstd * g_ref[...]
        h_ref[...] = (n * (1.0 + sc_ref[...]) + sh_ref[...]).astype(BF16)

    row = pl.BlockSpec((tr, d), lambda i: (i, 0))
    return _pc(body, out_shape=_sds((s, d), BF16), grid=(s // tr,),
               in_specs=[row, _vec_spec(d), _vec_spec(d), _vec_spec(d)], out_specs=row,
               compiler_params=_params("parallel"), name=name)(x, g, shift, scale)


def _through_gate(dx, f_ref, gate_ref, df_ref, dgate_ref):
    df_ref[...] = (dx * gate_ref[...]).astype(BF16)
    dgate_ref[...] += jnp.sum(dx * f_ref[...].astype(F32), axis=0, keepdims=True)


def _norm_mod_bwd(dh, x, g, scale, dres, name, f=None, gate=None, rows=None, exchange=None):
    d = x.shape[1]
    first_row, s = rows if rows else (0, x.shape[0])
    gated = f is not None
    terms = list(zip(*dh)) if isinstance(dh, tuple) else None
    tr = _tile(s, 256, 16)
    b0 = first_row // tr
    assert first_row % tr == 0
    n_lead = 2 * len(terms) if terms else 1

    def body(*refs):
        lead, (x_ref, g_ref, sc_ref, dres_ref), rest = refs[:n_lead], refs[n_lead:n_lead + 4], refs[n_lead + 4:]
        f_ref, gate_ref = rest[:2] if gated else (None, None)
        dx_ref, dsh_ref, dsc_ref, dg_ref = rest[2:6] if gated else rest[:4]
        df_ref, dgate_ref = rest[6:8] if gated else (None, None)

        @pl.when(pl.program_id(0) == 0)
        def _():
            for ref in (dsh_ref, dsc_ref, dg_ref) + ((dgate_ref,) if gated else ()):
                ref[...] = jnp.zeros_like(ref)

        if terms:
            dhv = lax.dot_general(lead[0][...], lead[1][...], _NN, preferred_element_type=F32)
            for p in range(1, len(terms)):
                dhv += lax.dot_general(lead[2 * p][...], lead[2 * p + 1][...], _NN, preferred_element_type=F32)
        else:
            dhv = lead[0][...]
        xv = x_ref[...]
        gv = g_ref[...]
        rstd = lax.rsqrt(jnp.mean(xv * xv, axis=-1, keepdims=True) + EPS)
        xhat = xv * rstd
        dn = dhv * (1.0 + sc_ref[...])
        dsh_ref[...] += jnp.sum(dhv, axis=0, keepdims=True)
        dsc_ref[...] += jnp.sum(dhv * (xhat * gv), axis=0, keepdims=True)
        dg_ref[...] += jnp.sum(dn * xhat, axis=0, keepdims=True)
        dxh = dn * gv
        proj = jnp.mean(dxh * xhat, axis=-1, keepdims=True)
        dx = dres_ref[...] + rstd * (dxh - xhat * proj)
        dx_ref[...] = dx
        if gated:
            _through_gate(dx, f_ref, gate_ref, df_ref, dgate_ref)

    row = pl.BlockSpec((tr, d), lambda i: (b0 + i, 0))
    out_row = pl.BlockSpec((tr, d), lambda i: (i, 0))
    vec = _vec_spec(d)
    if terms:
        in_specs, args = [], []
        for l, r in terms:
            assert l.shape[1] == r.shape[0] <= MAX_CONTRACTION and r.shape[1] == d
            in_specs += [pl.BlockSpec((tr, l.shape[1]), lambda i: (b0 + i, 0)), pl.BlockSpec(r.shape, lambda i: (0, 0))]
            args += [l, r]
    else:
        in_specs, args = [row], [dh]
    in_specs += [row, vec, vec, row]
    args += [x, g, scale, dres]
    out_shape = [_sds((s, d), F32), _sds((1, d), F32), _sds((1, d), F32), _sds((1, d), F32)]
    out_specs = [out_row, vec, vec, vec]
    if gated:
        out_shape += [_sds((s, d), BF16), _sds((1, d), F32)]
        out_specs += [out_row, vec]
        in_specs += [row, vec]
        args += [f, gate]
    return _pc(body, exchange, out_shape=tuple(out_shape), grid=(s // tr,), in_specs=in_specs,
               out_specs=tuple(out_specs), compiler_params=_params("arbitrary"), name=name)(*args)


def _down_final_loss(hid, wd, res, gate, g, target, name):
    s, d = res.shape
    k = hid.shape[1]
    assert k <= MAX_CONTRACTION
    tr = _tile(s, 256, 16)
    nsteps = s // tr

    def body(hid_ref, wd_ref, res_ref, gate_ref, g_ref, t_ref, dx_ref, loss_ref, dg_ref, df_ref, dgate_ref):
        i = pl.program_id(0)

        @pl.when(i == 0)
        def _():
            loss_ref[...] = jnp.zeros_like(loss_ref)
            dg_ref[...] = jnp.zeros_like(dg_ref)
            dgate_ref[...] = jnp.zeros_like(dgate_ref)

        f = lax.dot_general(hid_ref[...], wd_ref[...], _NN, preferred_element_type=F32)
        gatev = gate_ref[...]
        xv = res_ref[...] + gatev * f
        gv = g_ref[...]
        rstd = lax.rsqrt(jnp.mean(xv * xv, axis=-1, keepdims=True) + EPS)
        xhat = xv * rstd
        err = xhat * gv - t_ref[...]
        dy = err * (1.0 / d)
        loss_ref[...] += jnp.sum(0.5 * err * dy, axis=0, keepdims=True)
        dg_ref[...] += jnp.sum(dy * xhat, axis=0, keepdims=True)
        dxh = dy * gv
        proj = jnp.mean(dxh * xhat, axis=-1, keepdims=True)
        dx = rstd * (dxh - xhat * proj)
        dx_ref[...] = dx
        df_ref[...] = (dx * gatev).astype(BF16)
        dgate_ref[...] += jnp.sum(dx * f, axis=0, keepdims=True)

        @pl.when(i == nsteps - 1)
        def _():
            loss_ref[...] = jnp.broadcast_to(jnp.sum(loss_ref[...], axis=-1, keepdims=True), loss_ref.shape)

    row = pl.BlockSpec((tr, d), lambda i: (i, 0))
    vec = _vec_spec(d)
    return _pc(body, out_shape=(_sds((s, d), F32), _sds((1, d), F32), _sds((1, d), F32), _sds((s, d), BF16), _sds((1, d), F32)),
               grid=(nsteps,),
               in_specs=[pl.BlockSpec((tr, k), lambda i: (i, 0)), pl.BlockSpec((k, d), lambda i: (0, 0)), row, vec, vec, row],
               out_specs=(row, vec, vec, row, vec),
               compiler_params=_params("arbitrary"), name=name)(hid, wd, res, gate, g, target)


def _mm(lhs, rhs, dims, out_dtype, name, res=None, gate=None, aux_dtype=None, norm=None, exchange=None):
    lhs_list = list(lhs) if isinstance(lhs, (list, tuple)) else [lhs]
    rhs_list = list(rhs) if isinstance(rhs, (list, tuple)) else [rhs]
    n_terms = len(lhs_list)
    assert n_terms == len(rhs_list)
    m = lhs_list[0].shape[1 if dims == "tn" else 0]
    n = rhs_list[0].shape[0 if dims == "nt" else 1]
    tn = _tile(n, 1024, LANES)
    tm = _tile(m, 512, LANES if dims == "tn" else 16)
    dn = {"nn": _NN, "nt": _NT, "tn": _TN}[dims]
    in_specs, args = [], []
    for l, r in zip(lhs_list, rhs_list):
        k = l.shape[0 if dims == "tn" else 1]
        assert k == r.shape[1 if dims == "nt" else 0] and k <= MAX_CONTRACTION, (l.shape, r.shape, dims)
        in_specs.append(pl.BlockSpec((k, tm), lambda i, j: (0, i)) if dims == "tn" else pl.BlockSpec((tm, k), lambda i, j: (i, 0)))
        in_specs.append(pl.BlockSpec((tn, k), lambda i, j: (j, 0)) if dims == "nt" else pl.BlockSpec((k, tn), lambda i, j: (0, j)))
        args += [l, r]
    out_spec = pl.BlockSpec((tm, tn), lambda i, j: (i, j))
    has_res, has_gate, has_aux, has_norm = res is not None, gate is not None, aux_dtype is not None, norm is not None
    assert not has_norm or tn == n

    def body(*refs):
        refs = list(refs)
        pos = 2 * n_terms
        res_ref = gate_ref = aux_ref = None
        if has_res:
            res_ref = refs[pos]; pos += 1
        if has_gate:
            gate_ref = refs[pos]; pos += 1
        if has_norm:
            ng_ref, nsh_ref, nsc_ref = refs[pos:pos + 3]; pos += 3
        out_ref = refs[pos]; pos += 1
        if has_aux:
            aux_ref = refs[pos]; pos += 1
        acc = lax.dot_general(refs[0][...], refs[1][...], dn, preferred_element_type=F32)
        for p in range(1, n_terms):
            acc += lax.dot_general(refs[2 * p][...], refs[2 * p + 1][...], dn, preferred_element_type=F32)
        if has_aux:
            aux_ref[...] = acc.astype(aux_dtype)
        if has_gate:
            acc = acc * gate_ref[...]
        if has_res:
            acc = res_ref[...] + acc
        out_ref[...] = acc.astype(out_dtype)
        if has_norm:
            rstd = lax.rsqrt(jnp.mean(acc * acc, axis=-1, keepdims=True) + EPS)
            refs[pos][...] = (acc * rstd * ng_ref[...] * (1.0 + nsc_ref[...]) + nsh_ref[...]).astype(BF16)

    if has_res:
        in_specs.append(out_spec); args.append(res)
    if has_gate:
        in_specs.append(pl.BlockSpec((1, tn), lambda i, j: (0, j))); args.append(gate)
    if has_norm:
        in_specs += [pl.BlockSpec((1, tn), lambda i, j: (0, j))] * 3
        args += list(norm)
    out_shape = [_sds((m, n), out_dtype)]
    out_specs = [out_spec]
    if has_aux:
        out_shape.append(_sds((m, n), aux_dtype)); out_specs.append(out_spec)
    if has_norm:
        out_shape.append(_sds((m, n), BF16)); out_specs.append(out_spec)
    outs = _pc(body, exchange, out_shape=tuple(out_shape), grid=(m // tm, n // tn), in_specs=in_specs,
               out_specs=tuple(out_specs), compiler_params=_params("parallel", "parallel"), name=name)(*args)
    return outs if len(out_shape) > 1 else outs[0]


def _project(h, weights_t, out_dtypes, name, exchange=None):
    s, d = h.shape
    n = len(weights_t)
    tm = _tile(s, 512, 16)

    def body(*refs):
        hv = refs[0][...]
        for i in range(n):
            acc = lax.dot_general(hv, refs[1 + i][...], _NT, preferred_element_type=F32)
            refs[1 + n + i][...] = acc.astype(out_dtypes[i])

    return _pc(body, exchange, out_shape=tuple(_sds((s, w.shape[0]), dt) for w, dt in zip(weights_t, out_dtypes)),
               grid=(s // tm,),
               in_specs=[pl.BlockSpec((tm, d), lambda i: (i, 0))] + [pl.BlockSpec(w.shape, lambda i: (0, 0)) for w in weights_t],
               out_specs=tuple(pl.BlockSpec((tm, w.shape[0]), lambda i: (i, 0)) for w in weights_t),
               compiler_params=_params("parallel"), name=name)(h, *weights_t)


def _weight_grads(cotangents, h, name, exchange=None):
    s, d = h.shape
    m = cotangents[0].shape[1]
    n = len(cotangents)
    assert s <= MAX_CONTRACTION and all(c.shape == (s, m) for c in cotangents)
    tm = _tile(m, 256, LANES)

    def body(*refs):
        hv = refs[n][...]
        for i in range(n):
            refs[n + 1 + i][...] = lax.dot_general(refs[i][...], hv, _TN, preferred_element_type=F32).astype(BF16)

    return _pc(body, exchange, out_shape=(_sds((m, d), BF16),) * n, grid=(m // tm,),
               in_specs=[pl.BlockSpec((s, tm), lambda i: (0, i))] * n + [pl.BlockSpec((s, d), lambda i: (0, 0))],
               out_specs=(pl.BlockSpec((tm, d), lambda i: (i, 0)),) * n,
               compiler_params=_params("parallel"), name=name)(*cotangents, h)


def _ffn_up(h, wg_t, wu_t, name, exchange=None):
    s, d = h.shape
    f = wg_t.shape[0]
    tm = _tile(s, 512, 16)
    tn = _tile(f, 1408, LANES)

    def body(h_ref, wg_ref, wu_ref, a_ref, u_ref, hid_ref):
        hv = h_ref[...]
        a = lax.dot_general(hv, wg_ref[...], _NT, preferred_element_type=F32)
        u = lax.dot_general(hv, wu_ref[...], _NT, preferred_element_type=F32)
        sig = jax.nn.sigmoid(a)
        silu = a * sig
        a_ref[...] = (u * (sig + silu * (1.0 - sig))).astype(BF16)
        u_ref[...] = silu.astype(BF16)
        hid_ref[...] = (silu * u).astype(BF16)

    hs = pl.BlockSpec((tm, d), lambda i, j: (i, 0))
    ws = pl.BlockSpec((tn, d), lambda i, j: (j, 0))
    os_ = pl.BlockSpec((tm, tn), lambda i, j: (i, j))
    return _pc(body, exchange, out_shape=(_sds((s, f), BF16),) * 3, grid=(s // tm, f // tn),
               in_specs=[hs, ws, ws], out_specs=(os_, os_, os_),
               compiler_params=_params("parallel", "parallel"), name=name)(h, wg_t, wu_t)


def _ffn_dact(df, wd, a, u, name, exchange=None):
    s, d = df.shape
    f = wd.shape[0]
    tm = _tile(s, 512, 16)
    tn = _tile(f, 1408, LANES)

    def body(df_ref, wd_ref, a_ref, u_ref, da_ref, du_ref):
        dhid = lax.dot_general(df_ref[...], wd_ref[...], _NT, preferred_element_type=F32)
        da_ref[...] = (dhid * a_ref[...].astype(F32)).astype(BF16)
        du_ref[...] = (dhid * u_ref[...].astype(F32)).astype(BF16)

    ds_ = pl.BlockSpec((tm, d), lambda i, j: (i, 0))
    ws = pl.BlockSpec((tn, d), lambda i, j: (j, 0))
    os_ = pl.BlockSpec((tm, tn), lambda i, j: (i, j))
    return _pc(body, exchange, out_shape=(_sds((s, f), BF16),) * 2, grid=(s // tm, f // tn),
               in_specs=[ds_, ws, os_, os_], out_specs=(os_, os_),
               compiler_params=_params("parallel", "parallel"), name=name)(df, wd, a, u)


def _split3(v):
    hi = v.astype(BF16)
    r1 = v - hi.astype(F32)
    mid = r1.astype(BF16)
    lo = (r1 - mid.astype(F32)).astype(BF16)
    return hi, mid, lo


def _dot3(v, mat):
    hi, mid, lo = _split3(v)
    out = lax.dot_general(hi, mat, _NN, preferred_element_type=F32)
    out += lax.dot_general(mid, mat, _NN, preferred_element_type=F32)
    out += lax.dot_general(lo, mat, _NN, preferred_element_type=F32)
    return out


def _forget_fwd(flog_t, bias, name):
    h, s = flog_t.shape
    blk = _tile(s, 512, LANES)
    tri = (jnp.arange(blk)[:, None] <= jnp.arange(blk)[None, :]).astype(BF16)

    def body(z_ref, b_ref, tri_ref, f_ref, carry):
        @pl.when(pl.program_id(0) == 0)
        def _():
            carry[...] = jnp.zeros_like(carry)

        z = z_ref[...] + b_ref[...]
        e = jnp.exp(-jnp.abs(z))
        w = 1.0 + e
        log1p_e = jnp.where(w == 1.0, e, jnp.log(w) * (e / (w - 1.0)))
        lf = jnp.minimum(z, 0.0) - log1p_e
        out = carry[...] + _dot3(lf, tri_ref[...])
        for j, piece in enumerate(_split3(out)):
            f_ref[j] = piece
        carry[...] = out[:, blk - 1:blk]

    zs = pl.BlockSpec((h, blk), lambda i: (0, i))
    return _pc(body, out_shape=_sds((3, h, s), BF16), grid=(s // blk,),
               in_specs=[zs, pl.BlockSpec((h, 1), lambda i: (0, 0)), pl.BlockSpec((blk, blk), lambda i: (0, 0))],
               out_specs=pl.BlockSpec((3, h, blk), lambda i: (0, 0, i)), scratch_shapes=[pltpu.VMEM((h, 1), F32)],
               compiler_params=_params("arbitrary"), name=name)(flog_t, bias, tri)


def _forget_bwd(df_t, flog_t, bias, name):
    h, s = flog_t.shape
    blk = _tile(s, 512, LANES)
    nb = s // blk
    tri = (jnp.arange(blk)[:, None] >= jnp.arange(blk)[None, :]).astype(BF16)

    def body(df_ref, z_ref, b_ref, tri_ref, dz_ref, db_ref, carry):
        @pl.when(pl.program_id(0) == 0)
        def _():
            carry[...] = jnp.zeros_like(carry)
            db_ref[...] = jnp.zeros_like(db_ref)

        rc = carry[...] + _dot3(df_ref[...], tri_ref[...])
        carry[...] = rc[:, 0:1]
        dz = rc * jax.nn.sigmoid(-(z_ref[...] + b_ref[...]))
        dz_ref[...] = dz
        db_ref[...] += jnp.sum(dz, axis=-1, keepdims=True)

    rev = pl.BlockSpec((h, blk), lambda i: (0, nb - 1 - i))
    col = pl.BlockSpec((h, 1), lambda i: (0, 0))
    return _pc(body, out_shape=(_sds((h, s), F32), _sds((h, 1), F32)), grid=(nb,),
               in_specs=[rev, rev, col, pl.BlockSpec((blk, blk), lambda i: (0, 0))],
               out_specs=(rev, col), scratch_shapes=[pltpu.VMEM((h, 1), F32)],
               compiler_params=_params("arbitrary"), name=name)(df_t, flog_t, bias, tri)


def _attn_tiles(s):
    return _tile(s, 1024, LANES)


def _attn_half(t):
    return t // 2 if t >= 4 * LANES else t


BIAS_ROWS = 16


def _attn_prep(qkv, f_pieces, name):
    s = qkv.shape[0]
    a_w = qkv.shape[1] // 3
    npair = a_w // LANES
    t = _attn_tiles(s)
    scale = 1.0 / math.sqrt(HEAD_DIM)

    six = f_pieces[:, :2 * npair].reshape(3, npair, 2, s).transpose(1, 3, 2, 0).reshape(npair, s, 6)
    feat = jnp.concatenate([six, jnp.ones((npair, s, 1), BF16), jnp.zeros((npair, s, BIAS_ROWS - 7), BF16)], axis=-1)
    place_q = [[0.0] * (2 * LANES) for _ in range(BIAS_ROWS)]
    place_k = [[0.0] * (2 * LANES) for _ in range(BIAS_ROWS)]
    for hh in range(2):
        b0 = hh * LANES + (HEAD_DIM if hh == 0 else 0)
        for j in range(3):
            place_q[3 * hh + j][b0 + j] = 1.0
            place_q[6][b0 + 3 + j] = 1.0
            place_k[6][b0 + j] = 1.0
            place_k[3 * hh + j][b0 + 3 + j] = -1.0
    place_q = jnp.array(place_q, BF16)
    place_k = jnp.array(place_k, BF16)

    def body(q_ref, k_ref, v_ref, f_ref, pq_ref, pk_ref, qa_ref, ka_ref, va_ref):
        lane = lax.broadcasted_iota(jnp.int32, (1, LANES), 1)
        q2 = (q_ref[...].astype(F32) * scale).astype(BF16)
        k2, v2 = k_ref[...], v_ref[...]
        qx = lax.dot_general(f_ref[0], pq_ref[...], _NN, preferred_element_type=F32).astype(BF16)
        kx = lax.dot_general(f_ref[0], pk_ref[...], _NN, preferred_element_type=F32).astype(BF16)
        for hh in range(2):
            real = (lane < HEAD_DIM) if hh == 0 else (lane >= HEAD_DIM)
            cols = slice(hh * LANES, (hh + 1) * LANES)
            qa_ref[:, cols] = jnp.where(real, q2, qx[:, cols])
            ka_ref[:, cols] = jnp.where(real, k2, kx[:, cols])
            va_ref[:, cols] = jnp.where(real, v2, jnp.zeros_like(v2))

    def col(off):
        return pl.BlockSpec((t, LANES), lambda p, i: (i, off + p))

    out = pl.BlockSpec((t, 2 * LANES), lambda p, i: (i, p))
    place = pl.BlockSpec((BIAS_ROWS, 2 * LANES), lambda p, i: (0, 0))
    return _pc(body, out_shape=(_sds((s, 2 * a_w), BF16),) * 3, grid=(npair, s // t),
               in_specs=[col(0), col(npair), col(2 * npair), pl.BlockSpec((1, t, BIAS_ROWS), lambda p, i: (p, i, 0)),
                         place, place],
               out_specs=(out, out, out), compiler_params=_params("parallel", "parallel"), name=name)(
                   qkv, qkv, qkv, feat, place_q, place_k)


def _attn_fwd(qa, ka, va, name, exchange=None):
    s = qa.shape[0]
    a_w = qa.shape[1] // 2
    npair = a_w // LANES
    t = _attn_tiles(s)
    nq = s // t
    half = _attn_half(t)

    def body(q_ref, k_ref, v_ref, o_ref, lse_ref, m_sc, l_sc, acc_sc):
        qi = pl.program_id(1)
        first = lax.broadcasted_iota(jnp.int32, (1, LANES), 1) < HEAD_DIM
        m_sc[...] = jnp.full_like(m_sc, NEG_BIG)
        l_sc[...] = jnp.zeros_like(l_sc)
        acc_sc[...] = jnp.zeros_like(acc_sc)

        def step(q0, k_start, size, diag):
            q_sl = slice(q0, q0 + size)
            k_rows = pl.ds(pl.multiple_of(k_start, size), size)
            m_old = m_sc[q_sl, :]
            keep = None
            if diag:
                keep = (lax.broadcasted_iota(jnp.int32, (size, size), 0) >= lax.broadcasted_iota(jnp.int32, (size, size), 1))
            m_new, rs, pv = [], [], []
            for hh in range(2):
                cols = slice(hh * LANES, (hh + 1) * LANES)
                sc = lax.dot_general(q_ref[q_sl, cols], k_ref[k_rows, cols], _NT, preferred_element_type=F32)
                if diag:
                    sc = jnp.where(keep, sc, NEG_BIG)
                mo = m_old[:, hh * HEAD_DIM:hh * HEAD_DIM + 1]
                mn = jnp.maximum(mo, jnp.max(sc, axis=1, keepdims=True))
                p = jnp.exp(sc - mn)
                m_new.append(mn)
                rs.append(jnp.sum(p, axis=1, keepdims=True))
                pv.append(lax.dot_general(p.astype(BF16), v_ref[k_rows, cols], _NN, preferred_element_type=F32))
            m2 = jnp.where(first, m_new[0], m_new[1])
            alpha = jnp.exp(m_old - m2)
            m_sc[q_sl, :] = m2
            l_sc[q_sl, :] = alpha * l_sc[q_sl, :] + jnp.where(first, rs[0], rs[1])
            acc_sc[q_sl, :] = alpha * acc_sc[q_sl, :] + pv[0] + pv[1]

        def below_diagonal(ki, carry):
            step(0, ki * t, t, False)
            return carry

        lax.fori_loop(0, qi, below_diagonal, 0)
        step(0, qi * t, half, True)
        if half < t:
            step(half, qi * t, half, False)
            step(half, qi * t + half, half, True)
        l2 = l_sc[...]
        o_ref[...] = acc_sc[...] / l2
        lse_ref[...] = m_sc[...] + jnp.log(l2)

    qs = pl.BlockSpec((t, 2 * LANES), lambda p, qi: (qi, p))
    ks = pl.BlockSpec((s, 2 * LANES), lambda p, qi: (0, p))
    os_ = pl.BlockSpec((t, LANES), lambda p, qi: (qi, p))
    return _pc(body, exchange, out_shape=(_sds((s, a_w), F32), _sds((s, a_w), F32)), grid=(npair, nq),
               in_specs=[qs, ks, ks], out_specs=(os_, os_),
               scratch_shapes=[pltpu.VMEM((t, LANES), F32)] * 3,
               compiler_params=_params("parallel", "arbitrary"), name=name)(qa, ka, va)


def _attn_bwd(qa, ka, va, do, o, lse, name, exchange=None):
    s = qa.shape[0]
    a_w = qa.shape[1] // 2
    npair = a_w // LANES
    t = _attn_tiles(s)
    nq = s // t
    half = _attn_half(t)
    scale = 1.0 / math.sqrt(HEAD_DIM)

    def body(q_ref, k_ref, v_ref, do_ref, o_ref, lse_ref, dq_ref, dk_ref, dv_ref, qx_ref, kx_ref, dk_sc, dv_sc, kx_sc):
        ki = pl.program_id(1)
        first = lax.broadcasted_iota(jnp.int32, (1, LANES), 1) < HEAD_DIM

        @pl.when(ki == 0)
        def _():
            dq_ref[...] = jnp.zeros_like(dq_ref)
            qx_ref[...] = jnp.zeros_like(qx_ref)

        def step(q_start, k0, size, diag, assign):
            rows = pl.ds(pl.multiple_of(q_start, size), size)
            k_sl = slice(k0, k0 + size)
            do2 = do_ref[rows, :]
            lse2 = lse_ref[rows, :]
            dd = do2.astype(F32) * o_ref[rows, :]
            keep = None
            if diag:
                keep = (lax.broadcasted_iota(jnp.int32, (size, size), 0) >= lax.broadcasted_iota(jnp.int32, (size, size), 1))
            dq_h, dk_h, dv_h = [], [], []
            for hh in range(2):
                sel = first if hh == 0 else jnp.logical_not(first)
                cols = slice(hh * LANES, (hh + 1) * LANES)
                qh, kh, vh = q_ref[rows, cols], k_ref[k_sl, cols], v_ref[k_sl, cols]
                delta = jnp.sum(jnp.where(sel, dd, 0.0), axis=1, keepdims=True)
                sc = lax.dot_general(qh, kh, _NT, preferred_element_type=F32)
                if diag:
                    sc = jnp.where(keep, sc, NEG_BIG)
                p = jnp.exp(sc - lse2[:, hh * HEAD_DIM:hh * HEAD_DIM + 1])
                dp = lax.dot_general(do2, vh, _NT, preferred_element_type=F32)
                ds_b = (p * (dp - delta)).astype(BF16)
                dv_h.append(lax.dot_general(p.astype(BF16), do2, _TN, preferred_element_type=F32))
                dk_h.append(lax.dot_general(ds_b, qh, _TN, preferred_element_type=F32))
                dq_h.append(lax.dot_general(ds_b, kh, _NN, preferred_element_type=F32))
            dq_ref[rows, :] += jnp.where(first, dq_h[0], dq_h[1]) * scale
            qx_ref[rows, :] += jnp.where(first, dq_h[1], dq_h[0])
            dk_new = jnp.where(first, dk_h[0], dk_h[1])
            kx_new = jnp.where(first, dk_h[1], dk_h[0])
            dv_new = jnp.where(first, dv_h[0], dv_h[1])
            if assign:
                dk_sc[k_sl, :] = dk_new
                kx_sc[k_sl, :] = kx_new
                dv_sc[k_sl, :] = dv_new
            else:
                dk_sc[k_sl, :] += dk_new
                kx_sc[k_sl, :] += kx_new
                dv_sc[k_sl, :] += dv_new

        def below_diagonal(qi, carry):
            step(qi * t, 0, t, False, False)
            return carry

        step(ki * t, 0, half, True, True)
        if half < t:
            step(ki * t + half, 0, half, False, False)
            step(ki * t + half, half, half, True, True)
        lax.fori_loop(ki + 1, nq, below_diagonal, 0)
        dk_ref[...] = dk_sc[...].astype(BF16)
        dv_ref[...] = dv_sc[...].astype(BF16)
        kx_ref[...] = kx_sc[...]

    ks2 = pl.BlockSpec((t, 2 * LANES), lambda p, ki: (ki, p))
    qs2 = pl.BlockSpec((s, 2 * LANES), lambda p, ki: (0, p))
    whole = pl.BlockSpec((s, LANES), lambda p, ki: (0, p))
    kout = pl.BlockSpec((t, LANES), lambda p, ki: (ki, p))
    return _pc(body, exchange,
               out_shape=(_sds((s, a_w), F32), _sds((s, a_w), BF16), _sds((s, a_w), BF16), _sds((s, a_w), F32),
                          _sds((s, a_w), F32)),
               grid=(npair, nq), in_specs=[qs2, ks2, ks2, whole, whole, whole],
               out_specs=(whole, kout, kout, whole, kout),
               scratch_shapes=[pltpu.VMEM((t, LANES), F32)] * 3,
               compiler_params=_params("parallel", "arbitrary"), name=name)(qa, ka, va, do, o, lse)

def _decay_grads(qx, kx, name):
    s, a_w = qx.shape
    n_heads = a_w // HEAD_DIM
    tr = _tile(s, 512, 8)
    pick_q = [[0.0] * LANES for _ in range(a_w)]
    pick_k = [[0.0] * LANES for _ in range(a_w)]
    for h in range(n_heads):
        b0 = (h // 2) * LANES + (HEAD_DIM if h % 2 == 0 else 0)
        pick_q[b0][h] = 1.0
        pick_k[b0 + 3][h] = 1.0
    pick_q = jnp.array(pick_q, BF16)
    pick_k = jnp.array(pick_k, BF16)

    def body(qx_ref, kx_ref, pq_ref, pk_ref, o_ref):
        o_ref[...] = _dot3(qx_ref[...], pq_ref[...]) - _dot3(kx_ref[...], pk_ref[...])

    row = pl.BlockSpec((tr, a_w), lambda i: (i, 0))
    pick = pl.BlockSpec((a_w, LANES), lambda i: (0, 0))
    return _pc(body, out_shape=_sds((s, LANES), F32), grid=(s // tr,), in_specs=[row, row, pick, pick],
               out_specs=pl.BlockSpec((tr, LANES), lambda i: (i, 0)),
               compiler_params=_params("parallel"), name=name)(qx, kx, pick_q, pick_k)


def _shift_down(z, k, rows):
    return jnp.where(rows >= k, pltpu.roll(z, k, 0), 0.0)


def _shift_up(z, k, rows, n):
    return jnp.where(rows < n - k, pltpu.roll(z, n - k, 0), 0.0)


def _conv_fwd(bcx, conv_w, name):
    s = bcx.shape[0]
    cw = bcx.shape[1] // 3
    nb = cw // LANES

    def body(b_ref, c_ref, x_ref, w_ref, cv_ref):
        rows = lax.broadcasted_iota(jnp.int32, (s, LANES), 0)
        z = c_ref[...] * x_ref[...]
        w = w_ref[...]
        y = w[2:3, :] * z + w[1:2, :] * _shift_down(z, 1, rows) + w[0:1, :] * _shift_down(z, 2, rows)
        cv_ref[...] = b_ref[...] * y

    def col(off):
        return pl.BlockSpec((s, LANES), lambda j: (0, j + off))

    return _pc(body, out_shape=_sds((s, cw), F32), grid=(nb,),
               in_specs=[col(0), col(nb), col(2 * nb), pl.BlockSpec((CONV_K, LANES), lambda j: (0, j))],
               out_specs=col(0), compiler_params=_params("parallel"), name=name)(bcx, bcx, bcx, conv_w)


def _conv_bwd(dcv, bcx, conv_w, name):
    s = bcx.shape[0]
    cw = bcx.shape[1] // 3
    nb = cw // LANES

    def body(dcv_ref, b_ref, c_ref, x_ref, w_ref, db_ref, dc_ref, dxc_ref, dw_ref):
        rows = lax.broadcasted_iota(jnp.int32, (s, LANES), 0)
        cv_, xv = c_ref[...], x_ref[...]
        z = cv_ * xv
        w = w_ref[...]
        z1 = _shift_down(z, 1, rows)
        z2 = _shift_down(z, 2, rows)
        y = w[2:3, :] * z + w[1:2, :] * z1 + w[0:1, :] * z2
        dcvv = dcv_ref[...]
        db_ref[...] = (dcvv * y).astype(BF16)
        dy = dcvv * b_ref[...]
        dw_ref[0:1, :] = jnp.sum(dy * z2, axis=0, keepdims=True)
        dw_ref[1:2, :] = jnp.sum(dy * z1, axis=0, keepdims=True)
        dw_ref[2:3, :] = jnp.sum(dy * z, axis=0, keepdims=True)
        dz = w[2:3, :] * dy + w[1:2, :] * _shift_up(dy, 1, rows, s) + w[0:1, :] * _shift_up(dy, 2, rows, s)
        dc_ref[...] = (dz * xv).astype(BF16)
        dxc_ref[...] = (dz * cv_).astype(BF16)

    def col(off):
        return pl.BlockSpec((s, LANES), lambda j: (0, j + off))

    wspec = pl.BlockSpec((CONV_K, LANES), lambda j: (0, j))
    db, dc, dxc, dw = _pc(body, out_shape=(_sds((s, cw), BF16),) * 3 + (_sds((CONV_K, cw), F32),), grid=(nb,),
                          in_specs=[col(0), col(0), col(nb), col(2 * nb), wspec],
                          out_specs=(col(0), col(0), col(0), wspec),
                          compiler_params=_params("parallel"), name=name)(dcv, bcx, bcx, bcx, conv_w)
    return db, dc, dxc, dw


def _group_matrix():
    idx = jnp.arange(LANES) // HEAD_DIM
    return (idx[:, None] == idx[None, :]).astype(BF16)


def _group_sum(v, gmat):
    return _dot3(v, gmat)


def _gnorm_fwd(att, cv, gg, name):
    s, a_w = att.shape
    cw = cv.shape[1]
    d = a_w + cw
    tr = _tile(s, 512, 16)
    gmat = _group_matrix()

    def body(att_ref, cv_ref, gg_ref, gm_ref, yn_ref):
        gm = gm_ref[...]
        for c0 in range(0, d, LANES):
            y = att_ref[:, c0:c0 + LANES] if c0 < a_w else cv_ref[:, c0 - a_w:c0 - a_w + LANES]
            ms = _group_sum(y * y, gm) * (1.0 / HEAD_DIM)
            yn_ref[:, c0:c0 + LANES] = (y * lax.rsqrt(ms + EPS) * gg_ref[:, c0:c0 + LANES]).astype(BF16)

    return _pc(body, out_shape=_sds((s, d), BF16), grid=(s // tr,),
               in_specs=[pl.BlockSpec((tr, a_w), lambda i: (i, 0)), pl.BlockSpec((tr, cw), lambda i: (i, 0)),
                         _vec_spec(d), pl.BlockSpec((LANES, LANES), lambda i: (0, 0))],
               out_specs=pl.BlockSpec((tr, d), lambda i: (i, 0)),
               compiler_params=_params("parallel"), name=name)(att, cv, gg, gmat)


def _gnorm_bwd(dyn, att, cv, gg, name):
    s, a_w = att.shape
    cw = cv.shape[1]
    d = a_w + cw
    tr = _tile(s, 256, 16)
    gmat = _group_matrix()

    def body(dyn_ref, att_ref, cv_ref, gg_ref, gm_ref, datt_ref, dcv_ref, dgg_ref):
        @pl.when(pl.program_id(0) == 0)
        def _():
            dgg_ref[...] = jnp.zeros_like(dgg_ref)

        gm = gm_ref[...]
        for c0 in range(0, d, LANES):
            y = att_ref[:, c0:c0 + LANES] if c0 < a_w else cv_ref[:, c0 - a_w:c0 - a_w + LANES]
            dv = dyn_ref[:, c0:c0 + LANES]
            r = lax.rsqrt(_group_sum(y * y, gm) * (1.0 / HEAD_DIM) + EPS)
            xhat = y * r
            dgg_ref[:, c0:c0 + LANES] += jnp.sum(dv * xhat, axis=0, keepdims=True)
            dxh = dv * gg_ref[:, c0:c0 + LANES]
            proj = _group_sum(dxh * xhat, gm) * (1.0 / HEAD_DIM)
            dy = r * (dxh - xhat * proj)
            if c0 < a_w:
                datt_ref[:, c0:c0 + LANES] = dy.astype(BF16)
            else:
                dcv_ref[:, c0 - a_w:c0 - a_w + LANES] = dy

    return _pc(body, out_shape=(_sds((s, a_w), BF16), _sds((s, cw), F32), _sds((1, d), F32)), grid=(s // tr,),
               in_specs=[pl.BlockSpec((tr, d), lambda i: (i, 0)), pl.BlockSpec((tr, a_w), lambda i: (i, 0)),
                         pl.BlockSpec((tr, cw), lambda i: (i, 0)), _vec_spec(d),
                         pl.BlockSpec((LANES, LANES), lambda i: (0, 0))],
               out_specs=(pl.BlockSpec((tr, a_w), lambda i: (i, 0)), pl.BlockSpec((tr, cw), lambda i: (i, 0)),
                          _vec_spec(d)),
               compiler_params=_params("arbitrary"), name=name)(dyn, att, cv, gg, gmat)


def _adamw_math(w, g, m, v):
    m_new = ADAM_B1 * m + (1.0 - ADAM_B1) * g
    v_new = ADAM_B2 * v + (1.0 - ADAM_B2) * (g * g)
    m_hat = m_new / (1.0 - ADAM_B1 ** ADAM_STEP)
    v_hat = v_new / (1.0 - ADAM_B2 ** ADAM_STEP)
    delta = -ADAM_LR * (m_hat / (jnp.sqrt(v_hat) + ADAM_EPS) + ADAM_WD * w)
    return delta, m_new, v_new


def _row_tile(r, c):
    return _tile(r, max(8, ((1 << 19) // c) // 8 * 8), 8)


def _adamw(w, g, m, v, name):
    r, c = w.shape
    tr = _row_tile(r, c)

    def body(w_ref, g_ref, m_ref, v_ref, d_ref, mo_ref, vo_ref):
        d, mn, vn = _adamw_math(w_ref[...], g_ref[...], m_ref[...], v_ref[...])
        d_ref[...] = d
        mo_ref[...] = mn
        vo_ref[...] = vn

    spec = pl.BlockSpec((tr, c), lambda i: (i, 0))
    return _pc(body, out_shape=(_sds((r, c), F32),) * 3, grid=(r // tr,), in_specs=[spec] * 4,
               out_specs=(spec,) * 3, compiler_params=_params("parallel"), name=name)(w, g, m, v)


def _adamw_halves(w, mine, theirs, m, v, core, name):
    r2, c = w.shape
    r = r2 // 2
    assert mine.shape == (r, c) and theirs.shape == (r, c)
    tr = _row_tile(r, c)
    nb = r // tr

    def body(core_ref, w_ref, a_ref, b_ref, m_ref, v_ref, g_ref, d_ref, mo_ref, vo_ref):
        g = jnp.where(pl.program_id(0) == core_ref[0], a_ref[...], b_ref[...])
        d, mn, vn = _adamw_math(w_ref[...], g, m_ref[...], v_ref[...])
        g_ref[...] = g
        d_ref[...] = d
        mo_ref[...] = mn
        vo_ref[...] = vn

    full = pl.BlockSpec((tr, c), lambda h, i, core_ref: (h * nb + i, 0))
    half = pl.BlockSpec((tr, c), lambda h, i, core_ref: (i, 0))
    grid_spec = pltpu.PrefetchScalarGridSpec(
        num_scalar_prefetch=1, grid=(2, nb), in_specs=[full, half, half, full, full], out_specs=(full,) * 4)
    return _pc(body, out_shape=(_sds((r2, c), F32),) * 4, grid_spec=grid_spec,
               compiler_params=_params("parallel", "parallel"), name=name)(core, w, mine, theirs, m, v)


def _ada_fwd(c16, ada_w, ada_b, name):
    d, n = ada_w.shape
    tn = _tile(n, 768, LANES)

    def body(c_ref, w_ref, b_ref, o_ref):
        cv = c_ref[...]
        sc = (cv * jax.nn.sigmoid(cv)).astype(BF16)
        o_ref[...] = lax.dot_general(sc, w_ref[...].astype(BF16), _NN, preferred_element_type=F32) + b_ref[...]

    return _pc(body, out_shape=_sds((16, n), F32), grid=(n // tn,),
               in_specs=[pl.BlockSpec((16, d), lambda j: (0, 0)), pl.BlockSpec((d, tn), lambda j: (0, j)),
                         pl.BlockSpec((1, tn), lambda j: (0, j))],
               out_specs=pl.BlockSpec((16, tn), lambda j: (0, j)),
               compiler_params=_params("parallel"), name=name)(c16, ada_w, ada_b)


def _ada_update(c16_t, dmod16, w, m, v, name, exchange=None):
    r, c = w.shape
    tr = _row_tile(r, c)

    def body(c_ref, dm_ref, w_ref, m_ref, v_ref, g_ref, d_ref, mo_ref, vo_ref):
        cv = c_ref[...]
        sc = (cv * jax.nn.sigmoid(cv)).astype(BF16)
        g = lax.dot_general(sc, dm_ref[...].astype(BF16), _NN, preferred_element_type=F32)
        d, mn, vn = _adamw_math(w_ref[...], g, m_ref[...], v_ref[...])
        g_ref[...] = g
        d_ref[...] = d
        mo_ref[...] = mn
        vo_ref[...] = vn

    spec = pl.BlockSpec((tr, c), lambda i: (i, 0))
    return _pc(body, exchange, out_shape=(_sds((r, c), F32),) * 4, grid=(r // tr,),
               in_specs=[pl.BlockSpec((tr, 16), lambda i: (i, 0)), pl.BlockSpec((16, c), lambda i: (0, 0)),
                         spec, spec, spec],
               out_specs=(spec,) * 4, compiler_params=_params("parallel"), name=name)(c16_t, dmod16, w, m, v)


def _add_half(dw, recv, core, name):
    _, _, r, w = dw.shape
    tr = _tile(r, 512, 16)

    def body(core_ref, a_ref, b_ref, o_ref):
        o_ref[...] = (a_ref[...].astype(F32) + b_ref[...].astype(F32)).astype(BF16)

    grid_spec = pltpu.PrefetchScalarGridSpec(
        num_scalar_prefetch=1, grid=(N_CHIPS, r // tr),
        in_specs=[pl.BlockSpec((None, None, tr, w), lambda s, i, core_ref: (s, core_ref[0], i, 0)),
                  pl.BlockSpec((None, tr, w), lambda s, i, core_ref: (s, i, 0))],
        out_specs=pl.BlockSpec((None, tr, w), lambda s, i, core_ref: (s, i, 0)))
    return _pc(body, out_shape=_sds((N_CHIPS, r, w), BF16), grid_spec=grid_spec,
               compiler_params=_params("parallel", "parallel"), name=name)(core, dw, recv)


def _sum_chips(own, recv, chip, name):
    _, r, w = own.shape
    tr = _tile(r, 512, 16)

    def body(chip_ref, own_ref, p_ref, o_ref):
        acc = own_ref[...].astype(F32)
        for q in range(N_CHIPS - 1):
            acc = acc + p_ref[q].astype(F32)
        o_ref[...] = acc

    grid_spec = pltpu.PrefetchScalarGridSpec(
        num_scalar_prefetch=1, grid=(r // tr,),
        in_specs=[pl.BlockSpec((None, tr, w), lambda i, chip_ref: (chip_ref[0], i, 0)),
                  pl.BlockSpec((N_CHIPS - 1, tr, w), lambda i, chip_ref: (0, i, 0))],
        out_specs=pl.BlockSpec((tr, w), lambda i, chip_ref: (i, 0)))
    return _pc(body, out_shape=_sds((r, w), F32), grid_spec=grid_spec,
               compiler_params=_params("parallel"), name=name)(chip, own, recv)


def _sum_devices(parts, name):
    nd, r, w = parts.shape

    def body(p_ref, o_ref):
        acc = p_ref[0]
        for q in range(1, nd):
            acc = acc + p_ref[q]
        o_ref[...] = acc

    return _pc(body, out_shape=_sds((r, w), F32), name=name)(parts)


def _place():
    x, y, c = lax.axis_index("x"), lax.axis_index("y"), lax.axis_index("c")
    chips = [(1 - x, y), (x, 1 - y), (1 - x, 1 - y)]
    return x, y, c, chips


def _small_gather_exchange(blk):
    r, w = blk.shape

    def copies(src, dst, send_sems, recv_sems):
        x, y, c, chips = _place()
        me, sibling = (x, y, c), (x, y, 1 - c)

        def rows(px, py, pc):
            return dst[0].at[pl.ds((4 * px + 2 * py + pc) * r, r), :]

        def copy(k, block, to, own=False):
            return _remote(src[0] if own else rows(*block), rows(*block), send_sems, recv_sems, k, to)

        mine = pltpu.make_async_copy(src[0], rows(*me), send_sems.at[7])
        first = [copy(0, me, sibling, own=True)] + [copy(1 + j, me, (*chip, c), own=True) for j, chip in enumerate(chips)]
        passed = [copy(4 + j, (*chip, c), sibling) for j, chip in enumerate(chips)]
        landed = [copy(1 + j, (*chip, c), me) for j, chip in enumerate(chips)]
        from_sibling = [copy(0, sibling, me)] + [copy(4 + j, (*chip, 1 - c), me) for j, chip in enumerate(chips)]
        return mine, first, passed, landed, from_sibling

    def start(src, dst, send_sems, recv_sems):
        mine, first, _, _, _ = copies(src, dst, send_sems, recv_sems)
        mine.start()
        for cp in first:
            cp.start()

    def finish(src, dst, send_sems, recv_sems):
        mine, first, passed, landed, from_sibling = copies(src, dst, send_sems, recv_sems)
        for arrival, onward in zip(landed, passed):
            arrival.wait_recv()
            onward.start()
        for cp in from_sibling:
            cp.wait_recv()
        for cp in first + passed:
            cp.wait_send()
        mine.wait()

    return _Exchange([blk], [_sds((N_DEV * r, w), blk.dtype)], 8, start, finish)


def _remote(src, dst, send_sems, recv_sems, k, to):
    return pltpu.make_async_remote_copy(src_ref=src, dst_ref=dst, send_sem=send_sems.at[k], recv_sem=recv_sems.at[k],
                                        device_id=to, device_id_type=MESH)


def _exchange_of(inputs, out_shapes, n_sems, copies, aliases=None):
    def start(src, dst, send_sems, recv_sems):
        for cp in copies(src, dst, send_sems, recv_sems)[0]:
            cp.start()

    def finish(src, dst, send_sems, recv_sems):
        sends, arrivals = copies(src, dst, send_sems, recv_sems)
        for cp in arrivals:
            cp.wait_recv()
        for cp in sends:
            cp.wait_send()

    return _Exchange(inputs, out_shapes, n_sems, start, finish, aliases)


def _run_exchange(ex, name):
    n_in, n_out = len(ex.inputs), len(ex.out_shapes)

    def body(*refs):
        src, dst = refs[:n_in], refs[n_in:n_in + n_out]
        send_sems, recv_sems = refs[n_in + n_out:]
        ex.start(src, dst, send_sems, recv_sems)
        ex.finish(src, dst, send_sems, recv_sems)

    ex.set_results(pl.pallas_call(
        body, out_shape=tuple(ex.out_shapes), in_specs=[_ANY] * n_in, out_specs=(_ANY,) * n_out,
        scratch_shapes=[pltpu.SemaphoreType.DMA((ex.n_sems,)), pltpu.SemaphoreType.DMA((ex.n_sems,))],
        input_output_aliases=ex.aliases, name=name)(*ex.inputs))


def _gather_ici_exchange(shards):
    n = len(shards)

    def copies(own, out, send_sems, recv_sems):
        x, y, c, chips = _place()
        my_chip = 2 * x + y
        sends, arrivals = [], []
        for i in range(n):
            for j, chip in enumerate(chips):
                to = (*chip, c)
                sends.append(_remote(own[i].at[c], out[i].at[my_chip, c], send_sems, recv_sems, 4 * i + j, to))
                arrivals.append(_remote(own[i].at[c], out[i].at[2 * chip[0] + chip[1], c], send_sems, recv_sems, 4 * i + j, to))
            whole = _remote(own[i], out[i].at[my_chip], send_sems, recv_sems, 4 * i + 3, (x, y, 1 - c))
            sends.append(whole)
            arrivals.append(whole)
        return sends, arrivals

    return _exchange_of(shards, [_sds((N_CHIPS,) + s.shape, s.dtype) for s in shards], 4 * n, copies)


def _gather_pass_exchange(gathered):
    n = len(gathered)

    def copies(src, dst, send_sems, recv_sems):
        x, y, c, chips = _place()
        sends, arrivals = [], []
        for i in range(n):
            for j, chip in enumerate(chips):
                idx = 2 * chip[0] + chip[1]
                sends.append(_remote(src[i].at[idx, c], dst[i].at[idx, c], send_sems, recv_sems, 3 * i + j, (x, y, 1 - c)))
                arrivals.append(_remote(src[i].at[idx, c], dst[i].at[idx, 1 - c], send_sems, recv_sems, 3 * i + j, (x, y, 1 - c)))
        return sends, arrivals

    return _exchange_of(gathered, [_sds(g.shape, g.dtype) for g in gathered], 3 * n, copies,
                        aliases={i: i for i in range(n)})


def _reduce_sibling_exchange(grads):
    n = len(grads)

    def copies(src, dst, send_sems, recv_sems):
        x, y, c, _ = _place()
        both = [_remote(src[i].at[s, 1 - c], dst[i].at[s], send_sems, recv_sems, N_CHIPS * i + s, (x, y, 1 - c))
                for i in range(n) for s in range(N_CHIPS)]
        return both, both

    return _exchange_of(grads, [_sds((N_CHIPS,) + g.shape[2:], g.dtype) for g in grads], N_CHIPS * n, copies)


def _reduce_chips_exchange(parts):
    n = len(parts)

    def copies(src, dst, send_sems, recv_sems):
        x, y, c, chips = _place()
        both = [_remote(src[i].at[2 * chip[0] + chip[1]], dst[i].at[j], send_sems, recv_sems, 3 * i + j, (*chip, c))
                for i in range(n) for j, chip in enumerate(chips)]
        return both, both

    return _exchange_of(parts, [_sds((N_CHIPS - 1,) + p.shape[1:], p.dtype) for p in parts], 3 * n, copies)


def _share_exchange(halves):
    n = len(halves)

    def copies(src, dst, send_sems, recv_sems):
        x, y, c, _ = _place()
        both = [_remote(src[i], dst[i], send_sems, recv_sems, i, (x, y, 1 - c)) for i in range(n)]
        return both, both

    return _exchange_of(halves, [_sds(h.shape, h.dtype) for h in halves], n, copies)


HEAD_ROWS = 16


class _WeightTraffic:
    def __init__(self, shards, core, chip):
        self.shards, self.core, self.chip = shards, core, chip
        self.gather, self.grads, self.reduce, self.chip_sums, self.half_sums, self.shared = {}, {}, {}, {}, {}, {}

    def gather_ici(self, grp):
        self.gather[grp] = _gather_ici_exchange(self.shards[grp])
        return self.gather[grp]

    def gather_pass(self, grp):
        self.gather[grp] = _gather_pass_exchange(self.gather[grp].results)
        return self.gather[grp]

    def weights(self, grp):
        return [g.reshape(-1, g.shape[-1]) for g in self.gather[grp].results]

    def reduce_sibling(self, grp, grads):
        self.grads[grp] = [g.reshape(N_CHIPS, 2, g.shape[0] // (2 * N_CHIPS), g.shape[1]) for g in grads]
        self.reduce[grp] = _reduce_sibling_exchange(self.grads[grp])
        return self.reduce[grp]

    def add_halves(self, grp):
        self.chip_sums[grp] = [_add_half(g, r, self.core, "add_half_%s%d" % (grp, i))
                               for i, (g, r) in enumerate(zip(self.grads[grp], self.reduce[grp].results))]

    def reduce_chips(self, grp):
        self.reduce[grp] = _reduce_chips_exchange(self.chip_sums[grp])
        return self.reduce[grp]

    def sum_chips(self, grp):
        self.half_sums[grp] = [_sum_chips(o, p, self.chip, "sum_chips_%s%d" % (grp, i))
                               for i, (o, p) in enumerate(zip(self.chip_sums[grp], self.reduce[grp].results))]

    def share(self, grp):
        self.shared[grp] = _share_exchange(self.half_sums[grp])
        return self.shared[grp]

    def totals(self, grp):
        return list(zip(self.half_sums[grp], self.shared[grp].results))


def _ffn_fwd(x, norm_g, shift, scale, gate, wg_t, wu_t, wd, tag, next_norm, up_exchange=None, down_exchange=None):
    h = _norm_mod_fwd(x, norm_g, shift, scale, tag + "_norm_fwd")
    a, u, hid = _ffn_up(h, wg_t, wu_t, tag + "_up", exchange=up_exchange)
    wd = wd() if callable(wd) else wd
    x_out, f, h_next = _mm(hid, wd, "nn", F32, tag + "_down", res=x, gate=gate, aux_dtype=BF16, norm=next_norm,
                           exchange=down_exchange() if down_exchange else None)
    return x_out, (h, a, u, hid, f), h_next


def _ffn_bwd(dx_out, df, x, saved, norm_g, scale, wg_t, wu_t, wd, tag, traffic, below=None, dact_exchange=None,
             dw_exchange=None, finish_reduction=False):
    h, a, u, hid, _ = saved
    f_below, gate_below = below if below else (None, None)
    da, du = _ffn_dact(df, wd, a, u, tag + "_dact", exchange=dact_exchange)
    dwd = _mm(hid, df, "tn", BF16, tag + "_dwd", exchange=dw_exchange() if dw_exchange else None)
    if not finish_reduction:
        dwg_t, dwu_t = _weight_grads([da, du], h, tag + "_dwg_dwu")
        dx, dshift, dscale, dnorm_g, *gated = _norm_mod_bwd(
            ([da, du], [wg_t, wu_t]), x, norm_g, scale, dx_out, tag + "_dh_norm_bwd", f=f_below, gate=gate_below,
            exchange=traffic.reduce_sibling(tag, [dwg_t, dwu_t, dwd]))
        traffic.add_halves(tag)
        return dx, (dshift, dscale, dnorm_g), gated
    kd, kg, ku = tag + "_wd", tag + "_wg", tag + "_wu"
    dwg_t = _mm(da, h, "tn", BF16, tag + "_dwg", exchange=traffic.reduce_sibling(kd, [dwd]))
    traffic.add_halves(kd)
    dwu_t = _mm(du, h, "tn", BF16, tag + "_dwu",
                exchange=_join(traffic.reduce_chips(kd), traffic.reduce_sibling(kg, [dwg_t])))
    traffic.add_halves(kg)
    half = x.shape[0] // 2
    top = _norm_mod_bwd(([da, du], [wg_t, wu_t]), x, norm_g, scale, dx_out, tag + "_dh_norm_bwd_top", f=f_below,
                        gate=gate_below, rows=(0, half),
                        exchange=_join(traffic.reduce_chips(kg), traffic.reduce_sibling(ku, [dwu_t])))
    traffic.add_halves(ku)
    traffic.sum_chips(kd)
    traffic.sum_chips(kg)
    bottom = _norm_mod_bwd(([da, du], [wg_t, wu_t]), x, norm_g, scale, dx_out, tag + "_dh_norm_bwd_bottom", f=f_below,
                           gate=gate_below, rows=(half, half),
                           exchange=_join(traffic.reduce_chips(ku), traffic.share(kd), traffic.share(kg)))
    traffic.sum_chips(ku)
    dx, dshift, dscale, dnorm_g, *gated = [jnp.concatenate([a, b]) if a.shape[0] == half else a + b
                                           for a, b in zip(top, bottom)]
    return dx, (dshift, dscale, dnorm_g), gated


def _layer_step(x, target, mod, gains, forget_bias, conv_w, traffic, att_w, in_shard, in_rows):
    sh1, sc1, g1, sh2, sc2, g2, sh3, sc3, g3 = mod
    norm1_g, norm2_g, norm3_g, final_g, group_g = gains
    s, d = x.shape
    n_heads = att_w // HEAD_DIM
    npair = n_heads // 2
    gate1, gate3 = 0.5 * g1, 0.5 * g3

    def split_w_in(w_in_pad):
        w_in_t = w_in_pad.reshape(N_CHIPS, in_rows, d)[:, :in_shard].reshape(N_CHIPS * in_shard, d)
        return (w_in_t[:3 * att_w], _pad_rows(w_in_t[3 * att_w:3 * att_w + n_heads], LANES), w_in_t[3 * att_w + n_heads:])

    wg1_t, wu1_t = traffic.weights("ffn1_gu")

    def wd1_ready():
        _run_exchange(traffic.gather_pass("ffn1_d"), "gather_ffn1_down_pass")
        return traffic.weights("ffn1_d")[0]

    x1, saved1, h2 = _ffn_fwd(x, norm1_g, sh1, sc1, gate1, wg1_t, wu1_t, wd1_ready, "ffn1", (norm2_g, sh2, sc2),
                              up_exchange=_join(traffic.gather_ici("ffn1_d"), traffic.gather_ici("mix_in")),
                              down_exchange=lambda: _join(traffic.gather_pass("mix_in"), traffic.gather_ici("mix_out")))
    wd1 = traffic.weights("ffn1_d")[0]
    wqkv_t, wf_t, wbcx_t = split_w_in(traffic.weights("mix_in")[0])

    qkv, bcx, flog = _project(h2, [wqkv_t, wbcx_t, wf_t], [BF16, F32, F32], "mix_proj",
                              exchange=traffic.gather_pass("mix_out"))
    w_out = traffic.weights("mix_out")[0]
    flog_t = jnp.pad(flog[:, :n_heads].T, ((0, HEAD_ROWS - n_heads), (0, 0)))
    bias_col = jnp.pad(forget_bias, (0, HEAD_ROWS - n_heads))[:, None]
    f_pieces = _forget_fwd(flog_t, bias_col, "forget_fwd")
    qa, ka, va = _attn_prep(qkv, f_pieces, "attn_prep")
    att, lse = _attn_fwd(qa, ka, va, "attn_fwd", exchange=traffic.gather_ici("ffn2"))
    cv = _conv_fwd(bcx, conv_w, "conv_fwd")
    yn = _gnorm_fwd(att, cv, group_g, "gnorm_fwd")
    x2, mix, h3 = _mm(yn, w_out, "nn", F32, "mix_out", res=x1, gate=g2, aux_dtype=BF16, norm=(norm3_g, sh3, sc3),
                      exchange=traffic.gather_pass("ffn2"))
    wg2_t, wu2_t, wd2 = traffic.weights("ffn2")

    a3, u3, hid3 = _ffn_up(h3, wg2_t, wu2_t, "ffn2_up")
    saved3 = (h3, a3, u3, hid3, None)
    dx3, loss_row, dfinal_g, df2, dgate3 = _down_final_loss(hid3, wd2, x2, gate3, final_g, target, "ffn2_down_loss")

    dx2, (dsh3, dsc3, dnorm3_g), (dmix, dg2) = _ffn_bwd(
        dx3, df2, x2, saved3, norm3_g, sc3, wg2_t, wu2_t, wd2, "ffn2", traffic, below=(mix, g2))
    dyn = _mm(dmix, w_out, "nt", F32, "mix_out_dyn")
    dw_out = _mm(yn, dmix, "tn", BF16, "mix_out_dw")
    datt, dcv, dgroup_g = _gnorm_bwd(dyn, att, cv, group_g, "gnorm_bwd")
    db, dc, dxc, dconv_w = _conv_bwd(dcv, bcx, conv_w, "conv_bwd")
    dbcx = jnp.concatenate([db, dc, dxc], axis=1)
    dq, dk, dv, qx, kx = _attn_bwd(qa, ka, va, datt, att, lse, "attn_bwd", exchange=traffic.reduce_chips("ffn2"))
    traffic.sum_chips("ffn2")
    dqkv = jnp.concatenate([dq.astype(BF16), dk, dv], axis=1)
    df_t = _decay_grads(qx, kx, "decay_grads")[:, :HEAD_ROWS].T
    dflog_t, dbias_col = _forget_bwd(df_t, flog_t, bias_col, "forget_bwd")
    dflog = jnp.pad(dflog_t[:n_heads].T, ((0, 0), (0, LANES - n_heads))).astype(BF16)
    dwqkv_t, dwbcx_t = _weight_grads([dqkv, dbcx], h2, "mix_dw", exchange=traffic.share("ffn2"))
    dwf_t = _mm(dflog, h2, "tn", BF16, "mix_dw_f")
    dw_in_t = jnp.concatenate([dwqkv_t, dwf_t[:n_heads], dwbcx_t], axis=0).reshape(N_CHIPS, in_shard, d)
    dw_in_t = jnp.pad(dw_in_t, ((0, 0), (0, in_rows - in_shard), (0, 0))).reshape(N_CHIPS * in_rows, d)
    dx1, dsh2, dsc2, dnorm2_g, df1, dgate1 = _norm_mod_bwd(
        ([dqkv, dbcx, dflog], [wqkv_t, wbcx_t, wf_t]), x1, norm2_g, sc2, dx2, "mix_dh_norm_bwd", f=saved1[4], gate=gate1,
        exchange=traffic.reduce_sibling("mix", [dw_in_t, dw_out]))
    traffic.add_halves("mix")

    def share_mix():
        traffic.sum_chips("mix")
        return traffic.share("mix")

    dx, (dsh1, dsc1, dnorm1_g), _ = _ffn_bwd(
        dx1, df1, x, saved1, norm1_g, sc1, wg1_t, wu1_t, wd1, "ffn1", traffic,
        dact_exchange=traffic.reduce_chips("mix"), dw_exchange=share_mix, finish_reduction=True)

    dmod = [dsh1, dsc1, 0.5 * dgate1, dsh2, dsc2, dg2, dsh3, dsc3, 0.5 * dgate3]
    dgains = [dnorm1_g, dnorm2_g, dnorm3_g, dfinal_g, dgroup_g]
    dbias = dbias_col[:n_heads, 0]
    return dx, loss_row, dmod, dgains, dbias, dconv_w


SMALL_ROWS = 24
ROW_GAINS, ROW_LOSS, ROW_FORGET, ROW_CONV, ROW_MOD = 0, 5, 6, 7, 10
PROW_ADA_B, PROW_GAINS, PROW_FORGET, PROW_CONV = 0, 9, 14, 15


def _round_up(n, m):
    return -(-n // m) * m


def _pad_rows(a, rows):
    return jnp.pad(a, ((0, rows - a.shape[0]), (0, 0)))


def _halves(a):
    return a.reshape(2, a.shape[0] // 2, a.shape[1])


def _rows_at(a, r0, total, width):
    return jnp.pad(a, ((r0, total - r0 - a.shape[0]), (0, width - a.shape[1])))


def kernel(x, c, ada_w, ada_b, norm1_g, ffn1_w_gate, ffn1_w_up, ffn1_w_down, norm2_g, w_in, forget_bias, conv_w, group_norm_g, w_out, norm3_g, ffn2_w_gate, ffn2_w_up, ffn2_w_down, final_g, loss_target, m_ada_w, m_ada_b, m_norm1_g, m_ffn1_w_gate, m_ffn1_w_up, m_ffn1_w_down, m_norm2_g, m_w_in, m_forget_bias, m_conv_w, m_group_norm_g, m_w_out, m_norm3_g, m_ffn2_w_gate, m_ffn2_w_up, m_ffn2_w_down, m_final_g, v_ada_w, v_ada_b, v_norm1_g, v_ffn1_w_gate, v_ffn1_w_up, v_ffn1_w_down, v_norm2_g, v_w_in, v_forget_bias, v_conv_w, v_group_norm_g, v_w_out, v_norm3_g, v_ffn2_w_gate, v_ffn2_w_up, v_ffn2_w_down, v_final_g):
    xi, yi, ci = lax.axis_index("x"), lax.axis_index("y"), lax.axis_index("c")
    chip = 2 * xi + yi
    dev = 4 * xi + 2 * yi + ci
    _, s, d = x.shape
    att_w = d // 2
    conv_width = d - att_w
    n_heads = att_w // HEAD_DIM
    in_shard = w_in.shape[1]
    in_rows = _round_up(in_shard, 32)
    cs = conv_w.shape[1]
    mod_shard = ada_w.shape[1]
    assert N_MOD * d == N_CHIPS * mod_shard and conv_width == N_CHIPS * cs and n_heads % 2 == 0

    def t_bf(w):
        return w.T.astype(BF16)

    shards = {"ffn1_gu": [_halves(t_bf(ffn1_w_gate)), _halves(t_bf(ffn1_w_up))], "ffn1_d": [_halves(ffn1_w_down.astype(BF16))],
              "mix_in": [_halves(_pad_rows(t_bf(w_in), in_rows))], "mix_out": [_halves(w_out.astype(BF16))],
              "ffn2": [_halves(t_bf(ffn2_w_gate)), _halves(t_bf(ffn2_w_up)), _halves(ffn2_w_down.astype(BF16))]}
    core = ci.astype(jnp.int32).reshape(1)
    chip_arr = chip.astype(jnp.int32).reshape(1)
    traffic = _WeightTraffic(shards, core, chip_arr)

    cond = _small_gather_exchange(_rows_at(c, 0, 8, d) + _rows_at(conv_w, 1, 8, d))
    _run_exchange(_join(traffic.gather_ici("ffn1_gu"), cond), "gather_ffn1_ici")
    got0 = cond.results[0].reshape(N_DEV, 8, d)
    c16 = _pad_rows(got0[:, 0, :], 16)
    conv_full = got0[0::2, 1:1 + CONV_K, :cs].transpose(1, 0, 2).reshape(CONV_K, conv_width)

    ada_b_mine = lax.dynamic_slice(ada_b, (chip * mod_shard,), (mod_shard,))[None, :]
    mods = _small_gather_exchange(_ada_fwd(c16, ada_w, ada_b_mine, "ada_fwd"))
    _run_exchange(_join(traffic.gather_pass("ffn1_gu"), mods), "gather_ffn1_pass")
    got1 = mods.results[0].reshape(N_DEV, 16, mod_shard)
    mod_mine = lax.dynamic_index_in_dim(got1[0::2], dev, axis=1, keepdims=False).reshape(N_MOD, d)
    mod = [mod_mine[i:i + 1] for i in range(N_MOD)]

    gains = [g[None, :] for g in (norm1_g, norm2_g, norm3_g, final_g, group_norm_g)]
    dx, loss_row, dmod, dgains, dbias, dconv_w = _layer_step(
        x[0], loss_target[0], mod, gains, forget_bias, conv_full, traffic, att_w, in_shard, in_rows)

    pack = sum(_rows_at(g, ROW_GAINS + i, SMALL_ROWS, d) for i, g in enumerate(dgains))
    pack += _rows_at(loss_row, ROW_LOSS, SMALL_ROWS, d) + _rows_at(dbias[None, :], ROW_FORGET, SMALL_ROWS, d)
    pack += _rows_at(dconv_w, ROW_CONV, SMALL_ROWS, d)
    pack += sum(_rows_at(g, ROW_MOD + i, SMALL_ROWS, d) for i, g in enumerate(dmod))
    small = _small_gather_exchange(pack)
    _run_exchange(_join(traffic.share("ffn1_wu"), small), "gather_small_grads")
    got2 = small.results[0].reshape(N_DEV, SMALL_ROWS, d)
    tot = _sum_devices(got2, "sum_small_grads")
    loss = tot[ROW_LOSS, 0]
    grad_ada_b = tot[ROW_MOD:ROW_MOD + N_MOD].reshape(N_MOD * d)
    grad_conv = lax.dynamic_slice(tot[ROW_CONV:ROW_CONV + CONV_K], (0, chip * cs), (CONV_K, cs))
    dmod_all = got2[:, ROW_MOD:ROW_MOD + N_MOD, :].reshape(N_DEV, N_MOD * d)
    dmod16 = _pad_rows(lax.dynamic_slice(dmod_all, (0, chip * mod_shard), (N_DEV, mod_shard)), 16)

    out = {"ada_w": tuple(_ada_update(c16.T, dmod16, ada_w, m_ada_w, v_ada_w, "adamw_ada_w"))}
    totals = (traffic.totals("ffn1_wg") + traffic.totals("ffn1_wu") + traffic.totals("ffn1_wd")
              + traffic.totals("mix") + traffic.totals("ffn2"))

    names = ("ffn1_w_gate", "ffn1_w_up", "ffn1_w_down", "w_in", "w_out", "ffn2_w_gate", "ffn2_w_up", "ffn2_w_down")
    transposed = ("ffn1_w_gate", "ffn1_w_up", "w_in", "ffn2_w_gate", "ffn2_w_up")
    params = {"ffn1_w_gate": (ffn1_w_gate, m_ffn1_w_gate, v_ffn1_w_gate), "ffn1_w_up": (ffn1_w_up, m_ffn1_w_up, v_ffn1_w_up),
              "ffn1_w_down": (ffn1_w_down, m_ffn1_w_down, v_ffn1_w_down), "w_in": (w_in, m_w_in, v_w_in),
              "w_out": (w_out, m_w_out, v_w_out), "ffn2_w_gate": (ffn2_w_gate, m_ffn2_w_gate, v_ffn2_w_gate),
              "ffn2_w_up": (ffn2_w_up, m_ffn2_w_up, v_ffn2_w_up), "ffn2_w_down": (ffn2_w_down, m_ffn2_w_down, v_ffn2_w_down)}
    for name_, (mine, theirs) in zip(names, totals):
        w, m, v = params[name_]
        if name_ in transposed:
            w, m, v = w.T, m.T, v.T
        if name_ == "w_in":
            both = jnp.where(ci == 0, jnp.concatenate([mine, theirs]), jnp.concatenate([theirs, mine]))[:in_shard]
            res = (both,) + tuple(_adamw(w, both, m, v, "adamw_" + name_))
        else:
            res = _adamw_halves(w, mine, theirs, m, v, core, "adamw_" + name_)
        out[name_] = tuple(r.T for r in res) if name_ in transposed else tuple(res)

    def small_pack(ada_b_, gains_, forget_, conv_):
        p = _rows_at(ada_b_.reshape(N_MOD, d), PROW_ADA_B, SMALL_ROWS, d)
        p += sum(_rows_at(g[None, :], PROW_GAINS + i, SMALL_ROWS, d) for i, g in enumerate(gains_))
        p += _rows_at(forget_[None, :], PROW_FORGET, SMALL_ROWS, d) + _rows_at(conv_, PROW_CONV, SMALL_ROWS, d)
        return p

    g_gains = [tot[ROW_GAINS + i] for i in range(5)]
    g_forget = tot[ROW_FORGET, :n_heads]
    sw = small_pack(ada_b, (norm1_g, norm2_g, norm3_g, final_g, group_norm_g), forget_bias, conv_w)
    sm = small_pack(m_ada_b, (m_norm1_g, m_norm2_g, m_norm3_g, m_final_g, m_group_norm_g), m_forget_bias, m_conv_w)
    sv = small_pack(v_ada_b, (v_norm1_g, v_norm2_g, v_norm3_g, v_final_g, v_group_norm_g), v_forget_bias, v_conv_w)
    sg = small_pack(grad_ada_b, g_gains, g_forget, grad_conv)
    small = (sg,) + tuple(_adamw(sw, sg, sm, sv, "adamw_small"))

    def unpack(p):
        r = {"ada_b": p[PROW_ADA_B:PROW_ADA_B + N_MOD].reshape(N_MOD * d), "forget_bias": p[PROW_FORGET, :n_heads],
             "conv_w": p[PROW_CONV:PROW_CONV + CONV_K, :cs]}
        for i, nm in enumerate(("norm1_g", "norm2_g", "norm3_g", "final_g", "group_norm_g")):
            r[nm] = p[PROW_GAINS + i]
        return r

    small = [unpack(p) for p in small]
    order = ("ada_w", "ada_b", "norm1_g", "ffn1_w_gate", "ffn1_w_up", "ffn1_w_down", "norm2_g", "w_in", "forget_bias",
             "conv_w", "group_norm_g", "w_out", "norm3_g", "ffn2_w_gate", "ffn2_w_up", "ffn2_w_down", "final_g")
    result = [loss, dx[None]]
    for k in range(4):
        result += [out[nm][k] if nm in out else small[k][nm] for nm in order]
    return tuple(result)
```
